```python
import math
import jax
import jax.numpy as jnp
from jax import lax
import numpy as np

D_MODEL = 1024
BATCH = 4
SEQ = 4096
DEPTH = 1

GLA_HEADS = 4
GLA_DK = 128
GLA_DV = 256
GLA_GATE_RANK = 16
GLA_TAU = 16.0
GLA_CHUNK = 64
NSA_HEADS = 8
NSA_GROUPS = 2
NSA_REP = NSA_HEADS // NSA_GROUPS
NSA_DH = 64
NSA_CMP_LEN = 32
NSA_CMP_STRIDE = 16
NSA_SLC_LEN = 64
NSA_N_SEL = 16
NSA_WINDOW = 512
NSA_Q_BLOCK = 128
D_FF = 2816
RMS_EPS = 1e-6
NEG = -1e30

GLA_QK_W = GLA_HEADS * GLA_DK
GLA_V_W = GLA_HEADS * GLA_DV
NSA_Q_W = NSA_HEADS * NSA_DH
NSA_KV_W = NSA_GROUPS * NSA_DH
NSA_GATE_W = NSA_HEADS * 3
IN_SPLITS = (GLA_QK_W, GLA_QK_W, GLA_V_W, GLA_V_W, GLA_GATE_RANK,
             NSA_Q_W, NSA_KV_W, NSA_KV_W, NSA_KV_W, NSA_KV_W, NSA_KV_W, NSA_KV_W,
             NSA_GATE_W, D_MODEL, D_MODEL)
IN_WIDTH = 4 * GLA_QK_W // 2 + 2 * GLA_V_W + GLA_GATE_RANK + NSA_Q_W + 6 * NSA_KV_W + NSA_GATE_W + 2 * D_MODEL

kernel_name = 'gla_nsa_hybrid_block'


def rms_norm(x, g):
    xf = x.astype(jnp.float32)
    y = xf * lax.rsqrt(jnp.mean(xf * xf, axis=-1, keepdims=True) + RMS_EPS)
    return (y * g.astype(jnp.float32)).astype(x.dtype)


def masked_softmax(s, mask):
    s = jnp.where(mask, s, NEG)
    p = jax.nn.softmax(s, axis=-1)
    return jnp.where(mask, p, 0.0)


def gla_mixer(q, k, v, r, a_lr, w_alpha2, b_alpha, norm_g):
    B, S = q.shape[0], q.shape[1]
    C = GLA_CHUNK
    nc = S // C

    def heads(t, d):
        return t.reshape(B, nc, C, GLA_HEADS, d).transpose(0, 3, 1, 2, 4).astype(jnp.float32)

    qh = heads(q, GLA_DK) * (GLA_DK ** -0.5)
    kh = heads(k, GLA_DK)
    vh = heads(v, GLA_DV)
    log_a = jax.nn.log_sigmoid((a_lr @ w_alpha2 + b_alpha).astype(jnp.float32)) / GLA_TAU
    b = jnp.cumsum(heads(log_a, GLA_DK), axis=3)
    b_last = b[:, :, :, -1:, :]
    qe = qh * jnp.exp(b)
    ke = kh * jnp.exp(-b)
    kd = kh * jnp.exp(b_last - b)
    causal = jnp.tril(jnp.ones((C, C), dtype=bool))
    attn = jnp.where(causal, jnp.einsum('bhncd,bhnsd->bhncs', qe, ke), 0.0)
    o = jnp.einsum('bhncs,bhnse->bhnce', attn, vh)
    upd = jnp.einsum('bhncd,bhnce->nbhde', kd, vh)
    decay = jnp.exp(b_last[:, :, :, 0, :]).transpose(2, 0, 1, 3)

    def step(state, inp):
        dec, u = inp
        return dec[..., None] * state + u, state

    init = jnp.zeros((B, GLA_HEADS, GLA_DK, GLA_DV), jnp.float32)
    _, s_prev = lax.scan(step, init, (decay, upd))
    o = o + jnp.einsum('bhncd,nbhde->bhnce', qe, s_prev)
    o = o * lax.rsqrt(jnp.mean(o * o, axis=-1, keepdims=True) + RMS_EPS) * norm_g.astype(jnp.float32)
    o = o.reshape(B, GLA_HEADS, S, GLA_DV).transpose(0, 2, 1, 3).reshape(B, S, GLA_V_W)
    return (o * jax.nn.silu(r.astype(jnp.float32))).astype(r.dtype)


def nsa_mixer(q, k_c, v_c, k_s, v_s, k_w, v_w, gate_logits, pe_k, w1_k, w2_k, pe_v, w1_v, w2_v):
    B, S = q.shape[0], q.shape[1]
    G, R, DH = NSA_GROUPS, NSA_REP, NSA_DH
    L, ST, SL, QB, W = NSA_CMP_LEN, NSA_CMP_STRIDE, NSA_SLC_LEN, NSA_Q_BLOCK, NSA_WINDOW
    qh = q.reshape(B, S, G, R, DH).transpose(0, 2, 3, 1, 4).astype(jnp.float32) * (DH ** -0.5)

    def kv_heads(t):
        return t.reshape(B, S, G, DH).transpose(0, 2, 1, 3).astype(jnp.float32)

    gates = jax.nn.sigmoid(gate_logits.astype(jnp.float32)).reshape(B, S, G, R, 3).transpose(0, 2, 3, 1, 4)
    h_idx = jnp.arange(NSA_HEADS, dtype=jnp.float32)
    slopes = jnp.exp2(-8.0 * (h_idx + 1.0) / NSA_HEADS).reshape(1, G, R, 1, 1)

    n_cmp = (S - L) // ST + 1
    starts_c = ST * jnp.arange(n_cmp)
    idx_c = starts_c[:, None] + jnp.arange(L)[None, :]

    def compress(t, pe, w1, w2):
        blocks = t[:, :, idx_c] + pe
        flat = blocks.reshape(B, G, n_cmp, L * DH)
        return jax.nn.silu(flat @ w1) @ w2

    kc = compress(kv_heads(k_c), pe_k, w1_k, w2_k).astype(jnp.float32)
    vc = compress(kv_heads(v_c), pe_v, w1_v, w2_v).astype(jnp.float32)
    end_c = starts_c + L - 1

    n_slc = S // SL
    n_sel = min(NSA_N_SEL, n_slc)
    starts_s = SL * jnp.arange(n_slc)
    ov = jnp.clip(jnp.minimum(starts_c[:, None] + L, starts_s[None, :] + SL)
                  - jnp.maximum(starts_c[:, None], starts_s[None, :]), 0, None).astype(jnp.float32) / L
    ks_blocks = kv_heads(k_s).reshape(B, G, n_slc, SL, DH)
    vs_blocks = kv_heads(v_s).reshape(B, G, n_slc, SL, DH)
    gather = jax.vmap(jax.vmap(lambda blk, ix: blk[ix]))

    kw_pad = jnp.pad(kv_heads(k_w), ((0, 0), (0, 0), (W, 0), (0, 0)))
    vw_pad = jnp.pad(kv_heads(v_w), ((0, 0), (0, 0), (W, 0), (0, 0)))

    nq = S // QB
    q_blocks = qh.reshape(B, G, R, nq, QB, DH).transpose(3, 0, 1, 2, 4, 5)
    g_blocks = gates.reshape(B, G, R, nq, QB, 3).transpose(3, 0, 1, 2, 4, 5)
    jb = jnp.arange(n_slc)
    r_w = jnp.arange(W + QB)

    def attend_block(args):
        i, qb, gb = args
        q0 = i * QB
        t = q0 + jnp.arange(QB)
        tf = t.astype(jnp.float32)
        dist_c = tf[:, None] - end_c[None, :].astype(jnp.float32)
        s_c = jnp.einsum('bgrqd,bgnd->bgrqn', qb, kc) - slopes * dist_c
        p_c = masked_softmax(s_c, end_c[None, :] <= t[:, None])
        o_c = jnp.einsum('bgrqn,bgnd->bgrqd', p_c, vc)
        imp = jnp.einsum('bgrqn,nj->bgqj', p_c, ov)
        cur = t // SL
        forced = (jb[None, :] == 0) | (jb[None, :] == cur[:, None]) | (jb[None, :] == cur[:, None] - 1)
        score = jnp.where(jb[None, :] > cur[:, None], NEG, jnp.where(forced, -NEG, imp))
        _, sel = lax.top_k(score, n_sel)
        ks = gather(ks_blocks, sel).reshape(B, G, QB, n_sel * SL, DH)
        vs = gather(vs_blocks, sel).reshape(B, G, QB, n_sel * SL, DH)
        pos_s = (sel[..., None] * SL + jnp.arange(SL)).reshape(B, G, QB, n_sel * SL)
        dist_s = (tf[None, None, :, None] - pos_s.astype(jnp.float32))[:, :, None]
        s_s = jnp.einsum('bgrqd,bgqkd->bgrqk', qb, ks) - slopes * dist_s
        p_s = masked_softmax(s_s, (pos_s <= t[None, None, :, None])[:, :, None])
        o_s = jnp.einsum('bgrqk,bgqkd->bgrqd', p_s, vs)
        kw = lax.dynamic_slice_in_dim(kw_pad, q0, W + QB, axis=2)
        vw = lax.dynamic_slice_in_dim(vw_pad, q0, W + QB, axis=2)
        pos_w = q0 - W + r_w
        dist_w = t[:, None] - pos_w[None, :]
        mask_w = (pos_w[None, :] >= 0) & (dist_w >= 0) & (dist_w < W)
        s_w = jnp.einsum('bgrqd,bgkd->bgrqk', qb, kw) - slopes * dist_w.astype(jnp.float32)
        p_w = masked_softmax(s_w, mask_w)
        o_w = jnp.einsum('bgrqk,bgkd->bgrqd', p_w, vw)
        return gb[..., 0:1] * o_c + gb[..., 1:2] * o_s + gb[..., 2:3] * o_w

    o = lax.map(attend_block, (jnp.arange(nq), q_blocks, g_blocks))
    return o.transpose(1, 0, 4, 2, 3, 5).reshape(B, S, NSA_Q_W).astype(q.dtype)


def setup_inputs(seed: int = 0) -> dict:
    key = jax.random.key(seed)
    ks = jax.random.split(key, 24)
    f32 = jnp.float32

    def normal(k, shape, fan_in):
        return jax.random.normal(k, shape, f32) * (fan_in ** -0.5)

    def gain(k, shape):
        return 1.0 + 0.1 * jax.random.normal(k, shape, f32)

    Ld = DEPTH
    return {
        'x': jax.random.normal(ks[0], (BATCH, SEQ, D_MODEL), f32),
        'norm_mix_pre': gain(ks[1], (Ld, D_MODEL)),
        'norm_mix_post': gain(ks[2], (Ld, D_MODEL)),
        'norm_ffn_pre': gain(ks[3], (Ld, D_MODEL)),
        'norm_ffn_post': gain(ks[4], (Ld, D_MODEL)),
        'w_in': normal(ks[5], (Ld, D_MODEL, IN_WIDTH), D_MODEL),
        'gla_w_alpha2': normal(ks[6], (Ld, GLA_GATE_RANK, GLA_QK_W), GLA_GATE_RANK),
        'gla_b_alpha': 0.1 * jax.random.normal(ks[7], (Ld, GLA_QK_W), f32),
        'gla_norm_g': gain(ks[8], (Ld, GLA_DV)),
        'nsa_cmp_pe_k': 0.1 * jax.random.normal(ks[9], (Ld, NSA_CMP_LEN, NSA_DH), f32),
        'nsa_cmp_w1_k': normal(ks[10], (Ld, NSA_CMP_LEN * NSA_DH, NSA_DH), NSA_CMP_LEN * NSA_DH),
        'nsa_cmp_w2_k': normal(ks[11], (Ld, NSA_DH, NSA_DH), NSA_DH),
        'nsa_cmp_pe_v': 0.1 * jax.random.normal(ks[12], (Ld, NSA_CMP_LEN, NSA_DH), f32),
        'nsa_cmp_w1_v': normal(ks[13], (Ld, NSA_CMP_LEN * NSA_DH, NSA_DH), NSA_CMP_LEN * NSA_DH),
        'nsa_cmp_w2_v': normal(ks[14], (Ld, NSA_DH, NSA_DH), NSA_DH),
        'w_proj_gla': normal(ks[15], (Ld, GLA_V_W, D_MODEL), GLA_V_W),
        'w_proj_nsa': normal(ks[16], (Ld, NSA_Q_W, D_MODEL), NSA_Q_W),
        'w_out': normal(ks[17], (Ld, D_MODEL, D_MODEL), D_MODEL),
        'w_ffn_gate': normal(ks[18], (Ld, D_MODEL, D_FF), D_MODEL),
        'w_ffn_up': normal(ks[19], (Ld, D_MODEL, D_FF), D_MODEL),
        'w_ffn_down': normal(ks[20], (Ld, D_FF, D_MODEL), D_FF),
    }


def reference(x, norm_mix_pre, norm_mix_post, norm_ffn_pre, norm_ffn_post, w_in,
              gla_w_alpha2, gla_b_alpha, gla_norm_g,
              nsa_cmp_pe_k, nsa_cmp_w1_k, nsa_cmp_w2_k, nsa_cmp_pe_v, nsa_cmp_w1_v, nsa_cmp_w2_v,
              w_proj_gla, w_proj_nsa, w_out, w_ffn_gate, w_ffn_up, w_ffn_down):
    split_points = [int(c) for c in np.cumsum(IN_SPLITS)[:-1]]
    for l in range(DEPTH):
        h = rms_norm(x, norm_mix_pre[l])
        proj = h @ w_in[l]
        (g_q, g_k, g_v, g_r, g_a, n_q, n_kc, n_vc, n_ks, n_vs, n_kw, n_vw,
         n_gate, merge_gla, merge_nsa) = jnp.split(proj, split_points, axis=-1)
        o_gla = gla_mixer(g_q, g_k, g_v, g_r, g_a, gla_w_alpha2[l], gla_b_alpha[l], gla_norm_g[l])
        o_nsa = nsa_mixer(n_q, n_kc, n_vc, n_ks, n_vs, n_kw, n_vw, n_gate,
                          nsa_cmp_pe_k[l], nsa_cmp_w1_k[l], nsa_cmp_w2_k[l],
                          nsa_cmp_pe_v[l], nsa_cmp_w1_v[l], nsa_cmp_w2_v[l])
        mixed = (jax.nn.sigmoid(merge_gla) * (o_gla @ w_proj_gla[l])
                 + jax.nn.sigmoid(merge_nsa) * (o_nsa @ w_proj_nsa[l]))
        x = x + rms_norm(mixed @ w_out[l], norm_mix_post[l])
        h = rms_norm(x, norm_ffn_pre[l])
        f = (jax.nn.silu(h @ w_ffn_gate[l]) * (h @ w_ffn_up[l])) @ w_ffn_down[l]
        x = x + rms_norm(f, norm_ffn_post[l])
    return x
```

```python
import functools

import numpy as np
import jax
import jax.numpy as jnp
from jax import lax
from jax.experimental import pallas as pl
from jax.experimental.pallas import tpu as pltpu

D_MODEL = 1024
GLA_HEADS = 4
GLA_DK = 128
GLA_DV = 256
GLA_RANK = 16
GLA_TAU = 16.0
GLA_CHUNK = 64
NSA_HEADS = 8
NSA_GROUPS = 2
NSA_REP = 4
NSA_DH = 64
CMP_LEN = 32
CMP_STRIDE = 16
SLC_LEN = 64
N_SEL = 16
WINDOW = 512
QB = 128
D_FF = 2816
EPS = 1e-6
NEG = -1e30

LANES = 128
VMEM_LIMIT = 56 * 1024 * 1024
BF = jnp.bfloat16
F32 = jnp.float32

OFF_GQ = 0
OFF_GK = 512
OFF_GV = 1024
OFF_GR = 2048
OFF_NQ = 3072
OFF_MG = 4096
OFF_MN = 5120
OFF_KC = 6144
OFF_VC = 6272
OFF_KS = 6400
OFF_VS = 6656
OFF_KW = 6912
OFF_VW = 7168
N_MAIN = 7424
N_SMALL = 384

NT = (((1,), (1,)), ((), ()))
TN = (((0,), (0,)), ((), ()))


def _resident(shape, index_map):
    return pl.BlockSpec(shape, index_map, pipeline_mode=pl.Buffered(1))


def _params(sem):
    return pltpu.CompilerParams(dimension_semantics=sem, vmem_limit_bytes=VMEM_LIMIT)


def _in_proj_kernel(x_ref, g_ref, wm_ref, ws_ref, om_ref, os_ref, h_scr):
    j = pl.program_id(1)

    @pl.when(j == 0)
    def _():
        x = x_ref[...]
        y = x * lax.rsqrt(jnp.mean(x * x, axis=-1, keepdims=True) + EPS) * g_ref[...]
        h_scr[...] = y.astype(BF)
        os_ref[...] = jnp.dot(h_scr[...], ws_ref[...], preferred_element_type=F32)

    tn = om_ref.shape[1]
    step = 512
    for c0 in range(0, tn, step):
        c1 = min(c0 + step, tn)
        om_ref[:, c0:c1] = jnp.dot(h_scr[...], wm_ref[:, c0:c1], preferred_element_type=F32).astype(BF)


def _in_proj(x2, g, w_main, w_small, tm=512, n_split=2):
    n_tok = x2.shape[0]
    tn = N_MAIN // n_split
    return pl.pallas_call(
        _in_proj_kernel,
        grid=(n_tok // tm, n_split),
        in_specs=[
            pl.BlockSpec((tm, D_MODEL), lambda i, j: (i, 0)),
            _resident((1, D_MODEL), lambda i, j: (0, 0)),
            pl.BlockSpec((D_MODEL, tn), lambda i, j: (0, j)),
            _resident((D_MODEL, N_SMALL), lambda i, j: (0, 0)),
        ],
        out_specs=[
            pl.BlockSpec((tm, tn), lambda i, j: (i, j)),
            pl.BlockSpec((tm, N_SMALL), lambda i, j: (i, 0)),
        ],
        out_shape=[
            jax.ShapeDtypeStruct((n_tok, N_MAIN), BF),
            jax.ShapeDtypeStruct((n_tok, N_SMALL), F32),
        ],
        scratch_shapes=[pltpu.VMEM((tm, D_MODEL), BF)],
        compiler_params=_params(("parallel", "arbitrary")),
    )(x2, g, w_main, w_small)


def _gla_kernel(q_ref, k_ref, v_ref, r_ref, a_ref, w2_ref, b2_ref, ng_ref, o_ref, st_scr, *, n_chunks):
    blk = pl.program_id(2)

    @pl.when(blk == 0)
    def _():
        st_scr[...] = jnp.zeros_like(st_scr)

    C = GLA_CHUNK
    z = jnp.dot(a_ref[0], w2_ref[...], precision=lax.Precision.HIGHEST,
                preferred_element_type=F32) + b2_ref[...]
    log_a = (jnp.minimum(z, 0.0) - jnp.log1p(jnp.exp(-jnp.abs(z)))) * (1.0 / GLA_TAU)
    row = lax.broadcasted_iota(jnp.int32, (C, C), 0)
    col = lax.broadcasted_iota(jnp.int32, (C, C), 1)
    causal = col <= row
    tri = causal.astype(F32)
    scale = GLA_DK ** -0.5
    ng = ng_ref[...]
    for c in range(n_chunks):
        sl = slice(c * C, (c + 1) * C)
        bcum = jnp.dot(tri, log_a[sl], precision=lax.Precision.HIGHEST, preferred_element_type=F32)
        b_last = bcum[C - 1:C, :]
        q = q_ref[0, sl, :].astype(F32)
        k = k_ref[0, sl, :].astype(F32)
        v = v_ref[0, sl, :]
        qe = (q * (scale * jnp.exp(bcum))).astype(BF)
        ke = (k * jnp.exp(-bcum)).astype(BF)
        kd = (k * jnp.exp(b_last - bcum)).astype(BF)
        attn = lax.dot_general(qe, ke, NT, preferred_element_type=F32)
        attn = jnp.where(causal, attn, 0.0).astype(BF)
        st = st_scr[...]
        o = jnp.dot(attn, v, preferred_element_type=F32)
        o = o + lax.dot_general(qe, st.astype(BF), NT, preferred_element_type=F32)
        upd = lax.dot_general(v, kd, TN, preferred_element_type=F32)
        st_scr[...] = st * jnp.exp(b_last) + upd
        o = o * lax.rsqrt(jnp.mean(o * o, axis=-1, keepdims=True) + EPS) * ng
        r = r_ref[0, sl, :].astype(F32)
        o_ref[0, sl, :] = (o * (r * jax.nn.sigmoid(r))).astype(BF)


def _gla(proj3, small3, w2p, b2, ng, blk_tokens=512):
    B, S, _ = proj3.shape
    nblk = S // blk_tokens
    kern = functools.partial(_gla_kernel, n_chunks=blk_tokens // GLA_CHUNK)
    return pl.pallas_call(
        kern,
        grid=(B, GLA_HEADS, nblk),
        in_specs=[
            pl.BlockSpec((1, blk_tokens, GLA_DK), lambda b, h, i: (b, i, OFF_GQ // GLA_DK + h)),
            pl.BlockSpec((1, blk_tokens, GLA_DK), lambda b, h, i: (b, i, OFF_GK // GLA_DK + h)),
            pl.BlockSpec((1, blk_tokens, GLA_DV), lambda b, h, i: (b, i, OFF_GV // GLA_DV + h)),
            pl.BlockSpec((1, blk_tokens, GLA_DV), lambda b, h, i: (b, i, OFF_GR // GLA_DV + h)),
            pl.BlockSpec((1, blk_tokens, LANES), lambda b, h, i: (b, i, 0)),
            pl.BlockSpec((LANES, GLA_DK), lambda b, h, i: (0, h)),
            pl.BlockSpec((1, GLA_DK), lambda b, h, i: (0, h)),
            pl.BlockSpec((1, GLA_DV), lambda b, h, i: (0, 0)),
        ],
        out_specs=pl.BlockSpec((1, blk_tokens, GLA_DV), lambda b, h, i: (b, i, h)),
        out_shape=jax.ShapeDtypeStruct((B, S, GLA_HEADS * GLA_DV), BF),
        scratch_shapes=[pltpu.VMEM((GLA_DV, GLA_DK), F32)],
        compiler_params=_params(("parallel", "parallel", "arbitrary")),
    )(proj3, proj3, proj3, proj3, small3, w2p, b2, ng)


def _compress_kernel(xk_ref, xv_ref, pek_ref, pev_ref, w1k_ref, w1v_ref, w2k_ref, w2v_ref, kc_ref, vc_ref):
    n_rows = xk_ref.shape[1]
    half = CMP_STRIDE * LANES
    lane = lax.broadcasted_iota(jnp.int32, (n_rows, LANES), 1)
    row = lax.broadcasted_iota(jnp.int32, (n_rows, LANES), 0)
    end_c = CMP_STRIDE * row + (CMP_LEN - 1)
    c_k = jnp.where(lane == NSA_DH, end_c >> 6, jnp.where(lane == NSA_DH + 1, end_c & 63, 0)).astype(F32)
    c_v = jnp.where(lane == NSA_DH, 1.0, 0.0).astype(F32)

    def branch(x_ref, pe_ref, w1_ref, w2_ref, const, o_ref):
        x = x_ref[0].astype(F32)
        xa = (x + pe_ref[0:1, :]).astype(BF)
        xb = (x + pe_ref[1:2, :]).astype(BF)
        a = jnp.dot(xa, w1_ref[0:half, :], preferred_element_type=F32)
        b = jnp.dot(xb, w1_ref[half:2 * half, :], preferred_element_type=F32)
        pre = a + pltpu.roll(b, n_rows - 1, 0)
        hid = (pre * jax.nn.sigmoid(pre)).astype(BF)
        for g in range(NSA_GROUPS):
            o_ref[0, g] = (jnp.dot(hid, w2_ref[g], preferred_element_type=F32) + const).astype(BF)

    branch(xk_ref, pek_ref, w1k_ref, w2k_ref, c_k, kc_ref)
    branch(xv_ref, pev_ref, w1v_ref, w2v_ref, c_v, vc_ref)


def _nsa_compress(xk, xv, pek, pev, w1k, w1v, w2k, w2v):
    B, n_rows, width = xk.shape
    full = lambda shape: _resident(shape, lambda b: (0,) * len(shape))
    out = jax.ShapeDtypeStruct((B, NSA_GROUPS, n_rows, LANES), BF)
    return pl.pallas_call(
        _compress_kernel,
        grid=(B,),
        in_specs=[
            pl.BlockSpec((1, n_rows, width), lambda b: (b, 0, 0)),
            pl.BlockSpec((1, n_rows, width), lambda b: (b, 0, 0)),
            full(pek.shape), full(pev.shape), full(w1k.shape), full(w1v.shape),
            full(w2k.shape), full(w2v.shape),
        ],
        out_specs=[pl.BlockSpec((1, NSA_GROUPS, n_rows, LANES), lambda b: (b, 0, 0, 0))] * 2,
        out_shape=[out, out],
        compiler_params=_params(("parallel",)),
    )(xk, xv, pek, pev, w1k, w1v, w2k, w2v)


def _nsa_kernel(q_ref, ks_ref, vs_ref, kw_ref, vw_ref, kc_ref, vc_ref, gt_ref, sl_ref, ov_ref,
                o_ref, ksa, vsa, kwa, vwa, m_scr, acc_scr, *, seq, kt):
    qi = pl.program_id(2)
    n_slc = seq // SLC_LEN
    R = NSA_REP
    RQ = R * QB

    @pl.when(qi == 0)
    def _():
        lane = lax.broadcasted_iota(jnp.int32, (seq, LANES), 1)
        pos = lax.broadcasted_iota(jnp.int32, (seq, LANES), 0)
        blk = pos >> 6
        off = pos & 63
        c_s = jnp.where(lane == LANES - 1, off,
                        jnp.where((lane >= NSA_DH) & (lane - (NSA_DH - 1) == blk), 1, 0))
        ksa[...] = ks_ref[0] + c_s.astype(BF)
        ones = jnp.where(lane == NSA_DH, 1, 0).astype(BF)
        vsa[...] = vs_ref[0] + ones
        c_w = jnp.where(lane == NSA_DH, blk, jnp.where(lane == NSA_DH + 1, off, 0))
        kwa[0:WINDOW, :] = jnp.zeros((WINDOW, LANES), BF)
        vwa[0:WINDOW, :] = jnp.zeros((WINDOW, LANES), BF)
        kwa[WINDOW:WINDOW + seq, :] = kw_ref[0] + c_w.astype(BF)
        vwa[WINDOW:WINDOW + seq, :] = vw_ref[0] + ones

    q0 = qi * QB
    lane1 = lax.broadcasted_iota(jnp.int32, (1, LANES), 1)
    slopes = [sl_ref[0, r:r + 1, :] for r in range(R)]

    qw = jnp.concatenate(
        [q_ref[0, :, r * LANES:(r + 1) * LANES]
         + jnp.where(lane1 == NSA_DH, slopes[r] * 64.0, jnp.where(lane1 == NSA_DH + 1, slopes[r], 0.0)).astype(BF)
         for r in range(R)], axis=0)

    n_cmp = kc_ref.shape[2]
    s_c = lax.dot_general(qw, kc_ref[0, 0], NT, preferred_element_type=F32).reshape(R, QB, n_cmp)
    t_c = q0 + lax.broadcasted_iota(jnp.int32, (QB, n_cmp), 0)
    e_c = CMP_STRIDE * lax.broadcasted_iota(jnp.int32, (QB, n_cmp), 1) + (CMP_LEN - 1)
    mask_c = e_c <= t_c
    s_c = jnp.where(mask_c[None], s_c, NEG)
    m_c = jnp.max(s_c, axis=-1, keepdims=True)
    p_c = jnp.where(mask_c[None], jnp.exp(s_c - m_c), 0.0)
    l_c = jnp.sum(p_c, axis=-1, keepdims=True)
    p_c = p_c * jnp.where(l_c > 0.0, 1.0 / l_c, 0.0)
    o_c = jnp.dot(p_c.reshape(RQ, n_cmp).astype(BF), vc_ref[0, 0], preferred_element_type=F32)

    psum = jnp.sum(p_c, axis=0)
    ovt = ov_ref[...]
    p_hi = psum.astype(BF)
    rem = psum - p_hi.astype(F32)
    p_mid = rem.astype(BF)
    p_lo = (rem - p_mid.astype(F32)).astype(BF)
    imp = (lax.dot_general(ovt, p_hi, NT, preferred_element_type=F32)
           + lax.dot_general(ovt, p_mid, NT, preferred_element_type=F32)
           + lax.dot_general(ovt, p_lo, NT, preferred_element_type=F32))

    NR = ovt.shape[0]
    SUB = 8
    jblk = lax.broadcasted_iota(jnp.int32, (NR, QB), 0)
    t_q = q0 + lax.broadcasted_iota(jnp.int32, (NR, QB), 1)
    cur = t_q >> 6
    forced = (jblk == 0) | (jblk == cur) | (jblk == cur - 1)
    score = jnp.where(jblk > cur, NEG, jnp.where(forced, -NEG, imp))
    n_slab = -(-n_slc // SUB)
    slabs = [score[a * SUB:(a + 1) * SUB, :] for a in range(n_slab)]
    cnts = [jnp.zeros((SUB, QB), F32) for _ in range(n_slab)]
    isub = lax.broadcasted_iota(jnp.int32, (SUB, QB), 0)
    for jp in range(n_slc):
        rowv = jnp.broadcast_to(score[jp:jp + 1, :], (SUB, QB))
        for a in range(n_slab):
            if a < jp // SUB:
                beats = jnp.where(rowv > slabs[a], 1.0, 0.0)
            elif a > jp // SUB:
                beats = jnp.where(rowv >= slabs[a], 1.0, 0.0)
            else:
                beats = jnp.where(isub > jp % SUB,
                                  jnp.where(rowv >= slabs[a], 1.0, 0.0), jnp.where(rowv > slabs[a], 1.0, 0.0))
            cnts[a] = cnts[a] + beats
    cnt = jnp.concatenate(cnts + [jnp.full((NR - n_slab * SUB, QB), float(NR), F32)] * (NR > n_slab * SUB), axis=0)
    sel = (cnt < float(N_SEL)) & (jblk <= cur) & (jblk < n_slc)
    a_nat = jnp.where(jblk == 0, 1.0, jnp.where(sel, (SLC_LEN * jblk).astype(F32), NEG))
    a_nat = jnp.where(jblk < n_slc, a_nat, 0.0)
    a_t = pltpu.roll(a_nat, NR - 1, 0)
    a_q = jnp.concatenate([jnp.zeros((LANES - NR, QB), F32), a_t], axis=0).T

    qs = jnp.concatenate(
        [q_ref[0, :, r * LANES:(r + 1) * LANES] + (a_q * slopes[r]).astype(BF) for r in range(R)], axis=0)

    m_scr[...] = jnp.full(m_scr.shape, NEG, F32)
    acc_scr[...] = jnp.zeros(acc_scr.shape, F32)

    def slc_tile(k0, diag):
        k = ksa[pl.ds(k0, kt), :]
        v = vsa[pl.ds(k0, kt), :]
        s = lax.dot_general(qs, k, NT, preferred_element_type=F32)
        if diag:
            t_d = q0 + lax.broadcasted_iota(jnp.int32, (QB, kt), 0)
            p_d = k0 + lax.broadcasted_iota(jnp.int32, (QB, kt), 1)
            bias = jnp.where(p_d <= t_d, 0.0, NEG)
            s = (s.reshape(R, QB, kt) + bias[None]).reshape(RQ, kt)
        m_prev = m_scr[...]
        m_new = jnp.maximum(m_prev, jnp.max(s, axis=-1, keepdims=True))
        alpha = jnp.exp(m_prev - m_new)
        p = jnp.concatenate(
            [jnp.exp(s[:, c * LANES:(c + 1) * LANES] - m_new) for c in range(kt // LANES)], axis=1)
        acc_scr[...] = acc_scr[...] * alpha + jnp.dot(p.astype(BF), v, preferred_element_type=F32)
        m_scr[...] = m_new

    n_full = (q0 + QB - 1) // kt

    def body(i, carry):
        slc_tile(pl.multiple_of(i * kt, kt), False)
        return carry

    lax.fori_loop(0, n_full, body, 0)
    slc_tile(pl.multiple_of(n_full * kt, kt), True)
    acc_s = acc_scr[...]

    WK = WINDOW + QB
    kw = kwa[pl.ds(pl.multiple_of(q0, QB), WK), :]
    vw = vwa[pl.ds(pl.multiple_of(q0, QB), WK), :]
    s_w = lax.dot_general(qw, kw, NT, preferred_element_type=F32).reshape(R, QB, WK)
    d_w = (lax.broadcasted_iota(jnp.int32, (QB, WK), 1) - lax.broadcasted_iota(jnp.int32, (QB, WK), 0))
    c_w = lax.broadcasted_iota(jnp.int32, (QB, WK), 1)
    mask_w = (d_w > 0) & (d_w <= WINDOW) & (c_w >= WINDOW - q0)
    s_w = jnp.where(mask_w[None], s_w, NEG)
    m_w = jnp.max(s_w, axis=-1, keepdims=True)
    p_w = jnp.exp(s_w - m_w).reshape(RQ, WK)
    acc_w = jnp.dot(p_w.astype(BF), vw, preferred_element_type=F32)

    sg = jax.nn.sigmoid(gt_ref[0])
    for r in range(R):
        rs = slice(r * QB, (r + 1) * QB)
        a_s = acc_s[rs]
        a_w = acc_w[rs]
        g_c = sg[:, 3 * r:3 * r + 1]
        g_s = sg[:, 3 * r + 1:3 * r + 2] / a_s[:, NSA_DH:NSA_DH + 1]
        g_w = sg[:, 3 * r + 2:3 * r + 3] / a_w[:, NSA_DH:NSA_DH + 1]
        o_ref[0, :, r * LANES:(r + 1) * LANES] = (g_c * o_c[rs] + g_s * a_s + g_w * a_w).astype(BF)


def _nsa_attend(proj3, small3, kc, vc, slope_tab, ovt, kt=512):
    B, S, _ = proj3.shape
    kt = min(kt, S)
    G, R = NSA_GROUPS, NSA_REP
    n_cmp = kc.shape[2]
    kern = functools.partial(_nsa_kernel, seq=S, kt=kt)
    kv_spec = lambda off: pl.BlockSpec((1, S, LANES), lambda b, g, i: (b, 0, off // LANES + g))
    return pl.pallas_call(
        kern,
        grid=(B, G, S // QB),
        in_specs=[
            pl.BlockSpec((1, QB, R * LANES), lambda b, g, i: (b, i, OFF_NQ // (R * LANES) + g)),
            kv_spec(OFF_KS), kv_spec(OFF_VS), kv_spec(OFF_KW), kv_spec(OFF_VW),
            pl.BlockSpec((1, 1, n_cmp, LANES), lambda b, g, i: (b, g, 0, 0)),
            pl.BlockSpec((1, 1, n_cmp, LANES), lambda b, g, i: (b, g, 0, 0)),
            pl.BlockSpec((1, QB, LANES), lambda b, g, i: (b, i, 1 + g)),
            pl.BlockSpec((1, 8, LANES), lambda b, g, i: (g, 0, 0)),
            _resident(ovt.shape, lambda b, g, i: (0, 0)),
        ],
        out_specs=pl.BlockSpec((1, QB, R * LANES), lambda b, g, i: (b, i, g)),
        out_shape=jax.ShapeDtypeStruct((B, S, G * R * LANES), BF),
        scratch_shapes=[
            pltpu.VMEM((S, LANES), BF), pltpu.VMEM((S, LANES), BF),
            pltpu.VMEM((S + WINDOW, LANES), BF), pltpu.VMEM((S + WINDOW, LANES), BF),
            pltpu.VMEM((R * QB, LANES), F32), pltpu.VMEM((R * QB, LANES), F32),
        ],
        compiler_params=_params(("parallel", "parallel", "arbitrary")),
    )(proj3, proj3, proj3, proj3, proj3, kc, vc, small3, slope_tab, ovt)


def _merge_kernel(og_ref, on_ref, mg_ref, mn_ref, x_ref, wg_ref, wn_ref, wo_ref, g_ref, o_ref):
    a = jnp.dot(og_ref[...], wg_ref[...], preferred_element_type=F32)
    b = jnp.dot(on_ref[...], wn_ref[...], preferred_element_type=F32)
    mixed = jax.nn.sigmoid(mg_ref[...].astype(F32)) * a + jax.nn.sigmoid(mn_ref[...].astype(F32)) * b
    y = jnp.dot(mixed.astype(BF), wo_ref[...], preferred_element_type=F32)
    y = y * lax.rsqrt(jnp.mean(y * y, axis=-1, keepdims=True) + EPS) * g_ref[...]
    o_ref[...] = x_ref[...] + y


def _merge_out(o_gla2, o_nsa2, proj2, x2, wg, wn, wo, g, tm=512):
    n_tok = x2.shape[0]
    D = D_MODEL
    return pl.pallas_call(
        _merge_kernel,
        grid=(n_tok // tm,),
        in_specs=[
            pl.BlockSpec((tm, D), lambda i: (i, 0)),
            pl.BlockSpec((tm, D), lambda i: (i, 0)),
            pl.BlockSpec((tm, D), lambda i: (i, OFF_MG // D)),
            pl.BlockSpec((tm, D), lambda i: (i, OFF_MN // D)),
            pl.BlockSpec((tm, D), lambda i: (i, 0)),
            _resident(wg.shape, lambda i: (0, 0)),
            _resident(wn.shape, lambda i: (0, 0)),
            _resident(wo.shape, lambda i: (0, 0)),
            _resident((1, D), lambda i: (0, 0)),
        ],
        out_specs=pl.BlockSpec((tm, D), lambda i: (i, 0)),
        out_shape=jax.ShapeDtypeStruct((n_tok, D), F32),
        compiler_params=_params(("parallel",)),
    )(o_gla2, o_nsa2, proj2, proj2, x2, wg, wn, wo, g)


def _ffn_kernel(x_ref, gpre_ref, wg_ref, wu_ref, wd_ref, gpost_ref, o_ref, acc_scr, *, chunk):
    x = x_ref[...]
    h = (x * lax.rsqrt(jnp.mean(x * x, axis=-1, keepdims=True) + EPS) * gpre_ref[...]).astype(BF)
    d_ff = wg_ref.shape[1]
    for n, c0 in enumerate(range(0, d_ff, chunk)):
        c1 = min(c0 + chunk, d_ff)
        a = jnp.dot(h, wg_ref[:, c0:c1], preferred_element_type=F32)
        u = jnp.dot(h, wu_ref[:, c0:c1], preferred_element_type=F32)
        t = (a * jax.nn.sigmoid(a) * u).astype(BF)
        part = jnp.dot(t, wd_ref[c0:c1, :], preferred_element_type=F32)
        if n == 0:
            acc_scr[...] = part
        else:
            acc_scr[...] += part
    f = acc_scr[...]
    o_ref[...] = x + f * lax.rsqrt(jnp.mean(f * f, axis=-1, keepdims=True) + EPS) * gpost_ref[...]


def _ffn(x2, gpre, wg, wu, wd, gpost, tm=512, chunk=512):
    n_tok = x2.shape[0]
    D = D_MODEL
    kern = functools.partial(_ffn_kernel, chunk=chunk)
    return pl.pallas_call(
        kern,
        grid=(n_tok // tm,),
        in_specs=[
            pl.BlockSpec((tm, D), lambda i: (i, 0)),
            _resident((1, D), lambda i: (0, 0)),
            _resident(wg.shape, lambda i: (0, 0)),
            _resident(wu.shape, lambda i: (0, 0)),
            _resident(wd.shape, lambda i: (0, 0)),
            _resident((1, D), lambda i: (0, 0)),
        ],
        out_specs=pl.BlockSpec((tm, D), lambda i: (i, 0)),
        out_shape=jax.ShapeDtypeStruct((n_tok, D), F32),
        scratch_shapes=[pltpu.VMEM((tm, D), F32)],
        compiler_params=_params(("parallel",)),
    )(x2, gpre, wg, wu, wd, gpost)


def _pad_heads(w, n_heads, scale=1.0):
    d = w.shape[0]
    w = (w * scale).reshape(d, n_heads, NSA_DH)
    return jnp.pad(w, ((0, 0), (0, 0), (0, LANES - NSA_DH))).reshape(d, n_heads * LANES)


def _prep_in_weights(w_in):
    splits = np.cumsum([512, 512, 1024, 1024, GLA_RANK, 512, 128, 128, 128, 128, 128, 128, 24, 1024])
    (g_q, g_k, g_v, g_r, g_a, n_q, n_kc, n_vc, n_ks, n_vs, n_kw, n_vw, n_gate, m_g, m_n) = jnp.split(
        w_in, [int(s) for s in splits], axis=1)
    w_main = jnp.concatenate([
        g_q, g_k, g_v, g_r,
        _pad_heads(n_q, NSA_HEADS, NSA_DH ** -0.5),
        m_g, m_n,
        n_kc, n_vc,
        _pad_heads(n_ks, NSA_GROUPS), _pad_heads(n_vs, NSA_GROUPS),
        _pad_heads(n_kw, NSA_GROUPS), _pad_heads(n_vw, NSA_GROUPS)], axis=1).astype(BF)
    d = w_in.shape[0]
    per_g = NSA_REP * 3
    gates = jnp.pad(n_gate.reshape(d, NSA_GROUPS, per_g), ((0, 0), (0, 0), (0, LANES - per_g)))
    w_small = jnp.concatenate(
        [jnp.pad(g_a, ((0, 0), (0, LANES - GLA_RANK))), gates.reshape(d, NSA_GROUPS * LANES)], axis=1).astype(BF)
    return w_main, w_small


def _prep_compress(pe, w1, w2):
    eye = jnp.eye(NSA_GROUPS, dtype=F32)
    w1r = w1.reshape(CMP_LEN, NSA_DH, NSA_DH)
    w1e = jnp.einsum('lde,gh->lgdhe', w1r, eye).reshape(CMP_LEN * NSA_GROUPS * NSA_DH, NSA_GROUPS * NSA_DH)
    pe_e = jnp.broadcast_to(pe[:, None, :], (CMP_LEN, NSA_GROUPS, NSA_DH)).reshape(2, CMP_STRIDE * LANES)
    pe_e = jnp.pad(pe_e, ((0, 6), (0, 0)))
    w2e = jnp.stack([
        jnp.pad(jnp.pad(w2, ((g * NSA_DH, (NSA_GROUPS - 1 - g) * NSA_DH), (0, 0))), ((0, 0), (0, LANES - NSA_DH)))
        for g in range(NSA_GROUPS)])
    return pe_e.astype(F32), w1e.astype(BF), w2e.astype(BF)


def _overlap_table(seq):
    n_cmp = (seq - CMP_LEN) // CMP_STRIDE + 1
    n_slc = seq // SLC_LEN
    sc = CMP_STRIDE * np.arange(n_cmp)
    ss = SLC_LEN * np.arange(n_slc)
    ov = np.clip(np.minimum(sc[:, None] + CMP_LEN, ss[None, :] + SLC_LEN)
                 - np.maximum(sc[:, None], ss[None, :]), 0, None).astype(np.float32) / CMP_LEN
    ovt = np.zeros((NSA_DH, n_cmp + 1), np.float32)
    ovt[:n_slc, :n_cmp] = ov.T
    return jnp.asarray(ovt, dtype=BF)


def kernel(x, norm_mix_pre, norm_mix_post, norm_ffn_pre, norm_ffn_post, w_in, gla_w_alpha2, gla_b_alpha, gla_norm_g, nsa_cmp_pe_k, nsa_cmp_w1_k, nsa_cmp_w2_k, nsa_cmp_pe_v, nsa_cmp_w1_v, nsa_cmp_w2_v, w_proj_gla, w_proj_nsa, w_out, w_ffn_gate, w_ffn_up, w_ffn_down):
    B, S, D = x.shape
    depth = w_in.shape[0]
    n_tok = B * S
    h_idx = jnp.arange(NSA_HEADS, dtype=F32)
    slopes = jnp.exp2(-8.0 * (h_idx + 1.0) / NSA_HEADS).reshape(NSA_GROUPS, NSA_REP, 1)
    slope_tab = jnp.broadcast_to(jnp.pad(slopes, ((0, 0), (0, 8 - NSA_REP), (0, 0))), (NSA_GROUPS, 8, LANES))
    ovt = _overlap_table(S)
    x2 = x.reshape(n_tok, D)
    for l in range(depth):
        w_main, w_small = _prep_in_weights(w_in[l])
        proj2, small2 = _in_proj(x2, norm_mix_pre[l][None, :], w_main, w_small)
        proj3 = proj2.reshape(B, S, N_MAIN)
        small3 = small2.reshape(B, S, N_SMALL)

        w2p = jnp.pad(gla_w_alpha2[l], ((0, LANES - GLA_RANK), (0, 0)))
        o_gla = _gla(proj3, small3, w2p, gla_b_alpha[l][None, :], gla_norm_g[l][None, :])

        xk = proj3[:, :, OFF_KC:OFF_KC + LANES].reshape(B, S // CMP_STRIDE, CMP_STRIDE * LANES)
        xv = proj3[:, :, OFF_VC:OFF_VC + LANES].reshape(B, S // CMP_STRIDE, CMP_STRIDE * LANES)
        pek, w1k, w2k = _prep_compress(nsa_cmp_pe_k[l], nsa_cmp_w1_k[l], nsa_cmp_w2_k[l])
        pev, w1v, w2v = _prep_compress(nsa_cmp_pe_v[l], nsa_cmp_w1_v[l], nsa_cmp_w2_v[l])
        kc, vc = _nsa_compress(xk, xv, pek, pev, w1k, w1v, w2k, w2v)
        o_nsa = _nsa_attend(proj3, small3, kc, vc, slope_tab, ovt)

        wn = jnp.pad(w_proj_nsa[l].reshape(NSA_HEADS, NSA_DH, D), ((0, 0), (0, LANES - NSA_DH), (0, 0)))
        x2 = _merge_out(o_gla.reshape(n_tok, -1), o_nsa.reshape(n_tok, -1), proj2, x2,
                        w_proj_gla[l].astype(BF), wn.reshape(NSA_HEADS * LANES, D).astype(BF),
                        w_out[l].astype(BF), norm_mix_post[l][None, :])
        x2 = _ffn(x2, norm_ffn_pre[l][None, :], w_ffn_gate[l].astype(BF), w_ffn_up[l].astype(BF),
                  w_ffn_down[l].astype(BF), norm_ffn_post[l][None, :])
    return x2.reshape(B, S, D)
```

```python
import functools

import numpy as np
import jax
import jax.numpy as jnp
from jax import lax
from jax.experimental import pallas as pl
from jax.experimental.pallas import tpu as pltpu

D_MODEL = 1024
GLA_HEADS = 4
GLA_DK = 128
GLA_DV = 256
GLA_RANK = 16
GLA_TAU = 16.0
GLA_CHUNK = 64
NSA_HEADS = 8
NSA_GROUPS = 2
NSA_REP = 4
NSA_DH = 64
CMP_LEN = 32
CMP_STRIDE = 16
SLC_LEN = 64
N_SEL = 16
WINDOW = 512
QB = 128
SLC_TILE = 256
D_FF = 2816
EPS = 1e-6
NEG = -1e30

LANES = 128
VMEM_LIMIT = 56 * 1024 * 1024
BF = jnp.bfloat16
F32 = jnp.float32

OFF_GQ = 0
OFF_GK = 512
OFF_GV = 1024
OFF_GR = 2048
OFF_NQ = 3072
OFF_MG = 4096
OFF_MN = 5120
OFF_KC = 6144
OFF_VC = 6272
OFF_KS = 6400
OFF_VS = 6656
OFF_KW = 6912
OFF_VW = 7168
N_MAIN = 7424
N_SMALL = 384

NT = (((1,), (1,)), ((), ()))
TN = (((0,), (0,)), ((), ()))


def _resident(shape, index_map):
    return pl.BlockSpec(shape, index_map, pipeline_mode=pl.Buffered(1))


def _params(sem):
    return pltpu.CompilerParams(dimension_semantics=sem, vmem_limit_bytes=VMEM_LIMIT)


def _in_proj_kernel(x_ref, g_ref, wm_ref, ws_ref, om_ref, os_ref, h_scr):
    j = pl.program_id(1)

    @pl.when(j == 0)
    def _():
        x = x_ref[...]
        y = x * lax.rsqrt(jnp.mean(x * x, axis=-1, keepdims=True) + EPS) * g_ref[...]
        h_scr[...] = y.astype(BF)
        os_ref[...] = jnp.dot(h_scr[...], ws_ref[...], preferred_element_type=F32)

    tn = om_ref.shape[1]
    step = 512
    for c0 in range(0, tn, step):
        c1 = min(c0 + step, tn)
        om_ref[:, c0:c1] = jnp.dot(h_scr[...], wm_ref[:, c0:c1], preferred_element_type=F32).astype(BF)


def _in_proj(x2, g, w_main, w_small, tm=512, n_split=2):
    n_tok = x2.shape[0]
    tn = N_MAIN // n_split
    return pl.pallas_call(
        _in_proj_kernel,
        grid=(n_tok // tm, n_split),
        in_specs=[
            pl.BlockSpec((tm, D_MODEL), lambda i, j: (i, 0)),
            _resident((1, D_MODEL), lambda i, j: (0, 0)),
            pl.BlockSpec((D_MODEL, tn), lambda i, j: (0, j)),
            _resident((D_MODEL, N_SMALL), lambda i, j: (0, 0)),
        ],
        out_specs=[
            pl.BlockSpec((tm, tn), lambda i, j: (i, j)),
            pl.BlockSpec((tm, N_SMALL), lambda i, j: (i, 0)),
        ],
        out_shape=[
            jax.ShapeDtypeStruct((n_tok, N_MAIN), BF),
            jax.ShapeDtypeStruct((n_tok, N_SMALL), F32),
        ],
        scratch_shapes=[pltpu.VMEM((tm, D_MODEL), BF)],
        compiler_params=_params(("parallel", "arbitrary")),
    )(x2, g, w_main, w_small)


def _gla_kernel(q_ref, k_ref, v_ref, r_ref, a_ref, w2_ref, b2_ref, ng_ref, o_ref, st_scr, *, n_chunks):
    blk = pl.program_id(2)

    @pl.when(blk == 0)
    def _():
        st_scr[...] = jnp.zeros_like(st_scr)

    C = GLA_CHUNK
    z = jnp.dot(a_ref[0], w2_ref[...], precision=lax.Precision.HIGHEST,
                preferred_element_type=F32) + b2_ref[...]
    log_a = (jnp.minimum(z, 0.0) - jnp.log1p(jnp.exp(-jnp.abs(z)))) * (1.0 / GLA_TAU)
    row = lax.broadcasted_iota(jnp.int32, (C, C), 0)
    col = lax.broadcasted_iota(jnp.int32, (C, C), 1)
    causal = col <= row
    tri = causal.astype(F32)
    scale = GLA_DK ** -0.5
    ng = ng_ref[...]
    for c in range(n_chunks):
        sl = slice(c * C, (c + 1) * C)
        bcum = jnp.dot(tri, log_a[sl], precision=lax.Precision.HIGHEST, preferred_element_type=F32)
        b_last = bcum[C - 1:C, :]
        q = q_ref[0, sl, :].astype(F32)
        k = k_ref[0, sl, :].astype(F32)
        v = v_ref[0, sl, :]
        qe = (q * (scale * jnp.exp(bcum))).astype(BF)
        ke = (k * jnp.exp(-bcum)).astype(BF)
        kd = (k * jnp.exp(b_last - bcum)).astype(BF)
        attn = lax.dot_general(qe, ke, NT, preferred_element_type=F32)
        attn = jnp.where(causal, attn, 0.0).astype(BF)
        st = st_scr[...]
        o = jnp.dot(attn, v, preferred_element_type=F32)
        o = o + lax.dot_general(qe, st.astype(BF), NT, preferred_element_type=F32)
        upd = lax.dot_general(v, kd, TN, preferred_element_type=F32)
        st_scr[...] = st * jnp.exp(b_last) + upd
        o = o * lax.rsqrt(jnp.mean(o * o, axis=-1, keepdims=True) + EPS) * ng
        r = r_ref[0, sl, :].astype(F32)
        o_ref[0, sl, :] = (o * (r * jax.nn.sigmoid(r))).astype(BF)


def _gla(proj3, small3, w2p, b2, ng, blk_tokens=512):
    B, S, _ = proj3.shape
    nblk = S // blk_tokens
    kern = functools.partial(_gla_kernel, n_chunks=blk_tokens // GLA_CHUNK)
    return pl.pallas_call(
        kern,
        grid=(B, GLA_HEADS, nblk),
        in_specs=[
            pl.BlockSpec((1, blk_tokens, GLA_DK), lambda b, h, i: (b, i, OFF_GQ // GLA_DK + h)),
            pl.BlockSpec((1, blk_tokens, GLA_DK), lambda b, h, i: (b, i, OFF_GK // GLA_DK + h)),
            pl.BlockSpec((1, blk_tokens, GLA_DV), lambda b, h, i: (b, i, OFF_GV // GLA_DV + h)),
            pl.BlockSpec((1, blk_tokens, GLA_DV), lambda b, h, i: (b, i, OFF_GR // GLA_DV + h)),
            pl.BlockSpec((1, blk_tokens, LANES), lambda b, h, i: (b, i, 0)),
            pl.BlockSpec((LANES, GLA_DK), lambda b, h, i: (0, h)),
            pl.BlockSpec((1, GLA_DK), lambda b, h, i: (0, h)),
            pl.BlockSpec((1, GLA_DV), lambda b, h, i: (0, 0)),
        ],
        out_specs=pl.BlockSpec((1, blk_tokens, GLA_DV), lambda b, h, i: (b, i, h)),
        out_shape=jax.ShapeDtypeStruct((B, S, GLA_HEADS * GLA_DV), BF),
        scratch_shapes=[pltpu.VMEM((GLA_DV, GLA_DK), F32)],
        compiler_params=_params(("parallel", "parallel", "arbitrary")),
    )(proj3, proj3, proj3, proj3, small3, w2p, b2, ng)


def _compress_kernel(xk_ref, xv_ref, pek_ref, pev_ref, w1k_ref, w1v_ref, w2k_ref, w2v_ref, kc_ref, vc_ref):
    n_rows = xk_ref.shape[1]
    half = CMP_STRIDE * LANES
    lane = lax.broadcasted_iota(jnp.int32, (n_rows, LANES), 1)
    row = lax.broadcasted_iota(jnp.int32, (n_rows, LANES), 0)
    end_c = CMP_STRIDE * row + (CMP_LEN - 1)
    c_k = jnp.where(lane == NSA_DH, end_c >> 6, jnp.where(lane == NSA_DH + 1, end_c & 63, 0)).astype(F32)
    c_v = jnp.where(lane == NSA_DH, 1.0, 0.0).astype(F32)

    def branch(x_ref, pe_ref, w1_ref, w2_ref, const, o_ref):
        x = x_ref[0].astype(F32)
        xa = (x + pe_ref[0:1, :]).astype(BF)
        xb = (x + pe_ref[1:2, :]).astype(BF)
        a = jnp.dot(xa, w1_ref[0:half, :], preferred_element_type=F32)
        b = jnp.dot(xb, w1_ref[half:2 * half, :], preferred_element_type=F32)
        pre = a + pltpu.roll(b, n_rows - 1, 0)
        hid = (pre * jax.nn.sigmoid(pre)).astype(BF)
        for g in range(NSA_GROUPS):
            o_ref[0, g] = (jnp.dot(hid, w2_ref[g], preferred_element_type=F32) + const).astype(BF)

    branch(xk_ref, pek_ref, w1k_ref, w2k_ref, c_k, kc_ref)
    branch(xv_ref, pev_ref, w1v_ref, w2v_ref, c_v, vc_ref)


def _nsa_compress(xk, xv, pek, pev, w1k, w1v, w2k, w2v):
    B, n_rows, width = xk.shape
    full = lambda shape: _resident(shape, lambda b: (0,) * len(shape))
    out = jax.ShapeDtypeStruct((B, NSA_GROUPS, n_rows, LANES), BF)
    return pl.pallas_call(
        _compress_kernel,
        grid=(B,),
        in_specs=[
            pl.BlockSpec((1, n_rows, width), lambda b: (b, 0, 0)),
            pl.BlockSpec((1, n_rows, width), lambda b: (b, 0, 0)),
            full(pek.shape), full(pev.shape), full(w1k.shape), full(w1v.shape),
            full(w2k.shape), full(w2v.shape),
        ],
        out_specs=[pl.BlockSpec((1, NSA_GROUPS, n_rows, LANES), lambda b: (b, 0, 0, 0))] * 2,
        out_shape=[out, out],
        compiler_params=_params(("parallel",)),
    )(xk, xv, pek, pev, w1k, w1v, w2k, w2v)


def _nsa_kernel(q_ref, ks_ref, vs_ref, kw_ref, vw_ref, kc_ref, vc_ref, gt_ref, sl_ref, ov_ref,
                o_ref, ksa, kwa, vst, vwt, vct, acc_scr, act_ref, *, seq):
    qi = pl.program_id(2)
    n_slc = seq // SLC_LEN
    R = NSA_REP
    KT = SLC_TILE
    WK = WINDOW + QB
    n_cmp = kc_ref.shape[2]

    def t_bf(x):
        return x.astype(F32).T.astype(BF)

    @pl.when(qi == 0)
    def _():
        lane = lax.broadcasted_iota(jnp.int32, (seq, LANES), 1)
        pos = lax.broadcasted_iota(jnp.int32, (seq, LANES), 0)
        blk = pos >> 6
        off = pos & 63
        c_s = jnp.where(lane == LANES - 1, off,
                        jnp.where((lane >= NSA_DH) & (lane - (NSA_DH - 1) == blk), 1, 0))
        ksa[...] = ks_ref[0] + c_s.astype(BF)
        c_w = jnp.where(lane == NSA_DH, blk, jnp.where(lane == NSA_DH + 1, off, 0))
        kwa[0:WINDOW, :] = jnp.zeros((WINDOW, LANES), BF)
        kwa[WINDOW:WINDOW + seq, :] = kw_ref[0] + c_w.astype(BF)
        ones = jnp.where(lax.broadcasted_iota(jnp.int32, (LANES, LANES), 1) == NSA_DH, 1, 0).astype(BF)
        n_pad = WINDOW // LANES
        for c in range(n_pad):
            vwt[c] = jnp.zeros((LANES, LANES), BF)

        def fill(c, carry):
            rows = pl.ds(pl.multiple_of(c * LANES, LANES), LANES)
            vst[c] = t_bf(vs_ref[0, rows, :] + ones)
            vwt[c + n_pad] = t_bf(vw_ref[0, rows, :] + ones)
            return carry

        lax.fori_loop(0, seq // LANES, fill, 0)
        for c in range(n_cmp // LANES):
            vct[:, c * LANES:(c + 1) * LANES] = t_bf(vc_ref[0, 0, c * LANES:(c + 1) * LANES, :])

    q0 = qi * QB
    rowi = lax.broadcasted_iota(jnp.int32, (LANES, QB), 0)
    slopes = [sl_ref[0, r:r + 1, :] for r in range(R)]
    q_t = [q_ref[0, :, r * LANES:(r + 1) * LANES].astype(F32).T for r in range(R)]

    qw = jnp.concatenate(
        [(q_t[r] + jnp.where(rowi == NSA_DH, slopes[r] * 64.0, jnp.where(rowi == NSA_DH + 1, slopes[r], 0.0))
          ).astype(BF) for r in range(R)], axis=1)

    def tile4(x):
        return jnp.concatenate([x] * R, axis=1)

    s_c = jnp.dot(kc_ref[0, 0], qw, preferred_element_type=F32)
    e_c = CMP_STRIDE * lax.broadcasted_iota(jnp.int32, (n_cmp, QB), 0) + (CMP_LEN - 1)
    t_c = q0 + lax.broadcasted_iota(jnp.int32, (n_cmp, QB), 1)
    mask_c = tile4(e_c <= t_c)
    s_c = jnp.where(mask_c, s_c, NEG)
    m_c = jnp.max(s_c, axis=0, keepdims=True)
    p_c = jnp.where(mask_c, jnp.exp(s_c - m_c), 0.0)
    l_c = jnp.sum(p_c, axis=0, keepdims=True)
    p_c = p_c * jnp.where(l_c > 0.0, 1.0 / l_c, 0.0)
    o_c = jnp.dot(vct[...], p_c.astype(BF), preferred_element_type=F32)

    psum = p_c[:, 0:QB]
    for r in range(1, R):
        psum = psum + p_c[:, r * QB:(r + 1) * QB]
    ovt = ov_ref[...]
    p_hi = psum.astype(BF)
    rem = psum - p_hi.astype(F32)
    p_mid = rem.astype(BF)
    p_lo = (rem - p_mid.astype(F32)).astype(BF)
    imp = (jnp.dot(ovt, p_hi, preferred_element_type=F32)
           + jnp.dot(ovt, p_mid, preferred_element_type=F32)
           + jnp.dot(ovt, p_lo, preferred_element_type=F32))

    NR = ovt.shape[0]
    SUB = 8
    jblk = lax.broadcasted_iota(jnp.int32, (NR, QB), 0)
    t_q = q0 + lax.broadcasted_iota(jnp.int32, (NR, QB), 1)
    cur = t_q >> 6
    forced = (jblk == 0) | (jblk == cur) | (jblk == cur - 1)
    score = jnp.where(jblk > cur, NEG, jnp.where(forced, -NEG, imp))
    n_slab = -(-n_slc // SUB)
    slabs = [score[a * SUB:(a + 1) * SUB, :] for a in range(n_slab)]
    cnts = [jnp.zeros((SUB, QB), F32) for _ in range(n_slab)]
    isub = lax.broadcasted_iota(jnp.int32, (SUB, QB), 0)
    for jp in range(n_slc):
        rowv = jnp.broadcast_to(score[jp:jp + 1, :], (SUB, QB))
        for a in range(n_slab):
            if a < jp // SUB:
                beats = jnp.where(rowv > slabs[a], 1.0, 0.0)
            elif a > jp // SUB:
                beats = jnp.where(rowv >= slabs[a], 1.0, 0.0)
            else:
                beats = jnp.where(isub > jp % SUB,
                                  jnp.where(rowv >= slabs[a], 1.0, 0.0), jnp.where(rowv > slabs[a], 1.0, 0.0))
            cnts[a] = cnts[a] + beats
    cnt = jnp.concatenate(cnts + [jnp.full((NR - n_slab * SUB, QB), float(NR), F32)] * (NR > n_slab * SUB), axis=0)
    sel = (cnt < float(N_SEL)) & (jblk <= cur) & (jblk < n_slc)
    a_nat = jnp.where(jblk == 0, 1.0, jnp.where(sel, (SLC_LEN * jblk).astype(F32), NEG))
    a_nat = jnp.where(jblk < n_slc, a_nat, 0.0)
    a_t = pltpu.roll(a_nat, NR - 1, 0)
    a_pad = jnp.concatenate([jnp.zeros((LANES - NR, QB), F32), a_t], axis=0)

    any_q = jnp.max(jnp.where(sel, 1.0, 0.0), axis=1, keepdims=True)
    jcol = lax.broadcasted_iota(jnp.int32, (NR, 1), 0)
    bits = jnp.where(any_q > 0.0, lax.shift_left(jnp.int32(1), jcol & 31), 0)
    word0 = jnp.sum(jnp.where(jcol < 32, bits, 0))
    word1 = jnp.sum(jnp.where(jcol >= 32, bits, 0))

    qs = jnp.concatenate([(q_t[r] + a_pad * slopes[r]).astype(BF) for r in range(R)], axis=1)

    kw = kwa[pl.ds(pl.multiple_of(q0, QB), WK), :]
    s_w = jnp.dot(kw, qw, preferred_element_type=F32)
    c_w = lax.broadcasted_iota(jnp.int32, (WK, QB), 0)
    d_w = c_w - lax.broadcasted_iota(jnp.int32, (WK, QB), 1)
    mask_w = (d_w > 0) & (d_w <= WINDOW) & (c_w >= WINDOW - q0)
    s_w = s_w + tile4(jnp.where(mask_w, 0.0, NEG))
    m_w = jnp.max(s_w, axis=0, keepdims=True)
    p_w = jnp.exp(s_w - m_w).astype(BF)
    vw_t = jnp.concatenate([vwt[qi + c] for c in range(WK // LANES)], axis=1)
    acc_w = jnp.dot(vw_t, p_w, preferred_element_type=F32)

    tiles_per = KT // LANES
    blocks_per = KT // SLC_LEN
    n_full = q0 // KT

    def scan(ti, n):
        word = jnp.where(ti < 32 // blocks_per, word0, word1)
        hit = (lax.shift_right_logical(word, (ti * blocks_per) & 31) & ((1 << blocks_per) - 1)) != 0
        act_ref[n] = ti
        return n + jnp.where(hit, 1, 0)

    n_act = lax.fori_loop(0, n_full, scan, 0)
    act_ref[n_act] = n_full

    def scores(ti):
        k0 = pl.multiple_of(ti * KT, KT)
        return jnp.dot(ksa[pl.ds(k0, KT), :], qs, preferred_element_type=F32)

    def absorb(s, ti, m_prev, diag):
        if diag:
            p_d = ti * KT + lax.broadcasted_iota(jnp.int32, (KT, QB), 0)
            t_d = q0 + lax.broadcasted_iota(jnp.int32, (KT, QB), 1)
            s = s + tile4(jnp.where(p_d <= t_d, 0.0, NEG))
        m_new = jnp.maximum(m_prev, jnp.max(s, axis=0, keepdims=True))
        alpha = jnp.exp(m_prev - m_new)
        p = jnp.exp(s - m_new).astype(BF)
        v_t = jnp.concatenate([vst[ti * tiles_per + c] for c in range(tiles_per)], axis=1)
        acc_scr[...] = acc_scr[...] * alpha + jnp.dot(v_t, p, preferred_element_type=F32)
        return m_new

    acc_scr[...] = jnp.zeros(acc_scr.shape, F32)

    def body(i, carry):
        s, m_prev = carry
        s_next = scores(act_ref[i + 1])
        return s_next, absorb(s, act_ref[i], m_prev, False)

    s_last, m_last = lax.fori_loop(0, n_act, body, (scores(act_ref[0]), jnp.full((1, R * QB), NEG, F32)))
    absorb(s_last, n_full, m_last, True)
    acc_s = acc_scr[...]


    sg_t = jax.nn.sigmoid(gt_ref[0]).T
    for r in range(R):
        cs = slice(r * QB, (r + 1) * QB)
        a_s = acc_s[:, cs]
        a_w = acc_w[:, cs]
        g_c = sg_t[3 * r:3 * r + 1, :]
        g_s = sg_t[3 * r + 1:3 * r + 2, :] / a_s[NSA_DH:NSA_DH + 1, :]
        g_w = sg_t[3 * r + 2:3 * r + 3, :] / a_w[NSA_DH:NSA_DH + 1, :]
        out_t = g_c * o_c[:, cs] + g_s * a_s + g_w * a_w
        o_ref[0, :, r * LANES:(r + 1) * LANES] = out_t.T.astype(BF)


def _nsa_attend(proj3, small3, kc, vc, slope_tab, ovt):
    B, S, _ = proj3.shape
    G, R = NSA_GROUPS, NSA_REP
    n_cmp = kc.shape[2]
    kern = functools.partial(_nsa_kernel, seq=S)
    kv_spec = lambda off: pl.BlockSpec((1, S, LANES), lambda b, g, i: (b, 0, off // LANES + g))
    return pl.pallas_call(
        kern,
        grid=(B, G, S // QB),
        in_specs=[
            pl.BlockSpec((1, QB, R * LANES), lambda b, g, i: (b, i, OFF_NQ // (R * LANES) + g)),
            kv_spec(OFF_KS), kv_spec(OFF_VS), kv_spec(OFF_KW), kv_spec(OFF_VW),
            pl.BlockSpec((1, 1, n_cmp, LANES), lambda b, g, i: (b, g, 0, 0)),
            pl.BlockSpec((1, 1, n_cmp, LANES), lambda b, g, i: (b, g, 0, 0)),
            pl.BlockSpec((1, QB, LANES), lambda b, g, i: (b, i, 1 + g)),
            pl.BlockSpec((1, 8, LANES), lambda b, g, i: (g, 0, 0)),
            _resident(ovt.shape, lambda b, g, i: (0, 0)),
        ],
        out_specs=pl.BlockSpec((1, QB, R * LANES), lambda b, g, i: (b, i, g)),
        out_shape=jax.ShapeDtypeStruct((B, S, G * R * LANES), BF),
        scratch_shapes=[
            pltpu.VMEM((S, LANES), BF),
            pltpu.VMEM((S + WINDOW, LANES), BF),
            pltpu.VMEM((S // LANES, LANES, LANES), BF),
            pltpu.VMEM(((S + WINDOW) // LANES, LANES, LANES), BF),
            pltpu.VMEM((LANES, n_cmp), BF),
            pltpu.VMEM((LANES, R * QB), F32),
            pltpu.SMEM((S // SLC_TILE + 1,), jnp.int32),
        ],
        compiler_params=_params(("parallel", "parallel", "arbitrary")),
    )(proj3, proj3, proj3, proj3, proj3, kc, vc, small3, slope_tab, ovt)


def _merge_kernel(og_ref, on_ref, mg_ref, mn_ref, x_ref, wg_ref, wn_ref, wo_ref, g_ref, o_ref):
    a = jnp.dot(og_ref[...], wg_ref[...], preferred_element_type=F32)
    b = jnp.dot(on_ref[...], wn_ref[...], preferred_element_type=F32)
    mixed = jax.nn.sigmoid(mg_ref[...].astype(F32)) * a + jax.nn.sigmoid(mn_ref[...].astype(F32)) * b
    y = jnp.dot(mixed.astype(BF), wo_ref[...], preferred_element_type=F32)
    y = y * lax.rsqrt(jnp.mean(y * y, axis=-1, keepdims=True) + EPS) * g_ref[...]
    o_ref[...] = x_ref[...] + y


def _merge_out(o_gla2, o_nsa2, proj2, x2, wg, wn, wo, g, tm=512):
    n_tok = x2.shape[0]
    D = D_MODEL
    return pl.pallas_call(
        _merge_kernel,
        grid=(n_tok // tm,),
        in_specs=[
            pl.BlockSpec((tm, D), lambda i: (i, 0)),
            pl.BlockSpec((tm, D), lambda i: (i, 0)),
            pl.BlockSpec((tm, D), lambda i: (i, OFF_MG // D)),
            pl.BlockSpec((tm, D), lambda i: (i, OFF_MN // D)),
            pl.BlockSpec((tm, D), lambda i: (i, 0)),
            _resident(wg.shape, lambda i: (0, 0)),
            _resident(wn.shape, lambda i: (0, 0)),
            _resident(wo.shape, lambda i: (0, 0)),
            _resident((1, D), lambda i: (0, 0)),
        ],
        out_specs=pl.BlockSpec((tm, D), lambda i: (i, 0)),
        out_shape=jax.ShapeDtypeStruct((n_tok, D), F32),
        compiler_params=_params(("parallel",)),
    )(o_gla2, o_nsa2, proj2, proj2, x2, wg, wn, wo, g)


def _ffn_kernel(x_ref, gpre_ref, wg_ref, wu_ref, wd_ref, gpost_ref, o_ref, acc_scr, *, chunk):
    x = x_ref[...]
    h = (x * lax.rsqrt(jnp.mean(x * x, axis=-1, keepdims=True) + EPS) * gpre_ref[...]).astype(BF)
    d_ff = wg_ref.shape[1]
    for n, c0 in enumerate(range(0, d_ff, chunk)):
        c1 = min(c0 + chunk, d_ff)
        a = jnp.dot(h, wg_ref[:, c0:c1], preferred_element_type=F32)
        u = jnp.dot(h, wu_ref[:, c0:c1], preferred_element_type=F32)
        t = (a * jax.nn.sigmoid(a) * u).astype(BF)
        part = jnp.dot(t, wd_ref[c0:c1, :], preferred_element_type=F32)
        if n == 0:
            acc_scr[...] = part
        else:
            acc_scr[...] += part
    f = acc_scr[...]
    o_ref[...] = x + f * lax.rsqrt(jnp.mean(f * f, axis=-1, keepdims=True) + EPS) * gpost_ref[...]


def _ffn(x2, gpre, wg, wu, wd, gpost, tm=512, chunk=512):
    n_tok = x2.shape[0]
    D = D_MODEL
    kern = functools.partial(_ffn_kernel, chunk=chunk)
    return pl.pallas_call(
        kern,
        grid=(n_tok // tm,),
        in_specs=[
            pl.BlockSpec((tm, D), lambda i: (i, 0)),
            _resident((1, D), lambda i: (0, 0)),
            _resident(wg.shape, lambda i: (0, 0)),
            _resident(wu.shape, lambda i: (0, 0)),
            _resident(wd.shape, lambda i: (0, 0)),
            _resident((1, D), lambda i: (0, 0)),
        ],
        out_specs=pl.BlockSpec((tm, D), lambda i: (i, 0)),
        out_shape=jax.ShapeDtypeStruct((n_tok, D), F32),
        scratch_shapes=[pltpu.VMEM((tm, D), F32)],
        compiler_params=_params(("parallel",)),
    )(x2, gpre, wg, wu, wd, gpost)


def _pad_heads(w, n_heads, scale=1.0):
    d = w.shape[0]
    w = (w * scale).reshape(d, n_heads, NSA_DH)
    return jnp.pad(w, ((0, 0), (0, 0), (0, LANES - NSA_DH))).reshape(d, n_heads * LANES)


def _prep_in_weights(w_in):
    splits = np.cumsum([512, 512, 1024, 1024, GLA_RANK, 512, 128, 128, 128, 128, 128, 128, 24, 1024])
    (g_q, g_k, g_v, g_r, g_a, n_q, n_kc, n_vc, n_ks, n_vs, n_kw, n_vw, n_gate, m_g, m_n) = jnp.split(
        w_in, [int(s) for s in splits], axis=1)
    w_main = jnp.concatenate([
        g_q, g_k, g_v, g_r,
        _pad_heads(n_q, NSA_HEADS, NSA_DH ** -0.5),
        m_g, m_n,
        n_kc, n_vc,
        _pad_heads(n_ks, NSA_GROUPS), _pad_heads(n_vs, NSA_GROUPS),
        _pad_heads(n_kw, NSA_GROUPS), _pad_heads(n_vw, NSA_GROUPS)], axis=1).astype(BF)
    d = w_in.shape[0]
    per_g = NSA_REP * 3
    gates = jnp.pad(n_gate.reshape(d, NSA_GROUPS, per_g), ((0, 0), (0, 0), (0, LANES - per_g)))
    w_small = jnp.concatenate(
        [jnp.pad(g_a, ((0, 0), (0, LANES - GLA_RANK))), gates.reshape(d, NSA_GROUPS * LANES)], axis=1).astype(BF)
    return w_main, w_small


def _prep_compress(pe, w1, w2):
    eye = jnp.eye(NSA_GROUPS, dtype=F32)
    w1r = w1.reshape(CMP_LEN, NSA_DH, NSA_DH)
    w1e = jnp.einsum('lde,gh->lgdhe', w1r, eye).reshape(CMP_LEN * NSA_GROUPS * NSA_DH, NSA_GROUPS * NSA_DH)
    pe_e = jnp.broadcast_to(pe[:, None, :], (CMP_LEN, NSA_GROUPS, NSA_DH)).reshape(2, CMP_STRIDE * LANES)
    pe_e = jnp.pad(pe_e, ((0, 6), (0, 0)))
    w2e = jnp.stack([
        jnp.pad(jnp.pad(w2, ((g * NSA_DH, (NSA_GROUPS - 1 - g) * NSA_DH), (0, 0))), ((0, 0), (0, LANES - NSA_DH)))
        for g in range(NSA_GROUPS)])
    return pe_e.astype(F32), w1e.astype(BF), w2e.astype(BF)


def _overlap_table(seq):
    n_cmp = (seq - CMP_LEN) // CMP_STRIDE + 1
    n_slc = seq // SLC_LEN
    sc = CMP_STRIDE * np.arange(n_cmp)
    ss = SLC_LEN * np.arange(n_slc)
    ov = np.clip(np.minimum(sc[:, None] + CMP_LEN, ss[None, :] + SLC_LEN)
                 - np.maximum(sc[:, None], ss[None, :]), 0, None).astype(np.float32) / CMP_LEN
    ovt = np.zeros((NSA_DH, n_cmp + 1), np.float32)
    ovt[:n_slc, :n_cmp] = ov.T
    return jnp.asarray(ovt, dtype=BF)


def kernel(x, norm_mix_pre, norm_mix_post, norm_ffn_pre, norm_ffn_post, w_in, gla_w_alpha2, gla_b_alpha, gla_norm_g, nsa_cmp_pe_k, nsa_cmp_w1_k, nsa_cmp_w2_k, nsa_cmp_pe_v, nsa_cmp_w1_v, nsa_cmp_w2_v, w_proj_gla, w_proj_nsa, w_out, w_ffn_gate, w_ffn_up, w_ffn_down):
    B, S, D = x.shape
    depth = w_in.shape[0]
    n_tok = B * S
    h_idx = jnp.arange(NSA_HEADS, dtype=F32)
    slopes = jnp.exp2(-8.0 * (h_idx + 1.0) / NSA_HEADS).reshape(NSA_GROUPS, NSA_REP, 1)
    slope_tab = jnp.broadcast_to(jnp.pad(slopes, ((0, 0), (0, 8 - NSA_REP), (0, 0))), (NSA_GROUPS, 8, LANES))
    ovt = _overlap_table(S)
    x2 = x.reshape(n_tok, D)
    for l in range(depth):
        w_main, w_small = _prep_in_weights(w_in[l])
        proj2, small2 = _in_proj(x2, norm_mix_pre[l][None, :], w_main, w_small)
        proj3 = proj2.reshape(B, S, N_MAIN)
        small3 = small2.reshape(B, S, N_SMALL)

        w2p = jnp.pad(gla_w_alpha2[l], ((0, LANES - GLA_RANK), (0, 0)))
        o_gla = _gla(proj3, small3, w2p, gla_b_alpha[l][None, :], gla_norm_g[l][None, :])

        xk = proj3[:, :, OFF_KC:OFF_KC + LANES].reshape(B, S // CMP_STRIDE, CMP_STRIDE * LANES)
        xv = proj3[:, :, OFF_VC:OFF_VC + LANES].reshape(B, S // CMP_STRIDE, CMP_STRIDE * LANES)
        pek, w1k, w2k = _prep_compress(nsa_cmp_pe_k[l], nsa_cmp_w1_k[l], nsa_cmp_w2_k[l])
        pev, w1v, w2v = _prep_compress(nsa_cmp_pe_v[l], nsa_cmp_w1_v[l], nsa_cmp_w2_v[l])
        kc, vc = _nsa_compress(xk, xv, pek, pev, w1k, w1v, w2k, w2v)
        o_nsa = _nsa_attend(proj3, small3, kc, vc, slope_tab, ovt)

        wn = jnp.pad(w_proj_nsa[l].reshape(NSA_HEADS, NSA_DH, D), ((0, 0), (0, LANES - NSA_DH), (0, 0)))
        x2 = _merge_out(o_gla.reshape(n_tok, -1), o_nsa.reshape(n_tok, -1), proj2, x2,
                        w_proj_gla[l].astype(BF), wn.reshape(NSA_HEADS * LANES, D).astype(BF),
                        w_out[l].astype(BF), norm_mix_post[l][None, :])
        x2 = _ffn(x2, norm_ffn_pre[l][None, :], w_ffn_gate[l].astype(BF), w_ffn_up[l].astype(BF),
                  w_ffn_down[l].astype(BF), norm_ffn_post[l][None, :])
    return x2.reshape(B, S, D)
```

```python
import functools

import numpy as np
import jax
import jax.numpy as jnp
from jax import lax
from jax.experimental import pallas as pl
from jax.experimental.pallas import tpu as pltpu

D_MODEL = 1024
GLA_HEADS = 4
GLA_DK = 128
GLA_DV = 256
GLA_RANK = 16
GLA_TAU = 16.0
GLA_CHUNK = 64
NSA_HEADS = 8
NSA_GROUPS = 2
NSA_REP = 4
NSA_DH = 64
CMP_LEN = 32
CMP_STRIDE = 16
SLC_LEN = 64
N_SEL = 16
WINDOW = 512
QB = 128
SLC_TILE = 256
D_FF = 2816
EPS = 1e-6
NEG = -1e30

LANES = 128
VMEM_LIMIT = 56 * 1024 * 1024
BF = jnp.bfloat16
F32 = jnp.float32

OFF_GQ = 0
OFF_GK = 512
OFF_GV = 1024
OFF_GR = 2048
OFF_NQ = 3072
OFF_MG = 4096
OFF_MN = 5120
OFF_KC = 6144
OFF_VC = 6272
OFF_KS = 6400
OFF_VS = 6656
OFF_KW = 6912
OFF_VW = 7168
N_MAIN = 7424
N_SMALL = 384

NT = (((1,), (1,)), ((), ()))
TN = (((0,), (0,)), ((), ()))


def _resident(shape, index_map):
    return pl.BlockSpec(shape, index_map, pipeline_mode=pl.Buffered(1))


def _params(sem):
    return pltpu.CompilerParams(dimension_semantics=sem, vmem_limit_bytes=VMEM_LIMIT)


def _in_proj_kernel(x_ref, g_ref, wm_ref, ws_ref, om_ref, os_ref, h_scr):
    j = pl.program_id(1)

    @pl.when(j == 0)
    def _():
        x = x_ref[...]
        y = x * lax.rsqrt(jnp.mean(x * x, axis=-1, keepdims=True) + EPS) * g_ref[...]
        h_scr[...] = y.astype(BF)
        os_ref[...] = jnp.dot(h_scr[...], ws_ref[...], preferred_element_type=F32)

    tn = om_ref.shape[1]
    step = 512
    for c0 in range(0, tn, step):
        c1 = min(c0 + step, tn)
        om_ref[:, c0:c1] = jnp.dot(h_scr[...], wm_ref[:, c0:c1], preferred_element_type=F32).astype(BF)


def _in_proj(x2, g, w_main, w_small, tm=512, n_split=2):
    n_tok = x2.shape[0]
    tn = N_MAIN // n_split
    return pl.pallas_call(
        _in_proj_kernel,
        grid=(n_tok // tm, n_split),
        in_specs=[
            pl.BlockSpec((tm, D_MODEL), lambda i, j: (i, 0)),
            _resident((1, D_MODEL), lambda i, j: (0, 0)),
            pl.BlockSpec((D_MODEL, tn), lambda i, j: (0, j)),
            _resident((D_MODEL, N_SMALL), lambda i, j: (0, 0)),
        ],
        out_specs=[
            pl.BlockSpec((tm, tn), lambda i, j: (i, j)),
            pl.BlockSpec((tm, N_SMALL), lambda i, j: (i, 0)),
        ],
        out_shape=[
            jax.ShapeDtypeStruct((n_tok, N_MAIN), BF),
            jax.ShapeDtypeStruct((n_tok, N_SMALL), F32),
        ],
        scratch_shapes=[pltpu.VMEM((tm, D_MODEL), BF)],
        compiler_params=_params(("parallel", "arbitrary")),
    )(x2, g, w_main, w_small)


def _gla_kernel(q_ref, k_ref, v_ref, r_ref, a_ref, w2_ref, b2_ref, ng_ref, o_ref, st_scr, *, n_chunks):
    blk = pl.program_id(2)

    @pl.when(blk == 0)
    def _():
        st_scr[...] = jnp.zeros_like(st_scr)

    C = GLA_CHUNK
    T = n_chunks * C

    def split3(x):
        hi = x.astype(BF)
        rem = x - hi.astype(F32)
        mid = rem.astype(BF)
        return hi, mid, (rem - mid.astype(F32)).astype(BF)

    a = a_ref[0]
    a_hi = a.astype(BF)
    a_lo = (a - a_hi.astype(F32)).astype(BF)
    lane = lax.broadcasted_iota(jnp.int32, (T, LANES), 1)
    in_lo = (lane >= GLA_RANK) & (lane < 2 * GLA_RANK)
    z = jnp.dot(jnp.where(in_lo, a_lo, a_hi), w2_ref[...], preferred_element_type=F32) + b2_ref[...]
    log_a = (jnp.minimum(z, 0.0) - jnp.log1p(jnp.exp(-jnp.abs(z)))) * (1.0 / GLA_TAU)

    x_wide = jnp.concatenate([log_a[c * C:(c + 1) * C] for c in range(n_chunks)], axis=1)
    x3 = jnp.concatenate(split3(x_wide), axis=0)
    r3 = lax.broadcasted_iota(jnp.int32, (C, 3 * C), 0)
    c3 = lax.broadcasted_iota(jnp.int32, (C, 3 * C), 1) & (C - 1)
    tri3 = jnp.where(c3 <= r3, 1.0, 0.0).astype(BF)
    b_wide = jnp.dot(tri3, x3, preferred_element_type=F32)
    bcum = jnp.concatenate([b_wide[:, c * LANES:(c + 1) * LANES] for c in range(n_chunks)], axis=0)
    last_rows = [b_wide[C - 1:C, c * LANES:(c + 1) * LANES] for c in range(n_chunks)]
    b_last = jnp.concatenate([jnp.broadcast_to(lr, (C, LANES)) for lr in last_rows], axis=0)

    q = q_ref[0].astype(F32)
    k = k_ref[0].astype(F32)
    v = v_ref[0]
    qe = (q * ((GLA_DK ** -0.5) * jnp.exp(bcum))).astype(BF)
    ke = (k * jnp.exp(-bcum)).astype(BF)
    kd = (k * jnp.exp(b_last - bcum)).astype(BF)

    H = min(T, 4 * C)
    row = lax.broadcasted_iota(jnp.int32, (H, H), 0)
    col = lax.broadcasted_iota(jnp.int32, (H, H), 1)
    keep = (col <= row) & ((col >> 6) == (row >> 6))
    intra = []
    for h0 in range(0, T, H):
        hs = slice(h0, h0 + H)
        attn = lax.dot_general(qe[hs], ke[hs], NT, preferred_element_type=F32)
        intra.append(jnp.dot(jnp.where(keep, attn, 0.0).astype(BF), v[hs], preferred_element_type=F32))
    o = jnp.concatenate(intra, axis=0)

    st = st_scr[...]
    inter = []
    for c in range(n_chunks):
        sl = slice(c * C, (c + 1) * C)
        inter.append(lax.dot_general(qe[sl], st.astype(BF), NT, preferred_element_type=F32))
        upd = lax.dot_general(v[sl], kd[sl], TN, preferred_element_type=F32)
        st = st * jnp.exp(last_rows[c]) + upd
    st_scr[...] = st
    o = o + jnp.concatenate(inter, axis=0)

    o = o * lax.rsqrt(jnp.mean(o * o, axis=-1, keepdims=True) + EPS) * ng_ref[...]
    r = r_ref[0].astype(F32)
    o_ref[0] = (o * (r * jax.nn.sigmoid(r))).astype(BF)


def _gla(proj3, small3, w2p, b2, ng, blk_tokens=512):
    B, S, _ = proj3.shape
    nblk = S // blk_tokens
    kern = functools.partial(_gla_kernel, n_chunks=blk_tokens // GLA_CHUNK)
    return pl.pallas_call(
        kern,
        grid=(B, GLA_HEADS, nblk),
        in_specs=[
            pl.BlockSpec((1, blk_tokens, GLA_DK), lambda b, h, i: (b, i, OFF_GQ // GLA_DK + h)),
            pl.BlockSpec((1, blk_tokens, GLA_DK), lambda b, h, i: (b, i, OFF_GK // GLA_DK + h)),
            pl.BlockSpec((1, blk_tokens, GLA_DV), lambda b, h, i: (b, i, OFF_GV // GLA_DV + h)),
            pl.BlockSpec((1, blk_tokens, GLA_DV), lambda b, h, i: (b, i, OFF_GR // GLA_DV + h)),
            pl.BlockSpec((1, blk_tokens, LANES), lambda b, h, i: (b, i, 0)),
            pl.BlockSpec((LANES, GLA_DK), lambda b, h, i: (0, h)),
            pl.BlockSpec((1, GLA_DK), lambda b, h, i: (0, h)),
            pl.BlockSpec((1, GLA_DV), lambda b, h, i: (0, 0)),
        ],
        out_specs=pl.BlockSpec((1, blk_tokens, GLA_DV), lambda b, h, i: (b, i, h)),
        out_shape=jax.ShapeDtypeStruct((B, S, GLA_HEADS * GLA_DV), BF),
        scratch_shapes=[pltpu.VMEM((GLA_DV, GLA_DK), F32)],
        compiler_params=_params(("parallel", "parallel", "arbitrary")),
    )(proj3, proj3, proj3, proj3, small3, w2p, b2, ng)


def _compress_kernel(xk_ref, xv_ref, pek_ref, pev_ref, w1k_ref, w1v_ref, w2k_ref, w2v_ref, kc_ref, vc_ref):
    n_rows = xk_ref.shape[1]
    half = CMP_STRIDE * LANES
    lane = lax.broadcasted_iota(jnp.int32, (n_rows, LANES), 1)
    row = lax.broadcasted_iota(jnp.int32, (n_rows, LANES), 0)
    end_c = CMP_STRIDE * row + (CMP_LEN - 1)
    c_k = jnp.where(lane == NSA_DH, end_c >> 6, jnp.where(lane == NSA_DH + 1, end_c & 63, 0)).astype(F32)
    c_v = jnp.where(lane == NSA_DH, 1.0, 0.0).astype(F32)

    def branch(x_ref, pe_ref, w1_ref, w2_ref, const, o_ref):
        x = x_ref[0].astype(F32)
        xa = (x + pe_ref[0:1, :]).astype(BF)
        xb = (x + pe_ref[1:2, :]).astype(BF)
        a = jnp.dot(xa, w1_ref[0:half, :], preferred_element_type=F32)
        b = jnp.dot(xb, w1_ref[half:2 * half, :], preferred_element_type=F32)
        pre = a + pltpu.roll(b, n_rows - 1, 0)
        hid = (pre * jax.nn.sigmoid(pre)).astype(BF)
        for g in range(NSA_GROUPS):
            o_ref[0, g] = (jnp.dot(hid, w2_ref[g], preferred_element_type=F32) + const).astype(BF)

    branch(xk_ref, pek_ref, w1k_ref, w2k_ref, c_k, kc_ref)
    branch(xv_ref, pev_ref, w1v_ref, w2v_ref, c_v, vc_ref)


def _nsa_compress(xk, xv, pek, pev, w1k, w1v, w2k, w2v):
    B, n_rows, width = xk.shape
    full = lambda shape: _resident(shape, lambda b: (0,) * len(shape))
    out = jax.ShapeDtypeStruct((B, NSA_GROUPS, n_rows, LANES), BF)
    return pl.pallas_call(
        _compress_kernel,
        grid=(B,),
        in_specs=[
            pl.BlockSpec((1, n_rows, width), lambda b: (b, 0, 0)),
            pl.BlockSpec((1, n_rows, width), lambda b: (b, 0, 0)),
            full(pek.shape), full(pev.shape), full(w1k.shape), full(w1v.shape),
            full(w2k.shape), full(w2v.shape),
        ],
        out_specs=[pl.BlockSpec((1, NSA_GROUPS, n_rows, LANES), lambda b: (b, 0, 0, 0))] * 2,
        out_shape=[out, out],
        compiler_params=_params(("parallel",)),
    )(xk, xv, pek, pev, w1k, w1v, w2k, w2v)


def _nsa_kernel(q_ref, ks_ref, vs_ref, kw_ref, vw_ref, kc_ref, vc_ref, gt_ref, sl_ref, ov_ref,
                o_ref, ksa, kwa, vst, vwt, vct, acc_scr, act_ref, *, seq):
    qi = pl.program_id(2)
    n_slc = seq // SLC_LEN
    R = NSA_REP
    KT = SLC_TILE
    WK = WINDOW + QB
    n_cmp = kc_ref.shape[2]

    def t_bf(x):
        return x.astype(F32).T.astype(BF)

    @pl.when(qi == 0)
    def _():
        lane = lax.broadcasted_iota(jnp.int32, (seq, LANES), 1)
        pos = lax.broadcasted_iota(jnp.int32, (seq, LANES), 0)
        blk = pos >> 6
        off = pos & 63
        c_s = jnp.where(lane == LANES - 1, off,
                        jnp.where((lane >= NSA_DH) & (lane - (NSA_DH - 1) == blk), 1, 0))
        ksa[...] = ks_ref[0] + c_s.astype(BF)
        c_w = jnp.where(lane == NSA_DH, blk, jnp.where(lane == NSA_DH + 1, off, 0))
        kwa[0:WINDOW, :] = jnp.zeros((WINDOW, LANES), BF)
        kwa[WINDOW:WINDOW + seq, :] = kw_ref[0] + c_w.astype(BF)
        ones = jnp.where(lax.broadcasted_iota(jnp.int32, (LANES, LANES), 1) == NSA_DH, 1, 0).astype(BF)
        n_pad = WINDOW // LANES
        for c in range(n_pad):
            vwt[c] = jnp.zeros((LANES, LANES), BF)

        def fill(c, carry):
            rows = pl.ds(pl.multiple_of(c * LANES, LANES), LANES)
            vst[c] = t_bf(vs_ref[0, rows, :] + ones)
            vwt[c + n_pad] = t_bf(vw_ref[0, rows, :] + ones)
            return carry

        lax.fori_loop(0, seq // LANES, fill, 0)
        for c in range(n_cmp // LANES):
            vct[:, c * LANES:(c + 1) * LANES] = t_bf(vc_ref[0, 0, c * LANES:(c + 1) * LANES, :])

    q0 = qi * QB
    rowi = lax.broadcasted_iota(jnp.int32, (LANES, QB), 0)
    slopes = [sl_ref[0, r:r + 1, :] for r in range(R)]
    q_t = [q_ref[0, :, r * LANES:(r + 1) * LANES].astype(F32).T for r in range(R)]

    qw = jnp.concatenate(
        [(q_t[r] + jnp.where(rowi == NSA_DH, slopes[r] * 64.0, jnp.where(rowi == NSA_DH + 1, slopes[r], 0.0))
          ).astype(BF) for r in range(R)], axis=1)

    def tile4(x):
        return jnp.concatenate([x] * R, axis=1)

    s_c = jnp.dot(kc_ref[0, 0], qw, preferred_element_type=F32)
    e_c = CMP_STRIDE * lax.broadcasted_iota(jnp.int32, (n_cmp, QB), 0) + (CMP_LEN - 1)
    t_c = q0 + lax.broadcasted_iota(jnp.int32, (n_cmp, QB), 1)
    mask_c = tile4(e_c <= t_c)
    s_c = jnp.where(mask_c, s_c, NEG)
    m_c = jnp.max(s_c, axis=0, keepdims=True)
    p_c = jnp.where(mask_c, jnp.exp(s_c - m_c), 0.0)
    l_c = jnp.sum(p_c, axis=0, keepdims=True)
    p_c = p_c * jnp.where(l_c > 0.0, 1.0 / l_c, 0.0)
    o_c = jnp.dot(vct[...], p_c.astype(BF), preferred_element_type=F32)

    psum = p_c[:, 0:QB]
    for r in range(1, R):
        psum = psum + p_c[:, r * QB:(r + 1) * QB]
    ovt = ov_ref[...]
    p_hi = psum.astype(BF)
    rem = psum - p_hi.astype(F32)
    p_mid = rem.astype(BF)
    p_lo = (rem - p_mid.astype(F32)).astype(BF)
    imp = (jnp.dot(ovt, p_hi, preferred_element_type=F32)
           + jnp.dot(ovt, p_mid, preferred_element_type=F32)
           + jnp.dot(ovt, p_lo, preferred_element_type=F32))

    NR = ovt.shape[0]
    SUB = 8
    jblk = lax.broadcasted_iota(jnp.int32, (NR, QB), 0)
    t_q = q0 + lax.broadcasted_iota(jnp.int32, (NR, QB), 1)
    cur = t_q >> 6
    forced = (jblk == 0) | (jblk == cur) | (jblk == cur - 1)
    score = jnp.where(jblk > cur, NEG, jnp.where(forced, -NEG, imp))
    n_slab = -(-n_slc // SUB)
    slabs = [score[a * SUB:(a + 1) * SUB, :] for a in range(n_slab)]
    cnts = [jnp.zeros((SUB, QB), F32) for _ in range(n_slab)]
    isub = lax.broadcasted_iota(jnp.int32, (SUB, QB), 0)
    for jp in range(n_slc):
        rowv = jnp.broadcast_to(score[jp:jp + 1, :], (SUB, QB))
        for a in range(n_slab):
            if a < jp // SUB:
                beats = jnp.where(rowv > slabs[a], 1.0, 0.0)
            elif a > jp // SUB:
                beats = jnp.where(rowv >= slabs[a], 1.0, 0.0)
            else:
                beats = jnp.where(isub > jp % SUB,
                                  jnp.where(rowv >= slabs[a], 1.0, 0.0), jnp.where(rowv > slabs[a], 1.0, 0.0))
            cnts[a] = cnts[a] + beats
    cnt = jnp.concatenate(cnts + [jnp.full((NR - n_slab * SUB, QB), float(NR), F32)] * (NR > n_slab * SUB), axis=0)
    sel = (cnt < float(N_SEL)) & (jblk <= cur) & (jblk < n_slc)
    a_nat = jnp.where(jblk == 0, 1.0, jnp.where(sel, (SLC_LEN * jblk).astype(F32), NEG))
    a_nat = jnp.where(jblk < n_slc, a_nat, 0.0)
    a_t = pltpu.roll(a_nat, NR - 1, 0)
    a_pad = jnp.concatenate([jnp.zeros((LANES - NR, QB), F32), a_t], axis=0)

    any_q = jnp.max(jnp.where(sel, 1.0, 0.0), axis=1, keepdims=True)
    jcol = lax.broadcasted_iota(jnp.int32, (NR, 1), 0)
    bits = jnp.where(any_q > 0.0, lax.shift_left(jnp.int32(1), jcol & 31), 0)
    word0 = jnp.sum(jnp.where(jcol < 32, bits, 0))
    word1 = jnp.sum(jnp.where(jcol >= 32, bits, 0))

    qs = jnp.concatenate([(q_t[r] + a_pad * slopes[r]).astype(BF) for r in range(R)], axis=1)

    kw = kwa[pl.ds(pl.multiple_of(q0, QB), WK), :]
    s_w = jnp.dot(kw, qw, preferred_element_type=F32)
    c_w = lax.broadcasted_iota(jnp.int32, (WK, QB), 0)
    d_w = c_w - lax.broadcasted_iota(jnp.int32, (WK, QB), 1)
    mask_w = (d_w > 0) & (d_w <= WINDOW) & (c_w >= WINDOW - q0)
    s_w = s_w + tile4(jnp.where(mask_w, 0.0, NEG))
    m_w = jnp.max(s_w, axis=0, keepdims=True)
    p_w = jnp.exp(s_w - m_w).astype(BF)
    vw_t = jnp.concatenate([vwt[qi + c] for c in range(WK // LANES)], axis=1)
    acc_w = jnp.dot(vw_t, p_w, preferred_element_type=F32)

    tiles_per = KT // LANES
    blocks_per = KT // SLC_LEN
    n_full = q0 // KT

    def scan(ti, n):
        word = jnp.where(ti < 32 // blocks_per, word0, word1)
        hit = (lax.shift_right_logical(word, (ti * blocks_per) & 31) & ((1 << blocks_per) - 1)) != 0
        act_ref[n] = ti
        return n + jnp.where(hit, 1, 0)

    n_act = lax.fori_loop(0, n_full, scan, 0)
    act_ref[n_act] = n_full

    def scores(ti):
        k0 = pl.multiple_of(ti * KT, KT)
        return jnp.dot(ksa[pl.ds(k0, KT), :], qs, preferred_element_type=F32)

    def absorb(s, ti, m_prev, diag):
        if diag:
            p_d = ti * KT + lax.broadcasted_iota(jnp.int32, (KT, QB), 0)
            t_d = q0 + lax.broadcasted_iota(jnp.int32, (KT, QB), 1)
            s = s + tile4(jnp.where(p_d <= t_d, 0.0, NEG))
        m_new = jnp.maximum(m_prev, jnp.max(s, axis=0, keepdims=True))
        alpha = jnp.exp(m_prev - m_new)
        p = jnp.exp(s - m_new).astype(BF)
        v_t = jnp.concatenate([vst[ti * tiles_per + c] for c in range(tiles_per)], axis=1)
        acc_scr[...] = acc_scr[...] * alpha + jnp.dot(v_t, p, preferred_element_type=F32)
        return m_new

    acc_scr[...] = jnp.zeros(acc_scr.shape, F32)

    def body(i, carry):
        s, m_prev = carry
        s_next = scores(act_ref[i + 1])
        return s_next, absorb(s, act_ref[i], m_prev, False)

    s_last, m_last = lax.fori_loop(0, n_act, body, (scores(act_ref[0]), jnp.full((1, R * QB), NEG, F32)))
    absorb(s_last, n_full, m_last, True)
    acc_s = acc_scr[...]


    sg_t = jax.nn.sigmoid(gt_ref[0]).T
    for r in range(R):
        cs = slice(r * QB, (r + 1) * QB)
        a_s = acc_s[:, cs]
        a_w = acc_w[:, cs]
        g_c = sg_t[3 * r:3 * r + 1, :]
        g_s = sg_t[3 * r + 1:3 * r + 2, :] / a_s[NSA_DH:NSA_DH + 1, :]
        g_w = sg_t[3 * r + 2:3 * r + 3, :] / a_w[NSA_DH:NSA_DH + 1, :]
        out_t = g_c * o_c[:, cs] + g_s * a_s + g_w * a_w
        o_ref[0, :, r * LANES:(r + 1) * LANES] = out_t.T.astype(BF)


def _nsa_attend(proj3, small3, kc, vc, slope_tab, ovt):
    B, S, _ = proj3.shape
    G, R = NSA_GROUPS, NSA_REP
    n_cmp = kc.shape[2]
    kern = functools.partial(_nsa_kernel, seq=S)
    kv_spec = lambda off: pl.BlockSpec((1, S, LANES), lambda b, g, i: (b, 0, off // LANES + g))
    return pl.pallas_call(
        kern,
        grid=(B, G, S // QB),
        in_specs=[
            pl.BlockSpec((1, QB, R * LANES), lambda b, g, i: (b, i, OFF_NQ // (R * LANES) + g)),
            kv_spec(OFF_KS), kv_spec(OFF_VS), kv_spec(OFF_KW), kv_spec(OFF_VW),
            pl.BlockSpec((1, 1, n_cmp, LANES), lambda b, g, i: (b, g, 0, 0)),
            pl.BlockSpec((1, 1, n_cmp, LANES), lambda b, g, i: (b, g, 0, 0)),
            pl.BlockSpec((1, QB, LANES), lambda b, g, i: (b, i, 1 + g)),
            pl.BlockSpec((1, 8, LANES), lambda b, g, i: (g, 0, 0)),
            _resident(ovt.shape, lambda b, g, i: (0, 0)),
        ],
        out_specs=pl.BlockSpec((1, QB, R * LANES), lambda b, g, i: (b, i, g)),
        out_shape=jax.ShapeDtypeStruct((B, S, G * R * LANES), BF),
        scratch_shapes=[
            pltpu.VMEM((S, LANES), BF),
            pltpu.VMEM((S + WINDOW, LANES), BF),
            pltpu.VMEM((S // LANES, LANES, LANES), BF),
            pltpu.VMEM(((S + WINDOW) // LANES, LANES, LANES), BF),
            pltpu.VMEM((LANES, n_cmp), BF),
            pltpu.VMEM((LANES, R * QB), F32),
            pltpu.SMEM((S // SLC_TILE + 1,), jnp.int32),
        ],
        compiler_params=_params(("parallel", "parallel", "arbitrary")),
    )(proj3, proj3, proj3, proj3, proj3, kc, vc, small3, slope_tab, ovt)


def _merge_kernel(og_ref, on_ref, mg_ref, mn_ref, x_ref, wg_ref, wn_ref, wo_ref, g_ref, o_ref):
    a = jnp.dot(og_ref[...], wg_ref[...], preferred_element_type=F32)
    b = jnp.dot(on_ref[...], wn_ref[...], preferred_element_type=F32)
    mixed = jax.nn.sigmoid(mg_ref[...].astype(F32)) * a + jax.nn.sigmoid(mn_ref[...].astype(F32)) * b
    y = jnp.dot(mixed.astype(BF), wo_ref[...], preferred_element_type=F32)
    y = y * lax.rsqrt(jnp.mean(y * y, axis=-1, keepdims=True) + EPS) * g_ref[...]
    o_ref[...] = x_ref[...] + y


def _merge_out(o_gla2, o_nsa2, proj2, x2, wg, wn, wo, g, tm=512):
    n_tok = x2.shape[0]
    D = D_MODEL
    return pl.pallas_call(
        _merge_kernel,
        grid=(n_tok // tm,),
        in_specs=[
            pl.BlockSpec((tm, D), lambda i: (i, 0)),
            pl.BlockSpec((tm, D), lambda i: (i, 0)),
            pl.BlockSpec((tm, D), lambda i: (i, OFF_MG // D)),
            pl.BlockSpec((tm, D), lambda i: (i, OFF_MN // D)),
            pl.BlockSpec((tm, D), lambda i: (i, 0)),
            _resident(wg.shape, lambda i: (0, 0)),
            _resident(wn.shape, lambda i: (0, 0)),
            _resident(wo.shape, lambda i: (0, 0)),
            _resident((1, D), lambda i: (0, 0)),
        ],
        out_specs=pl.BlockSpec((tm, D), lambda i: (i, 0)),
        out_shape=jax.ShapeDtypeStruct((n_tok, D), F32),
        compiler_params=_params(("parallel",)),
    )(o_gla2, o_nsa2, proj2, proj2, x2, wg, wn, wo, g)


def _ffn_kernel(x_ref, gpre_ref, wg_ref, wu_ref, wd_ref, gpost_ref, o_ref, acc_scr, *, chunk):
    x = x_ref[...]
    h = (x * lax.rsqrt(jnp.mean(x * x, axis=-1, keepdims=True) + EPS) * gpre_ref[...]).astype(BF)
    d_ff = wg_ref.shape[1]
    for n, c0 in enumerate(range(0, d_ff, chunk)):
        c1 = min(c0 + chunk, d_ff)
        a = jnp.dot(h, wg_ref[:, c0:c1], preferred_element_type=F32)
        u = jnp.dot(h, wu_ref[:, c0:c1], preferred_element_type=F32)
        t = (a * jax.nn.sigmoid(a) * u).astype(BF)
        part = jnp.dot(t, wd_ref[c0:c1, :], preferred_element_type=F32)
        if n == 0:
            acc_scr[...] = part
        else:
            acc_scr[...] += part
    f = acc_scr[...]
    o_ref[...] = x + f * lax.rsqrt(jnp.mean(f * f, axis=-1, keepdims=True) + EPS) * gpost_ref[...]


def _ffn(x2, gpre, wg, wu, wd, gpost, tm=512, chunk=512):
    n_tok = x2.shape[0]
    D = D_MODEL
    kern = functools.partial(_ffn_kernel, chunk=chunk)
    return pl.pallas_call(
        kern,
        grid=(n_tok // tm,),
        in_specs=[
            pl.BlockSpec((tm, D), lambda i: (i, 0)),
            _resident((1, D), lambda i: (0, 0)),
            _resident(wg.shape, lambda i: (0, 0)),
            _resident(wu.shape, lambda i: (0, 0)),
            _resident(wd.shape, lambda i: (0, 0)),
            _resident((1, D), lambda i: (0, 0)),
        ],
        out_specs=pl.BlockSpec((tm, D), lambda i: (i, 0)),
        out_shape=jax.ShapeDtypeStruct((n_tok, D), F32),
        scratch_shapes=[pltpu.VMEM((tm, D), F32)],
        compiler_params=_params(("parallel",)),
    )(x2, gpre, wg, wu, wd, gpost)


def _pad_heads(w, n_heads, scale=1.0):
    d = w.shape[0]
    w = (w * scale).reshape(d, n_heads, NSA_DH)
    return jnp.pad(w, ((0, 0), (0, 0), (0, LANES - NSA_DH))).reshape(d, n_heads * LANES)


def _prep_in_weights(w_in):
    splits = np.cumsum([512, 512, 1024, 1024, GLA_RANK, 512, 128, 128, 128, 128, 128, 128, 24, 1024])
    (g_q, g_k, g_v, g_r, g_a, n_q, n_kc, n_vc, n_ks, n_vs, n_kw, n_vw, n_gate, m_g, m_n) = jnp.split(
        w_in, [int(s) for s in splits], axis=1)
    w_main = jnp.concatenate([
        g_q, g_k, g_v, g_r,
        _pad_heads(n_q, NSA_HEADS, NSA_DH ** -0.5),
        m_g, m_n,
        n_kc, n_vc,
        _pad_heads(n_ks, NSA_GROUPS), _pad_heads(n_vs, NSA_GROUPS),
        _pad_heads(n_kw, NSA_GROUPS), _pad_heads(n_vw, NSA_GROUPS)], axis=1).astype(BF)
    d = w_in.shape[0]
    per_g = NSA_REP * 3
    gates = jnp.pad(n_gate.reshape(d, NSA_GROUPS, per_g), ((0, 0), (0, 0), (0, LANES - per_g)))
    g_a3 = jnp.pad(jnp.concatenate([g_a, g_a, g_a], axis=1), ((0, 0), (0, LANES - 3 * GLA_RANK)))
    w_small = jnp.concatenate([g_a3, gates.reshape(d, NSA_GROUPS * LANES)], axis=1).astype(BF)
    return w_main, w_small


def _prep_compress(pe, w1, w2):
    eye = jnp.eye(NSA_GROUPS, dtype=F32)
    w1r = w1.reshape(CMP_LEN, NSA_DH, NSA_DH)
    w1e = jnp.einsum('lde,gh->lgdhe', w1r, eye).reshape(CMP_LEN * NSA_GROUPS * NSA_DH, NSA_GROUPS * NSA_DH)
    pe_e = jnp.broadcast_to(pe[:, None, :], (CMP_LEN, NSA_GROUPS, NSA_DH)).reshape(2, CMP_STRIDE * LANES)
    pe_e = jnp.pad(pe_e, ((0, 6), (0, 0)))
    w2e = jnp.stack([
        jnp.pad(jnp.pad(w2, ((g * NSA_DH, (NSA_GROUPS - 1 - g) * NSA_DH), (0, 0))), ((0, 0), (0, LANES - NSA_DH)))
        for g in range(NSA_GROUPS)])
    return pe_e.astype(F32), w1e.astype(BF), w2e.astype(BF)


def _overlap_table(seq):
    n_cmp = (seq - CMP_LEN) // CMP_STRIDE + 1
    n_slc = seq // SLC_LEN
    sc = CMP_STRIDE * np.arange(n_cmp)
    ss = SLC_LEN * np.arange(n_slc)
    ov = np.clip(np.minimum(sc[:, None] + CMP_LEN, ss[None, :] + SLC_LEN)
                 - np.maximum(sc[:, None], ss[None, :]), 0, None).astype(np.float32) / CMP_LEN
    ovt = np.zeros((NSA_DH, n_cmp + 1), np.float32)
    ovt[:n_slc, :n_cmp] = ov.T
    return jnp.asarray(ovt, dtype=BF)


def kernel(x, norm_mix_pre, norm_mix_post, norm_ffn_pre, norm_ffn_post, w_in, gla_w_alpha2, gla_b_alpha, gla_norm_g, nsa_cmp_pe_k, nsa_cmp_w1_k, nsa_cmp_w2_k, nsa_cmp_pe_v, nsa_cmp_w1_v, nsa_cmp_w2_v, w_proj_gla, w_proj_nsa, w_out, w_ffn_gate, w_ffn_up, w_ffn_down):
    B, S, D = x.shape
    depth = w_in.shape[0]
    n_tok = B * S
    h_idx = jnp.arange(NSA_HEADS, dtype=F32)
    slopes = jnp.exp2(-8.0 * (h_idx + 1.0) / NSA_HEADS).reshape(NSA_GROUPS, NSA_REP, 1)
    slope_tab = jnp.broadcast_to(jnp.pad(slopes, ((0, 0), (0, 8 - NSA_REP), (0, 0))), (NSA_GROUPS, 8, LANES))
    ovt = _overlap_table(S)
    x2 = x.reshape(n_tok, D)
    for l in range(depth):
        w_main, w_small = _prep_in_weights(w_in[l])
        proj2, small2 = _in_proj(x2, norm_mix_pre[l][None, :], w_main, w_small)
        proj3 = proj2.reshape(B, S, N_MAIN)
        small3 = small2.reshape(B, S, N_SMALL)

        w2_hi = gla_w_alpha2[l].astype(BF)
        w2_lo = (gla_w_alpha2[l] - w2_hi.astype(F32)).astype(BF)
        w2p = jnp.pad(jnp.concatenate([w2_hi, w2_hi, w2_lo], axis=0), ((0, LANES - 3 * GLA_RANK), (0, 0)))
        o_gla = _gla(proj3, small3, w2p, gla_b_alpha[l][None, :], gla_norm_g[l][None, :])

        xk = proj3[:, :, OFF_KC:OFF_KC + LANES].reshape(B, S // CMP_STRIDE, CMP_STRIDE * LANES)
        xv = proj3[:, :, OFF_VC:OFF_VC + LANES].reshape(B, S // CMP_STRIDE, CMP_STRIDE * LANES)
        pek, w1k, w2k = _prep_compress(nsa_cmp_pe_k[l], nsa_cmp_w1_k[l], nsa_cmp_w2_k[l])
        pev, w1v, w2v = _prep_compress(nsa_cmp_pe_v[l], nsa_cmp_w1_v[l], nsa_cmp_w2_v[l])
        kc, vc = _nsa_compress(xk, xv, pek, pev, w1k, w1v, w2k, w2v)
        o_nsa = _nsa_attend(proj3, small3, kc, vc, slope_tab, ovt)

        wn = jnp.pad(w_proj_nsa[l].reshape(NSA_HEADS, NSA_DH, D), ((0, 0), (0, LANES - NSA_DH), (0, 0)))
        x2 = _merge_out(o_gla.reshape(n_tok, -1), o_nsa.reshape(n_tok, -1), proj2, x2,
                        w_proj_gla[l].astype(BF), wn.reshape(NSA_HEADS * LANES, D).astype(BF),
                        w_out[l].astype(BF), norm_mix_post[l][None, :])
        x2 = _ffn(x2, norm_ffn_pre[l][None, :], w_ffn_gate[l].astype(BF), w_ffn_up[l].astype(BF),
                  w_ffn_down[l].astype(BF), norm_ffn_post[l][None, :])
    return x2.reshape(B, S, D)
```

```python
import functools

import numpy as np
import jax
import jax.numpy as jnp
from jax import lax
from jax.experimental import pallas as pl
from jax.experimental.pallas import tpu as pltpu

D_MODEL = 1024
GLA_HEADS = 4
GLA_DK = 128
GLA_DV = 256
GLA_RANK = 16
GLA_TAU = 16.0
GLA_CHUNK = 64
NSA_HEADS = 8
NSA_GROUPS = 2
NSA_REP = 4
NSA_DH = 64
CMP_LEN = 32
CMP_STRIDE = 16
SLC_LEN = 64
N_SEL = 16
WINDOW = 512
QB = 128
SLC_TILE = 256
D_FF = 2816
EPS = 1e-6
NEG = -1e30

LANES = 128
VMEM_LIMIT = 56 * 1024 * 1024
BF = jnp.bfloat16
F32 = jnp.float32

OFF_GQ = 0
OFF_GK = 512
OFF_GV = 1024
OFF_GR = 2048
OFF_NQ = 3072
OFF_MG = 4096
OFF_MN = 5120
OFF_KC = 6144
OFF_VC = 6272
OFF_KS = 6400
OFF_VS = 6656
OFF_KW = 6912
OFF_VW = 7168
N_MAIN = 7424
N_SMALL = 384

NT = (((1,), (1,)), ((), ()))
TN = (((0,), (0,)), ((), ()))


def _resident(shape, index_map):
    return pl.BlockSpec(shape, index_map, pipeline_mode=pl.Buffered(1))


def _params(sem):
    return pltpu.CompilerParams(dimension_semantics=sem, vmem_limit_bytes=VMEM_LIMIT)


def _in_proj_kernel(x_ref, g_ref, wm_ref, ws_ref, om_ref, os_ref, h_scr):
    j = pl.program_id(1)

    @pl.when(j == 0)
    def _():
        x = x_ref[...]
        y = x * lax.rsqrt(jnp.mean(x * x, axis=-1, keepdims=True) + EPS) * g_ref[...]
        h_scr[...] = y.astype(BF)
        os_ref[...] = jnp.dot(h_scr[...], ws_ref[...], preferred_element_type=F32)

    tn = om_ref.shape[1]
    step = 512
    for c0 in range(0, tn, step):
        c1 = min(c0 + step, tn)
        om_ref[:, c0:c1] = jnp.dot(h_scr[...], wm_ref[:, c0:c1], preferred_element_type=F32).astype(BF)


def _in_proj(x2, g, w_main, w_small, tm=512, n_split=2):
    n_tok = x2.shape[0]
    tn = N_MAIN // n_split
    return pl.pallas_call(
        _in_proj_kernel,
        grid=(n_tok // tm, n_split),
        in_specs=[
            pl.BlockSpec((tm, D_MODEL), lambda i, j: (i, 0)),
            _resident((1, D_MODEL), lambda i, j: (0, 0)),
            pl.BlockSpec((D_MODEL, tn), lambda i, j: (0, j)),
            _resident((D_MODEL, N_SMALL), lambda i, j: (0, 0)),
        ],
        out_specs=[
            pl.BlockSpec((tm, tn), lambda i, j: (i, j)),
            pl.BlockSpec((tm, N_SMALL), lambda i, j: (i, 0)),
        ],
        out_shape=[
            jax.ShapeDtypeStruct((n_tok, N_MAIN), BF),
            jax.ShapeDtypeStruct((n_tok, N_SMALL), F32),
        ],
        scratch_shapes=[pltpu.VMEM((tm, D_MODEL), BF)],
        compiler_params=_params(("parallel", "arbitrary")),
    )(x2, g, w_main, w_small)


def _gla_kernel(q_ref, k_ref, v_ref, r_ref, a_ref, w2_ref, b2_ref, ng_ref, o_ref, st_scr, *, n_chunks):
    blk = pl.program_id(2)

    @pl.when(blk == 0)
    def _():
        st_scr[...] = jnp.zeros_like(st_scr)

    C = GLA_CHUNK
    T = n_chunks * C

    def split3(x):
        hi = x.astype(BF)
        rem = x - hi.astype(F32)
        mid = rem.astype(BF)
        return hi, mid, (rem - mid.astype(F32)).astype(BF)

    a = a_ref[0]
    a_hi = a.astype(BF)
    a_lo = (a - a_hi.astype(F32)).astype(BF)
    lane = lax.broadcasted_iota(jnp.int32, (T, LANES), 1)
    in_lo = (lane >= GLA_RANK) & (lane < 2 * GLA_RANK)
    z = jnp.dot(jnp.where(in_lo, a_lo, a_hi), w2_ref[...], preferred_element_type=F32) + b2_ref[...]
    log_a = (jnp.minimum(z, 0.0) - jnp.log1p(jnp.exp(-jnp.abs(z)))) * (1.0 / GLA_TAU)

    x_wide = jnp.concatenate([log_a[c * C:(c + 1) * C] for c in range(n_chunks)], axis=1)
    x3 = jnp.concatenate(split3(x_wide), axis=0)
    r3 = lax.broadcasted_iota(jnp.int32, (C, 3 * C), 0)
    c3 = lax.broadcasted_iota(jnp.int32, (C, 3 * C), 1) & (C - 1)
    tri3 = jnp.where(c3 <= r3, 1.0, 0.0).astype(BF)
    b_wide = jnp.dot(tri3, x3, preferred_element_type=F32)
    bcum = jnp.concatenate([b_wide[:, c * LANES:(c + 1) * LANES] for c in range(n_chunks)], axis=0)
    last_rows = [b_wide[C - 1:C, c * LANES:(c + 1) * LANES] for c in range(n_chunks)]
    b_last = jnp.concatenate([jnp.broadcast_to(lr, (C, LANES)) for lr in last_rows], axis=0)

    q = q_ref[0].astype(F32)
    k = k_ref[0].astype(F32)
    v = v_ref[0]
    qe = (q * ((GLA_DK ** -0.5) * jnp.exp(bcum))).astype(BF)
    ke = (k * jnp.exp(-bcum)).astype(BF)
    kd = (k * jnp.exp(b_last - bcum)).astype(BF)

    H = min(T, 4 * C)
    row = lax.broadcasted_iota(jnp.int32, (H, H), 0)
    col = lax.broadcasted_iota(jnp.int32, (H, H), 1)
    keep = (col <= row) & ((col >> 6) == (row >> 6))
    intra = []
    for h0 in range(0, T, H):
        hs = slice(h0, h0 + H)
        attn = lax.dot_general(qe[hs], ke[hs], NT, preferred_element_type=F32)
        intra.append(jnp.dot(jnp.where(keep, attn, 0.0).astype(BF), v[hs], preferred_element_type=F32))
    o = jnp.concatenate(intra, axis=0)

    st = st_scr[...]
    inter = []
    for c in range(n_chunks):
        sl = slice(c * C, (c + 1) * C)
        inter.append(lax.dot_general(qe[sl], st.astype(BF), NT, preferred_element_type=F32))
        upd = lax.dot_general(v[sl], kd[sl], TN, preferred_element_type=F32)
        st = st * jnp.exp(last_rows[c]) + upd
    st_scr[...] = st
    o = o + jnp.concatenate(inter, axis=0)

    o = o * lax.rsqrt(jnp.mean(o * o, axis=-1, keepdims=True) + EPS) * ng_ref[...]
    r = r_ref[0].astype(F32)
    o_ref[0] = (o * (r * jax.nn.sigmoid(r))).astype(BF)


def _gla(proj3, small3, w2p, b2, ng, blk_tokens=512):
    B, S, _ = proj3.shape
    nblk = S // blk_tokens
    kern = functools.partial(_gla_kernel, n_chunks=blk_tokens // GLA_CHUNK)
    return pl.pallas_call(
        kern,
        grid=(B, GLA_HEADS, nblk),
        in_specs=[
            pl.BlockSpec((1, blk_tokens, GLA_DK), lambda b, h, i: (b, i, OFF_GQ // GLA_DK + h)),
            pl.BlockSpec((1, blk_tokens, GLA_DK), lambda b, h, i: (b, i, OFF_GK // GLA_DK + h)),
            pl.BlockSpec((1, blk_tokens, GLA_DV), lambda b, h, i: (b, i, OFF_GV // GLA_DV + h)),
            pl.BlockSpec((1, blk_tokens, GLA_DV), lambda b, h, i: (b, i, OFF_GR // GLA_DV + h)),
            pl.BlockSpec((1, blk_tokens, LANES), lambda b, h, i: (b, i, 0)),
            pl.BlockSpec((LANES, GLA_DK), lambda b, h, i: (0, h)),
            pl.BlockSpec((1, GLA_DK), lambda b, h, i: (0, h)),
            pl.BlockSpec((1, GLA_DV), lambda b, h, i: (0, 0)),
        ],
        out_specs=pl.BlockSpec((1, blk_tokens, GLA_DV), lambda b, h, i: (b, i, h)),
        out_shape=jax.ShapeDtypeStruct((B, S, GLA_HEADS * GLA_DV), BF),
        scratch_shapes=[pltpu.VMEM((GLA_DV, GLA_DK), F32)],
        compiler_params=_params(("parallel", "parallel", "arbitrary")),
    )(proj3, proj3, proj3, proj3, small3, w2p, b2, ng)


def _compress_kernel(xk_ref, xv_ref, pek_ref, pev_ref, w1k_ref, w1v_ref, w2k_ref, w2v_ref, kc_ref, vc_ref):
    n_rows = xk_ref.shape[1]
    half = CMP_STRIDE * LANES
    lane = lax.broadcasted_iota(jnp.int32, (n_rows, LANES), 1)
    row = lax.broadcasted_iota(jnp.int32, (n_rows, LANES), 0)
    end_c = CMP_STRIDE * row + (CMP_LEN - 1)
    c_k = jnp.where(lane == NSA_DH, end_c >> 6, jnp.where(lane == NSA_DH + 1, end_c & 63, 0)).astype(F32)
    c_v = jnp.where(lane == NSA_DH, 1.0, 0.0).astype(F32)

    def branch(x_ref, pe_ref, w1_ref, w2_ref, const, o_ref):
        x = x_ref[0].astype(F32)
        xa = (x + pe_ref[0:1, :]).astype(BF)
        xb = (x + pe_ref[1:2, :]).astype(BF)
        a = jnp.dot(xa, w1_ref[0:half, :], preferred_element_type=F32)
        b = jnp.dot(xb, w1_ref[half:2 * half, :], preferred_element_type=F32)
        pre = a + pltpu.roll(b, n_rows - 1, 0)
        hid = (pre * jax.nn.sigmoid(pre)).astype(BF)
        for g in range(NSA_GROUPS):
            o_ref[0, g] = (jnp.dot(hid, w2_ref[g], preferred_element_type=F32) + const).astype(BF)

    branch(xk_ref, pek_ref, w1k_ref, w2k_ref, c_k, kc_ref)
    branch(xv_ref, pev_ref, w1v_ref, w2v_ref, c_v, vc_ref)


def _nsa_compress(xk, xv, pek, pev, w1k, w1v, w2k, w2v):
    B, n_rows, width = xk.shape
    full = lambda shape: _resident(shape, lambda b: (0,) * len(shape))
    out = jax.ShapeDtypeStruct((B, NSA_GROUPS, n_rows, LANES), BF)
    return pl.pallas_call(
        _compress_kernel,
        grid=(B,),
        in_specs=[
            pl.BlockSpec((1, n_rows, width), lambda b: (b, 0, 0)),
            pl.BlockSpec((1, n_rows, width), lambda b: (b, 0, 0)),
            full(pek.shape), full(pev.shape), full(w1k.shape), full(w1v.shape),
            full(w2k.shape), full(w2v.shape),
        ],
        out_specs=[pl.BlockSpec((1, NSA_GROUPS, n_rows, LANES), lambda b: (b, 0, 0, 0))] * 2,
        out_shape=[out, out],
        compiler_params=_params(("parallel",)),
    )(xk, xv, pek, pev, w1k, w1v, w2k, w2v)


def _nsa_kernel(q_ref, ks_ref, vs_ref, kw_ref, vw_ref, kc_ref, vc_ref, gt_ref, sl_ref, ov_ref,
                o_ref, ksa, kwa, vst, vwt, vct, acc_scr, s_buf, p_buf, act_ref, *, seq):
    qi = pl.program_id(2)
    n_slc = seq // SLC_LEN
    R = NSA_REP
    KT = SLC_TILE
    WK = WINDOW + QB
    n_cmp = kc_ref.shape[2]

    def t_bf(x):
        return x.astype(F32).T.astype(BF)

    @pl.when(qi == 0)
    def _():
        lane = lax.broadcasted_iota(jnp.int32, (seq, LANES), 1)
        pos = lax.broadcasted_iota(jnp.int32, (seq, LANES), 0)
        blk = pos >> 6
        off = pos & 63
        c_s = jnp.where(lane == LANES - 1, off,
                        jnp.where((lane >= NSA_DH) & (lane - (NSA_DH - 1) == blk), 1, 0))
        ksa[0:seq, :] = ks_ref[0] + c_s.astype(BF)
        lane_d = lax.broadcasted_iota(jnp.int32, (KT, LANES), 1)
        ksa[seq:seq + KT, :] = jnp.where((lane_d >= NSA_DH) & (lane_d < LANES - 1), 1, 0).astype(BF)
        for c in range(KT // LANES):
            vst[seq // LANES + c] = jnp.zeros((LANES, LANES), BF)
        p_buf[...] = jnp.zeros(p_buf.shape, BF)
        c_w = jnp.where(lane == NSA_DH, blk, jnp.where(lane == NSA_DH + 1, off, 0))
        kwa[0:WINDOW, :] = jnp.zeros((WINDOW, LANES), BF)
        kwa[WINDOW:WINDOW + seq, :] = kw_ref[0] + c_w.astype(BF)
        ones = jnp.where(lax.broadcasted_iota(jnp.int32, (LANES, LANES), 1) == NSA_DH, 1, 0).astype(BF)
        n_pad = WINDOW // LANES
        for c in range(n_pad):
            vwt[c] = jnp.zeros((LANES, LANES), BF)

        def fill(c, carry):
            rows = pl.ds(pl.multiple_of(c * LANES, LANES), LANES)
            vst[c] = t_bf(vs_ref[0, rows, :] + ones)
            vwt[c + n_pad] = t_bf(vw_ref[0, rows, :] + ones)
            return carry

        lax.fori_loop(0, seq // LANES, fill, 0)
        for c in range(n_cmp // LANES):
            vct[:, c * LANES:(c + 1) * LANES] = t_bf(vc_ref[0, 0, c * LANES:(c + 1) * LANES, :])

    q0 = qi * QB
    rowi = lax.broadcasted_iota(jnp.int32, (LANES, QB), 0)
    slopes = [sl_ref[0, r:r + 1, :] for r in range(R)]
    q_t = [q_ref[0, :, r * LANES:(r + 1) * LANES].astype(F32).T for r in range(R)]

    qw = jnp.concatenate(
        [(q_t[r] + jnp.where(rowi == NSA_DH, slopes[r] * 64.0, jnp.where(rowi == NSA_DH + 1, slopes[r], 0.0))
          ).astype(BF) for r in range(R)], axis=1)

    def tile4(x):
        return jnp.concatenate([x] * R, axis=1)

    s_c = jnp.dot(kc_ref[0, 0], qw, preferred_element_type=F32)
    e_c = CMP_STRIDE * lax.broadcasted_iota(jnp.int32, (n_cmp, QB), 0) + (CMP_LEN - 1)
    t_c = q0 + lax.broadcasted_iota(jnp.int32, (n_cmp, QB), 1)
    mask_c = tile4(e_c <= t_c)
    s_c = jnp.where(mask_c, s_c, NEG)
    m_c = jnp.max(s_c, axis=0, keepdims=True)
    p_c = jnp.where(mask_c, jnp.exp(s_c - m_c), 0.0)
    l_c = jnp.sum(p_c, axis=0, keepdims=True)
    p_c = p_c * jnp.where(l_c > 0.0, 1.0 / l_c, 0.0)
    o_c = jnp.dot(vct[...], p_c.astype(BF), preferred_element_type=F32)

    psum = p_c[:, 0:QB]
    for r in range(1, R):
        psum = psum + p_c[:, r * QB:(r + 1) * QB]
    ovt = ov_ref[...]
    p_hi = psum.astype(BF)
    rem = psum - p_hi.astype(F32)
    p_mid = rem.astype(BF)
    p_lo = (rem - p_mid.astype(F32)).astype(BF)
    imp = (jnp.dot(ovt, p_hi, preferred_element_type=F32)
           + jnp.dot(ovt, p_mid, preferred_element_type=F32)
           + jnp.dot(ovt, p_lo, preferred_element_type=F32))

    NR = ovt.shape[0]
    SUB = 8
    jblk = lax.broadcasted_iota(jnp.int32, (NR, QB), 0)
    t_q = q0 + lax.broadcasted_iota(jnp.int32, (NR, QB), 1)
    cur = t_q >> 6
    forced = (jblk == 0) | (jblk == cur) | (jblk == cur - 1)
    score = jnp.where(jblk > cur, NEG, jnp.where(forced, -NEG, imp))
    n_slab = -(-n_slc // SUB)
    slabs = [score[a * SUB:(a + 1) * SUB, :] for a in range(n_slab)]
    cnts = [jnp.zeros((SUB, QB), F32) for _ in range(n_slab)]
    isub = lax.broadcasted_iota(jnp.int32, (SUB, QB), 0)
    for jp in range(n_slc):
        rowv = jnp.broadcast_to(score[jp:jp + 1, :], (SUB, QB))
        for a in range(n_slab):
            if a < jp // SUB:
                beats = jnp.where(rowv > slabs[a], 1.0, 0.0)
            elif a > jp // SUB:
                beats = jnp.where(rowv >= slabs[a], 1.0, 0.0)
            else:
                beats = jnp.where(isub > jp % SUB,
                                  jnp.where(rowv >= slabs[a], 1.0, 0.0), jnp.where(rowv > slabs[a], 1.0, 0.0))
            cnts[a] = cnts[a] + beats
    cnt = jnp.concatenate(cnts + [jnp.full((NR - n_slab * SUB, QB), float(NR), F32)] * (NR > n_slab * SUB), axis=0)
    sel = (cnt < float(N_SEL)) & (jblk <= cur) & (jblk < n_slc)
    a_nat = jnp.where(jblk == 0, 1.0, jnp.where(sel, (SLC_LEN * jblk).astype(F32), NEG))
    a_nat = jnp.where(jblk < n_slc, a_nat, 0.0)
    a_t = pltpu.roll(a_nat, NR - 1, 0)
    a_pad = jnp.concatenate([jnp.zeros((LANES - NR, QB), F32), a_t], axis=0)

    any_q = jnp.max(jnp.where(sel, 1.0, 0.0), axis=1, keepdims=True)
    jcol = lax.broadcasted_iota(jnp.int32, (NR, 1), 0)
    bits = jnp.where(any_q > 0.0, lax.shift_left(jnp.int32(1), jcol & 31), 0)
    word0 = jnp.sum(jnp.where(jcol < 32, bits, 0))
    word1 = jnp.sum(jnp.where(jcol >= 32, bits, 0))

    qs = jnp.concatenate([(q_t[r] + a_pad * slopes[r]).astype(BF) for r in range(R)], axis=1)

    tiles_per = KT // LANES
    blocks_per = KT // SLC_LEN
    n_full = q0 // KT
    dummy = seq // KT

    def scan(ti, n):
        word = jnp.where(ti < 32 // blocks_per, word0, word1)
        hit = (lax.shift_right_logical(word, (ti * blocks_per) & 31) & ((1 << blocks_per) - 1)) != 0
        act_ref[n] = ti
        return n + jnp.where(hit, 1, 0)

    n_act = lax.fori_loop(0, n_full, scan, 0)
    act_ref[n_act] = dummy
    n_pairs = (n_act + 1) >> 1
    act_ref[2 * n_pairs] = n_full
    act_ref[2 * n_pairs + 1] = dummy

    def pair_scores(j, slot):
        for h in range(2):
            k0 = pl.multiple_of(act_ref[2 * j + h] * KT, KT)
            s_buf[slot, h * KT:(h + 1) * KT, :] = jnp.dot(ksa[pl.ds(k0, KT), :], qs, preferred_element_type=F32)

    def soft(slot, m_prev, diag):
        s = s_buf[slot]
        if diag:
            r_d = lax.broadcasted_iota(jnp.int32, (2 * KT, QB), 0)
            t_d = q0 + lax.broadcasted_iota(jnp.int32, (2 * KT, QB), 1)
            s = s + tile4(jnp.where((r_d < KT) & (n_full * KT + r_d > t_d), NEG, 0.0))
        m_new = jnp.maximum(m_prev, jnp.max(s, axis=0, keepdims=True))
        p_buf[slot] = jnp.exp(s - m_new).astype(BF)
        return m_new, jnp.exp(m_prev - m_new)

    def accumulate(ta, tb, slot, alpha):
        v_t = jnp.concatenate([vst[ta * tiles_per + c] for c in range(tiles_per)]
                              + [vst[tb * tiles_per + c] for c in range(tiles_per)], axis=1)
        acc_scr[...] = acc_scr[...] * alpha + jnp.dot(v_t, p_buf[slot], preferred_element_type=F32)

    acc_scr[...] = jnp.zeros(acc_scr.shape, F32)
    pair_scores(0, 0)

    def body(j, carry):
        m_prev, alpha_prev, ta_prev, tb_prev = carry
        slot = j & 1
        accumulate(ta_prev, tb_prev, 1 - slot, alpha_prev)
        m_new, alpha = soft(slot, m_prev, False)
        pair_scores(j + 1, 1 - slot)
        return m_new, alpha, act_ref[2 * j], act_ref[2 * j + 1]

    init = (jnp.full((1, R * QB), NEG, F32), jnp.ones((1, R * QB), F32), jnp.int32(dummy), jnp.int32(dummy))
    m_last, alpha_prev, ta_prev, tb_prev = lax.fori_loop(0, n_pairs, body, init)
    slot = n_pairs & 1
    accumulate(ta_prev, tb_prev, 1 - slot, alpha_prev)
    _, alpha_d = soft(slot, m_last, True)
    accumulate(n_full, dummy, slot, alpha_d)
    acc_s = acc_scr[...]

    kw = kwa[pl.ds(pl.multiple_of(q0, QB), WK), :]
    s_w = jnp.dot(kw, qw, preferred_element_type=F32)
    c_w = lax.broadcasted_iota(jnp.int32, (WK, QB), 0)
    d_w = c_w - lax.broadcasted_iota(jnp.int32, (WK, QB), 1)
    mask_w = (d_w > 0) & (d_w <= WINDOW) & (c_w >= WINDOW - q0)
    s_w = s_w + tile4(jnp.where(mask_w, 0.0, NEG))
    m_w = jnp.max(s_w, axis=0, keepdims=True)
    p_w = jnp.exp(s_w - m_w).astype(BF)
    vw_t = jnp.concatenate([vwt[qi + c] for c in range(WK // LANES)], axis=1)
    acc_w = jnp.dot(vw_t, p_w, preferred_element_type=F32)


    sg_t = jax.nn.sigmoid(gt_ref[0]).T
    for r in range(R):
        cs = slice(r * QB, (r + 1) * QB)
        a_s = acc_s[:, cs]
        a_w = acc_w[:, cs]
        g_c = sg_t[3 * r:3 * r + 1, :]
        g_s = sg_t[3 * r + 1:3 * r + 2, :] / a_s[NSA_DH:NSA_DH + 1, :]
        g_w = sg_t[3 * r + 2:3 * r + 3, :] / a_w[NSA_DH:NSA_DH + 1, :]
        out_t = g_c * o_c[:, cs] + g_s * a_s + g_w * a_w
        o_ref[0, :, r * LANES:(r + 1) * LANES] = out_t.T.astype(BF)


def _nsa_attend(proj3, small3, kc, vc, slope_tab, ovt):
    B, S, _ = proj3.shape
    G, R = NSA_GROUPS, NSA_REP
    n_cmp = kc.shape[2]
    kern = functools.partial(_nsa_kernel, seq=S)
    kv_spec = lambda off: pl.BlockSpec((1, S, LANES), lambda b, g, i: (b, 0, off // LANES + g))
    return pl.pallas_call(
        kern,
        grid=(B, G, S // QB),
        in_specs=[
            pl.BlockSpec((1, QB, R * LANES), lambda b, g, i: (b, i, OFF_NQ // (R * LANES) + g)),
            kv_spec(OFF_KS), kv_spec(OFF_VS), kv_spec(OFF_KW), kv_spec(OFF_VW),
            pl.BlockSpec((1, 1, n_cmp, LANES), lambda b, g, i: (b, g, 0, 0)),
            pl.BlockSpec((1, 1, n_cmp, LANES), lambda b, g, i: (b, g, 0, 0)),
            pl.BlockSpec((1, QB, LANES), lambda b, g, i: (b, i, 1 + g)),
            pl.BlockSpec((1, 8, LANES), lambda b, g, i: (g, 0, 0)),
            _resident(ovt.shape, lambda b, g, i: (0, 0)),
        ],
        out_specs=pl.BlockSpec((1, QB, R * LANES), lambda b, g, i: (b, i, g)),
        out_shape=jax.ShapeDtypeStruct((B, S, G * R * LANES), BF),
        scratch_shapes=[
            pltpu.VMEM((S + SLC_TILE, LANES), BF),
            pltpu.VMEM((S + WINDOW, LANES), BF),
            pltpu.VMEM(((S + SLC_TILE) // LANES, LANES, LANES), BF),
            pltpu.VMEM(((S + WINDOW) // LANES, LANES, LANES), BF),
            pltpu.VMEM((LANES, n_cmp), BF),
            pltpu.VMEM((LANES, R * QB), F32),
            pltpu.VMEM((2, 2 * SLC_TILE, R * QB), F32),
            pltpu.VMEM((2, 2 * SLC_TILE, R * QB), BF),
            pltpu.SMEM((S // SLC_TILE + 4,), jnp.int32),
        ],
        compiler_params=_params(("parallel", "parallel", "arbitrary")),
    )(proj3, proj3, proj3, proj3, proj3, kc, vc, small3, slope_tab, ovt)


def _merge_kernel(og_ref, on_ref, mg_ref, mn_ref, x_ref, wg_ref, wn_ref, wo_ref, g_ref, o_ref):
    a = jnp.dot(og_ref[...], wg_ref[...], preferred_element_type=F32)
    b = jnp.dot(on_ref[...], wn_ref[...], preferred_element_type=F32)
    mixed = jax.nn.sigmoid(mg_ref[...].astype(F32)) * a + jax.nn.sigmoid(mn_ref[...].astype(F32)) * b
    y = jnp.dot(mixed.astype(BF), wo_ref[...], preferred_element_type=F32)
    y = y * lax.rsqrt(jnp.mean(y * y, axis=-1, keepdims=True) + EPS) * g_ref[...]
    o_ref[...] = x_ref[...] + y


def _merge_out(o_gla2, o_nsa2, proj2, x2, wg, wn, wo, g, tm=512):
    n_tok = x2.shape[0]
    D = D_MODEL
    return pl.pallas_call(
        _merge_kernel,
        grid=(n_tok // tm,),
        in_specs=[
            pl.BlockSpec((tm, D), lambda i: (i, 0)),
            pl.BlockSpec((tm, D), lambda i: (i, 0)),
            pl.BlockSpec((tm, D), lambda i: (i, OFF_MG // D)),
            pl.BlockSpec((tm, D), lambda i: (i, OFF_MN // D)),
            pl.BlockSpec((tm, D), lambda i: (i, 0)),
            _resident(wg.shape, lambda i: (0, 0)),
            _resident(wn.shape, lambda i: (0, 0)),
            _resident(wo.shape, lambda i: (0, 0)),
            _resident((1, D), lambda i: (0, 0)),
        ],
        out_specs=pl.BlockSpec((tm, D), lambda i: (i, 0)),
        out_shape=jax.ShapeDtypeStruct((n_tok, D), F32),
        compiler_params=_params(("parallel",)),
    )(o_gla2, o_nsa2, proj2, proj2, x2, wg, wn, wo, g)


def _ffn_kernel(x_ref, gpre_ref, wg_ref, wu_ref, wd_ref, gpost_ref, o_ref, acc_scr, *, chunk):
    x = x_ref[...]
    h = (x * lax.rsqrt(jnp.mean(x * x, axis=-1, keepdims=True) + EPS) * gpre_ref[...]).astype(BF)
    d_ff = wg_ref.shape[1]
    for n, c0 in enumerate(range(0, d_ff, chunk)):
        c1 = min(c0 + chunk, d_ff)
        a = jnp.dot(h, wg_ref[:, c0:c1], preferred_element_type=F32)
        u = jnp.dot(h, wu_ref[:, c0:c1], preferred_element_type=F32)
        t = (a * jax.nn.sigmoid(a) * u).astype(BF)
        part = jnp.dot(t, wd_ref[c0:c1, :], preferred_element_type=F32)
        if n == 0:
            acc_scr[...] = part
        else:
            acc_scr[...] += part
    f = acc_scr[...]
    o_ref[...] = x + f * lax.rsqrt(jnp.mean(f * f, axis=-1, keepdims=True) + EPS) * gpost_ref[...]


def _ffn(x2, gpre, wg, wu, wd, gpost, tm=512, chunk=512):
    n_tok = x2.shape[0]
    D = D_MODEL
    kern = functools.partial(_ffn_kernel, chunk=chunk)
    return pl.pallas_call(
        kern,
        grid=(n_tok // tm,),
        in_specs=[
            pl.BlockSpec((tm, D), lambda i: (i, 0)),
            _resident((1, D), lambda i: (0, 0)),
            _resident(wg.shape, lambda i: (0, 0)),
            _resident(wu.shape, lambda i: (0, 0)),
            _resident(wd.shape, lambda i: (0, 0)),
            _resident((1, D), lambda i: (0, 0)),
        ],
        out_specs=pl.BlockSpec((tm, D), lambda i: (i, 0)),
        out_shape=jax.ShapeDtypeStruct((n_tok, D), F32),
        scratch_shapes=[pltpu.VMEM((tm, D), F32)],
        compiler_params=_params(("parallel",)),
    )(x2, gpre, wg, wu, wd, gpost)


def _pad_heads(w, n_heads, scale=1.0):
    d = w.shape[0]
    w = (w * scale).reshape(d, n_heads, NSA_DH)
    return jnp.pad(w, ((0, 0), (0, 0), (0, LANES - NSA_DH))).reshape(d, n_heads * LANES)


def _prep_in_weights(w_in):
    splits = np.cumsum([512, 512, 1024, 1024, GLA_RANK, 512, 128, 128, 128, 128, 128, 128, 24, 1024])
    (g_q, g_k, g_v, g_r, g_a, n_q, n_kc, n_vc, n_ks, n_vs, n_kw, n_vw, n_gate, m_g, m_n) = jnp.split(
        w_in, [int(s) for s in splits], axis=1)
    w_main = jnp.concatenate([
        g_q, g_k, g_v, g_r,
        _pad_heads(n_q, NSA_HEADS, NSA_DH ** -0.5),
        m_g, m_n,
        n_kc, n_vc,
        _pad_heads(n_ks, NSA_GROUPS), _pad_heads(n_vs, NSA_GROUPS),
        _pad_heads(n_kw, NSA_GROUPS), _pad_heads(n_vw, NSA_GROUPS)], axis=1).astype(BF)
    d = w_in.shape[0]
    per_g = NSA_REP * 3
    gates = jnp.pad(n_gate.reshape(d, NSA_GROUPS, per_g), ((0, 0), (0, 0), (0, LANES - per_g)))
    g_a3 = jnp.pad(jnp.concatenate([g_a, g_a, g_a], axis=1), ((0, 0), (0, LANES - 3 * GLA_RANK)))
    w_small = jnp.concatenate([g_a3, gates.reshape(d, NSA_GROUPS * LANES)], axis=1).astype(BF)
    return w_main, w_small


def _prep_compress(pe, w1, w2):
    eye = jnp.eye(NSA_GROUPS, dtype=F32)
    w1r = w1.reshape(CMP_LEN, NSA_DH, NSA_DH)
    w1e = jnp.einsum('lde,gh->lgdhe', w1r, eye).reshape(CMP_LEN * NSA_GROUPS * NSA_DH, NSA_GROUPS * NSA_DH)
    pe_e = jnp.broadcast_to(pe[:, None, :], (CMP_LEN, NSA_GROUPS, NSA_DH)).reshape(2, CMP_STRIDE * LANES)
    pe_e = jnp.pad(pe_e, ((0, 6), (0, 0)))
    w2e = jnp.stack([
        jnp.pad(jnp.pad(w2, ((g * NSA_DH, (NSA_GROUPS - 1 - g) * NSA_DH), (0, 0))), ((0, 0), (0, LANES - NSA_DH)))
        for g in range(NSA_GROUPS)])
    return pe_e.astype(F32), w1e.astype(BF), w2e.astype(BF)


def _overlap_table(seq):
    n_cmp = (seq - CMP_LEN) // CMP_STRIDE + 1
    n_slc = seq // SLC_LEN
    sc = CMP_STRIDE * np.arange(n_cmp)
    ss = SLC_LEN * np.arange(n_slc)
    ov = np.clip(np.minimum(sc[:, None] + CMP_LEN, ss[None, :] + SLC_LEN)
                 - np.maximum(sc[:, None], ss[None, :]), 0, None).astype(np.float32) / CMP_LEN
    ovt = np.zeros((NSA_DH, n_cmp + 1), np.float32)
    ovt[:n_slc, :n_cmp] = ov.T
    return jnp.asarray(ovt, dtype=BF)


def kernel(x, norm_mix_pre, norm_mix_post, norm_ffn_pre, norm_ffn_post, w_in, gla_w_alpha2, gla_b_alpha, gla_norm_g, nsa_cmp_pe_k, nsa_cmp_w1_k, nsa_cmp_w2_k, nsa_cmp_pe_v, nsa_cmp_w1_v, nsa_cmp_w2_v, w_proj_gla, w_proj_nsa, w_out, w_ffn_gate, w_ffn_up, w_ffn_down):
    B, S, D = x.shape
    depth = w_in.shape[0]
    n_tok = B * S
    h_idx = jnp.arange(NSA_HEADS, dtype=F32)
    slopes = jnp.exp2(-8.0 * (h_idx + 1.0) / NSA_HEADS).reshape(NSA_GROUPS, NSA_REP, 1)
    slope_tab = jnp.broadcast_to(jnp.pad(slopes, ((0, 0), (0, 8 - NSA_REP), (0, 0))), (NSA_GROUPS, 8, LANES))
    ovt = _overlap_table(S)
    x2 = x.reshape(n_tok, D)
    for l in range(depth):
        w_main, w_small = _prep_in_weights(w_in[l])
        proj2, small2 = _in_proj(x2, norm_mix_pre[l][None, :], w_main, w_small)
        proj3 = proj2.reshape(B, S, N_MAIN)
        small3 = small2.reshape(B, S, N_SMALL)

        w2_hi = gla_w_alpha2[l].astype(BF)
        w2_lo = (gla_w_alpha2[l] - w2_hi.astype(F32)).astype(BF)
        w2p = jnp.pad(jnp.concatenate([w2_hi, w2_hi, w2_lo], axis=0), ((0, LANES - 3 * GLA_RANK), (0, 0)))
        o_gla = _gla(proj3, small3, w2p, gla_b_alpha[l][None, :], gla_norm_g[l][None, :])

        xk = proj3[:, :, OFF_KC:OFF_KC + LANES].reshape(B, S // CMP_STRIDE, CMP_STRIDE * LANES)
        xv = proj3[:, :, OFF_VC:OFF_VC + LANES].reshape(B, S // CMP_STRIDE, CMP_STRIDE * LANES)
        pek, w1k, w2k = _prep_compress(nsa_cmp_pe_k[l], nsa_cmp_w1_k[l], nsa_cmp_w2_k[l])
        pev, w1v, w2v = _prep_compress(nsa_cmp_pe_v[l], nsa_cmp_w1_v[l], nsa_cmp_w2_v[l])
        kc, vc = _nsa_compress(xk, xv, pek, pev, w1k, w1v, w2k, w2v)
        o_nsa = _nsa_attend(proj3, small3, kc, vc, slope_tab, ovt)

        wn = jnp.pad(w_proj_nsa[l].reshape(NSA_HEADS, NSA_DH, D), ((0, 0), (0, LANES - NSA_DH), (0, 0)))
        x2 = _merge_out(o_gla.reshape(n_tok, -1), o_nsa.reshape(n_tok, -1), proj2, x2,
                        w_proj_gla[l].astype(BF), wn.reshape(NSA_HEADS * LANES, D).astype(BF),
                        w_out[l].astype(BF), norm_mix_post[l][None, :])
        x2 = _ffn(x2, norm_ffn_pre[l][None, :], w_ffn_gate[l].astype(BF), w_ffn_up[l].astype(BF),
                  w_ffn_down[l].astype(BF), norm_ffn_post[l][None, :])
    return x2.reshape(B, S, D)
```

```python
import functools

import numpy as np
import jax
import jax.numpy as jnp
from jax import lax
from jax.experimental import pallas as pl
from jax.experimental.pallas import tpu as pltpu

D_MODEL = 1024
GLA_HEADS = 4
GLA_DK = 128
GLA_DV = 256
GLA_RANK = 16
GLA_TAU = 16.0
GLA_CHUNK = 64
NSA_HEADS = 8
NSA_GROUPS = 2
NSA_REP = 4
NSA_DH = 64
CMP_LEN = 32
CMP_STRIDE = 16
SLC_LEN = 64
N_SEL = 16
WINDOW = 512
QB = 256
SLC_TILE = 256
D_FF = 2816
EPS = 1e-6
NEG = -1e30

LANES = 128
VMEM_LIMIT = 56 * 1024 * 1024
BF = jnp.bfloat16
F32 = jnp.float32

OFF_GQ = 0
OFF_GK = 512
OFF_GV = 1024
OFF_GR = 2048
OFF_NQ = 3072
OFF_MG = 4096
OFF_MN = 5120
OFF_KC = 6144
OFF_VC = 6272
OFF_KS = 6400
OFF_VS = 6656
OFF_KW = 6912
OFF_VW = 7168
N_MAIN = 7424
N_SMALL = 384

NT = (((1,), (1,)), ((), ()))
TN = (((0,), (0,)), ((), ()))


def _resident(shape, index_map):
    return pl.BlockSpec(shape, index_map, pipeline_mode=pl.Buffered(1))


def _params(sem):
    return pltpu.CompilerParams(dimension_semantics=sem, vmem_limit_bytes=VMEM_LIMIT)


def _in_proj_kernel(x_ref, g_ref, wm_ref, ws_ref, om_ref, os_ref, h_scr):
    j = pl.program_id(1)

    @pl.when(j == 0)
    def _():
        x = x_ref[...]
        y = x * lax.rsqrt(jnp.mean(x * x, axis=-1, keepdims=True) + EPS) * g_ref[...]
        h_scr[...] = y.astype(BF)
        os_ref[...] = jnp.dot(h_scr[...], ws_ref[...], preferred_element_type=F32)

    tn = om_ref.shape[1]
    step = 512
    for c0 in range(0, tn, step):
        c1 = min(c0 + step, tn)
        om_ref[:, c0:c1] = jnp.dot(h_scr[...], wm_ref[:, c0:c1], preferred_element_type=F32).astype(BF)


def _in_proj(x2, g, w_main, w_small, tm=512, n_split=2):
    n_tok = x2.shape[0]
    tn = N_MAIN // n_split
    return pl.pallas_call(
        _in_proj_kernel,
        grid=(n_tok // tm, n_split),
        in_specs=[
            pl.BlockSpec((tm, D_MODEL), lambda i, j: (i, 0)),
            _resident((1, D_MODEL), lambda i, j: (0, 0)),
            pl.BlockSpec((D_MODEL, tn), lambda i, j: (0, j)),
            _resident((D_MODEL, N_SMALL), lambda i, j: (0, 0)),
        ],
        out_specs=[
            pl.BlockSpec((tm, tn), lambda i, j: (i, j)),
            pl.BlockSpec((tm, N_SMALL), lambda i, j: (i, 0)),
        ],
        out_shape=[
            jax.ShapeDtypeStruct((n_tok, N_MAIN), BF),
            jax.ShapeDtypeStruct((n_tok, N_SMALL), F32),
        ],
        scratch_shapes=[pltpu.VMEM((tm, D_MODEL), BF)],
        compiler_params=_params(("parallel", "arbitrary")),
    )(x2, g, w_main, w_small)


def _gla_kernel(q_ref, k_ref, v_ref, r_ref, a_ref, w2_ref, b2_ref, ng_ref, o_ref, st_scr, *, n_chunks):
    blk = pl.program_id(2)

    @pl.when(blk == 0)
    def _():
        st_scr[...] = jnp.zeros_like(st_scr)

    C = GLA_CHUNK
    T = n_chunks * C

    def split3(x):
        hi = x.astype(BF)
        rem = x - hi.astype(F32)
        mid = rem.astype(BF)
        return hi, mid, (rem - mid.astype(F32)).astype(BF)

    a = a_ref[0]
    a_hi = a.astype(BF)
    a_lo = (a - a_hi.astype(F32)).astype(BF)
    lane = lax.broadcasted_iota(jnp.int32, (T, LANES), 1)
    in_lo = (lane >= GLA_RANK) & (lane < 2 * GLA_RANK)
    z = jnp.dot(jnp.where(in_lo, a_lo, a_hi), w2_ref[...], preferred_element_type=F32) + b2_ref[...]
    log_a = (jnp.minimum(z, 0.0) - jnp.log1p(jnp.exp(-jnp.abs(z)))) * (1.0 / GLA_TAU)

    x_wide = jnp.concatenate([log_a[c * C:(c + 1) * C] for c in range(n_chunks)], axis=1)
    x3 = jnp.concatenate(split3(x_wide), axis=0)
    r3 = lax.broadcasted_iota(jnp.int32, (C, 3 * C), 0)
    c3 = lax.broadcasted_iota(jnp.int32, (C, 3 * C), 1) & (C - 1)
    tri3 = jnp.where(c3 <= r3, 1.0, 0.0).astype(BF)
    b_wide = jnp.dot(tri3, x3, preferred_element_type=F32)
    bcum = jnp.concatenate([b_wide[:, c * LANES:(c + 1) * LANES] for c in range(n_chunks)], axis=0)
    last_rows = [b_wide[C - 1:C, c * LANES:(c + 1) * LANES] for c in range(n_chunks)]
    b_last = jnp.concatenate([jnp.broadcast_to(lr, (C, LANES)) for lr in last_rows], axis=0)

    q = q_ref[0].astype(F32)
    k = k_ref[0].astype(F32)
    v = v_ref[0]
    qe = (q * ((GLA_DK ** -0.5) * jnp.exp(bcum))).astype(BF)
    ke = (k * jnp.exp(-bcum)).astype(BF)
    kd = (k * jnp.exp(b_last - bcum)).astype(BF)

    H = min(T, 4 * C)
    row = lax.broadcasted_iota(jnp.int32, (H, H), 0)
    col = lax.broadcasted_iota(jnp.int32, (H, H), 1)
    keep = (col <= row) & ((col >> 6) == (row >> 6))
    intra = []
    for h0 in range(0, T, H):
        hs = slice(h0, h0 + H)
        attn = lax.dot_general(qe[hs], ke[hs], NT, preferred_element_type=F32)
        intra.append(jnp.dot(jnp.where(keep, attn, 0.0).astype(BF), v[hs], preferred_element_type=F32))
    o = jnp.concatenate(intra, axis=0)

    st = st_scr[...]
    inter = []
    for c in range(n_chunks):
        sl = slice(c * C, (c + 1) * C)
        inter.append(lax.dot_general(qe[sl], st.astype(BF), NT, preferred_element_type=F32))
        upd = lax.dot_general(v[sl], kd[sl], TN, preferred_element_type=F32)
        st = st * jnp.exp(last_rows[c]) + upd
    st_scr[...] = st
    o = o + jnp.concatenate(inter, axis=0)

    o = o * lax.rsqrt(jnp.mean(o * o, axis=-1, keepdims=True) + EPS) * ng_ref[...]
    r = r_ref[0].astype(F32)
    o_ref[0] = (o * (r * jax.nn.sigmoid(r))).astype(BF)


def _gla(proj3, small3, w2p, b2, ng, blk_tokens=512):
    B, S, _ = proj3.shape
    nblk = S // blk_tokens
    kern = functools.partial(_gla_kernel, n_chunks=blk_tokens // GLA_CHUNK)
    return pl.pallas_call(
        kern,
        grid=(B, GLA_HEADS, nblk),
        in_specs=[
            pl.BlockSpec((1, blk_tokens, GLA_DK), lambda b, h, i: (b, i, OFF_GQ // GLA_DK + h)),
            pl.BlockSpec((1, blk_tokens, GLA_DK), lambda b, h, i: (b, i, OFF_GK // GLA_DK + h)),
            pl.BlockSpec((1, blk_tokens, GLA_DV), lambda b, h, i: (b, i, OFF_GV // GLA_DV + h)),
            pl.BlockSpec((1, blk_tokens, GLA_DV), lambda b, h, i: (b, i, OFF_GR // GLA_DV + h)),
            pl.BlockSpec((1, blk_tokens, LANES), lambda b, h, i: (b, i, 0)),
            pl.BlockSpec((LANES, GLA_DK), lambda b, h, i: (0, h)),
            pl.BlockSpec((1, GLA_DK), lambda b, h, i: (0, h)),
            pl.BlockSpec((1, GLA_DV), lambda b, h, i: (0, 0)),
        ],
        out_specs=pl.BlockSpec((1, blk_tokens, GLA_DV), lambda b, h, i: (b, i, h)),
        out_shape=jax.ShapeDtypeStruct((B, S, GLA_HEADS * GLA_DV), BF),
        scratch_shapes=[pltpu.VMEM((GLA_DV, GLA_DK), F32)],
        compiler_params=_params(("parallel", "parallel", "arbitrary")),
    )(proj3, proj3, proj3, proj3, small3, w2p, b2, ng)


def _compress_kernel(xk_ref, xv_ref, pek_ref, pev_ref, w1k_ref, w1v_ref, w2k_ref, w2v_ref, kc_ref, vc_ref):
    n_rows = xk_ref.shape[1]
    half = CMP_STRIDE * LANES
    lane = lax.broadcasted_iota(jnp.int32, (n_rows, LANES), 1)
    row = lax.broadcasted_iota(jnp.int32, (n_rows, LANES), 0)
    end_c = CMP_STRIDE * row + (CMP_LEN - 1)
    c_k = jnp.where(lane == NSA_DH, end_c >> 6, jnp.where(lane == NSA_DH + 1, end_c & 63, 0)).astype(F32)
    c_v = jnp.where(lane == NSA_DH, 1.0, 0.0).astype(F32)

    def branch(x_ref, pe_ref, w1_ref, w2_ref, const, o_ref):
        x = x_ref[0].astype(F32)
        xa = (x + pe_ref[0:1, :]).astype(BF)
        xb = (x + pe_ref[1:2, :]).astype(BF)
        a = jnp.dot(xa, w1_ref[0:half, :], preferred_element_type=F32)
        b = jnp.dot(xb, w1_ref[half:2 * half, :], preferred_element_type=F32)
        pre = a + pltpu.roll(b, n_rows - 1, 0)
        hid = (pre * jax.nn.sigmoid(pre)).astype(BF)
        for g in range(NSA_GROUPS):
            o_ref[0, g] = (jnp.dot(hid, w2_ref[g], preferred_element_type=F32) + const).astype(BF)

    branch(xk_ref, pek_ref, w1k_ref, w2k_ref, c_k, kc_ref)
    branch(xv_ref, pev_ref, w1v_ref, w2v_ref, c_v, vc_ref)


def _nsa_compress(xk, xv, pek, pev, w1k, w1v, w2k, w2v):
    B, n_rows, width = xk.shape
    full = lambda shape: _resident(shape, lambda b: (0,) * len(shape))
    out = jax.ShapeDtypeStruct((B, NSA_GROUPS, n_rows, LANES), BF)
    return pl.pallas_call(
        _compress_kernel,
        grid=(B,),
        in_specs=[
            pl.BlockSpec((1, n_rows, width), lambda b: (b, 0, 0)),
            pl.BlockSpec((1, n_rows, width), lambda b: (b, 0, 0)),
            full(pek.shape), full(pev.shape), full(w1k.shape), full(w1v.shape),
            full(w2k.shape), full(w2v.shape),
        ],
        out_specs=[pl.BlockSpec((1, NSA_GROUPS, n_rows, LANES), lambda b: (b, 0, 0, 0))] * 2,
        out_shape=[out, out],
        compiler_params=_params(("parallel",)),
    )(xk, xv, pek, pev, w1k, w1v, w2k, w2v)


def _nsa_kernel(q_ref, ks_ref, vs_ref, kw_ref, vw_ref, kc_ref, vc_ref, gt_ref, sl_ref, ov_ref,
                o_ref, ksa, kwa, vst, vwt, vct, acc_scr, act_ref, *, seq):
    qi = pl.program_id(2)
    n_slc = seq // SLC_LEN
    R = NSA_REP
    KT = SLC_TILE
    WK = WINDOW + QB
    n_cmp = kc_ref.shape[2]

    def t_bf(x):
        return x.astype(F32).T.astype(BF)

    @pl.when(qi == 0)
    def _():
        lane = lax.broadcasted_iota(jnp.int32, (seq, LANES), 1)
        pos = lax.broadcasted_iota(jnp.int32, (seq, LANES), 0)
        blk = pos >> 6
        off = pos & 63
        c_s = jnp.where(lane == LANES - 1, off,
                        jnp.where((lane >= NSA_DH) & (lane - (NSA_DH - 1) == blk), 1, 0))
        ksa[...] = ks_ref[0] + c_s.astype(BF)
        c_w = jnp.where(lane == NSA_DH, blk, jnp.where(lane == NSA_DH + 1, off, 0))
        kwa[0:WINDOW, :] = jnp.zeros((WINDOW, LANES), BF)
        kwa[WINDOW:WINDOW + seq, :] = kw_ref[0] + c_w.astype(BF)
        ones = jnp.where(lax.broadcasted_iota(jnp.int32, (LANES, LANES), 1) == NSA_DH, 1, 0).astype(BF)
        n_pad = WINDOW // LANES
        for c in range(n_pad):
            vwt[c] = jnp.zeros((LANES, LANES), BF)

        def fill(c, carry):
            rows = pl.ds(pl.multiple_of(c * LANES, LANES), LANES)
            vst[c] = t_bf(vs_ref[0, rows, :] + ones)
            vwt[c + n_pad] = t_bf(vw_ref[0, rows, :] + ones)
            return carry

        lax.fori_loop(0, seq // LANES, fill, 0)
        for c in range(n_cmp // LANES):
            vct[:, c * LANES:(c + 1) * LANES] = t_bf(vc_ref[0, 0, c * LANES:(c + 1) * LANES, :])

    q0 = qi * QB
    rowi = lax.broadcasted_iota(jnp.int32, (LANES, QB), 0)
    slopes = [jnp.concatenate([sl_ref[0, r:r + 1, :]] * (QB // LANES), axis=1) for r in range(R)]
    q_t = [q_ref[0, :, r * LANES:(r + 1) * LANES].astype(F32).T for r in range(R)]

    qw = jnp.concatenate(
        [(q_t[r] + jnp.where(rowi == NSA_DH, slopes[r] * 64.0, jnp.where(rowi == NSA_DH + 1, slopes[r], 0.0))
          ).astype(BF) for r in range(R)], axis=1)

    def tile4(x):
        return jnp.concatenate([x] * R, axis=1)

    s_c = jnp.dot(kc_ref[0, 0], qw, preferred_element_type=F32)
    e_c = CMP_STRIDE * lax.broadcasted_iota(jnp.int32, (n_cmp, QB), 0) + (CMP_LEN - 1)
    t_c = q0 + lax.broadcasted_iota(jnp.int32, (n_cmp, QB), 1)
    mask_c = tile4(e_c <= t_c)
    s_c = jnp.where(mask_c, s_c, NEG)
    m_c = jnp.max(s_c, axis=0, keepdims=True)
    p_c = jnp.where(mask_c, jnp.exp(s_c - m_c), 0.0)
    l_c = jnp.sum(p_c, axis=0, keepdims=True)
    p_c = p_c * jnp.where(l_c > 0.0, 1.0 / l_c, 0.0)
    o_c = jnp.dot(vct[...], p_c.astype(BF), preferred_element_type=F32)

    psum = p_c[:, 0:QB]
    for r in range(1, R):
        psum = psum + p_c[:, r * QB:(r + 1) * QB]
    ovt = ov_ref[...]
    p_hi = psum.astype(BF)
    rem = psum - p_hi.astype(F32)
    p_mid = rem.astype(BF)
    p_lo = (rem - p_mid.astype(F32)).astype(BF)
    imp = (jnp.dot(ovt, p_hi, preferred_element_type=F32)
           + jnp.dot(ovt, p_mid, preferred_element_type=F32)
           + jnp.dot(ovt, p_lo, preferred_element_type=F32))

    NR = ovt.shape[0]
    SUB = 8
    jblk = lax.broadcasted_iota(jnp.int32, (NR, QB), 0)
    t_q = q0 + lax.broadcasted_iota(jnp.int32, (NR, QB), 1)
    cur = t_q >> 6
    forced = (jblk == 0) | (jblk == cur) | (jblk == cur - 1)
    score = jnp.where(jblk > cur, NEG, jnp.where(forced, -NEG, imp))
    n_slab = -(-n_slc // SUB)
    slabs = [score[a * SUB:(a + 1) * SUB, :] for a in range(n_slab)]
    cnts = [jnp.zeros((SUB, QB), F32) for _ in range(n_slab)]
    isub = lax.broadcasted_iota(jnp.int32, (SUB, QB), 0)
    for jp in range(n_slc):
        rowv = jnp.broadcast_to(score[jp:jp + 1, :], (SUB, QB))
        for a in range(n_slab):
            if a < jp // SUB:
                beats = jnp.where(rowv > slabs[a], 1.0, 0.0)
            elif a > jp // SUB:
                beats = jnp.where(rowv >= slabs[a], 1.0, 0.0)
            else:
                beats = jnp.where(isub > jp % SUB,
                                  jnp.where(rowv >= slabs[a], 1.0, 0.0), jnp.where(rowv > slabs[a], 1.0, 0.0))
            cnts[a] = cnts[a] + beats
    cnt = jnp.concatenate(cnts + [jnp.full((NR - n_slab * SUB, QB), float(NR), F32)] * (NR > n_slab * SUB), axis=0)
    sel = (cnt < float(N_SEL)) & (jblk <= cur) & (jblk < n_slc)
    a_nat = jnp.where(jblk == 0, 1.0, jnp.where(sel, (SLC_LEN * jblk).astype(F32), NEG))
    a_nat = jnp.where(jblk < n_slc, a_nat, 0.0)
    a_t = pltpu.roll(a_nat, NR - 1, 0)
    a_pad = jnp.concatenate([jnp.zeros((LANES - NR, QB), F32), a_t], axis=0)

    any_q = jnp.max(jnp.where(sel, 1.0, 0.0), axis=1, keepdims=True)
    jcol = lax.broadcasted_iota(jnp.int32, (NR, 1), 0)
    bits = jnp.where(any_q > 0.0, lax.shift_left(jnp.int32(1), jcol & 31), 0)
    word0 = jnp.sum(jnp.where(jcol < 32, bits, 0))
    word1 = jnp.sum(jnp.where(jcol >= 32, bits, 0))

    qs = jnp.concatenate([(q_t[r] + a_pad * slopes[r]).astype(BF) for r in range(R)], axis=1)

    tiles_per = KT // LANES
    blocks_per = KT // SLC_LEN
    n_full = q0 // KT

    def scan(ti, n):
        word = jnp.where(ti < 32 // blocks_per, word0, word1)
        hit = (lax.shift_right_logical(word, (ti * blocks_per) & 31) & ((1 << blocks_per) - 1)) != 0
        act_ref[n] = ti
        return n + jnp.where(hit, 1, 0)

    n_act = lax.fori_loop(0, n_full, scan, 0)
    for d in range(QB // KT):
        act_ref[n_act + d] = n_full + d

    def scores(ti):
        k0 = pl.multiple_of(ti * KT, KT)
        return jnp.dot(ksa[pl.ds(k0, KT), :], qs, preferred_element_type=F32)

    def absorb(s, ti, m_prev, diag):
        if diag:
            p_d = ti * KT + lax.broadcasted_iota(jnp.int32, (KT, QB), 0)
            t_d = q0 + lax.broadcasted_iota(jnp.int32, (KT, QB), 1)
            s = s + tile4(jnp.where(p_d <= t_d, 0.0, NEG))
        m_new = jnp.maximum(m_prev, jnp.max(s, axis=0, keepdims=True))
        alpha = jnp.exp(m_prev - m_new)
        p = jnp.exp(s - m_new).astype(BF)
        v_t = jnp.concatenate([vst[ti * tiles_per + c] for c in range(tiles_per)], axis=1)
        acc_scr[...] = acc_scr[...] * alpha + jnp.dot(v_t, p, preferred_element_type=F32)
        return m_new

    acc_scr[...] = jnp.zeros(acc_scr.shape, F32)

    def body(i, carry):
        s, m_prev = carry
        s_next = scores(act_ref[i + 1])
        return s_next, absorb(s, act_ref[i], m_prev, False)

    s_d, m_d = lax.fori_loop(0, n_act, body, (scores(act_ref[0]), jnp.full((1, R * QB), NEG, F32)))
    for d in range(QB // KT):
        s_next = scores(n_full + d + 1) if d + 1 < QB // KT else None
        m_d = absorb(s_d, n_full + d, m_d, True)
        s_d = s_next
    acc_s = acc_scr[...]

    kw = kwa[pl.ds(pl.multiple_of(q0, QB), WK), :]
    s_w = jnp.dot(kw, qw, preferred_element_type=F32)
    c_w = lax.broadcasted_iota(jnp.int32, (WK, QB), 0)
    d_w = c_w - lax.broadcasted_iota(jnp.int32, (WK, QB), 1)
    mask_w = (d_w > 0) & (d_w <= WINDOW) & (c_w >= WINDOW - q0)
    s_w = s_w + tile4(jnp.where(mask_w, 0.0, NEG))
    m_w = jnp.max(s_w, axis=0, keepdims=True)
    p_w = jnp.exp(s_w - m_w).astype(BF)
    vw_t = jnp.concatenate([vwt[qi * (QB // LANES) + c] for c in range(WK // LANES)], axis=1)
    acc_w = jnp.dot(vw_t, p_w, preferred_element_type=F32)


    sg_t = jax.nn.sigmoid(gt_ref[0]).T
    for r in range(R):
        cs = slice(r * QB, (r + 1) * QB)
        a_s = acc_s[:, cs]
        a_w = acc_w[:, cs]
        g_c = sg_t[3 * r:3 * r + 1, :]
        g_s = sg_t[3 * r + 1:3 * r + 2, :] / a_s[NSA_DH:NSA_DH + 1, :]
        g_w = sg_t[3 * r + 2:3 * r + 3, :] / a_w[NSA_DH:NSA_DH + 1, :]
        out_t = g_c * o_c[:, cs] + g_s * a_s + g_w * a_w
        o_ref[0, :, r * LANES:(r + 1) * LANES] = out_t.T.astype(BF)


def _nsa_attend(proj3, small3, kc, vc, slope_tab, ovt):
    B, S, _ = proj3.shape
    G, R = NSA_GROUPS, NSA_REP
    n_cmp = kc.shape[2]
    kern = functools.partial(_nsa_kernel, seq=S)
    kv_spec = lambda off: pl.BlockSpec((1, S, LANES), lambda b, g, i: (b, 0, off // LANES + g))
    return pl.pallas_call(
        kern,
        grid=(B, G, S // QB),
        in_specs=[
            pl.BlockSpec((1, QB, R * LANES), lambda b, g, i: (b, i, OFF_NQ // (R * LANES) + g)),
            kv_spec(OFF_KS), kv_spec(OFF_VS), kv_spec(OFF_KW), kv_spec(OFF_VW),
            pl.BlockSpec((1, 1, n_cmp, LANES), lambda b, g, i: (b, g, 0, 0)),
            pl.BlockSpec((1, 1, n_cmp, LANES), lambda b, g, i: (b, g, 0, 0)),
            pl.BlockSpec((1, QB, LANES), lambda b, g, i: (b, i, 1 + g)),
            pl.BlockSpec((1, 8, LANES), lambda b, g, i: (g, 0, 0)),
            _resident(ovt.shape, lambda b, g, i: (0, 0)),
        ],
        out_specs=pl.BlockSpec((1, QB, R * LANES), lambda b, g, i: (b, i, g)),
        out_shape=jax.ShapeDtypeStruct((B, S, G * R * LANES), BF),
        scratch_shapes=[
            pltpu.VMEM((S, LANES), BF),
            pltpu.VMEM((S + WINDOW, LANES), BF),
            pltpu.VMEM((S // LANES, LANES, LANES), BF),
            pltpu.VMEM(((S + WINDOW) // LANES, LANES, LANES), BF),
            pltpu.VMEM((LANES, n_cmp), BF),
            pltpu.VMEM((LANES, R * QB), F32),
            pltpu.SMEM((S // SLC_TILE + 1,), jnp.int32),
        ],
        compiler_params=_params(("parallel", "parallel", "arbitrary")),
    )(proj3, proj3, proj3, proj3, proj3, kc, vc, small3, slope_tab, ovt)


def _merge_kernel(og_ref, on_ref, mg_ref, mn_ref, x_ref, wg_ref, wn_ref, wo_ref, g_ref, o_ref):
    a = jnp.dot(og_ref[...], wg_ref[...], preferred_element_type=F32)
    b = jnp.dot(on_ref[...], wn_ref[...], preferred_element_type=F32)
    mixed = jax.nn.sigmoid(mg_ref[...].astype(F32)) * a + jax.nn.sigmoid(mn_ref[...].astype(F32)) * b
    y = jnp.dot(mixed.astype(BF), wo_ref[...], preferred_element_type=F32)
    y = y * lax.rsqrt(jnp.mean(y * y, axis=-1, keepdims=True) + EPS) * g_ref[...]
    o_ref[...] = x_ref[...] + y


def _merge_out(o_gla2, o_nsa2, proj2, x2, wg, wn, wo, g, tm=512):
    n_tok = x2.shape[0]
    D = D_MODEL
    return pl.pallas_call(
        _merge_kernel,
        grid=(n_tok // tm,),
        in_specs=[
            pl.BlockSpec((tm, D), lambda i: (i, 0)),
            pl.BlockSpec((tm, D), lambda i: (i, 0)),
            pl.BlockSpec((tm, D), lambda i: (i, OFF_MG // D)),
            pl.BlockSpec((tm, D), lambda i: (i, OFF_MN // D)),
            pl.BlockSpec((tm, D), lambda i: (i, 0)),
            _resident(wg.shape, lambda i: (0, 0)),
            _resident(wn.shape, lambda i: (0, 0)),
            _resident(wo.shape, lambda i: (0, 0)),
            _resident((1, D), lambda i: (0, 0)),
        ],
        out_specs=pl.BlockSpec((tm, D), lambda i: (i, 0)),
        out_shape=jax.ShapeDtypeStruct((n_tok, D), F32),
        compiler_params=_params(("parallel",)),
    )(o_gla2, o_nsa2, proj2, proj2, x2, wg, wn, wo, g)


def _ffn_kernel(x_ref, gpre_ref, wg_ref, wu_ref, wd_ref, gpost_ref, o_ref, acc_scr, *, chunk):
    x = x_ref[...]
    h = (x * lax.rsqrt(jnp.mean(x * x, axis=-1, keepdims=True) + EPS) * gpre_ref[...]).astype(BF)
    d_ff = wg_ref.shape[1]
    for n, c0 in enumerate(range(0, d_ff, chunk)):
        c1 = min(c0 + chunk, d_ff)
        a = jnp.dot(h, wg_ref[:, c0:c1], preferred_element_type=F32)
        u = jnp.dot(h, wu_ref[:, c0:c1], preferred_element_type=F32)
        t = (a * jax.nn.sigmoid(a) * u).astype(BF)
        part = jnp.dot(t, wd_ref[c0:c1, :], preferred_element_type=F32)
        if n == 0:
            acc_scr[...] = part
        else:
            acc_scr[...] += part
    f = acc_scr[...]
    o_ref[...] = x + f * lax.rsqrt(jnp.mean(f * f, axis=-1, keepdims=True) + EPS) * gpost_ref[...]


def _ffn(x2, gpre, wg, wu, wd, gpost, tm=512, chunk=512):
    n_tok = x2.shape[0]
    D = D_MODEL
    kern = functools.partial(_ffn_kernel, chunk=chunk)
    return pl.pallas_call(
        kern,
        grid=(n_tok // tm,),
        in_specs=[
            pl.BlockSpec((tm, D), lambda i: (i, 0)),
            _resident((1, D), lambda i: (0, 0)),
            _resident(wg.shape, lambda i: (0, 0)),
            _resident(wu.shape, lambda i: (0, 0)),
            _resident(wd.shape, lambda i: (0, 0)),
            _resident((1, D), lambda i: (0, 0)),
        ],
        out_specs=pl.BlockSpec((tm, D), lambda i: (i, 0)),
        out_shape=jax.ShapeDtypeStruct((n_tok, D), F32),
        scratch_shapes=[pltpu.VMEM((tm, D), F32)],
        compiler_params=_params(("parallel",)),
    )(x2, gpre, wg, wu, wd, gpost)


def _pad_heads(w, n_heads, scale=1.0):
    d = w.shape[0]
    w = (w * scale).reshape(d, n_heads, NSA_DH)
    return jnp.pad(w, ((0, 0), (0, 0), (0, LANES - NSA_DH))).reshape(d, n_heads * LANES)


def _prep_in_weights(w_in):
    splits = np.cumsum([512, 512, 1024, 1024, GLA_RANK, 512, 128, 128, 128, 128, 128, 128, 24, 1024])
    (g_q, g_k, g_v, g_r, g_a, n_q, n_kc, n_vc, n_ks, n_vs, n_kw, n_vw, n_gate, m_g, m_n) = jnp.split(
        w_in, [int(s) for s in splits], axis=1)
    w_main = jnp.concatenate([
        g_q, g_k, g_v, g_r,
        _pad_heads(n_q, NSA_HEADS, NSA_DH ** -0.5),
        m_g, m_n,
        n_kc, n_vc,
        _pad_heads(n_ks, NSA_GROUPS), _pad_heads(n_vs, NSA_GROUPS),
        _pad_heads(n_kw, NSA_GROUPS), _pad_heads(n_vw, NSA_GROUPS)], axis=1).astype(BF)
    d = w_in.shape[0]
    per_g = NSA_REP * 3
    gates = jnp.pad(n_gate.reshape(d, NSA_GROUPS, per_g), ((0, 0), (0, 0), (0, LANES - per_g)))
    g_a3 = jnp.pad(jnp.concatenate([g_a, g_a, g_a], axis=1), ((0, 0), (0, LANES - 3 * GLA_RANK)))
    w_small = jnp.concatenate([g_a3, gates.reshape(d, NSA_GROUPS * LANES)], axis=1).astype(BF)
    return w_main, w_small


def _prep_compress(pe, w1, w2):
    eye = jnp.eye(NSA_GROUPS, dtype=F32)
    w1r = w1.reshape(CMP_LEN, NSA_DH, NSA_DH)
    w1e = jnp.einsum('lde,gh->lgdhe', w1r, eye).reshape(CMP_LEN * NSA_GROUPS * NSA_DH, NSA_GROUPS * NSA_DH)
    pe_e = jnp.broadcast_to(pe[:, None, :], (CMP_LEN, NSA_GROUPS, NSA_DH)).reshape(2, CMP_STRIDE * LANES)
    pe_e = jnp.pad(pe_e, ((0, 6), (0, 0)))
    w2e = jnp.stack([
        jnp.pad(jnp.pad(w2, ((g * NSA_DH, (NSA_GROUPS - 1 - g) * NSA_DH), (0, 0))), ((0, 0), (0, LANES - NSA_DH)))
        for g in range(NSA_GROUPS)])
    return pe_e.astype(F32), w1e.astype(BF), w2e.astype(BF)


def _overlap_table(seq):
    n_cmp = (seq - CMP_LEN) // CMP_STRIDE + 1
    n_slc = seq // SLC_LEN
    sc = CMP_STRIDE * np.arange(n_cmp)
    ss = SLC_LEN * np.arange(n_slc)
    ov = np.clip(np.minimum(sc[:, None] + CMP_LEN, ss[None, :] + SLC_LEN)
                 - np.maximum(sc[:, None], ss[None, :]), 0, None).astype(np.float32) / CMP_LEN
    ovt = np.zeros((NSA_DH, n_cmp + 1), np.float32)
    ovt[:n_slc, :n_cmp] = ov.T
    return jnp.asarray(ovt, dtype=BF)


def kernel(x, norm_mix_pre, norm_mix_post, norm_ffn_pre, norm_ffn_post, w_in, gla_w_alpha2, gla_b_alpha, gla_norm_g, nsa_cmp_pe_k, nsa_cmp_w1_k, nsa_cmp_w2_k, nsa_cmp_pe_v, nsa_cmp_w1_v, nsa_cmp_w2_v, w_proj_gla, w_proj_nsa, w_out, w_ffn_gate, w_ffn_up, w_ffn_down):
    B, S, D = x.shape
    depth = w_in.shape[0]
    n_tok = B * S
    h_idx = jnp.arange(NSA_HEADS, dtype=F32)
    slopes = jnp.exp2(-8.0 * (h_idx + 1.0) / NSA_HEADS).reshape(NSA_GROUPS, NSA_REP, 1)
    slope_tab = jnp.broadcast_to(jnp.pad(slopes, ((0, 0), (0, 8 - NSA_REP), (0, 0))), (NSA_GROUPS, 8, LANES))
    ovt = _overlap_table(S)
    x2 = x.reshape(n_tok, D)
    for l in range(depth):
        w_main, w_small = _prep_in_weights(w_in[l])
        proj2, small2 = _in_proj(x2, norm_mix_pre[l][None, :], w_main, w_small)
        proj3 = proj2.reshape(B, S, N_MAIN)
        small3 = small2.reshape(B, S, N_SMALL)

        w2_hi = gla_w_alpha2[l].astype(BF)
        w2_lo = (gla_w_alpha2[l] - w2_hi.astype(F32)).astype(BF)
        w2p = jnp.pad(jnp.concatenate([w2_hi, w2_hi, w2_lo], axis=0), ((0, LANES - 3 * GLA_RANK), (0, 0)))
        o_gla = _gla(proj3, small3, w2p, gla_b_alpha[l][None, :], gla_norm_g[l][None, :])

        xk = proj3[:, :, OFF_KC:OFF_KC + LANES].reshape(B, S // CMP_STRIDE, CMP_STRIDE * LANES)
        xv = proj3[:, :, OFF_VC:OFF_VC + LANES].reshape(B, S // CMP_STRIDE, CMP_STRIDE * LANES)
        pek, w1k, w2k = _prep_compress(nsa_cmp_pe_k[l], nsa_cmp_w1_k[l], nsa_cmp_w2_k[l])
        pev, w1v, w2v = _prep_compress(nsa_cmp_pe_v[l], nsa_cmp_w1_v[l], nsa_cmp_w2_v[l])
        kc, vc = _nsa_compress(xk, xv, pek, pev, w1k, w1v, w2k, w2v)
        o_nsa = _nsa_attend(proj3, small3, kc, vc, slope_tab, ovt)

        wn = jnp.pad(w_proj_nsa[l].reshape(NSA_HEADS, NSA_DH, D), ((0, 0), (0, LANES - NSA_DH), (0, 0)))
        x2 = _merge_out(o_gla.reshape(n_tok, -1), o_nsa.reshape(n_tok, -1), proj2, x2,
                        w_proj_gla[l].astype(BF), wn.reshape(NSA_HEADS * LANES, D).astype(BF),
                        w_out[l].astype(BF), norm_mix_post[l][None, :])
        x2 = _ffn(x2, norm_ffn_pre[l][None, :], w_ffn_gate[l].astype(BF), w_ffn_up[l].astype(BF),
                  w_ffn_down[l].astype(BF), norm_ffn_post[l][None, :])
    return x2.reshape(B, S, D)
```

```python
import functools

import numpy as np
import jax
import jax.numpy as jnp
from jax import lax
from jax.experimental import pallas as pl
from jax.experimental.pallas import tpu as pltpu

D_MODEL = 1024
GLA_HEADS = 4
GLA_DK = 128
GLA_DV = 256
GLA_RANK = 16
GLA_TAU = 16.0
GLA_CHUNK = 64
NSA_HEADS = 8
NSA_GROUPS = 2
NSA_REP = 4
NSA_DH = 64
CMP_LEN = 32
CMP_STRIDE = 16
SLC_LEN = 64
N_SEL = 16
WINDOW = 512
QB = 256
SLC_TILE = 256
D_FF = 2816
EPS = 1e-6
NEG = -1e30

LANES = 128
VMEM_LIMIT = 56 * 1024 * 1024
BF = jnp.bfloat16
F32 = jnp.float32

OFF_GQ = 0
OFF_GK = 512
OFF_GV = 1024
OFF_GR = 2048
OFF_NQ = 3072
OFF_MG = 4096
OFF_MN = 5120
OFF_KC = 6144
OFF_VC = 6272
OFF_KS = 6400
OFF_VS = 6656
OFF_KW = 6912
OFF_VW = 7168
N_MAIN = 7424
N_SMALL = 384

NT = (((1,), (1,)), ((), ()))
TN = (((0,), (0,)), ((), ()))


def _resident(shape, index_map):
    return pl.BlockSpec(shape, index_map, pipeline_mode=pl.Buffered(1))


def _params(sem):
    return pltpu.CompilerParams(dimension_semantics=sem, vmem_limit_bytes=VMEM_LIMIT)


def _in_proj_kernel(x_ref, g_ref, wm_ref, ws_ref, om_ref, os_ref):
    x = x_ref[...]
    h = (x * lax.rsqrt(jnp.mean(x * x, axis=-1, keepdims=True) + EPS) * g_ref[...]).astype(BF)
    os_ref[...] = jnp.dot(h, ws_ref[...], preferred_element_type=F32)
    n_out = om_ref.shape[1]
    step = 512
    for c0 in range(0, n_out, step):
        c1 = min(c0 + step, n_out)
        om_ref[:, c0:c1] = jnp.dot(h, wm_ref[:, c0:c1], preferred_element_type=F32).astype(BF)


def _in_proj(x2, g, w_main, w_small, tm=512):
    n_tok = x2.shape[0]
    return pl.pallas_call(
        _in_proj_kernel,
        grid=(n_tok // tm,),
        in_specs=[
            pl.BlockSpec((tm, D_MODEL), lambda i: (i, 0)),
            _resident((1, D_MODEL), lambda i: (0, 0)),
            _resident((D_MODEL, N_MAIN), lambda i: (0, 0)),
            _resident((D_MODEL, N_SMALL), lambda i: (0, 0)),
        ],
        out_specs=[
            pl.BlockSpec((tm, N_MAIN), lambda i: (i, 0)),
            pl.BlockSpec((tm, N_SMALL), lambda i: (i, 0)),
        ],
        out_shape=[
            jax.ShapeDtypeStruct((n_tok, N_MAIN), BF),
            jax.ShapeDtypeStruct((n_tok, N_SMALL), F32),
        ],
        compiler_params=_params(("parallel",)),
    )(x2, g, w_main, w_small)


def _gla_kernel(q_ref, k_ref, v_ref, r_ref, a_ref, w2_ref, b2_ref, ng_ref, o_ref, st_scr, *, n_chunks):
    blk = pl.program_id(2)

    @pl.when(blk == 0)
    def _():
        st_scr[...] = jnp.zeros_like(st_scr)

    C = GLA_CHUNK
    T = n_chunks * C

    def split3(x):
        hi = x.astype(BF)
        rem = x - hi.astype(F32)
        mid = rem.astype(BF)
        return hi, mid, (rem - mid.astype(F32)).astype(BF)

    a = a_ref[0]
    a_hi = a.astype(BF)
    a_lo = (a - a_hi.astype(F32)).astype(BF)
    lane = lax.broadcasted_iota(jnp.int32, (T, LANES), 1)
    in_lo = (lane >= GLA_RANK) & (lane < 2 * GLA_RANK)
    z = jnp.dot(jnp.where(in_lo, a_lo, a_hi), w2_ref[...], preferred_element_type=F32) + b2_ref[...]
    log_a = (jnp.minimum(z, 0.0) - jnp.log1p(jnp.exp(-jnp.abs(z)))) * (1.0 / GLA_TAU)

    x_wide = jnp.concatenate([log_a[c * C:(c + 1) * C] for c in range(n_chunks)], axis=1)
    x3 = jnp.concatenate(split3(x_wide), axis=0)
    r3 = lax.broadcasted_iota(jnp.int32, (C, 3 * C), 0)
    c3 = lax.broadcasted_iota(jnp.int32, (C, 3 * C), 1) & (C - 1)
    tri3 = jnp.where(c3 <= r3, 1.0, 0.0).astype(BF)
    b_wide = jnp.dot(tri3, x3, preferred_element_type=F32)
    bcum = jnp.concatenate([b_wide[:, c * LANES:(c + 1) * LANES] for c in range(n_chunks)], axis=0)
    last_rows = [b_wide[C - 1:C, c * LANES:(c + 1) * LANES] for c in range(n_chunks)]
    b_last = jnp.concatenate([jnp.broadcast_to(lr, (C, LANES)) for lr in last_rows], axis=0)

    q = q_ref[0].astype(F32)
    k = k_ref[0].astype(F32)
    v = v_ref[0]
    qe = (q * ((GLA_DK ** -0.5) * jnp.exp(bcum))).astype(BF)
    ke = (k * jnp.exp(-bcum)).astype(BF)
    kd = (k * jnp.exp(b_last - bcum)).astype(BF)

    H = min(T, 4 * C)
    row = lax.broadcasted_iota(jnp.int32, (H, H), 0)
    col = lax.broadcasted_iota(jnp.int32, (H, H), 1)
    keep = (col <= row) & ((col >> 6) == (row >> 6))
    intra = []
    for h0 in range(0, T, H):
        hs = slice(h0, h0 + H)
        attn = lax.dot_general(qe[hs], ke[hs], NT, preferred_element_type=F32)
        intra.append(jnp.dot(jnp.where(keep, attn, 0.0).astype(BF), v[hs], preferred_element_type=F32))
    o = jnp.concatenate(intra, axis=0)

    st = st_scr[...]
    inter = []
    for c in range(n_chunks):
        sl = slice(c * C, (c + 1) * C)
        inter.append(lax.dot_general(qe[sl], st.astype(BF), NT, preferred_element_type=F32))
        upd = lax.dot_general(v[sl], kd[sl], TN, preferred_element_type=F32)
        st = st * jnp.exp(last_rows[c]) + upd
    st_scr[...] = st
    o = o + jnp.concatenate(inter, axis=0)

    o = o * lax.rsqrt(jnp.mean(o * o, axis=-1, keepdims=True) + EPS) * ng_ref[...]
    r = r_ref[0].astype(F32)
    o_ref[0] = (o * (r * jax.nn.sigmoid(r))).astype(BF)


def _gla(proj3, small3, w2p, b2, ng, blk_tokens=512):
    B, S, _ = proj3.shape
    nblk = S // blk_tokens
    kern = functools.partial(_gla_kernel, n_chunks=blk_tokens // GLA_CHUNK)
    return pl.pallas_call(
        kern,
        grid=(B, GLA_HEADS, nblk),
        in_specs=[
            pl.BlockSpec((1, blk_tokens, GLA_DK), lambda b, h, i: (b, i, OFF_GQ // GLA_DK + h)),
            pl.BlockSpec((1, blk_tokens, GLA_DK), lambda b, h, i: (b, i, OFF_GK // GLA_DK + h)),
            pl.BlockSpec((1, blk_tokens, GLA_DV), lambda b, h, i: (b, i, OFF_GV // GLA_DV + h)),
            pl.BlockSpec((1, blk_tokens, GLA_DV), lambda b, h, i: (b, i, OFF_GR // GLA_DV + h)),
            pl.BlockSpec((1, blk_tokens, LANES), lambda b, h, i: (b, i, 0)),
            pl.BlockSpec((LANES, GLA_DK), lambda b, h, i: (0, h)),
            pl.BlockSpec((1, GLA_DK), lambda b, h, i: (0, h)),
            pl.BlockSpec((1, GLA_DV), lambda b, h, i: (0, 0)),
        ],
        out_specs=pl.BlockSpec((1, blk_tokens, GLA_DV), lambda b, h, i: (b, i, h)),
        out_shape=jax.ShapeDtypeStruct((B, S, GLA_HEADS * GLA_DV), BF),
        scratch_shapes=[pltpu.VMEM((GLA_DV, GLA_DK), F32)],
        compiler_params=_params(("parallel", "parallel", "arbitrary")),
    )(proj3, proj3, proj3, proj3, small3, w2p, b2, ng)


def _compress_kernel(xk_ref, xv_ref, pek_ref, pev_ref, w1k_ref, w1v_ref, w2k_ref, w2v_ref, kc_ref, vc_ref):
    n_rows = xk_ref.shape[1]
    half = CMP_STRIDE * LANES
    lane = lax.broadcasted_iota(jnp.int32, (n_rows, LANES), 1)
    row = lax.broadcasted_iota(jnp.int32, (n_rows, LANES), 0)
    end_c = CMP_STRIDE * row + (CMP_LEN - 1)
    c_k = jnp.where(lane == NSA_DH, end_c >> 6, jnp.where(lane == NSA_DH + 1, end_c & 63, 0)).astype(F32)
    c_v = jnp.where(lane == NSA_DH, 1.0, 0.0).astype(F32)

    def branch(x_ref, pe_ref, w1_ref, w2_ref, const, o_ref):
        x = x_ref[0].astype(F32)
        xa = (x + pe_ref[0:1, :]).astype(BF)
        xb = (x + pe_ref[1:2, :]).astype(BF)
        a = jnp.dot(xa, w1_ref[0:half, :], preferred_element_type=F32)
        b = jnp.dot(xb, w1_ref[half:2 * half, :], preferred_element_type=F32)
        pre = a + pltpu.roll(b, n_rows - 1, 0)
        hid = (pre * jax.nn.sigmoid(pre)).astype(BF)
        for g in range(NSA_GROUPS):
            o_ref[0, g] = (jnp.dot(hid, w2_ref[g], preferred_element_type=F32) + const).astype(BF)

    branch(xk_ref, pek_ref, w1k_ref, w2k_ref, c_k, kc_ref)
    branch(xv_ref, pev_ref, w1v_ref, w2v_ref, c_v, vc_ref)


def _nsa_compress(xk, xv, pek, pev, w1k, w1v, w2k, w2v):
    B, n_rows, width = xk.shape
    full = lambda shape: _resident(shape, lambda b: (0,) * len(shape))
    out = jax.ShapeDtypeStruct((B, NSA_GROUPS, n_rows, LANES), BF)
    return pl.pallas_call(
        _compress_kernel,
        grid=(B,),
        in_specs=[
            pl.BlockSpec((1, n_rows, width), lambda b: (b, 0, 0)),
            pl.BlockSpec((1, n_rows, width), lambda b: (b, 0, 0)),
            full(pek.shape), full(pev.shape), full(w1k.shape), full(w1v.shape),
            full(w2k.shape), full(w2v.shape),
        ],
        out_specs=[pl.BlockSpec((1, NSA_GROUPS, n_rows, LANES), lambda b: (b, 0, 0, 0))] * 2,
        out_shape=[out, out],
        compiler_params=_params(("parallel",)),
    )(xk, xv, pek, pev, w1k, w1v, w2k, w2v)


def _nsa_kernel(q_ref, ks_ref, vs_ref, kw_ref, vw_ref, kc_ref, vc_ref, gt_ref, sl_ref, ov_ref,
                o_ref, ksa, kwa, vst, vwt, vct, acc_scr, act_ref, *, seq):
    qi = pl.program_id(2)
    n_slc = seq // SLC_LEN
    R = NSA_REP
    KT = SLC_TILE
    WK = WINDOW + QB
    n_cmp = kc_ref.shape[2]

    def t_bf(x):
        return x.astype(F32).T.astype(BF)

    @pl.when(qi == 0)
    def _():
        lane = lax.broadcasted_iota(jnp.int32, (seq, LANES), 1)
        pos = lax.broadcasted_iota(jnp.int32, (seq, LANES), 0)
        blk = pos >> 6
        off = pos & 63
        c_s = jnp.where(lane == LANES - 1, off,
                        jnp.where((lane >= NSA_DH) & (lane - (NSA_DH - 1) == blk), 1, 0))
        ksa[...] = ks_ref[0] + c_s.astype(BF)
        c_w = jnp.where(lane == NSA_DH, blk, jnp.where(lane == NSA_DH + 1, off, 0))
        kwa[0:WINDOW, :] = jnp.zeros((WINDOW, LANES), BF)
        kwa[WINDOW:WINDOW + seq, :] = kw_ref[0] + c_w.astype(BF)
        ones = jnp.where(lax.broadcasted_iota(jnp.int32, (LANES, LANES), 1) == NSA_DH, 1, 0).astype(BF)
        n_pad = WINDOW // LANES
        for c in range(n_pad):
            vwt[c] = jnp.zeros((LANES, LANES), BF)

        def fill(c, carry):
            rows = pl.ds(pl.multiple_of(c * LANES, LANES), LANES)
            vst[c] = t_bf(vs_ref[0, rows, :] + ones)
            vwt[c + n_pad] = t_bf(vw_ref[0, rows, :] + ones)
            return carry

        lax.fori_loop(0, seq // LANES, fill, 0)
        for c in range(n_cmp // LANES):
            vct[:, c * LANES:(c + 1) * LANES] = t_bf(vc_ref[0, 0, c * LANES:(c + 1) * LANES, :])

    q0 = qi * QB
    rowi = lax.broadcasted_iota(jnp.int32, (LANES, QB), 0)
    slopes = [jnp.concatenate([sl_ref[0, r:r + 1, :]] * (QB // LANES), axis=1) for r in range(R)]
    q_t = [q_ref[0, :, r * LANES:(r + 1) * LANES].astype(F32).T for r in range(R)]

    qw = jnp.concatenate(
        [(q_t[r] + jnp.where(rowi == NSA_DH, slopes[r] * 64.0, jnp.where(rowi == NSA_DH + 1, slopes[r], 0.0))
          ).astype(BF) for r in range(R)], axis=1)

    def tile4(x):
        return jnp.concatenate([x] * R, axis=1)

    s_c = jnp.dot(kc_ref[0, 0], qw, preferred_element_type=F32)
    e_c = CMP_STRIDE * lax.broadcasted_iota(jnp.int32, (n_cmp, QB), 0) + (CMP_LEN - 1)
    t_c = q0 + lax.broadcasted_iota(jnp.int32, (n_cmp, QB), 1)
    s_c = s_c + tile4(jnp.where(e_c <= t_c, 0.0, NEG))
    m_c = jnp.maximum(jnp.max(s_c, axis=0, keepdims=True), 0.1 * NEG)
    p_c = jnp.exp(s_c - m_c)
    l_c = jnp.sum(p_c, axis=0, keepdims=True)
    p_c = p_c * jnp.where(l_c > 0.0, 1.0 / l_c, 0.0)
    o_c = jnp.dot(vct[...], p_c.astype(BF), preferred_element_type=F32)

    psum = p_c[:, 0:QB]
    for r in range(1, R):
        psum = psum + p_c[:, r * QB:(r + 1) * QB]
    ovt = ov_ref[...]
    p_hi = psum.astype(BF)
    rem = psum - p_hi.astype(F32)
    p_mid = rem.astype(BF)
    p_lo = (rem - p_mid.astype(F32)).astype(BF)
    imp = (jnp.dot(ovt, p_hi, preferred_element_type=F32)
           + jnp.dot(ovt, p_mid, preferred_element_type=F32)
           + jnp.dot(ovt, p_lo, preferred_element_type=F32))

    NR = ovt.shape[0]
    SUB = 8
    jblk = lax.broadcasted_iota(jnp.int32, (NR, QB), 0)
    t_q = q0 + lax.broadcasted_iota(jnp.int32, (NR, QB), 1)
    cur = t_q >> 6
    forced = (jblk == 0) | (jblk == cur) | (jblk == cur - 1)
    score = jnp.where(jblk > cur, NEG, jnp.where(forced, -NEG, imp))
    n_slab = -(-n_slc // SUB)
    isub = lax.broadcasted_iota(jnp.int32, (SUB, QB), 0)

    def rank_counts(ns):
        def fn():
            slabs = [score[a * SUB:(a + 1) * SUB, :] for a in range(ns)]
            cnts = [jnp.zeros((SUB, QB), F32) for _ in range(ns)]
            for jp in range(min(ns * SUB, n_slc)):
                rowv = jnp.broadcast_to(score[jp:jp + 1, :], (SUB, QB))
                for a in range(ns):
                    if a < jp // SUB:
                        beats = jnp.where(rowv > slabs[a], 1.0, 0.0)
                    elif a > jp // SUB:
                        beats = jnp.where(rowv >= slabs[a], 1.0, 0.0)
                    else:
                        beats = jnp.where(isub > jp % SUB, jnp.where(rowv >= slabs[a], 1.0, 0.0),
                                          jnp.where(rowv > slabs[a], 1.0, 0.0))
                    cnts[a] = cnts[a] + beats
            rest = NR - ns * SUB
            return jnp.concatenate(cnts + [jnp.full((rest, QB), float(NR), F32)] * (rest > 0), axis=0)
        return fn

    last_blk = (q0 + QB - 1) >> 6
    cnt = lax.switch(last_blk // SUB, [rank_counts(ns) for ns in range(1, n_slab + 1)])
    sel = (cnt < float(N_SEL)) & (jblk <= cur) & (jblk < n_slc)
    a_nat = jnp.where(jblk == 0, 1.0, jnp.where(sel, (SLC_LEN * jblk).astype(F32), NEG))
    a_nat = jnp.where(jblk < n_slc, a_nat, 0.0)
    a_t = pltpu.roll(a_nat, NR - 1, 0)
    a_pad = jnp.concatenate([jnp.zeros((LANES - NR, QB), F32), a_t], axis=0)

    any_q = jnp.max(jnp.where(sel, 1.0, 0.0), axis=1, keepdims=True)
    jcol = lax.broadcasted_iota(jnp.int32, (NR, 1), 0)
    bits = jnp.where(any_q > 0.0, lax.shift_left(jnp.int32(1), jcol & 31), 0)
    word0 = jnp.sum(jnp.where(jcol < 32, bits, 0))
    word1 = jnp.sum(jnp.where(jcol >= 32, bits, 0))

    qs = jnp.concatenate([(q_t[r] + a_pad * slopes[r]).astype(BF) for r in range(R)], axis=1)

    tiles_per = KT // LANES
    blocks_per = KT // SLC_LEN
    n_full = q0 // KT

    def scan(ti, n):
        word = jnp.where(ti < 32 // blocks_per, word0, word1)
        hit = (lax.shift_right_logical(word, (ti * blocks_per) & 31) & ((1 << blocks_per) - 1)) != 0
        act_ref[n] = ti
        return n + jnp.where(hit, 1, 0)

    n_act = lax.fori_loop(0, n_full, scan, 0)
    for d in range(QB // KT):
        act_ref[n_act + d] = n_full + d

    def scores(ti):
        k0 = pl.multiple_of(ti * KT, KT)
        return jnp.dot(ksa[pl.ds(k0, KT), :], qs, preferred_element_type=F32)

    def absorb(s, ti, m_prev, diag):
        if diag:
            p_d = ti * KT + lax.broadcasted_iota(jnp.int32, (KT, QB), 0)
            t_d = q0 + lax.broadcasted_iota(jnp.int32, (KT, QB), 1)
            s = s + tile4(jnp.where(p_d <= t_d, 0.0, NEG))
        m_new = jnp.maximum(m_prev, jnp.max(s, axis=0, keepdims=True))
        alpha = jnp.exp(m_prev - m_new)
        p = jnp.exp(s - m_new).astype(BF)
        v_t = jnp.concatenate([vst[ti * tiles_per + c] for c in range(tiles_per)], axis=1)
        acc_scr[...] = acc_scr[...] * alpha + jnp.dot(v_t, p, preferred_element_type=F32)
        return m_new

    acc_scr[...] = jnp.zeros(acc_scr.shape, F32)

    def body(i, carry):
        s, m_prev = carry
        s_next = scores(act_ref[i + 1])
        return s_next, absorb(s, act_ref[i], m_prev, False)

    s_d, m_d = lax.fori_loop(0, n_act, body, (scores(act_ref[0]), jnp.full((1, R * QB), NEG, F32)))
    for d in range(QB // KT):
        s_next = scores(n_full + d + 1) if d + 1 < QB // KT else None
        m_d = absorb(s_d, n_full + d, m_d, True)
        s_d = s_next
    acc_s = acc_scr[...]

    kw = kwa[pl.ds(pl.multiple_of(q0, QB), WK), :]
    s_w = jnp.dot(kw, qw, preferred_element_type=F32)
    c_w = lax.broadcasted_iota(jnp.int32, (WK, QB), 0)
    d_w = c_w - lax.broadcasted_iota(jnp.int32, (WK, QB), 1)
    mask_w = (d_w > 0) & (d_w <= WINDOW) & (c_w >= WINDOW - q0)
    s_w = s_w + tile4(jnp.where(mask_w, 0.0, NEG))
    m_w = jnp.max(s_w, axis=0, keepdims=True)
    p_w = jnp.exp(s_w - m_w).astype(BF)
    vw_t = jnp.concatenate([vwt[qi * (QB // LANES) + c] for c in range(WK // LANES)], axis=1)
    acc_w = jnp.dot(vw_t, p_w, preferred_element_type=F32)


    sg_t = jax.nn.sigmoid(gt_ref[0]).T
    for r in range(R):
        cs = slice(r * QB, (r + 1) * QB)
        a_s = acc_s[:, cs]
        a_w = acc_w[:, cs]
        g_c = sg_t[3 * r:3 * r + 1, :]
        g_s = sg_t[3 * r + 1:3 * r + 2, :] / a_s[NSA_DH:NSA_DH + 1, :]
        g_w = sg_t[3 * r + 2:3 * r + 3, :] / a_w[NSA_DH:NSA_DH + 1, :]
        out_t = g_c * o_c[:, cs] + g_s * a_s + g_w * a_w
        o_ref[0, :, r * LANES:(r + 1) * LANES] = out_t.T.astype(BF)


def _nsa_attend(proj3, small3, kc, vc, slope_tab, ovt):
    B, S, _ = proj3.shape
    G, R = NSA_GROUPS, NSA_REP
    n_cmp = kc.shape[2]
    kern = functools.partial(_nsa_kernel, seq=S)
    kv_spec = lambda off: pl.BlockSpec((1, S, LANES), lambda b, g, i: (b, 0, off // LANES + g))
    return pl.pallas_call(
        kern,
        grid=(B, G, S // QB),
        in_specs=[
            pl.BlockSpec((1, QB, R * LANES), lambda b, g, i: (b, i, OFF_NQ // (R * LANES) + g)),
            kv_spec(OFF_KS), kv_spec(OFF_VS), kv_spec(OFF_KW), kv_spec(OFF_VW),
            pl.BlockSpec((1, 1, n_cmp, LANES), lambda b, g, i: (b, g, 0, 0)),
            pl.BlockSpec((1, 1, n_cmp, LANES), lambda b, g, i: (b, g, 0, 0)),
            pl.BlockSpec((1, QB, LANES), lambda b, g, i: (b, i, 1 + g)),
            pl.BlockSpec((1, 8, LANES), lambda b, g, i: (g, 0, 0)),
            _resident(ovt.shape, lambda b, g, i: (0, 0)),
        ],
        out_specs=pl.BlockSpec((1, QB, R * LANES), lambda b, g, i: (b, i, g)),
        out_shape=jax.ShapeDtypeStruct((B, S, G * R * LANES), BF),
        scratch_shapes=[
            pltpu.VMEM((S, LANES), BF),
            pltpu.VMEM((S + WINDOW, LANES), BF),
            pltpu.VMEM((S // LANES, LANES, LANES), BF),
            pltpu.VMEM(((S + WINDOW) // LANES, LANES, LANES), BF),
            pltpu.VMEM((LANES, n_cmp), BF),
            pltpu.VMEM((LANES, R * QB), F32),
            pltpu.SMEM((S // SLC_TILE + 1,), jnp.int32),
        ],
        compiler_params=_params(("parallel", "parallel", "arbitrary")),
    )(proj3, proj3, proj3, proj3, proj3, kc, vc, small3, slope_tab, ovt)


def _merge_kernel(og_ref, on_ref, mg_ref, mn_ref, x_ref, wg_ref, wn_ref, wo_ref, g_ref, o_ref):
    a = jnp.dot(og_ref[...], wg_ref[...], preferred_element_type=F32)
    b = jnp.dot(on_ref[...], wn_ref[...], preferred_element_type=F32)
    mixed = jax.nn.sigmoid(mg_ref[...].astype(F32)) * a + jax.nn.sigmoid(mn_ref[...].astype(F32)) * b
    y = jnp.dot(mixed.astype(BF), wo_ref[...], preferred_element_type=F32)
    y = y * lax.rsqrt(jnp.mean(y * y, axis=-1, keepdims=True) + EPS) * g_ref[...]
    o_ref[...] = x_ref[...] + y


def _merge_out(o_gla2, o_nsa2, proj2, x2, wg, wn, wo, g, tm=512):
    n_tok = x2.shape[0]
    D = D_MODEL
    return pl.pallas_call(
        _merge_kernel,
        grid=(n_tok // tm,),
        in_specs=[
            pl.BlockSpec((tm, D), lambda i: (i, 0)),
            pl.BlockSpec((tm, D), lambda i: (i, 0)),
            pl.BlockSpec((tm, D), lambda i: (i, OFF_MG // D)),
            pl.BlockSpec((tm, D), lambda i: (i, OFF_MN // D)),
            pl.BlockSpec((tm, D), lambda i: (i, 0)),
            _resident(wg.shape, lambda i: (0, 0)),
            _resident(wn.shape, lambda i: (0, 0)),
            _resident(wo.shape, lambda i: (0, 0)),
            _resident((1, D), lambda i: (0, 0)),
        ],
        out_specs=pl.BlockSpec((tm, D), lambda i: (i, 0)),
        out_shape=jax.ShapeDtypeStruct((n_tok, D), F32),
        compiler_params=_params(("parallel",)),
    )(o_gla2, o_nsa2, proj2, proj2, x2, wg, wn, wo, g)


def _ffn_kernel(x_ref, gpre_ref, wg_ref, wu_ref, wd_ref, gpost_ref, o_ref, acc_scr, *, chunk):
    x = x_ref[...]
    h = (x * lax.rsqrt(jnp.mean(x * x, axis=-1, keepdims=True) + EPS) * gpre_ref[...]).astype(BF)
    d_ff = wg_ref.shape[1]
    for n, c0 in enumerate(range(0, d_ff, chunk)):
        c1 = min(c0 + chunk, d_ff)
        a = jnp.dot(h, wg_ref[:, c0:c1], preferred_element_type=F32)
        u = jnp.dot(h, wu_ref[:, c0:c1], preferred_element_type=F32)
        t = (a * jax.nn.sigmoid(a) * u).astype(BF)
        part = jnp.dot(t, wd_ref[c0:c1, :], preferred_element_type=F32)
        if n == 0:
            acc_scr[...] = part
        else:
            acc_scr[...] += part
    f = acc_scr[...]
    o_ref[...] = x + f * lax.rsqrt(jnp.mean(f * f, axis=-1, keepdims=True) + EPS) * gpost_ref[...]


def _ffn(x2, gpre, wg, wu, wd, gpost, tm=512, chunk=512):
    n_tok = x2.shape[0]
    D = D_MODEL
    kern = functools.partial(_ffn_kernel, chunk=chunk)
    return pl.pallas_call(
        kern,
        grid=(n_tok // tm,),
        in_specs=[
            pl.BlockSpec((tm, D), lambda i: (i, 0)),
            _resident((1, D), lambda i: (0, 0)),
            _resident(wg.shape, lambda i: (0, 0)),
            _resident(wu.shape, lambda i: (0, 0)),
            _resident(wd.shape, lambda i: (0, 0)),
            _resident((1, D), lambda i: (0, 0)),
        ],
        out_specs=pl.BlockSpec((tm, D), lambda i: (i, 0)),
        out_shape=jax.ShapeDtypeStruct((n_tok, D), F32),
        scratch_shapes=[pltpu.VMEM((tm, D), F32)],
        compiler_params=_params(("parallel",)),
    )(x2, gpre, wg, wu, wd, gpost)


def _pad_heads(w, n_heads, scale=1.0):
    d = w.shape[0]
    w = (w * scale).reshape(d, n_heads, NSA_DH)
    return jnp.pad(w, ((0, 0), (0, 0), (0, LANES - NSA_DH))).reshape(d, n_heads * LANES)


def _prep_in_weights(w_in):
    splits = np.cumsum([512, 512, 1024, 1024, GLA_RANK, 512, 128, 128, 128, 128, 128, 128, 24, 1024])
    (g_q, g_k, g_v, g_r, g_a, n_q, n_kc, n_vc, n_ks, n_vs, n_kw, n_vw, n_gate, m_g, m_n) = jnp.split(
        w_in, [int(s) for s in splits], axis=1)
    w_main = jnp.concatenate([
        g_q, g_k, g_v, g_r,
        _pad_heads(n_q, NSA_HEADS, NSA_DH ** -0.5),
        m_g, m_n,
        n_kc, n_vc,
        _pad_heads(n_ks, NSA_GROUPS), _pad_heads(n_vs, NSA_GROUPS),
        _pad_heads(n_kw, NSA_GROUPS), _pad_heads(n_vw, NSA_GROUPS)], axis=1).astype(BF)
    d = w_in.shape[0]
    per_g = NSA_REP * 3
    gates = jnp.pad(n_gate.reshape(d, NSA_GROUPS, per_g), ((0, 0), (0, 0), (0, LANES - per_g)))
    g_a3 = jnp.pad(jnp.concatenate([g_a, g_a, g_a], axis=1), ((0, 0), (0, LANES - 3 * GLA_RANK)))
    w_small = jnp.concatenate([g_a3, gates.reshape(d, NSA_GROUPS * LANES)], axis=1).astype(BF)
    return w_main, w_small


def _prep_compress(pe, w1, w2):
    eye = jnp.eye(NSA_GROUPS, dtype=F32)
    w1r = w1.reshape(CMP_LEN, NSA_DH, NSA_DH)
    w1e = jnp.einsum('lde,gh->lgdhe', w1r, eye).reshape(CMP_LEN * NSA_GROUPS * NSA_DH, NSA_GROUPS * NSA_DH)
    pe_e = jnp.broadcast_to(pe[:, None, :], (CMP_LEN, NSA_GROUPS, NSA_DH)).reshape(2, CMP_STRIDE * LANES)
    pe_e = jnp.pad(pe_e, ((0, 6), (0, 0)))
    w2e = jnp.stack([
        jnp.pad(jnp.pad(w2, ((g * NSA_DH, (NSA_GROUPS - 1 - g) * NSA_DH), (0, 0))), ((0, 0), (0, LANES - NSA_DH)))
        for g in range(NSA_GROUPS)])
    return pe_e.astype(F32), w1e.astype(BF), w2e.astype(BF)


def _overlap_table(seq):
    n_cmp = (seq - CMP_LEN) // CMP_STRIDE + 1
    n_slc = seq // SLC_LEN
    sc = CMP_STRIDE * np.arange(n_cmp)
    ss = SLC_LEN * np.arange(n_slc)
    ov = np.clip(np.minimum(sc[:, None] + CMP_LEN, ss[None, :] + SLC_LEN)
                 - np.maximum(sc[:, None], ss[None, :]), 0, None).astype(np.float32) / CMP_LEN
    ovt = np.zeros((NSA_DH, n_cmp + 1), np.float32)
    ovt[:n_slc, :n_cmp] = ov.T
    return jnp.asarray(ovt, dtype=BF)


def kernel(x, norm_mix_pre, norm_mix_post, norm_ffn_pre, norm_ffn_post, w_in, gla_w_alpha2, gla_b_alpha, gla_norm_g, nsa_cmp_pe_k, nsa_cmp_w1_k, nsa_cmp_w2_k, nsa_cmp_pe_v, nsa_cmp_w1_v, nsa_cmp_w2_v, w_proj_gla, w_proj_nsa, w_out, w_ffn_gate, w_ffn_up, w_ffn_down):
    B, S, D = x.shape
    depth = w_in.shape[0]
    n_tok = B * S
    h_idx = jnp.arange(NSA_HEADS, dtype=F32)
    slopes = jnp.exp2(-8.0 * (h_idx + 1.0) / NSA_HEADS).reshape(NSA_GROUPS, NSA_REP, 1)
    slope_tab = jnp.broadcast_to(jnp.pad(slopes, ((0, 0), (0, 8 - NSA_REP), (0, 0))), (NSA_GROUPS, 8, LANES))
    ovt = _overlap_table(S)
    x2 = x.reshape(n_tok, D)
    for l in range(depth):
        w_main, w_small = _prep_in_weights(w_in[l])
        proj2, small2 = _in_proj(x2, norm_mix_pre[l][None, :], w_main, w_small)
        proj3 = proj2.reshape(B, S, N_MAIN)
        small3 = small2.reshape(B, S, N_SMALL)

        w2_hi = gla_w_alpha2[l].astype(BF)
        w2_lo = (gla_w_alpha2[l] - w2_hi.astype(F32)).astype(BF)
        w2p = jnp.pad(jnp.concatenate([w2_hi, w2_hi, w2_lo], axis=0), ((0, LANES - 3 * GLA_RANK), (0, 0)))
        o_gla = _gla(proj3, small3, w2p, gla_b_alpha[l][None, :], gla_norm_g[l][None, :])

        xk = proj3[:, :, OFF_KC:OFF_KC + LANES].reshape(B, S // CMP_STRIDE, CMP_STRIDE * LANES)
        xv = proj3[:, :, OFF_VC:OFF_VC + LANES].reshape(B, S // CMP_STRIDE, CMP_STRIDE * LANES)
        pek, w1k, w2k = _prep_compress(nsa_cmp_pe_k[l], nsa_cmp_w1_k[l], nsa_cmp_w2_k[l])
        pev, w1v, w2v = _prep_compress(nsa_cmp_pe_v[l], nsa_cmp_w1_v[l], nsa_cmp_w2_v[l])
        kc, vc = _nsa_compress(xk, xv, pek, pev, w1k, w1v, w2k, w2v)
        o_nsa = _nsa_attend(proj3, small3, kc, vc, slope_tab, ovt)

        wn = jnp.pad(w_proj_nsa[l].reshape(NSA_HEADS, NSA_DH, D), ((0, 0), (0, LANES - NSA_DH), (0, 0)))
        x2 = _merge_out(o_gla.reshape(n_tok, -1), o_nsa.reshape(n_tok, -1), proj2, x2,
                        w_proj_gla[l].astype(BF), wn.reshape(NSA_HEADS * LANES, D).astype(BF),
                        w_out[l].astype(BF), norm_mix_post[l][None, :])
        x2 = _ffn(x2, norm_ffn_pre[l][None, :], w_ffn_gate[l].astype(BF), w_ffn_up[l].astype(BF),
                  w_ffn_down[l].astype(BF), norm_ffn_post[l][None, :])
    return x2.reshape(B, S, D)
```

```python
import functools

import numpy as np
import jax
import jax.numpy as jnp
from jax import lax
from jax.experimental import pallas as pl
from jax.experimental.pallas import tpu as pltpu

D_MODEL = 1024
GLA_HEADS = 4
GLA_DK = 128
GLA_DV = 256
GLA_RANK = 16
GLA_TAU = 16.0
GLA_CHUNK = 64
NSA_HEADS = 8
NSA_GROUPS = 2
NSA_REP = 4
NSA_DH = 64
CMP_LEN = 32
CMP_STRIDE = 16
SLC_LEN = 64
N_SEL = 16
WINDOW = 512
QB = 256
SLC_TILE = 256
VROWS = 80
D_FF = 2816
EPS = 1e-6
NEG = -1e30

LANES = 128
VMEM_LIMIT = 56 * 1024 * 1024
BF = jnp.bfloat16
F32 = jnp.float32

OFF_GQ = 0
OFF_GK = 512
OFF_GV = 1024
OFF_GR = 2048
OFF_NQ = 3072
OFF_MG = 4096
OFF_MN = 5120
OFF_KC = 6144
OFF_VC = 6272
OFF_KS = 6400
OFF_VS = 6656
OFF_KW = 6912
OFF_VW = 7168
N_MAIN = 7424
N_SMALL = 384

NT = (((1,), (1,)), ((), ()))
TN = (((0,), (0,)), ((), ()))


def _resident(shape, index_map):
    return pl.BlockSpec(shape, index_map, pipeline_mode=pl.Buffered(1))


def _params(sem):
    return pltpu.CompilerParams(dimension_semantics=sem, vmem_limit_bytes=VMEM_LIMIT)


def _in_proj_kernel(x_ref, g_ref, wm_ref, ws_ref, om_ref, os_ref):
    x = x_ref[...]
    h = (x * lax.rsqrt(jnp.mean(x * x, axis=-1, keepdims=True) + EPS) * g_ref[...]).astype(BF)
    os_ref[...] = jnp.dot(h, ws_ref[...], preferred_element_type=F32)
    n_out = om_ref.shape[1]
    step = 512
    for c0 in range(0, n_out, step):
        c1 = min(c0 + step, n_out)
        om_ref[:, c0:c1] = jnp.dot(h, wm_ref[:, c0:c1], preferred_element_type=F32).astype(BF)


def _in_proj(x2, g, w_main, w_small, tm=512):
    n_tok = x2.shape[0]
    return pl.pallas_call(
        _in_proj_kernel,
        grid=(n_tok // tm,),
        in_specs=[
            pl.BlockSpec((tm, D_MODEL), lambda i: (i, 0)),
            _resident((1, D_MODEL), lambda i: (0, 0)),
            _resident((D_MODEL, N_MAIN), lambda i: (0, 0)),
            _resident((D_MODEL, N_SMALL), lambda i: (0, 0)),
        ],
        out_specs=[
            pl.BlockSpec((tm, N_MAIN), lambda i: (i, 0)),
            pl.BlockSpec((tm, N_SMALL), lambda i: (i, 0)),
        ],
        out_shape=[
            jax.ShapeDtypeStruct((n_tok, N_MAIN), BF),
            jax.ShapeDtypeStruct((n_tok, N_SMALL), F32),
        ],
        compiler_params=_params(("parallel",)),
    )(x2, g, w_main, w_small)


def _gla_kernel(q_ref, k_ref, v_ref, r_ref, a_ref, w2_ref, b2_ref, ng_ref, o_ref, st_scr, *, n_chunks):
    blk = pl.program_id(2)

    @pl.when(blk == 0)
    def _():
        st_scr[...] = jnp.zeros_like(st_scr)

    C = GLA_CHUNK
    T = n_chunks * C

    def split3(x):
        hi = x.astype(BF)
        rem = x - hi.astype(F32)
        mid = rem.astype(BF)
        return hi, mid, (rem - mid.astype(F32)).astype(BF)

    a = a_ref[0]
    a_hi = a.astype(BF)
    a_lo = (a - a_hi.astype(F32)).astype(BF)
    lane = lax.broadcasted_iota(jnp.int32, (T, LANES), 1)
    in_lo = (lane >= GLA_RANK) & (lane < 2 * GLA_RANK)
    z = jnp.dot(jnp.where(in_lo, a_lo, a_hi), w2_ref[...], preferred_element_type=F32) + b2_ref[...]
    log_a = (jnp.minimum(z, 0.0) - jnp.log1p(jnp.exp(-jnp.abs(z)))) * (1.0 / GLA_TAU)

    x_wide = jnp.concatenate([log_a[c * C:(c + 1) * C] for c in range(n_chunks)], axis=1)
    x3 = jnp.concatenate(split3(x_wide), axis=0)
    r3 = lax.broadcasted_iota(jnp.int32, (C, 3 * C), 0)
    c3 = lax.broadcasted_iota(jnp.int32, (C, 3 * C), 1) & (C - 1)
    tri3 = jnp.where(c3 <= r3, 1.0, 0.0).astype(BF)
    b_wide = jnp.dot(tri3, x3, preferred_element_type=F32)
    bcum = jnp.concatenate([b_wide[:, c * LANES:(c + 1) * LANES] for c in range(n_chunks)], axis=0)
    last_rows = [b_wide[C - 1:C, c * LANES:(c + 1) * LANES] for c in range(n_chunks)]
    b_last = jnp.concatenate([jnp.broadcast_to(lr, (C, LANES)) for lr in last_rows], axis=0)

    q = q_ref[0].astype(F32)
    k = k_ref[0].astype(F32)
    v = v_ref[0]
    qe = (q * ((GLA_DK ** -0.5) * jnp.exp(bcum))).astype(BF)
    ke = (k * jnp.exp(-bcum)).astype(BF)
    kd = (k * jnp.exp(b_last - bcum)).astype(BF)

    H = min(T, 4 * C)
    row = lax.broadcasted_iota(jnp.int32, (H, H), 0)
    col = lax.broadcasted_iota(jnp.int32, (H, H), 1)
    keep = (col <= row) & ((col >> 6) == (row >> 6))
    intra = []
    for h0 in range(0, T, H):
        hs = slice(h0, h0 + H)
        attn = lax.dot_general(qe[hs], ke[hs], NT, preferred_element_type=F32)
        intra.append(jnp.dot(jnp.where(keep, attn, 0.0).astype(BF), v[hs], preferred_element_type=F32))
    o = jnp.concatenate(intra, axis=0)

    st = st_scr[...]
    inter = []
    for c in range(n_chunks):
        sl = slice(c * C, (c + 1) * C)
        inter.append(lax.dot_general(qe[sl], st.astype(BF), NT, preferred_element_type=F32))
        upd = lax.dot_general(v[sl], kd[sl], TN, preferred_element_type=F32)
        st = st * jnp.exp(last_rows[c]) + upd
    st_scr[...] = st
    o = o + jnp.concatenate(inter, axis=0)

    o = o * lax.rsqrt(jnp.mean(o * o, axis=-1, keepdims=True) + EPS) * ng_ref[...]
    r = r_ref[0].astype(F32)
    o_ref[0] = (o * (r * jax.nn.sigmoid(r))).astype(BF)


def _gla(proj3, small3, w2p, b2, ng, blk_tokens=512):
    B, S, _ = proj3.shape
    nblk = S // blk_tokens
    kern = functools.partial(_gla_kernel, n_chunks=blk_tokens // GLA_CHUNK)
    return pl.pallas_call(
        kern,
        grid=(B, GLA_HEADS, nblk),
        in_specs=[
            pl.BlockSpec((1, blk_tokens, GLA_DK), lambda b, h, i: (b, i, OFF_GQ // GLA_DK + h)),
            pl.BlockSpec((1, blk_tokens, GLA_DK), lambda b, h, i: (b, i, OFF_GK // GLA_DK + h)),
            pl.BlockSpec((1, blk_tokens, GLA_DV), lambda b, h, i: (b, i, OFF_GV // GLA_DV + h)),
            pl.BlockSpec((1, blk_tokens, GLA_DV), lambda b, h, i: (b, i, OFF_GR // GLA_DV + h)),
            pl.BlockSpec((1, blk_tokens, LANES), lambda b, h, i: (b, i, 0)),
            pl.BlockSpec((LANES, GLA_DK), lambda b, h, i: (0, h)),
            pl.BlockSpec((1, GLA_DK), lambda b, h, i: (0, h)),
            pl.BlockSpec((1, GLA_DV), lambda b, h, i: (0, 0)),
        ],
        out_specs=pl.BlockSpec((1, blk_tokens, GLA_DV), lambda b, h, i: (b, i, h)),
        out_shape=jax.ShapeDtypeStruct((B, S, GLA_HEADS * GLA_DV), BF),
        scratch_shapes=[pltpu.VMEM((GLA_DV, GLA_DK), F32)],
        compiler_params=_params(("parallel", "parallel", "arbitrary")),
    )(proj3, proj3, proj3, proj3, small3, w2p, b2, ng)


def _compress_kernel(xk_ref, xv_ref, pek_ref, pev_ref, w1k_ref, w1v_ref, w2k_ref, w2v_ref, kc_ref, vc_ref):
    n_rows = xk_ref.shape[1]
    half = CMP_STRIDE * LANES
    lane = lax.broadcasted_iota(jnp.int32, (n_rows, LANES), 1)
    row = lax.broadcasted_iota(jnp.int32, (n_rows, LANES), 0)
    end_c = CMP_STRIDE * row + (CMP_LEN - 1)
    c_k = jnp.where(lane == NSA_DH, end_c >> 6, jnp.where(lane == NSA_DH + 1, end_c & 63, 0)).astype(F32)
    c_v = jnp.where(lane == NSA_DH, 1.0, 0.0).astype(F32)

    def branch(x_ref, pe_ref, w1_ref, w2_ref, const, o_ref):
        x = x_ref[0].astype(F32)
        xa = (x + pe_ref[0:1, :]).astype(BF)
        xb = (x + pe_ref[1:2, :]).astype(BF)
        a = jnp.dot(xa, w1_ref[0:half, :], preferred_element_type=F32)
        b = jnp.dot(xb, w1_ref[half:2 * half, :], preferred_element_type=F32)
        pre = a + pltpu.roll(b, n_rows - 1, 0)
        hid = (pre * jax.nn.sigmoid(pre)).astype(BF)
        for g in range(NSA_GROUPS):
            o_ref[0, g] = (jnp.dot(hid, w2_ref[g], preferred_element_type=F32) + const).astype(BF)

    branch(xk_ref, pek_ref, w1k_ref, w2k_ref, c_k, kc_ref)
    branch(xv_ref, pev_ref, w1v_ref, w2v_ref, c_v, vc_ref)


def _nsa_compress(xk, xv, pek, pev, w1k, w1v, w2k, w2v):
    B, n_rows, width = xk.shape
    full = lambda shape: _resident(shape, lambda b: (0,) * len(shape))
    out = jax.ShapeDtypeStruct((B, NSA_GROUPS, n_rows, LANES), BF)
    return pl.pallas_call(
        _compress_kernel,
        grid=(B,),
        in_specs=[
            pl.BlockSpec((1, n_rows, width), lambda b: (b, 0, 0)),
            pl.BlockSpec((1, n_rows, width), lambda b: (b, 0, 0)),
            full(pek.shape), full(pev.shape), full(w1k.shape), full(w1v.shape),
            full(w2k.shape), full(w2v.shape),
        ],
        out_specs=[pl.BlockSpec((1, NSA_GROUPS, n_rows, LANES), lambda b: (b, 0, 0, 0))] * 2,
        out_shape=[out, out],
        compiler_params=_params(("parallel",)),
    )(xk, xv, pek, pev, w1k, w1v, w2k, w2v)


def _nsa_kernel(q_ref, ks_ref, vs_ref, kw_ref, vw_ref, kc_ref, vc_ref, gt_ref, sl_ref, ov_ref,
                o_ref, ksa, kwa, vst, vwt, vct, acc_scr, act_ref, *, seq):
    qi = pl.program_id(2)
    n_slc = seq // SLC_LEN
    R = NSA_REP
    KT = SLC_TILE
    WK = WINDOW + QB
    n_cmp = kc_ref.shape[2]

    def t_bf(x):
        return x.astype(F32).T[0:VROWS].astype(BF)

    @pl.when(qi == 0)
    def _():
        lane = lax.broadcasted_iota(jnp.int32, (seq, LANES), 1)
        pos = lax.broadcasted_iota(jnp.int32, (seq, LANES), 0)
        blk = pos >> 6
        off = pos & 63
        c_s = jnp.where(lane == LANES - 1, off,
                        jnp.where((lane >= NSA_DH) & (lane - (NSA_DH - 1) == blk), 1, 0))
        ksa[...] = ks_ref[0] + c_s.astype(BF)
        c_w = jnp.where(lane == NSA_DH, blk, jnp.where(lane == NSA_DH + 1, off, 0))
        kwa[0:WINDOW, :] = jnp.zeros((WINDOW, LANES), BF)
        kwa[WINDOW:WINDOW + seq, :] = kw_ref[0] + c_w.astype(BF)
        ones = jnp.where(lax.broadcasted_iota(jnp.int32, (LANES, LANES), 1) == NSA_DH, 1, 0).astype(BF)
        n_pad = WINDOW // LANES
        for c in range(n_pad):
            vwt[c] = jnp.zeros((VROWS, LANES), BF)

        def fill(c, carry):
            rows = pl.ds(pl.multiple_of(c * LANES, LANES), LANES)
            vst[c] = t_bf(vs_ref[0, rows, :] + ones)
            vwt[c + n_pad] = t_bf(vw_ref[0, rows, :] + ones)
            return carry

        lax.fori_loop(0, seq // LANES, fill, 0)
        for c in range(n_cmp // LANES):
            vct[:, c * LANES:(c + 1) * LANES] = t_bf(vc_ref[0, 0, c * LANES:(c + 1) * LANES, :])

    q0 = qi * QB
    rowi = lax.broadcasted_iota(jnp.int32, (LANES, QB), 0)
    slopes = [jnp.concatenate([sl_ref[0, r:r + 1, :]] * (QB // LANES), axis=1) for r in range(R)]
    q_t = [q_ref[0, :, r * LANES:(r + 1) * LANES].astype(F32).T for r in range(R)]

    qw = jnp.concatenate(
        [(q_t[r] + jnp.where(rowi == NSA_DH, slopes[r] * 64.0, jnp.where(rowi == NSA_DH + 1, slopes[r], 0.0))
          ).astype(BF) for r in range(R)], axis=1)

    def tile4(x):
        return jnp.concatenate([x] * R, axis=1)

    s_c = jnp.dot(kc_ref[0, 0], qw, preferred_element_type=F32)
    e_c = CMP_STRIDE * lax.broadcasted_iota(jnp.int32, (n_cmp, QB), 0) + (CMP_LEN - 1)
    t_c = q0 + lax.broadcasted_iota(jnp.int32, (n_cmp, QB), 1)
    s_c = s_c + tile4(jnp.where(e_c <= t_c, 0.0, NEG))
    m_c = jnp.maximum(jnp.max(s_c, axis=0, keepdims=True), 0.1 * NEG)
    p_c = jnp.exp(s_c - m_c)
    l_c = jnp.sum(p_c, axis=0, keepdims=True)
    p_c = p_c * jnp.where(l_c > 0.0, 1.0 / l_c, 0.0)
    o_c = jnp.dot(vct[...], p_c.astype(BF), preferred_element_type=F32)

    psum = p_c[:, 0:QB]
    for r in range(1, R):
        psum = psum + p_c[:, r * QB:(r + 1) * QB]
    ovt = ov_ref[...]
    p_hi = psum.astype(BF)
    rem = psum - p_hi.astype(F32)
    p_mid = rem.astype(BF)
    p_lo = (rem - p_mid.astype(F32)).astype(BF)
    imp = (jnp.dot(ovt, p_hi, preferred_element_type=F32)
           + jnp.dot(ovt, p_mid, preferred_element_type=F32)
           + jnp.dot(ovt, p_lo, preferred_element_type=F32))

    NR = ovt.shape[0]
    SUB = 8
    jblk = lax.broadcasted_iota(jnp.int32, (NR, QB), 0)
    t_q = q0 + lax.broadcasted_iota(jnp.int32, (NR, QB), 1)
    cur = t_q >> 6
    forced = (jblk == 0) | (jblk == cur) | (jblk == cur - 1)
    score = jnp.where(jblk > cur, NEG, jnp.where(forced, -NEG, imp))
    n_slab = -(-n_slc // SUB)
    isub = lax.broadcasted_iota(jnp.int32, (SUB, QB), 0)

    def rank_counts(ns):
        def fn():
            slabs = [score[a * SUB:(a + 1) * SUB, :] for a in range(ns)]
            cnts = [jnp.zeros((SUB, QB), F32) for _ in range(ns)]
            for jp in range(min(ns * SUB, n_slc)):
                rowv = jnp.broadcast_to(score[jp:jp + 1, :], (SUB, QB))
                for a in range(ns):
                    if a < jp // SUB:
                        beats = jnp.where(rowv > slabs[a], 1.0, 0.0)
                    elif a > jp // SUB:
                        beats = jnp.where(rowv >= slabs[a], 1.0, 0.0)
                    else:
                        beats = jnp.where(isub > jp % SUB, jnp.where(rowv >= slabs[a], 1.0, 0.0),
                                          jnp.where(rowv > slabs[a], 1.0, 0.0))
                    cnts[a] = cnts[a] + beats
            rest = NR - ns * SUB
            return jnp.concatenate(cnts + [jnp.full((rest, QB), float(NR), F32)] * (rest > 0), axis=0)
        return fn

    last_blk = (q0 + QB - 1) >> 6
    cnt = lax.switch(last_blk // SUB, [rank_counts(ns) for ns in range(1, n_slab + 1)])
    sel = (cnt < float(N_SEL)) & (jblk <= cur) & (jblk < n_slc)
    a_nat = jnp.where(jblk == 0, 1.0, jnp.where(sel, (SLC_LEN * jblk).astype(F32), NEG))
    a_nat = jnp.where(jblk < n_slc, a_nat, 0.0)
    a_t = pltpu.roll(a_nat, NR - 1, 0)
    a_pad = jnp.concatenate([jnp.zeros((LANES - NR, QB), F32), a_t], axis=0)

    any_q = jnp.max(jnp.where(sel, 1.0, 0.0), axis=1, keepdims=True)
    jcol = lax.broadcasted_iota(jnp.int32, (NR, 1), 0)
    bits = jnp.where(any_q > 0.0, lax.shift_left(jnp.int32(1), jcol & 31), 0)
    word0 = jnp.sum(jnp.where(jcol < 32, bits, 0))
    word1 = jnp.sum(jnp.where(jcol >= 32, bits, 0))

    qs = jnp.concatenate([(q_t[r] + a_pad * slopes[r]).astype(BF) for r in range(R)], axis=1)

    tiles_per = KT // LANES
    blocks_per = KT // SLC_LEN
    n_full = q0 // KT

    def scan(ti, n):
        word = jnp.where(ti < 32 // blocks_per, word0, word1)
        hit = (lax.shift_right_logical(word, (ti * blocks_per) & 31) & ((1 << blocks_per) - 1)) != 0
        act_ref[n] = ti
        return n + jnp.where(hit, 1, 0)

    n_act = lax.fori_loop(0, n_full, scan, 0)
    for d in range(QB // KT):
        act_ref[n_act + d] = n_full + d

    kw = kwa[pl.ds(pl.multiple_of(q0, QB), WK), :]
    s_w = jnp.dot(kw, qw, preferred_element_type=F32)
    c_w = lax.broadcasted_iota(jnp.int32, (WK, QB), 0)
    d_w = c_w - lax.broadcasted_iota(jnp.int32, (WK, QB), 1)
    mask_w = (d_w > 0) & (d_w <= WINDOW) & (c_w >= WINDOW - q0)
    s_w = s_w + tile4(jnp.where(mask_w, 0.0, NEG))
    m_w = jnp.max(s_w, axis=0, keepdims=True)
    p_w = jnp.exp(s_w - m_w).astype(BF)

    def scores(ti):
        k0 = pl.multiple_of(ti * KT, KT)
        return jnp.dot(ksa[pl.ds(k0, KT), :], qs, preferred_element_type=F32)

    def absorb(s, ti, m_prev, diag):
        if diag:
            p_d = ti * KT + lax.broadcasted_iota(jnp.int32, (KT, QB), 0)
            t_d = q0 + lax.broadcasted_iota(jnp.int32, (KT, QB), 1)
            s = s + tile4(jnp.where(p_d <= t_d, 0.0, NEG))
        m_new = jnp.maximum(m_prev, jnp.max(s, axis=0, keepdims=True))
        alpha = jnp.exp(m_prev - m_new)
        p = jnp.exp(s - m_new).astype(BF)
        v_t = jnp.concatenate([vst[ti * tiles_per + c] for c in range(tiles_per)], axis=1)
        acc_scr[...] = acc_scr[...] * alpha + jnp.dot(v_t, p, preferred_element_type=F32)
        return m_new

    acc_scr[...] = jnp.zeros(acc_scr.shape, F32)

    def body(i, carry):
        s, m_prev = carry
        s_next = scores(act_ref[i + 1])
        return s_next, absorb(s, act_ref[i], m_prev, False)

    s_d, m_d = lax.fori_loop(0, n_act, body, (scores(act_ref[0]), jnp.full((1, R * QB), NEG, F32)))
    for d in range(QB // KT):
        s_next = scores(n_full + d + 1) if d + 1 < QB // KT else None
        m_d = absorb(s_d, n_full + d, m_d, True)
        s_d = s_next
    acc_s = acc_scr[...]

    vw_t = jnp.concatenate([vwt[qi * (QB // LANES) + c] for c in range(WK // LANES)], axis=1)
    acc_w = jnp.dot(vw_t, p_w, preferred_element_type=F32)


    sg_t = jax.nn.sigmoid(gt_ref[0]).T
    for r in range(R):
        cs = slice(r * QB, (r + 1) * QB)
        a_s = acc_s[:, cs]
        a_w = acc_w[:, cs]
        g_c = sg_t[3 * r:3 * r + 1, :]
        g_s = sg_t[3 * r + 1:3 * r + 2, :] / a_s[NSA_DH:NSA_DH + 1, :]
        g_w = sg_t[3 * r + 2:3 * r + 3, :] / a_w[NSA_DH:NSA_DH + 1, :]
        out_t = g_c * o_c[:, cs] + g_s * a_s + g_w * a_w
        out_t = jnp.concatenate([out_t, jnp.zeros((LANES - VROWS, QB), F32)], axis=0)
        o_ref[0, :, r * LANES:(r + 1) * LANES] = out_t.T.astype(BF)


def _nsa_attend(proj3, small3, kc, vc, slope_tab, ovt):
    B, S, _ = proj3.shape
    G, R = NSA_GROUPS, NSA_REP
    n_cmp = kc.shape[2]
    kern = functools.partial(_nsa_kernel, seq=S)
    kv_spec = lambda off: pl.BlockSpec((1, S, LANES), lambda b, g, i: (b, 0, off // LANES + g))
    return pl.pallas_call(
        kern,
        grid=(B, G, S // QB),
        in_specs=[
            pl.BlockSpec((1, QB, R * LANES), lambda b, g, i: (b, i, OFF_NQ // (R * LANES) + g)),
            kv_spec(OFF_KS), kv_spec(OFF_VS), kv_spec(OFF_KW), kv_spec(OFF_VW),
            pl.BlockSpec((1, 1, n_cmp, LANES), lambda b, g, i: (b, g, 0, 0)),
            pl.BlockSpec((1, 1, n_cmp, LANES), lambda b, g, i: (b, g, 0, 0)),
            pl.BlockSpec((1, QB, LANES), lambda b, g, i: (b, i, 1 + g)),
            pl.BlockSpec((1, 8, LANES), lambda b, g, i: (g, 0, 0)),
            _resident(ovt.shape, lambda b, g, i: (0, 0)),
        ],
        out_specs=pl.BlockSpec((1, QB, R * LANES), lambda b, g, i: (b, i, g)),
        out_shape=jax.ShapeDtypeStruct((B, S, G * R * LANES), BF),
        scratch_shapes=[
            pltpu.VMEM((S, LANES), BF),
            pltpu.VMEM((S + WINDOW, LANES), BF),
            pltpu.VMEM((S // LANES, VROWS, LANES), BF),
            pltpu.VMEM(((S + WINDOW) // LANES, VROWS, LANES), BF),
            pltpu.VMEM((VROWS, n_cmp), BF),
            pltpu.VMEM((VROWS, R * QB), F32),
            pltpu.SMEM((S // SLC_TILE + 1,), jnp.int32),
        ],
        compiler_params=_params(("parallel", "parallel", "arbitrary")),
    )(proj3, proj3, proj3, proj3, proj3, kc, vc, small3, slope_tab, ovt)


def _merge_kernel(og_ref, on_ref, mg_ref, mn_ref, x_ref, wg_ref, wn_ref, wo_ref, g_ref, o_ref):
    a = jnp.dot(og_ref[...], wg_ref[...], preferred_element_type=F32)
    b = jnp.dot(on_ref[...], wn_ref[...], preferred_element_type=F32)
    mixed = jax.nn.sigmoid(mg_ref[...].astype(F32)) * a + jax.nn.sigmoid(mn_ref[...].astype(F32)) * b
    y = jnp.dot(mixed.astype(BF), wo_ref[...], preferred_element_type=F32)
    y = y * lax.rsqrt(jnp.mean(y * y, axis=-1, keepdims=True) + EPS) * g_ref[...]
    o_ref[...] = x_ref[...] + y


def _merge_out(o_gla2, o_nsa2, proj2, x2, wg, wn, wo, g, tm=512):
    n_tok = x2.shape[0]
    D = D_MODEL
    return pl.pallas_call(
        _merge_kernel,
        grid=(n_tok // tm,),
        in_specs=[
            pl.BlockSpec((tm, D), lambda i: (i, 0)),
            pl.BlockSpec((tm, D), lambda i: (i, 0)),
            pl.BlockSpec((tm, D), lambda i: (i, OFF_MG // D)),
            pl.BlockSpec((tm, D), lambda i: (i, OFF_MN // D)),
            pl.BlockSpec((tm, D), lambda i: (i, 0)),
            _resident(wg.shape, lambda i: (0, 0)),
            _resident(wn.shape, lambda i: (0, 0)),
            _resident(wo.shape, lambda i: (0, 0)),
            _resident((1, D), lambda i: (0, 0)),
        ],
        out_specs=pl.BlockSpec((tm, D), lambda i: (i, 0)),
        out_shape=jax.ShapeDtypeStruct((n_tok, D), F32),
        compiler_params=_params(("parallel",)),
    )(o_gla2, o_nsa2, proj2, proj2, x2, wg, wn, wo, g)


def _ffn_kernel(x_ref, gpre_ref, wg_ref, wu_ref, wd_ref, gpost_ref, o_ref, acc_scr, *, chunk):
    x = x_ref[...]
    h = (x * lax.rsqrt(jnp.mean(x * x, axis=-1, keepdims=True) + EPS) * gpre_ref[...]).astype(BF)
    d_ff = wg_ref.shape[1]
    for n, c0 in enumerate(range(0, d_ff, chunk)):
        c1 = min(c0 + chunk, d_ff)
        a = jnp.dot(h, wg_ref[:, c0:c1], preferred_element_type=F32)
        u = jnp.dot(h, wu_ref[:, c0:c1], preferred_element_type=F32)
        t = (a * jax.nn.sigmoid(a) * u).astype(BF)
        part = jnp.dot(t, wd_ref[c0:c1, :], preferred_element_type=F32)
        if n == 0:
            acc_scr[...] = part
        else:
            acc_scr[...] += part
    f = acc_scr[...]
    o_ref[...] = x + f * lax.rsqrt(jnp.mean(f * f, axis=-1, keepdims=True) + EPS) * gpost_ref[...]


def _ffn(x2, gpre, wg, wu, wd, gpost, tm=512, chunk=512):
    n_tok = x2.shape[0]
    D = D_MODEL
    kern = functools.partial(_ffn_kernel, chunk=chunk)
    return pl.pallas_call(
        kern,
        grid=(n_tok // tm,),
        in_specs=[
            pl.BlockSpec((tm, D), lambda i: (i, 0)),
            _resident((1, D), lambda i: (0, 0)),
            _resident(wg.shape, lambda i: (0, 0)),
            _resident(wu.shape, lambda i: (0, 0)),
            _resident(wd.shape, lambda i: (0, 0)),
            _resident((1, D), lambda i: (0, 0)),
        ],
        out_specs=pl.BlockSpec((tm, D), lambda i: (i, 0)),
        out_shape=jax.ShapeDtypeStruct((n_tok, D), F32),
        scratch_shapes=[pltpu.VMEM((tm, D), F32)],
        compiler_params=_params(("parallel",)),
    )(x2, gpre, wg, wu, wd, gpost)


def _pad_heads(w, n_heads, scale=1.0):
    d = w.shape[0]
    w = (w * scale).reshape(d, n_heads, NSA_DH)
    return jnp.pad(w, ((0, 0), (0, 0), (0, LANES - NSA_DH))).reshape(d, n_heads * LANES)


def _prep_in_weights(w_in):
    splits = np.cumsum([512, 512, 1024, 1024, GLA_RANK, 512, 128, 128, 128, 128, 128, 128, 24, 1024])
    (g_q, g_k, g_v, g_r, g_a, n_q, n_kc, n_vc, n_ks, n_vs, n_kw, n_vw, n_gate, m_g, m_n) = jnp.split(
        w_in, [int(s) for s in splits], axis=1)
    w_main = jnp.concatenate([
        g_q, g_k, g_v, g_r,
        _pad_heads(n_q, NSA_HEADS, NSA_DH ** -0.5),
        m_g, m_n,
        n_kc, n_vc,
        _pad_heads(n_ks, NSA_GROUPS), _pad_heads(n_vs, NSA_GROUPS),
        _pad_heads(n_kw, NSA_GROUPS), _pad_heads(n_vw, NSA_GROUPS)], axis=1).astype(BF)
    d = w_in.shape[0]
    per_g = NSA_REP * 3
    gates = jnp.pad(n_gate.reshape(d, NSA_GROUPS, per_g), ((0, 0), (0, 0), (0, LANES - per_g)))
    g_a3 = jnp.pad(jnp.concatenate([g_a, g_a, g_a], axis=1), ((0, 0), (0, LANES - 3 * GLA_RANK)))
    w_small = jnp.concatenate([g_a3, gates.reshape(d, NSA_GROUPS * LANES)], axis=1).astype(BF)
    return w_main, w_small


def _prep_compress(pe, w1, w2):
    eye = jnp.eye(NSA_GROUPS, dtype=F32)
    w1r = w1.reshape(CMP_LEN, NSA_DH, NSA_DH)
    w1e = jnp.einsum('lde,gh->lgdhe', w1r, eye).reshape(CMP_LEN * NSA_GROUPS * NSA_DH, NSA_GROUPS * NSA_DH)
    pe_e = jnp.broadcast_to(pe[:, None, :], (CMP_LEN, NSA_GROUPS, NSA_DH)).reshape(2, CMP_STRIDE * LANES)
    pe_e = jnp.pad(pe_e, ((0, 6), (0, 0)))
    w2e = jnp.stack([
        jnp.pad(jnp.pad(w2, ((g * NSA_DH, (NSA_GROUPS - 1 - g) * NSA_DH), (0, 0))), ((0, 0), (0, LANES - NSA_DH)))
        for g in range(NSA_GROUPS)])
    return pe_e.astype(F32), w1e.astype(BF), w2e.astype(BF)


def _overlap_table(seq):
    n_cmp = (seq - CMP_LEN) // CMP_STRIDE + 1
    n_slc = seq // SLC_LEN
    sc = CMP_STRIDE * np.arange(n_cmp)
    ss = SLC_LEN * np.arange(n_slc)
    ov = np.clip(np.minimum(sc[:, None] + CMP_LEN, ss[None, :] + SLC_LEN)
                 - np.maximum(sc[:, None], ss[None, :]), 0, None).astype(np.float32) / CMP_LEN
    ovt = np.zeros((NSA_DH, n_cmp + 1), np.float32)
    ovt[:n_slc, :n_cmp] = ov.T
    return jnp.asarray(ovt, dtype=BF)


def kernel(x, norm_mix_pre, norm_mix_post, norm_ffn_pre, norm_ffn_post, w_in, gla_w_alpha2, gla_b_alpha, gla_norm_g, nsa_cmp_pe_k, nsa_cmp_w1_k, nsa_cmp_w2_k, nsa_cmp_pe_v, nsa_cmp_w1_v, nsa_cmp_w2_v, w_proj_gla, w_proj_nsa, w_out, w_ffn_gate, w_ffn_up, w_ffn_down):
    B, S, D = x.shape
    depth = w_in.shape[0]
    n_tok = B * S
    h_idx = jnp.arange(NSA_HEADS, dtype=F32)
    slopes = jnp.exp2(-8.0 * (h_idx + 1.0) / NSA_HEADS).reshape(NSA_GROUPS, NSA_REP, 1)
    slope_tab = jnp.broadcast_to(jnp.pad(slopes, ((0, 0), (0, 8 - NSA_REP), (0, 0))), (NSA_GROUPS, 8, LANES))
    ovt = _overlap_table(S)
    x2 = x.reshape(n_tok, D)
    for l in range(depth):
        w_main, w_small = _prep_in_weights(w_in[l])
        proj2, small2 = _in_proj(x2, norm_mix_pre[l][None, :], w_main, w_small)
        proj3 = proj2.reshape(B, S, N_MAIN)
        small3 = small2.reshape(B, S, N_SMALL)

        w2_hi = gla_w_alpha2[l].astype(BF)
        w2_lo = (gla_w_alpha2[l] - w2_hi.astype(F32)).astype(BF)
        w2p = jnp.pad(jnp.concatenate([w2_hi, w2_hi, w2_lo], axis=0), ((0, LANES - 3 * GLA_RANK), (0, 0)))
        o_gla = _gla(proj3, small3, w2p, gla_b_alpha[l][None, :], gla_norm_g[l][None, :])

        xk = proj3[:, :, OFF_KC:OFF_KC + LANES].reshape(B, S // CMP_STRIDE, CMP_STRIDE * LANES)
        xv = proj3[:, :, OFF_VC:OFF_VC + LANES].reshape(B, S // CMP_STRIDE, CMP_STRIDE * LANES)
        pek, w1k, w2k = _prep_compress(nsa_cmp_pe_k[l], nsa_cmp_w1_k[l], nsa_cmp_w2_k[l])
        pev, w1v, w2v = _prep_compress(nsa_cmp_pe_v[l], nsa_cmp_w1_v[l], nsa_cmp_w2_v[l])
        kc, vc = _nsa_compress(xk, xv, pek, pev, w1k, w1v, w2k, w2v)
        o_nsa = _nsa_attend(proj3, small3, kc, vc, slope_tab, ovt)

        wn = jnp.pad(w_proj_nsa[l].reshape(NSA_HEADS, NSA_DH, D), ((0, 0), (0, LANES - NSA_DH), (0, 0)))
        x2 = _merge_out(o_gla.reshape(n_tok, -1), o_nsa.reshape(n_tok, -1), proj2, x2,
                        w_proj_gla[l].astype(BF), wn.reshape(NSA_HEADS * LANES, D).astype(BF),
                        w_out[l].astype(BF), norm_mix_post[l][None, :])
        x2 = _ffn(x2, norm_ffn_pre[l][None, :], w_ffn_gate[l].astype(BF), w_ffn_up[l].astype(BF),
                  w_ffn_down[l].astype(BF), norm_ffn_post[l][None, :])
    return x2.reshape(B, S, D)
```

```python
import functools

import numpy as np
import jax
import jax.numpy as jnp
from jax import lax
from jax.experimental import pallas as pl
from jax.experimental.pallas import tpu as pltpu

D_MODEL = 1024
GLA_HEADS = 4
GLA_DK = 128
GLA_DV = 256
GLA_RANK = 16
GLA_TAU = 16.0
GLA_CHUNK = 64
NSA_HEADS = 8
NSA_GROUPS = 2
NSA_REP = 4
NSA_DH = 64
CMP_LEN = 32
CMP_STRIDE = 16
SLC_LEN = 64
N_SEL = 16
WINDOW = 512
QB = 256
SLC_TILE = 256
VROWS = 80
D_FF = 2816
EPS = 1e-6
NEG = -1e30

LANES = 128
VMEM_LIMIT = 56 * 1024 * 1024
BF = jnp.bfloat16
F32 = jnp.float32

OFF_GQ = 0
OFF_GK = 512
OFF_GV = 1024
OFF_GR = 2048
OFF_MG = 3072
OFF_MN = 4096
OFF_NQ = 5120
OFF_KC = 5632
OFF_VC = 5760
OFF_KS = 5888
OFF_VS = 6016
OFF_KW = 6144
OFF_VW = 6272
N_MAIN = 6400
N_SMALL = 384

NT = (((1,), (1,)), ((), ()))
TN = (((0,), (0,)), ((), ()))


def _resident(shape, index_map):
    return pl.BlockSpec(shape, index_map, pipeline_mode=pl.Buffered(1))


def _params(sem):
    return pltpu.CompilerParams(dimension_semantics=sem, vmem_limit_bytes=VMEM_LIMIT)


def _in_proj_kernel(x_ref, g_ref, wm_ref, ws_ref, om_ref, os_ref):
    x = x_ref[...]
    h = (x * lax.rsqrt(jnp.mean(x * x, axis=-1, keepdims=True) + EPS) * g_ref[...]).astype(BF)
    os_ref[...] = jnp.dot(h, ws_ref[...], preferred_element_type=F32)
    n_out = om_ref.shape[1]
    step = 512
    for c0 in range(0, n_out, step):
        c1 = min(c0 + step, n_out)
        om_ref[:, c0:c1] = jnp.dot(h, wm_ref[:, c0:c1], preferred_element_type=F32).astype(BF)


def _in_proj(x2, g, w_main, w_small, tm=512):
    n_tok = x2.shape[0]
    return pl.pallas_call(
        _in_proj_kernel,
        grid=(n_tok // tm,),
        in_specs=[
            pl.BlockSpec((tm, D_MODEL), lambda i: (i, 0)),
            _resident((1, D_MODEL), lambda i: (0, 0)),
            _resident((D_MODEL, N_MAIN), lambda i: (0, 0)),
            _resident((D_MODEL, N_SMALL), lambda i: (0, 0)),
        ],
        out_specs=[
            pl.BlockSpec((tm, N_MAIN), lambda i: (i, 0)),
            pl.BlockSpec((tm, N_SMALL), lambda i: (i, 0)),
        ],
        out_shape=[
            jax.ShapeDtypeStruct((n_tok, N_MAIN), BF),
            jax.ShapeDtypeStruct((n_tok, N_SMALL), F32),
        ],
        compiler_params=_params(("parallel",)),
    )(x2, g, w_main, w_small)


def _gla_kernel(q_ref, k_ref, v_ref, r_ref, a_ref, w2_ref, b2_ref, ng_ref, o_ref, st_scr, *, n_chunks):
    blk = pl.program_id(2)

    @pl.when(blk == 0)
    def _():
        st_scr[...] = jnp.zeros_like(st_scr)

    C = GLA_CHUNK
    T = n_chunks * C

    def split3(x):
        hi = x.astype(BF)
        rem = x - hi.astype(F32)
        mid = rem.astype(BF)
        return hi, mid, (rem - mid.astype(F32)).astype(BF)

    a = a_ref[0]
    a_hi = a.astype(BF)
    a_lo = (a - a_hi.astype(F32)).astype(BF)
    lane = lax.broadcasted_iota(jnp.int32, (T, LANES), 1)
    in_lo = (lane >= GLA_RANK) & (lane < 2 * GLA_RANK)
    z = jnp.dot(jnp.where(in_lo, a_lo, a_hi), w2_ref[...], preferred_element_type=F32) + b2_ref[...]
    log_a = (jnp.minimum(z, 0.0) - jnp.log1p(jnp.exp(-jnp.abs(z)))) * (1.0 / GLA_TAU)

    x_wide = jnp.concatenate([log_a[c * C:(c + 1) * C] for c in range(n_chunks)], axis=1)
    x3 = jnp.concatenate(split3(x_wide), axis=0)
    r3 = lax.broadcasted_iota(jnp.int32, (C, 3 * C), 0)
    c3 = lax.broadcasted_iota(jnp.int32, (C, 3 * C), 1) & (C - 1)
    tri3 = jnp.where(c3 <= r3, 1.0, 0.0).astype(BF)
    b_wide = jnp.dot(tri3, x3, preferred_element_type=F32)
    bcum = jnp.concatenate([b_wide[:, c * LANES:(c + 1) * LANES] for c in range(n_chunks)], axis=0)
    last_rows = [b_wide[C - 1:C, c * LANES:(c + 1) * LANES] for c in range(n_chunks)]
    b_last = jnp.concatenate([jnp.broadcast_to(lr, (C, LANES)) for lr in last_rows], axis=0)

    q = q_ref[0].astype(F32)
    k = k_ref[0].astype(F32)
    v = v_ref[0]
    qe = (q * ((GLA_DK ** -0.5) * jnp.exp(bcum))).astype(BF)
    ke = (k * jnp.exp(-bcum)).astype(BF)
    kd = (k * jnp.exp(b_last - bcum)).astype(BF)

    H = min(T, 4 * C)
    row = lax.broadcasted_iota(jnp.int32, (H, H), 0)
    col = lax.broadcasted_iota(jnp.int32, (H, H), 1)
    keep = (col <= row) & ((col >> 6) == (row >> 6))
    intra = []
    for h0 in range(0, T, H):
        hs = slice(h0, h0 + H)
        attn = lax.dot_general(qe[hs], ke[hs], NT, preferred_element_type=F32)
        intra.append(jnp.dot(jnp.where(keep, attn, 0.0).astype(BF), v[hs], preferred_element_type=F32))
    o = jnp.concatenate(intra, axis=0)

    st = st_scr[...]
    inter = []
    for c in range(n_chunks):
        sl = slice(c * C, (c + 1) * C)
        inter.append(lax.dot_general(qe[sl], st.astype(BF), NT, preferred_element_type=F32))
        upd = lax.dot_general(v[sl], kd[sl], TN, preferred_element_type=F32)
        st = st * jnp.exp(last_rows[c]) + upd
    st_scr[...] = st
    o = o + jnp.concatenate(inter, axis=0)

    o = o * lax.rsqrt(jnp.mean(o * o, axis=-1, keepdims=True) + EPS) * ng_ref[...]
    r = r_ref[0].astype(F32)
    o_ref[0] = (o * (r * jax.nn.sigmoid(r))).astype(BF)


def _gla(proj3, small3, w2p, b2, ng, blk_tokens=512):
    B, S, _ = proj3.shape
    nblk = S // blk_tokens
    kern = functools.partial(_gla_kernel, n_chunks=blk_tokens // GLA_CHUNK)
    return pl.pallas_call(
        kern,
        grid=(B, GLA_HEADS, nblk),
        in_specs=[
            pl.BlockSpec((1, blk_tokens, GLA_DK), lambda b, h, i: (b, i, OFF_GQ // GLA_DK + h)),
            pl.BlockSpec((1, blk_tokens, GLA_DK), lambda b, h, i: (b, i, OFF_GK // GLA_DK + h)),
            pl.BlockSpec((1, blk_tokens, GLA_DV), lambda b, h, i: (b, i, OFF_GV // GLA_DV + h)),
            pl.BlockSpec((1, blk_tokens, GLA_DV), lambda b, h, i: (b, i, OFF_GR // GLA_DV + h)),
            pl.BlockSpec((1, blk_tokens, LANES), lambda b, h, i: (b, i, 0)),
            pl.BlockSpec((LANES, GLA_DK), lambda b, h, i: (0, h)),
            pl.BlockSpec((1, GLA_DK), lambda b, h, i: (0, h)),
            pl.BlockSpec((1, GLA_DV), lambda b, h, i: (0, 0)),
        ],
        out_specs=pl.BlockSpec((1, blk_tokens, GLA_DV), lambda b, h, i: (b, i, h)),
        out_shape=jax.ShapeDtypeStruct((B, S, GLA_HEADS * GLA_DV), BF),
        scratch_shapes=[pltpu.VMEM((GLA_DV, GLA_DK), F32)],
        compiler_params=_params(("parallel", "parallel", "arbitrary")),
    )(proj3, proj3, proj3, proj3, small3, w2p, b2, ng)


def _compress_kernel(xk_ref, xv_ref, pek_ref, pev_ref, w1k_ref, w1v_ref, w2k_ref, w2v_ref, kc_ref, vc_ref):
    n_rows = xk_ref.shape[1]
    half = CMP_STRIDE * LANES
    lane = lax.broadcasted_iota(jnp.int32, (n_rows, LANES), 1)
    row = lax.broadcasted_iota(jnp.int32, (n_rows, LANES), 0)
    end_c = CMP_STRIDE * row + (CMP_LEN - 1)
    c_k = jnp.where(lane == NSA_DH, end_c >> 6, jnp.where(lane == NSA_DH + 1, end_c & 63, 0)).astype(F32)
    c_v = jnp.where(lane == NSA_DH, 1.0, 0.0).astype(F32)

    def branch(x_ref, pe_ref, w1_ref, w2_ref, const, o_ref):
        x = x_ref[0].astype(F32)
        xa = (x + pe_ref[0:1, :]).astype(BF)
        xb = (x + pe_ref[1:2, :]).astype(BF)
        a = jnp.dot(xa, w1_ref[0:half, :], preferred_element_type=F32)
        b = jnp.dot(xb, w1_ref[half:2 * half, :], preferred_element_type=F32)
        pre = a + pltpu.roll(b, n_rows - 1, 0)
        hid = (pre * jax.nn.sigmoid(pre)).astype(BF)
        for g in range(NSA_GROUPS):
            o_ref[0, g] = (jnp.dot(hid, w2_ref[g], preferred_element_type=F32) + const).astype(BF)

    branch(xk_ref, pek_ref, w1k_ref, w2k_ref, c_k, kc_ref)
    branch(xv_ref, pev_ref, w1v_ref, w2v_ref, c_v, vc_ref)


def _nsa_compress(xk, xv, pek, pev, w1k, w1v, w2k, w2v):
    B, n_rows, width = xk.shape
    full = lambda shape: _resident(shape, lambda b: (0,) * len(shape))
    out = jax.ShapeDtypeStruct((B, NSA_GROUPS, n_rows, LANES), BF)
    return pl.pallas_call(
        _compress_kernel,
        grid=(B,),
        in_specs=[
            pl.BlockSpec((1, n_rows, width), lambda b: (b, 0, 0)),
            pl.BlockSpec((1, n_rows, width), lambda b: (b, 0, 0)),
            full(pek.shape), full(pev.shape), full(w1k.shape), full(w1v.shape),
            full(w2k.shape), full(w2v.shape),
        ],
        out_specs=[pl.BlockSpec((1, NSA_GROUPS, n_rows, LANES), lambda b: (b, 0, 0, 0))] * 2,
        out_shape=[out, out],
        compiler_params=_params(("parallel",)),
    )(xk, xv, pek, pev, w1k, w1v, w2k, w2v)


def _nsa_kernel(q_ref, ks_ref, vs_ref, kw_ref, vw_ref, kc_ref, vc_ref, gt_ref, sl_ref, ov_ref,
                o_ref, ksa, kwa, vst, vwt, vct, acc_scr, act_ref, *, seq):
    qi = pl.program_id(2)
    n_slc = seq // SLC_LEN
    R = NSA_REP
    KT = SLC_TILE
    WK = WINDOW + QB
    n_cmp = kc_ref.shape[2]

    def t_bf(x):
        return x.astype(F32).T[0:VROWS].astype(BF)

    @pl.when(qi == 0)
    def _():
        grp = pl.program_id(1)
        lane = lax.broadcasted_iota(jnp.int32, (seq, LANES), 1)
        pos = lax.broadcasted_iota(jnp.int32, (seq, LANES), 0)
        blk = pos >> 6
        off = pos & 63
        r_i = lax.broadcasted_iota(jnp.int32, (LANES, LANES), 0)
        c_i = lax.broadcasted_iota(jnp.int32, (LANES, LANES), 1)
        pick = jnp.where((c_i < NSA_DH) & (r_i == c_i + grp * NSA_DH), 1.0, 0.0).astype(BF)
        c_s = jnp.where(lane == LANES - 1, off,
                        jnp.where((lane >= NSA_DH) & (lane - (NSA_DH - 1) == blk), 1, 0))
        ksa[...] = (jnp.dot(ks_ref[0], pick, preferred_element_type=F32) + c_s.astype(F32)).astype(BF)
        c_w = jnp.where(lane == NSA_DH, blk, jnp.where(lane == NSA_DH + 1, off, 0))
        kwa[0:WINDOW, :] = jnp.zeros((WINDOW, LANES), BF)
        kwa[WINDOW:WINDOW + seq, :] = (jnp.dot(kw_ref[0], pick, preferred_element_type=F32)
                                       + c_w.astype(F32)).astype(BF)
        ones_rows = jnp.where(lax.broadcasted_iota(jnp.int32, (VROWS - NSA_DH, LANES), 0) == 0, 1.0, 0.0)
        n_pad = WINDOW // LANES
        for c in range(n_pad):
            vwt[c] = jnp.zeros((VROWS, LANES), BF)

        def v_tile(x):
            xt = x.astype(F32).T
            dims = jnp.where(grp == 0, xt[0:NSA_DH], xt[NSA_DH:2 * NSA_DH])
            return jnp.concatenate([dims, ones_rows], axis=0).astype(BF)

        def fill(c, carry):
            rows = pl.ds(pl.multiple_of(c * LANES, LANES), LANES)
            vst[c] = v_tile(vs_ref[0, rows, :])
            vwt[c + n_pad] = v_tile(vw_ref[0, rows, :])
            return carry

        lax.fori_loop(0, seq // LANES, fill, 0)
        for c in range(n_cmp // LANES):
            vct[:, c * LANES:(c + 1) * LANES] = t_bf(vc_ref[0, 0, c * LANES:(c + 1) * LANES, :])

    q0 = qi * QB
    rowi = lax.broadcasted_iota(jnp.int32, (NSA_DH, QB), 0)
    slopes = [jnp.concatenate([sl_ref[0, r:r + 1, :]] * (QB // LANES), axis=1) for r in range(R)]
    q_all = q_ref[0].astype(F32).T
    q_t = [q_all[r * NSA_DH:(r + 1) * NSA_DH] for r in range(R)]

    qw = jnp.concatenate(
        [jnp.concatenate([q_t[r], jnp.where(rowi == 0, slopes[r] * 64.0, jnp.where(rowi == 1, slopes[r], 0.0))],
                         axis=0).astype(BF) for r in range(R)], axis=1)

    def tile4(x):
        return jnp.concatenate([x] * R, axis=1)

    s_c = jnp.dot(kc_ref[0, 0], qw, preferred_element_type=F32)
    e_c = CMP_STRIDE * lax.broadcasted_iota(jnp.int32, (n_cmp, QB), 0) + (CMP_LEN - 1)
    t_c = q0 + lax.broadcasted_iota(jnp.int32, (n_cmp, QB), 1)
    s_c = s_c + tile4(jnp.where(e_c <= t_c, 0.0, NEG))
    m_c = jnp.maximum(jnp.max(s_c, axis=0, keepdims=True), 0.1 * NEG)
    p_c = jnp.exp(s_c - m_c)
    l_c = jnp.sum(p_c, axis=0, keepdims=True)
    p_c = p_c * jnp.where(l_c > 0.0, 1.0 / l_c, 0.0)
    o_c = jnp.dot(vct[...], p_c.astype(BF), preferred_element_type=F32)

    psum = p_c[:, 0:QB]
    for r in range(1, R):
        psum = psum + p_c[:, r * QB:(r + 1) * QB]
    ovt = ov_ref[...]
    p_hi = psum.astype(BF)
    rem = psum - p_hi.astype(F32)
    p_mid = rem.astype(BF)
    p_lo = (rem - p_mid.astype(F32)).astype(BF)
    imp = (jnp.dot(ovt, p_hi, preferred_element_type=F32)
           + jnp.dot(ovt, p_mid, preferred_element_type=F32)
           + jnp.dot(ovt, p_lo, preferred_element_type=F32))

    NR = ovt.shape[0]
    SUB = 8
    jblk = lax.broadcasted_iota(jnp.int32, (NR, QB), 0)
    t_q = q0 + lax.broadcasted_iota(jnp.int32, (NR, QB), 1)
    cur = t_q >> 6
    forced = (jblk == 0) | (jblk == cur) | (jblk == cur - 1)
    score = jnp.where(jblk > cur, NEG, jnp.where(forced, -NEG, imp))
    n_slab = -(-n_slc // SUB)
    isub = lax.broadcasted_iota(jnp.int32, (SUB, QB), 0)

    def rank_counts(ns):
        def fn():
            slabs = [score[a * SUB:(a + 1) * SUB, :] for a in range(ns)]
            cnts = [jnp.zeros((SUB, QB), F32) for _ in range(ns)]
            for jp in range(min(ns * SUB, n_slc)):
                rowv = jnp.broadcast_to(score[jp:jp + 1, :], (SUB, QB))
                for a in range(ns):
                    if a < jp // SUB:
                        beats = jnp.where(rowv > slabs[a], 1.0, 0.0)
                    elif a > jp // SUB:
                        beats = jnp.where(rowv >= slabs[a], 1.0, 0.0)
                    else:
                        beats = jnp.where(isub > jp % SUB, jnp.where(rowv >= slabs[a], 1.0, 0.0),
                                          jnp.where(rowv > slabs[a], 1.0, 0.0))
                    cnts[a] = cnts[a] + beats
            rest = NR - ns * SUB
            return jnp.concatenate(cnts + [jnp.full((rest, QB), float(NR), F32)] * (rest > 0), axis=0)
        return fn

    last_blk = (q0 + QB - 1) >> 6
    cnt = lax.switch(last_blk // SUB, [rank_counts(ns) for ns in range(1, n_slab + 1)])
    sel = (cnt < float(N_SEL)) & (jblk <= cur) & (jblk < n_slc)
    a_nat = jnp.where(jblk == 0, 1.0, jnp.where(sel, (SLC_LEN * jblk).astype(F32), NEG))
    a_nat = jnp.where(jblk < n_slc, a_nat, 0.0)
    a_t = pltpu.roll(a_nat, NR - 1, 0)

    any_q = jnp.max(jnp.where(sel, 1.0, 0.0), axis=1, keepdims=True)
    jcol = lax.broadcasted_iota(jnp.int32, (NR, 1), 0)
    bits = jnp.where(any_q > 0.0, lax.shift_left(jnp.int32(1), jcol & 31), 0)
    word0 = jnp.sum(jnp.where(jcol < 32, bits, 0))
    word1 = jnp.sum(jnp.where(jcol >= 32, bits, 0))

    qs = jnp.concatenate([jnp.concatenate([q_t[r], a_t * slopes[r]], axis=0).astype(BF) for r in range(R)],
                         axis=1)

    tiles_per = KT // LANES
    blocks_per = KT // SLC_LEN
    n_full = q0 // KT

    def scan(ti, n):
        word = jnp.where(ti < 32 // blocks_per, word0, word1)
        hit = (lax.shift_right_logical(word, (ti * blocks_per) & 31) & ((1 << blocks_per) - 1)) != 0
        act_ref[n] = ti
        return n + jnp.where(hit, 1, 0)

    n_act = lax.fori_loop(0, n_full, scan, 0)
    for d in range(QB // KT):
        act_ref[n_act + d] = n_full + d

    kw = kwa[pl.ds(pl.multiple_of(q0, QB), WK), :]
    s_w = jnp.dot(kw, qw, preferred_element_type=F32)
    c_w = lax.broadcasted_iota(jnp.int32, (WK, QB), 0)
    d_w = c_w - lax.broadcasted_iota(jnp.int32, (WK, QB), 1)
    mask_w = (d_w > 0) & (d_w <= WINDOW) & (c_w >= WINDOW - q0)
    s_w = s_w + tile4(jnp.where(mask_w, 0.0, NEG))
    m_w = jnp.max(s_w, axis=0, keepdims=True)
    p_w = jnp.exp(s_w - m_w).astype(BF)

    def scores(ti):
        k0 = pl.multiple_of(ti * KT, KT)
        return jnp.dot(ksa[pl.ds(k0, KT), :], qs, preferred_element_type=F32)

    def absorb(s, ti, m_prev, diag):
        if diag:
            p_d = ti * KT + lax.broadcasted_iota(jnp.int32, (KT, QB), 0)
            t_d = q0 + lax.broadcasted_iota(jnp.int32, (KT, QB), 1)
            s = s + tile4(jnp.where(p_d <= t_d, 0.0, NEG))
        m_new = jnp.maximum(m_prev, jnp.max(s, axis=0, keepdims=True))
        alpha = jnp.exp(m_prev - m_new)
        p = jnp.exp(s - m_new).astype(BF)
        v_t = jnp.concatenate([vst[ti * tiles_per + c] for c in range(tiles_per)], axis=1)
        acc_scr[...] = acc_scr[...] * alpha + jnp.dot(v_t, p, preferred_element_type=F32)
        return m_new

    acc_scr[...] = jnp.zeros(acc_scr.shape, F32)

    def body(i, carry):
        s, m_prev = carry
        s_next = scores(act_ref[i + 1])
        return s_next, absorb(s, act_ref[i], m_prev, False)

    s_d, m_d = lax.fori_loop(0, n_act, body, (scores(act_ref[0]), jnp.full((1, R * QB), NEG, F32)))
    for d in range(QB // KT):
        s_next = scores(n_full + d + 1) if d + 1 < QB // KT else None
        m_d = absorb(s_d, n_full + d, m_d, True)
        s_d = s_next
    acc_s = acc_scr[...]

    vw_t = jnp.concatenate([vwt[qi * (QB // LANES) + c] for c in range(WK // LANES)], axis=1)
    acc_w = jnp.dot(vw_t, p_w, preferred_element_type=F32)


    sg_t = jax.nn.sigmoid(gt_ref[0]).T
    outs = []
    for r in range(R):
        cs = slice(r * QB, (r + 1) * QB)
        a_s = acc_s[:, cs]
        a_w = acc_w[:, cs]
        g_c = sg_t[3 * r:3 * r + 1, :]
        g_s = sg_t[3 * r + 1:3 * r + 2, :] / a_s[NSA_DH:NSA_DH + 1, :]
        g_w = sg_t[3 * r + 2:3 * r + 3, :] / a_w[NSA_DH:NSA_DH + 1, :]
        outs.append((g_c * o_c[:, cs] + g_s * a_s + g_w * a_w)[0:NSA_DH])
    o_ref[0] = jnp.concatenate(outs, axis=0).T.astype(BF)


def _nsa_attend(proj3, small3, kc, vc, slope_tab, ovt):
    B, S, _ = proj3.shape
    G, R = NSA_GROUPS, NSA_REP
    n_cmp = kc.shape[2]
    kern = functools.partial(_nsa_kernel, seq=S)
    kv_spec = lambda off: pl.BlockSpec((1, S, LANES), lambda b, g, i: (b, 0, off // LANES))
    return pl.pallas_call(
        kern,
        grid=(B, G, S // QB),
        in_specs=[
            pl.BlockSpec((1, QB, R * NSA_DH), lambda b, g, i: (b, i, OFF_NQ // (R * NSA_DH) + g)),
            kv_spec(OFF_KS), kv_spec(OFF_VS), kv_spec(OFF_KW), kv_spec(OFF_VW),
            pl.BlockSpec((1, 1, n_cmp, LANES), lambda b, g, i: (b, g, 0, 0)),
            pl.BlockSpec((1, 1, n_cmp, LANES), lambda b, g, i: (b, g, 0, 0)),
            pl.BlockSpec((1, QB, LANES), lambda b, g, i: (b, i, 1 + g)),
            pl.BlockSpec((1, 8, LANES), lambda b, g, i: (g, 0, 0)),
            _resident(ovt.shape, lambda b, g, i: (0, 0)),
        ],
        out_specs=pl.BlockSpec((1, QB, R * NSA_DH), lambda b, g, i: (b, i, g)),
        out_shape=jax.ShapeDtypeStruct((B, S, G * R * NSA_DH), BF),
        scratch_shapes=[
            pltpu.VMEM((S, LANES), BF),
            pltpu.VMEM((S + WINDOW, LANES), BF),
            pltpu.VMEM((S // LANES, VROWS, LANES), BF),
            pltpu.VMEM(((S + WINDOW) // LANES, VROWS, LANES), BF),
            pltpu.VMEM((VROWS, n_cmp), BF),
            pltpu.VMEM((VROWS, R * QB), F32),
            pltpu.SMEM((S // SLC_TILE + 1,), jnp.int32),
        ],
        compiler_params=_params(("parallel", "parallel", "arbitrary")),
    )(proj3, proj3, proj3, proj3, proj3, kc, vc, small3, slope_tab, ovt)


def _merge_kernel(og_ref, on_ref, mg_ref, mn_ref, x_ref, wg_ref, wn_ref, wo_ref, g_ref, o_ref):
    a = jnp.dot(og_ref[...], wg_ref[...], preferred_element_type=F32)
    b = jnp.dot(on_ref[...], wn_ref[...], preferred_element_type=F32)
    mixed = jax.nn.sigmoid(mg_ref[...].astype(F32)) * a + jax.nn.sigmoid(mn_ref[...].astype(F32)) * b
    y = jnp.dot(mixed.astype(BF), wo_ref[...], preferred_element_type=F32)
    y = y * lax.rsqrt(jnp.mean(y * y, axis=-1, keepdims=True) + EPS) * g_ref[...]
    o_ref[...] = x_ref[...] + y


def _merge_out(o_gla2, o_nsa2, proj2, x2, wg, wn, wo, g, tm=512):
    n_tok = x2.shape[0]
    D = D_MODEL
    return pl.pallas_call(
        _merge_kernel,
        grid=(n_tok // tm,),
        in_specs=[
            pl.BlockSpec((tm, D), lambda i: (i, 0)),
            pl.BlockSpec((tm, o_nsa2.shape[1]), lambda i: (i, 0)),
            pl.BlockSpec((tm, D), lambda i: (i, OFF_MG // D)),
            pl.BlockSpec((tm, D), lambda i: (i, OFF_MN // D)),
            pl.BlockSpec((tm, D), lambda i: (i, 0)),
            _resident(wg.shape, lambda i: (0, 0)),
            _resident(wn.shape, lambda i: (0, 0)),
            _resident(wo.shape, lambda i: (0, 0)),
            _resident((1, D), lambda i: (0, 0)),
        ],
        out_specs=pl.BlockSpec((tm, D), lambda i: (i, 0)),
        out_shape=jax.ShapeDtypeStruct((n_tok, D), F32),
        compiler_params=_params(("parallel",)),
    )(o_gla2, o_nsa2, proj2, proj2, x2, wg, wn, wo, g)


def _ffn_kernel(x_ref, gpre_ref, wg_ref, wu_ref, wd_ref, gpost_ref, o_ref, acc_scr, *, chunk):
    x = x_ref[...]
    h = (x * lax.rsqrt(jnp.mean(x * x, axis=-1, keepdims=True) + EPS) * gpre_ref[...]).astype(BF)
    d_ff = wg_ref.shape[1]
    for n, c0 in enumerate(range(0, d_ff, chunk)):
        c1 = min(c0 + chunk, d_ff)
        a = jnp.dot(h, wg_ref[:, c0:c1], preferred_element_type=F32)
        u = jnp.dot(h, wu_ref[:, c0:c1], preferred_element_type=F32)
        t = (a * jax.nn.sigmoid(a) * u).astype(BF)
        part = jnp.dot(t, wd_ref[c0:c1, :], preferred_element_type=F32)
        if n == 0:
            acc_scr[...] = part
        else:
            acc_scr[...] += part
    f = acc_scr[...]
    o_ref[...] = x + f * lax.rsqrt(jnp.mean(f * f, axis=-1, keepdims=True) + EPS) * gpost_ref[...]


def _ffn(x2, gpre, wg, wu, wd, gpost, tm=512, chunk=512):
    n_tok = x2.shape[0]
    D = D_MODEL
    kern = functools.partial(_ffn_kernel, chunk=chunk)
    return pl.pallas_call(
        kern,
        grid=(n_tok // tm,),
        in_specs=[
            pl.BlockSpec((tm, D), lambda i: (i, 0)),
            _resident((1, D), lambda i: (0, 0)),
            _resident(wg.shape, lambda i: (0, 0)),
            _resident(wu.shape, lambda i: (0, 0)),
            _resident(wd.shape, lambda i: (0, 0)),
            _resident((1, D), lambda i: (0, 0)),
        ],
        out_specs=pl.BlockSpec((tm, D), lambda i: (i, 0)),
        out_shape=jax.ShapeDtypeStruct((n_tok, D), F32),
        scratch_shapes=[pltpu.VMEM((tm, D), F32)],
        compiler_params=_params(("parallel",)),
    )(x2, gpre, wg, wu, wd, gpost)


def _prep_in_weights(w_in):
    splits = np.cumsum([512, 512, 1024, 1024, GLA_RANK, 512, 128, 128, 128, 128, 128, 128, 24, 1024])
    (g_q, g_k, g_v, g_r, g_a, n_q, n_kc, n_vc, n_ks, n_vs, n_kw, n_vw, n_gate, m_g, m_n) = jnp.split(
        w_in, [int(s) for s in splits], axis=1)
    w_main = jnp.concatenate([g_q, g_k, g_v, g_r, m_g, m_n, n_q * (NSA_DH ** -0.5),
                              n_kc, n_vc, n_ks, n_vs, n_kw, n_vw], axis=1).astype(BF)
    d = w_in.shape[0]
    per_g = NSA_REP * 3
    gates = jnp.pad(n_gate.reshape(d, NSA_GROUPS, per_g), ((0, 0), (0, 0), (0, LANES - per_g)))
    g_a3 = jnp.pad(jnp.concatenate([g_a, g_a, g_a], axis=1), ((0, 0), (0, LANES - 3 * GLA_RANK)))
    w_small = jnp.concatenate([g_a3, gates.reshape(d, NSA_GROUPS * LANES)], axis=1).astype(BF)
    return w_main, w_small


def _prep_compress(pe, w1, w2):
    eye = jnp.eye(NSA_GROUPS, dtype=F32)
    w1r = w1.reshape(CMP_LEN, NSA_DH, NSA_DH)
    w1e = jnp.einsum('lde,gh->lgdhe', w1r, eye).reshape(CMP_LEN * NSA_GROUPS * NSA_DH, NSA_GROUPS * NSA_DH)
    pe_e = jnp.broadcast_to(pe[:, None, :], (CMP_LEN, NSA_GROUPS, NSA_DH)).reshape(2, CMP_STRIDE * LANES)
    pe_e = jnp.pad(pe_e, ((0, 6), (0, 0)))
    w2e = jnp.stack([
        jnp.pad(jnp.pad(w2, ((g * NSA_DH, (NSA_GROUPS - 1 - g) * NSA_DH), (0, 0))), ((0, 0), (0, LANES - NSA_DH)))
        for g in range(NSA_GROUPS)])
    return pe_e.astype(F32), w1e.astype(BF), w2e.astype(BF)


def _overlap_table(seq):
    n_cmp = (seq - CMP_LEN) // CMP_STRIDE + 1
    n_slc = seq // SLC_LEN
    sc = CMP_STRIDE * np.arange(n_cmp)
    ss = SLC_LEN * np.arange(n_slc)
    ov = np.clip(np.minimum(sc[:, None] + CMP_LEN, ss[None, :] + SLC_LEN)
                 - np.maximum(sc[:, None], ss[None, :]), 0, None).astype(np.float32) / CMP_LEN
    ovt = np.zeros((NSA_DH, n_cmp + 1), np.float32)
    ovt[:n_slc, :n_cmp] = ov.T
    return jnp.asarray(ovt, dtype=BF)


def kernel(x, norm_mix_pre, norm_mix_post, norm_ffn_pre, norm_ffn_post, w_in, gla_w_alpha2, gla_b_alpha, gla_norm_g, nsa_cmp_pe_k, nsa_cmp_w1_k, nsa_cmp_w2_k, nsa_cmp_pe_v, nsa_cmp_w1_v, nsa_cmp_w2_v, w_proj_gla, w_proj_nsa, w_out, w_ffn_gate, w_ffn_up, w_ffn_down):
    B, S, D = x.shape
    depth = w_in.shape[0]
    n_tok = B * S
    h_idx = jnp.arange(NSA_HEADS, dtype=F32)
    slopes = jnp.exp2(-8.0 * (h_idx + 1.0) / NSA_HEADS).reshape(NSA_GROUPS, NSA_REP, 1)
    slope_tab = jnp.broadcast_to(jnp.pad(slopes, ((0, 0), (0, 8 - NSA_REP), (0, 0))), (NSA_GROUPS, 8, LANES))
    ovt = _overlap_table(S)
    x2 = x.reshape(n_tok, D)
    for l in range(depth):
        w_main, w_small = _prep_in_weights(w_in[l])
        proj2, small2 = _in_proj(x2, norm_mix_pre[l][None, :], w_main, w_small)
        proj3 = proj2.reshape(B, S, N_MAIN)
        small3 = small2.reshape(B, S, N_SMALL)

        w2_hi = gla_w_alpha2[l].astype(BF)
        w2_lo = (gla_w_alpha2[l] - w2_hi.astype(F32)).astype(BF)
        w2p = jnp.pad(jnp.concatenate([w2_hi, w2_hi, w2_lo], axis=0), ((0, LANES - 3 * GLA_RANK), (0, 0)))
        o_gla = _gla(proj3, small3, w2p, gla_b_alpha[l][None, :], gla_norm_g[l][None, :])

        xk = proj3[:, :, OFF_KC:OFF_KC + LANES].reshape(B, S // CMP_STRIDE, CMP_STRIDE * LANES)
        xv = proj3[:, :, OFF_VC:OFF_VC + LANES].reshape(B, S // CMP_STRIDE, CMP_STRIDE * LANES)
        pek, w1k, w2k = _prep_compress(nsa_cmp_pe_k[l], nsa_cmp_w1_k[l], nsa_cmp_w2_k[l])
        pev, w1v, w2v = _prep_compress(nsa_cmp_pe_v[l], nsa_cmp_w1_v[l], nsa_cmp_w2_v[l])
        kc, vc = _nsa_compress(xk, xv, pek, pev, w1k, w1v, w2k, w2v)
        o_nsa = _nsa_attend(proj3, small3, kc, vc, slope_tab, ovt)

        x2 = _merge_out(o_gla.reshape(n_tok, -1), o_nsa.reshape(n_tok, -1), proj2, x2,
                        w_proj_gla[l].astype(BF), w_proj_nsa[l].astype(BF),
                        w_out[l].astype(BF), norm_mix_post[l][None, :])
        x2 = _ffn(x2, norm_ffn_pre[l][None, :], w_ffn_gate[l].astype(BF), w_ffn_up[l].astype(BF),
                  w_ffn_down[l].astype(BF), norm_ffn_post[l][None, :])
    return x2.reshape(B, S, D)
```

```python
import functools

import numpy as np
import jax
import jax.numpy as jnp
from jax import lax
from jax.experimental import pallas as pl
from jax.experimental.pallas import tpu as pltpu

D_MODEL = 1024
GLA_HEADS = 4
GLA_DK = 128
GLA_DV = 256
GLA_RANK = 16
GLA_TAU = 16.0
GLA_CHUNK = 64
NSA_HEADS = 8
NSA_GROUPS = 2
NSA_REP = 4
NSA_DH = 64
CMP_LEN = 32
CMP_STRIDE = 16
SLC_LEN = 64
N_SEL = 16
WINDOW = 512
QB = 256
SLC_TILE = 256
VROWS = 80
D_FF = 2816
EPS = 1e-6
NEG = -1e30

LANES = 128
VMEM_LIMIT = 56 * 1024 * 1024
BF = jnp.bfloat16
F32 = jnp.float32

OFF_GQ = 0
OFF_GK = 512
OFF_GV = 1024
OFF_GR = 2048
OFF_MG = 3072
OFF_MN = 4096
OFF_NQ = 5120
OFF_KC = 5632
OFF_VC = 5760
OFF_KS = 5888
OFF_VS = 6016
OFF_KW = 6144
OFF_VW = 6272
N_MAIN = 6400
N_SMALL = 384

NT = (((1,), (1,)), ((), ()))
TN = (((0,), (0,)), ((), ()))


def _resident(shape, index_map):
    return pl.BlockSpec(shape, index_map, pipeline_mode=pl.Buffered(1))


def _params(sem):
    return pltpu.CompilerParams(dimension_semantics=sem, vmem_limit_bytes=VMEM_LIMIT)


def _in_proj_kernel(x_ref, g_ref, wm_ref, ws_ref, om_ref, os_ref):
    x = x_ref[...]
    h = (x * lax.rsqrt(jnp.mean(x * x, axis=-1, keepdims=True) + EPS) * g_ref[...]).astype(BF)
    os_ref[...] = jnp.dot(h, ws_ref[...], preferred_element_type=F32)
    n_out = om_ref.shape[1]
    step = 512
    for c0 in range(0, n_out, step):
        c1 = min(c0 + step, n_out)
        om_ref[:, c0:c1] = jnp.dot(h, wm_ref[:, c0:c1], preferred_element_type=F32).astype(BF)


def _in_proj(x2, g, w_main, w_small, tm=512):
    n_tok = x2.shape[0]
    return pl.pallas_call(
        _in_proj_kernel,
        grid=(n_tok // tm,),
        in_specs=[
            pl.BlockSpec((tm, D_MODEL), lambda i: (i, 0)),
            _resident((1, D_MODEL), lambda i: (0, 0)),
            _resident((D_MODEL, N_MAIN), lambda i: (0, 0)),
            _resident((D_MODEL, N_SMALL), lambda i: (0, 0)),
        ],
        out_specs=[
            pl.BlockSpec((tm, N_MAIN), lambda i: (i, 0)),
            pl.BlockSpec((tm, N_SMALL), lambda i: (i, 0)),
        ],
        out_shape=[
            jax.ShapeDtypeStruct((n_tok, N_MAIN), BF),
            jax.ShapeDtypeStruct((n_tok, N_SMALL), F32),
        ],
        compiler_params=_params(("parallel",)),
    )(x2, g, w_main, w_small)


def _gla_kernel(q_ref, k_ref, v_ref, r_ref, a_ref, w2_ref, b2_ref, ng_ref, o_ref, st_scr, *, n_chunks, n_heads):
    blk = pl.program_id(2)

    @pl.when(blk == 0)
    def _():
        st_scr[...] = jnp.zeros_like(st_scr)

    C = GLA_CHUNK
    T = n_chunks * C
    W = n_heads * GLA_DK

    def split3(x):
        hi = x.astype(BF)
        rem = x - hi.astype(F32)
        mid = rem.astype(BF)
        return hi, mid, (rem - mid.astype(F32)).astype(BF)

    a = a_ref[0]
    a_hi = a.astype(BF)
    a_lo = (a - a_hi.astype(F32)).astype(BF)
    lane = lax.broadcasted_iota(jnp.int32, (T, LANES), 1)
    in_lo = (lane >= GLA_RANK) & (lane < 2 * GLA_RANK)
    z = jnp.dot(jnp.where(in_lo, a_lo, a_hi), w2_ref[...], preferred_element_type=F32) + b2_ref[...]
    log_a = (jnp.minimum(z, 0.0) - jnp.log1p(jnp.exp(-jnp.abs(z)))) * (1.0 / GLA_TAU)

    x_wide = jnp.concatenate([log_a[c * C:(c + 1) * C] for c in range(n_chunks)], axis=1)
    x3 = jnp.concatenate(split3(x_wide), axis=0)
    r3 = lax.broadcasted_iota(jnp.int32, (C, 3 * C), 0)
    c3 = lax.broadcasted_iota(jnp.int32, (C, 3 * C), 1) & (C - 1)
    tri3 = jnp.where(c3 <= r3, 1.0, 0.0).astype(BF)
    b_wide = jnp.dot(tri3, x3, preferred_element_type=F32)
    bcum = jnp.concatenate([b_wide[:, c * W:(c + 1) * W] for c in range(n_chunks)], axis=0)
    last_rows = [b_wide[C - 1:C, c * W:(c + 1) * W] for c in range(n_chunks)]
    b_last = jnp.concatenate([jnp.broadcast_to(lr, (C, W)) for lr in last_rows], axis=0)
    decay = [jnp.exp(lr) for lr in last_rows]

    q = q_ref[0].astype(F32)
    k = k_ref[0].astype(F32)
    v = v_ref[0]
    qe = (q * ((GLA_DK ** -0.5) * jnp.exp(bcum))).astype(BF)
    ke = (k * jnp.exp(-bcum)).astype(BF)
    kd = (k * jnp.exp(b_last - bcum)).astype(BF)

    H = min(T, 4 * C)
    row = lax.broadcasted_iota(jnp.int32, (H, H), 0)
    col = lax.broadcasted_iota(jnp.int32, (H, H), 1)
    keep = (col <= row) & ((col >> 6) == (row >> 6))
    ng = ng_ref[...]
    r_all = r_ref[0].astype(F32)
    for hh in range(n_heads):
        ks_ = slice(hh * GLA_DK, (hh + 1) * GLA_DK)
        vs_ = slice(hh * GLA_DV, (hh + 1) * GLA_DV)
        intra = []
        for h0 in range(0, T, H):
            hs = slice(h0, h0 + H)
            attn = lax.dot_general(qe[hs, ks_], ke[hs, ks_], NT, preferred_element_type=F32)
            intra.append(jnp.dot(jnp.where(keep, attn, 0.0).astype(BF), v[hs, vs_], preferred_element_type=F32))
        o = jnp.concatenate(intra, axis=0)

        st = st_scr[hh]
        inter = []
        for c in range(n_chunks):
            sl = slice(c * C, (c + 1) * C)
            inter.append(lax.dot_general(qe[sl, ks_], st.astype(BF), NT, preferred_element_type=F32))
            upd = lax.dot_general(v[sl, vs_], kd[sl, ks_], TN, preferred_element_type=F32)
            st = st * decay[c][:, ks_] + upd
        st_scr[hh] = st
        o = o + jnp.concatenate(inter, axis=0)

        o = o * lax.rsqrt(jnp.mean(o * o, axis=-1, keepdims=True) + EPS) * ng
        r = r_all[:, vs_]
        o_ref[0, :, vs_] = (o * (r * jax.nn.sigmoid(r))).astype(BF)


def _gla(proj3, small3, w2p, b2, ng, blk_tokens=512, heads_per_step=4):
    B, S, _ = proj3.shape
    nblk = S // blk_tokens
    hb = heads_per_step
    kern = functools.partial(_gla_kernel, n_chunks=blk_tokens // GLA_CHUNK, n_heads=hb)
    wk, wv = hb * GLA_DK, hb * GLA_DV
    return pl.pallas_call(
        kern,
        grid=(B, GLA_HEADS // hb, nblk),
        in_specs=[
            pl.BlockSpec((1, blk_tokens, wk), lambda b, h, i: (b, i, OFF_GQ // wk + h)),
            pl.BlockSpec((1, blk_tokens, wk), lambda b, h, i: (b, i, OFF_GK // wk + h)),
            pl.BlockSpec((1, blk_tokens, wv), lambda b, h, i: (b, i, OFF_GV // wv + h)),
            pl.BlockSpec((1, blk_tokens, wv), lambda b, h, i: (b, i, OFF_GR // wv + h)),
            pl.BlockSpec((1, blk_tokens, LANES), lambda b, h, i: (b, i, 0)),
            pl.BlockSpec((LANES, wk), lambda b, h, i: (0, h)),
            pl.BlockSpec((1, wk), lambda b, h, i: (0, h)),
            pl.BlockSpec((1, GLA_DV), lambda b, h, i: (0, 0)),
        ],
        out_specs=pl.BlockSpec((1, blk_tokens, wv), lambda b, h, i: (b, i, h)),
        out_shape=jax.ShapeDtypeStruct((B, S, GLA_HEADS * GLA_DV), BF),
        scratch_shapes=[pltpu.VMEM((hb, GLA_DV, GLA_DK), F32)],
        compiler_params=_params(("parallel", "parallel", "arbitrary")),
    )(proj3, proj3, proj3, proj3, small3, w2p, b2, ng)


def _compress_kernel(xk_ref, xv_ref, pek_ref, pev_ref, w1k_ref, w1v_ref, w2k_ref, w2v_ref, kc_ref, vc_ref):
    n_rows = xk_ref.shape[1]
    half = CMP_STRIDE * LANES
    lane = lax.broadcasted_iota(jnp.int32, (n_rows, LANES), 1)
    row = lax.broadcasted_iota(jnp.int32, (n_rows, LANES), 0)
    end_c = CMP_STRIDE * row + (CMP_LEN - 1)
    c_k = jnp.where(lane == NSA_DH, end_c >> 6, jnp.where(lane == NSA_DH + 1, end_c & 63, 0)).astype(F32)
    c_v = jnp.where(lane == NSA_DH, 1.0, 0.0).astype(F32)

    def branch(x_ref, pe_ref, w1_ref, w2_ref, const, o_ref):
        x = x_ref[0].astype(F32)
        xa = (x + pe_ref[0:1, :]).astype(BF)
        xb = (x + pe_ref[1:2, :]).astype(BF)
        a = jnp.dot(xa, w1_ref[0:half, :], preferred_element_type=F32)
        b = jnp.dot(xb, w1_ref[half:2 * half, :], preferred_element_type=F32)
        pre = a + pltpu.roll(b, n_rows - 1, 0)
        hid = (pre * jax.nn.sigmoid(pre)).astype(BF)
        for g in range(NSA_GROUPS):
            o_ref[0, g] = (jnp.dot(hid, w2_ref[g], preferred_element_type=F32) + const).astype(BF)

    branch(xk_ref, pek_ref, w1k_ref, w2k_ref, c_k, kc_ref)
    branch(xv_ref, pev_ref, w1v_ref, w2v_ref, c_v, vc_ref)


def _nsa_compress(xk, xv, pek, pev, w1k, w1v, w2k, w2v):
    B, n_rows, width = xk.shape
    full = lambda shape: _resident(shape, lambda b: (0,) * len(shape))
    out = jax.ShapeDtypeStruct((B, NSA_GROUPS, n_rows, LANES), BF)
    return pl.pallas_call(
        _compress_kernel,
        grid=(B,),
        in_specs=[
            pl.BlockSpec((1, n_rows, width), lambda b: (b, 0, 0)),
            pl.BlockSpec((1, n_rows, width), lambda b: (b, 0, 0)),
            full(pek.shape), full(pev.shape), full(w1k.shape), full(w1v.shape),
            full(w2k.shape), full(w2v.shape),
        ],
        out_specs=[pl.BlockSpec((1, NSA_GROUPS, n_rows, LANES), lambda b: (b, 0, 0, 0))] * 2,
        out_shape=[out, out],
        compiler_params=_params(("parallel",)),
    )(xk, xv, pek, pev, w1k, w1v, w2k, w2v)


def _nsa_kernel(q_ref, ks_ref, vs_ref, kw_ref, vw_ref, kc_ref, vc_ref, gt_ref, sl_ref, ov_ref,
                o_ref, ksa, kwa, vst, vwt, vct, acc_scr, act_ref, *, seq):
    qi = pl.program_id(2)
    n_slc = seq // SLC_LEN
    R = NSA_REP
    KT = SLC_TILE
    WK = WINDOW + QB
    n_cmp = kc_ref.shape[2]

    def t_bf(x):
        return x.astype(F32).T[0:VROWS].astype(BF)

    @pl.when(qi == 0)
    def _():
        grp = pl.program_id(1)
        lane = lax.broadcasted_iota(jnp.int32, (seq, LANES), 1)
        pos = lax.broadcasted_iota(jnp.int32, (seq, LANES), 0)
        blk = pos >> 6
        off = pos & 63
        r_i = lax.broadcasted_iota(jnp.int32, (LANES, LANES), 0)
        c_i = lax.broadcasted_iota(jnp.int32, (LANES, LANES), 1)
        pick = jnp.where((c_i < NSA_DH) & (r_i == c_i + grp * NSA_DH), 1.0, 0.0).astype(BF)
        c_s = jnp.where(lane == LANES - 1, off,
                        jnp.where((lane >= NSA_DH) & (lane - (NSA_DH - 1) == blk), 1, 0))
        ksa[...] = (jnp.dot(ks_ref[0], pick, preferred_element_type=F32) + c_s.astype(F32)).astype(BF)
        c_w = jnp.where(lane == NSA_DH, blk, jnp.where(lane == NSA_DH + 1, off, 0))
        kwa[0:WINDOW, :] = jnp.zeros((WINDOW, LANES), BF)
        kwa[WINDOW:WINDOW + seq, :] = (jnp.dot(kw_ref[0], pick, preferred_element_type=F32)
                                       + c_w.astype(F32)).astype(BF)
        ones_rows = jnp.where(lax.broadcasted_iota(jnp.int32, (VROWS - NSA_DH, LANES), 0) == 0, 1.0, 0.0)
        n_pad = WINDOW // LANES
        for c in range(n_pad):
            vwt[c] = jnp.zeros((VROWS, LANES), BF)

        def v_tile(x):
            xt = x.astype(F32).T
            dims = jnp.where(grp == 0, xt[0:NSA_DH], xt[NSA_DH:2 * NSA_DH])
            return jnp.concatenate([dims, ones_rows], axis=0).astype(BF)

        def fill(c, carry):
            rows = pl.ds(pl.multiple_of(c * LANES, LANES), LANES)
            vst[c] = v_tile(vs_ref[0, rows, :])
            vwt[c + n_pad] = v_tile(vw_ref[0, rows, :])
            return carry

        lax.fori_loop(0, seq // LANES, fill, 0)
        for c in range(n_cmp // LANES):
            vct[:, c * LANES:(c + 1) * LANES] = t_bf(vc_ref[0, 0, c * LANES:(c + 1) * LANES, :])

    q0 = qi * QB
    rowi = lax.broadcasted_iota(jnp.int32, (NSA_DH, QB), 0)
    slopes = [jnp.concatenate([sl_ref[0, r:r + 1, :]] * (QB // LANES), axis=1) for r in range(R)]
    q_all = q_ref[0].astype(F32).T
    q_t = [q_all[r * NSA_DH:(r + 1) * NSA_DH] for r in range(R)]

    qw = jnp.concatenate(
        [jnp.concatenate([q_t[r], jnp.where(rowi == 0, slopes[r] * 64.0, jnp.where(rowi == 1, slopes[r], 0.0))],
                         axis=0).astype(BF) for r in range(R)], axis=1)

    def tile4(x):
        return jnp.concatenate([x] * R, axis=1)

    s_c = jnp.dot(kc_ref[0, 0], qw, preferred_element_type=F32)
    e_c = CMP_STRIDE * lax.broadcasted_iota(jnp.int32, (n_cmp, QB), 0) + (CMP_LEN - 1)
    t_c = q0 + lax.broadcasted_iota(jnp.int32, (n_cmp, QB), 1)
    s_c = s_c + tile4(jnp.where(e_c <= t_c, 0.0, NEG))
    m_c = jnp.maximum(jnp.max(s_c, axis=0, keepdims=True), 0.1 * NEG)
    p_c = jnp.exp(s_c - m_c)
    l_c = jnp.sum(p_c, axis=0, keepdims=True)
    p_c = p_c * jnp.where(l_c > 0.0, 1.0 / l_c, 0.0)
    o_c = jnp.dot(vct[...], p_c.astype(BF), preferred_element_type=F32)

    psum = p_c[:, 0:QB]
    for r in range(1, R):
        psum = psum + p_c[:, r * QB:(r + 1) * QB]
    ovt = ov_ref[...]
    p_hi = psum.astype(BF)
    rem = psum - p_hi.astype(F32)
    p_mid = rem.astype(BF)
    p_lo = (rem - p_mid.astype(F32)).astype(BF)
    imp = (jnp.dot(ovt, p_hi, preferred_element_type=F32)
           + jnp.dot(ovt, p_mid, preferred_element_type=F32)
           + jnp.dot(ovt, p_lo, preferred_element_type=F32))

    NR = ovt.shape[0]
    SUB = 8
    jblk = lax.broadcasted_iota(jnp.int32, (NR, QB), 0)
    t_q = q0 + lax.broadcasted_iota(jnp.int32, (NR, QB), 1)
    cur = t_q >> 6
    forced = (jblk == 0) | (jblk == cur) | (jblk == cur - 1)
    score = jnp.where(jblk > cur, NEG, jnp.where(forced, -NEG, imp))
    n_slab = -(-n_slc // SUB)
    isub = lax.broadcasted_iota(jnp.int32, (SUB, QB), 0)

    def rank_counts(ns):
        def fn():
            slabs = [score[a * SUB:(a + 1) * SUB, :] for a in range(ns)]
            cnts = [jnp.zeros((SUB, QB), F32) for _ in range(ns)]
            for jp in range(min(ns * SUB, n_slc)):
                rowv = jnp.broadcast_to(score[jp:jp + 1, :], (SUB, QB))
                for a in range(ns):
                    if a < jp // SUB:
                        beats = jnp.where(rowv > slabs[a], 1.0, 0.0)
                    elif a > jp // SUB:
                        beats = jnp.where(rowv >= slabs[a], 1.0, 0.0)
                    else:
                        beats = jnp.where(isub > jp % SUB, jnp.where(rowv >= slabs[a], 1.0, 0.0),
                                          jnp.where(rowv > slabs[a], 1.0, 0.0))
                    cnts[a] = cnts[a] + beats
            rest = NR - ns * SUB
            return jnp.concatenate(cnts + [jnp.full((rest, QB), float(NR), F32)] * (rest > 0), axis=0)
        return fn

    last_blk = (q0 + QB - 1) >> 6
    cnt = lax.switch(last_blk // SUB, [rank_counts(ns) for ns in range(1, n_slab + 1)])
    sel = (cnt < float(N_SEL)) & (jblk <= cur) & (jblk < n_slc)
    a_nat = jnp.where(jblk == 0, 1.0, jnp.where(sel, (SLC_LEN * jblk).astype(F32), NEG))
    a_nat = jnp.where(jblk < n_slc, a_nat, 0.0)
    a_t = pltpu.roll(a_nat, NR - 1, 0)

    any_q = jnp.max(jnp.where(sel, 1.0, 0.0), axis=1, keepdims=True)
    jcol = lax.broadcasted_iota(jnp.int32, (NR, 1), 0)
    bits = jnp.where(any_q > 0.0, lax.shift_left(jnp.int32(1), jcol & 31), 0)
    word0 = jnp.sum(jnp.where(jcol < 32, bits, 0))
    word1 = jnp.sum(jnp.where(jcol >= 32, bits, 0))

    qs = jnp.concatenate([jnp.concatenate([q_t[r], a_t * slopes[r]], axis=0).astype(BF) for r in range(R)],
                         axis=1)

    tiles_per = KT // LANES
    blocks_per = KT // SLC_LEN
    n_full = q0 // KT

    def scan(ti, n):
        word = jnp.where(ti < 32 // blocks_per, word0, word1)
        hit = (lax.shift_right_logical(word, (ti * blocks_per) & 31) & ((1 << blocks_per) - 1)) != 0
        act_ref[n] = ti
        return n + jnp.where(hit, 1, 0)

    n_act = lax.fori_loop(0, n_full, scan, 0)
    for d in range(QB // KT):
        act_ref[n_act + d] = n_full + d

    kw = kwa[pl.ds(pl.multiple_of(q0, QB), WK), :]
    s_w = jnp.dot(kw, qw, preferred_element_type=F32)
    c_w = lax.broadcasted_iota(jnp.int32, (WK, QB), 0)
    d_w = c_w - lax.broadcasted_iota(jnp.int32, (WK, QB), 1)
    mask_w = (d_w > 0) & (d_w <= WINDOW) & (c_w >= WINDOW - q0)
    s_w = s_w + tile4(jnp.where(mask_w, 0.0, NEG))
    m_w = jnp.max(s_w, axis=0, keepdims=True)
    p_w = jnp.exp(s_w - m_w).astype(BF)

    def scores(ti):
        k0 = pl.multiple_of(ti * KT, KT)
        return jnp.dot(ksa[pl.ds(k0, KT), :], qs, preferred_element_type=F32)

    def absorb(s, ti, m_prev, diag):
        if diag:
            p_d = ti * KT + lax.broadcasted_iota(jnp.int32, (KT, QB), 0)
            t_d = q0 + lax.broadcasted_iota(jnp.int32, (KT, QB), 1)
            s = s + tile4(jnp.where(p_d <= t_d, 0.0, NEG))
        m_new = jnp.maximum(m_prev, jnp.max(s, axis=0, keepdims=True))
        alpha = jnp.exp(m_prev - m_new)
        p = jnp.exp(s - m_new).astype(BF)
        v_t = jnp.concatenate([vst[ti * tiles_per + c] for c in range(tiles_per)], axis=1)
        acc_scr[...] = acc_scr[...] * alpha + jnp.dot(v_t, p, preferred_element_type=F32)
        return m_new

    acc_scr[...] = jnp.zeros(acc_scr.shape, F32)

    def body(i, carry):
        s, m_prev = carry
        s_next = scores(act_ref[i + 1])
        return s_next, absorb(s, act_ref[i], m_prev, False)

    s_d, m_d = lax.fori_loop(0, n_act, body, (scores(act_ref[0]), jnp.full((1, R * QB), NEG, F32)))
    for d in range(QB // KT):
        s_next = scores(n_full + d + 1) if d + 1 < QB // KT else None
        m_d = absorb(s_d, n_full + d, m_d, True)
        s_d = s_next
    acc_s = acc_scr[...]

    vw_t = jnp.concatenate([vwt[qi * (QB // LANES) + c] for c in range(WK // LANES)], axis=1)
    acc_w = jnp.dot(vw_t, p_w, preferred_element_type=F32)


    sg_t = jax.nn.sigmoid(gt_ref[0]).T
    outs = []
    for r in range(R):
        cs = slice(r * QB, (r + 1) * QB)
        a_s = acc_s[:, cs]
        a_w = acc_w[:, cs]
        g_c = sg_t[3 * r:3 * r + 1, :]
        g_s = sg_t[3 * r + 1:3 * r + 2, :] / a_s[NSA_DH:NSA_DH + 1, :]
        g_w = sg_t[3 * r + 2:3 * r + 3, :] / a_w[NSA_DH:NSA_DH + 1, :]
        outs.append((g_c * o_c[:, cs] + g_s * a_s + g_w * a_w)[0:NSA_DH])
    o_ref[0] = jnp.concatenate(outs, axis=0).T.astype(BF)


def _nsa_attend(proj3, small3, kc, vc, slope_tab, ovt):
    B, S, _ = proj3.shape
    G, R = NSA_GROUPS, NSA_REP
    n_cmp = kc.shape[2]
    kern = functools.partial(_nsa_kernel, seq=S)
    kv_spec = lambda off: pl.BlockSpec((1, S, LANES), lambda b, g, i: (b, 0, off // LANES))
    return pl.pallas_call(
        kern,
        grid=(B, G, S // QB),
        in_specs=[
            pl.BlockSpec((1, QB, R * NSA_DH), lambda b, g, i: (b, i, OFF_NQ // (R * NSA_DH) + g)),
            kv_spec(OFF_KS), kv_spec(OFF_VS), kv_spec(OFF_KW), kv_spec(OFF_VW),
            pl.BlockSpec((1, 1, n_cmp, LANES), lambda b, g, i: (b, g, 0, 0)),
            pl.BlockSpec((1, 1, n_cmp, LANES), lambda b, g, i: (b, g, 0, 0)),
            pl.BlockSpec((1, QB, LANES), lambda b, g, i: (b, i, 1 + g)),
            pl.BlockSpec((1, 8, LANES), lambda b, g, i: (g, 0, 0)),
            _resident(ovt.shape, lambda b, g, i: (0, 0)),
        ],
        out_specs=pl.BlockSpec((1, QB, R * NSA_DH), lambda b, g, i: (b, i, g)),
        out_shape=jax.ShapeDtypeStruct((B, S, G * R * NSA_DH), BF),
        scratch_shapes=[
            pltpu.VMEM((S, LANES), BF),
            pltpu.VMEM((S + WINDOW, LANES), BF),
            pltpu.VMEM((S // LANES, VROWS, LANES), BF),
            pltpu.VMEM(((S + WINDOW) // LANES, VROWS, LANES), BF),
            pltpu.VMEM((VROWS, n_cmp), BF),
            pltpu.VMEM((VROWS, R * QB), F32),
            pltpu.SMEM((S // SLC_TILE + 1,), jnp.int32),
        ],
        compiler_params=_params(("parallel", "parallel", "arbitrary")),
    )(proj3, proj3, proj3, proj3, proj3, kc, vc, small3, slope_tab, ovt)


def _merge_kernel(og_ref, on_ref, mg_ref, mn_ref, x_ref, wg_ref, wn_ref, wo_ref, g_ref, o_ref):
    a = jnp.dot(og_ref[...], wg_ref[...], preferred_element_type=F32)
    b = jnp.dot(on_ref[...], wn_ref[...], preferred_element_type=F32)
    mixed = jax.nn.sigmoid(mg_ref[...].astype(F32)) * a + jax.nn.sigmoid(mn_ref[...].astype(F32)) * b
    y = jnp.dot(mixed.astype(BF), wo_ref[...], preferred_element_type=F32)
    y = y * lax.rsqrt(jnp.mean(y * y, axis=-1, keepdims=True) + EPS) * g_ref[...]
    o_ref[...] = x_ref[...] + y


def _merge_out(o_gla2, o_nsa2, proj2, x2, wg, wn, wo, g, tm=512):
    n_tok = x2.shape[0]
    D = D_MODEL
    return pl.pallas_call(
        _merge_kernel,
        grid=(n_tok // tm,),
        in_specs=[
            pl.BlockSpec((tm, D), lambda i: (i, 0)),
            pl.BlockSpec((tm, o_nsa2.shape[1]), lambda i: (i, 0)),
            pl.BlockSpec((tm, D), lambda i: (i, OFF_MG // D)),
            pl.BlockSpec((tm, D), lambda i: (i, OFF_MN // D)),
            pl.BlockSpec((tm, D), lambda i: (i, 0)),
            _resident(wg.shape, lambda i: (0, 0)),
            _resident(wn.shape, lambda i: (0, 0)),
            _resident(wo.shape, lambda i: (0, 0)),
            _resident((1, D), lambda i: (0, 0)),
        ],
        out_specs=pl.BlockSpec((tm, D), lambda i: (i, 0)),
        out_shape=jax.ShapeDtypeStruct((n_tok, D), F32),
        compiler_params=_params(("parallel",)),
    )(o_gla2, o_nsa2, proj2, proj2, x2, wg, wn, wo, g)


def _ffn_kernel(x_ref, gpre_ref, wg_ref, wu_ref, wd_ref, gpost_ref, o_ref, acc_scr, *, chunk):
    x = x_ref[...]
    h = (x * lax.rsqrt(jnp.mean(x * x, axis=-1, keepdims=True) + EPS) * gpre_ref[...]).astype(BF)
    d_ff = wg_ref.shape[1]
    for n, c0 in enumerate(range(0, d_ff, chunk)):
        c1 = min(c0 + chunk, d_ff)
        a = jnp.dot(h, wg_ref[:, c0:c1], preferred_element_type=F32)
        u = jnp.dot(h, wu_ref[:, c0:c1], preferred_element_type=F32)
        t = (a * jax.nn.sigmoid(a) * u).astype(BF)
        part = jnp.dot(t, wd_ref[c0:c1, :], preferred_element_type=F32)
        if n == 0:
            acc_scr[...] = part
        else:
            acc_scr[...] += part
    f = acc_scr[...]
    o_ref[...] = x + f * lax.rsqrt(jnp.mean(f * f, axis=-1, keepdims=True) + EPS) * gpost_ref[...]


def _ffn(x2, gpre, wg, wu, wd, gpost, tm=512, chunk=512):
    n_tok = x2.shape[0]
    D = D_MODEL
    kern = functools.partial(_ffn_kernel, chunk=chunk)
    return pl.pallas_call(
        kern,
        grid=(n_tok // tm,),
        in_specs=[
            pl.BlockSpec((tm, D), lambda i: (i, 0)),
            _resident((1, D), lambda i: (0, 0)),
            _resident(wg.shape, lambda i: (0, 0)),
            _resident(wu.shape, lambda i: (0, 0)),
            _resident(wd.shape, lambda i: (0, 0)),
            _resident((1, D), lambda i: (0, 0)),
        ],
        out_specs=pl.BlockSpec((tm, D), lambda i: (i, 0)),
        out_shape=jax.ShapeDtypeStruct((n_tok, D), F32),
        scratch_shapes=[pltpu.VMEM((tm, D), F32)],
        compiler_params=_params(("parallel",)),
    )(x2, gpre, wg, wu, wd, gpost)


def _prep_in_weights(w_in):
    splits = np.cumsum([512, 512, 1024, 1024, GLA_RANK, 512, 128, 128, 128, 128, 128, 128, 24, 1024])
    (g_q, g_k, g_v, g_r, g_a, n_q, n_kc, n_vc, n_ks, n_vs, n_kw, n_vw, n_gate, m_g, m_n) = jnp.split(
        w_in, [int(s) for s in splits], axis=1)
    w_main = jnp.concatenate([g_q, g_k, g_v, g_r, m_g, m_n, n_q * (NSA_DH ** -0.5),
                              n_kc, n_vc, n_ks, n_vs, n_kw, n_vw], axis=1).astype(BF)
    d = w_in.shape[0]
    per_g = NSA_REP * 3
    gates = jnp.pad(n_gate.reshape(d, NSA_GROUPS, per_g), ((0, 0), (0, 0), (0, LANES - per_g)))
    g_a3 = jnp.pad(jnp.concatenate([g_a, g_a, g_a], axis=1), ((0, 0), (0, LANES - 3 * GLA_RANK)))
    w_small = jnp.concatenate([g_a3, gates.reshape(d, NSA_GROUPS * LANES)], axis=1).astype(BF)
    return w_main, w_small


def _prep_compress(pe, w1, w2):
    eye = jnp.eye(NSA_GROUPS, dtype=F32)
    w1r = w1.reshape(CMP_LEN, NSA_DH, NSA_DH)
    w1e = jnp.einsum('lde,gh->lgdhe', w1r, eye).reshape(CMP_LEN * NSA_GROUPS * NSA_DH, NSA_GROUPS * NSA_DH)
    pe_e = jnp.broadcast_to(pe[:, None, :], (CMP_LEN, NSA_GROUPS, NSA_DH)).reshape(2, CMP_STRIDE * LANES)
    pe_e = jnp.pad(pe_e, ((0, 6), (0, 0)))
    w2e = jnp.stack([
        jnp.pad(jnp.pad(w2, ((g * NSA_DH, (NSA_GROUPS - 1 - g) * NSA_DH), (0, 0))), ((0, 0), (0, LANES - NSA_DH)))
        for g in range(NSA_GROUPS)])
    return pe_e.astype(F32), w1e.astype(BF), w2e.astype(BF)


def _overlap_table(seq):
    n_cmp = (seq - CMP_LEN) // CMP_STRIDE + 1
    n_slc = seq // SLC_LEN
    sc = CMP_STRIDE * np.arange(n_cmp)
    ss = SLC_LEN * np.arange(n_slc)
    ov = np.clip(np.minimum(sc[:, None] + CMP_LEN, ss[None, :] + SLC_LEN)
                 - np.maximum(sc[:, None], ss[None, :]), 0, None).astype(np.float32) / CMP_LEN
    ovt = np.zeros((NSA_DH, n_cmp + 1), np.float32)
    ovt[:n_slc, :n_cmp] = ov.T
    return jnp.asarray(ovt, dtype=BF)


def kernel(x, norm_mix_pre, norm_mix_post, norm_ffn_pre, norm_ffn_post, w_in, gla_w_alpha2, gla_b_alpha, gla_norm_g, nsa_cmp_pe_k, nsa_cmp_w1_k, nsa_cmp_w2_k, nsa_cmp_pe_v, nsa_cmp_w1_v, nsa_cmp_w2_v, w_proj_gla, w_proj_nsa, w_out, w_ffn_gate, w_ffn_up, w_ffn_down):
    B, S, D = x.shape
    depth = w_in.shape[0]
    n_tok = B * S
    h_idx = jnp.arange(NSA_HEADS, dtype=F32)
    slopes = jnp.exp2(-8.0 * (h_idx + 1.0) / NSA_HEADS).reshape(NSA_GROUPS, NSA_REP, 1)
    slope_tab = jnp.broadcast_to(jnp.pad(slopes, ((0, 0), (0, 8 - NSA_REP), (0, 0))), (NSA_GROUPS, 8, LANES))
    ovt = _overlap_table(S)
    x2 = x.reshape(n_tok, D)
    for l in range(depth):
        w_main, w_small = _prep_in_weights(w_in[l])
        proj2, small2 = _in_proj(x2, norm_mix_pre[l][None, :], w_main, w_small)
        proj3 = proj2.reshape(B, S, N_MAIN)
        small3 = small2.reshape(B, S, N_SMALL)

        w2_hi = gla_w_alpha2[l].astype(BF)
        w2_lo = (gla_w_alpha2[l] - w2_hi.astype(F32)).astype(BF)
        w2p = jnp.pad(jnp.concatenate([w2_hi, w2_hi, w2_lo], axis=0), ((0, LANES - 3 * GLA_RANK), (0, 0)))
        o_gla = _gla(proj3, small3, w2p, gla_b_alpha[l][None, :], gla_norm_g[l][None, :])

        xk = proj3[:, :, OFF_KC:OFF_KC + LANES].reshape(B, S // CMP_STRIDE, CMP_STRIDE * LANES)
        xv = proj3[:, :, OFF_VC:OFF_VC + LANES].reshape(B, S // CMP_STRIDE, CMP_STRIDE * LANES)
        pek, w1k, w2k = _prep_compress(nsa_cmp_pe_k[l], nsa_cmp_w1_k[l], nsa_cmp_w2_k[l])
        pev, w1v, w2v = _prep_compress(nsa_cmp_pe_v[l], nsa_cmp_w1_v[l], nsa_cmp_w2_v[l])
        kc, vc = _nsa_compress(xk, xv, pek, pev, w1k, w1v, w2k, w2v)
        o_nsa = _nsa_attend(proj3, small3, kc, vc, slope_tab, ovt)

        x2 = _merge_out(o_gla.reshape(n_tok, -1), o_nsa.reshape(n_tok, -1), proj2, x2,
                        w_proj_gla[l].astype(BF), w_proj_nsa[l].astype(BF),
                        w_out[l].astype(BF), norm_mix_post[l][None, :])
        x2 = _ffn(x2, norm_ffn_pre[l][None, :], w_ffn_gate[l].astype(BF), w_ffn_up[l].astype(BF),
                  w_ffn_down[l].astype(BF), norm_ffn_post[l][None, :])
    return x2.reshape(B, S, D)
```

```python
import functools

import numpy as np
import jax
import jax.numpy as jnp
from jax import lax
from jax.experimental import pallas as pl
from jax.experimental.pallas import tpu as pltpu

D_MODEL = 1024
GLA_HEADS = 4
GLA_DK = 128
GLA_DV = 256
GLA_RANK = 16
GLA_TAU = 16.0
GLA_CHUNK = 64
NSA_HEADS = 8
NSA_GROUPS = 2
NSA_REP = 4
NSA_DH = 64
CMP_LEN = 32
CMP_STRIDE = 16
SLC_LEN = 64
N_SEL = 16
WINDOW = 512
QB = 256
SLC_TILE = 256
VROWS = 80
D_FF = 2816
EPS = 1e-6
NEG = -1e30

LANES = 128
VMEM_LIMIT = 56 * 1024 * 1024
BF = jnp.bfloat16
F32 = jnp.float32

OFF_GQ = 0
OFF_GK = 512
OFF_GV = 1024
OFF_GR = 2048
OFF_MG = 3072
OFF_MN = 4096
OFF_NQ = 5120
OFF_KC = 5632
OFF_VC = 5760
OFF_KS = 5888
OFF_VS = 6016
OFF_KW = 6144
OFF_VW = 6272
N_MAIN = 6400
N_SMALL = 384

NT = (((1,), (1,)), ((), ()))
TN = (((0,), (0,)), ((), ()))


def _resident(shape, index_map):
    return pl.BlockSpec(shape, index_map, pipeline_mode=pl.Buffered(1))


def _params(sem):
    return pltpu.CompilerParams(dimension_semantics=sem, vmem_limit_bytes=VMEM_LIMIT)


def _in_proj_kernel(x_ref, g_ref, wm_ref, ws_ref, om_ref, os_ref):
    x = x_ref[...]
    h = (x * lax.rsqrt(jnp.mean(x * x, axis=-1, keepdims=True) + EPS) * g_ref[...]).astype(BF)
    os_ref[...] = jnp.dot(h, ws_ref[...], preferred_element_type=F32)
    n_out = om_ref.shape[1]
    step = 512
    for c0 in range(0, n_out, step):
        c1 = min(c0 + step, n_out)
        om_ref[:, c0:c1] = jnp.dot(h, wm_ref[:, c0:c1], preferred_element_type=F32).astype(BF)


def _in_proj(x2, g, w_main, w_small, tm=512):
    n_tok = x2.shape[0]
    return pl.pallas_call(
        _in_proj_kernel,
        grid=(n_tok // tm,),
        in_specs=[
            pl.BlockSpec((tm, D_MODEL), lambda i: (i, 0)),
            _resident((1, D_MODEL), lambda i: (0, 0)),
            _resident((D_MODEL, N_MAIN), lambda i: (0, 0)),
            _resident((D_MODEL, N_SMALL), lambda i: (0, 0)),
        ],
        out_specs=[
            pl.BlockSpec((tm, N_MAIN), lambda i: (i, 0)),
            pl.BlockSpec((tm, N_SMALL), lambda i: (i, 0)),
        ],
        out_shape=[
            jax.ShapeDtypeStruct((n_tok, N_MAIN), BF),
            jax.ShapeDtypeStruct((n_tok, N_SMALL), F32),
        ],
        compiler_params=_params(("parallel",)),
    )(x2, g, w_main, w_small)


def _gla_kernel(q_ref, k_ref, v_ref, r_ref, a_ref, w2_ref, b2_ref, ng_ref, o_ref, st_scr, *, n_chunks, n_heads):
    blk = pl.program_id(2)

    @pl.when(blk == 0)
    def _():
        st_scr[...] = jnp.zeros_like(st_scr)

    C = GLA_CHUNK
    T = n_chunks * C
    W = n_heads * GLA_DK

    def split3(x):
        hi = x.astype(BF)
        rem = x - hi.astype(F32)
        mid = rem.astype(BF)
        return hi, mid, (rem - mid.astype(F32)).astype(BF)

    a = a_ref[0]
    a_hi = a.astype(BF)
    a_lo = (a - a_hi.astype(F32)).astype(BF)
    lane = lax.broadcasted_iota(jnp.int32, (T, LANES), 1)
    in_lo = (lane >= GLA_RANK) & (lane < 2 * GLA_RANK)
    z = jnp.dot(jnp.where(in_lo, a_lo, a_hi), w2_ref[...], preferred_element_type=F32) + b2_ref[...]
    log_a = (jnp.minimum(z, 0.0) - jnp.log1p(jnp.exp(-jnp.abs(z)))) * (1.0 / GLA_TAU)

    x_wide = jnp.concatenate([log_a[c * C:(c + 1) * C] for c in range(n_chunks)], axis=1)
    x3 = jnp.concatenate(split3(x_wide), axis=0)
    r3 = lax.broadcasted_iota(jnp.int32, (C, 3 * C), 0)
    c3 = lax.broadcasted_iota(jnp.int32, (C, 3 * C), 1) & (C - 1)
    tri3 = jnp.where(c3 <= r3, 1.0, 0.0).astype(BF)
    b_wide = jnp.dot(tri3, x3, preferred_element_type=F32)
    bcum = jnp.concatenate([b_wide[:, c * W:(c + 1) * W] for c in range(n_chunks)], axis=0)
    last_rows = [b_wide[C - 1:C, c * W:(c + 1) * W] for c in range(n_chunks)]
    b_last = jnp.concatenate([jnp.broadcast_to(lr, (C, W)) for lr in last_rows], axis=0)
    decay = [jnp.exp(lr) for lr in last_rows]

    q = q_ref[0].astype(F32)
    k = k_ref[0].astype(F32)
    v = v_ref[0]
    qe = (q * ((GLA_DK ** -0.5) * jnp.exp(bcum))).astype(BF)
    ke = (k * jnp.exp(-bcum)).astype(BF)
    kd = (k * jnp.exp(b_last - bcum)).astype(BF)

    H = min(T, 4 * C)
    row = lax.broadcasted_iota(jnp.int32, (H, H), 0)
    col = lax.broadcasted_iota(jnp.int32, (H, H), 1)
    keep = (col <= row) & ((col >> 6) == (row >> 6))
    ng = ng_ref[...]
    r_all = r_ref[0].astype(F32)
    for hh in range(n_heads):
        ks_ = slice(hh * GLA_DK, (hh + 1) * GLA_DK)
        vs_ = slice(hh * GLA_DV, (hh + 1) * GLA_DV)
        intra = []
        for h0 in range(0, T, H):
            hs = slice(h0, h0 + H)
            attn = lax.dot_general(qe[hs, ks_], ke[hs, ks_], NT, preferred_element_type=F32)
            intra.append(jnp.dot(jnp.where(keep, attn, 0.0).astype(BF), v[hs, vs_], preferred_element_type=F32))
        o = jnp.concatenate(intra, axis=0)

        st = st_scr[hh]
        inter = []
        for c in range(n_chunks):
            sl = slice(c * C, (c + 1) * C)
            inter.append(lax.dot_general(qe[sl, ks_], st.astype(BF), NT, preferred_element_type=F32))
            upd = lax.dot_general(v[sl, vs_], kd[sl, ks_], TN, preferred_element_type=F32)
            st = st * decay[c][:, ks_] + upd
        st_scr[hh] = st
        o = o + jnp.concatenate(inter, axis=0)

        o = o * lax.rsqrt(jnp.mean(o * o, axis=-1, keepdims=True) + EPS) * ng
        r = r_all[:, vs_]
        o_ref[0, :, vs_] = (o * (r * jax.nn.sigmoid(r))).astype(BF)


def _gla(proj3, small3, w2p, b2, ng, blk_tokens=512, heads_per_step=4):
    B, S, _ = proj3.shape
    nblk = S // blk_tokens
    hb = heads_per_step
    kern = functools.partial(_gla_kernel, n_chunks=blk_tokens // GLA_CHUNK, n_heads=hb)
    wk, wv = hb * GLA_DK, hb * GLA_DV
    return pl.pallas_call(
        kern,
        grid=(B, GLA_HEADS // hb, nblk),
        in_specs=[
            pl.BlockSpec((1, blk_tokens, wk), lambda b, h, i: (b, i, OFF_GQ // wk + h)),
            pl.BlockSpec((1, blk_tokens, wk), lambda b, h, i: (b, i, OFF_GK // wk + h)),
            pl.BlockSpec((1, blk_tokens, wv), lambda b, h, i: (b, i, OFF_GV // wv + h)),
            pl.BlockSpec((1, blk_tokens, wv), lambda b, h, i: (b, i, OFF_GR // wv + h)),
            pl.BlockSpec((1, blk_tokens, LANES), lambda b, h, i: (b, i, 0)),
            pl.BlockSpec((LANES, wk), lambda b, h, i: (0, h)),
            pl.BlockSpec((1, wk), lambda b, h, i: (0, h)),
            pl.BlockSpec((1, GLA_DV), lambda b, h, i: (0, 0)),
        ],
        out_specs=pl.BlockSpec((1, blk_tokens, wv), lambda b, h, i: (b, i, h)),
        out_shape=jax.ShapeDtypeStruct((B, S, GLA_HEADS * GLA_DV), BF),
        scratch_shapes=[pltpu.VMEM((hb, GLA_DV, GLA_DK), F32)],
        compiler_params=_params(("parallel", "parallel", "arbitrary")),
    )(proj3, proj3, proj3, proj3, small3, w2p, b2, ng)


def _compress_kernel(xk_ref, xv_ref, pek_ref, pev_ref, w1k_ref, w1v_ref, w2k_ref, w2v_ref, kc_ref, vc_ref):
    n_rows = xk_ref.shape[1]
    half = CMP_STRIDE * LANES
    lane = lax.broadcasted_iota(jnp.int32, (n_rows, LANES), 1)
    row = lax.broadcasted_iota(jnp.int32, (n_rows, LANES), 0)
    end_c = CMP_STRIDE * row + (CMP_LEN - 1)
    c_k = jnp.where(lane == NSA_DH, end_c >> 6, jnp.where(lane == NSA_DH + 1, end_c & 63, 0)).astype(F32)
    c_v = jnp.where(lane == NSA_DH, 1.0, 0.0).astype(F32)

    def branch(x_ref, pe_ref, w1_ref, w2_ref, const, o_ref):
        x = x_ref[0].astype(F32)
        xa = (x + pe_ref[0:1, :]).astype(BF)
        xb = (x + pe_ref[1:2, :]).astype(BF)
        a = jnp.dot(xa, w1_ref[0:half, :], preferred_element_type=F32)
        b = jnp.dot(xb, w1_ref[half:2 * half, :], preferred_element_type=F32)
        pre = a + pltpu.roll(b, n_rows - 1, 0)
        hid = (pre * jax.nn.sigmoid(pre)).astype(BF)
        for g in range(NSA_GROUPS):
            o_ref[0, g] = (jnp.dot(hid, w2_ref[g], preferred_element_type=F32) + const).astype(BF)

    branch(xk_ref, pek_ref, w1k_ref, w2k_ref, c_k, kc_ref)
    branch(xv_ref, pev_ref, w1v_ref, w2v_ref, c_v, vc_ref)


def _nsa_compress(xk, xv, pek, pev, w1k, w1v, w2k, w2v):
    B, n_rows, width = xk.shape
    full = lambda shape: _resident(shape, lambda b: (0,) * len(shape))
    out = jax.ShapeDtypeStruct((B, NSA_GROUPS, n_rows, LANES), BF)
    return pl.pallas_call(
        _compress_kernel,
        grid=(B,),
        in_specs=[
            pl.BlockSpec((1, n_rows, width), lambda b: (b, 0, 0)),
            pl.BlockSpec((1, n_rows, width), lambda b: (b, 0, 0)),
            full(pek.shape), full(pev.shape), full(w1k.shape), full(w1v.shape),
            full(w2k.shape), full(w2v.shape),
        ],
        out_specs=[pl.BlockSpec((1, NSA_GROUPS, n_rows, LANES), lambda b: (b, 0, 0, 0))] * 2,
        out_shape=[out, out],
        compiler_params=_params(("parallel",)),
    )(xk, xv, pek, pev, w1k, w1v, w2k, w2v)


def _nsa_kernel(q_ref, ks_ref, vs_ref, kw_ref, vw_ref, kc_ref, vc_ref, gt_ref, sl_ref, ov_ref,
                o_ref, ksa, kwa, vst, vwt, vct, acc_scr, act_ref, *, seq):
    qi = pl.program_id(2)
    n_slc = seq // SLC_LEN
    R = NSA_REP
    KT = SLC_TILE
    WK = WINDOW + QB
    n_cmp = kc_ref.shape[2]

    def t_bf(x):
        return x.astype(F32).T[0:VROWS].astype(BF)

    @pl.when(qi == 0)
    def _():
        grp = pl.program_id(1)
        lane = lax.broadcasted_iota(jnp.int32, (seq, LANES), 1)
        pos = lax.broadcasted_iota(jnp.int32, (seq, LANES), 0)
        blk = pos >> 6
        off = pos & 63
        r_i = lax.broadcasted_iota(jnp.int32, (LANES, LANES), 0)
        c_i = lax.broadcasted_iota(jnp.int32, (LANES, LANES), 1)
        pick = jnp.where((c_i < NSA_DH) & (r_i == c_i + grp * NSA_DH), 1.0, 0.0).astype(BF)
        c_s = jnp.where(lane == LANES - 1, off,
                        jnp.where((lane >= NSA_DH) & (lane - (NSA_DH - 1) == blk), 1, 0))
        ksa[0:seq, :] = (jnp.dot(ks_ref[0], pick, preferred_element_type=F32) + c_s.astype(F32)).astype(BF)
        lane_d = lax.broadcasted_iota(jnp.int32, (KT, LANES), 1)
        ksa[seq:seq + KT, :] = jnp.where((lane_d >= NSA_DH) & (lane_d < LANES - 1), 1, 0).astype(BF)
        for c in range(KT // LANES):
            vst[seq // LANES + c] = jnp.zeros((VROWS, LANES), BF)
        c_w = jnp.where(lane == NSA_DH, blk, jnp.where(lane == NSA_DH + 1, off, 0))
        kwa[0:WINDOW, :] = jnp.zeros((WINDOW, LANES), BF)
        kwa[WINDOW:WINDOW + seq, :] = (jnp.dot(kw_ref[0], pick, preferred_element_type=F32)
                                       + c_w.astype(F32)).astype(BF)
        ones_rows = jnp.where(lax.broadcasted_iota(jnp.int32, (VROWS - NSA_DH, LANES), 0) == 0, 1.0, 0.0)
        n_pad = WINDOW // LANES
        for c in range(n_pad):
            vwt[c] = jnp.zeros((VROWS, LANES), BF)

        def v_tile(x):
            xt = x.astype(F32).T
            dims = jnp.where(grp == 0, xt[0:NSA_DH], xt[NSA_DH:2 * NSA_DH])
            return jnp.concatenate([dims, ones_rows], axis=0).astype(BF)

        def fill(c, carry):
            rows = pl.ds(pl.multiple_of(c * LANES, LANES), LANES)
            vst[c] = v_tile(vs_ref[0, rows, :])
            vwt[c + n_pad] = v_tile(vw_ref[0, rows, :])
            return carry

        lax.fori_loop(0, seq // LANES, fill, 0)
        for c in range(n_cmp // LANES):
            vct[:, c * LANES:(c + 1) * LANES] = t_bf(vc_ref[0, 0, c * LANES:(c + 1) * LANES, :])

    q0 = qi * QB
    rowi = lax.broadcasted_iota(jnp.int32, (NSA_DH, QB), 0)
    slopes = [jnp.concatenate([sl_ref[0, r:r + 1, :]] * (QB // LANES), axis=1) for r in range(R)]
    q_all = q_ref[0].astype(F32).T
    q_t = [q_all[r * NSA_DH:(r + 1) * NSA_DH] for r in range(R)]

    qw = jnp.concatenate(
        [jnp.concatenate([q_t[r], jnp.where(rowi == 0, slopes[r] * 64.0, jnp.where(rowi == 1, slopes[r], 0.0))],
                         axis=0).astype(BF) for r in range(R)], axis=1)

    def tile4(x):
        return jnp.concatenate([x] * R, axis=1)

    s_c = jnp.dot(kc_ref[0, 0], qw, preferred_element_type=F32)
    e_c = CMP_STRIDE * lax.broadcasted_iota(jnp.int32, (n_cmp, QB), 0) + (CMP_LEN - 1)
    t_c = q0 + lax.broadcasted_iota(jnp.int32, (n_cmp, QB), 1)
    s_c = s_c + tile4(jnp.where(e_c <= t_c, 0.0, NEG))
    m_c = jnp.maximum(jnp.max(s_c, axis=0, keepdims=True), 0.1 * NEG)
    p_c = jnp.exp(s_c - m_c)
    l_c = jnp.sum(p_c, axis=0, keepdims=True)
    p_c = p_c * jnp.where(l_c > 0.0, 1.0 / l_c, 0.0)
    o_c = jnp.dot(vct[...], p_c.astype(BF), preferred_element_type=F32)

    psum = p_c[:, 0:QB]
    for r in range(1, R):
        psum = psum + p_c[:, r * QB:(r + 1) * QB]
    ovt = ov_ref[...]
    p_hi = psum.astype(BF)
    rem = psum - p_hi.astype(F32)
    p_mid = rem.astype(BF)
    p_lo = (rem - p_mid.astype(F32)).astype(BF)
    imp = (jnp.dot(ovt, p_hi, preferred_element_type=F32)
           + jnp.dot(ovt, p_mid, preferred_element_type=F32)
           + jnp.dot(ovt, p_lo, preferred_element_type=F32))

    NR = ovt.shape[0]
    SUB = 8
    jblk = lax.broadcasted_iota(jnp.int32, (NR, QB), 0)
    t_q = q0 + lax.broadcasted_iota(jnp.int32, (NR, QB), 1)
    cur = t_q >> 6
    forced = (jblk == 0) | (jblk == cur) | (jblk == cur - 1)
    score = jnp.where(jblk > cur, NEG, jnp.where(forced, -NEG, imp))
    n_slab = -(-n_slc // SUB)
    isub = lax.broadcasted_iota(jnp.int32, (SUB, QB), 0)

    def rank_counts(ns):
        def fn():
            slabs = [score[a * SUB:(a + 1) * SUB, :] for a in range(ns)]
            cnts = [jnp.zeros((SUB, QB), F32) for _ in range(ns)]
            for jp in range(min(ns * SUB, n_slc)):
                rowv = jnp.broadcast_to(score[jp:jp + 1, :], (SUB, QB))
                for a in range(ns):
                    if a < jp // SUB:
                        beats = jnp.where(rowv > slabs[a], 1.0, 0.0)
                    elif a > jp // SUB:
                        beats = jnp.where(rowv >= slabs[a], 1.0, 0.0)
                    else:
                        beats = jnp.where(isub > jp % SUB, jnp.where(rowv >= slabs[a], 1.0, 0.0),
                                          jnp.where(rowv > slabs[a], 1.0, 0.0))
                    cnts[a] = cnts[a] + beats
            rest = NR - ns * SUB
            return jnp.concatenate(cnts + [jnp.full((rest, QB), float(NR), F32)] * (rest > 0), axis=0)
        return fn

    last_blk = (q0 + QB - 1) >> 6
    cnt = lax.switch(last_blk // SUB, [rank_counts(ns) for ns in range(1, n_slab + 1)])
    sel = (cnt < float(N_SEL)) & (jblk <= cur) & (jblk < n_slc)
    a_nat = jnp.where(jblk == 0, 1.0, jnp.where(sel, (SLC_LEN * jblk).astype(F32), NEG))
    a_nat = jnp.where(jblk < n_slc, a_nat, 0.0)
    a_t = pltpu.roll(a_nat, NR - 1, 0)

    any_q = jnp.max(jnp.where(sel, 1.0, 0.0), axis=1, keepdims=True)
    jcol = lax.broadcasted_iota(jnp.int32, (NR, 1), 0)
    bits = jnp.where(any_q > 0.0, lax.shift_left(jnp.int32(1), jcol & 31), 0)
    word0 = jnp.sum(jnp.where(jcol < 32, bits, 0))
    word1 = jnp.sum(jnp.where(jcol >= 32, bits, 0))

    qs = jnp.concatenate([jnp.concatenate([q_t[r], a_t * slopes[r]], axis=0).astype(BF) for r in range(R)],
                         axis=1)

    tiles_per = KT // LANES
    blocks_per = KT // SLC_LEN
    n_full = q0 // KT

    def scan(ti, n):
        word = jnp.where(ti < 32 // blocks_per, word0, word1)
        hit = (lax.shift_right_logical(word, (ti * blocks_per) & 31) & ((1 << blocks_per) - 1)) != 0
        act_ref[n] = ti
        return n + jnp.where(hit, 1, 0)

    n_act = lax.fori_loop(0, n_full, scan, 0)
    act_ref[n_act] = seq // KT

    kw = kwa[pl.ds(pl.multiple_of(q0, QB), WK), :]
    s_w = jnp.dot(kw, qw, preferred_element_type=F32)
    c_w = lax.broadcasted_iota(jnp.int32, (WK, QB), 0)
    d_w = c_w - lax.broadcasted_iota(jnp.int32, (WK, QB), 1)
    mask_w = (d_w > 0) & (d_w <= WINDOW) & (c_w >= WINDOW - q0)
    s_w = s_w + tile4(jnp.where(mask_w, 0.0, NEG))
    m_w = jnp.max(s_w, axis=0, keepdims=True)
    p_w = jnp.exp(s_w - m_w).astype(BF)

    def scores(ti):
        k0 = pl.multiple_of(ti * KT, KT)
        return jnp.dot(ksa[pl.ds(k0, KT), :], qs, preferred_element_type=F32)

    def values_t(ti):
        return [vst[ti * tiles_per + c] for c in range(tiles_per)]

    def absorb(s, v_tiles, m_prev):
        m_new = jnp.maximum(m_prev, jnp.max(s, axis=0, keepdims=True))
        alpha = jnp.exp(m_prev - m_new)
        p = jnp.exp(s - m_new).astype(BF)
        acc_scr[...] = acc_scr[...] * alpha + jnp.dot(jnp.concatenate(v_tiles, axis=1), p,
                                                     preferred_element_type=F32)
        return m_new

    acc_scr[...] = jnp.zeros(acc_scr.shape, F32)

    def body(j, m_prev):
        ta = act_ref[2 * j]
        tb = act_ref[2 * j + 1]
        s_a = scores(ta)
        s_b = scores(tb)
        return absorb(s_b, values_t(tb), absorb(s_a, values_t(ta), m_prev))

    m_d = lax.fori_loop(0, (n_act + 1) >> 1, body, jnp.full((1, R * QB), NEG, F32))
    for d in range(QB // KT):
        ti = n_full + d
        p_d = ti * KT + lax.broadcasted_iota(jnp.int32, (KT, QB), 0)
        t_d = q0 + lax.broadcasted_iota(jnp.int32, (KT, QB), 1)
        m_d = absorb(scores(ti) + tile4(jnp.where(p_d <= t_d, 0.0, NEG)), values_t(ti), m_d)
    acc_s = acc_scr[...]

    vw_t = jnp.concatenate([vwt[qi * (QB // LANES) + c] for c in range(WK // LANES)], axis=1)
    acc_w = jnp.dot(vw_t, p_w, preferred_element_type=F32)


    sg_t = jax.nn.sigmoid(gt_ref[0]).T
    outs = []
    for r in range(R):
        cs = slice(r * QB, (r + 1) * QB)
        a_s = acc_s[:, cs]
        a_w = acc_w[:, cs]
        g_c = sg_t[3 * r:3 * r + 1, :]
        g_s = sg_t[3 * r + 1:3 * r + 2, :] / a_s[NSA_DH:NSA_DH + 1, :]
        g_w = sg_t[3 * r + 2:3 * r + 3, :] / a_w[NSA_DH:NSA_DH + 1, :]
        outs.append((g_c * o_c[:, cs] + g_s * a_s + g_w * a_w)[0:NSA_DH])
    o_ref[0] = jnp.concatenate(outs, axis=0).T.astype(BF)


def _nsa_attend(proj3, small3, kc, vc, slope_tab, ovt):
    B, S, _ = proj3.shape
    G, R = NSA_GROUPS, NSA_REP
    n_cmp = kc.shape[2]
    kern = functools.partial(_nsa_kernel, seq=S)
    kv_spec = lambda off: pl.BlockSpec((1, S, LANES), lambda b, g, i: (b, 0, off // LANES))
    return pl.pallas_call(
        kern,
        grid=(B, G, S // QB),
        in_specs=[
            pl.BlockSpec((1, QB, R * NSA_DH), lambda b, g, i: (b, i, OFF_NQ // (R * NSA_DH) + g)),
            kv_spec(OFF_KS), kv_spec(OFF_VS), kv_spec(OFF_KW), kv_spec(OFF_VW),
            pl.BlockSpec((1, 1, n_cmp, LANES), lambda b, g, i: (b, g, 0, 0)),
            pl.BlockSpec((1, 1, n_cmp, LANES), lambda b, g, i: (b, g, 0, 0)),
            pl.BlockSpec((1, QB, LANES), lambda b, g, i: (b, i, 1 + g)),
            pl.BlockSpec((1, 8, LANES), lambda b, g, i: (g, 0, 0)),
            _resident(ovt.shape, lambda b, g, i: (0, 0)),
        ],
        out_specs=pl.BlockSpec((1, QB, R * NSA_DH), lambda b, g, i: (b, i, g)),
        out_shape=jax.ShapeDtypeStruct((B, S, G * R * NSA_DH), BF),
        scratch_shapes=[
            pltpu.VMEM((S + SLC_TILE, LANES), BF),
            pltpu.VMEM((S + WINDOW, LANES), BF),
            pltpu.VMEM(((S + SLC_TILE) // LANES, VROWS, LANES), BF),
            pltpu.VMEM(((S + WINDOW) // LANES, VROWS, LANES), BF),
            pltpu.VMEM((VROWS, n_cmp), BF),
            pltpu.VMEM((VROWS, R * QB), F32),
            pltpu.SMEM((S // SLC_TILE + 1,), jnp.int32),
        ],
        compiler_params=_params(("parallel", "parallel", "arbitrary")),
    )(proj3, proj3, proj3, proj3, proj3, kc, vc, small3, slope_tab, ovt)


def _merge_kernel(og_ref, on_ref, mg_ref, mn_ref, x_ref, wg_ref, wn_ref, wo_ref, g_ref, o_ref):
    a = jnp.dot(og_ref[...], wg_ref[...], preferred_element_type=F32)
    b = jnp.dot(on_ref[...], wn_ref[...], preferred_element_type=F32)
    mixed = jax.nn.sigmoid(mg_ref[...].astype(F32)) * a + jax.nn.sigmoid(mn_ref[...].astype(F32)) * b
    y = jnp.dot(mixed.astype(BF), wo_ref[...], preferred_element_type=F32)
    y = y * lax.rsqrt(jnp.mean(y * y, axis=-1, keepdims=True) + EPS) * g_ref[...]
    o_ref[...] = x_ref[...] + y


def _merge_out(o_gla2, o_nsa2, proj2, x2, wg, wn, wo, g, tm=512):
    n_tok = x2.shape[0]
    D = D_MODEL
    return pl.pallas_call(
        _merge_kernel,
        grid=(n_tok // tm,),
        in_specs=[
            pl.BlockSpec((tm, D), lambda i: (i, 0)),
            pl.BlockSpec((tm, o_nsa2.shape[1]), lambda i: (i, 0)),
            pl.BlockSpec((tm, D), lambda i: (i, OFF_MG // D)),
            pl.BlockSpec((tm, D), lambda i: (i, OFF_MN // D)),
            pl.BlockSpec((tm, D), lambda i: (i, 0)),
            _resident(wg.shape, lambda i: (0, 0)),
            _resident(wn.shape, lambda i: (0, 0)),
            _resident(wo.shape, lambda i: (0, 0)),
            _resident((1, D), lambda i: (0, 0)),
        ],
        out_specs=pl.BlockSpec((tm, D), lambda i: (i, 0)),
        out_shape=jax.ShapeDtypeStruct((n_tok, D), F32),
        compiler_params=_params(("parallel",)),
    )(o_gla2, o_nsa2, proj2, proj2, x2, wg, wn, wo, g)


def _ffn_kernel(x_ref, gpre_ref, wg_ref, wu_ref, wd_ref, gpost_ref, o_ref, acc_scr, *, chunk):
    x = x_ref[...]
    h = (x * lax.rsqrt(jnp.mean(x * x, axis=-1, keepdims=True) + EPS) * gpre_ref[...]).astype(BF)
    d_ff = wg_ref.shape[1]
    for n, c0 in enumerate(range(0, d_ff, chunk)):
        c1 = min(c0 + chunk, d_ff)
        a = jnp.dot(h, wg_ref[:, c0:c1], preferred_element_type=F32)
        u = jnp.dot(h, wu_ref[:, c0:c1], preferred_element_type=F32)
        t = (a * jax.nn.sigmoid(a) * u).astype(BF)
        part = jnp.dot(t, wd_ref[c0:c1, :], preferred_element_type=F32)
        if n == 0:
            acc_scr[...] = part
        else:
            acc_scr[...] += part
    f = acc_scr[...]
    o_ref[...] = x + f * lax.rsqrt(jnp.mean(f * f, axis=-1, keepdims=True) + EPS) * gpost_ref[...]


def _ffn(x2, gpre, wg, wu, wd, gpost, tm=512, chunk=512):
    n_tok = x2.shape[0]
    D = D_MODEL
    kern = functools.partial(_ffn_kernel, chunk=chunk)
    return pl.pallas_call(
        kern,
        grid=(n_tok // tm,),
        in_specs=[
            pl.BlockSpec((tm, D), lambda i: (i, 0)),
            _resident((1, D), lambda i: (0, 0)),
            _resident(wg.shape, lambda i: (0, 0)),
            _resident(wu.shape, lambda i: (0, 0)),
            _resident(wd.shape, lambda i: (0, 0)),
            _resident((1, D), lambda i: (0, 0)),
        ],
        out_specs=pl.BlockSpec((tm, D), lambda i: (i, 0)),
        out_shape=jax.ShapeDtypeStruct((n_tok, D), F32),
        scratch_shapes=[pltpu.VMEM((tm, D), F32)],
        compiler_params=_params(("parallel",)),
    )(x2, gpre, wg, wu, wd, gpost)


def _prep_in_weights(w_in):
    splits = np.cumsum([512, 512, 1024, 1024, GLA_RANK, 512, 128, 128, 128, 128, 128, 128, 24, 1024])
    (g_q, g_k, g_v, g_r, g_a, n_q, n_kc, n_vc, n_ks, n_vs, n_kw, n_vw, n_gate, m_g, m_n) = jnp.split(
        w_in, [int(s) for s in splits], axis=1)
    w_main = jnp.concatenate([g_q, g_k, g_v, g_r, m_g, m_n, n_q * (NSA_DH ** -0.5),
                              n_kc, n_vc, n_ks, n_vs, n_kw, n_vw], axis=1).astype(BF)
    d = w_in.shape[0]
    per_g = NSA_REP * 3
    gates = jnp.pad(n_gate.reshape(d, NSA_GROUPS, per_g), ((0, 0), (0, 0), (0, LANES - per_g)))
    g_a3 = jnp.pad(jnp.concatenate([g_a, g_a, g_a], axis=1), ((0, 0), (0, LANES - 3 * GLA_RANK)))
    w_small = jnp.concatenate([g_a3, gates.reshape(d, NSA_GROUPS * LANES)], axis=1).astype(BF)
    return w_main, w_small


def _prep_compress(pe, w1, w2):
    eye = jnp.eye(NSA_GROUPS, dtype=F32)
    w1r = w1.reshape(CMP_LEN, NSA_DH, NSA_DH)
    w1e = jnp.einsum('lde,gh->lgdhe', w1r, eye).reshape(CMP_LEN * NSA_GROUPS * NSA_DH, NSA_GROUPS * NSA_DH)
    pe_e = jnp.broadcast_to(pe[:, None, :], (CMP_LEN, NSA_GROUPS, NSA_DH)).reshape(2, CMP_STRIDE * LANES)
    pe_e = jnp.pad(pe_e, ((0, 6), (0, 0)))
    w2e = jnp.stack([
        jnp.pad(jnp.pad(w2, ((g * NSA_DH, (NSA_GROUPS - 1 - g) * NSA_DH), (0, 0))), ((0, 0), (0, LANES - NSA_DH)))
        for g in range(NSA_GROUPS)])
    return pe_e.astype(F32), w1e.astype(BF), w2e.astype(BF)


def _overlap_table(seq):
    n_cmp = (seq - CMP_LEN) // CMP_STRIDE + 1
    n_slc = seq // SLC_LEN
    sc = CMP_STRIDE * np.arange(n_cmp)
    ss = SLC_LEN * np.arange(n_slc)
    ov = np.clip(np.minimum(sc[:, None] + CMP_LEN, ss[None, :] + SLC_LEN)
                 - np.maximum(sc[:, None], ss[None, :]), 0, None).astype(np.float32) / CMP_LEN
    ovt = np.zeros((NSA_DH, n_cmp + 1), np.float32)
    ovt[:n_slc, :n_cmp] = ov.T
    return jnp.asarray(ovt, dtype=BF)


def kernel(x, norm_mix_pre, norm_mix_post, norm_ffn_pre, norm_ffn_post, w_in, gla_w_alpha2, gla_b_alpha, gla_norm_g, nsa_cmp_pe_k, nsa_cmp_w1_k, nsa_cmp_w2_k, nsa_cmp_pe_v, nsa_cmp_w1_v, nsa_cmp_w2_v, w_proj_gla, w_proj_nsa, w_out, w_ffn_gate, w_ffn_up, w_ffn_down):
    B, S, D = x.shape
    depth = w_in.shape[0]
    n_tok = B * S
    h_idx = jnp.arange(NSA_HEADS, dtype=F32)
    slopes = jnp.exp2(-8.0 * (h_idx + 1.0) / NSA_HEADS).reshape(NSA_GROUPS, NSA_REP, 1)
    slope_tab = jnp.broadcast_to(jnp.pad(slopes, ((0, 0), (0, 8 - NSA_REP), (0, 0))), (NSA_GROUPS, 8, LANES))
    ovt = _overlap_table(S)
    x2 = x.reshape(n_tok, D)
    for l in range(depth):
        w_main, w_small = _prep_in_weights(w_in[l])
        proj2, small2 = _in_proj(x2, norm_mix_pre[l][None, :], w_main, w_small)
        proj3 = proj2.reshape(B, S, N_MAIN)
        small3 = small2.reshape(B, S, N_SMALL)

        w2_hi = gla_w_alpha2[l].astype(BF)
        w2_lo = (gla_w_alpha2[l] - w2_hi.astype(F32)).astype(BF)
        w2p = jnp.pad(jnp.concatenate([w2_hi, w2_hi, w2_lo], axis=0), ((0, LANES - 3 * GLA_RANK), (0, 0)))
        o_gla = _gla(proj3, small3, w2p, gla_b_alpha[l][None, :], gla_norm_g[l][None, :])

        xk = proj3[:, :, OFF_KC:OFF_KC + LANES].reshape(B, S // CMP_STRIDE, CMP_STRIDE * LANES)
        xv = proj3[:, :, OFF_VC:OFF_VC + LANES].reshape(B, S // CMP_STRIDE, CMP_STRIDE * LANES)
        pek, w1k, w2k = _prep_compress(nsa_cmp_pe_k[l], nsa_cmp_w1_k[l], nsa_cmp_w2_k[l])
        pev, w1v, w2v = _prep_compress(nsa_cmp_pe_v[l], nsa_cmp_w1_v[l], nsa_cmp_w2_v[l])
        kc, vc = _nsa_compress(xk, xv, pek, pev, w1k, w1v, w2k, w2v)
        o_nsa = _nsa_attend(proj3, small3, kc, vc, slope_tab, ovt)

        x2 = _merge_out(o_gla.reshape(n_tok, -1), o_nsa.reshape(n_tok, -1), proj2, x2,
                        w_proj_gla[l].astype(BF), w_proj_nsa[l].astype(BF),
                        w_out[l].astype(BF), norm_mix_post[l][None, :])
        x2 = _ffn(x2, norm_ffn_pre[l][None, :], w_ffn_gate[l].astype(BF), w_ffn_up[l].astype(BF),
                  w_ffn_down[l].astype(BF), norm_ffn_post[l][None, :])
    return x2.reshape(B, S, D)
```

```python
import functools

import numpy as np
import jax
import jax.numpy as jnp
from jax import lax
from jax.experimental import pallas as pl
from jax.experimental.pallas import tpu as pltpu

D_MODEL = 1024
GLA_HEADS = 4
GLA_DK = 128
GLA_DV = 256
GLA_RANK = 16
GLA_TAU = 16.0
GLA_CHUNK = 64
NSA_HEADS = 8
NSA_GROUPS = 2
NSA_REP = 4
NSA_DH = 64
CMP_LEN = 32
CMP_STRIDE = 16
SLC_LEN = 64
N_SEL = 16
WINDOW = 512
QB = 256
SLC_TILE = 256
VROWS = 80
D_FF = 2816
EPS = 1e-6
NEG = -1e30

LANES = 128
VMEM_LIMIT = 56 * 1024 * 1024
BF = jnp.bfloat16
F32 = jnp.float32

OFF_GQ = 0
OFF_GK = 512
OFF_GV = 1024
OFF_GR = 2048
OFF_MG = 3072
OFF_MN = 4096
OFF_NQ = 5120
OFF_KC = 5632
OFF_VC = 5760
OFF_KS = 5888
OFF_VS = 6016
OFF_KW = 6144
OFF_VW = 6272
N_MAIN = 6400
N_SMALL = 384

NT = (((1,), (1,)), ((), ()))
TN = (((0,), (0,)), ((), ()))


LOG2E = 1.4426950408889634
POS_TERMS = 3


def _pos_columns(lane, pos):
    first = lane - NSA_DH
    return jnp.where((first >= 0) & (first < POS_TERMS), pos >> 6,
                     jnp.where((first >= POS_TERMS) & (first < 2 * POS_TERMS), pos & 63, 0))


def _slope_rows(rowi, coef):
    terms = []
    rem = coef
    for _ in range(POS_TERMS):
        t = rem.astype(BF).astype(F32)
        terms.append(t)
        rem = rem - t
    out = jnp.zeros(rowi.shape, F32)
    for i, t in enumerate(terms):
        out = jnp.where(rowi == i, t * float(SLC_LEN), jnp.where(rowi == POS_TERMS + i, t, out))
    return out


def _resident(shape, index_map):
    return pl.BlockSpec(shape, index_map, pipeline_mode=pl.Buffered(1))


def _params(sem):
    return pltpu.CompilerParams(dimension_semantics=sem, vmem_limit_bytes=VMEM_LIMIT)


def _in_proj_kernel(x_ref, g_ref, wm_ref, ws_ref, om_ref, os_ref):
    x = x_ref[...]
    h = (x * lax.rsqrt(jnp.mean(x * x, axis=-1, keepdims=True) + EPS) * g_ref[...]).astype(BF)
    os_ref[...] = jnp.dot(h, ws_ref[...], preferred_element_type=F32)
    n_out = om_ref.shape[1]
    step = 512
    for c0 in range(0, n_out, step):
        c1 = min(c0 + step, n_out)
        om_ref[:, c0:c1] = jnp.dot(h, wm_ref[:, c0:c1], preferred_element_type=F32).astype(BF)


def _in_proj(x2, g, w_main, w_small, tm=512):
    n_tok = x2.shape[0]
    return pl.pallas_call(
        _in_proj_kernel,
        grid=(n_tok // tm,),
        in_specs=[
            pl.BlockSpec((tm, D_MODEL), lambda i: (i, 0)),
            _resident((1, D_MODEL), lambda i: (0, 0)),
            _resident((D_MODEL, N_MAIN), lambda i: (0, 0)),
            _resident((D_MODEL, N_SMALL), lambda i: (0, 0)),
        ],
        out_specs=[
            pl.BlockSpec((tm, N_MAIN), lambda i: (i, 0)),
            pl.BlockSpec((tm, N_SMALL), lambda i: (i, 0)),
        ],
        out_shape=[
            jax.ShapeDtypeStruct((n_tok, N_MAIN), BF),
            jax.ShapeDtypeStruct((n_tok, N_SMALL), F32),
        ],
        compiler_params=_params(("parallel",)),
    )(x2, g, w_main, w_small)


def _gla_kernel(q_ref, k_ref, v_ref, r_ref, a_ref, w2_ref, b2_ref, ng_ref, o_ref, st_scr, *, n_chunks, n_heads):
    blk = pl.program_id(2)

    @pl.when(blk == 0)
    def _():
        st_scr[...] = jnp.zeros_like(st_scr)

    C = GLA_CHUNK
    T = n_chunks * C
    W = n_heads * GLA_DK

    def split3(x):
        hi = x.astype(BF)
        rem = x - hi.astype(F32)
        mid = rem.astype(BF)
        return hi, mid, (rem - mid.astype(F32)).astype(BF)

    a = a_ref[0]
    a_hi = a.astype(BF)
    a_lo = (a - a_hi.astype(F32)).astype(BF)
    lane = lax.broadcasted_iota(jnp.int32, (T, LANES), 1)
    in_lo = (lane >= GLA_RANK) & (lane < 2 * GLA_RANK)
    z = jnp.dot(jnp.where(in_lo, a_lo, a_hi), w2_ref[...], preferred_element_type=F32) + b2_ref[...]
    log_a = (jnp.minimum(z, 0.0) - jnp.log1p(jnp.exp(-jnp.abs(z)))) * (1.0 / GLA_TAU)

    x_wide = jnp.concatenate([log_a[c * C:(c + 1) * C] for c in range(n_chunks)], axis=1)
    x3 = jnp.concatenate(split3(x_wide), axis=0)
    r3 = lax.broadcasted_iota(jnp.int32, (C, 3 * C), 0)
    c3 = lax.broadcasted_iota(jnp.int32, (C, 3 * C), 1) & (C - 1)
    tri3 = jnp.where(c3 <= r3, 1.0, 0.0).astype(BF)
    b_wide = jnp.dot(tri3, x3, preferred_element_type=F32)
    bcum = jnp.concatenate([b_wide[:, c * W:(c + 1) * W] for c in range(n_chunks)], axis=0)
    last_rows = [b_wide[C - 1:C, c * W:(c + 1) * W] for c in range(n_chunks)]
    b_last = jnp.concatenate([jnp.broadcast_to(lr, (C, W)) for lr in last_rows], axis=0)
    decay = [jnp.exp(lr) for lr in last_rows]

    q = q_ref[0].astype(F32)
    k = k_ref[0].astype(F32)
    v = v_ref[0]
    qe = (q * ((GLA_DK ** -0.5) * jnp.exp(bcum))).astype(BF)
    ke = (k * jnp.exp(-bcum)).astype(BF)
    kd = (k * jnp.exp(b_last - bcum)).astype(BF)

    H = min(T, 4 * C)
    row = lax.broadcasted_iota(jnp.int32, (H, H), 0)
    col = lax.broadcasted_iota(jnp.int32, (H, H), 1)
    keep = (col <= row) & ((col >> 6) == (row >> 6))
    ng = ng_ref[...]
    r_all = r_ref[0].astype(F32)
    for hh in range(n_heads):
        ks_ = slice(hh * GLA_DK, (hh + 1) * GLA_DK)
        vs_ = slice(hh * GLA_DV, (hh + 1) * GLA_DV)
        intra = []
        for h0 in range(0, T, H):
            hs = slice(h0, h0 + H)
            attn = lax.dot_general(qe[hs, ks_], ke[hs, ks_], NT, preferred_element_type=F32)
            intra.append(jnp.dot(jnp.where(keep, attn, 0.0).astype(BF), v[hs, vs_], preferred_element_type=F32))
        o = jnp.concatenate(intra, axis=0)

        st = st_scr[hh]
        inter = []
        for c in range(n_chunks):
            sl = slice(c * C, (c + 1) * C)
            inter.append(lax.dot_general(qe[sl, ks_], st.astype(BF), NT, preferred_element_type=F32))
            upd = lax.dot_general(v[sl, vs_], kd[sl, ks_], TN, preferred_element_type=F32)
            st = st * decay[c][:, ks_] + upd
        st_scr[hh] = st
        o = o + jnp.concatenate(inter, axis=0)

        o = o * lax.rsqrt(jnp.mean(o * o, axis=-1, keepdims=True) + EPS) * ng
        r = r_all[:, vs_]
        o_ref[0, :, vs_] = (o * (r * jax.nn.sigmoid(r))).astype(BF)


def _gla(proj3, small3, w2p, b2, ng, blk_tokens=512, heads_per_step=4):
    B, S, _ = proj3.shape
    nblk = S // blk_tokens
    hb = heads_per_step
    kern = functools.partial(_gla_kernel, n_chunks=blk_tokens // GLA_CHUNK, n_heads=hb)
    wk, wv = hb * GLA_DK, hb * GLA_DV
    return pl.pallas_call(
        kern,
        grid=(B, GLA_HEADS // hb, nblk),
        in_specs=[
            pl.BlockSpec((1, blk_tokens, wk), lambda b, h, i: (b, i, OFF_GQ // wk + h)),
            pl.BlockSpec((1, blk_tokens, wk), lambda b, h, i: (b, i, OFF_GK // wk + h)),
            pl.BlockSpec((1, blk_tokens, wv), lambda b, h, i: (b, i, OFF_GV // wv + h)),
            pl.BlockSpec((1, blk_tokens, wv), lambda b, h, i: (b, i, OFF_GR // wv + h)),
            pl.BlockSpec((1, blk_tokens, LANES), lambda b, h, i: (b, i, 0)),
            pl.BlockSpec((LANES, wk), lambda b, h, i: (0, h)),
            pl.BlockSpec((1, wk), lambda b, h, i: (0, h)),
            pl.BlockSpec((1, GLA_DV), lambda b, h, i: (0, 0)),
        ],
        out_specs=pl.BlockSpec((1, blk_tokens, wv), lambda b, h, i: (b, i, h)),
        out_shape=jax.ShapeDtypeStruct((B, S, GLA_HEADS * GLA_DV), BF),
        scratch_shapes=[pltpu.VMEM((hb, GLA_DV, GLA_DK), F32)],
        compiler_params=_params(("parallel", "parallel", "arbitrary")),
    )(proj3, proj3, proj3, proj3, small3, w2p, b2, ng)


def _compress_kernel(xk_ref, xv_ref, pek_ref, pev_ref, w1k_ref, w1v_ref, w2k_ref, w2v_ref, kc_ref, vc_ref):
    n_rows = xk_ref.shape[1]
    half = CMP_STRIDE * LANES
    lane = lax.broadcasted_iota(jnp.int32, (n_rows, LANES), 1)
    row = lax.broadcasted_iota(jnp.int32, (n_rows, LANES), 0)
    end_c = CMP_STRIDE * row + (CMP_LEN - 1)
    c_k = _pos_columns(lane, end_c).astype(F32)
    c_v = jnp.where(lane == NSA_DH, 1.0, 0.0).astype(F32)

    def branch(x_ref, pe_ref, w1_ref, w2_ref, const, o_ref):
        x = x_ref[0].astype(F32)
        xa = (x + pe_ref[0:1, :]).astype(BF)
        xb = (x + pe_ref[1:2, :]).astype(BF)
        a = jnp.dot(xa, w1_ref[0:half, :], preferred_element_type=F32)
        b = jnp.dot(xb, w1_ref[half:2 * half, :], preferred_element_type=F32)
        pre = a + pltpu.roll(b, n_rows - 1, 0)
        hid = (pre * jax.nn.sigmoid(pre)).astype(BF)
        for g in range(NSA_GROUPS):
            o_ref[0, g] = (jnp.dot(hid, w2_ref[g], preferred_element_type=F32) + const).astype(BF)

    branch(xk_ref, pek_ref, w1k_ref, w2k_ref, c_k, kc_ref)
    branch(xv_ref, pev_ref, w1v_ref, w2v_ref, c_v, vc_ref)


def _nsa_compress(xk, xv, pek, pev, w1k, w1v, w2k, w2v):
    B, n_rows, width = xk.shape
    full = lambda shape: _resident(shape, lambda b: (0,) * len(shape))
    out = jax.ShapeDtypeStruct((B, NSA_GROUPS, n_rows, LANES), BF)
    return pl.pallas_call(
        _compress_kernel,
        grid=(B,),
        in_specs=[
            pl.BlockSpec((1, n_rows, width), lambda b: (b, 0, 0)),
            pl.BlockSpec((1, n_rows, width), lambda b: (b, 0, 0)),
            full(pek.shape), full(pev.shape), full(w1k.shape), full(w1v.shape),
            full(w2k.shape), full(w2v.shape),
        ],
        out_specs=[pl.BlockSpec((1, NSA_GROUPS, n_rows, LANES), lambda b: (b, 0, 0, 0))] * 2,
        out_shape=[out, out],
        compiler_params=_params(("parallel",)),
    )(xk, xv, pek, pev, w1k, w1v, w2k, w2v)


def _nsa_kernel(q_ref, ks_ref, vs_ref, kw_ref, vw_ref, kc_ref, vc_ref, gt_ref, sl_ref, ov_ref,
                o_ref, ksa, kwa, vst, vwt, vct, acc_scr, act_ref, *, seq):
    qi = pl.program_id(2)
    n_slc = seq // SLC_LEN
    R = NSA_REP
    KT = SLC_TILE
    WK = WINDOW + QB
    n_cmp = kc_ref.shape[2]

    def t_bf(x):
        return x.astype(F32).T[0:VROWS].astype(BF)

    @pl.when(qi == 0)
    def _():
        grp = pl.program_id(1)
        lane = lax.broadcasted_iota(jnp.int32, (seq, LANES), 1)
        pos = lax.broadcasted_iota(jnp.int32, (seq, LANES), 0)
        blk = pos >> 6
        off = pos & 63
        r_i = lax.broadcasted_iota(jnp.int32, (LANES, LANES), 0)
        c_i = lax.broadcasted_iota(jnp.int32, (LANES, LANES), 1)
        pick = jnp.where((c_i < NSA_DH) & (r_i == c_i + grp * NSA_DH), 1.0, 0.0).astype(BF)
        c_s = jnp.where(lane == LANES - 1, off,
                        jnp.where((lane >= NSA_DH) & (lane - (NSA_DH - 1) == blk), 1, 0))
        ksa[0:seq, :] = (jnp.dot(ks_ref[0], pick, preferred_element_type=F32) + c_s.astype(F32)).astype(BF)
        lane_d = lax.broadcasted_iota(jnp.int32, (KT, LANES), 1)
        ksa[seq:seq + KT, :] = jnp.where((lane_d >= NSA_DH) & (lane_d < LANES - 1), 1, 0).astype(BF)
        for c in range(KT // LANES):
            vst[seq // LANES + c] = jnp.zeros((VROWS, LANES), BF)
        c_w = _pos_columns(lane, pos)
        kwa[0:WINDOW, :] = jnp.zeros((WINDOW, LANES), BF)
        kwa[WINDOW:WINDOW + seq, :] = (jnp.dot(kw_ref[0], pick, preferred_element_type=F32)
                                       + c_w.astype(F32)).astype(BF)
        ones_rows = jnp.where(lax.broadcasted_iota(jnp.int32, (VROWS - NSA_DH, LANES), 0) == 0, 1.0, 0.0)
        n_pad = WINDOW // LANES
        for c in range(n_pad):
            vwt[c] = jnp.zeros((VROWS, LANES), BF)

        def v_tile(x):
            xt = x.astype(F32).T
            dims = jnp.where(grp == 0, xt[0:NSA_DH], xt[NSA_DH:2 * NSA_DH])
            return jnp.concatenate([dims, ones_rows], axis=0).astype(BF)

        def fill(c, carry):
            rows = pl.ds(pl.multiple_of(c * LANES, LANES), LANES)
            vst[c] = v_tile(vs_ref[0, rows, :])
            vwt[c + n_pad] = v_tile(vw_ref[0, rows, :])
            return carry

        lax.fori_loop(0, seq // LANES, fill, 0)
        for c in range(n_cmp // LANES):
            vct[:, c * LANES:(c + 1) * LANES] = t_bf(vc_ref[0, 0, c * LANES:(c + 1) * LANES, :])

    q0 = qi * QB
    rowi = lax.broadcasted_iota(jnp.int32, (NSA_DH, QB), 0)
    slopes = [jnp.concatenate([sl_ref[0, r:r + 1, :]] * (QB // LANES), axis=1) for r in range(R)]
    q_all = q_ref[0].astype(F32).T
    q_t = [q_all[r * NSA_DH:(r + 1) * NSA_DH] for r in range(R)]

    qw = jnp.concatenate(
        [jnp.concatenate([q_t[r] * LOG2E, _slope_rows(rowi, slopes[r] * LOG2E)], axis=0).astype(BF)
         for r in range(R)], axis=1)

    def tile4(x):
        return jnp.concatenate([x] * R, axis=1)

    ovt = ov_ref[...]

    def cmp_branch(rows):
        def fn():
            s_c = jnp.dot(kc_ref[0, 0, 0:rows, :], qw, preferred_element_type=F32)
            e_c = CMP_STRIDE * lax.broadcasted_iota(jnp.int32, (rows, QB), 0) + (CMP_LEN - 1)
            t_c = q0 + lax.broadcasted_iota(jnp.int32, (rows, QB), 1)
            s_c = s_c + tile4(jnp.where(e_c <= t_c, 0.0, NEG))
            m_c = jnp.maximum(jnp.max(s_c, axis=0, keepdims=True), 0.1 * NEG)
            p_c = jnp.exp2(s_c - m_c)
            l_c = jnp.sum(p_c, axis=0, keepdims=True)
            p_c = p_c * jnp.where(l_c > 0.0, 1.0 / l_c, 0.0)
            o_cmp = jnp.dot(vct[:, 0:rows], p_c.astype(BF), preferred_element_type=F32)
            psum = p_c[:, 0:QB]
            for r in range(1, R):
                psum = psum + p_c[:, r * QB:(r + 1) * QB]
            p_hi = psum.astype(BF)
            rem = psum - p_hi.astype(F32)
            p_mid = rem.astype(BF)
            p_lo = (rem - p_mid.astype(F32)).astype(BF)
            ov = ovt[:, 0:rows]
            return o_cmp, (jnp.dot(ov, p_hi, preferred_element_type=F32)
                           + jnp.dot(ov, p_mid, preferred_element_type=F32)
                           + jnp.dot(ov, p_lo, preferred_element_type=F32))
        return fn

    half = n_cmp // 2
    o_c, imp = lax.cond(q0 + QB <= CMP_STRIDE * half + CMP_LEN - 1, cmp_branch(half), cmp_branch(n_cmp))

    NR = ov_ref.shape[0]
    SUB = 8
    jblk = lax.broadcasted_iota(jnp.int32, (NR, QB), 0)
    t_q = q0 + lax.broadcasted_iota(jnp.int32, (NR, QB), 1)
    cur = t_q >> 6
    forced = (jblk == 0) | (jblk == cur) | (jblk == cur - 1)
    score = jnp.where(jblk > cur, NEG, jnp.where(forced, -NEG, imp))
    n_slab = -(-n_slc // SUB)
    isub = lax.broadcasted_iota(jnp.int32, (SUB, QB), 0)

    def rank_counts(ns):
        def fn():
            slabs = [score[a * SUB:(a + 1) * SUB, :] for a in range(ns)]
            cnts = [jnp.zeros((SUB, QB), F32) for _ in range(ns)]
            for jp in range(min(ns * SUB, n_slc)):
                rowv = jnp.broadcast_to(score[jp:jp + 1, :], (SUB, QB))
                for a in range(ns):
                    if a < jp // SUB:
                        beats = jnp.where(rowv > slabs[a], 1.0, 0.0)
                    elif a > jp // SUB:
                        beats = jnp.where(rowv >= slabs[a], 1.0, 0.0)
                    else:
                        beats = jnp.where(isub > jp % SUB, jnp.where(rowv >= slabs[a], 1.0, 0.0),
                                          jnp.where(rowv > slabs[a], 1.0, 0.0))
                    cnts[a] = cnts[a] + beats
            rest = NR - ns * SUB
            return jnp.concatenate(cnts + [jnp.full((rest, QB), float(NR), F32)] * (rest > 0), axis=0)
        return fn

    last_blk = (q0 + QB - 1) >> 6
    cnt = lax.switch(last_blk // SUB, [rank_counts(ns) for ns in range(1, n_slab + 1)])
    sel = (cnt < float(N_SEL)) & (jblk <= cur) & (jblk < n_slc)
    a_nat = jnp.where(jblk == 0, 1.0, jnp.where(sel, (SLC_LEN * jblk).astype(F32), NEG))
    a_nat = jnp.where(jblk < n_slc, a_nat, 0.0)
    a_t = pltpu.roll(a_nat, NR - 1, 0)

    any_q = jnp.max(jnp.where(sel, 1.0, 0.0), axis=1, keepdims=True)
    jcol = lax.broadcasted_iota(jnp.int32, (NR, 1), 0)
    bits = jnp.where(any_q > 0.0, lax.shift_left(jnp.int32(1), jcol & 31), 0)
    word0 = jnp.sum(jnp.where(jcol < 32, bits, 0))
    word1 = jnp.sum(jnp.where(jcol >= 32, bits, 0))

    qs = jnp.concatenate([jnp.concatenate([q_t[r], a_t * slopes[r]], axis=0).astype(BF) for r in range(R)],
                         axis=1)

    tiles_per = KT // LANES
    blocks_per = KT // SLC_LEN
    n_full = q0 // KT

    def scan(ti, n):
        word = jnp.where(ti < 32 // blocks_per, word0, word1)
        hit = (lax.shift_right_logical(word, (ti * blocks_per) & 31) & ((1 << blocks_per) - 1)) != 0
        act_ref[n] = ti
        return n + jnp.where(hit, 1, 0)

    n_act = lax.fori_loop(0, n_full, scan, 0)
    act_ref[n_act] = seq // KT

    kw = kwa[pl.ds(pl.multiple_of(q0, QB), WK), :]
    s_w = jnp.dot(kw, qw, preferred_element_type=F32)
    c_w = lax.broadcasted_iota(jnp.int32, (WK, QB), 0)
    d_w = c_w - lax.broadcasted_iota(jnp.int32, (WK, QB), 1)
    mask_w = (d_w > 0) & (d_w <= WINDOW) & (c_w >= WINDOW - q0)
    s_w = s_w + tile4(jnp.where(mask_w, 0.0, NEG))
    m_w = jnp.max(s_w, axis=0, keepdims=True)
    p_w = jnp.exp2(s_w - m_w).astype(BF)

    def scores(ti):
        k0 = pl.multiple_of(ti * KT, KT)
        return jnp.dot(ksa[pl.ds(k0, KT), :], qs, preferred_element_type=F32)

    def values_t(ti):
        return [vst[ti * tiles_per + c] for c in range(tiles_per)]

    def absorb(s, v_tiles, m_prev):
        m_new = jnp.maximum(m_prev, jnp.max(s, axis=0, keepdims=True))
        alpha = jnp.exp(m_prev - m_new)
        p = jnp.exp(s - m_new).astype(BF)
        acc_scr[...] = acc_scr[...] * alpha + jnp.dot(jnp.concatenate(v_tiles, axis=1), p,
                                                     preferred_element_type=F32)
        return m_new

    acc_scr[...] = jnp.zeros(acc_scr.shape, F32)

    def body(j, m_prev):
        ta = act_ref[2 * j]
        tb = act_ref[2 * j + 1]
        s_a = scores(ta)
        s_b = scores(tb)
        return absorb(s_b, values_t(tb), absorb(s_a, values_t(ta), m_prev))

    m_d = lax.fori_loop(0, (n_act + 1) >> 1, body, jnp.full((1, R * QB), NEG, F32))
    for d in range(QB // KT):
        ti = n_full + d
        p_d = ti * KT + lax.broadcasted_iota(jnp.int32, (KT, QB), 0)
        t_d = q0 + lax.broadcasted_iota(jnp.int32, (KT, QB), 1)
        m_d = absorb(scores(ti) + tile4(jnp.where(p_d <= t_d, 0.0, NEG)), values_t(ti), m_d)
    acc_s = acc_scr[...]

    vw_t = jnp.concatenate([vwt[qi * (QB // LANES) + c] for c in range(WK // LANES)], axis=1)
    acc_w = jnp.dot(vw_t, p_w, preferred_element_type=F32)


    sg_t = jax.nn.sigmoid(gt_ref[0]).T
    outs = []
    for r in range(R):
        cs = slice(r * QB, (r + 1) * QB)
        a_s = acc_s[:, cs]
        a_w = acc_w[:, cs]
        g_c = sg_t[3 * r:3 * r + 1, :]
        g_s = sg_t[3 * r + 1:3 * r + 2, :] / a_s[NSA_DH:NSA_DH + 1, :]
        g_w = sg_t[3 * r + 2:3 * r + 3, :] / a_w[NSA_DH:NSA_DH + 1, :]
        outs.append((g_c * o_c[:, cs] + g_s * a_s + g_w * a_w)[0:NSA_DH])
    o_ref[0] = jnp.concatenate(outs, axis=0).T.astype(BF)


def _nsa_attend(proj3, small3, kc, vc, slope_tab, ovt):
    B, S, _ = proj3.shape
    G, R = NSA_GROUPS, NSA_REP
    n_cmp = kc.shape[2]
    kern = functools.partial(_nsa_kernel, seq=S)
    kv_spec = lambda off: pl.BlockSpec((1, S, LANES), lambda b, g, i: (b, 0, off // LANES))
    return pl.pallas_call(
        kern,
        grid=(B, G, S // QB),
        in_specs=[
            pl.BlockSpec((1, QB, R * NSA_DH), lambda b, g, i: (b, i, OFF_NQ // (R * NSA_DH) + g)),
            kv_spec(OFF_KS), kv_spec(OFF_VS), kv_spec(OFF_KW), kv_spec(OFF_VW),
            pl.BlockSpec((1, 1, n_cmp, LANES), lambda b, g, i: (b, g, 0, 0)),
            pl.BlockSpec((1, 1, n_cmp, LANES), lambda b, g, i: (b, g, 0, 0)),
            pl.BlockSpec((1, QB, LANES), lambda b, g, i: (b, i, 1 + g)),
            pl.BlockSpec((1, 8, LANES), lambda b, g, i: (g, 0, 0)),
            _resident(ovt.shape, lambda b, g, i: (0, 0)),
        ],
        out_specs=pl.BlockSpec((1, QB, R * NSA_DH), lambda b, g, i: (b, i, g)),
        out_shape=jax.ShapeDtypeStruct((B, S, G * R * NSA_DH), BF),
        scratch_shapes=[
            pltpu.VMEM((S + SLC_TILE, LANES), BF),
            pltpu.VMEM((S + WINDOW, LANES), BF),
            pltpu.VMEM(((S + SLC_TILE) // LANES, VROWS, LANES), BF),
            pltpu.VMEM(((S + WINDOW) // LANES, VROWS, LANES), BF),
            pltpu.VMEM((VROWS, n_cmp), BF),
            pltpu.VMEM((VROWS, R * QB), F32),
            pltpu.SMEM((S // SLC_TILE + 1,), jnp.int32),
        ],
        compiler_params=_params(("parallel", "parallel", "arbitrary")),
    )(proj3, proj3, proj3, proj3, proj3, kc, vc, small3, slope_tab, ovt)


def _merge_kernel(og_ref, on_ref, mg_ref, mn_ref, x_ref, wg_ref, wn_ref, wo_ref, g_ref, o_ref):
    a = jnp.dot(og_ref[...], wg_ref[...], preferred_element_type=F32)
    b = jnp.dot(on_ref[...], wn_ref[...], preferred_element_type=F32)
    mixed = jax.nn.sigmoid(mg_ref[...].astype(F32)) * a + jax.nn.sigmoid(mn_ref[...].astype(F32)) * b
    y = jnp.dot(mixed.astype(BF), wo_ref[...], preferred_element_type=F32)
    y = y * lax.rsqrt(jnp.mean(y * y, axis=-1, keepdims=True) + EPS) * g_ref[...]
    o_ref[...] = x_ref[...] + y


def _merge_out(o_gla2, o_nsa2, proj2, x2, wg, wn, wo, g, tm=512):
    n_tok = x2.shape[0]
    D = D_MODEL
    return pl.pallas_call(
        _merge_kernel,
        grid=(n_tok // tm,),
        in_specs=[
            pl.BlockSpec((tm, D), lambda i: (i, 0)),
            pl.BlockSpec((tm, o_nsa2.shape[1]), lambda i: (i, 0)),
            pl.BlockSpec((tm, D), lambda i: (i, OFF_MG // D)),
            pl.BlockSpec((tm, D), lambda i: (i, OFF_MN // D)),
            pl.BlockSpec((tm, D), lambda i: (i, 0)),
            _resident(wg.shape, lambda i: (0, 0)),
            _resident(wn.shape, lambda i: (0, 0)),
            _resident(wo.shape, lambda i: (0, 0)),
            _resident((1, D), lambda i: (0, 0)),
        ],
        out_specs=pl.BlockSpec((tm, D), lambda i: (i, 0)),
        out_shape=jax.ShapeDtypeStruct((n_tok, D), F32),
        compiler_params=_params(("parallel",)),
    )(o_gla2, o_nsa2, proj2, proj2, x2, wg, wn, wo, g)


def _ffn_kernel(x_ref, gpre_ref, wg_ref, wu_ref, wd_ref, gpost_ref, o_ref, acc_scr, *, chunk):
    x = x_ref[...]
    h = (x * lax.rsqrt(jnp.mean(x * x, axis=-1, keepdims=True) + EPS) * gpre_ref[...]).astype(BF)
    d_ff = wg_ref.shape[1]
    for n, c0 in enumerate(range(0, d_ff, chunk)):
        c1 = min(c0 + chunk, d_ff)
        a = jnp.dot(h, wg_ref[:, c0:c1], preferred_element_type=F32)
        u = jnp.dot(h, wu_ref[:, c0:c1], preferred_element_type=F32)
        t = (a * jax.nn.sigmoid(a) * u).astype(BF)
        part = jnp.dot(t, wd_ref[c0:c1, :], preferred_element_type=F32)
        if n == 0:
            acc_scr[...] = part
        else:
            acc_scr[...] += part
    f = acc_scr[...]
    o_ref[...] = x + f * lax.rsqrt(jnp.mean(f * f, axis=-1, keepdims=True) + EPS) * gpost_ref[...]


def _ffn(x2, gpre, wg, wu, wd, gpost, tm=512, chunk=512):
    n_tok = x2.shape[0]
    D = D_MODEL
    kern = functools.partial(_ffn_kernel, chunk=chunk)
    return pl.pallas_call(
        kern,
        grid=(n_tok // tm,),
        in_specs=[
            pl.BlockSpec((tm, D), lambda i: (i, 0)),
            _resident((1, D), lambda i: (0, 0)),
            _resident(wg.shape, lambda i: (0, 0)),
            _resident(wu.shape, lambda i: (0, 0)),
            _resident(wd.shape, lambda i: (0, 0)),
            _resident((1, D), lambda i: (0, 0)),
        ],
        out_specs=pl.BlockSpec((tm, D), lambda i: (i, 0)),
        out_shape=jax.ShapeDtypeStruct((n_tok, D), F32),
        scratch_shapes=[pltpu.VMEM((tm, D), F32)],
        compiler_params=_params(("parallel",)),
    )(x2, gpre, wg, wu, wd, gpost)


def _prep_in_weights(w_in):
    splits = np.cumsum([512, 512, 1024, 1024, GLA_RANK, 512, 128, 128, 128, 128, 128, 128, 24, 1024])
    (g_q, g_k, g_v, g_r, g_a, n_q, n_kc, n_vc, n_ks, n_vs, n_kw, n_vw, n_gate, m_g, m_n) = jnp.split(
        w_in, [int(s) for s in splits], axis=1)
    w_main = jnp.concatenate([g_q, g_k, g_v, g_r, m_g, m_n, n_q * (NSA_DH ** -0.5),
                              n_kc, n_vc, n_ks, n_vs, n_kw, n_vw], axis=1).astype(BF)
    d = w_in.shape[0]
    per_g = NSA_REP * 3
    gates = jnp.pad(n_gate.reshape(d, NSA_GROUPS, per_g), ((0, 0), (0, 0), (0, LANES - per_g)))
    g_a3 = jnp.pad(jnp.concatenate([g_a, g_a, g_a], axis=1), ((0, 0), (0, LANES - 3 * GLA_RANK)))
    w_small = jnp.concatenate([g_a3, gates.reshape(d, NSA_GROUPS * LANES)], axis=1).astype(BF)
    return w_main, w_small


def _prep_compress(pe, w1, w2):
    eye = jnp.eye(NSA_GROUPS, dtype=F32)
    w1r = w1.reshape(CMP_LEN, NSA_DH, NSA_DH)
    w1e = jnp.einsum('lde,gh->lgdhe', w1r, eye).reshape(CMP_LEN * NSA_GROUPS * NSA_DH, NSA_GROUPS * NSA_DH)
    pe_e = jnp.broadcast_to(pe[:, None, :], (CMP_LEN, NSA_GROUPS, NSA_DH)).reshape(2, CMP_STRIDE * LANES)
    pe_e = jnp.pad(pe_e, ((0, 6), (0, 0)))
    w2e = jnp.stack([
        jnp.pad(jnp.pad(w2, ((g * NSA_DH, (NSA_GROUPS - 1 - g) * NSA_DH), (0, 0))), ((0, 0), (0, LANES - NSA_DH)))
        for g in range(NSA_GROUPS)])
    return pe_e.astype(F32), w1e.astype(BF), w2e.astype(BF)


def _overlap_table(seq):
    n_cmp = (seq - CMP_LEN) // CMP_STRIDE + 1
    n_slc = seq // SLC_LEN
    sc = CMP_STRIDE * np.arange(n_cmp)
    ss = SLC_LEN * np.arange(n_slc)
    ov = np.clip(np.minimum(sc[:, None] + CMP_LEN, ss[None, :] + SLC_LEN)
                 - np.maximum(sc[:, None], ss[None, :]), 0, None).astype(np.float32) / CMP_LEN
    ovt = np.zeros((NSA_DH, n_cmp + 1), np.float32)
    ovt[:n_slc, :n_cmp] = ov.T
    return jnp.asarray(ovt, dtype=BF)


def kernel(x, norm_mix_pre, norm_mix_post, norm_ffn_pre, norm_ffn_post, w_in, gla_w_alpha2, gla_b_alpha, gla_norm_g, nsa_cmp_pe_k, nsa_cmp_w1_k, nsa_cmp_w2_k, nsa_cmp_pe_v, nsa_cmp_w1_v, nsa_cmp_w2_v, w_proj_gla, w_proj_nsa, w_out, w_ffn_gate, w_ffn_up, w_ffn_down):
    B, S, D = x.shape
    depth = w_in.shape[0]
    n_tok = B * S
    h_idx = jnp.arange(NSA_HEADS, dtype=F32)
    slopes = jnp.exp2(-8.0 * (h_idx + 1.0) / NSA_HEADS).reshape(NSA_GROUPS, NSA_REP, 1)
    slope_tab = jnp.broadcast_to(jnp.pad(slopes, ((0, 0), (0, 8 - NSA_REP), (0, 0))), (NSA_GROUPS, 8, LANES))
    ovt = _overlap_table(S)
    x2 = x.reshape(n_tok, D)
    for l in range(depth):
        w_main, w_small = _prep_in_weights(w_in[l])
        proj2, small2 = _in_proj(x2, norm_mix_pre[l][None, :], w_main, w_small)
        proj3 = proj2.reshape(B, S, N_MAIN)
        small3 = small2.reshape(B, S, N_SMALL)

        w2_hi = gla_w_alpha2[l].astype(BF)
        w2_lo = (gla_w_alpha2[l] - w2_hi.astype(F32)).astype(BF)
        w2p = jnp.pad(jnp.concatenate([w2_hi, w2_hi, w2_lo], axis=0), ((0, LANES - 3 * GLA_RANK), (0, 0)))
        o_gla = _gla(proj3, small3, w2p, gla_b_alpha[l][None, :], gla_norm_g[l][None, :])

        xk = proj3[:, :, OFF_KC:OFF_KC + LANES].reshape(B, S // CMP_STRIDE, CMP_STRIDE * LANES)
        xv = proj3[:, :, OFF_VC:OFF_VC + LANES].reshape(B, S // CMP_STRIDE, CMP_STRIDE * LANES)
        pek, w1k, w2k = _prep_compress(nsa_cmp_pe_k[l], nsa_cmp_w1_k[l], nsa_cmp_w2_k[l])
        pev, w1v, w2v = _prep_compress(nsa_cmp_pe_v[l], nsa_cmp_w1_v[l], nsa_cmp_w2_v[l])
        kc, vc = _nsa_compress(xk, xv, pek, pev, w1k, w1v, w2k, w2v)
        o_nsa = _nsa_attend(proj3, small3, kc, vc, slope_tab, ovt)

        x2 = _merge_out(o_gla.reshape(n_tok, -1), o_nsa.reshape(n_tok, -1), proj2, x2,
                        w_proj_gla[l].astype(BF), w_proj_nsa[l].astype(BF),
                        w_out[l].astype(BF), norm_mix_post[l][None, :])
        x2 = _ffn(x2, norm_ffn_pre[l][None, :], w_ffn_gate[l].astype(BF), w_ffn_up[l].astype(BF),
                  w_ffn_down[l].astype(BF), norm_ffn_post[l][None, :])
    return x2.reshape(B, S, D)
```

```python
import functools

import numpy as np
import jax
import jax.numpy as jnp
from jax import lax
from jax.experimental import pallas as pl
from jax.experimental.pallas import tpu as pltpu

D_MODEL = 1024
GLA_HEADS = 4
GLA_DK = 128
GLA_DV = 256
GLA_RANK = 16
GLA_TAU = 16.0
GLA_CHUNK = 64
NSA_HEADS = 8
NSA_GROUPS = 2
NSA_REP = 4
NSA_DH = 64
CMP_LEN = 32
CMP_STRIDE = 16
SLC_LEN = 64
N_SEL = 16
WINDOW = 512
QB = 256
SLC_TILE = 256
VROWS = 80
D_FF = 2816
EPS = 1e-6
NEG = -1e30

LANES = 128
VMEM_LIMIT = 56 * 1024 * 1024
BF = jnp.bfloat16
F32 = jnp.float32

OFF_GQ = 0
OFF_GK = 512
OFF_GV = 1024
OFF_GR = 2048
OFF_MG = 3072
OFF_MN = 4096
OFF_NQ = 5120
OFF_KC = 5632
OFF_VC = 5760
OFF_KS = 5888
OFF_VS = 6016
OFF_KW = 6144
OFF_VW = 6272
N_MAIN = 6400
N_SMALL = 384

NT = (((1,), (1,)), ((), ()))
TN = (((0,), (0,)), ((), ()))


LOG2E = 1.4426950408889634
POS_TERMS = 3


def _pos_columns(lane, pos):
    first = lane - NSA_DH
    return jnp.where((first >= 0) & (first < POS_TERMS), pos >> 6,
                     jnp.where((first >= POS_TERMS) & (first < 2 * POS_TERMS), pos & 63, 0))


def _slope_rows(rowi, coef):
    terms = []
    rem = coef
    for _ in range(POS_TERMS):
        t = rem.astype(BF).astype(F32)
        terms.append(t)
        rem = rem - t
    out = jnp.zeros(rowi.shape, F32)
    for i, t in enumerate(terms):
        out = jnp.where(rowi == i, t * float(SLC_LEN), jnp.where(rowi == POS_TERMS + i, t, out))
    return out


def _resident(shape, index_map):
    return pl.BlockSpec(shape, index_map, pipeline_mode=pl.Buffered(1))


def _params(sem):
    return pltpu.CompilerParams(dimension_semantics=sem, vmem_limit_bytes=VMEM_LIMIT)


def _in_proj_kernel(x_ref, g_ref, wm_ref, ws_ref, om_ref, os_ref):
    x = x_ref[...]
    h = (x * lax.rsqrt(jnp.mean(x * x, axis=-1, keepdims=True) + EPS) * g_ref[...]).astype(BF)
    os_ref[...] = jnp.dot(h, ws_ref[...], preferred_element_type=F32)
    n_out = om_ref.shape[1]
    step = 512
    for c0 in range(0, n_out, step):
        c1 = min(c0 + step, n_out)
        om_ref[:, c0:c1] = jnp.dot(h, wm_ref[:, c0:c1], preferred_element_type=F32).astype(BF)


def _in_proj(x2, g, w_main, w_small, tm=512):
    n_tok = x2.shape[0]
    return pl.pallas_call(
        _in_proj_kernel,
        grid=(n_tok // tm,),
        in_specs=[
            pl.BlockSpec((tm, D_MODEL), lambda i: (i, 0)),
            _resident((1, D_MODEL), lambda i: (0, 0)),
            _resident((D_MODEL, N_MAIN), lambda i: (0, 0)),
            _resident((D_MODEL, N_SMALL), lambda i: (0, 0)),
        ],
        out_specs=[
            pl.BlockSpec((tm, N_MAIN), lambda i: (i, 0)),
            pl.BlockSpec((tm, N_SMALL), lambda i: (i, 0)),
        ],
        out_shape=[
            jax.ShapeDtypeStruct((n_tok, N_MAIN), BF),
            jax.ShapeDtypeStruct((n_tok, N_SMALL), F32),
        ],
        compiler_params=_params(("parallel",)),
    )(x2, g, w_main, w_small)


def _gla_kernel(q_ref, k_ref, v_ref, r_ref, a_ref, w2_ref, b2_ref, ng_ref, o_ref, st_scr, *, n_chunks, n_heads):
    blk = pl.program_id(2)

    @pl.when(blk == 0)
    def _():
        st_scr[...] = jnp.zeros_like(st_scr)

    C = GLA_CHUNK
    T = n_chunks * C
    W = n_heads * GLA_DK

    def split3(x):
        hi = x.astype(BF)
        rem = x - hi.astype(F32)
        mid = rem.astype(BF)
        return hi, mid, (rem - mid.astype(F32)).astype(BF)

    a = a_ref[0]
    a_hi = a.astype(BF)
    a_lo = (a - a_hi.astype(F32)).astype(BF)
    lane = lax.broadcasted_iota(jnp.int32, (T, LANES), 1)
    in_lo = (lane >= GLA_RANK) & (lane < 2 * GLA_RANK)
    z = jnp.dot(jnp.where(in_lo, a_lo, a_hi), w2_ref[...], preferred_element_type=F32) + b2_ref[...]
    log_a = (jnp.minimum(z, 0.0) - jnp.log1p(jnp.exp(-jnp.abs(z)))) * (1.0 / GLA_TAU)

    x_wide = jnp.concatenate([log_a[c * C:(c + 1) * C] for c in range(n_chunks)], axis=1)
    x3 = jnp.concatenate(split3(x_wide), axis=0)
    r3 = lax.broadcasted_iota(jnp.int32, (C, 3 * C), 0)
    c3 = lax.broadcasted_iota(jnp.int32, (C, 3 * C), 1) & (C - 1)
    tri3 = jnp.where(c3 <= r3, 1.0, 0.0).astype(BF)
    b_wide = jnp.dot(tri3, x3, preferred_element_type=F32)
    bcum = jnp.concatenate([b_wide[:, c * W:(c + 1) * W] for c in range(n_chunks)], axis=0)
    last_rows = [b_wide[C - 1:C, c * W:(c + 1) * W] for c in range(n_chunks)]
    b_last = jnp.concatenate([jnp.broadcast_to(lr, (C, W)) for lr in last_rows], axis=0)
    decay = [jnp.exp(lr) for lr in last_rows]

    q = q_ref[0].astype(F32)
    k = k_ref[0].astype(F32)
    v = v_ref[0]
    qe = (q * ((GLA_DK ** -0.5) * jnp.exp(bcum))).astype(BF)
    ke = (k * jnp.exp(-bcum)).astype(BF)
    kd = (k * jnp.exp(b_last - bcum)).astype(BF)

    H = min(T, 4 * C)
    row = lax.broadcasted_iota(jnp.int32, (H, H), 0)
    col = lax.broadcasted_iota(jnp.int32, (H, H), 1)
    keep = (col <= row) & ((col >> 6) == (row >> 6))
    ng = ng_ref[...]
    r_all = r_ref[0].astype(F32)
    for hh in range(n_heads):
        ks_ = slice(hh * GLA_DK, (hh + 1) * GLA_DK)
        vs_ = slice(hh * GLA_DV, (hh + 1) * GLA_DV)
        intra = []
        for h0 in range(0, T, H):
            hs = slice(h0, h0 + H)
            attn = lax.dot_general(qe[hs, ks_], ke[hs, ks_], NT, preferred_element_type=F32)
            intra.append(jnp.dot(jnp.where(keep, attn, 0.0).astype(BF), v[hs, vs_], preferred_element_type=F32))
        o = jnp.concatenate(intra, axis=0)

        st = st_scr[hh]
        inter = []
        for c in range(n_chunks):
            sl = slice(c * C, (c + 1) * C)
            inter.append(lax.dot_general(qe[sl, ks_], st.astype(BF), NT, preferred_element_type=F32))
            upd = lax.dot_general(v[sl, vs_], kd[sl, ks_], TN, preferred_element_type=F32)
            st = st * decay[c][:, ks_] + upd
        st_scr[hh] = st
        o = o + jnp.concatenate(inter, axis=0)

        o = o * lax.rsqrt(jnp.mean(o * o, axis=-1, keepdims=True) + EPS) * ng
        r = r_all[:, vs_]
        o_ref[0, :, vs_] = (o * (r * jax.nn.sigmoid(r))).astype(BF)


def _gla(proj3, small3, w2p, b2, ng, blk_tokens=512, heads_per_step=4):
    B, S, _ = proj3.shape
    nblk = S // blk_tokens
    hb = heads_per_step
    kern = functools.partial(_gla_kernel, n_chunks=blk_tokens // GLA_CHUNK, n_heads=hb)
    wk, wv = hb * GLA_DK, hb * GLA_DV
    return pl.pallas_call(
        kern,
        grid=(B, GLA_HEADS // hb, nblk),
        in_specs=[
            pl.BlockSpec((1, blk_tokens, wk), lambda b, h, i: (b, i, OFF_GQ // wk + h)),
            pl.BlockSpec((1, blk_tokens, wk), lambda b, h, i: (b, i, OFF_GK // wk + h)),
            pl.BlockSpec((1, blk_tokens, wv), lambda b, h, i: (b, i, OFF_GV // wv + h)),
            pl.BlockSpec((1, blk_tokens, wv), lambda b, h, i: (b, i, OFF_GR // wv + h)),
            pl.BlockSpec((1, blk_tokens, LANES), lambda b, h, i: (b, i, 0)),
            pl.BlockSpec((LANES, wk), lambda b, h, i: (0, h)),
            pl.BlockSpec((1, wk), lambda b, h, i: (0, h)),
            pl.BlockSpec((1, GLA_DV), lambda b, h, i: (0, 0)),
        ],
        out_specs=pl.BlockSpec((1, blk_tokens, wv), lambda b, h, i: (b, i, h)),
        out_shape=jax.ShapeDtypeStruct((B, S, GLA_HEADS * GLA_DV), BF),
        scratch_shapes=[pltpu.VMEM((hb, GLA_DV, GLA_DK), F32)],
        compiler_params=_params(("parallel", "parallel", "arbitrary")),
    )(proj3, proj3, proj3, proj3, small3, w2p, b2, ng)


def _compress_kernel(xk_ref, xv_ref, pek_ref, pev_ref, w1k_ref, w1v_ref, w2k_ref, w2v_ref, kc_ref, vc_ref):
    n_rows = xk_ref.shape[1]
    half = CMP_STRIDE * LANES
    lane = lax.broadcasted_iota(jnp.int32, (n_rows, LANES), 1)
    row = lax.broadcasted_iota(jnp.int32, (n_rows, LANES), 0)
    end_c = CMP_STRIDE * row + (CMP_LEN - 1)
    c_k = _pos_columns(lane, end_c).astype(F32)
    c_v = jnp.where(lane == NSA_DH, 1.0, 0.0).astype(F32)

    def branch(x_ref, pe_ref, w1_ref, w2_ref, const, o_ref):
        x = x_ref[0].astype(F32)
        xa = (x + pe_ref[0:1, :]).astype(BF)
        xb = (x + pe_ref[1:2, :]).astype(BF)
        a = jnp.dot(xa, w1_ref[0:half, :], preferred_element_type=F32)
        b = jnp.dot(xb, w1_ref[half:2 * half, :], preferred_element_type=F32)
        pre = a + pltpu.roll(b, n_rows - 1, 0)
        hid = (pre * jax.nn.sigmoid(pre)).astype(BF)
        for g in range(NSA_GROUPS):
            o_ref[0, g] = (jnp.dot(hid, w2_ref[g], preferred_element_type=F32) + const).astype(BF)

    branch(xk_ref, pek_ref, w1k_ref, w2k_ref, c_k, kc_ref)
    branch(xv_ref, pev_ref, w1v_ref, w2v_ref, c_v, vc_ref)


def _nsa_compress(xk, xv, pek, pev, w1k, w1v, w2k, w2v):
    B, n_rows, width = xk.shape
    full = lambda shape: _resident(shape, lambda b: (0,) * len(shape))
    out = jax.ShapeDtypeStruct((B, NSA_GROUPS, n_rows, LANES), BF)
    return pl.pallas_call(
        _compress_kernel,
        grid=(B,),
        in_specs=[
            pl.BlockSpec((1, n_rows, width), lambda b: (b, 0, 0)),
            pl.BlockSpec((1, n_rows, width), lambda b: (b, 0, 0)),
            full(pek.shape), full(pev.shape), full(w1k.shape), full(w1v.shape),
            full(w2k.shape), full(w2v.shape),
        ],
        out_specs=[pl.BlockSpec((1, NSA_GROUPS, n_rows, LANES), lambda b: (b, 0, 0, 0))] * 2,
        out_shape=[out, out],
        compiler_params=_params(("parallel",)),
    )(xk, xv, pek, pev, w1k, w1v, w2k, w2v)


def _nsa_kernel(q_ref, ks_ref, vs_ref, kw_ref, vw_ref, kc_ref, vc_ref, gt_ref, sl_ref, ov_ref,
                o_ref, ksa, kwa, vst, vwt, vct, acc_scr, act_ref, *, seq):
    qi = pl.program_id(2)
    n_slc = seq // SLC_LEN
    R = NSA_REP
    KT = SLC_TILE
    WK = WINDOW + QB
    n_cmp = kc_ref.shape[2]

    def t_bf(x):
        return x.astype(F32).T[0:VROWS].astype(BF)

    @pl.when(qi == 0)
    def _():
        grp = pl.program_id(1)
        lane = lax.broadcasted_iota(jnp.int32, (seq, LANES), 1)
        pos = lax.broadcasted_iota(jnp.int32, (seq, LANES), 0)
        blk = pos >> 6
        off = pos & 63
        r_i = lax.broadcasted_iota(jnp.int32, (LANES, LANES), 0)
        c_i = lax.broadcasted_iota(jnp.int32, (LANES, LANES), 1)
        pick = jnp.where((c_i < NSA_DH) & (r_i == c_i + grp * NSA_DH), 1.0, 0.0).astype(BF)
        c_s = jnp.where(lane == LANES - 1, off,
                        jnp.where((lane >= NSA_DH) & (lane - (NSA_DH - 1) == blk), 1, 0))
        ksa[0:seq, :] = (jnp.dot(ks_ref[0], pick, preferred_element_type=F32) + c_s.astype(F32)).astype(BF)
        lane_d = lax.broadcasted_iota(jnp.int32, (KT, LANES), 1)
        ksa[seq:seq + KT, :] = jnp.where((lane_d >= NSA_DH) & (lane_d < LANES - 1), 1, 0).astype(BF)
        for c in range(KT // LANES):
            vst[seq // LANES + c] = jnp.zeros((VROWS, LANES), BF)
        c_w = _pos_columns(lane, pos)
        kwa[0:WINDOW, :] = jnp.zeros((WINDOW, LANES), BF)
        kwa[WINDOW:WINDOW + seq, :] = (jnp.dot(kw_ref[0], pick, preferred_element_type=F32)
                                       + c_w.astype(F32)).astype(BF)
        ones_rows = jnp.where(lax.broadcasted_iota(jnp.int32, (VROWS - NSA_DH, LANES), 0) == 0, 1.0, 0.0)
        n_pad = WINDOW // LANES
        for c in range(n_pad):
            vwt[c] = jnp.zeros((VROWS, LANES), BF)

        def v_tile(x):
            xt = x.astype(F32).T
            dims = jnp.where(grp == 0, xt[0:NSA_DH], xt[NSA_DH:2 * NSA_DH])
            return jnp.concatenate([dims, ones_rows], axis=0).astype(BF)

        def fill(c, carry):
            rows = pl.ds(pl.multiple_of(c * LANES, LANES), LANES)
            vst[c] = v_tile(vs_ref[0, rows, :])
            vwt[c + n_pad] = v_tile(vw_ref[0, rows, :])
            return carry

        lax.fori_loop(0, seq // LANES, fill, 0)
        for c in range(n_cmp // LANES):
            vct[:, c * LANES:(c + 1) * LANES] = t_bf(vc_ref[0, 0, c * LANES:(c + 1) * LANES, :])

    q0 = qi * QB
    rowi = lax.broadcasted_iota(jnp.int32, (NSA_DH, QB), 0)
    slopes = [jnp.concatenate([sl_ref[0, r:r + 1, :]] * (QB // LANES), axis=1) for r in range(R)]
    q_all = q_ref[0].astype(F32).T
    q_t = [q_all[r * NSA_DH:(r + 1) * NSA_DH] for r in range(R)]

    qw = jnp.concatenate(
        [jnp.concatenate([q_t[r] * LOG2E, _slope_rows(rowi, slopes[r] * LOG2E)], axis=0).astype(BF)
         for r in range(R)], axis=1)

    def tile4(x):
        return jnp.concatenate([x] * R, axis=1)

    ovt = ov_ref[...]

    def cmp_branch(rows):
        def fn():
            s_c = jnp.dot(kc_ref[0, 0, 0:rows, :], qw, preferred_element_type=F32)
            e_c = CMP_STRIDE * lax.broadcasted_iota(jnp.int32, (rows, QB), 0) + (CMP_LEN - 1)
            t_c = q0 + lax.broadcasted_iota(jnp.int32, (rows, QB), 1)
            s_c = s_c + tile4(jnp.where(e_c <= t_c, 0.0, NEG))
            m_c = jnp.maximum(jnp.max(s_c, axis=0, keepdims=True), 0.1 * NEG)
            p_c = jnp.exp2(s_c - m_c)
            l_c = jnp.sum(p_c, axis=0, keepdims=True)
            p_c = p_c * jnp.where(l_c > 0.0, 1.0 / l_c, 0.0)
            o_cmp = jnp.dot(vct[:, 0:rows], p_c.astype(BF), preferred_element_type=F32)
            psum = p_c[:, 0:QB]
            for r in range(1, R):
                psum = psum + p_c[:, r * QB:(r + 1) * QB]
            p_hi = psum.astype(BF)
            rem = psum - p_hi.astype(F32)
            p_mid = rem.astype(BF)
            p_lo = (rem - p_mid.astype(F32)).astype(BF)
            ov = ovt[:, 0:rows]
            return o_cmp, (jnp.dot(ov, p_hi, preferred_element_type=F32)
                           + jnp.dot(ov, p_mid, preferred_element_type=F32)
                           + jnp.dot(ov, p_lo, preferred_element_type=F32))
        return fn

    half = n_cmp // 2
    o_c, imp = lax.cond(q0 + QB <= CMP_STRIDE * half + CMP_LEN - 1, cmp_branch(half), cmp_branch(n_cmp))

    NR = ov_ref.shape[0]
    SUB = 8
    jblk = lax.broadcasted_iota(jnp.int32, (NR, QB), 0)
    t_q = q0 + lax.broadcasted_iota(jnp.int32, (NR, QB), 1)
    cur = t_q >> 6
    forced = (jblk == 0) | (jblk == cur) | (jblk == cur - 1)
    score = jnp.where(jblk > cur, NEG, jnp.where(forced, -NEG, imp))
    n_slab = -(-n_slc // SUB)
    isub = lax.broadcasted_iota(jnp.int32, (SUB, QB), 0)

    def rank_counts(ns):
        def fn():
            slabs = [score[a * SUB:(a + 1) * SUB, :] for a in range(ns)]
            cnts = [jnp.zeros((SUB, QB), F32) for _ in range(ns)]
            for jp in range(min(ns * SUB, n_slc)):
                rowv = jnp.broadcast_to(score[jp:jp + 1, :], (SUB, QB))
                for a in range(ns):
                    if a < jp // SUB:
                        beats = jnp.where(rowv > slabs[a], 1.0, 0.0)
                    elif a > jp // SUB:
                        beats = jnp.where(rowv >= slabs[a], 1.0, 0.0)
                    else:
                        beats = jnp.where(isub > jp % SUB, jnp.where(rowv >= slabs[a], 1.0, 0.0),
                                          jnp.where(rowv > slabs[a], 1.0, 0.0))
                    cnts[a] = cnts[a] + beats
            rest = NR - ns * SUB
            return jnp.concatenate(cnts + [jnp.full((rest, QB), float(NR), F32)] * (rest > 0), axis=0)
        return fn

    last_blk = (q0 + QB - 1) >> 6
    cnt = lax.switch(last_blk // SUB, [rank_counts(ns) for ns in range(1, n_slab + 1)])
    sel = (cnt < float(N_SEL)) & (jblk <= cur) & (jblk < n_slc)
    a_nat = jnp.where(jblk == 0, 1.0, jnp.where(sel, (SLC_LEN * jblk).astype(F32), NEG))
    a_nat = jnp.where(jblk < n_slc, a_nat, 0.0)
    a_t = pltpu.roll(a_nat, NR - 1, 0)

    any_q = jnp.max(jnp.where(sel, 1.0, 0.0), axis=1, keepdims=True)
    jcol = lax.broadcasted_iota(jnp.int32, (NR, 1), 0)
    bits = jnp.where(any_q > 0.0, lax.shift_left(jnp.int32(1), jcol & 31), 0)
    word0 = jnp.sum(jnp.where(jcol < 32, bits, 0))
    word1 = jnp.sum(jnp.where(jcol >= 32, bits, 0))

    qs = jnp.concatenate([jnp.concatenate([q_t[r], a_t * slopes[r]], axis=0).astype(BF) for r in range(R)],
                         axis=1)

    tiles_per = KT // LANES
    blocks_per = KT // SLC_LEN
    n_full = q0 // KT

    def scan(ti, n):
        word = jnp.where(ti < 32 // blocks_per, word0, word1)
        hit = (lax.shift_right_logical(word, (ti * blocks_per) & 31) & ((1 << blocks_per) - 1)) != 0
        act_ref[n] = ti
        return n + jnp.where(hit, 1, 0)

    n_act = lax.fori_loop(0, n_full, scan, 0)
    act_ref[n_act] = seq // KT

    kw = kwa[pl.ds(pl.multiple_of(q0, QB), WK), :]
    s_w = jnp.dot(kw, qw, preferred_element_type=F32)
    c_w = lax.broadcasted_iota(jnp.int32, (WK, QB), 0)
    d_w = c_w - lax.broadcasted_iota(jnp.int32, (WK, QB), 1)
    mask_w = (d_w > 0) & (d_w <= WINDOW) & (c_w >= WINDOW - q0)
    s_w = s_w + tile4(jnp.where(mask_w, 0.0, NEG))
    m_w = jnp.max(s_w, axis=0, keepdims=True)
    p_w = jnp.exp2(s_w - m_w).astype(BF)

    def scores(ti):
        k0 = pl.multiple_of(ti * KT, KT)
        return jnp.dot(ksa[pl.ds(k0, KT), :], qs, preferred_element_type=F32)

    def values_t(ti):
        return [vst[ti * tiles_per + c] for c in range(tiles_per)]

    def absorb(s, v_tiles, m_prev):
        m_new = jnp.maximum(m_prev, jnp.max(s, axis=0, keepdims=True))
        alpha = jnp.exp(m_prev - m_new)
        p = jnp.exp(s - m_new).astype(BF)
        acc_scr[...] = acc_scr[...] * alpha + jnp.dot(jnp.concatenate(v_tiles, axis=1), p,
                                                     preferred_element_type=F32)
        return m_new

    acc_scr[...] = jnp.zeros(acc_scr.shape, F32)

    def chains(first, count, m_run):
        tiles = [act_ref[first + i] for i in range(count)]
        s_all = [scores(t) for t in tiles]
        for t, s in zip(tiles, s_all):
            m_run = absorb(s, values_t(t), m_run)
        return m_run

    n_quad = n_act >> 2
    m_d = lax.fori_loop(0, n_quad, lambda j, m: chains(4 * j, 4, m), jnp.full((1, R * QB), NEG, F32))
    n_rest = n_act - 4 * n_quad
    m_d = lax.fori_loop(0, (n_rest + 1) >> 1, lambda j, m: chains(4 * n_quad + 2 * j, 2, m), m_d)
    for d in range(QB // KT):
        ti = n_full + d
        p_d = ti * KT + lax.broadcasted_iota(jnp.int32, (KT, QB), 0)
        t_d = q0 + lax.broadcasted_iota(jnp.int32, (KT, QB), 1)
        m_d = absorb(scores(ti) + tile4(jnp.where(p_d <= t_d, 0.0, NEG)), values_t(ti), m_d)
    acc_s = acc_scr[...]

    vw_t = jnp.concatenate([vwt[qi * (QB // LANES) + c] for c in range(WK // LANES)], axis=1)
    acc_w = jnp.dot(vw_t, p_w, preferred_element_type=F32)


    sg_t = jax.nn.sigmoid(gt_ref[0]).T
    outs = []
    for r in range(R):
        cs = slice(r * QB, (r + 1) * QB)
        a_s = acc_s[:, cs]
        a_w = acc_w[:, cs]
        g_c = sg_t[3 * r:3 * r + 1, :]
        g_s = sg_t[3 * r + 1:3 * r + 2, :] / a_s[NSA_DH:NSA_DH + 1, :]
        g_w = sg_t[3 * r + 2:3 * r + 3, :] / a_w[NSA_DH:NSA_DH + 1, :]
        outs.append((g_c * o_c[:, cs] + g_s * a_s + g_w * a_w)[0:NSA_DH])
    o_ref[0] = jnp.concatenate(outs, axis=0).T.astype(BF)


def _nsa_attend(proj3, small3, kc, vc, slope_tab, ovt):
    B, S, _ = proj3.shape
    G, R = NSA_GROUPS, NSA_REP
    n_cmp = kc.shape[2]
    kern = functools.partial(_nsa_kernel, seq=S)
    kv_spec = lambda off: pl.BlockSpec((1, S, LANES), lambda b, g, i: (b, 0, off // LANES))
    return pl.pallas_call(
        kern,
        grid=(B, G, S // QB),
        in_specs=[
            pl.BlockSpec((1, QB, R * NSA_DH), lambda b, g, i: (b, i, OFF_NQ // (R * NSA_DH) + g)),
            kv_spec(OFF_KS), kv_spec(OFF_VS), kv_spec(OFF_KW), kv_spec(OFF_VW),
            pl.BlockSpec((1, 1, n_cmp, LANES), lambda b, g, i: (b, g, 0, 0)),
            pl.BlockSpec((1, 1, n_cmp, LANES), lambda b, g, i: (b, g, 0, 0)),
            pl.BlockSpec((1, QB, LANES), lambda b, g, i: (b, i, 1 + g)),
            pl.BlockSpec((1, 8, LANES), lambda b, g, i: (g, 0, 0)),
            _resident(ovt.shape, lambda b, g, i: (0, 0)),
        ],
        out_specs=pl.BlockSpec((1, QB, R * NSA_DH), lambda b, g, i: (b, i, g)),
        out_shape=jax.ShapeDtypeStruct((B, S, G * R * NSA_DH), BF),
        scratch_shapes=[
            pltpu.VMEM((S + SLC_TILE, LANES), BF),
            pltpu.VMEM((S + WINDOW, LANES), BF),
            pltpu.VMEM(((S + SLC_TILE) // LANES, VROWS, LANES), BF),
            pltpu.VMEM(((S + WINDOW) // LANES, VROWS, LANES), BF),
            pltpu.VMEM((VROWS, n_cmp), BF),
            pltpu.VMEM((VROWS, R * QB), F32),
            pltpu.SMEM((S // SLC_TILE + 1,), jnp.int32),
        ],
        compiler_params=_params(("parallel", "parallel", "arbitrary")),
    )(proj3, proj3, proj3, proj3, proj3, kc, vc, small3, slope_tab, ovt)


def _merge_kernel(og_ref, on_ref, mg_ref, mn_ref, x_ref, wg_ref, wn_ref, wo_ref, g_ref, o_ref):
    a = jnp.dot(og_ref[...], wg_ref[...], preferred_element_type=F32)
    b = jnp.dot(on_ref[...], wn_ref[...], preferred_element_type=F32)
    mixed = jax.nn.sigmoid(mg_ref[...].astype(F32)) * a + jax.nn.sigmoid(mn_ref[...].astype(F32)) * b
    y = jnp.dot(mixed.astype(BF), wo_ref[...], preferred_element_type=F32)
    y = y * lax.rsqrt(jnp.mean(y * y, axis=-1, keepdims=True) + EPS) * g_ref[...]
    o_ref[...] = x_ref[...] + y


def _merge_out(o_gla2, o_nsa2, proj2, x2, wg, wn, wo, g, tm=512):
    n_tok = x2.shape[0]
    D = D_MODEL
    return pl.pallas_call(
        _merge_kernel,
        grid=(n_tok // tm,),
        in_specs=[
            pl.BlockSpec((tm, D), lambda i: (i, 0)),
            pl.BlockSpec((tm, o_nsa2.shape[1]), lambda i: (i, 0)),
            pl.BlockSpec((tm, D), lambda i: (i, OFF_MG // D)),
            pl.BlockSpec((tm, D), lambda i: (i, OFF_MN // D)),
            pl.BlockSpec((tm, D), lambda i: (i, 0)),
            _resident(wg.shape, lambda i: (0, 0)),
            _resident(wn.shape, lambda i: (0, 0)),
            _resident(wo.shape, lambda i: (0, 0)),
            _resident((1, D), lambda i: (0, 0)),
        ],
        out_specs=pl.BlockSpec((tm, D), lambda i: (i, 0)),
        out_shape=jax.ShapeDtypeStruct((n_tok, D), F32),
        compiler_params=_params(("parallel",)),
    )(o_gla2, o_nsa2, proj2, proj2, x2, wg, wn, wo, g)


def _ffn_kernel(x_ref, gpre_ref, wg_ref, wu_ref, wd_ref, gpost_ref, o_ref, acc_scr, *, chunk):
    x = x_ref[...]
    h = (x * lax.rsqrt(jnp.mean(x * x, axis=-1, keepdims=True) + EPS) * gpre_ref[...]).astype(BF)
    d_ff = wg_ref.shape[1]
    for n, c0 in enumerate(range(0, d_ff, chunk)):
        c1 = min(c0 + chunk, d_ff)
        a = jnp.dot(h, wg_ref[:, c0:c1], preferred_element_type=F32)
        u = jnp.dot(h, wu_ref[:, c0:c1], preferred_element_type=F32)
        t = (a * jax.nn.sigmoid(a) * u).astype(BF)
        part = jnp.dot(t, wd_ref[c0:c1, :], preferred_element_type=F32)
        if n == 0:
            acc_scr[...] = part
        else:
            acc_scr[...] += part
    f = acc_scr[...]
    o_ref[...] = x + f * lax.rsqrt(jnp.mean(f * f, axis=-1, keepdims=True) + EPS) * gpost_ref[...]


def _ffn(x2, gpre, wg, wu, wd, gpost, tm=512, chunk=512):
    n_tok = x2.shape[0]
    D = D_MODEL
    kern = functools.partial(_ffn_kernel, chunk=chunk)
    return pl.pallas_call(
        kern,
        grid=(n_tok // tm,),
        in_specs=[
            pl.BlockSpec((tm, D), lambda i: (i, 0)),
            _resident((1, D), lambda i: (0, 0)),
            _resident(wg.shape, lambda i: (0, 0)),
            _resident(wu.shape, lambda i: (0, 0)),
            _resident(wd.shape, lambda i: (0, 0)),
            _resident((1, D), lambda i: (0, 0)),
        ],
        out_specs=pl.BlockSpec((tm, D), lambda i: (i, 0)),
        out_shape=jax.ShapeDtypeStruct((n_tok, D), F32),
        scratch_shapes=[pltpu.VMEM((tm, D), F32)],
        compiler_params=_params(("parallel",)),
    )(x2, gpre, wg, wu, wd, gpost)


def _prep_in_weights(w_in):
    splits = np.cumsum([512, 512, 1024, 1024, GLA_RANK, 512, 128, 128, 128, 128, 128, 128, 24, 1024])
    (g_q, g_k, g_v, g_r, g_a, n_q, n_kc, n_vc, n_ks, n_vs, n_kw, n_vw, n_gate, m_g, m_n) = jnp.split(
        w_in, [int(s) for s in splits], axis=1)
    w_main = jnp.concatenate([g_q, g_k, g_v, g_r, m_g, m_n, n_q * (NSA_DH ** -0.5),
                              n_kc, n_vc, n_ks, n_vs, n_kw, n_vw], axis=1).astype(BF)
    d = w_in.shape[0]
    per_g = NSA_REP * 3
    gates = jnp.pad(n_gate.reshape(d, NSA_GROUPS, per_g), ((0, 0), (0, 0), (0, LANES - per_g)))
    g_a3 = jnp.pad(jnp.concatenate([g_a, g_a, g_a], axis=1), ((0, 0), (0, LANES - 3 * GLA_RANK)))
    w_small = jnp.concatenate([g_a3, gates.reshape(d, NSA_GROUPS * LANES)], axis=1).astype(BF)
    return w_main, w_small


def _prep_compress(pe, w1, w2):
    eye = jnp.eye(NSA_GROUPS, dtype=F32)
    w1r = w1.reshape(CMP_LEN, NSA_DH, NSA_DH)
    w1e = jnp.einsum('lde,gh->lgdhe', w1r, eye).reshape(CMP_LEN * NSA_GROUPS * NSA_DH, NSA_GROUPS * NSA_DH)
    pe_e = jnp.broadcast_to(pe[:, None, :], (CMP_LEN, NSA_GROUPS, NSA_DH)).reshape(2, CMP_STRIDE * LANES)
    pe_e = jnp.pad(pe_e, ((0, 6), (0, 0)))
    w2e = jnp.stack([
        jnp.pad(jnp.pad(w2, ((g * NSA_DH, (NSA_GROUPS - 1 - g) * NSA_DH), (0, 0))), ((0, 0), (0, LANES - NSA_DH)))
        for g in range(NSA_GROUPS)])
    return pe_e.astype(F32), w1e.astype(BF), w2e.astype(BF)


def _overlap_table(seq):
    n_cmp = (seq - CMP_LEN) // CMP_STRIDE + 1
    n_slc = seq // SLC_LEN
    sc = CMP_STRIDE * np.arange(n_cmp)
    ss = SLC_LEN * np.arange(n_slc)
    ov = np.clip(np.minimum(sc[:, None] + CMP_LEN, ss[None, :] + SLC_LEN)
                 - np.maximum(sc[:, None], ss[None, :]), 0, None).astype(np.float32) / CMP_LEN
    ovt = np.zeros((NSA_DH, n_cmp + 1), np.float32)
    ovt[:n_slc, :n_cmp] = ov.T
    return jnp.asarray(ovt, dtype=BF)


def kernel(x, norm_mix_pre, norm_mix_post, norm_ffn_pre, norm_ffn_post, w_in, gla_w_alpha2, gla_b_alpha, gla_norm_g, nsa_cmp_pe_k, nsa_cmp_w1_k, nsa_cmp_w2_k, nsa_cmp_pe_v, nsa_cmp_w1_v, nsa_cmp_w2_v, w_proj_gla, w_proj_nsa, w_out, w_ffn_gate, w_ffn_up, w_ffn_down):
    B, S, D = x.shape
    depth = w_in.shape[0]
    n_tok = B * S
    h_idx = jnp.arange(NSA_HEADS, dtype=F32)
    slopes = jnp.exp2(-8.0 * (h_idx + 1.0) / NSA_HEADS).reshape(NSA_GROUPS, NSA_REP, 1)
    slope_tab = jnp.broadcast_to(jnp.pad(slopes, ((0, 0), (0, 8 - NSA_REP), (0, 0))), (NSA_GROUPS, 8, LANES))
    ovt = _overlap_table(S)
    x2 = x.reshape(n_tok, D)
    for l in range(depth):
        w_main, w_small = _prep_in_weights(w_in[l])
        proj2, small2 = _in_proj(x2, norm_mix_pre[l][None, :], w_main, w_small)
        proj3 = proj2.reshape(B, S, N_MAIN)
        small3 = small2.reshape(B, S, N_SMALL)

        w2_hi = gla_w_alpha2[l].astype(BF)
        w2_lo = (gla_w_alpha2[l] - w2_hi.astype(F32)).astype(BF)
        w2p = jnp.pad(jnp.concatenate([w2_hi, w2_hi, w2_lo], axis=0), ((0, LANES - 3 * GLA_RANK), (0, 0)))
        o_gla = _gla(proj3, small3, w2p, gla_b_alpha[l][None, :], gla_norm_g[l][None, :])

        xk = proj3[:, :, OFF_KC:OFF_KC + LANES].reshape(B, S // CMP_STRIDE, CMP_STRIDE * LANES)
        xv = proj3[:, :, OFF_VC:OFF_VC + LANES].reshape(B, S // CMP_STRIDE, CMP_STRIDE * LANES)
        pek, w1k, w2k = _prep_compress(nsa_cmp_pe_k[l], nsa_cmp_w1_k[l], nsa_cmp_w2_k[l])
        pev, w1v, w2v = _prep_compress(nsa_cmp_pe_v[l], nsa_cmp_w1_v[l], nsa_cmp_w2_v[l])
        kc, vc = _nsa_compress(xk, xv, pek, pev, w1k, w1v, w2k, w2v)
        o_nsa = _nsa_attend(proj3, small3, kc, vc, slope_tab, ovt)

        x2 = _merge_out(o_gla.reshape(n_tok, -1), o_nsa.reshape(n_tok, -1), proj2, x2,
                        w_proj_gla[l].astype(BF), w_proj_nsa[l].astype(BF),
                        w_out[l].astype(BF), norm_mix_post[l][None, :])
        x2 = _ffn(x2, norm_ffn_pre[l][None, :], w_ffn_gate[l].astype(BF), w_ffn_up[l].astype(BF),
                  w_ffn_down[l].astype(BF), norm_ffn_post[l][None, :])
    return x2.reshape(B, S, D)
```

```python
import functools

import numpy as np
import jax
import jax.numpy as jnp
from jax import lax
from jax.experimental import pallas as pl
from jax.experimental.pallas import tpu as pltpu

D_MODEL = 1024
GLA_HEADS = 4
GLA_DK = 128
GLA_DV = 256
GLA_RANK = 16
GLA_TAU = 16.0
GLA_CHUNK = 64
NSA_HEADS = 8
NSA_GROUPS = 2
NSA_REP = 4
NSA_DH = 64
CMP_LEN = 32
CMP_STRIDE = 16
SLC_LEN = 64
N_SEL = 16
WINDOW = 512
QB = 256
SLC_TILE = 256
VROWS = 80
D_FF = 2816
EPS = 1e-6
NEG = -1e30

LANES = 128
VMEM_LIMIT = 56 * 1024 * 1024
BF = jnp.bfloat16
F32 = jnp.float32

OFF_GQ = 0
OFF_GK = 512
OFF_GV = 1024
OFF_GR = 2048
OFF_MG = 3072
OFF_MN = 4096
OFF_NQ = 5120
OFF_KC = 5632
OFF_VC = 5760
OFF_KS = 5888
OFF_VS = 6016
OFF_KW = 6144
OFF_VW = 6272
N_MAIN = 6400
N_SMALL = 384

NT = (((1,), (1,)), ((), ()))
TN = (((0,), (0,)), ((), ()))


LOG2E = 1.4426950408889634
POS_TERMS = 3


def _pos_columns(lane, pos):
    first = lane - NSA_DH
    return jnp.where((first >= 0) & (first < POS_TERMS), pos >> 6,
                     jnp.where((first >= POS_TERMS) & (first < 2 * POS_TERMS), pos & 63, 0))


def _slope_rows(rowi, coef):
    terms = []
    rem = coef
    for _ in range(POS_TERMS):
        t = rem.astype(BF).astype(F32)
        terms.append(t)
        rem = rem - t
    out = jnp.zeros(rowi.shape, F32)
    for i, t in enumerate(terms):
        out = jnp.where(rowi == i, t * float(SLC_LEN), jnp.where(rowi == POS_TERMS + i, t, out))
    return out


def _resident(shape, index_map):
    return pl.BlockSpec(shape, index_map, pipeline_mode=pl.Buffered(1))


def _params(sem):
    return pltpu.CompilerParams(dimension_semantics=sem, vmem_limit_bytes=VMEM_LIMIT)


SRC_GA = 3072
SRC_NQ = 3088
SRC_KV = 3600
SRC_GATE = 4368
SRC_MERGE = 4392
IN_WIDTH_PAD = 6528
ALIGNED = OFF_MG
SHIFTED_RUNS = ((SRC_MERGE, 2048, OFF_MG, 1.0), (SRC_NQ, 512, OFF_NQ, NSA_DH ** -0.5), (SRC_KV, 768, OFF_KC, 1.0))


def _in_proj_kernel(x_ref, g_ref, wr_ref, om_ref, os_ref, wm_scr, ws_scr):
    @pl.when(pl.program_id(0) == 0)
    def _():
        r_i = lax.broadcasted_iota(jnp.int32, (2 * LANES, LANES), 0)
        c_i = lax.broadcasted_iota(jnp.int32, (2 * LANES, LANES), 1)
        for src, width, dst, scale in SHIFTED_RUNS:
            shift = jnp.where(r_i == c_i + src % LANES, 1.0, 0.0).astype(BF)
            for j in range(width // LANES):
                b0 = (src // LANES + j) * LANES
                moved = jnp.dot(wr_ref[:, b0:b0 + 2 * LANES], shift, preferred_element_type=F32)
                d0 = dst - ALIGNED + j * LANES
                wm_scr[:, d0:d0 + LANES] = (moved * scale).astype(BF)
        r1 = lax.broadcasted_iota(jnp.int32, (LANES, LANES), 0)
        c1 = lax.broadcasted_iota(jnp.int32, (LANES, LANES), 1)
        rep3 = jnp.where((r1 < GLA_RANK) & (c1 < 3 * GLA_RANK) & ((c1 % GLA_RANK) == r1), 1.0, 0.0).astype(BF)
        ws_scr[:, 0:LANES] = jnp.dot(wr_ref[:, SRC_GA:SRC_GA + LANES], rep3,
                                     preferred_element_type=F32).astype(BF)
        g_blk = SRC_GATE // LANES * LANES
        per_g = NSA_REP * 3
        for g in range(NSA_GROUPS):
            pick = jnp.where((c1 < per_g) & (r1 == c1 + SRC_GATE - g_blk + per_g * g), 1.0, 0.0).astype(BF)
            ws_scr[:, (1 + g) * LANES:(2 + g) * LANES] = jnp.dot(
                wr_ref[:, g_blk:g_blk + LANES], pick, preferred_element_type=F32).astype(BF)

    x = x_ref[...]
    h = (x * lax.rsqrt(jnp.mean(x * x, axis=-1, keepdims=True) + EPS) * g_ref[...]).astype(BF)
    os_ref[...] = jnp.dot(h, ws_scr[...], preferred_element_type=F32)
    step = 512
    for c0 in range(0, ALIGNED, step):
        om_ref[:, c0:c0 + step] = jnp.dot(h, wr_ref[:, c0:c0 + step], preferred_element_type=F32).astype(BF)
    for c0 in range(0, N_MAIN - ALIGNED, step):
        c1_ = min(c0 + step, N_MAIN - ALIGNED)
        om_ref[:, ALIGNED + c0:ALIGNED + c1_] = jnp.dot(h, wm_scr[:, c0:c1_],
                                                        preferred_element_type=F32).astype(BF)


def _in_proj(x2, g, w_raw, tm=512):
    n_tok = x2.shape[0]
    return pl.pallas_call(
        _in_proj_kernel,
        grid=(n_tok // tm,),
        in_specs=[
            pl.BlockSpec((tm, D_MODEL), lambda i: (i, 0)),
            _resident((1, D_MODEL), lambda i: (0, 0)),
            _resident((D_MODEL, IN_WIDTH_PAD), lambda i: (0, 0)),
        ],
        out_specs=[
            pl.BlockSpec((tm, N_MAIN), lambda i: (i, 0)),
            pl.BlockSpec((tm, N_SMALL), lambda i: (i, 0)),
        ],
        out_shape=[
            jax.ShapeDtypeStruct((n_tok, N_MAIN), BF),
            jax.ShapeDtypeStruct((n_tok, N_SMALL), F32),
        ],
        scratch_shapes=[pltpu.VMEM((D_MODEL, N_MAIN - ALIGNED), BF), pltpu.VMEM((D_MODEL, N_SMALL), BF)],
        compiler_params=_params(("arbitrary",)),
    )(x2, g, w_raw)


def _gla_kernel(q_ref, k_ref, v_ref, r_ref, a_ref, w2_ref, b2_ref, ng_ref, o_ref, st_scr, *, n_chunks, n_heads):
    blk = pl.program_id(2)

    @pl.when(blk == 0)
    def _():
        st_scr[...] = jnp.zeros_like(st_scr)

    C = GLA_CHUNK
    T = n_chunks * C
    W = n_heads * GLA_DK

    def split3(x):
        hi = x.astype(BF)
        rem = x - hi.astype(F32)
        mid = rem.astype(BF)
        return hi, mid, (rem - mid.astype(F32)).astype(BF)

    a = a_ref[0]
    a_hi = a.astype(BF)
    a_lo = (a - a_hi.astype(F32)).astype(BF)
    lane = lax.broadcasted_iota(jnp.int32, (T, LANES), 1)
    in_lo = (lane >= GLA_RANK) & (lane < 2 * GLA_RANK)
    z = jnp.dot(jnp.where(in_lo, a_lo, a_hi), w2_ref[...], preferred_element_type=F32) + b2_ref[...]
    log_a = (jnp.minimum(z, 0.0) - jnp.log1p(jnp.exp(-jnp.abs(z)))) * (1.0 / GLA_TAU)

    x_wide = jnp.concatenate([log_a[c * C:(c + 1) * C] for c in range(n_chunks)], axis=1)
    x3 = jnp.concatenate(split3(x_wide), axis=0)
    r3 = lax.broadcasted_iota(jnp.int32, (C, 3 * C), 0)
    c3 = lax.broadcasted_iota(jnp.int32, (C, 3 * C), 1) & (C - 1)
    tri3 = jnp.where(c3 <= r3, 1.0, 0.0).astype(BF)
    b_wide = jnp.dot(tri3, x3, preferred_element_type=F32)
    bcum = jnp.concatenate([b_wide[:, c * W:(c + 1) * W] for c in range(n_chunks)], axis=0)
    last_rows = [b_wide[C - 1:C, c * W:(c + 1) * W] for c in range(n_chunks)]
    b_last = jnp.concatenate([jnp.broadcast_to(lr, (C, W)) for lr in last_rows], axis=0)
    decay = [jnp.exp(lr) for lr in last_rows]

    q = q_ref[0].astype(F32)
    k = k_ref[0].astype(F32)
    v = v_ref[0]
    qe = (q * ((GLA_DK ** -0.5) * jnp.exp(bcum))).astype(BF)
    ke = (k * jnp.exp(-bcum)).astype(BF)
    kd = (k * jnp.exp(b_last - bcum)).astype(BF)

    H = min(T, 4 * C)
    row = lax.broadcasted_iota(jnp.int32, (H, H), 0)
    col = lax.broadcasted_iota(jnp.int32, (H, H), 1)
    keep = (col <= row) & ((col >> 6) == (row >> 6))
    ng = ng_ref[...]
    r_all = r_ref[0].astype(F32)
    for hh in range(n_heads):
        ks_ = slice(hh * GLA_DK, (hh + 1) * GLA_DK)
        vs_ = slice(hh * GLA_DV, (hh + 1) * GLA_DV)
        intra = []
        for h0 in range(0, T, H):
            hs = slice(h0, h0 + H)
            attn = lax.dot_general(qe[hs, ks_], ke[hs, ks_], NT, preferred_element_type=F32)
            intra.append(jnp.dot(jnp.where(keep, attn, 0.0).astype(BF), v[hs, vs_], preferred_element_type=F32))
        o = jnp.concatenate(intra, axis=0)

        st = st_scr[hh]
        inter = []
        for c in range(n_chunks):
            sl = slice(c * C, (c + 1) * C)
            inter.append(lax.dot_general(qe[sl, ks_], st.astype(BF), NT, preferred_element_type=F32))
            upd = lax.dot_general(v[sl, vs_], kd[sl, ks_], TN, preferred_element_type=F32)
            st = st * decay[c][:, ks_] + upd
        st_scr[hh] = st
        o = o + jnp.concatenate(inter, axis=0)

        o = o * lax.rsqrt(jnp.mean(o * o, axis=-1, keepdims=True) + EPS) * ng
        r = r_all[:, vs_]
        o_ref[0, :, vs_] = (o * (r * jax.nn.sigmoid(r))).astype(BF)


def _gla(proj3, small3, w2p, b2, ng, blk_tokens=512, heads_per_step=4):
    B, S, _ = proj3.shape
    nblk = S // blk_tokens
    hb = heads_per_step
    kern = functools.partial(_gla_kernel, n_chunks=blk_tokens // GLA_CHUNK, n_heads=hb)
    wk, wv = hb * GLA_DK, hb * GLA_DV
    return pl.pallas_call(
        kern,
        grid=(B, GLA_HEADS // hb, nblk),
        in_specs=[
            pl.BlockSpec((1, blk_tokens, wk), lambda b, h, i: (b, i, OFF_GQ // wk + h)),
            pl.BlockSpec((1, blk_tokens, wk), lambda b, h, i: (b, i, OFF_GK // wk + h)),
            pl.BlockSpec((1, blk_tokens, wv), lambda b, h, i: (b, i, OFF_GV // wv + h)),
            pl.BlockSpec((1, blk_tokens, wv), lambda b, h, i: (b, i, OFF_GR // wv + h)),
            pl.BlockSpec((1, blk_tokens, LANES), lambda b, h, i: (b, i, 0)),
            pl.BlockSpec((LANES, wk), lambda b, h, i: (0, h)),
            pl.BlockSpec((1, wk), lambda b, h, i: (0, h)),
            pl.BlockSpec((1, GLA_DV), lambda b, h, i: (0, 0)),
        ],
        out_specs=pl.BlockSpec((1, blk_tokens, wv), lambda b, h, i: (b, i, h)),
        out_shape=jax.ShapeDtypeStruct((B, S, GLA_HEADS * GLA_DV), BF),
        scratch_shapes=[pltpu.VMEM((hb, GLA_DV, GLA_DK), F32)],
        compiler_params=_params(("parallel", "parallel", "arbitrary")),
    )(proj3, proj3, proj3, proj3, small3, w2p, b2, ng)


def _compress_kernel(xk_ref, xv_ref, pek_ref, pev_ref, w1k_ref, w1v_ref, w2k_ref, w2v_ref, kc_ref, vc_ref):
    n_rows = xk_ref.shape[1]
    half = CMP_STRIDE * LANES
    lane = lax.broadcasted_iota(jnp.int32, (n_rows, LANES), 1)
    row = lax.broadcasted_iota(jnp.int32, (n_rows, LANES), 0)
    end_c = CMP_STRIDE * row + (CMP_LEN - 1)
    c_k = _pos_columns(lane, end_c).astype(F32)
    c_v = jnp.where(lane == NSA_DH, 1.0, 0.0).astype(F32)

    def branch(x_ref, pe_ref, w1_ref, w2_ref, const, o_ref):
        x = x_ref[0].astype(F32)
        xa = (x + pe_ref[0:1, :]).astype(BF)
        xb = (x + pe_ref[1:2, :]).astype(BF)
        a = jnp.dot(xa, w1_ref[0:half, :], preferred_element_type=F32)
        b = jnp.dot(xb, w1_ref[half:2 * half, :], preferred_element_type=F32)
        pre = a + pltpu.roll(b, n_rows - 1, 0)
        hid = (pre * jax.nn.sigmoid(pre)).astype(BF)
        for g in range(NSA_GROUPS):
            o_ref[0, g] = (jnp.dot(hid, w2_ref[g], preferred_element_type=F32) + const).astype(BF)

    branch(xk_ref, pek_ref, w1k_ref, w2k_ref, c_k, kc_ref)
    branch(xv_ref, pev_ref, w1v_ref, w2v_ref, c_v, vc_ref)


def _nsa_compress(xk, xv, pek, pev, w1k, w1v, w2k, w2v):
    B, n_rows, width = xk.shape
    full = lambda shape: _resident(shape, lambda b: (0,) * len(shape))
    out = jax.ShapeDtypeStruct((B, NSA_GROUPS, n_rows, LANES), BF)
    return pl.pallas_call(
        _compress_kernel,
        grid=(B,),
        in_specs=[
            pl.BlockSpec((1, n_rows, width), lambda b: (b, 0, 0)),
            pl.BlockSpec((1, n_rows, width), lambda b: (b, 0, 0)),
            full(pek.shape), full(pev.shape), full(w1k.shape), full(w1v.shape),
            full(w2k.shape), full(w2v.shape),
        ],
        out_specs=[pl.BlockSpec((1, NSA_GROUPS, n_rows, LANES), lambda b: (b, 0, 0, 0))] * 2,
        out_shape=[out, out],
        compiler_params=_params(("parallel",)),
    )(xk, xv, pek, pev, w1k, w1v, w2k, w2v)


def _nsa_kernel(q_ref, ks_ref, vs_ref, kw_ref, vw_ref, kc_ref, vc_ref, gt_ref, sl_ref, ov_ref,
                o_ref, ksa, kwa, vst, vwt, vct, acc_scr, act_ref, *, seq):
    qi = pl.program_id(2)
    n_slc = seq // SLC_LEN
    R = NSA_REP
    KT = SLC_TILE
    WK = WINDOW + QB
    n_cmp = kc_ref.shape[2]

    def t_bf(x):
        return x.astype(F32).T[0:VROWS].astype(BF)

    @pl.when(qi == 0)
    def _():
        grp = pl.program_id(1)
        lane = lax.broadcasted_iota(jnp.int32, (seq, LANES), 1)
        pos = lax.broadcasted_iota(jnp.int32, (seq, LANES), 0)
        blk = pos >> 6
        off = pos & 63
        r_i = lax.broadcasted_iota(jnp.int32, (LANES, LANES), 0)
        c_i = lax.broadcasted_iota(jnp.int32, (LANES, LANES), 1)
        pick = jnp.where((c_i < NSA_DH) & (r_i == c_i + grp * NSA_DH), 1.0, 0.0).astype(BF)
        c_s = jnp.where(lane == LANES - 1, off,
                        jnp.where((lane >= NSA_DH) & (lane - (NSA_DH - 1) == blk), 1, 0))
        ksa[0:seq, :] = (jnp.dot(ks_ref[0], pick, preferred_element_type=F32) + c_s.astype(F32)).astype(BF)
        lane_d = lax.broadcasted_iota(jnp.int32, (KT, LANES), 1)
        ksa[seq:seq + KT, :] = jnp.where((lane_d >= NSA_DH) & (lane_d < LANES - 1), 1, 0).astype(BF)
        for c in range(KT // LANES):
            vst[seq // LANES + c] = jnp.zeros((VROWS, LANES), BF)
        c_w = _pos_columns(lane, pos)
        kwa[0:WINDOW, :] = jnp.zeros((WINDOW, LANES), BF)
        kwa[WINDOW:WINDOW + seq, :] = (jnp.dot(kw_ref[0], pick, preferred_element_type=F32)
                                       + c_w.astype(F32)).astype(BF)
        ones_rows = jnp.where(lax.broadcasted_iota(jnp.int32, (VROWS - NSA_DH, LANES), 0) == 0, 1.0, 0.0)
        n_pad = WINDOW // LANES
        for c in range(n_pad):
            vwt[c] = jnp.zeros((VROWS, LANES), BF)

        def v_tile(x):
            xt = x.astype(F32).T
            dims = jnp.where(grp == 0, xt[0:NSA_DH], xt[NSA_DH:2 * NSA_DH])
            return jnp.concatenate([dims, ones_rows], axis=0).astype(BF)

        def fill(c, carry):
            rows = pl.ds(pl.multiple_of(c * LANES, LANES), LANES)
            vst[c] = v_tile(vs_ref[0, rows, :])
            vwt[c + n_pad] = v_tile(vw_ref[0, rows, :])
            return carry

        lax.fori_loop(0, seq // LANES, fill, 0)
        for c in range(n_cmp // LANES):
            vct[:, c * LANES:(c + 1) * LANES] = t_bf(vc_ref[0, 0, c * LANES:(c + 1) * LANES, :])

    q0 = qi * QB
    rowi = lax.broadcasted_iota(jnp.int32, (NSA_DH, QB), 0)
    slopes = [jnp.concatenate([sl_ref[0, r:r + 1, :]] * (QB // LANES), axis=1) for r in range(R)]
    q_all = q_ref[0].astype(F32).T
    q_t = [q_all[r * NSA_DH:(r + 1) * NSA_DH] for r in range(R)]

    qw = jnp.concatenate(
        [jnp.concatenate([q_t[r] * LOG2E, _slope_rows(rowi, slopes[r] * LOG2E)], axis=0).astype(BF)
         for r in range(R)], axis=1)

    def tile4(x):
        return jnp.concatenate([x] * R, axis=1)

    ovt = ov_ref[...]

    def cmp_branch(rows):
        def fn():
            s_c = jnp.dot(kc_ref[0, 0, 0:rows, :], qw, preferred_element_type=F32)
            e_c = CMP_STRIDE * lax.broadcasted_iota(jnp.int32, (rows, QB), 0) + (CMP_LEN - 1)
            t_c = q0 + lax.broadcasted_iota(jnp.int32, (rows, QB), 1)
            s_c = s_c + tile4(jnp.where(e_c <= t_c, 0.0, NEG))
            m_c = jnp.maximum(jnp.max(s_c, axis=0, keepdims=True), 0.1 * NEG)
            p_c = jnp.exp2(s_c - m_c)
            l_c = jnp.sum(p_c, axis=0, keepdims=True)
            p_c = p_c * jnp.where(l_c > 0.0, 1.0 / l_c, 0.0)
            o_cmp = jnp.dot(vct[:, 0:rows], p_c.astype(BF), preferred_element_type=F32)
            psum = p_c[:, 0:QB]
            for r in range(1, R):
                psum = psum + p_c[:, r * QB:(r + 1) * QB]
            p_hi = psum.astype(BF)
            rem = psum - p_hi.astype(F32)
            p_mid = rem.astype(BF)
            p_lo = (rem - p_mid.astype(F32)).astype(BF)
            ov = ovt[:, 0:rows]
            return o_cmp, (jnp.dot(ov, p_hi, preferred_element_type=F32)
                           + jnp.dot(ov, p_mid, preferred_element_type=F32)
                           + jnp.dot(ov, p_lo, preferred_element_type=F32))
        return fn

    half = n_cmp // 2
    o_c, imp = lax.cond(q0 + QB <= CMP_STRIDE * half + CMP_LEN - 1, cmp_branch(half), cmp_branch(n_cmp))

    NR = ov_ref.shape[0]
    SUB = 8
    jblk = lax.broadcasted_iota(jnp.int32, (NR, QB), 0)
    t_q = q0 + lax.broadcasted_iota(jnp.int32, (NR, QB), 1)
    cur = t_q >> 6
    forced = (jblk == 0) | (jblk == cur) | (jblk == cur - 1)
    score = jnp.where(jblk > cur, NEG, jnp.where(forced, -NEG, imp))
    n_slab = -(-n_slc // SUB)
    isub = lax.broadcasted_iota(jnp.int32, (SUB, QB), 0)

    def rank_counts(ns):
        def fn():
            slabs = [score[a * SUB:(a + 1) * SUB, :] for a in range(ns)]
            cnts = [jnp.zeros((SUB, QB), F32) for _ in range(ns)]
            for jp in range(min(ns * SUB, n_slc)):
                rowv = jnp.broadcast_to(score[jp:jp + 1, :], (SUB, QB))
                for a in range(ns):
                    if a < jp // SUB:
                        beats = jnp.where(rowv > slabs[a], 1.0, 0.0)
                    elif a > jp // SUB:
                        beats = jnp.where(rowv >= slabs[a], 1.0, 0.0)
                    else:
                        beats = jnp.where(isub > jp % SUB, jnp.where(rowv >= slabs[a], 1.0, 0.0),
                                          jnp.where(rowv > slabs[a], 1.0, 0.0))
                    cnts[a] = cnts[a] + beats
            rest = NR - ns * SUB
            return jnp.concatenate(cnts + [jnp.full((rest, QB), float(NR), F32)] * (rest > 0), axis=0)
        return fn

    last_blk = (q0 + QB - 1) >> 6
    cnt = lax.switch(last_blk // SUB, [rank_counts(ns) for ns in range(1, n_slab + 1)])
    sel = (cnt < float(N_SEL)) & (jblk <= cur) & (jblk < n_slc)
    a_nat = jnp.where(jblk == 0, 1.0, jnp.where(sel, (SLC_LEN * jblk).astype(F32), NEG))
    a_nat = jnp.where(jblk < n_slc, a_nat, 0.0)
    a_t = pltpu.roll(a_nat, NR - 1, 0)

    any_q = jnp.max(jnp.where(sel, 1.0, 0.0), axis=1, keepdims=True)
    jcol = lax.broadcasted_iota(jnp.int32, (NR, 1), 0)
    bits = jnp.where(any_q > 0.0, lax.shift_left(jnp.int32(1), jcol & 31), 0)
    word0 = jnp.sum(jnp.where(jcol < 32, bits, 0))
    word1 = jnp.sum(jnp.where(jcol >= 32, bits, 0))

    qs = jnp.concatenate([jnp.concatenate([q_t[r], a_t * slopes[r]], axis=0).astype(BF) for r in range(R)],
                         axis=1)

    tiles_per = KT // LANES
    blocks_per = KT // SLC_LEN
    n_full = q0 // KT

    def scan(ti, n):
        word = jnp.where(ti < 32 // blocks_per, word0, word1)
        hit = (lax.shift_right_logical(word, (ti * blocks_per) & 31) & ((1 << blocks_per) - 1)) != 0
        act_ref[n] = ti
        return n + jnp.where(hit, 1, 0)

    n_act = lax.fori_loop(0, n_full, scan, 0)
    act_ref[n_act] = seq // KT

    kw = kwa[pl.ds(pl.multiple_of(q0, QB), WK), :]
    s_w = jnp.dot(kw, qw, preferred_element_type=F32)
    c_w = lax.broadcasted_iota(jnp.int32, (WK, QB), 0)
    d_w = c_w - lax.broadcasted_iota(jnp.int32, (WK, QB), 1)
    mask_w = (d_w > 0) & (d_w <= WINDOW) & (c_w >= WINDOW - q0)
    s_w = s_w + tile4(jnp.where(mask_w, 0.0, NEG))
    m_w = jnp.max(s_w, axis=0, keepdims=True)
    p_w = jnp.exp2(s_w - m_w).astype(BF)

    def scores(ti):
        k0 = pl.multiple_of(ti * KT, KT)
        return jnp.dot(ksa[pl.ds(k0, KT), :], qs, preferred_element_type=F32)

    def values_t(ti):
        return [vst[ti * tiles_per + c] for c in range(tiles_per)]

    def absorb(s, v_tiles, m_prev):
        m_new = jnp.maximum(m_prev, jnp.max(s, axis=0, keepdims=True))
        alpha = jnp.exp(m_prev - m_new)
        p = jnp.exp(s - m_new).astype(BF)
        acc_scr[...] = acc_scr[...] * alpha + jnp.dot(jnp.concatenate(v_tiles, axis=1), p,
                                                     preferred_element_type=F32)
        return m_new

    acc_scr[...] = jnp.zeros(acc_scr.shape, F32)

    def chains(first, count, m_run):
        tiles = [act_ref[first + i] for i in range(count)]
        s_all = [scores(t) for t in tiles]
        for t, s in zip(tiles, s_all):
            m_run = absorb(s, values_t(t), m_run)
        return m_run

    n_quad = n_act >> 2
    m_d = lax.fori_loop(0, n_quad, lambda j, m: chains(4 * j, 4, m), jnp.full((1, R * QB), NEG, F32))
    n_rest = n_act - 4 * n_quad
    m_d = lax.fori_loop(0, (n_rest + 1) >> 1, lambda j, m: chains(4 * n_quad + 2 * j, 2, m), m_d)
    for d in range(QB // KT):
        ti = n_full + d
        p_d = ti * KT + lax.broadcasted_iota(jnp.int32, (KT, QB), 0)
        t_d = q0 + lax.broadcasted_iota(jnp.int32, (KT, QB), 1)
        m_d = absorb(scores(ti) + tile4(jnp.where(p_d <= t_d, 0.0, NEG)), values_t(ti), m_d)
    acc_s = acc_scr[...]

    vw_t = jnp.concatenate([vwt[qi * (QB // LANES) + c] for c in range(WK // LANES)], axis=1)
    acc_w = jnp.dot(vw_t, p_w, preferred_element_type=F32)


    sg_t = jax.nn.sigmoid(gt_ref[0]).T
    outs = []
    for r in range(R):
        cs = slice(r * QB, (r + 1) * QB)
        a_s = acc_s[:, cs]
        a_w = acc_w[:, cs]
        g_c = sg_t[3 * r:3 * r + 1, :]
        g_s = sg_t[3 * r + 1:3 * r + 2, :] / a_s[NSA_DH:NSA_DH + 1, :]
        g_w = sg_t[3 * r + 2:3 * r + 3, :] / a_w[NSA_DH:NSA_DH + 1, :]
        outs.append((g_c * o_c[:, cs] + g_s * a_s + g_w * a_w)[0:NSA_DH])
    o_ref[0] = jnp.concatenate(outs, axis=0).T.astype(BF)


def _nsa_attend(proj3, small3, kc, vc, slope_tab, ovt):
    B, S, _ = proj3.shape
    G, R = NSA_GROUPS, NSA_REP
    n_cmp = kc.shape[2]
    kern = functools.partial(_nsa_kernel, seq=S)
    kv_spec = lambda off: pl.BlockSpec((1, S, LANES), lambda b, g, i: (b, 0, off // LANES))
    return pl.pallas_call(
        kern,
        grid=(B, G, S // QB),
        in_specs=[
            pl.BlockSpec((1, QB, R * NSA_DH), lambda b, g, i: (b, i, OFF_NQ // (R * NSA_DH) + g)),
            kv_spec(OFF_KS), kv_spec(OFF_VS), kv_spec(OFF_KW), kv_spec(OFF_VW),
            pl.BlockSpec((1, 1, n_cmp, LANES), lambda b, g, i: (b, g, 0, 0)),
            pl.BlockSpec((1, 1, n_cmp, LANES), lambda b, g, i: (b, g, 0, 0)),
            pl.BlockSpec((1, QB, LANES), lambda b, g, i: (b, i, 1 + g)),
            pl.BlockSpec((1, 8, LANES), lambda b, g, i: (g, 0, 0)),
            _resident(ovt.shape, lambda b, g, i: (0, 0)),
        ],
        out_specs=pl.BlockSpec((1, QB, R * NSA_DH), lambda b, g, i: (b, i, g)),
        out_shape=jax.ShapeDtypeStruct((B, S, G * R * NSA_DH), BF),
        scratch_shapes=[
            pltpu.VMEM((S + SLC_TILE, LANES), BF),
            pltpu.VMEM((S + WINDOW, LANES), BF),
            pltpu.VMEM(((S + SLC_TILE) // LANES, VROWS, LANES), BF),
            pltpu.VMEM(((S + WINDOW) // LANES, VROWS, LANES), BF),
            pltpu.VMEM((VROWS, n_cmp), BF),
            pltpu.VMEM((VROWS, R * QB), F32),
            pltpu.SMEM((S // SLC_TILE + 1,), jnp.int32),
        ],
        compiler_params=_params(("parallel", "parallel", "arbitrary")),
    )(proj3, proj3, proj3, proj3, proj3, kc, vc, small3, slope_tab, ovt)


def _merge_kernel(og_ref, on_ref, mg_ref, mn_ref, x_ref, wg_ref, wn_ref, wo_ref, g_ref, o_ref):
    a = jnp.dot(og_ref[...], wg_ref[...], preferred_element_type=F32)
    b = jnp.dot(on_ref[...], wn_ref[...], preferred_element_type=F32)
    mixed = jax.nn.sigmoid(mg_ref[...].astype(F32)) * a + jax.nn.sigmoid(mn_ref[...].astype(F32)) * b
    y = jnp.dot(mixed.astype(BF), wo_ref[...], preferred_element_type=F32)
    y = y * lax.rsqrt(jnp.mean(y * y, axis=-1, keepdims=True) + EPS) * g_ref[...]
    o_ref[...] = x_ref[...] + y


def _merge_out(o_gla2, o_nsa2, proj2, x2, wg, wn, wo, g, tm=512):
    n_tok = x2.shape[0]
    D = D_MODEL
    return pl.pallas_call(
        _merge_kernel,
        grid=(n_tok // tm,),
        in_specs=[
            pl.BlockSpec((tm, D), lambda i: (i, 0)),
            pl.BlockSpec((tm, o_nsa2.shape[1]), lambda i: (i, 0)),
            pl.BlockSpec((tm, D), lambda i: (i, OFF_MG // D)),
            pl.BlockSpec((tm, D), lambda i: (i, OFF_MN // D)),
            pl.BlockSpec((tm, D), lambda i: (i, 0)),
            _resident(wg.shape, lambda i: (0, 0)),
            _resident(wn.shape, lambda i: (0, 0)),
            _resident(wo.shape, lambda i: (0, 0)),
            _resident((1, D), lambda i: (0, 0)),
        ],
        out_specs=pl.BlockSpec((tm, D), lambda i: (i, 0)),
        out_shape=jax.ShapeDtypeStruct((n_tok, D), F32),
        compiler_params=_params(("parallel",)),
    )(o_gla2, o_nsa2, proj2, proj2, x2, wg, wn, wo, g)


def _ffn_kernel(x_ref, gpre_ref, wg_ref, wu_ref, wd_ref, gpost_ref, o_ref, acc_scr, *, chunk):
    x = x_ref[...]
    h = (x * lax.rsqrt(jnp.mean(x * x, axis=-1, keepdims=True) + EPS) * gpre_ref[...]).astype(BF)
    d_ff = wg_ref.shape[1]
    for n, c0 in enumerate(range(0, d_ff, chunk)):
        c1 = min(c0 + chunk, d_ff)
        a = jnp.dot(h, wg_ref[:, c0:c1], preferred_element_type=F32)
        u = jnp.dot(h, wu_ref[:, c0:c1], preferred_element_type=F32)
        t = (a * jax.nn.sigmoid(a) * u).astype(BF)
        part = jnp.dot(t, wd_ref[c0:c1, :], preferred_element_type=F32)
        if n == 0:
            acc_scr[...] = part
        else:
            acc_scr[...] += part
    f = acc_scr[...]
    o_ref[...] = x + f * lax.rsqrt(jnp.mean(f * f, axis=-1, keepdims=True) + EPS) * gpost_ref[...]


def _ffn(x2, gpre, wg, wu, wd, gpost, tm=512, chunk=512):
    n_tok = x2.shape[0]
    D = D_MODEL
    kern = functools.partial(_ffn_kernel, chunk=chunk)
    return pl.pallas_call(
        kern,
        grid=(n_tok // tm,),
        in_specs=[
            pl.BlockSpec((tm, D), lambda i: (i, 0)),
            _resident((1, D), lambda i: (0, 0)),
            _resident(wg.shape, lambda i: (0, 0)),
            _resident(wu.shape, lambda i: (0, 0)),
            _resident(wd.shape, lambda i: (0, 0)),
            _resident((1, D), lambda i: (0, 0)),
        ],
        out_specs=pl.BlockSpec((tm, D), lambda i: (i, 0)),
        out_shape=jax.ShapeDtypeStruct((n_tok, D), F32),
        scratch_shapes=[pltpu.VMEM((tm, D), F32)],
        compiler_params=_params(("parallel",)),
    )(x2, gpre, wg, wu, wd, gpost)


def _prep_compress(pe, w1, w2):
    eye = jnp.eye(NSA_GROUPS, dtype=F32)
    w1r = w1.reshape(CMP_LEN, NSA_DH, NSA_DH)
    w1e = jnp.einsum('lde,gh->lgdhe', w1r, eye).reshape(CMP_LEN * NSA_GROUPS * NSA_DH, NSA_GROUPS * NSA_DH)
    pe_e = jnp.broadcast_to(pe[:, None, :], (CMP_LEN, NSA_GROUPS, NSA_DH)).reshape(2, CMP_STRIDE * LANES)
    pe_e = jnp.pad(pe_e, ((0, 6), (0, 0)))
    w2e = jnp.stack([
        jnp.pad(jnp.pad(w2, ((g * NSA_DH, (NSA_GROUPS - 1 - g) * NSA_DH), (0, 0))), ((0, 0), (0, LANES - NSA_DH)))
        for g in range(NSA_GROUPS)])
    return pe_e.astype(F32), w1e.astype(BF), w2e.astype(BF)


def _overlap_table(seq):
    n_cmp = (seq - CMP_LEN) // CMP_STRIDE + 1
    n_slc = seq // SLC_LEN
    sc = CMP_STRIDE * np.arange(n_cmp)
    ss = SLC_LEN * np.arange(n_slc)
    ov = np.clip(np.minimum(sc[:, None] + CMP_LEN, ss[None, :] + SLC_LEN)
                 - np.maximum(sc[:, None], ss[None, :]), 0, None).astype(np.float32) / CMP_LEN
    ovt = np.zeros((NSA_DH, n_cmp + 1), np.float32)
    ovt[:n_slc, :n_cmp] = ov.T
    return jnp.asarray(ovt, dtype=BF)


def kernel(x, norm_mix_pre, norm_mix_post, norm_ffn_pre, norm_ffn_post, w_in, gla_w_alpha2, gla_b_alpha, gla_norm_g, nsa_cmp_pe_k, nsa_cmp_w1_k, nsa_cmp_w2_k, nsa_cmp_pe_v, nsa_cmp_w1_v, nsa_cmp_w2_v, w_proj_gla, w_proj_nsa, w_out, w_ffn_gate, w_ffn_up, w_ffn_down):
    B, S, D = x.shape
    depth = w_in.shape[0]
    n_tok = B * S
    h_idx = jnp.arange(NSA_HEADS, dtype=F32)
    slopes = jnp.exp2(-8.0 * (h_idx + 1.0) / NSA_HEADS).reshape(NSA_GROUPS, NSA_REP, 1)
    slope_tab = jnp.broadcast_to(jnp.pad(slopes, ((0, 0), (0, 8 - NSA_REP), (0, 0))), (NSA_GROUPS, 8, LANES))
    ovt = _overlap_table(S)
    x2 = x.reshape(n_tok, D)
    for l in range(depth):
        w_raw = jnp.pad(w_in[l].astype(BF), ((0, 0), (0, IN_WIDTH_PAD - w_in.shape[2])))
        proj2, small2 = _in_proj(x2, norm_mix_pre[l][None, :], w_raw)
        proj3 = proj2.reshape(B, S, N_MAIN)
        small3 = small2.reshape(B, S, N_SMALL)

        w2_hi = gla_w_alpha2[l].astype(BF)
        w2_lo = (gla_w_alpha2[l] - w2_hi.astype(F32)).astype(BF)
        w2p = jnp.pad(jnp.concatenate([w2_hi, w2_hi, w2_lo], axis=0), ((0, LANES - 3 * GLA_RANK), (0, 0)))
        o_gla = _gla(proj3, small3, w2p, gla_b_alpha[l][None, :], gla_norm_g[l][None, :])

        xk = proj3[:, :, OFF_KC:OFF_KC + LANES].reshape(B, S // CMP_STRIDE, CMP_STRIDE * LANES)
        xv = proj3[:, :, OFF_VC:OFF_VC + LANES].reshape(B, S // CMP_STRIDE, CMP_STRIDE * LANES)
        pek, w1k, w2k = _prep_compress(nsa_cmp_pe_k[l], nsa_cmp_w1_k[l], nsa_cmp_w2_k[l])
        pev, w1v, w2v = _prep_compress(nsa_cmp_pe_v[l], nsa_cmp_w1_v[l], nsa_cmp_w2_v[l])
        kc, vc = _nsa_compress(xk, xv, pek, pev, w1k, w1v, w2k, w2v)
        o_nsa = _nsa_attend(proj3, small3, kc, vc, slope_tab, ovt)

        x2 = _merge_out(o_gla.reshape(n_tok, -1), o_nsa.reshape(n_tok, -1), proj2, x2,
                        w_proj_gla[l].astype(BF), w_proj_nsa[l].astype(BF),
                        w_out[l].astype(BF), norm_mix_post[l][None, :])
        x2 = _ffn(x2, norm_ffn_pre[l][None, :], w_ffn_gate[l].astype(BF), w_ffn_up[l].astype(BF),
                  w_ffn_down[l].astype(BF), norm_ffn_post[l][None, :])
    return x2.reshape(B, S, D)
```

```python
import functools

import numpy as np
import jax
import jax.numpy as jnp
from jax import lax
from jax.experimental import pallas as pl
from jax.experimental.pallas import tpu as pltpu

D_MODEL = 1024
GLA_HEADS = 4
GLA_DK = 128
GLA_DV = 256
GLA_RANK = 16
GLA_TAU = 16.0
GLA_CHUNK = 64
NSA_HEADS = 8
NSA_GROUPS = 2
NSA_REP = 4
NSA_DH = 64
CMP_LEN = 32
CMP_STRIDE = 16
SLC_LEN = 64
N_SEL = 16
WINDOW = 512
QB = 256
SLC_TILE = 256
VROWS = 80
D_FF = 2816
EPS = 1e-6
NEG = -1e30

LANES = 128
VMEM_LIMIT = 56 * 1024 * 1024
BF = jnp.bfloat16
F32 = jnp.float32

OFF_GQ = 0
OFF_GK = 512
OFF_GV = 1024
OFF_GR = 2048
OFF_MG = 3072
OFF_MN = 4096
OFF_NQ = 5120
OFF_KC = 5632
OFF_VC = 5760
OFF_KS = 5888
OFF_VS = 6016
OFF_KW = 6144
OFF_VW = 6272
N_MAIN = 6400
N_SMALL = 384

NT = (((1,), (1,)), ((), ()))
TN = (((0,), (0,)), ((), ()))


LOG2E = 1.4426950408889634
POS_TERMS = 3


def _pos_columns(lane, pos):
    first = lane - NSA_DH
    return jnp.where((first >= 0) & (first < POS_TERMS), pos >> 6,
                     jnp.where((first >= POS_TERMS) & (first < 2 * POS_TERMS), pos & 63, 0))


def _slope_rows(rowi, coef):
    terms = []
    rem = coef
    for _ in range(POS_TERMS):
        t = rem.astype(BF).astype(F32)
        terms.append(t)
        rem = rem - t
    out = jnp.zeros(rowi.shape, F32)
    for i, t in enumerate(terms):
        out = jnp.where(rowi == i, t * float(SLC_LEN), jnp.where(rowi == POS_TERMS + i, t, out))
    return out


def _resident(shape, index_map):
    return pl.BlockSpec(shape, index_map, pipeline_mode=pl.Buffered(1))


def _params(sem):
    return pltpu.CompilerParams(dimension_semantics=sem, vmem_limit_bytes=VMEM_LIMIT)


SRC_GA = 3072
SRC_NQ = 3088
SRC_KV = 3600
SRC_GATE = 4368
SRC_MERGE = 4392
IN_WIDTH = 6440
TAIL_START = IN_WIDTH // LANES * LANES
ALIGNED = OFF_MG
SHIFTED_RUNS = ((SRC_MERGE, 2048, OFF_MG, 1.0), (SRC_NQ, 512, OFF_NQ, NSA_DH ** -0.5), (SRC_KV, 768, OFF_KC, 1.0))


def _in_proj_kernel(x_ref, g_ref, wr_ref, wt_ref, om_ref, os_ref, wm_scr, ws_scr):
    @pl.when(pl.program_id(0) == 0)
    def _():
        r_i = lax.broadcasted_iota(jnp.int32, (2 * LANES, LANES), 0)
        c_i = lax.broadcasted_iota(jnp.int32, (2 * LANES, LANES), 1)
        for src, width, dst, scale in SHIFTED_RUNS:
            shift = jnp.where(r_i == c_i + src % LANES, 1.0, 0.0).astype(BF)
            for j in range(width // LANES):
                b0 = (src // LANES + j) * LANES
                if b0 + 2 * LANES <= TAIL_START:
                    pair = wr_ref[:, b0:b0 + 2 * LANES]
                else:
                    pair = jnp.concatenate([wr_ref[:, b0:b0 + LANES], wt_ref[...]], axis=1)
                moved = jnp.dot(pair, shift, preferred_element_type=F32)
                d0 = dst - ALIGNED + j * LANES
                wm_scr[:, d0:d0 + LANES] = (moved * scale).astype(BF)
        r1 = lax.broadcasted_iota(jnp.int32, (LANES, LANES), 0)
        c1 = lax.broadcasted_iota(jnp.int32, (LANES, LANES), 1)
        rep3 = jnp.where((r1 < GLA_RANK) & (c1 < 3 * GLA_RANK) & ((c1 % GLA_RANK) == r1), 1.0, 0.0).astype(BF)
        ws_scr[:, 0:LANES] = jnp.dot(wr_ref[:, SRC_GA:SRC_GA + LANES], rep3,
                                     preferred_element_type=F32).astype(BF)
        g_blk = SRC_GATE // LANES * LANES
        per_g = NSA_REP * 3
        for g in range(NSA_GROUPS):
            pick = jnp.where((c1 < per_g) & (r1 == c1 + SRC_GATE - g_blk + per_g * g), 1.0, 0.0).astype(BF)
            ws_scr[:, (1 + g) * LANES:(2 + g) * LANES] = jnp.dot(
                wr_ref[:, g_blk:g_blk + LANES], pick, preferred_element_type=F32).astype(BF)

    x = x_ref[...]
    h = (x * lax.rsqrt(jnp.mean(x * x, axis=-1, keepdims=True) + EPS) * g_ref[...]).astype(BF)
    os_ref[...] = jnp.dot(h, ws_scr[...], preferred_element_type=F32)
    step = 512
    for c0 in range(0, ALIGNED, step):
        om_ref[:, c0:c0 + step] = jnp.dot(h, wr_ref[:, c0:c0 + step], preferred_element_type=F32).astype(BF)
    for c0 in range(0, N_MAIN - ALIGNED, step):
        c1_ = min(c0 + step, N_MAIN - ALIGNED)
        om_ref[:, ALIGNED + c0:ALIGNED + c1_] = jnp.dot(h, wm_scr[:, c0:c1_],
                                                        preferred_element_type=F32).astype(BF)


def _in_proj(x2, g, w_raw, w_tail, tm=512):
    n_tok = x2.shape[0]
    return pl.pallas_call(
        _in_proj_kernel,
        grid=(n_tok // tm,),
        in_specs=[
            pl.BlockSpec((tm, D_MODEL), lambda i: (i, 0)),
            _resident((1, D_MODEL), lambda i: (0, 0)),
            _resident((D_MODEL, IN_WIDTH), lambda i: (0, 0)),
            _resident((D_MODEL, LANES), lambda i: (0, 0)),
        ],
        out_specs=[
            pl.BlockSpec((tm, N_MAIN), lambda i: (i, 0)),
            pl.BlockSpec((tm, N_SMALL), lambda i: (i, 0)),
        ],
        out_shape=[
            jax.ShapeDtypeStruct((n_tok, N_MAIN), BF),
            jax.ShapeDtypeStruct((n_tok, N_SMALL), F32),
        ],
        scratch_shapes=[pltpu.VMEM((D_MODEL, N_MAIN - ALIGNED), BF), pltpu.VMEM((D_MODEL, N_SMALL), BF)],
        compiler_params=_params(("arbitrary",)),
    )(x2, g, w_raw, w_tail)


def _gla_kernel(q_ref, k_ref, v_ref, r_ref, a_ref, w2_ref, b2_ref, ng_ref, o_ref, st_scr, *, n_chunks, n_heads):
    blk = pl.program_id(2)

    @pl.when(blk == 0)
    def _():
        st_scr[...] = jnp.zeros_like(st_scr)

    C = GLA_CHUNK
    T = n_chunks * C
    W = n_heads * GLA_DK

    def split3(x):
        hi = x.astype(BF)
        rem = x - hi.astype(F32)
        mid = rem.astype(BF)
        return hi, mid, (rem - mid.astype(F32)).astype(BF)

    a = a_ref[0]
    a_hi = a.astype(BF)
    a_lo = (a - a_hi.astype(F32)).astype(BF)
    lane = lax.broadcasted_iota(jnp.int32, (T, LANES), 1)
    in_lo = (lane >= GLA_RANK) & (lane < 2 * GLA_RANK)
    z = jnp.dot(jnp.where(in_lo, a_lo, a_hi), w2_ref[...], preferred_element_type=F32) + b2_ref[...]
    log_a = (jnp.minimum(z, 0.0) - jnp.log(1.0 + jnp.exp(-jnp.abs(z)))) * (1.0 / GLA_TAU)

    x_wide = jnp.concatenate([log_a[c * C:(c + 1) * C] for c in range(n_chunks)], axis=1)
    x3 = jnp.concatenate(split3(x_wide), axis=0)
    r3 = lax.broadcasted_iota(jnp.int32, (C, 3 * C), 0)
    c3 = lax.broadcasted_iota(jnp.int32, (C, 3 * C), 1) & (C - 1)
    tri3 = jnp.where(c3 <= r3, 1.0, 0.0).astype(BF)
    b_wide = jnp.dot(tri3, x3, preferred_element_type=F32)
    bcum = jnp.concatenate([b_wide[:, c * W:(c + 1) * W] for c in range(n_chunks)], axis=0)
    last_rows = [b_wide[C - 1:C, c * W:(c + 1) * W] for c in range(n_chunks)]
    b_last = jnp.concatenate([jnp.broadcast_to(lr, (C, W)) for lr in last_rows], axis=0)
    decay = [jnp.exp(lr) for lr in last_rows]

    q = q_ref[0].astype(F32)
    k = k_ref[0].astype(F32)
    v = v_ref[0]
    qe = (q * ((GLA_DK ** -0.5) * jnp.exp(bcum))).astype(BF)
    ke = (k * jnp.exp(-bcum)).astype(BF)
    kd = (k * jnp.exp(b_last - bcum)).astype(BF)

    H = min(T, 4 * C)
    row = lax.broadcasted_iota(jnp.int32, (H, H), 0)
    col = lax.broadcasted_iota(jnp.int32, (H, H), 1)
    keep = (col <= row) & ((col >> 6) == (row >> 6))
    ng = ng_ref[...]
    r_all = r_ref[0].astype(F32)
    for hh in range(n_heads):
        ks_ = slice(hh * GLA_DK, (hh + 1) * GLA_DK)
        vs_ = slice(hh * GLA_DV, (hh + 1) * GLA_DV)
        intra = []
        for h0 in range(0, T, H):
            hs = slice(h0, h0 + H)
            attn = lax.dot_general(qe[hs, ks_], ke[hs, ks_], NT, preferred_element_type=F32)
            intra.append(jnp.dot(jnp.where(keep, attn, 0.0).astype(BF), v[hs, vs_], preferred_element_type=F32))
        o = jnp.concatenate(intra, axis=0)

        st = st_scr[hh]
        inter = []
        for c in range(n_chunks):
            sl = slice(c * C, (c + 1) * C)
            inter.append(lax.dot_general(qe[sl, ks_], st.astype(BF), NT, preferred_element_type=F32))
            upd = lax.dot_general(v[sl, vs_], kd[sl, ks_], TN, preferred_element_type=F32)
            st = st * decay[c][:, ks_] + upd
        st_scr[hh] = st
        o = o + jnp.concatenate(inter, axis=0)

        o = o * lax.rsqrt(jnp.mean(o * o, axis=-1, keepdims=True) + EPS) * ng
        r = r_all[:, vs_]
        o_ref[0, :, vs_] = (o * (r * jax.nn.sigmoid(r))).astype(BF)


def _gla(proj3, small3, w2p, b2, ng, blk_tokens=512, heads_per_step=4):
    B, S, _ = proj3.shape
    nblk = S // blk_tokens
    hb = heads_per_step
    kern = functools.partial(_gla_kernel, n_chunks=blk_tokens // GLA_CHUNK, n_heads=hb)
    wk, wv = hb * GLA_DK, hb * GLA_DV
    return pl.pallas_call(
        kern,
        grid=(B, GLA_HEADS // hb, nblk),
        in_specs=[
            pl.BlockSpec((1, blk_tokens, wk), lambda b, h, i: (b, i, OFF_GQ // wk + h)),
            pl.BlockSpec((1, blk_tokens, wk), lambda b, h, i: (b, i, OFF_GK // wk + h)),
            pl.BlockSpec((1, blk_tokens, wv), lambda b, h, i: (b, i, OFF_GV // wv + h)),
            pl.BlockSpec((1, blk_tokens, wv), lambda b, h, i: (b, i, OFF_GR // wv + h)),
            pl.BlockSpec((1, blk_tokens, LANES), lambda b, h, i: (b, i, 0)),
            pl.BlockSpec((LANES, wk), lambda b, h, i: (0, h)),
            pl.BlockSpec((1, wk), lambda b, h, i: (0, h)),
            pl.BlockSpec((1, GLA_DV), lambda b, h, i: (0, 0)),
        ],
        out_specs=pl.BlockSpec((1, blk_tokens, wv), lambda b, h, i: (b, i, h)),
        out_shape=jax.ShapeDtypeStruct((B, S, GLA_HEADS * GLA_DV), BF),
        scratch_shapes=[pltpu.VMEM((hb, GLA_DV, GLA_DK), F32)],
        compiler_params=_params(("parallel", "parallel", "arbitrary")),
    )(proj3, proj3, proj3, proj3, small3, w2p, b2, ng)


def _compress_kernel(xk_ref, xv_ref, pek_ref, pev_ref, w1k_ref, w1v_ref, w2k_ref, w2v_ref, kc_ref, vc_ref):
    n_rows = xk_ref.shape[1]
    half = CMP_STRIDE * LANES
    lane = lax.broadcasted_iota(jnp.int32, (n_rows, LANES), 1)
    row = lax.broadcasted_iota(jnp.int32, (n_rows, LANES), 0)
    end_c = CMP_STRIDE * row + (CMP_LEN - 1)
    c_k = _pos_columns(lane, end_c).astype(F32)
    c_v = jnp.where(lane == NSA_DH, 1.0, 0.0).astype(F32)

    def branch(x_ref, pe_ref, w1_ref, w2_ref, const, o_ref):
        x = x_ref[0].astype(F32)
        xa = (x + pe_ref[0:1, :]).astype(BF)
        xb = (x + pe_ref[1:2, :]).astype(BF)
        a = jnp.dot(xa, w1_ref[0:half, :], preferred_element_type=F32)
        b = jnp.dot(xb, w1_ref[half:2 * half, :], preferred_element_type=F32)
        pre = a + pltpu.roll(b, n_rows - 1, 0)
        hid = (pre * jax.nn.sigmoid(pre)).astype(BF)
        for g in range(NSA_GROUPS):
            o_ref[0, g] = (jnp.dot(hid, w2_ref[g], preferred_element_type=F32) + const).astype(BF)

    branch(xk_ref, pek_ref, w1k_ref, w2k_ref, c_k, kc_ref)
    branch(xv_ref, pev_ref, w1v_ref, w2v_ref, c_v, vc_ref)


def _nsa_compress(xk, xv, pek, pev, w1k, w1v, w2k, w2v):
    B, n_rows, width = xk.shape
    full = lambda shape: _resident(shape, lambda b: (0,) * len(shape))
    out = jax.ShapeDtypeStruct((B, NSA_GROUPS, n_rows, LANES), BF)
    return pl.pallas_call(
        _compress_kernel,
        grid=(B,),
        in_specs=[
            pl.BlockSpec((1, n_rows, width), lambda b: (b, 0, 0)),
            pl.BlockSpec((1, n_rows, width), lambda b: (b, 0, 0)),
            full(pek.shape), full(pev.shape), full(w1k.shape), full(w1v.shape),
            full(w2k.shape), full(w2v.shape),
        ],
        out_specs=[pl.BlockSpec((1, NSA_GROUPS, n_rows, LANES), lambda b: (b, 0, 0, 0))] * 2,
        out_shape=[out, out],
        compiler_params=_params(("parallel",)),
    )(xk, xv, pek, pev, w1k, w1v, w2k, w2v)


def _nsa_kernel(q_ref, ks_ref, vs_ref, kw_ref, vw_ref, kc_ref, vc_ref, gt_ref, sl_ref, ov_ref,
                o_ref, ksa, kwa, vst, vwt, vct, acc_scr, act_ref, *, seq):
    qi = pl.program_id(2)
    n_slc = seq // SLC_LEN
    R = NSA_REP
    KT = SLC_TILE
    WK = WINDOW + QB
    n_cmp = kc_ref.shape[2]

    def t_bf(x):
        return x.astype(F32).T[0:VROWS].astype(BF)

    @pl.when(qi == 0)
    def _():
        grp = pl.program_id(1)
        lane = lax.broadcasted_iota(jnp.int32, (seq, LANES), 1)
        pos = lax.broadcasted_iota(jnp.int32, (seq, LANES), 0)
        blk = pos >> 6
        off = pos & 63
        r_i = lax.broadcasted_iota(jnp.int32, (LANES, LANES), 0)
        c_i = lax.broadcasted_iota(jnp.int32, (LANES, LANES), 1)
        pick = jnp.where((c_i < NSA_DH) & (r_i == c_i + grp * NSA_DH), 1.0, 0.0).astype(BF)
        c_s = jnp.where(lane == LANES - 1, off,
                        jnp.where((lane >= NSA_DH) & (lane - (NSA_DH - 1) == blk), 1, 0))
        ksa[0:seq, :] = (jnp.dot(ks_ref[0], pick, preferred_element_type=F32) + c_s.astype(F32)).astype(BF)
        lane_d = lax.broadcasted_iota(jnp.int32, (KT, LANES), 1)
        ksa[seq:seq + KT, :] = jnp.where((lane_d >= NSA_DH) & (lane_d < LANES - 1), 1, 0).astype(BF)
        for c in range(KT // LANES):
            vst[seq // LANES + c] = jnp.zeros((VROWS, LANES), BF)
        c_w = _pos_columns(lane, pos)
        kwa[0:WINDOW, :] = jnp.zeros((WINDOW, LANES), BF)
        kwa[WINDOW:WINDOW + seq, :] = (jnp.dot(kw_ref[0], pick, preferred_element_type=F32)
                                       + c_w.astype(F32)).astype(BF)
        ones_rows = jnp.where(lax.broadcasted_iota(jnp.int32, (VROWS - NSA_DH, LANES), 0) == 0, 1.0, 0.0)
        n_pad = WINDOW // LANES
        for c in range(n_pad):
            vwt[c] = jnp.zeros((VROWS, LANES), BF)

        def v_tile(x):
            xt = x.astype(F32).T
            dims = jnp.where(grp == 0, xt[0:NSA_DH], xt[NSA_DH:2 * NSA_DH])
            return jnp.concatenate([dims, ones_rows], axis=0).astype(BF)

        def fill(c, carry):
            rows = pl.ds(pl.multiple_of(c * LANES, LANES), LANES)
            vst[c] = v_tile(vs_ref[0, rows, :])
            vwt[c + n_pad] = v_tile(vw_ref[0, rows, :])
            return carry

        lax.fori_loop(0, seq // LANES, fill, 0)
        for c in range(n_cmp // LANES):
            vct[:, c * LANES:(c + 1) * LANES] = t_bf(vc_ref[0, 0, c * LANES:(c + 1) * LANES, :])

    q0 = qi * QB
    rowi = lax.broadcasted_iota(jnp.int32, (NSA_DH, QB), 0)
    slopes = [jnp.concatenate([sl_ref[0, r:r + 1, :]] * (QB // LANES), axis=1) for r in range(R)]
    q_all = q_ref[0].astype(F32).T
    q_t = [q_all[r * NSA_DH:(r + 1) * NSA_DH] for r in range(R)]

    qw = jnp.concatenate(
        [jnp.concatenate([q_t[r] * LOG2E, _slope_rows(rowi, slopes[r] * LOG2E)], axis=0).astype(BF)
         for r in range(R)], axis=1)

    def tile4(x):
        return jnp.concatenate([x] * R, axis=1)

    ovt = ov_ref[...]

    def cmp_branch(rows):
        def fn():
            s_c = jnp.dot(kc_ref[0, 0, 0:rows, :], qw, preferred_element_type=F32)
            e_c = CMP_STRIDE * lax.broadcasted_iota(jnp.int32, (rows, QB), 0) + (CMP_LEN - 1)
            t_c = q0 + lax.broadcasted_iota(jnp.int32, (rows, QB), 1)
            s_c = s_c + tile4(jnp.where(e_c <= t_c, 0.0, NEG))
            m_c = jnp.maximum(jnp.max(s_c, axis=0, keepdims=True), 0.1 * NEG)
            p_c = jnp.exp2(s_c - m_c)
            l_c = jnp.sum(p_c, axis=0, keepdims=True)
            p_c = p_c * jnp.where(l_c > 0.0, 1.0 / l_c, 0.0)
            o_cmp = jnp.dot(vct[:, 0:rows], p_c.astype(BF), preferred_element_type=F32)
            psum = p_c[:, 0:QB]
            for r in range(1, R):
                psum = psum + p_c[:, r * QB:(r + 1) * QB]
            p_hi = psum.astype(BF)
            rem = psum - p_hi.astype(F32)
            p_mid = rem.astype(BF)
            p_lo = (rem - p_mid.astype(F32)).astype(BF)
            ov = ovt[:, 0:rows]
            return o_cmp, (jnp.dot(ov, p_hi, preferred_element_type=F32)
                           + jnp.dot(ov, p_mid, preferred_element_type=F32)
                           + jnp.dot(ov, p_lo, preferred_element_type=F32))
        return fn

    half = n_cmp // 2
    o_c, imp = lax.cond(q0 + QB <= CMP_STRIDE * half + CMP_LEN - 1, cmp_branch(half), cmp_branch(n_cmp))

    NR = ov_ref.shape[0]
    SUB = 8
    jblk = lax.broadcasted_iota(jnp.int32, (NR, QB), 0)
    t_q = q0 + lax.broadcasted_iota(jnp.int32, (NR, QB), 1)
    cur = t_q >> 6
    forced = (jblk == 0) | (jblk == cur) | (jblk == cur - 1)
    score = jnp.where(jblk > cur, NEG, jnp.where(forced, -NEG, imp))
    n_slab = -(-n_slc // SUB)
    isub = lax.broadcasted_iota(jnp.int32, (SUB, QB), 0)

    def rank_counts(ns):
        def fn():
            slabs = [score[a * SUB:(a + 1) * SUB, :] for a in range(ns)]
            cnts = [jnp.zeros((SUB, QB), F32) for _ in range(ns)]
            for jp in range(min(ns * SUB, n_slc)):
                rowv = jnp.broadcast_to(score[jp:jp + 1, :], (SUB, QB))
                for a in range(ns):
                    if a < jp // SUB:
                        beats = jnp.where(rowv > slabs[a], 1.0, 0.0)
                    elif a > jp // SUB:
                        beats = jnp.where(rowv >= slabs[a], 1.0, 0.0)
                    else:
                        beats = jnp.where(isub > jp % SUB, jnp.where(rowv >= slabs[a], 1.0, 0.0),
                                          jnp.where(rowv > slabs[a], 1.0, 0.0))
                    cnts[a] = cnts[a] + beats
            rest = NR - ns * SUB
            return jnp.concatenate(cnts + [jnp.full((rest, QB), float(NR), F32)] * (rest > 0), axis=0)
        return fn

    last_blk = (q0 + QB - 1) >> 6
    cnt = lax.switch(last_blk // SUB, [rank_counts(ns) for ns in range(1, n_slab + 1)])
    sel = (cnt < float(N_SEL)) & (jblk <= cur) & (jblk < n_slc)
    a_nat = jnp.where(jblk == 0, 1.0, jnp.where(sel, (SLC_LEN * jblk).astype(F32), NEG))
    a_nat = jnp.where(jblk < n_slc, a_nat, 0.0)
    a_t = pltpu.roll(a_nat, NR - 1, 0)

    any_q = jnp.max(jnp.where(sel, 1.0, 0.0), axis=1, keepdims=True)
    jcol = lax.broadcasted_iota(jnp.int32, (NR, 1), 0)
    bits = jnp.where(any_q > 0.0, lax.shift_left(jnp.int32(1), jcol & 31), 0)
    word0 = jnp.sum(jnp.where(jcol < 32, bits, 0))
    word1 = jnp.sum(jnp.where(jcol >= 32, bits, 0))

    qs = jnp.concatenate([jnp.concatenate([q_t[r], a_t * slopes[r]], axis=0).astype(BF) for r in range(R)],
                         axis=1)

    tiles_per = KT // LANES
    blocks_per = KT // SLC_LEN
    n_full = q0 // KT

    def scan(ti, n):
        word = jnp.where(ti < 32 // blocks_per, word0, word1)
        hit = (lax.shift_right_logical(word, (ti * blocks_per) & 31) & ((1 << blocks_per) - 1)) != 0
        act_ref[n] = ti
        return n + jnp.where(hit, 1, 0)

    n_act = lax.fori_loop(0, n_full, scan, 0)
    act_ref[n_act] = seq // KT

    kw = kwa[pl.ds(pl.multiple_of(q0, QB), WK), :]
    s_w = jnp.dot(kw, qw, preferred_element_type=F32)
    c_w = lax.broadcasted_iota(jnp.int32, (WK, QB), 0)
    d_w = c_w - lax.broadcasted_iota(jnp.int32, (WK, QB), 1)
    mask_w = (d_w > 0) & (d_w <= WINDOW) & (c_w >= WINDOW - q0)
    s_w = s_w + tile4(jnp.where(mask_w, 0.0, NEG))
    m_w = jnp.max(s_w, axis=0, keepdims=True)
    p_w = jnp.exp2(s_w - m_w).astype(BF)

    def scores(ti):
        k0 = pl.multiple_of(ti * KT, KT)
        return jnp.dot(ksa[pl.ds(k0, KT), :], qs, preferred_element_type=F32)

    def values_t(ti):
        return [vst[ti * tiles_per + c] for c in range(tiles_per)]

    def absorb(s, v_tiles, m_prev):
        m_new = jnp.maximum(m_prev, jnp.max(s, axis=0, keepdims=True))
        alpha = jnp.exp(m_prev - m_new)
        p = jnp.exp(s - m_new).astype(BF)
        acc_scr[...] = acc_scr[...] * alpha + jnp.dot(jnp.concatenate(v_tiles, axis=1), p,
                                                     preferred_element_type=F32)
        return m_new

    acc_scr[...] = jnp.zeros(acc_scr.shape, F32)

    def chains(first, count, m_run):
        tiles = [act_ref[first + i] for i in range(count)]
        s_all = [scores(t) for t in tiles]
        for t, s in zip(tiles, s_all):
            m_run = absorb(s, values_t(t), m_run)
        return m_run

    n_quad = n_act >> 2
    m_d = lax.fori_loop(0, n_quad, lambda j, m: chains(4 * j, 4, m), jnp.full((1, R * QB), NEG, F32))
    n_rest = n_act - 4 * n_quad
    m_d = lax.fori_loop(0, (n_rest + 1) >> 1, lambda j, m: chains(4 * n_quad + 2 * j, 2, m), m_d)
    for d in range(QB // KT):
        ti = n_full + d
        p_d = ti * KT + lax.broadcasted_iota(jnp.int32, (KT, QB), 0)
        t_d = q0 + lax.broadcasted_iota(jnp.int32, (KT, QB), 1)
        m_d = absorb(scores(ti) + tile4(jnp.where(p_d <= t_d, 0.0, NEG)), values_t(ti), m_d)
    acc_s = acc_scr[...]

    vw_t = jnp.concatenate([vwt[qi * (QB // LANES) + c] for c in range(WK // LANES)], axis=1)
    acc_w = jnp.dot(vw_t, p_w, preferred_element_type=F32)


    sg_t = jax.nn.sigmoid(gt_ref[0]).T
    outs = []
    for r in range(R):
        cs = slice(r * QB, (r + 1) * QB)
        a_s = acc_s[:, cs]
        a_w = acc_w[:, cs]
        g_c = sg_t[3 * r:3 * r + 1, :]
        g_s = sg_t[3 * r + 1:3 * r + 2, :] / a_s[NSA_DH:NSA_DH + 1, :]
        g_w = sg_t[3 * r + 2:3 * r + 3, :] / a_w[NSA_DH:NSA_DH + 1, :]
        outs.append((g_c * o_c[:, cs] + g_s * a_s + g_w * a_w)[0:NSA_DH])
    o_ref[0] = jnp.concatenate(outs, axis=0).T.astype(BF)


def _nsa_attend(proj3, small3, kc, vc, slope_tab, ovt):
    B, S, _ = proj3.shape
    G, R = NSA_GROUPS, NSA_REP
    n_cmp = kc.shape[2]
    kern = functools.partial(_nsa_kernel, seq=S)
    kv_spec = lambda off: pl.BlockSpec((1, S, LANES), lambda b, g, i: (b, 0, off // LANES))
    return pl.pallas_call(
        kern,
        grid=(B, G, S // QB),
        in_specs=[
            pl.BlockSpec((1, QB, R * NSA_DH), lambda b, g, i: (b, i, OFF_NQ // (R * NSA_DH) + g)),
            kv_spec(OFF_KS), kv_spec(OFF_VS), kv_spec(OFF_KW), kv_spec(OFF_VW),
            pl.BlockSpec((1, 1, n_cmp, LANES), lambda b, g, i: (b, g, 0, 0)),
            pl.BlockSpec((1, 1, n_cmp, LANES), lambda b, g, i: (b, g, 0, 0)),
            pl.BlockSpec((1, QB, LANES), lambda b, g, i: (b, i, 1 + g)),
            pl.BlockSpec((1, 8, LANES), lambda b, g, i: (g, 0, 0)),
            _resident(ovt.shape, lambda b, g, i: (0, 0)),
        ],
        out_specs=pl.BlockSpec((1, QB, R * NSA_DH), lambda b, g, i: (b, i, g)),
        out_shape=jax.ShapeDtypeStruct((B, S, G * R * NSA_DH), BF),
        scratch_shapes=[
            pltpu.VMEM((S + SLC_TILE, LANES), BF),
            pltpu.VMEM((S + WINDOW, LANES), BF),
            pltpu.VMEM(((S + SLC_TILE) // LANES, VROWS, LANES), BF),
            pltpu.VMEM(((S + WINDOW) // LANES, VROWS, LANES), BF),
            pltpu.VMEM((VROWS, n_cmp), BF),
            pltpu.VMEM((VROWS, R * QB), F32),
            pltpu.SMEM((S // SLC_TILE + 1,), jnp.int32),
        ],
        compiler_params=_params(("parallel", "parallel", "arbitrary")),
    )(proj3, proj3, proj3, proj3, proj3, kc, vc, small3, slope_tab, ovt)


def _merge_kernel(og_ref, on_ref, mg_ref, mn_ref, x_ref, wg_ref, wn_ref, wo_ref, g_ref, o_ref):
    a = jnp.dot(og_ref[...], wg_ref[...], preferred_element_type=F32)
    b = jnp.dot(on_ref[...], wn_ref[...], preferred_element_type=F32)
    mixed = jax.nn.sigmoid(mg_ref[...].astype(F32)) * a + jax.nn.sigmoid(mn_ref[...].astype(F32)) * b
    y = jnp.dot(mixed.astype(BF), wo_ref[...], preferred_element_type=F32)
    y = y * lax.rsqrt(jnp.mean(y * y, axis=-1, keepdims=True) + EPS) * g_ref[...]
    o_ref[...] = x_ref[...] + y


def _merge_out(o_gla2, o_nsa2, proj2, x2, wg, wn, wo, g, tm=512):
    n_tok = x2.shape[0]
    D = D_MODEL
    return pl.pallas_call(
        _merge_kernel,
        grid=(n_tok // tm,),
        in_specs=[
            pl.BlockSpec((tm, D), lambda i: (i, 0)),
            pl.BlockSpec((tm, o_nsa2.shape[1]), lambda i: (i, 0)),
            pl.BlockSpec((tm, D), lambda i: (i, OFF_MG // D)),
            pl.BlockSpec((tm, D), lambda i: (i, OFF_MN // D)),
            pl.BlockSpec((tm, D), lambda i: (i, 0)),
            _resident(wg.shape, lambda i: (0, 0)),
            _resident(wn.shape, lambda i: (0, 0)),
            _resident(wo.shape, lambda i: (0, 0)),
            _resident((1, D), lambda i: (0, 0)),
        ],
        out_specs=pl.BlockSpec((tm, D), lambda i: (i, 0)),
        out_shape=jax.ShapeDtypeStruct((n_tok, D), F32),
        compiler_params=_params(("parallel",)),
    )(o_gla2, o_nsa2, proj2, proj2, x2, wg, wn, wo, g)


def _ffn_kernel(x_ref, gpre_ref, wg_ref, wu_ref, wd_ref, gpost_ref, o_ref, acc_scr, *, chunk):
    x = x_ref[...]
    h = (x * lax.rsqrt(jnp.mean(x * x, axis=-1, keepdims=True) + EPS) * gpre_ref[...]).astype(BF)
    d_ff = wg_ref.shape[1]
    for n, c0 in enumerate(range(0, d_ff, chunk)):
        c1 = min(c0 + chunk, d_ff)
        a = jnp.dot(h, wg_ref[:, c0:c1], preferred_element_type=F32)
        u = jnp.dot(h, wu_ref[:, c0:c1], preferred_element_type=F32)
        t = (a * jax.nn.sigmoid(a) * u).astype(BF)
        part = jnp.dot(t, wd_ref[c0:c1, :], preferred_element_type=F32)
        if n == 0:
            acc_scr[...] = part
        else:
            acc_scr[...] += part
    f = acc_scr[...]
    o_ref[...] = x + f * lax.rsqrt(jnp.mean(f * f, axis=-1, keepdims=True) + EPS) * gpost_ref[...]


def _ffn(x2, gpre, wg, wu, wd, gpost, tm=512, chunk=512):
    n_tok = x2.shape[0]
    D = D_MODEL
    kern = functools.partial(_ffn_kernel, chunk=chunk)
    return pl.pallas_call(
        kern,
        grid=(n_tok // tm,),
        in_specs=[
            pl.BlockSpec((tm, D), lambda i: (i, 0)),
            _resident((1, D), lambda i: (0, 0)),
            _resident(wg.shape, lambda i: (0, 0)),
            _resident(wu.shape, lambda i: (0, 0)),
            _resident(wd.shape, lambda i: (0, 0)),
            _resident((1, D), lambda i: (0, 0)),
        ],
        out_specs=pl.BlockSpec((tm, D), lambda i: (i, 0)),
        out_shape=jax.ShapeDtypeStruct((n_tok, D), F32),
        scratch_shapes=[pltpu.VMEM((tm, D), F32)],
        compiler_params=_params(("parallel",)),
    )(x2, gpre, wg, wu, wd, gpost)


def _prep_compress(pe, w1, w2):
    eye = jnp.eye(NSA_GROUPS, dtype=F32)
    w1r = w1.reshape(CMP_LEN, NSA_DH, NSA_DH)
    w1e = jnp.einsum('lde,gh->lgdhe', w1r, eye).reshape(CMP_LEN * NSA_GROUPS * NSA_DH, NSA_GROUPS * NSA_DH)
    pe_e = jnp.broadcast_to(pe[:, None, :], (CMP_LEN, NSA_GROUPS, NSA_DH)).reshape(2, CMP_STRIDE * LANES)
    pe_e = jnp.pad(pe_e, ((0, 6), (0, 0)))
    w2e = jnp.stack([
        jnp.pad(jnp.pad(w2, ((g * NSA_DH, (NSA_GROUPS - 1 - g) * NSA_DH), (0, 0))), ((0, 0), (0, LANES - NSA_DH)))
        for g in range(NSA_GROUPS)])
    return pe_e.astype(F32), w1e.astype(BF), w2e.astype(BF)


def _overlap_table(seq):
    n_cmp = (seq - CMP_LEN) // CMP_STRIDE + 1
    n_slc = seq // SLC_LEN
    sc = CMP_STRIDE * np.arange(n_cmp)
    ss = SLC_LEN * np.arange(n_slc)
    ov = np.clip(np.minimum(sc[:, None] + CMP_LEN, ss[None, :] + SLC_LEN)
                 - np.maximum(sc[:, None], ss[None, :]), 0, None).astype(np.float32) / CMP_LEN
    ovt = np.zeros((NSA_DH, n_cmp + 1), np.float32)
    ovt[:n_slc, :n_cmp] = ov.T
    return jnp.asarray(ovt, dtype=BF)


def kernel(x, norm_mix_pre, norm_mix_post, norm_ffn_pre, norm_ffn_post, w_in, gla_w_alpha2, gla_b_alpha, gla_norm_g, nsa_cmp_pe_k, nsa_cmp_w1_k, nsa_cmp_w2_k, nsa_cmp_pe_v, nsa_cmp_w1_v, nsa_cmp_w2_v, w_proj_gla, w_proj_nsa, w_out, w_ffn_gate, w_ffn_up, w_ffn_down):
    B, S, D = x.shape
    depth = w_in.shape[0]
    n_tok = B * S
    h_idx = jnp.arange(NSA_HEADS, dtype=F32)
    slopes = jnp.exp2(-8.0 * (h_idx + 1.0) / NSA_HEADS).reshape(NSA_GROUPS, NSA_REP, 1)
    slope_tab = jnp.broadcast_to(jnp.pad(slopes, ((0, 0), (0, 8 - NSA_REP), (0, 0))), (NSA_GROUPS, 8, LANES))
    ovt = _overlap_table(S)
    x2 = x.reshape(n_tok, D)
    for l in range(depth):
        w_raw = w_in[l].astype(BF)
        w_tail = jnp.pad(w_raw[:, TAIL_START:], ((0, 0), (0, TAIL_START + LANES - IN_WIDTH)))
        proj2, small2 = _in_proj(x2, norm_mix_pre[l][None, :], w_raw, w_tail)
        proj3 = proj2.reshape(B, S, N_MAIN)
        small3 = small2.reshape(B, S, N_SMALL)

        w2_hi = gla_w_alpha2[l].astype(BF)
        w2_lo = (gla_w_alpha2[l] - w2_hi.astype(F32)).astype(BF)
        w2p = jnp.pad(jnp.concatenate([w2_hi, w2_hi, w2_lo], axis=0), ((0, LANES - 3 * GLA_RANK), (0, 0)))
        o_gla = _gla(proj3, small3, w2p, gla_b_alpha[l][None, :], gla_norm_g[l][None, :])

        xk = proj3[:, :, OFF_KC:OFF_KC + LANES].reshape(B, S // CMP_STRIDE, CMP_STRIDE * LANES)
        xv = proj3[:, :, OFF_VC:OFF_VC + LANES].reshape(B, S // CMP_STRIDE, CMP_STRIDE * LANES)
        pek, w1k, w2k = _prep_compress(nsa_cmp_pe_k[l], nsa_cmp_w1_k[l], nsa_cmp_w2_k[l])
        pev, w1v, w2v = _prep_compress(nsa_cmp_pe_v[l], nsa_cmp_w1_v[l], nsa_cmp_w2_v[l])
        kc, vc = _nsa_compress(xk, xv, pek, pev, w1k, w1v, w2k, w2v)
        o_nsa = _nsa_attend(proj3, small3, kc, vc, slope_tab, ovt)

        x2 = _merge_out(o_gla.reshape(n_tok, -1), o_nsa.reshape(n_tok, -1), proj2, x2,
                        w_proj_gla[l].astype(BF), w_proj_nsa[l].astype(BF),
                        w_out[l].astype(BF), norm_mix_post[l][None, :])
        x2 = _ffn(x2, norm_ffn_pre[l][None, :], w_ffn_gate[l].astype(BF), w_ffn_up[l].astype(BF),
                  w_ffn_down[l].astype(BF), norm_ffn_post[l][None, :])
    return x2.reshape(B, S, D)
```

```python
import functools

import numpy as np
import jax
import jax.numpy as jnp
from jax import lax
from jax.experimental import pallas as pl
from jax.experimental.pallas import tpu as pltpu

D_MODEL = 1024
GLA_HEADS = 4
GLA_DK = 128
GLA_DV = 256
GLA_RANK = 16
GLA_TAU = 16.0
GLA_CHUNK = 64
NSA_HEADS = 8
NSA_GROUPS = 2
NSA_REP = 4
NSA_DH = 64
CMP_LEN = 32
CMP_STRIDE = 16
SLC_LEN = 64
N_SEL = 16
WINDOW = 512
QB = 256
SLC_TILE = 256
VROWS = 80
D_FF = 2816
EPS = 1e-6
NEG = -1e30

LANES = 128
VMEM_LIMIT = 56 * 1024 * 1024
BF = jnp.bfloat16
F32 = jnp.float32

OFF_GQ = 0
OFF_GK = 512
OFF_GV = 1024
OFF_GR = 2048
OFF_MG = 3072
OFF_MN = 4096
OFF_NQ = 5120
OFF_KC = 5632
OFF_VC = 5760
OFF_KS = 5888
OFF_VS = 6016
OFF_KW = 6144
OFF_VW = 6272
N_MAIN = 6400
N_SMALL = 384

NT = (((1,), (1,)), ((), ()))
TN = (((0,), (0,)), ((), ()))


LOG2E = 1.4426950408889634
POS_TERMS = 3


def _pos_columns(lane, pos):
    first = lane - NSA_DH
    return jnp.where((first >= 0) & (first < POS_TERMS), pos >> 6,
                     jnp.where((first >= POS_TERMS) & (first < 2 * POS_TERMS), pos & 63, 0))


def _slope_rows(rowi, coef):
    terms = []
    rem = coef
    for _ in range(POS_TERMS):
        t = rem.astype(BF).astype(F32)
        terms.append(t)
        rem = rem - t
    out = jnp.zeros(rowi.shape, F32)
    for i, t in enumerate(terms):
        out = jnp.where(rowi == i, t * float(SLC_LEN), jnp.where(rowi == POS_TERMS + i, t, out))
    return out


def _resident(shape, index_map):
    return pl.BlockSpec(shape, index_map, pipeline_mode=pl.Buffered(1))


def _params(sem):
    return pltpu.CompilerParams(dimension_semantics=sem, vmem_limit_bytes=VMEM_LIMIT)


SRC_GA = 3072
SRC_NQ = 3088
SRC_KV = 3600
SRC_GATE = 4368
SRC_MERGE = 4392
IN_WIDTH = 6440
TAIL_START = IN_WIDTH // LANES * LANES
ALIGNED = OFF_MG
SHIFTED_RUNS = ((SRC_MERGE, 2048, OFF_MG, 1.0), (SRC_NQ, 512, OFF_NQ, NSA_DH ** -0.5), (SRC_KV, 768, OFF_KC, 1.0))


def _in_proj_kernel(x_ref, g_ref, wr_ref, wt_ref, om_ref, os_ref, wm_scr, ws_scr):
    @pl.when(pl.program_id(0) == 0)
    def _():
        r_i = lax.broadcasted_iota(jnp.int32, (2 * LANES, LANES), 0)
        c_i = lax.broadcasted_iota(jnp.int32, (2 * LANES, LANES), 1)
        for src, width, dst, scale in SHIFTED_RUNS:
            shift = jnp.where(r_i == c_i + src % LANES, 1.0, 0.0).astype(BF)
            for j in range(width // LANES):
                b0 = (src // LANES + j) * LANES
                if b0 + 2 * LANES <= TAIL_START:
                    pair = wr_ref[:, b0:b0 + 2 * LANES]
                else:
                    pair = jnp.concatenate([wr_ref[:, b0:b0 + LANES], wt_ref[...]], axis=1)
                moved = jnp.dot(pair, shift, preferred_element_type=F32)
                d0 = dst - ALIGNED + j * LANES
                wm_scr[:, d0:d0 + LANES] = (moved * scale).astype(BF)
        r1 = lax.broadcasted_iota(jnp.int32, (LANES, LANES), 0)
        c1 = lax.broadcasted_iota(jnp.int32, (LANES, LANES), 1)
        rep3 = jnp.where((r1 < GLA_RANK) & (c1 < 3 * GLA_RANK) & ((c1 % GLA_RANK) == r1), 1.0, 0.0).astype(BF)
        ws_scr[:, 0:LANES] = jnp.dot(wr_ref[:, SRC_GA:SRC_GA + LANES], rep3,
                                     preferred_element_type=F32).astype(BF)
        g_blk = SRC_GATE // LANES * LANES
        per_g = NSA_REP * 3
        for g in range(NSA_GROUPS):
            pick = jnp.where((c1 < per_g) & (r1 == c1 + SRC_GATE - g_blk + per_g * g), 1.0, 0.0).astype(BF)
            ws_scr[:, (1 + g) * LANES:(2 + g) * LANES] = jnp.dot(
                wr_ref[:, g_blk:g_blk + LANES], pick, preferred_element_type=F32).astype(BF)

    x = x_ref[...]
    h = (x * lax.rsqrt(jnp.mean(x * x, axis=-1, keepdims=True) + EPS) * g_ref[...]).astype(BF)
    os_ref[...] = jnp.dot(h, ws_scr[...], preferred_element_type=F32)
    step = 512
    for c0 in range(0, ALIGNED, step):
        om_ref[:, c0:c0 + step] = jnp.dot(h, wr_ref[:, c0:c0 + step], preferred_element_type=F32).astype(BF)
    for c0 in range(0, N_MAIN - ALIGNED, step):
        c1_ = min(c0 + step, N_MAIN - ALIGNED)
        om_ref[:, ALIGNED + c0:ALIGNED + c1_] = jnp.dot(h, wm_scr[:, c0:c1_],
                                                        preferred_element_type=F32).astype(BF)


def _in_proj(x2, g, w_raw, w_tail, tm=512):
    n_tok = x2.shape[0]
    return pl.pallas_call(
        _in_proj_kernel,
        grid=(n_tok // tm,),
        in_specs=[
            pl.BlockSpec((tm, D_MODEL), lambda i: (i, 0)),
            _resident((1, D_MODEL), lambda i: (0, 0)),
            _resident((D_MODEL, IN_WIDTH), lambda i: (0, 0)),
            _resident((D_MODEL, LANES), lambda i: (0, 0)),
        ],
        out_specs=[
            pl.BlockSpec((tm, N_MAIN), lambda i: (i, 0)),
            pl.BlockSpec((tm, N_SMALL), lambda i: (i, 0)),
        ],
        out_shape=[
            jax.ShapeDtypeStruct((n_tok, N_MAIN), BF),
            jax.ShapeDtypeStruct((n_tok, N_SMALL), F32),
        ],
        scratch_shapes=[pltpu.VMEM((D_MODEL, N_MAIN - ALIGNED), BF), pltpu.VMEM((D_MODEL, N_SMALL), BF)],
        compiler_params=_params(("arbitrary",)),
    )(x2, g, w_raw, w_tail)


def _gla_kernel(q_ref, k_ref, v_ref, r_ref, a_ref, w2_ref, b2_ref, ng_ref, o_ref, st_scr, *, n_chunks, n_heads):
    blk = pl.program_id(2)

    @pl.when(blk == 0)
    def _():
        st_scr[...] = jnp.zeros_like(st_scr)

    C = GLA_CHUNK
    T = n_chunks * C
    W = n_heads * GLA_DK

    def split3(x):
        hi = x.astype(BF)
        rem = x - hi.astype(F32)
        mid = rem.astype(BF)
        return hi, mid, (rem - mid.astype(F32)).astype(BF)

    a = a_ref[0]
    a_hi = a.astype(BF)
    a_lo = (a - a_hi.astype(F32)).astype(BF)
    lane = lax.broadcasted_iota(jnp.int32, (T, LANES), 1)
    in_lo = (lane >= GLA_RANK) & (lane < 2 * GLA_RANK)
    z = jnp.dot(jnp.where(in_lo, a_lo, a_hi), w2_ref[...], preferred_element_type=F32) + b2_ref[...]
    log_a = (jnp.minimum(z, 0.0) - jnp.log(1.0 + jnp.exp(-jnp.abs(z)))) * (1.0 / GLA_TAU)

    x_wide = jnp.concatenate([log_a[c * C:(c + 1) * C] for c in range(n_chunks)], axis=1)
    x3 = jnp.concatenate(split3(x_wide), axis=0)
    r3 = lax.broadcasted_iota(jnp.int32, (C, 3 * C), 0)
    c3 = lax.broadcasted_iota(jnp.int32, (C, 3 * C), 1) & (C - 1)
    tri3 = jnp.where(c3 <= r3, 1.0, 0.0).astype(BF)
    b_wide = jnp.dot(tri3, x3, preferred_element_type=F32)
    bcum = jnp.concatenate([b_wide[:, c * W:(c + 1) * W] for c in range(n_chunks)], axis=0)
    last_rows = [b_wide[C - 1:C, c * W:(c + 1) * W] for c in range(n_chunks)]
    b_last = jnp.concatenate([jnp.broadcast_to(lr, (C, W)) for lr in last_rows], axis=0)
    decay = [jnp.exp(lr) for lr in last_rows]

    q = q_ref[0].astype(F32)
    k = k_ref[0].astype(F32)
    v = v_ref[0]
    qe = (q * ((GLA_DK ** -0.5) * jnp.exp(bcum))).astype(BF)
    ke = (k * jnp.exp(-bcum)).astype(BF)
    kd = (k * jnp.exp(b_last - bcum)).astype(BF)

    H = min(T, 4 * C)
    row = lax.broadcasted_iota(jnp.int32, (H, H), 0)
    col = lax.broadcasted_iota(jnp.int32, (H, H), 1)
    keep = (col <= row) & ((col >> 6) == (row >> 6))
    ng = ng_ref[...]
    r_all = r_ref[0].astype(F32)
    for hh in range(n_heads):
        ks_ = slice(hh * GLA_DK, (hh + 1) * GLA_DK)
        vs_ = slice(hh * GLA_DV, (hh + 1) * GLA_DV)
        intra = []
        for h0 in range(0, T, H):
            hs = slice(h0, h0 + H)
            attn = lax.dot_general(qe[hs, ks_], ke[hs, ks_], NT, preferred_element_type=F32)
            intra.append(jnp.dot(jnp.where(keep, attn, 0.0).astype(BF), v[hs, vs_], preferred_element_type=F32))
        o = jnp.concatenate(intra, axis=0)

        st = st_scr[hh]
        inter = []
        for c in range(n_chunks):
            sl = slice(c * C, (c + 1) * C)
            inter.append(lax.dot_general(qe[sl, ks_], st.astype(BF), NT, preferred_element_type=F32))
            upd = lax.dot_general(v[sl, vs_], kd[sl, ks_], TN, preferred_element_type=F32)
            st = st * decay[c][:, ks_] + upd
        st_scr[hh] = st
        o = o + jnp.concatenate(inter, axis=0)

        o = o * lax.rsqrt(jnp.mean(o * o, axis=-1, keepdims=True) + EPS) * ng
        r = r_all[:, vs_]
        o_ref[0, :, vs_] = (o * (r * jax.nn.sigmoid(r))).astype(BF)


def _gla(proj3, small3, w2p, b2, ng, blk_tokens=512, heads_per_step=4):
    B, S, _ = proj3.shape
    nblk = S // blk_tokens
    hb = heads_per_step
    kern = functools.partial(_gla_kernel, n_chunks=blk_tokens // GLA_CHUNK, n_heads=hb)
    wk, wv = hb * GLA_DK, hb * GLA_DV
    return pl.pallas_call(
        kern,
        grid=(B, GLA_HEADS // hb, nblk),
        in_specs=[
            pl.BlockSpec((1, blk_tokens, wk), lambda b, h, i: (b, i, OFF_GQ // wk + h)),
            pl.BlockSpec((1, blk_tokens, wk), lambda b, h, i: (b, i, OFF_GK // wk + h)),
            pl.BlockSpec((1, blk_tokens, wv), lambda b, h, i: (b, i, OFF_GV // wv + h)),
            pl.BlockSpec((1, blk_tokens, wv), lambda b, h, i: (b, i, OFF_GR // wv + h)),
            pl.BlockSpec((1, blk_tokens, LANES), lambda b, h, i: (b, i, 0)),
            pl.BlockSpec((LANES, wk), lambda b, h, i: (0, h)),
            pl.BlockSpec((1, wk), lambda b, h, i: (0, h)),
            pl.BlockSpec((1, GLA_DV), lambda b, h, i: (0, 0)),
        ],
        out_specs=pl.BlockSpec((1, blk_tokens, wv), lambda b, h, i: (b, i, h)),
        out_shape=jax.ShapeDtypeStruct((B, S, GLA_HEADS * GLA_DV), BF),
        scratch_shapes=[pltpu.VMEM((hb, GLA_DV, GLA_DK), F32)],
        compiler_params=_params(("parallel", "parallel", "arbitrary")),
    )(proj3, proj3, proj3, proj3, small3, w2p, b2, ng)


def _compress_kernel(xk_ref, xv_ref, pek_ref, pev_ref, w1k_ref, w1v_ref, w2k_ref, w2v_ref, kc_ref, vc_ref):
    n_rows = xk_ref.shape[1]
    half = CMP_STRIDE * LANES
    lane = lax.broadcasted_iota(jnp.int32, (n_rows, LANES), 1)
    row = lax.broadcasted_iota(jnp.int32, (n_rows, LANES), 0)
    end_c = CMP_STRIDE * row + (CMP_LEN - 1)
    c_k = _pos_columns(lane, end_c).astype(F32)
    c_v = jnp.where(lane == NSA_DH, 1.0, 0.0).astype(F32)

    def branch(x_ref, pe_ref, w1_ref, w2_ref, const, o_ref):
        x = x_ref[0].astype(F32)
        xa = (x + pe_ref[0:1, :]).astype(BF)
        xb = (x + pe_ref[1:2, :]).astype(BF)
        a = jnp.dot(xa, w1_ref[0:half, :], preferred_element_type=F32)
        b = jnp.dot(xb, w1_ref[half:2 * half, :], preferred_element_type=F32)
        pre = a + pltpu.roll(b, n_rows - 1, 0)
        hid = (pre * jax.nn.sigmoid(pre)).astype(BF)
        for g in range(NSA_GROUPS):
            o_ref[0, g] = (jnp.dot(hid, w2_ref[g], preferred_element_type=F32) + const).astype(BF)

    branch(xk_ref, pek_ref, w1k_ref, w2k_ref, c_k, kc_ref)
    branch(xv_ref, pev_ref, w1v_ref, w2v_ref, c_v, vc_ref)


def _nsa_compress(xk, xv, pek, pev, w1k, w1v, w2k, w2v):
    B, n_rows, width = xk.shape
    full = lambda shape: _resident(shape, lambda b: (0,) * len(shape))
    out = jax.ShapeDtypeStruct((B, NSA_GROUPS, n_rows, LANES), BF)
    return pl.pallas_call(
        _compress_kernel,
        grid=(B,),
        in_specs=[
            pl.BlockSpec((1, n_rows, width), lambda b: (b, 0, 0)),
            pl.BlockSpec((1, n_rows, width), lambda b: (b, 0, 0)),
            full(pek.shape), full(pev.shape), full(w1k.shape), full(w1v.shape),
            full(w2k.shape), full(w2v.shape),
        ],
        out_specs=[pl.BlockSpec((1, NSA_GROUPS, n_rows, LANES), lambda b: (b, 0, 0, 0))] * 2,
        out_shape=[out, out],
        compiler_params=_params(("parallel",)),
    )(xk, xv, pek, pev, w1k, w1v, w2k, w2v)


def _nsa_kernel(q_ref, ks_ref, vs_ref, kw_ref, vw_ref, kc_ref, vc_ref, gt_ref, sl_ref, ov_ref,
                o_ref, ksa, kwa, vst, vwt, vct, acc_scr, act_ref, *, seq):
    qi = pl.program_id(2)
    n_slc = seq // SLC_LEN
    R = NSA_REP
    KT = SLC_TILE
    WK = WINDOW + QB
    n_cmp = kc_ref.shape[2]

    def t_bf(x):
        return x.astype(F32).T[0:VROWS].astype(BF)

    @pl.when(qi == 0)
    def _():
        grp = pl.program_id(1)
        lane = lax.broadcasted_iota(jnp.int32, (seq, LANES), 1)
        pos = lax.broadcasted_iota(jnp.int32, (seq, LANES), 0)
        blk = pos >> 6
        off = pos & 63
        r_i = lax.broadcasted_iota(jnp.int32, (LANES, LANES), 0)
        c_i = lax.broadcasted_iota(jnp.int32, (LANES, LANES), 1)
        pick = jnp.where((c_i < NSA_DH) & (r_i == c_i + grp * NSA_DH), 1.0, 0.0).astype(BF)
        c_s = jnp.where(lane == LANES - 1, off,
                        jnp.where((lane >= NSA_DH) & (lane - (NSA_DH - 1) == blk), 1, 0))
        ksa[...] = (jnp.dot(ks_ref[0], pick, preferred_element_type=F32) + c_s.astype(F32)).astype(BF)
        c_w = _pos_columns(lane, pos)
        kwa[0:WINDOW, :] = jnp.zeros((WINDOW, LANES), BF)
        kwa[WINDOW:WINDOW + seq, :] = (jnp.dot(kw_ref[0], pick, preferred_element_type=F32)
                                       + c_w.astype(F32)).astype(BF)
        ones_rows = jnp.where(lax.broadcasted_iota(jnp.int32, (VROWS - NSA_DH, LANES), 0) == 0, 1.0, 0.0)
        n_pad = WINDOW // LANES
        for c in range(n_pad):
            vwt[c] = jnp.zeros((VROWS, LANES), BF)

        def v_tile(x):
            xt = x.astype(F32).T
            dims = jnp.where(grp == 0, xt[0:NSA_DH], xt[NSA_DH:2 * NSA_DH])
            return jnp.concatenate([dims, ones_rows], axis=0).astype(BF)

        def fill(c, carry):
            rows = pl.ds(pl.multiple_of(c * LANES, LANES), LANES)
            vst[c] = v_tile(vs_ref[0, rows, :])
            vwt[c + n_pad] = v_tile(vw_ref[0, rows, :])
            return carry

        lax.fori_loop(0, seq // LANES, fill, 0)
        for c in range(n_cmp // LANES):
            vct[:, c * LANES:(c + 1) * LANES] = t_bf(vc_ref[0, 0, c * LANES:(c + 1) * LANES, :])

    q0 = qi * QB
    rowi = lax.broadcasted_iota(jnp.int32, (NSA_DH, QB), 0)
    slopes = [jnp.concatenate([sl_ref[0, r:r + 1, :]] * (QB // LANES), axis=1) for r in range(R)]
    q_all = q_ref[0].astype(F32).T
    q_t = [q_all[r * NSA_DH:(r + 1) * NSA_DH] for r in range(R)]

    qw = jnp.concatenate(
        [jnp.concatenate([q_t[r] * LOG2E, _slope_rows(rowi, slopes[r] * LOG2E)], axis=0).astype(BF)
         for r in range(R)], axis=1)

    def tile4(x):
        return jnp.concatenate([x] * R, axis=1)

    ovt = ov_ref[...]

    def cmp_branch(rows):
        def fn():
            s_c = jnp.dot(kc_ref[0, 0, 0:rows, :], qw, preferred_element_type=F32)
            e_c = CMP_STRIDE * lax.broadcasted_iota(jnp.int32, (rows, QB), 0) + (CMP_LEN - 1)
            t_c = q0 + lax.broadcasted_iota(jnp.int32, (rows, QB), 1)
            s_c = s_c + tile4(jnp.where(e_c <= t_c, 0.0, NEG))
            m_c = jnp.maximum(jnp.max(s_c, axis=0, keepdims=True), 0.1 * NEG)
            p_c = jnp.exp2(s_c - m_c)
            l_c = jnp.sum(p_c, axis=0, keepdims=True)
            p_c = p_c * jnp.where(l_c > 0.0, 1.0 / l_c, 0.0)
            o_cmp = jnp.dot(vct[:, 0:rows], p_c.astype(BF), preferred_element_type=F32)
            psum = p_c[:, 0:QB]
            for r in range(1, R):
                psum = psum + p_c[:, r * QB:(r + 1) * QB]
            p_hi = psum.astype(BF)
            rem = psum - p_hi.astype(F32)
            p_mid = rem.astype(BF)
            p_lo = (rem - p_mid.astype(F32)).astype(BF)
            ov = ovt[:, 0:rows]
            return o_cmp, (jnp.dot(ov, p_hi, preferred_element_type=F32)
                           + jnp.dot(ov, p_mid, preferred_element_type=F32)
                           + jnp.dot(ov, p_lo, preferred_element_type=F32))
        return fn

    half = n_cmp // 2
    o_c, imp = lax.cond(q0 + QB <= CMP_STRIDE * half + CMP_LEN - 1, cmp_branch(half), cmp_branch(n_cmp))

    NR = ov_ref.shape[0]
    SUB = 8
    jblk = lax.broadcasted_iota(jnp.int32, (NR, QB), 0)
    t_q = q0 + lax.broadcasted_iota(jnp.int32, (NR, QB), 1)
    cur = t_q >> 6
    forced = (jblk == 0) | (jblk == cur) | (jblk == cur - 1)
    score = jnp.where(jblk > cur, NEG, jnp.where(forced, -NEG, imp))
    n_slab = -(-n_slc // SUB)
    isub = lax.broadcasted_iota(jnp.int32, (SUB, QB), 0)

    def rank_counts(ns):
        def fn():
            slabs = [score[a * SUB:(a + 1) * SUB, :] for a in range(ns)]
            cnts = [jnp.zeros((SUB, QB), F32) for _ in range(ns)]
            for jp in range(min(ns * SUB, n_slc)):
                rowv = jnp.broadcast_to(score[jp:jp + 1, :], (SUB, QB))
                for a in range(ns):
                    if a < jp // SUB:
                        beats = jnp.where(rowv > slabs[a], 1.0, 0.0)
                    elif a > jp // SUB:
                        beats = jnp.where(rowv >= slabs[a], 1.0, 0.0)
                    else:
                        beats = jnp.where(isub > jp % SUB, jnp.where(rowv >= slabs[a], 1.0, 0.0),
                                          jnp.where(rowv > slabs[a], 1.0, 0.0))
                    cnts[a] = cnts[a] + beats
            rest = NR - ns * SUB
            return jnp.concatenate(cnts + [jnp.full((rest, QB), float(NR), F32)] * (rest > 0), axis=0)
        return fn

    last_blk = (q0 + QB - 1) >> 6
    cnt = lax.switch(last_blk // SUB, [rank_counts(ns) for ns in range(1, n_slab + 1)])
    sel = (cnt < float(N_SEL)) & (jblk <= cur) & (jblk < n_slc)
    a_nat = jnp.where(jblk == 0, 1.0, jnp.where(sel, (SLC_LEN * jblk).astype(F32), NEG))
    a_nat = jnp.where(jblk < n_slc, a_nat, 0.0)
    a_t = pltpu.roll(a_nat, NR - 1, 0)

    any_q = jnp.max(jnp.where(sel, 1.0, 0.0), axis=1, keepdims=True)
    jcol = lax.broadcasted_iota(jnp.int32, (NR, 1), 0)
    bits = jnp.where(any_q > 0.0, lax.shift_left(jnp.int32(1), jcol & 31), 0)
    word0 = jnp.sum(jnp.where(jcol < 32, bits, 0))
    word1 = jnp.sum(jnp.where(jcol >= 32, bits, 0))

    qs = jnp.concatenate([jnp.concatenate([q_t[r], a_t * slopes[r]], axis=0).astype(BF) for r in range(R)],
                         axis=1)

    tiles_per = KT // LANES
    blocks_per = KT // SLC_LEN
    n_full = q0 // KT

    def scan(ti, n):
        word = jnp.where(ti < 32 // blocks_per, word0, word1)
        hit = (lax.shift_right_logical(word, (ti * blocks_per) & 31) & ((1 << blocks_per) - 1)) != 0
        act_ref[n] = ti
        return n + jnp.where(hit, 1, 0)

    n_act = lax.fori_loop(0, n_full, scan, 0)

    kw = kwa[pl.ds(pl.multiple_of(q0, QB), WK), :]
    s_w = jnp.dot(kw, qw, preferred_element_type=F32)
    c_w = lax.broadcasted_iota(jnp.int32, (WK, QB), 0)
    d_w = c_w - lax.broadcasted_iota(jnp.int32, (WK, QB), 1)
    mask_w = (d_w > 0) & (d_w <= WINDOW) & (c_w >= WINDOW - q0)
    s_w = s_w + tile4(jnp.where(mask_w, 0.0, NEG))
    m_w = jnp.max(s_w, axis=0, keepdims=True)
    p_w = jnp.exp2(s_w - m_w).astype(BF)

    def scores(ti):
        k0 = pl.multiple_of(ti * KT, KT)
        return jnp.dot(ksa[pl.ds(k0, KT), :], qs, preferred_element_type=F32)

    def values_t(ti):
        return [vst[ti * tiles_per + c] for c in range(tiles_per)]

    def absorb(s, v_tiles, m_prev):
        m_new = jnp.maximum(m_prev, jnp.max(s, axis=0, keepdims=True))
        alpha = jnp.exp(m_prev - m_new)
        p = jnp.exp(s - m_new).astype(BF)
        acc_scr[...] = acc_scr[...] * alpha + jnp.dot(jnp.concatenate(v_tiles, axis=1), p,
                                                     preferred_element_type=F32)
        return m_new

    acc_scr[...] = jnp.zeros(acc_scr.shape, F32)

    def chains(tiles, diag_flags, m_run):
        s_all = []
        for t, is_diag in zip(tiles, diag_flags):
            s = scores(t)
            if is_diag:
                p_d = t * KT + lax.broadcasted_iota(jnp.int32, (KT, QB), 0)
                t_d = q0 + lax.broadcasted_iota(jnp.int32, (KT, QB), 1)
                s = s + tile4(jnp.where(p_d <= t_d, 0.0, NEG))
            s_all.append(s)
        for t, s in zip(tiles, s_all):
            m_run = absorb(s, values_t(t), m_run)
        return m_run

    n_quad = n_act >> 2
    m_q = lax.fori_loop(0, n_quad, lambda j, m: chains([act_ref[4 * j + i] for i in range(4)], [False] * 4, m),
                        jnp.full((1, R * QB), NEG, F32))
    n_diag = QB // KT

    def finish(rest):
        def fn(m_run):
            tiles = [act_ref[4 * n_quad + i] for i in range(rest)] + [n_full + d for d in range(n_diag)]
            chains(tiles, [False] * rest + [True] * n_diag, m_run)
            acc_s = acc_scr[...]
            vw_t = jnp.concatenate([vwt[qi * (QB // LANES) + c] for c in range(WK // LANES)], axis=1)
            acc_w = jnp.dot(vw_t, p_w, preferred_element_type=F32)
            sg_t = jax.nn.sigmoid(gt_ref[0]).T
            outs = []
            for r in range(R):
                cs = slice(r * QB, (r + 1) * QB)
                a_s = acc_s[:, cs]
                a_w = acc_w[:, cs]
                g_c = sg_t[3 * r:3 * r + 1, :]
                g_s = sg_t[3 * r + 1:3 * r + 2, :] / a_s[NSA_DH:NSA_DH + 1, :]
                g_w = sg_t[3 * r + 2:3 * r + 3, :] / a_w[NSA_DH:NSA_DH + 1, :]
                outs.append((g_c * o_c[:, cs] + g_s * a_s + g_w * a_w)[0:NSA_DH])
            o_ref[0] = jnp.concatenate(outs, axis=0).T.astype(BF)
        return fn

    lax.switch(n_act - 4 * n_quad, [finish(rest) for rest in range(4)], m_q)


def _nsa_attend(proj3, small3, kc, vc, slope_tab, ovt):
    B, S, _ = proj3.shape
    G, R = NSA_GROUPS, NSA_REP
    n_cmp = kc.shape[2]
    kern = functools.partial(_nsa_kernel, seq=S)
    kv_spec = lambda off: pl.BlockSpec((1, S, LANES), lambda b, g, i: (b, 0, off // LANES))
    return pl.pallas_call(
        kern,
        grid=(B, G, S // QB),
        in_specs=[
            pl.BlockSpec((1, QB, R * NSA_DH), lambda b, g, i: (b, i, OFF_NQ // (R * NSA_DH) + g)),
            kv_spec(OFF_KS), kv_spec(OFF_VS), kv_spec(OFF_KW), kv_spec(OFF_VW),
            pl.BlockSpec((1, 1, n_cmp, LANES), lambda b, g, i: (b, g, 0, 0)),
            pl.BlockSpec((1, 1, n_cmp, LANES), lambda b, g, i: (b, g, 0, 0)),
            pl.BlockSpec((1, QB, LANES), lambda b, g, i: (b, i, 1 + g)),
            pl.BlockSpec((1, 8, LANES), lambda b, g, i: (g, 0, 0)),
            _resident(ovt.shape, lambda b, g, i: (0, 0)),
        ],
        out_specs=pl.BlockSpec((1, QB, R * NSA_DH), lambda b, g, i: (b, i, g)),
        out_shape=jax.ShapeDtypeStruct((B, S, G * R * NSA_DH), BF),
        scratch_shapes=[
            pltpu.VMEM((S, LANES), BF),
            pltpu.VMEM((S + WINDOW, LANES), BF),
            pltpu.VMEM((S // LANES, VROWS, LANES), BF),
            pltpu.VMEM(((S + WINDOW) // LANES, VROWS, LANES), BF),
            pltpu.VMEM((VROWS, n_cmp), BF),
            pltpu.VMEM((VROWS, R * QB), F32),
            pltpu.SMEM((S // SLC_TILE + 1,), jnp.int32),
        ],
        compiler_params=_params(("parallel", "parallel", "arbitrary")),
    )(proj3, proj3, proj3, proj3, proj3, kc, vc, small3, slope_tab, ovt)


def _merge_kernel(og_ref, on_ref, mg_ref, mn_ref, x_ref, wg_ref, wn_ref, wo_ref, g_ref, o_ref):
    a = jnp.dot(og_ref[...], wg_ref[...], preferred_element_type=F32)
    b = jnp.dot(on_ref[...], wn_ref[...], preferred_element_type=F32)
    mixed = jax.nn.sigmoid(mg_ref[...].astype(F32)) * a + jax.nn.sigmoid(mn_ref[...].astype(F32)) * b
    y = jnp.dot(mixed.astype(BF), wo_ref[...], preferred_element_type=F32)
    y = y * lax.rsqrt(jnp.mean(y * y, axis=-1, keepdims=True) + EPS) * g_ref[...]
    o_ref[...] = x_ref[...] + y


def _merge_out(o_gla2, o_nsa2, proj2, x2, wg, wn, wo, g, tm=512):
    n_tok = x2.shape[0]
    D = D_MODEL
    return pl.pallas_call(
        _merge_kernel,
        grid=(n_tok // tm,),
        in_specs=[
            pl.BlockSpec((tm, D), lambda i: (i, 0)),
            pl.BlockSpec((tm, o_nsa2.shape[1]), lambda i: (i, 0)),
            pl.BlockSpec((tm, D), lambda i: (i, OFF_MG // D)),
            pl.BlockSpec((tm, D), lambda i: (i, OFF_MN // D)),
            pl.BlockSpec((tm, D), lambda i: (i, 0)),
            _resident(wg.shape, lambda i: (0, 0)),
            _resident(wn.shape, lambda i: (0, 0)),
            _resident(wo.shape, lambda i: (0, 0)),
            _resident((1, D), lambda i: (0, 0)),
        ],
        out_specs=pl.BlockSpec((tm, D), lambda i: (i, 0)),
        out_shape=jax.ShapeDtypeStruct((n_tok, D), F32),
        compiler_params=_params(("parallel",)),
    )(o_gla2, o_nsa2, proj2, proj2, x2, wg, wn, wo, g)


def _ffn_kernel(x_ref, gpre_ref, wg_ref, wu_ref, wd_ref, gpost_ref, o_ref, acc_scr, *, chunk):
    x = x_ref[...]
    h = (x * lax.rsqrt(jnp.mean(x * x, axis=-1, keepdims=True) + EPS) * gpre_ref[...]).astype(BF)
    d_ff = wg_ref.shape[1]
    for n, c0 in enumerate(range(0, d_ff, chunk)):
        c1 = min(c0 + chunk, d_ff)
        a = jnp.dot(h, wg_ref[:, c0:c1], preferred_element_type=F32)
        u = jnp.dot(h, wu_ref[:, c0:c1], preferred_element_type=F32)
        t = (a * jax.nn.sigmoid(a) * u).astype(BF)
        part = jnp.dot(t, wd_ref[c0:c1, :], preferred_element_type=F32)
        if n == 0:
            acc_scr[...] = part
        else:
            acc_scr[...] += part
    f = acc_scr[...]
    o_ref[...] = x + f * lax.rsqrt(jnp.mean(f * f, axis=-1, keepdims=True) + EPS) * gpost_ref[...]


def _ffn(x2, gpre, wg, wu, wd, gpost, tm=512, chunk=512):
    n_tok = x2.shape[0]
    D = D_MODEL
    kern = functools.partial(_ffn_kernel, chunk=chunk)
    return pl.pallas_call(
        kern,
        grid=(n_tok // tm,),
        in_specs=[
            pl.BlockSpec((tm, D), lambda i: (i, 0)),
            _resident((1, D), lambda i: (0, 0)),
            _resident(wg.shape, lambda i: (0, 0)),
            _resident(wu.shape, lambda i: (0, 0)),
            _resident(wd.shape, lambda i: (0, 0)),
            _resident((1, D), lambda i: (0, 0)),
        ],
        out_specs=pl.BlockSpec((tm, D), lambda i: (i, 0)),
        out_shape=jax.ShapeDtypeStruct((n_tok, D), F32),
        scratch_shapes=[pltpu.VMEM((tm, D), F32)],
        compiler_params=_params(("parallel",)),
    )(x2, gpre, wg, wu, wd, gpost)


def _prep_compress(pe, w1, w2):
    eye = jnp.eye(NSA_GROUPS, dtype=F32)
    w1r = w1.reshape(CMP_LEN, NSA_DH, NSA_DH)
    w1e = jnp.einsum('lde,gh->lgdhe', w1r, eye).reshape(CMP_LEN * NSA_GROUPS * NSA_DH, NSA_GROUPS * NSA_DH)
    pe_e = jnp.broadcast_to(pe[:, None, :], (CMP_LEN, NSA_GROUPS, NSA_DH)).reshape(2, CMP_STRIDE * LANES)
    pe_e = jnp.pad(pe_e, ((0, 6), (0, 0)))
    w2e = jnp.stack([
        jnp.pad(jnp.pad(w2, ((g * NSA_DH, (NSA_GROUPS - 1 - g) * NSA_DH), (0, 0))), ((0, 0), (0, LANES - NSA_DH)))
        for g in range(NSA_GROUPS)])
    return pe_e.astype(F32), w1e.astype(BF), w2e.astype(BF)


def _overlap_table(seq):
    n_cmp = (seq - CMP_LEN) // CMP_STRIDE + 1
    n_slc = seq // SLC_LEN
    sc = CMP_STRIDE * np.arange(n_cmp)
    ss = SLC_LEN * np.arange(n_slc)
    ov = np.clip(np.minimum(sc[:, None] + CMP_LEN, ss[None, :] + SLC_LEN)
                 - np.maximum(sc[:, None], ss[None, :]), 0, None).astype(np.float32) / CMP_LEN
    ovt = np.zeros((NSA_DH, n_cmp + 1), np.float32)
    ovt[:n_slc, :n_cmp] = ov.T
    return jnp.asarray(ovt, dtype=BF)


def kernel(x, norm_mix_pre, norm_mix_post, norm_ffn_pre, norm_ffn_post, w_in, gla_w_alpha2, gla_b_alpha, gla_norm_g, nsa_cmp_pe_k, nsa_cmp_w1_k, nsa_cmp_w2_k, nsa_cmp_pe_v, nsa_cmp_w1_v, nsa_cmp_w2_v, w_proj_gla, w_proj_nsa, w_out, w_ffn_gate, w_ffn_up, w_ffn_down):
    B, S, D = x.shape
    depth = w_in.shape[0]
    n_tok = B * S
    h_idx = jnp.arange(NSA_HEADS, dtype=F32)
    slopes = jnp.exp2(-8.0 * (h_idx + 1.0) / NSA_HEADS).reshape(NSA_GROUPS, NSA_REP, 1)
    slope_tab = jnp.broadcast_to(jnp.pad(slopes, ((0, 0), (0, 8 - NSA_REP), (0, 0))), (NSA_GROUPS, 8, LANES))
    ovt = _overlap_table(S)
    x2 = x.reshape(n_tok, D)
    for l in range(depth):
        w_raw = w_in[l].astype(BF)
        w_tail = jnp.pad(w_raw[:, TAIL_START:], ((0, 0), (0, TAIL_START + LANES - IN_WIDTH)))
        proj2, small2 = _in_proj(x2, norm_mix_pre[l][None, :], w_raw, w_tail)
        proj3 = proj2.reshape(B, S, N_MAIN)
        small3 = small2.reshape(B, S, N_SMALL)

        w2_hi = gla_w_alpha2[l].astype(BF)
        w2_lo = (gla_w_alpha2[l] - w2_hi.astype(F32)).astype(BF)
        w2p = jnp.pad(jnp.concatenate([w2_hi, w2_hi, w2_lo], axis=0), ((0, LANES - 3 * GLA_RANK), (0, 0)))
        o_gla = _gla(proj3, small3, w2p, gla_b_alpha[l][None, :], gla_norm_g[l][None, :])

        xk = proj3[:, :, OFF_KC:OFF_KC + LANES].reshape(B, S // CMP_STRIDE, CMP_STRIDE * LANES)
        xv = proj3[:, :, OFF_VC:OFF_VC + LANES].reshape(B, S // CMP_STRIDE, CMP_STRIDE * LANES)
        pek, w1k, w2k = _prep_compress(nsa_cmp_pe_k[l], nsa_cmp_w1_k[l], nsa_cmp_w2_k[l])
        pev, w1v, w2v = _prep_compress(nsa_cmp_pe_v[l], nsa_cmp_w1_v[l], nsa_cmp_w2_v[l])
        kc, vc = _nsa_compress(xk, xv, pek, pev, w1k, w1v, w2k, w2v)
        o_nsa = _nsa_attend(proj3, small3, kc, vc, slope_tab, ovt)

        x2 = _merge_out(o_gla.reshape(n_tok, -1), o_nsa.reshape(n_tok, -1), proj2, x2,
                        w_proj_gla[l].astype(BF), w_proj_nsa[l].astype(BF),
                        w_out[l].astype(BF), norm_mix_post[l][None, :])
        x2 = _ffn(x2, norm_ffn_pre[l][None, :], w_ffn_gate[l].astype(BF), w_ffn_up[l].astype(BF),
                  w_ffn_down[l].astype(BF), norm_ffn_post[l][None, :])
    return x2.reshape(B, S, D)
```

```python
import functools

import numpy as np
import jax
import jax.numpy as jnp
from jax import lax
from jax.experimental import pallas as pl
from jax.experimental.pallas import tpu as pltpu

D_MODEL = 1024
GLA_HEADS = 4
GLA_DK = 128
GLA_DV = 256
GLA_RANK = 16
GLA_TAU = 16.0
GLA_CHUNK = 64
NSA_HEADS = 8
NSA_GROUPS = 2
NSA_REP = 4
NSA_DH = 64
CMP_LEN = 32
CMP_STRIDE = 16
SLC_LEN = 64
N_SEL = 16
WINDOW = 512
QB = 256
SLC_TILE = 256
VROWS = 80
D_FF = 2816
EPS = 1e-6
NEG = -1e30

LANES = 128
VMEM_LIMIT = 56 * 1024 * 1024
BF = jnp.bfloat16
F32 = jnp.float32

OFF_GQ = 0
OFF_GK = 512
OFF_GV = 1024
OFF_GR = 2048
OFF_MG = 3072
OFF_MN = 4096
OFF_NQ = 5120
OFF_KC = 5632
OFF_VC = 5760
OFF_KS = 5888
OFF_VS = 6016
OFF_KW = 6144
OFF_VW = 6272
N_MAIN = 6400
N_SMALL = 384

NT = (((1,), (1,)), ((), ()))
TN = (((0,), (0,)), ((), ()))


LOG2E = 1.4426950408889634
POS_TERMS = 3


def _pos_columns(lane, pos):
    first = lane - NSA_DH
    return jnp.where((first >= 0) & (first < POS_TERMS), pos >> 6,
                     jnp.where((first >= POS_TERMS) & (first < 2 * POS_TERMS), pos & 63, 0))


def _slope_rows(rowi, coef):
    terms = []
    rem = coef
    for _ in range(POS_TERMS):
        t = rem.astype(BF).astype(F32)
        terms.append(t)
        rem = rem - t
    out = jnp.zeros(rowi.shape, F32)
    for i, t in enumerate(terms):
        out = jnp.where(rowi == i, t * float(SLC_LEN), jnp.where(rowi == POS_TERMS + i, t, out))
    return out


def _resident(shape, index_map):
    return pl.BlockSpec(shape, index_map, pipeline_mode=pl.Buffered(1))


def _params(sem):
    return pltpu.CompilerParams(dimension_semantics=sem, vmem_limit_bytes=VMEM_LIMIT)


SRC_GA = 3072
SRC_NQ = 3088
SRC_KV = 3600
SRC_GATE = 4368
SRC_MERGE = 4392
IN_WIDTH = 6440
TAIL_START = IN_WIDTH // LANES * LANES
ALIGNED = OFF_MG
SHIFTED_RUNS = ((SRC_MERGE, 2048, OFF_MG, 1.0), (SRC_NQ, 512, OFF_NQ, NSA_DH ** -0.5), (SRC_KV, 768, OFF_KC, 1.0))


def _in_proj_kernel(x_ref, g_ref, wr_ref, wt_ref, om_ref, os_ref, wm_scr, ws_scr):
    @pl.when(pl.program_id(0) == 0)
    def _():
        r_i = lax.broadcasted_iota(jnp.int32, (2 * LANES, LANES), 0)
        c_i = lax.broadcasted_iota(jnp.int32, (2 * LANES, LANES), 1)
        for src, width, dst, scale in SHIFTED_RUNS:
            shift = jnp.where(r_i == c_i + src % LANES, 1.0, 0.0).astype(BF)
            for j in range(width // LANES):
                b0 = (src // LANES + j) * LANES
                if b0 + 2 * LANES <= TAIL_START:
                    pair = wr_ref[:, b0:b0 + 2 * LANES]
                else:
                    pair = jnp.concatenate([wr_ref[:, b0:b0 + LANES], wt_ref[...]], axis=1)
                moved = jnp.dot(pair, shift, preferred_element_type=F32)
                d0 = dst - ALIGNED + j * LANES
                wm_scr[:, d0:d0 + LANES] = (moved * scale).astype(BF)
        r1 = lax.broadcasted_iota(jnp.int32, (LANES, LANES), 0)
        c1 = lax.broadcasted_iota(jnp.int32, (LANES, LANES), 1)
        rep3 = jnp.where((r1 < GLA_RANK) & (c1 < 3 * GLA_RANK) & ((c1 % GLA_RANK) == r1), 1.0, 0.0).astype(BF)
        ws_scr[:, 0:LANES] = jnp.dot(wr_ref[:, SRC_GA:SRC_GA + LANES], rep3,
                                     preferred_element_type=F32).astype(BF)
        g_blk = SRC_GATE // LANES * LANES
        per_g = NSA_REP * 3
        for g in range(NSA_GROUPS):
            pick = jnp.where((c1 < per_g) & (r1 == c1 + SRC_GATE - g_blk + per_g * g), 1.0, 0.0).astype(BF)
            ws_scr[:, (1 + g) * LANES:(2 + g) * LANES] = jnp.dot(
                wr_ref[:, g_blk:g_blk + LANES], pick, preferred_element_type=F32).astype(BF)

    x = x_ref[...]
    h = (x * lax.rsqrt(jnp.mean(x * x, axis=-1, keepdims=True) + EPS) * g_ref[...]).astype(BF)
    os_ref[...] = jnp.dot(h, ws_scr[...], preferred_element_type=F32)
    step = 512
    for c0 in range(0, ALIGNED, step):
        om_ref[:, c0:c0 + step] = jnp.dot(h, wr_ref[:, c0:c0 + step], preferred_element_type=F32).astype(BF)
    for c0 in range(0, N_MAIN - ALIGNED, step):
        c1_ = min(c0 + step, N_MAIN - ALIGNED)
        om_ref[:, ALIGNED + c0:ALIGNED + c1_] = jnp.dot(h, wm_scr[:, c0:c1_],
                                                        preferred_element_type=F32).astype(BF)


def _in_proj(x2, g, w_raw, w_tail, tm=512):
    n_tok = x2.shape[0]
    return pl.pallas_call(
        _in_proj_kernel,
        grid=(n_tok // tm,),
        in_specs=[
            pl.BlockSpec((tm, D_MODEL), lambda i: (i, 0)),
            _resident((1, D_MODEL), lambda i: (0, 0)),
            _resident((D_MODEL, IN_WIDTH), lambda i: (0, 0)),
            _resident((D_MODEL, LANES), lambda i: (0, 0)),
        ],
        out_specs=[
            pl.BlockSpec((tm, N_MAIN), lambda i: (i, 0)),
            pl.BlockSpec((tm, N_SMALL), lambda i: (i, 0)),
        ],
        out_shape=[
            jax.ShapeDtypeStruct((n_tok, N_MAIN), BF),
            jax.ShapeDtypeStruct((n_tok, N_SMALL), F32),
        ],
        scratch_shapes=[pltpu.VMEM((D_MODEL, N_MAIN - ALIGNED), BF), pltpu.VMEM((D_MODEL, N_SMALL), BF)],
        compiler_params=_params(("arbitrary",)),
    )(x2, g, w_raw, w_tail)


def _gla_kernel(q_ref, k_ref, v_ref, r_ref, a_ref, w2_ref, b2_ref, ng_ref, o_ref, st_scr, *, n_chunks, n_heads):
    blk = pl.program_id(2)

    @pl.when(blk == 0)
    def _():
        st_scr[...] = jnp.zeros_like(st_scr)

    C = GLA_CHUNK
    T = n_chunks * C
    W = n_heads * GLA_DK

    def split3(x):
        hi = x.astype(BF)
        rem = x - hi.astype(F32)
        mid = rem.astype(BF)
        return hi, mid, (rem - mid.astype(F32)).astype(BF)

    a = a_ref[0]
    a_hi = a.astype(BF)
    a_lo = (a - a_hi.astype(F32)).astype(BF)
    lane = lax.broadcasted_iota(jnp.int32, (T, LANES), 1)
    in_lo = (lane >= GLA_RANK) & (lane < 2 * GLA_RANK)
    z = jnp.dot(jnp.where(in_lo, a_lo, a_hi), w2_ref[...], preferred_element_type=F32) + b2_ref[...]
    log_a = (jnp.minimum(z, 0.0) - jnp.log(1.0 + jnp.exp(-jnp.abs(z)))) * (1.0 / GLA_TAU)

    x_wide = jnp.concatenate([log_a[c * C:(c + 1) * C] for c in range(n_chunks)], axis=1)
    x3 = jnp.concatenate(split3(x_wide), axis=0)
    r3 = lax.broadcasted_iota(jnp.int32, (C, 3 * C), 0)
    c3 = lax.broadcasted_iota(jnp.int32, (C, 3 * C), 1) & (C - 1)
    tri3 = jnp.where(c3 <= r3, 1.0, 0.0).astype(BF)
    b_wide = jnp.dot(tri3, x3, preferred_element_type=F32)
    bcum = jnp.concatenate([b_wide[:, c * W:(c + 1) * W] for c in range(n_chunks)], axis=0)
    last_rows = [b_wide[C - 1:C, c * W:(c + 1) * W] for c in range(n_chunks)]
    b_last = jnp.concatenate([jnp.broadcast_to(lr, (C, W)) for lr in last_rows], axis=0)
    decay = [jnp.exp(lr) for lr in last_rows]

    q = q_ref[0].astype(F32)
    k = k_ref[0].astype(F32)
    v = v_ref[0]
    qe = (q * ((GLA_DK ** -0.5) * jnp.exp(bcum))).astype(BF)
    ke = (k * jnp.exp(-bcum)).astype(BF)
    kd = (k * jnp.exp(b_last - bcum)).astype(BF)

    H = min(T, 4 * C)
    row = lax.broadcasted_iota(jnp.int32, (H, H), 0)
    col = lax.broadcasted_iota(jnp.int32, (H, H), 1)
    keep = (col <= row) & ((col >> 6) == (row >> 6))
    ng = ng_ref[...]
    r_all = r_ref[0].astype(F32)
    for hh in range(n_heads):
        ks_ = slice(hh * GLA_DK, (hh + 1) * GLA_DK)
        vs_ = slice(hh * GLA_DV, (hh + 1) * GLA_DV)
        intra = []
        for h0 in range(0, T, H):
            hs = slice(h0, h0 + H)
            attn = lax.dot_general(qe[hs, ks_], ke[hs, ks_], NT, preferred_element_type=F32)
            intra.append(jnp.dot(jnp.where(keep, attn, 0.0).astype(BF), v[hs, vs_], preferred_element_type=F32))
        o = jnp.concatenate(intra, axis=0)

        st = st_scr[hh]
        inter = []
        for c in range(n_chunks):
            sl = slice(c * C, (c + 1) * C)
            inter.append(lax.dot_general(qe[sl, ks_], st.astype(BF), NT, preferred_element_type=F32))
            upd = lax.dot_general(v[sl, vs_], kd[sl, ks_], TN, preferred_element_type=F32)
            st = st * decay[c][:, ks_] + upd
        st_scr[hh] = st
        o = o + jnp.concatenate(inter, axis=0)

        o = o * lax.rsqrt(jnp.mean(o * o, axis=-1, keepdims=True) + EPS) * ng
        r = r_all[:, vs_]
        o_ref[0, :, vs_] = (o * (r * jax.nn.sigmoid(r))).astype(BF)


def _gla(proj3, small3, w2p, b2, ng, blk_tokens=512, heads_per_step=4):
    B, S, _ = proj3.shape
    nblk = S // blk_tokens
    hb = heads_per_step
    kern = functools.partial(_gla_kernel, n_chunks=blk_tokens // GLA_CHUNK, n_heads=hb)
    wk, wv = hb * GLA_DK, hb * GLA_DV
    return pl.pallas_call(
        kern,
        grid=(B, GLA_HEADS // hb, nblk),
        in_specs=[
            pl.BlockSpec((1, blk_tokens, wk), lambda b, h, i: (b, i, OFF_GQ // wk + h)),
            pl.BlockSpec((1, blk_tokens, wk), lambda b, h, i: (b, i, OFF_GK // wk + h)),
            pl.BlockSpec((1, blk_tokens, wv), lambda b, h, i: (b, i, OFF_GV // wv + h)),
            pl.BlockSpec((1, blk_tokens, wv), lambda b, h, i: (b, i, OFF_GR // wv + h)),
            pl.BlockSpec((1, blk_tokens, LANES), lambda b, h, i: (b, i, 0)),
            pl.BlockSpec((LANES, wk), lambda b, h, i: (0, h)),
            pl.BlockSpec((1, wk), lambda b, h, i: (0, h)),
            pl.BlockSpec((1, GLA_DV), lambda b, h, i: (0, 0)),
        ],
        out_specs=pl.BlockSpec((1, blk_tokens, wv), lambda b, h, i: (b, i, h)),
        out_shape=jax.ShapeDtypeStruct((B, S, GLA_HEADS * GLA_DV), BF),
        scratch_shapes=[pltpu.VMEM((hb, GLA_DV, GLA_DK), F32)],
        compiler_params=_params(("parallel", "parallel", "arbitrary")),
    )(proj3, proj3, proj3, proj3, small3, w2p, b2, ng)


def _compress_kernel(xk_ref, xv_ref, pek_ref, pev_ref, w1k_ref, w1v_ref, w2k_ref, w2v_ref, kc_ref, vc_ref):
    n_rows = xk_ref.shape[1]
    half = CMP_STRIDE * LANES
    lane = lax.broadcasted_iota(jnp.int32, (n_rows, LANES), 1)
    row = lax.broadcasted_iota(jnp.int32, (n_rows, LANES), 0)
    end_c = CMP_STRIDE * row + (CMP_LEN - 1)
    c_k = _pos_columns(lane, end_c).astype(F32)
    c_v = jnp.where(lane == NSA_DH, 1.0, 0.0).astype(F32)

    def branch(x_ref, pe_ref, w1_ref, w2_ref, const, o_ref):
        x = x_ref[0].astype(F32)
        xa = (x + pe_ref[0:1, :]).astype(BF)
        xb = (x + pe_ref[1:2, :]).astype(BF)
        a = jnp.dot(xa, w1_ref[0:half, :], preferred_element_type=F32)
        b = jnp.dot(xb, w1_ref[half:2 * half, :], preferred_element_type=F32)
        pre = a + pltpu.roll(b, n_rows - 1, 0)
        hid = (pre * jax.nn.sigmoid(pre)).astype(BF)
        for g in range(NSA_GROUPS):
            o_ref[0, g] = (jnp.dot(hid, w2_ref[g], preferred_element_type=F32) + const).astype(BF)

    branch(xk_ref, pek_ref, w1k_ref, w2k_ref, c_k, kc_ref)
    branch(xv_ref, pev_ref, w1v_ref, w2v_ref, c_v, vc_ref)


def _nsa_compress(xk, xv, pek, pev, w1k, w1v, w2k, w2v):
    B, n_rows, width = xk.shape
    full = lambda shape: _resident(shape, lambda b: (0,) * len(shape))
    out = jax.ShapeDtypeStruct((B, NSA_GROUPS, n_rows, LANES), BF)
    return pl.pallas_call(
        _compress_kernel,
        grid=(B,),
        in_specs=[
            pl.BlockSpec((1, n_rows, width), lambda b: (b, 0, 0)),
            pl.BlockSpec((1, n_rows, width), lambda b: (b, 0, 0)),
            full(pek.shape), full(pev.shape), full(w1k.shape), full(w1v.shape),
            full(w2k.shape), full(w2v.shape),
        ],
        out_specs=[pl.BlockSpec((1, NSA_GROUPS, n_rows, LANES), lambda b: (b, 0, 0, 0))] * 2,
        out_shape=[out, out],
        compiler_params=_params(("parallel",)),
    )(xk, xv, pek, pev, w1k, w1v, w2k, w2v)


def _nsa_kernel(q_ref, ks_ref, vs_ref, kw_ref, vw_ref, kc_ref, vc_ref, gt_ref, sl_ref, ov_ref,
                o_ref, ksa, kwa, vst, vwt, vct, acc_scr, pw_scr, act_ref, *, seq):
    qi = pl.program_id(2)
    n_slc = seq // SLC_LEN
    R = NSA_REP
    KT = SLC_TILE
    WK = WINDOW + QB
    n_cmp = kc_ref.shape[2]

    def t_bf(x):
        return x.astype(F32).T[0:VROWS].astype(BF)

    @pl.when(qi == 0)
    def _():
        grp = pl.program_id(1)
        lane = lax.broadcasted_iota(jnp.int32, (seq, LANES), 1)
        pos = lax.broadcasted_iota(jnp.int32, (seq, LANES), 0)
        blk = pos >> 6
        off = pos & 63
        r_i = lax.broadcasted_iota(jnp.int32, (LANES, LANES), 0)
        c_i = lax.broadcasted_iota(jnp.int32, (LANES, LANES), 1)
        pick = jnp.where((c_i < NSA_DH) & (r_i == c_i + grp * NSA_DH), 1.0, 0.0).astype(BF)
        c_s = jnp.where(lane == LANES - 1, off,
                        jnp.where((lane >= NSA_DH) & (lane - (NSA_DH - 1) == blk), 1, 0))
        ksa[...] = (jnp.dot(ks_ref[0], pick, preferred_element_type=F32) + c_s.astype(F32)).astype(BF)
        c_w = _pos_columns(lane, pos)
        kwa[0:WINDOW, :] = jnp.zeros((WINDOW, LANES), BF)
        kwa[WINDOW:WINDOW + seq, :] = (jnp.dot(kw_ref[0], pick, preferred_element_type=F32)
                                       + c_w.astype(F32)).astype(BF)
        ones_rows = jnp.where(lax.broadcasted_iota(jnp.int32, (VROWS - NSA_DH, LANES), 0) == 0, 1.0, 0.0)
        n_pad = WINDOW // LANES
        for c in range(n_pad):
            vwt[c] = jnp.zeros((VROWS, LANES), BF)

        def v_tile(x):
            xt = x.astype(F32).T
            dims = jnp.where(grp == 0, xt[0:NSA_DH], xt[NSA_DH:2 * NSA_DH])
            return jnp.concatenate([dims, ones_rows], axis=0).astype(BF)

        def fill(c, carry):
            rows = pl.ds(pl.multiple_of(c * LANES, LANES), LANES)
            vst[c] = v_tile(vs_ref[0, rows, :])
            vwt[c + n_pad] = v_tile(vw_ref[0, rows, :])
            return carry

        lax.fori_loop(0, seq // LANES, fill, 0)
        for c in range(n_cmp // LANES):
            vct[:, c * LANES:(c + 1) * LANES] = t_bf(vc_ref[0, 0, c * LANES:(c + 1) * LANES, :])

    q0 = qi * QB
    rowi = lax.broadcasted_iota(jnp.int32, (NSA_DH, QB), 0)
    slopes = [jnp.concatenate([sl_ref[0, r:r + 1, :]] * (QB // LANES), axis=1) for r in range(R)]
    q_all = q_ref[0].astype(F32).T
    q_t = [q_all[r * NSA_DH:(r + 1) * NSA_DH] for r in range(R)]

    qw = jnp.concatenate(
        [jnp.concatenate([q_t[r] * LOG2E, _slope_rows(rowi, slopes[r] * LOG2E)], axis=0).astype(BF)
         for r in range(R)], axis=1)

    def tile4(x):
        return jnp.concatenate([x] * R, axis=1)

    ovt = ov_ref[...]

    def cmp_branch(rows):
        def fn():
            kw = kwa[pl.ds(pl.multiple_of(q0, QB), WK), :]
            s_w = jnp.dot(kw, qw, preferred_element_type=F32)
            c_w = lax.broadcasted_iota(jnp.int32, (WK, QB), 0)
            d_w = c_w - lax.broadcasted_iota(jnp.int32, (WK, QB), 1)
            mask_w = (d_w > 0) & (d_w <= WINDOW) & (c_w >= WINDOW - q0)
            s_w = s_w + tile4(jnp.where(mask_w, 0.0, NEG))
            m_w = jnp.max(s_w, axis=0, keepdims=True)
            pw_scr[...] = jnp.exp2(s_w - m_w).astype(BF)

            s_c = jnp.dot(kc_ref[0, 0, 0:rows, :], qw, preferred_element_type=F32)
            e_c = CMP_STRIDE * lax.broadcasted_iota(jnp.int32, (rows, QB), 0) + (CMP_LEN - 1)
            t_c = q0 + lax.broadcasted_iota(jnp.int32, (rows, QB), 1)
            s_c = s_c + tile4(jnp.where(e_c <= t_c, 0.0, NEG))
            m_c = jnp.maximum(jnp.max(s_c, axis=0, keepdims=True), 0.1 * NEG)
            p_c = jnp.exp2(s_c - m_c)
            l_c = jnp.sum(p_c, axis=0, keepdims=True)
            p_c = p_c * jnp.where(l_c > 0.0, 1.0 / l_c, 0.0)
            o_cmp = jnp.dot(vct[:, 0:rows], p_c.astype(BF), preferred_element_type=F32)
            psum = p_c[:, 0:QB]
            for r in range(1, R):
                psum = psum + p_c[:, r * QB:(r + 1) * QB]
            p_hi = psum.astype(BF)
            rem = psum - p_hi.astype(F32)
            p_mid = rem.astype(BF)
            p_lo = (rem - p_mid.astype(F32)).astype(BF)
            ov = ovt[:, 0:rows]
            return o_cmp, (jnp.dot(ov, p_hi, preferred_element_type=F32)
                           + jnp.dot(ov, p_mid, preferred_element_type=F32)
                           + jnp.dot(ov, p_lo, preferred_element_type=F32))
        return fn

    half = n_cmp // 2
    o_c, imp = lax.cond(q0 + QB <= CMP_STRIDE * half + CMP_LEN - 1, cmp_branch(half), cmp_branch(n_cmp))

    NR = ov_ref.shape[0]
    SUB = 8
    jblk = lax.broadcasted_iota(jnp.int32, (NR, QB), 0)
    t_q = q0 + lax.broadcasted_iota(jnp.int32, (NR, QB), 1)
    cur = t_q >> 6
    forced = (jblk == 0) | (jblk == cur) | (jblk == cur - 1)
    score = jnp.where(jblk > cur, NEG, jnp.where(forced, -NEG, imp))
    n_slab = -(-n_slc // SUB)
    isub = lax.broadcasted_iota(jnp.int32, (SUB, QB), 0)

    def rank_counts(ns):
        def fn():
            slabs = [score[a * SUB:(a + 1) * SUB, :] for a in range(ns)]
            cnts = [jnp.zeros((SUB, QB), F32) for _ in range(ns)]
            for jp in range(min(ns * SUB, n_slc)):
                rowv = jnp.broadcast_to(score[jp:jp + 1, :], (SUB, QB))
                for a in range(ns):
                    if a < jp // SUB:
                        beats = jnp.where(rowv > slabs[a], 1.0, 0.0)
                    elif a > jp // SUB:
                        beats = jnp.where(rowv >= slabs[a], 1.0, 0.0)
                    else:
                        beats = jnp.where(isub > jp % SUB, jnp.where(rowv >= slabs[a], 1.0, 0.0),
                                          jnp.where(rowv > slabs[a], 1.0, 0.0))
                    cnts[a] = cnts[a] + beats
            rest = NR - ns * SUB
            return jnp.concatenate(cnts + [jnp.full((rest, QB), float(NR), F32)] * (rest > 0), axis=0)
        return fn

    last_blk = (q0 + QB - 1) >> 6
    cnt = lax.switch(last_blk // SUB, [rank_counts(ns) for ns in range(1, n_slab + 1)])
    sel = (cnt < float(N_SEL)) & (jblk <= cur) & (jblk < n_slc)
    a_nat = jnp.where(jblk == 0, 1.0, jnp.where(sel, (SLC_LEN * jblk).astype(F32), NEG))
    a_nat = jnp.where(jblk < n_slc, a_nat, 0.0)
    a_t = pltpu.roll(a_nat, NR - 1, 0)

    any_q = jnp.max(jnp.where(sel, 1.0, 0.0), axis=1, keepdims=True)
    jcol = lax.broadcasted_iota(jnp.int32, (NR, 1), 0)
    bits = jnp.where(any_q > 0.0, lax.shift_left(jnp.int32(1), jcol & 31), 0)
    word0 = jnp.sum(jnp.where(jcol < 32, bits, 0))
    word1 = jnp.sum(jnp.where(jcol >= 32, bits, 0))

    qs = jnp.concatenate([jnp.concatenate([q_t[r], a_t * slopes[r]], axis=0).astype(BF) for r in range(R)],
                         axis=1)

    tiles_per = KT // LANES
    blocks_per = KT // SLC_LEN
    n_full = q0 // KT

    def scan(ti, n):
        word = jnp.where(ti < 32 // blocks_per, word0, word1)
        hit = (lax.shift_right_logical(word, (ti * blocks_per) & 31) & ((1 << blocks_per) - 1)) != 0
        act_ref[n] = ti
        return n + jnp.where(hit, 1, 0)

    n_act = lax.fori_loop(0, n_full, scan, 0)

    def scores(ti):
        k0 = pl.multiple_of(ti * KT, KT)
        return jnp.dot(ksa[pl.ds(k0, KT), :], qs, preferred_element_type=F32)

    def values_t(ti):
        return [vst[ti * tiles_per + c] for c in range(tiles_per)]

    def absorb(s, v_tiles, m_prev):
        m_new = jnp.maximum(m_prev, jnp.max(s, axis=0, keepdims=True))
        alpha = jnp.exp(m_prev - m_new)
        p = jnp.exp(s - m_new).astype(BF)
        acc_scr[...] = acc_scr[...] * alpha + jnp.dot(jnp.concatenate(v_tiles, axis=1), p,
                                                     preferred_element_type=F32)
        return m_new

    acc_scr[...] = jnp.zeros(acc_scr.shape, F32)

    def chains(tiles, diag_flags, m_run):
        s_all = []
        for t, is_diag in zip(tiles, diag_flags):
            s = scores(t)
            if is_diag:
                p_d = t * KT + lax.broadcasted_iota(jnp.int32, (KT, QB), 0)
                t_d = q0 + lax.broadcasted_iota(jnp.int32, (KT, QB), 1)
                s = s + tile4(jnp.where(p_d <= t_d, 0.0, NEG))
            s_all.append(s)
        for t, s in zip(tiles, s_all):
            m_run = absorb(s, values_t(t), m_run)
        return m_run

    n_quad = n_act >> 2
    m_q = lax.fori_loop(0, n_quad, lambda j, m: chains([act_ref[4 * j + i] for i in range(4)], [False] * 4, m),
                        jnp.full((1, R * QB), NEG, F32))
    n_diag = QB // KT

    def finish(rest):
        def fn(m_run):
            tiles = [act_ref[4 * n_quad + i] for i in range(rest)] + [n_full + d for d in range(n_diag)]
            chains(tiles, [False] * rest + [True] * n_diag, m_run)
            acc_s = acc_scr[...]
            vw_t = jnp.concatenate([vwt[qi * (QB // LANES) + c] for c in range(WK // LANES)], axis=1)
            acc_w = jnp.dot(vw_t, pw_scr[...], preferred_element_type=F32)
            sg_t = jax.nn.sigmoid(gt_ref[0]).T
            outs = []
            for r in range(R):
                cs = slice(r * QB, (r + 1) * QB)
                a_s = acc_s[:, cs]
                a_w = acc_w[:, cs]
                g_c = sg_t[3 * r:3 * r + 1, :]
                g_s = sg_t[3 * r + 1:3 * r + 2, :] / a_s[NSA_DH:NSA_DH + 1, :]
                g_w = sg_t[3 * r + 2:3 * r + 3, :] / a_w[NSA_DH:NSA_DH + 1, :]
                outs.append((g_c * o_c[:, cs] + g_s * a_s + g_w * a_w)[0:NSA_DH])
            o_ref[0] = jnp.concatenate(outs, axis=0).T.astype(BF)
        return fn

    lax.switch(n_act - 4 * n_quad, [finish(rest) for rest in range(4)], m_q)


def _nsa_attend(proj3, small3, kc, vc, slope_tab, ovt):
    B, S, _ = proj3.shape
    G, R = NSA_GROUPS, NSA_REP
    n_cmp = kc.shape[2]
    kern = functools.partial(_nsa_kernel, seq=S)
    kv_spec = lambda off: pl.BlockSpec((1, S, LANES), lambda b, g, i: (b, 0, off // LANES))
    return pl.pallas_call(
        kern,
        grid=(B, G, S // QB),
        in_specs=[
            pl.BlockSpec((1, QB, R * NSA_DH), lambda b, g, i: (b, i, OFF_NQ // (R * NSA_DH) + g)),
            kv_spec(OFF_KS), kv_spec(OFF_VS), kv_spec(OFF_KW), kv_spec(OFF_VW),
            pl.BlockSpec((1, 1, n_cmp, LANES), lambda b, g, i: (b, g, 0, 0)),
            pl.BlockSpec((1, 1, n_cmp, LANES), lambda b, g, i: (b, g, 0, 0)),
            pl.BlockSpec((1, QB, LANES), lambda b, g, i: (b, i, 1 + g)),
            pl.BlockSpec((1, 8, LANES), lambda b, g, i: (g, 0, 0)),
            _resident(ovt.shape, lambda b, g, i: (0, 0)),
        ],
        out_specs=pl.BlockSpec((1, QB, R * NSA_DH), lambda b, g, i: (b, i, g)),
        out_shape=jax.ShapeDtypeStruct((B, S, G * R * NSA_DH), BF),
        scratch_shapes=[
            pltpu.VMEM((S, LANES), BF),
            pltpu.VMEM((S + WINDOW, LANES), BF),
            pltpu.VMEM((S // LANES, VROWS, LANES), BF),
            pltpu.VMEM(((S + WINDOW) // LANES, VROWS, LANES), BF),
            pltpu.VMEM((VROWS, n_cmp), BF),
            pltpu.VMEM((VROWS, R * QB), F32),
            pltpu.VMEM((WINDOW + QB, R * QB), BF),
            pltpu.SMEM((S // SLC_TILE + 1,), jnp.int32),
        ],
        compiler_params=_params(("parallel", "parallel", "arbitrary")),
    )(proj3, proj3, proj3, proj3, proj3, kc, vc, small3, slope_tab, ovt)


def _merge_kernel(og_ref, on_ref, mg_ref, mn_ref, x_ref, wg_ref, wn_ref, wo_ref, g_ref, o_ref):
    a = jnp.dot(og_ref[...], wg_ref[...], preferred_element_type=F32)
    b = jnp.dot(on_ref[...], wn_ref[...], preferred_element_type=F32)
    mixed = jax.nn.sigmoid(mg_ref[...].astype(F32)) * a + jax.nn.sigmoid(mn_ref[...].astype(F32)) * b
    y = jnp.dot(mixed.astype(BF), wo_ref[...], preferred_element_type=F32)
    y = y * lax.rsqrt(jnp.mean(y * y, axis=-1, keepdims=True) + EPS) * g_ref[...]
    o_ref[...] = x_ref[...] + y


def _merge_out(o_gla2, o_nsa2, proj2, x2, wg, wn, wo, g, tm=512):
    n_tok = x2.shape[0]
    D = D_MODEL
    return pl.pallas_call(
        _merge_kernel,
        grid=(n_tok // tm,),
        in_specs=[
            pl.BlockSpec((tm, D), lambda i: (i, 0)),
            pl.BlockSpec((tm, o_nsa2.shape[1]), lambda i: (i, 0)),
            pl.BlockSpec((tm, D), lambda i: (i, OFF_MG // D)),
            pl.BlockSpec((tm, D), lambda i: (i, OFF_MN // D)),
            pl.BlockSpec((tm, D), lambda i: (i, 0)),
            _resident(wg.shape, lambda i: (0, 0)),
            _resident(wn.shape, lambda i: (0, 0)),
            _resident(wo.shape, lambda i: (0, 0)),
            _resident((1, D), lambda i: (0, 0)),
        ],
        out_specs=pl.BlockSpec((tm, D), lambda i: (i, 0)),
        out_shape=jax.ShapeDtypeStruct((n_tok, D), F32),
        compiler_params=_params(("parallel",)),
    )(o_gla2, o_nsa2, proj2, proj2, x2, wg, wn, wo, g)


def _ffn_kernel(x_ref, gpre_ref, wg_ref, wu_ref, wd_ref, gpost_ref, o_ref, acc_scr, *, chunk):
    x = x_ref[...]
    h = (x * lax.rsqrt(jnp.mean(x * x, axis=-1, keepdims=True) + EPS) * gpre_ref[...]).astype(BF)
    d_ff = wg_ref.shape[1]
    for n, c0 in enumerate(range(0, d_ff, chunk)):
        c1 = min(c0 + chunk, d_ff)
        a = jnp.dot(h, wg_ref[:, c0:c1], preferred_element_type=F32)
        u = jnp.dot(h, wu_ref[:, c0:c1], preferred_element_type=F32)
        t = (a * jax.nn.sigmoid(a) * u).astype(BF)
        part = jnp.dot(t, wd_ref[c0:c1, :], preferred_element_type=F32)
        if n == 0:
            acc_scr[...] = part
        else:
            acc_scr[...] += part
    f = acc_scr[...]
    o_ref[...] = x + f * lax.rsqrt(jnp.mean(f * f, axis=-1, keepdims=True) + EPS) * gpost_ref[...]


def _ffn(x2, gpre, wg, wu, wd, gpost, tm=512, chunk=512):
    n_tok = x2.shape[0]
    D = D_MODEL
    kern = functools.partial(_ffn_kernel, chunk=chunk)
    return pl.pallas_call(
        kern,
        grid=(n_tok // tm,),
        in_specs=[
            pl.BlockSpec((tm, D), lambda i: (i, 0)),
            _resident((1, D), lambda i: (0, 0)),
            _resident(wg.shape, lambda i: (0, 0)),
            _resident(wu.shape, lambda i: (0, 0)),
            _resident(wd.shape, lambda i: (0, 0)),
            _resident((1, D), lambda i: (0, 0)),
        ],
        out_specs=pl.BlockSpec((tm, D), lambda i: (i, 0)),
        out_shape=jax.ShapeDtypeStruct((n_tok, D), F32),
        scratch_shapes=[pltpu.VMEM((tm, D), F32)],
        compiler_params=_params(("parallel",)),
    )(x2, gpre, wg, wu, wd, gpost)


def _prep_compress(pe, w1, w2):
    eye = jnp.eye(NSA_GROUPS, dtype=F32)
    w1r = w1.reshape(CMP_LEN, NSA_DH, NSA_DH)
    w1e = jnp.einsum('lde,gh->lgdhe', w1r, eye).reshape(CMP_LEN * NSA_GROUPS * NSA_DH, NSA_GROUPS * NSA_DH)
    pe_e = jnp.broadcast_to(pe[:, None, :], (CMP_LEN, NSA_GROUPS, NSA_DH)).reshape(2, CMP_STRIDE * LANES)
    pe_e = jnp.pad(pe_e, ((0, 6), (0, 0)))
    w2e = jnp.stack([
        jnp.pad(jnp.pad(w2, ((g * NSA_DH, (NSA_GROUPS - 1 - g) * NSA_DH), (0, 0))), ((0, 0), (0, LANES - NSA_DH)))
        for g in range(NSA_GROUPS)])
    return pe_e.astype(F32), w1e.astype(BF), w2e.astype(BF)


def _overlap_table(seq):
    n_cmp = (seq - CMP_LEN) // CMP_STRIDE + 1
    n_slc = seq // SLC_LEN
    sc = CMP_STRIDE * np.arange(n_cmp)
    ss = SLC_LEN * np.arange(n_slc)
    ov = np.clip(np.minimum(sc[:, None] + CMP_LEN, ss[None, :] + SLC_LEN)
                 - np.maximum(sc[:, None], ss[None, :]), 0, None).astype(np.float32) / CMP_LEN
    ovt = np.zeros((NSA_DH, n_cmp + 1), np.float32)
    ovt[:n_slc, :n_cmp] = ov.T
    return jnp.asarray(ovt, dtype=BF)


def kernel(x, norm_mix_pre, norm_mix_post, norm_ffn_pre, norm_ffn_post, w_in, gla_w_alpha2, gla_b_alpha, gla_norm_g, nsa_cmp_pe_k, nsa_cmp_w1_k, nsa_cmp_w2_k, nsa_cmp_pe_v, nsa_cmp_w1_v, nsa_cmp_w2_v, w_proj_gla, w_proj_nsa, w_out, w_ffn_gate, w_ffn_up, w_ffn_down):
    B, S, D = x.shape
    depth = w_in.shape[0]
    n_tok = B * S
    h_idx = jnp.arange(NSA_HEADS, dtype=F32)
    slopes = jnp.exp2(-8.0 * (h_idx + 1.0) / NSA_HEADS).reshape(NSA_GROUPS, NSA_REP, 1)
    slope_tab = jnp.broadcast_to(jnp.pad(slopes, ((0, 0), (0, 8 - NSA_REP), (0, 0))), (NSA_GROUPS, 8, LANES))
    ovt = _overlap_table(S)
    x2 = x.reshape(n_tok, D)
    for l in range(depth):
        w_raw = w_in[l].astype(BF)
        w_tail = jnp.pad(w_raw[:, TAIL_START:], ((0, 0), (0, TAIL_START + LANES - IN_WIDTH)))
        proj2, small2 = _in_proj(x2, norm_mix_pre[l][None, :], w_raw, w_tail)
        proj3 = proj2.reshape(B, S, N_MAIN)
        small3 = small2.reshape(B, S, N_SMALL)

        w2_hi = gla_w_alpha2[l].astype(BF)
        w2_lo = (gla_w_alpha2[l] - w2_hi.astype(F32)).astype(BF)
        w2p = jnp.pad(jnp.concatenate([w2_hi, w2_hi, w2_lo], axis=0), ((0, LANES - 3 * GLA_RANK), (0, 0)))
        o_gla = _gla(proj3, small3, w2p, gla_b_alpha[l][None, :], gla_norm_g[l][None, :])

        xk = proj3[:, :, OFF_KC:OFF_KC + LANES].reshape(B, S // CMP_STRIDE, CMP_STRIDE * LANES)
        xv = proj3[:, :, OFF_VC:OFF_VC + LANES].reshape(B, S // CMP_STRIDE, CMP_STRIDE * LANES)
        pek, w1k, w2k = _prep_compress(nsa_cmp_pe_k[l], nsa_cmp_w1_k[l], nsa_cmp_w2_k[l])
        pev, w1v, w2v = _prep_compress(nsa_cmp_pe_v[l], nsa_cmp_w1_v[l], nsa_cmp_w2_v[l])
        kc, vc = _nsa_compress(xk, xv, pek, pev, w1k, w1v, w2k, w2v)
        o_nsa = _nsa_attend(proj3, small3, kc, vc, slope_tab, ovt)

        x2 = _merge_out(o_gla.reshape(n_tok, -1), o_nsa.reshape(n_tok, -1), proj2, x2,
                        w_proj_gla[l].astype(BF), w_proj_nsa[l].astype(BF),
                        w_out[l].astype(BF), norm_mix_post[l][None, :])
        x2 = _ffn(x2, norm_ffn_pre[l][None, :], w_ffn_gate[l].astype(BF), w_ffn_up[l].astype(BF),
                  w_ffn_down[l].astype(BF), norm_ffn_post[l][None, :])
    return x2.reshape(B, S, D)
```

```python
import functools

import numpy as np
import jax
import jax.numpy as jnp
from jax import lax
from jax.experimental import pallas as pl
from jax.experimental.pallas import tpu as pltpu

D_MODEL = 1024
GLA_HEADS = 4
GLA_DK = 128
GLA_DV = 256
GLA_RANK = 16
GLA_TAU = 16.0
GLA_CHUNK = 64
NSA_HEADS = 8
NSA_GROUPS = 2
NSA_REP = 4
NSA_DH = 64
CMP_LEN = 32
CMP_STRIDE = 16
SLC_LEN = 64
N_SEL = 16
WINDOW = 512
QB = 256
SLC_TILE = 128
VROWS = 80
D_FF = 2816
EPS = 1e-6
NEG = -1e30

LANES = 128
VMEM_LIMIT = 56 * 1024 * 1024
BF = jnp.bfloat16
F32 = jnp.float32

OFF_GQ = 0
OFF_GK = 512
OFF_GV = 1024
OFF_GR = 2048
OFF_MG = 3072
OFF_MN = 4096
OFF_NQ = 5120
OFF_KC = 5632
OFF_VC = 5760
OFF_KS = 5888
OFF_VS = 6016
OFF_KW = 6144
OFF_VW = 6272
N_MAIN = 6400
N_SMALL = 384

NT = (((1,), (1,)), ((), ()))
TN = (((0,), (0,)), ((), ()))


LOG2E = 1.4426950408889634
POS_TERMS = 3


def _pos_columns(lane, pos):
    first = lane - NSA_DH
    return jnp.where((first >= 0) & (first < POS_TERMS), pos >> 6,
                     jnp.where((first >= POS_TERMS) & (first < 2 * POS_TERMS), pos & 63, 0))


def _slope_rows(rowi, coef):
    terms = []
    rem = coef
    for _ in range(POS_TERMS):
        t = rem.astype(BF).astype(F32)
        terms.append(t)
        rem = rem - t
    out = jnp.zeros(rowi.shape, F32)
    for i, t in enumerate(terms):
        out = jnp.where(rowi == i, t * float(SLC_LEN), jnp.where(rowi == POS_TERMS + i, t, out))
    return out


def _resident(shape, index_map):
    return pl.BlockSpec(shape, index_map, pipeline_mode=pl.Buffered(1))


def _params(sem):
    return pltpu.CompilerParams(dimension_semantics=sem, vmem_limit_bytes=VMEM_LIMIT)


SRC_GA = 3072
SRC_NQ = 3088
SRC_KV = 3600
SRC_GATE = 4368
SRC_MERGE = 4392
IN_WIDTH = 6440
TAIL_START = IN_WIDTH // LANES * LANES
ALIGNED = OFF_MG
SHIFTED_RUNS = ((SRC_MERGE, 2048, OFF_MG, 1.0), (SRC_NQ, 512, OFF_NQ, NSA_DH ** -0.5), (SRC_KV, 768, OFF_KC, 1.0))


def _in_proj_kernel(x_ref, g_ref, wr_ref, wt_ref, om_ref, os_ref, wm_scr, ws_scr):
    @pl.when(pl.program_id(0) == 0)
    def _():
        r_i = lax.broadcasted_iota(jnp.int32, (2 * LANES, LANES), 0)
        c_i = lax.broadcasted_iota(jnp.int32, (2 * LANES, LANES), 1)
        for src, width, dst, scale in SHIFTED_RUNS:
            shift = jnp.where(r_i == c_i + src % LANES, 1.0, 0.0).astype(BF)
            for j in range(width // LANES):
                b0 = (src // LANES + j) * LANES
                if b0 + 2 * LANES <= TAIL_START:
                    pair = wr_ref[:, b0:b0 + 2 * LANES]
                else:
                    pair = jnp.concatenate([wr_ref[:, b0:b0 + LANES], wt_ref[...]], axis=1)
                moved = jnp.dot(pair, shift, preferred_element_type=F32)
                d0 = dst - ALIGNED + j * LANES
                wm_scr[:, d0:d0 + LANES] = (moved * scale).astype(BF)
        r1 = lax.broadcasted_iota(jnp.int32, (LANES, LANES), 0)
        c1 = lax.broadcasted_iota(jnp.int32, (LANES, LANES), 1)
        rep3 = jnp.where((r1 < GLA_RANK) & (c1 < 3 * GLA_RANK) & ((c1 % GLA_RANK) == r1), 1.0, 0.0).astype(BF)
        ws_scr[:, 0:LANES] = jnp.dot(wr_ref[:, SRC_GA:SRC_GA + LANES], rep3,
                                     preferred_element_type=F32).astype(BF)
        g_blk = SRC_GATE // LANES * LANES
        per_g = NSA_REP * 3
        for g in range(NSA_GROUPS):
            pick = jnp.where((c1 < per_g) & (r1 == c1 + SRC_GATE - g_blk + per_g * g), 1.0, 0.0).astype(BF)
            ws_scr[:, (1 + g) * LANES:(2 + g) * LANES] = jnp.dot(
                wr_ref[:, g_blk:g_blk + LANES], pick, preferred_element_type=F32).astype(BF)

    x = x_ref[...]
    h = (x * lax.rsqrt(jnp.mean(x * x, axis=-1, keepdims=True) + EPS) * g_ref[...]).astype(BF)
    os_ref[...] = jnp.dot(h, ws_scr[...], preferred_element_type=F32)
    step = 512
    for c0 in range(0, ALIGNED, step):
        om_ref[:, c0:c0 + step] = jnp.dot(h, wr_ref[:, c0:c0 + step], preferred_element_type=F32).astype(BF)
    for c0 in range(0, N_MAIN - ALIGNED, step):
        c1_ = min(c0 + step, N_MAIN - ALIGNED)
        om_ref[:, ALIGNED + c0:ALIGNED + c1_] = jnp.dot(h, wm_scr[:, c0:c1_],
                                                        preferred_element_type=F32).astype(BF)


def _in_proj(x2, g, w_raw, w_tail, tm=512):
    n_tok = x2.shape[0]
    return pl.pallas_call(
        _in_proj_kernel,
        grid=(n_tok // tm,),
        in_specs=[
            pl.BlockSpec((tm, D_MODEL), lambda i: (i, 0)),
            _resident((1, D_MODEL), lambda i: (0, 0)),
            _resident((D_MODEL, IN_WIDTH), lambda i: (0, 0)),
            _resident((D_MODEL, LANES), lambda i: (0, 0)),
        ],
        out_specs=[
            pl.BlockSpec((tm, N_MAIN), lambda i: (i, 0)),
            pl.BlockSpec((tm, N_SMALL), lambda i: (i, 0)),
        ],
        out_shape=[
            jax.ShapeDtypeStruct((n_tok, N_MAIN), BF),
            jax.ShapeDtypeStruct((n_tok, N_SMALL), F32),
        ],
        scratch_shapes=[pltpu.VMEM((D_MODEL, N_MAIN - ALIGNED), BF), pltpu.VMEM((D_MODEL, N_SMALL), BF)],
        compiler_params=_params(("arbitrary",)),
    )(x2, g, w_raw, w_tail)


def _gla_kernel(q_ref, k_ref, v_ref, r_ref, a_ref, w2_ref, b2_ref, ng_ref, o_ref, st_scr, *, n_chunks, n_heads):
    blk = pl.program_id(2)

    @pl.when(blk == 0)
    def _():
        st_scr[...] = jnp.zeros_like(st_scr)

    C = GLA_CHUNK
    T = n_chunks * C
    W = n_heads * GLA_DK

    def split3(x):
        hi = x.astype(BF)
        rem = x - hi.astype(F32)
        mid = rem.astype(BF)
        return hi, mid, (rem - mid.astype(F32)).astype(BF)

    a = a_ref[0]
    a_hi = a.astype(BF)
    a_lo = (a - a_hi.astype(F32)).astype(BF)
    lane = lax.broadcasted_iota(jnp.int32, (T, LANES), 1)
    in_lo = (lane >= GLA_RANK) & (lane < 2 * GLA_RANK)
    z = jnp.dot(jnp.where(in_lo, a_lo, a_hi), w2_ref[...], preferred_element_type=F32) + b2_ref[...]
    log_a = (jnp.minimum(z, 0.0) - jnp.log(1.0 + jnp.exp(-jnp.abs(z)))) * (1.0 / GLA_TAU)

    x_wide = jnp.concatenate([log_a[c * C:(c + 1) * C] for c in range(n_chunks)], axis=1)
    x3 = jnp.concatenate(split3(x_wide), axis=0)
    r3 = lax.broadcasted_iota(jnp.int32, (C, 3 * C), 0)
    c3 = lax.broadcasted_iota(jnp.int32, (C, 3 * C), 1) & (C - 1)
    tri3 = jnp.where(c3 <= r3, 1.0, 0.0).astype(BF)
    b_wide = jnp.dot(tri3, x3, preferred_element_type=F32)
    bcum = jnp.concatenate([b_wide[:, c * W:(c + 1) * W] for c in range(n_chunks)], axis=0)
    last_rows = [b_wide[C - 1:C, c * W:(c + 1) * W] for c in range(n_chunks)]
    b_last = jnp.concatenate([jnp.broadcast_to(lr, (C, W)) for lr in last_rows], axis=0)
    decay = [jnp.exp(lr) for lr in last_rows]

    q = q_ref[0].astype(F32)
    k = k_ref[0].astype(F32)
    v = v_ref[0]
    qe = (q * ((GLA_DK ** -0.5) * jnp.exp(bcum))).astype(BF)
    ke = (k * jnp.exp(-bcum)).astype(BF)
    kd = (k * jnp.exp(b_last - bcum)).astype(BF)

    H = min(T, 4 * C)
    row = lax.broadcasted_iota(jnp.int32, (H, H), 0)
    col = lax.broadcasted_iota(jnp.int32, (H, H), 1)
    keep = (col <= row) & ((col >> 6) == (row >> 6))
    ng = ng_ref[...]
    r_all = r_ref[0].astype(F32)
    for hh in range(n_heads):
        ks_ = slice(hh * GLA_DK, (hh + 1) * GLA_DK)
        vs_ = slice(hh * GLA_DV, (hh + 1) * GLA_DV)
        intra = []
        for h0 in range(0, T, H):
            hs = slice(h0, h0 + H)
            attn = lax.dot_general(qe[hs, ks_], ke[hs, ks_], NT, preferred_element_type=F32)
            intra.append(jnp.dot(jnp.where(keep, attn, 0.0).astype(BF), v[hs, vs_], preferred_element_type=F32))
        o = jnp.concatenate(intra, axis=0)

        st = st_scr[hh]
        inter = []
        for c in range(n_chunks):
            sl = slice(c * C, (c + 1) * C)
            inter.append(lax.dot_general(qe[sl, ks_], st.astype(BF), NT, preferred_element_type=F32))
            upd = lax.dot_general(v[sl, vs_], kd[sl, ks_], TN, preferred_element_type=F32)
            st = st * decay[c][:, ks_] + upd
        st_scr[hh] = st
        o = o + jnp.concatenate(inter, axis=0)

        o = o * lax.rsqrt(jnp.mean(o * o, axis=-1, keepdims=True) + EPS) * ng
        r = r_all[:, vs_]
        o_ref[0, :, vs_] = (o * (r * jax.nn.sigmoid(r))).astype(BF)


def _gla(proj3, small3, w2p, b2, ng, blk_tokens=512, heads_per_step=4):
    B, S, _ = proj3.shape
    nblk = S // blk_tokens
    hb = heads_per_step
    kern = functools.partial(_gla_kernel, n_chunks=blk_tokens // GLA_CHUNK, n_heads=hb)
    wk, wv = hb * GLA_DK, hb * GLA_DV
    return pl.pallas_call(
        kern,
        grid=(B, GLA_HEADS // hb, nblk),
        in_specs=[
            pl.BlockSpec((1, blk_tokens, wk), lambda b, h, i: (b, i, OFF_GQ // wk + h)),
            pl.BlockSpec((1, blk_tokens, wk), lambda b, h, i: (b, i, OFF_GK // wk + h)),
            pl.BlockSpec((1, blk_tokens, wv), lambda b, h, i: (b, i, OFF_GV // wv + h)),
            pl.BlockSpec((1, blk_tokens, wv), lambda b, h, i: (b, i, OFF_GR // wv + h)),
            pl.BlockSpec((1, blk_tokens, LANES), lambda b, h, i: (b, i, 0)),
            pl.BlockSpec((LANES, wk), lambda b, h, i: (0, h)),
            pl.BlockSpec((1, wk), lambda b, h, i: (0, h)),
            pl.BlockSpec((1, GLA_DV), lambda b, h, i: (0, 0)),
        ],
        out_specs=pl.BlockSpec((1, blk_tokens, wv), lambda b, h, i: (b, i, h)),
        out_shape=jax.ShapeDtypeStruct((B, S, GLA_HEADS * GLA_DV), BF),
        scratch_shapes=[pltpu.VMEM((hb, GLA_DV, GLA_DK), F32)],
        compiler_params=_params(("parallel", "parallel", "arbitrary")),
    )(proj3, proj3, proj3, proj3, small3, w2p, b2, ng)


def _compress_kernel(xk_ref, xv_ref, pek_ref, pev_ref, w1k_ref, w1v_ref, w2k_ref, w2v_ref, kc_ref, vc_ref):
    n_rows = xk_ref.shape[1]
    half = CMP_STRIDE * LANES
    lane = lax.broadcasted_iota(jnp.int32, (n_rows, LANES), 1)
    row = lax.broadcasted_iota(jnp.int32, (n_rows, LANES), 0)
    end_c = CMP_STRIDE * row + (CMP_LEN - 1)
    c_k = _pos_columns(lane, end_c).astype(F32)
    c_v = jnp.where(lane == NSA_DH, 1.0, 0.0).astype(F32)

    def branch(x_ref, pe_ref, w1_ref, w2_ref, const, o_ref):
        x = x_ref[0].astype(F32)
        xa = (x + pe_ref[0:1, :]).astype(BF)
        xb = (x + pe_ref[1:2, :]).astype(BF)
        a = jnp.dot(xa, w1_ref[0:half, :], preferred_element_type=F32)
        b = jnp.dot(xb, w1_ref[half:2 * half, :], preferred_element_type=F32)
        pre = a + pltpu.roll(b, n_rows - 1, 0)
        hid = (pre * jax.nn.sigmoid(pre)).astype(BF)
        for g in range(NSA_GROUPS):
            o_ref[0, g] = (jnp.dot(hid, w2_ref[g], preferred_element_type=F32) + const).astype(BF)

    branch(xk_ref, pek_ref, w1k_ref, w2k_ref, c_k, kc_ref)
    branch(xv_ref, pev_ref, w1v_ref, w2v_ref, c_v, vc_ref)


def _nsa_compress(xk, xv, pek, pev, w1k, w1v, w2k, w2v):
    B, n_rows, width = xk.shape
    full = lambda shape: _resident(shape, lambda b: (0,) * len(shape))
    out = jax.ShapeDtypeStruct((B, NSA_GROUPS, n_rows, LANES), BF)
    return pl.pallas_call(
        _compress_kernel,
        grid=(B,),
        in_specs=[
            pl.BlockSpec((1, n_rows, width), lambda b: (b, 0, 0)),
            pl.BlockSpec((1, n_rows, width), lambda b: (b, 0, 0)),
            full(pek.shape), full(pev.shape), full(w1k.shape), full(w1v.shape),
            full(w2k.shape), full(w2v.shape),
        ],
        out_specs=[pl.BlockSpec((1, NSA_GROUPS, n_rows, LANES), lambda b: (b, 0, 0, 0))] * 2,
        out_shape=[out, out],
        compiler_params=_params(("parallel",)),
    )(xk, xv, pek, pev, w1k, w1v, w2k, w2v)


def _nsa_kernel(q_ref, ks_ref, vs_ref, kw_ref, vw_ref, kc_ref, vc_ref, gt_ref, sl_ref, ov_ref,
                o_ref, ksa, kwa, vst, vwt, vct, acc_scr, pw_scr, act_ref, *, seq):
    qi = pl.program_id(2)
    n_slc = seq // SLC_LEN
    R = NSA_REP
    KT = SLC_TILE
    WK = WINDOW + QB
    n_cmp = kc_ref.shape[2]

    def t_bf(x):
        return x.astype(F32).T[0:VROWS].astype(BF)

    @pl.when(qi == 0)
    def _():
        grp = pl.program_id(1)
        lane = lax.broadcasted_iota(jnp.int32, (seq, LANES), 1)
        pos = lax.broadcasted_iota(jnp.int32, (seq, LANES), 0)
        blk = pos >> 6
        off = pos & 63
        r_i = lax.broadcasted_iota(jnp.int32, (LANES, LANES), 0)
        c_i = lax.broadcasted_iota(jnp.int32, (LANES, LANES), 1)
        pick = jnp.where((c_i < NSA_DH) & (r_i == c_i + grp * NSA_DH), 1.0, 0.0).astype(BF)
        c_s = jnp.where(lane == LANES - 1, off,
                        jnp.where((lane >= NSA_DH) & (lane - (NSA_DH - 1) == blk), 1, 0))
        ksa[...] = (jnp.dot(ks_ref[0], pick, preferred_element_type=F32) + c_s.astype(F32)).astype(BF)
        c_w = _pos_columns(lane, pos)
        kwa[0:WINDOW, :] = jnp.zeros((WINDOW, LANES), BF)
        kwa[WINDOW:WINDOW + seq, :] = (jnp.dot(kw_ref[0], pick, preferred_element_type=F32)
                                       + c_w.astype(F32)).astype(BF)
        ones_rows = jnp.where(lax.broadcasted_iota(jnp.int32, (VROWS - NSA_DH, LANES), 0) == 0, 1.0, 0.0)
        n_pad = WINDOW // LANES
        for c in range(n_pad):
            vwt[c] = jnp.zeros((VROWS, LANES), BF)

        def v_tile(x):
            xt = x.astype(F32).T
            dims = jnp.where(grp == 0, xt[0:NSA_DH], xt[NSA_DH:2 * NSA_DH])
            return jnp.concatenate([dims, ones_rows], axis=0).astype(BF)

        def fill(c, carry):
            rows = pl.ds(pl.multiple_of(c * LANES, LANES), LANES)
            vst[c] = v_tile(vs_ref[0, rows, :])
            vwt[c + n_pad] = v_tile(vw_ref[0, rows, :])
            return carry

        lax.fori_loop(0, seq // LANES, fill, 0)
        for c in range(n_cmp // LANES):
            vct[:, c * LANES:(c + 1) * LANES] = t_bf(vc_ref[0, 0, c * LANES:(c + 1) * LANES, :])

    q0 = qi * QB
    rowi = lax.broadcasted_iota(jnp.int32, (NSA_DH, QB), 0)
    slopes = [jnp.concatenate([sl_ref[0, r:r + 1, :]] * (QB // LANES), axis=1) for r in range(R)]
    q_all = q_ref[0].astype(F32).T
    q_t = [q_all[r * NSA_DH:(r + 1) * NSA_DH] for r in range(R)]

    qw = jnp.concatenate(
        [jnp.concatenate([q_t[r] * LOG2E, _slope_rows(rowi, slopes[r] * LOG2E)], axis=0).astype(BF)
         for r in range(R)], axis=1)

    def tile4(x):
        return jnp.concatenate([x] * R, axis=1)

    ovt = ov_ref[...]

    def cmp_branch(rows):
        def fn():
            kw = kwa[pl.ds(pl.multiple_of(q0, QB), WK), :]
            s_w = jnp.dot(kw, qw, preferred_element_type=F32)
            c_w = lax.broadcasted_iota(jnp.int32, (WK, QB), 0)
            d_w = c_w - lax.broadcasted_iota(jnp.int32, (WK, QB), 1)
            mask_w = (d_w > 0) & (d_w <= WINDOW) & (c_w >= WINDOW - q0)
            s_w = s_w + tile4(jnp.where(mask_w, 0.0, NEG))
            m_w = jnp.max(s_w, axis=0, keepdims=True)
            pw_scr[...] = jnp.exp2(s_w - m_w).astype(BF)

            s_c = jnp.dot(kc_ref[0, 0, 0:rows, :], qw, preferred_element_type=F32)
            e_c = CMP_STRIDE * lax.broadcasted_iota(jnp.int32, (rows, QB), 0) + (CMP_LEN - 1)
            t_c = q0 + lax.broadcasted_iota(jnp.int32, (rows, QB), 1)
            s_c = s_c + tile4(jnp.where(e_c <= t_c, 0.0, NEG))
            m_c = jnp.maximum(jnp.max(s_c, axis=0, keepdims=True), 0.1 * NEG)
            p_c = jnp.exp2(s_c - m_c)
            l_c = jnp.sum(p_c, axis=0, keepdims=True)
            p_c = p_c * jnp.where(l_c > 0.0, 1.0 / l_c, 0.0)
            o_cmp = jnp.dot(vct[:, 0:rows], p_c.astype(BF), preferred_element_type=F32)
            psum = p_c[:, 0:QB]
            for r in range(1, R):
                psum = psum + p_c[:, r * QB:(r + 1) * QB]
            p_hi = psum.astype(BF)
            rem = psum - p_hi.astype(F32)
            p_mid = rem.astype(BF)
            p_lo = (rem - p_mid.astype(F32)).astype(BF)
            ov = ovt[:, 0:rows]
            return o_cmp, (jnp.dot(ov, p_hi, preferred_element_type=F32)
                           + jnp.dot(ov, p_mid, preferred_element_type=F32)
                           + jnp.dot(ov, p_lo, preferred_element_type=F32))
        return fn

    half = n_cmp // 2
    o_c, imp = lax.cond(q0 + QB <= CMP_STRIDE * half + CMP_LEN - 1, cmp_branch(half), cmp_branch(n_cmp))

    NR = ov_ref.shape[0]
    SUB = 8
    jblk = lax.broadcasted_iota(jnp.int32, (NR, QB), 0)
    t_q = q0 + lax.broadcasted_iota(jnp.int32, (NR, QB), 1)
    cur = t_q >> 6
    forced = (jblk == 0) | (jblk == cur) | (jblk == cur - 1)
    score = jnp.where(jblk > cur, NEG, jnp.where(forced, -NEG, imp))
    n_slab = -(-n_slc // SUB)
    isub = lax.broadcasted_iota(jnp.int32, (SUB, QB), 0)

    def rank_counts(ns):
        def fn():
            slabs = [score[a * SUB:(a + 1) * SUB, :] for a in range(ns)]
            cnts = [jnp.zeros((SUB, QB), F32) for _ in range(ns)]
            for jp in range(min(ns * SUB, n_slc)):
                rowv = jnp.broadcast_to(score[jp:jp + 1, :], (SUB, QB))
                for a in range(ns):
                    if a < jp // SUB:
                        beats = jnp.where(rowv > slabs[a], 1.0, 0.0)
                    elif a > jp // SUB:
                        beats = jnp.where(rowv >= slabs[a], 1.0, 0.0)
                    else:
                        beats = jnp.where(isub > jp % SUB, jnp.where(rowv >= slabs[a], 1.0, 0.0),
                                          jnp.where(rowv > slabs[a], 1.0, 0.0))
                    cnts[a] = cnts[a] + beats
            rest = NR - ns * SUB
            return jnp.concatenate(cnts + [jnp.full((rest, QB), float(NR), F32)] * (rest > 0), axis=0)
        return fn

    last_blk = (q0 + QB - 1) >> 6
    cnt = lax.switch(last_blk // SUB, [rank_counts(ns) for ns in range(1, n_slab + 1)])
    sel = (cnt < float(N_SEL)) & (jblk <= cur) & (jblk < n_slc)
    a_nat = jnp.where(jblk == 0, 1.0, jnp.where(sel, (SLC_LEN * jblk).astype(F32), NEG))
    a_nat = jnp.where(jblk < n_slc, a_nat, 0.0)
    a_t = pltpu.roll(a_nat, NR - 1, 0)

    any_q = jnp.max(jnp.where(sel, 1.0, 0.0), axis=1, keepdims=True)
    jcol = lax.broadcasted_iota(jnp.int32, (NR, 1), 0)
    bits = jnp.where(any_q > 0.0, lax.shift_left(jnp.int32(1), jcol & 31), 0)
    word0 = jnp.sum(jnp.where(jcol < 32, bits, 0))
    word1 = jnp.sum(jnp.where(jcol >= 32, bits, 0))

    qs = jnp.concatenate([jnp.concatenate([q_t[r], a_t * slopes[r]], axis=0).astype(BF) for r in range(R)],
                         axis=1)

    tiles_per = KT // LANES
    blocks_per = KT // SLC_LEN
    n_full = q0 // KT

    def scan(ti, n):
        word = jnp.where(ti < 32 // blocks_per, word0, word1)
        hit = (lax.shift_right_logical(word, (ti * blocks_per) & 31) & ((1 << blocks_per) - 1)) != 0
        act_ref[n] = ti
        return n + jnp.where(hit, 1, 0)

    n_act = lax.fori_loop(0, n_full, scan, 0)

    def scores(ti):
        k0 = pl.multiple_of(ti * KT, KT)
        return jnp.dot(ksa[pl.ds(k0, KT), :], qs, preferred_element_type=F32)

    def values_t(ti):
        return [vst[ti * tiles_per + c] for c in range(tiles_per)]

    def absorb(s, v_tiles, m_prev):
        m_new = jnp.maximum(m_prev, jnp.max(s, axis=0, keepdims=True))
        alpha = jnp.exp(m_prev - m_new)
        p = jnp.exp(s - m_new).astype(BF)
        acc_scr[...] = acc_scr[...] * alpha + jnp.dot(jnp.concatenate(v_tiles, axis=1), p,
                                                     preferred_element_type=F32)
        return m_new

    acc_scr[...] = jnp.zeros(acc_scr.shape, F32)

    def chains(tiles, diag_flags, m_run):
        s_all = []
        for t, is_diag in zip(tiles, diag_flags):
            s = scores(t)
            if is_diag:
                p_d = t * KT + lax.broadcasted_iota(jnp.int32, (KT, QB), 0)
                t_d = q0 + lax.broadcasted_iota(jnp.int32, (KT, QB), 1)
                s = s + tile4(jnp.where(p_d <= t_d, 0.0, NEG))
            s_all.append(s)
        for t, s in zip(tiles, s_all):
            m_run = absorb(s, values_t(t), m_run)
        return m_run

    n_quad = n_act >> 2
    m_q = lax.fori_loop(0, n_quad, lambda j, m: chains([act_ref[4 * j + i] for i in range(4)], [False] * 4, m),
                        jnp.full((1, R * QB), NEG, F32))
    n_diag = QB // KT

    def finish(rest):
        def fn(m_run):
            tiles = [act_ref[4 * n_quad + i] for i in range(rest)] + [n_full + d for d in range(n_diag)]
            chains(tiles, [False] * rest + [True] * n_diag, m_run)
            acc_s = acc_scr[...]
            vw_t = jnp.concatenate([vwt[qi * (QB // LANES) + c] for c in range(WK // LANES)], axis=1)
            acc_w = jnp.dot(vw_t, pw_scr[...], preferred_element_type=F32)
            sg_t = jax.nn.sigmoid(gt_ref[0]).T
            outs = []
            for r in range(R):
                cs = slice(r * QB, (r + 1) * QB)
                a_s = acc_s[:, cs]
                a_w = acc_w[:, cs]
                g_c = sg_t[3 * r:3 * r + 1, :]
                g_s = sg_t[3 * r + 1:3 * r + 2, :] / a_s[NSA_DH:NSA_DH + 1, :]
                g_w = sg_t[3 * r + 2:3 * r + 3, :] / a_w[NSA_DH:NSA_DH + 1, :]
                outs.append((g_c * o_c[:, cs] + g_s * a_s + g_w * a_w)[0:NSA_DH])
            o_ref[0] = jnp.concatenate(outs, axis=0).T.astype(BF)
        return fn

    lax.switch(n_act - 4 * n_quad, [finish(rest) for rest in range(4)], m_q)


def _nsa_attend(proj3, small3, kc, vc, slope_tab, ovt):
    B, S, _ = proj3.shape
    G, R = NSA_GROUPS, NSA_REP
    n_cmp = kc.shape[2]
    kern = functools.partial(_nsa_kernel, seq=S)
    kv_spec = lambda off: pl.BlockSpec((1, S, LANES), lambda b, g, i: (b, 0, off // LANES))
    return pl.pallas_call(
        kern,
        grid=(B, G, S // QB),
        in_specs=[
            pl.BlockSpec((1, QB, R * NSA_DH), lambda b, g, i: (b, i, OFF_NQ // (R * NSA_DH) + g)),
            kv_spec(OFF_KS), kv_spec(OFF_VS), kv_spec(OFF_KW), kv_spec(OFF_VW),
            pl.BlockSpec((1, 1, n_cmp, LANES), lambda b, g, i: (b, g, 0, 0)),
            pl.BlockSpec((1, 1, n_cmp, LANES), lambda b, g, i: (b, g, 0, 0)),
            pl.BlockSpec((1, QB, LANES), lambda b, g, i: (b, i, 1 + g)),
            pl.BlockSpec((1, 8, LANES), lambda b, g, i: (g, 0, 0)),
            _resident(ovt.shape, lambda b, g, i: (0, 0)),
        ],
        out_specs=pl.BlockSpec((1, QB, R * NSA_DH), lambda b, g, i: (b, i, g)),
        out_shape=jax.ShapeDtypeStruct((B, S, G * R * NSA_DH), BF),
        scratch_shapes=[
            pltpu.VMEM((S, LANES), BF),
            pltpu.VMEM((S + WINDOW, LANES), BF),
            pltpu.VMEM((S // LANES, VROWS, LANES), BF),
            pltpu.VMEM(((S + WINDOW) // LANES, VROWS, LANES), BF),
            pltpu.VMEM((VROWS, n_cmp), BF),
            pltpu.VMEM((VROWS, R * QB), F32),
            pltpu.VMEM((WINDOW + QB, R * QB), BF),
            pltpu.SMEM((S // SLC_TILE + 1,), jnp.int32),
        ],
        compiler_params=_params(("parallel", "parallel", "arbitrary")),
    )(proj3, proj3, proj3, proj3, proj3, kc, vc, small3, slope_tab, ovt)


def _merge_kernel(og_ref, on_ref, mg_ref, mn_ref, x_ref, wg_ref, wn_ref, wo_ref, g_ref, o_ref):
    a = jnp.dot(og_ref[...], wg_ref[...], preferred_element_type=F32)
    b = jnp.dot(on_ref[...], wn_ref[...], preferred_element_type=F32)
    mixed = jax.nn.sigmoid(mg_ref[...].astype(F32)) * a + jax.nn.sigmoid(mn_ref[...].astype(F32)) * b
    y = jnp.dot(mixed.astype(BF), wo_ref[...], preferred_element_type=F32)
    y = y * lax.rsqrt(jnp.mean(y * y, axis=-1, keepdims=True) + EPS) * g_ref[...]
    o_ref[...] = x_ref[...] + y


def _merge_out(o_gla2, o_nsa2, proj2, x2, wg, wn, wo, g, tm=512):
    n_tok = x2.shape[0]
    D = D_MODEL
    return pl.pallas_call(
        _merge_kernel,
        grid=(n_tok // tm,),
        in_specs=[
            pl.BlockSpec((tm, D), lambda i: (i, 0)),
            pl.BlockSpec((tm, o_nsa2.shape[1]), lambda i: (i, 0)),
            pl.BlockSpec((tm, D), lambda i: (i, OFF_MG // D)),
            pl.BlockSpec((tm, D), lambda i: (i, OFF_MN // D)),
            pl.BlockSpec((tm, D), lambda i: (i, 0)),
            _resident(wg.shape, lambda i: (0, 0)),
            _resident(wn.shape, lambda i: (0, 0)),
            _resident(wo.shape, lambda i: (0, 0)),
            _resident((1, D), lambda i: (0, 0)),
        ],
        out_specs=pl.BlockSpec((tm, D), lambda i: (i, 0)),
        out_shape=jax.ShapeDtypeStruct((n_tok, D), F32),
        compiler_params=_params(("parallel",)),
    )(o_gla2, o_nsa2, proj2, proj2, x2, wg, wn, wo, g)


def _ffn_kernel(x_ref, gpre_ref, wg_ref, wu_ref, wd_ref, gpost_ref, o_ref, acc_scr, *, chunk):
    x = x_ref[...]
    h = (x * lax.rsqrt(jnp.mean(x * x, axis=-1, keepdims=True) + EPS) * gpre_ref[...]).astype(BF)
    d_ff = wg_ref.shape[1]
    for n, c0 in enumerate(range(0, d_ff, chunk)):
        c1 = min(c0 + chunk, d_ff)
        a = jnp.dot(h, wg_ref[:, c0:c1], preferred_element_type=F32)
        u = jnp.dot(h, wu_ref[:, c0:c1], preferred_element_type=F32)
        t = (a * jax.nn.sigmoid(a) * u).astype(BF)
        part = jnp.dot(t, wd_ref[c0:c1, :], preferred_element_type=F32)
        if n == 0:
            acc_scr[...] = part
        else:
            acc_scr[...] += part
    f = acc_scr[...]
    o_ref[...] = x + f * lax.rsqrt(jnp.mean(f * f, axis=-1, keepdims=True) + EPS) * gpost_ref[...]


def _ffn(x2, gpre, wg, wu, wd, gpost, tm=512, chunk=512):
    n_tok = x2.shape[0]
    D = D_MODEL
    kern = functools.partial(_ffn_kernel, chunk=chunk)
    return pl.pallas_call(
        kern,
        grid=(n_tok // tm,),
        in_specs=[
            pl.BlockSpec((tm, D), lambda i: (i, 0)),
            _resident((1, D), lambda i: (0, 0)),
            _resident(wg.shape, lambda i: (0, 0)),
            _resident(wu.shape, lambda i: (0, 0)),
            _resident(wd.shape, lambda i: (0, 0)),
            _resident((1, D), lambda i: (0, 0)),
        ],
        out_specs=pl.BlockSpec((tm, D), lambda i: (i, 0)),
        out_shape=jax.ShapeDtypeStruct((n_tok, D), F32),
        scratch_shapes=[pltpu.VMEM((tm, D), F32)],
        compiler_params=_params(("parallel",)),
    )(x2, gpre, wg, wu, wd, gpost)


def _prep_compress(pe, w1, w2):
    eye = jnp.eye(NSA_GROUPS, dtype=F32)
    w1r = w1.reshape(CMP_LEN, NSA_DH, NSA_DH)
    w1e = jnp.einsum('lde,gh->lgdhe', w1r, eye).reshape(CMP_LEN * NSA_GROUPS * NSA_DH, NSA_GROUPS * NSA_DH)
    pe_e = jnp.broadcast_to(pe[:, None, :], (CMP_LEN, NSA_GROUPS, NSA_DH)).reshape(2, CMP_STRIDE * LANES)
    pe_e = jnp.pad(pe_e, ((0, 6), (0, 0)))
    w2e = jnp.stack([
        jnp.pad(jnp.pad(w2, ((g * NSA_DH, (NSA_GROUPS - 1 - g) * NSA_DH), (0, 0))), ((0, 0), (0, LANES - NSA_DH)))
        for g in range(NSA_GROUPS)])
    return pe_e.astype(F32), w1e.astype(BF), w2e.astype(BF)


def _overlap_table(seq):
    n_cmp = (seq - CMP_LEN) // CMP_STRIDE + 1
    n_slc = seq // SLC_LEN
    sc = CMP_STRIDE * np.arange(n_cmp)
    ss = SLC_LEN * np.arange(n_slc)
    ov = np.clip(np.minimum(sc[:, None] + CMP_LEN, ss[None, :] + SLC_LEN)
                 - np.maximum(sc[:, None], ss[None, :]), 0, None).astype(np.float32) / CMP_LEN
    ovt = np.zeros((NSA_DH, n_cmp + 1), np.float32)
    ovt[:n_slc, :n_cmp] = ov.T
    return jnp.asarray(ovt, dtype=BF)


def kernel(x, norm_mix_pre, norm_mix_post, norm_ffn_pre, norm_ffn_post, w_in, gla_w_alpha2, gla_b_alpha, gla_norm_g, nsa_cmp_pe_k, nsa_cmp_w1_k, nsa_cmp_w2_k, nsa_cmp_pe_v, nsa_cmp_w1_v, nsa_cmp_w2_v, w_proj_gla, w_proj_nsa, w_out, w_ffn_gate, w_ffn_up, w_ffn_down):
    B, S, D = x.shape
    depth = w_in.shape[0]
    n_tok = B * S
    h_idx = jnp.arange(NSA_HEADS, dtype=F32)
    slopes = jnp.exp2(-8.0 * (h_idx + 1.0) / NSA_HEADS).reshape(NSA_GROUPS, NSA_REP, 1)
    slope_tab = jnp.broadcast_to(jnp.pad(slopes, ((0, 0), (0, 8 - NSA_REP), (0, 0))), (NSA_GROUPS, 8, LANES))
    ovt = _overlap_table(S)
    x2 = x.reshape(n_tok, D)
    for l in range(depth):
        w_raw = w_in[l].astype(BF)
        w_tail = jnp.pad(w_raw[:, TAIL_START:], ((0, 0), (0, TAIL_START + LANES - IN_WIDTH)))
        proj2, small2 = _in_proj(x2, norm_mix_pre[l][None, :], w_raw, w_tail)
        proj3 = proj2.reshape(B, S, N_MAIN)
        small3 = small2.reshape(B, S, N_SMALL)

        w2_hi = gla_w_alpha2[l].astype(BF)
        w2_lo = (gla_w_alpha2[l] - w2_hi.astype(F32)).astype(BF)
        w2p = jnp.pad(jnp.concatenate([w2_hi, w2_hi, w2_lo], axis=0), ((0, LANES - 3 * GLA_RANK), (0, 0)))
        o_gla = _gla(proj3, small3, w2p, gla_b_alpha[l][None, :], gla_norm_g[l][None, :])

        xk = proj3[:, :, OFF_KC:OFF_KC + LANES].reshape(B, S // CMP_STRIDE, CMP_STRIDE * LANES)
        xv = proj3[:, :, OFF_VC:OFF_VC + LANES].reshape(B, S // CMP_STRIDE, CMP_STRIDE * LANES)
        pek, w1k, w2k = _prep_compress(nsa_cmp_pe_k[l], nsa_cmp_w1_k[l], nsa_cmp_w2_k[l])
        pev, w1v, w2v = _prep_compress(nsa_cmp_pe_v[l], nsa_cmp_w1_v[l], nsa_cmp_w2_v[l])
        kc, vc = _nsa_compress(xk, xv, pek, pev, w1k, w1v, w2k, w2v)
        o_nsa = _nsa_attend(proj3, small3, kc, vc, slope_tab, ovt)

        x2 = _merge_out(o_gla.reshape(n_tok, -1), o_nsa.reshape(n_tok, -1), proj2, x2,
                        w_proj_gla[l].astype(BF), w_proj_nsa[l].astype(BF),
                        w_out[l].astype(BF), norm_mix_post[l][None, :])
        x2 = _ffn(x2, norm_ffn_pre[l][None, :], w_ffn_gate[l].astype(BF), w_ffn_up[l].astype(BF),
                  w_ffn_down[l].astype(BF), norm_ffn_post[l][None, :])
    return x2.reshape(B, S, D)
```

```python
import functools

import numpy as np
import jax
import jax.numpy as jnp
from jax import lax
from jax.experimental import pallas as pl
from jax.experimental.pallas import tpu as pltpu

D_MODEL = 1024
GLA_HEADS = 4
GLA_DK = 128
GLA_DV = 256
GLA_RANK = 16
GLA_TAU = 16.0
GLA_CHUNK = 64
NSA_HEADS = 8
NSA_GROUPS = 2
NSA_REP = 4
NSA_DH = 64
CMP_LEN = 32
CMP_STRIDE = 16
SLC_LEN = 64
N_SEL = 16
WINDOW = 512
QB = 256
SLC_TILE = 128
TRIP_TILES = 6
VROWS = 80
D_FF = 2816
EPS = 1e-6
NEG = -1e30

LANES = 128
VMEM_LIMIT = 56 * 1024 * 1024
BF = jnp.bfloat16
F32 = jnp.float32

OFF_GQ = 0
OFF_GK = 512
OFF_GV = 1024
OFF_GR = 2048
OFF_MG = 3072
OFF_MN = 4096
OFF_NQ = 5120
OFF_KC = 5632
OFF_VC = 5760
OFF_KS = 5888
OFF_VS = 6016
OFF_KW = 6144
OFF_VW = 6272
N_MAIN = 6400
N_SMALL = 384

NT = (((1,), (1,)), ((), ()))
TN = (((0,), (0,)), ((), ()))


LOG2E = 1.4426950408889634
POS_TERMS = 3


def _pos_columns(lane, pos):
    first = lane - NSA_DH
    return jnp.where((first >= 0) & (first < POS_TERMS), pos >> 6,
                     jnp.where((first >= POS_TERMS) & (first < 2 * POS_TERMS), pos & 63, 0))


def _slope_rows(rowi, coef):
    terms = []
    rem = coef
    for _ in range(POS_TERMS):
        t = rem.astype(BF).astype(F32)
        terms.append(t)
        rem = rem - t
    out = jnp.zeros(rowi.shape, F32)
    for i, t in enumerate(terms):
        out = jnp.where(rowi == i, t * float(SLC_LEN), jnp.where(rowi == POS_TERMS + i, t, out))
    return out


def _resident(shape, index_map):
    return pl.BlockSpec(shape, index_map, pipeline_mode=pl.Buffered(1))


def _params(sem):
    return pltpu.CompilerParams(dimension_semantics=sem, vmem_limit_bytes=VMEM_LIMIT)


SRC_GA = 3072
SRC_NQ = 3088
SRC_KV = 3600
SRC_GATE = 4368
SRC_MERGE = 4392
IN_WIDTH = 6440
TAIL_START = IN_WIDTH // LANES * LANES
ALIGNED = OFF_MG
SHIFTED_RUNS = ((SRC_MERGE, 2048, OFF_MG, 1.0), (SRC_NQ, 512, OFF_NQ, NSA_DH ** -0.5), (SRC_KV, 768, OFF_KC, 1.0))


def _in_proj_kernel(x_ref, g_ref, wr_ref, wt_ref, om_ref, os_ref, wm_scr, ws_scr):
    @pl.when(pl.program_id(0) == 0)
    def _():
        r_i = lax.broadcasted_iota(jnp.int32, (2 * LANES, LANES), 0)
        c_i = lax.broadcasted_iota(jnp.int32, (2 * LANES, LANES), 1)
        for src, width, dst, scale in SHIFTED_RUNS:
            shift = jnp.where(r_i == c_i + src % LANES, 1.0, 0.0).astype(BF)
            for j in range(width // LANES):
                b0 = (src // LANES + j) * LANES
                if b0 + 2 * LANES <= TAIL_START:
                    pair = wr_ref[:, b0:b0 + 2 * LANES]
                else:
                    pair = jnp.concatenate([wr_ref[:, b0:b0 + LANES], wt_ref[...]], axis=1)
                moved = jnp.dot(pair, shift, preferred_element_type=F32)
                d0 = dst - ALIGNED + j * LANES
                wm_scr[:, d0:d0 + LANES] = (moved * scale).astype(BF)
        r1 = lax.broadcasted_iota(jnp.int32, (LANES, LANES), 0)
        c1 = lax.broadcasted_iota(jnp.int32, (LANES, LANES), 1)
        rep3 = jnp.where((r1 < GLA_RANK) & (c1 < 3 * GLA_RANK) & ((c1 % GLA_RANK) == r1), 1.0, 0.0).astype(BF)
        ws_scr[:, 0:LANES] = jnp.dot(wr_ref[:, SRC_GA:SRC_GA + LANES], rep3,
                                     preferred_element_type=F32).astype(BF)
        g_blk = SRC_GATE // LANES * LANES
        per_g = NSA_REP * 3
        for g in range(NSA_GROUPS):
            pick = jnp.where((c1 < per_g) & (r1 == c1 + SRC_GATE - g_blk + per_g * g), 1.0, 0.0).astype(BF)
            ws_scr[:, (1 + g) * LANES:(2 + g) * LANES] = jnp.dot(
                wr_ref[:, g_blk:g_blk + LANES], pick, preferred_element_type=F32).astype(BF)

    x = x_ref[...]
    h = (x * lax.rsqrt(jnp.mean(x * x, axis=-1, keepdims=True) + EPS) * g_ref[...]).astype(BF)
    os_ref[...] = jnp.dot(h, ws_scr[...], preferred_element_type=F32)
    step = 512
    for c0 in range(0, ALIGNED, step):
        om_ref[:, c0:c0 + step] = jnp.dot(h, wr_ref[:, c0:c0 + step], preferred_element_type=F32).astype(BF)
    for c0 in range(0, N_MAIN - ALIGNED, step):
        c1_ = min(c0 + step, N_MAIN - ALIGNED)
        om_ref[:, ALIGNED + c0:ALIGNED + c1_] = jnp.dot(h, wm_scr[:, c0:c1_],
                                                        preferred_element_type=F32).astype(BF)


def _in_proj(x2, g, w_raw, w_tail, tm=512):
    n_tok = x2.shape[0]
    return pl.pallas_call(
        _in_proj_kernel,
        grid=(n_tok // tm,),
        in_specs=[
            pl.BlockSpec((tm, D_MODEL), lambda i: (i, 0)),
            _resident((1, D_MODEL), lambda i: (0, 0)),
            _resident((D_MODEL, IN_WIDTH), lambda i: (0, 0)),
            _resident((D_MODEL, LANES), lambda i: (0, 0)),
        ],
        out_specs=[
            pl.BlockSpec((tm, N_MAIN), lambda i: (i, 0)),
            pl.BlockSpec((tm, N_SMALL), lambda i: (i, 0)),
        ],
        out_shape=[
            jax.ShapeDtypeStruct((n_tok, N_MAIN), BF),
            jax.ShapeDtypeStruct((n_tok, N_SMALL), F32),
        ],
        scratch_shapes=[pltpu.VMEM((D_MODEL, N_MAIN - ALIGNED), BF), pltpu.VMEM((D_MODEL, N_SMALL), BF)],
        compiler_params=_params(("arbitrary",)),
    )(x2, g, w_raw, w_tail)


def _gla_kernel(q_ref, k_ref, v_ref, r_ref, a_ref, w2_ref, b2_ref, ng_ref, o_ref, st_scr, *, n_chunks, n_heads):
    blk = pl.program_id(2)

    @pl.when(blk == 0)
    def _():
        st_scr[...] = jnp.zeros_like(st_scr)

    C = GLA_CHUNK
    T = n_chunks * C
    W = n_heads * GLA_DK

    def split3(x):
        hi = x.astype(BF)
        rem = x - hi.astype(F32)
        mid = rem.astype(BF)
        return hi, mid, (rem - mid.astype(F32)).astype(BF)

    a = a_ref[0]
    a_hi = a.astype(BF)
    a_lo = (a - a_hi.astype(F32)).astype(BF)
    lane = lax.broadcasted_iota(jnp.int32, (T, LANES), 1)
    in_lo = (lane >= GLA_RANK) & (lane < 2 * GLA_RANK)
    z = jnp.dot(jnp.where(in_lo, a_lo, a_hi), w2_ref[...], preferred_element_type=F32) + b2_ref[...]
    log_a = (jnp.minimum(z, 0.0) - jnp.log(1.0 + jnp.exp(-jnp.abs(z)))) * (1.0 / GLA_TAU)

    x_wide = jnp.concatenate([log_a[c * C:(c + 1) * C] for c in range(n_chunks)], axis=1)
    x3 = jnp.concatenate(split3(x_wide), axis=0)
    r3 = lax.broadcasted_iota(jnp.int32, (C, 3 * C), 0)
    c3 = lax.broadcasted_iota(jnp.int32, (C, 3 * C), 1) & (C - 1)
    tri3 = jnp.where(c3 <= r3, 1.0, 0.0).astype(BF)
    b_wide = jnp.dot(tri3, x3, preferred_element_type=F32)
    bcum = jnp.concatenate([b_wide[:, c * W:(c + 1) * W] for c in range(n_chunks)], axis=0)
    last_rows = [b_wide[C - 1:C, c * W:(c + 1) * W] for c in range(n_chunks)]
    b_last = jnp.concatenate([jnp.broadcast_to(lr, (C, W)) for lr in last_rows], axis=0)
    decay = [jnp.exp(lr) for lr in last_rows]

    q = q_ref[0].astype(F32)
    k = k_ref[0].astype(F32)
    v = v_ref[0]
    qe = (q * ((GLA_DK ** -0.5) * jnp.exp(bcum))).astype(BF)
    ke = (k * jnp.exp(-bcum)).astype(BF)
    kd = (k * jnp.exp(b_last - bcum)).astype(BF)

    H = min(T, 4 * C)
    row = lax.broadcasted_iota(jnp.int32, (H, H), 0)
    col = lax.broadcasted_iota(jnp.int32, (H, H), 1)
    keep = (col <= row) & ((col >> 6) == (row >> 6))
    ng = ng_ref[...]
    r_all = r_ref[0].astype(F32)
    for hh in range(n_heads):
        ks_ = slice(hh * GLA_DK, (hh + 1) * GLA_DK)
        vs_ = slice(hh * GLA_DV, (hh + 1) * GLA_DV)
        intra = []
        for h0 in range(0, T, H):
            hs = slice(h0, h0 + H)
            attn = lax.dot_general(qe[hs, ks_], ke[hs, ks_], NT, preferred_element_type=F32)
            intra.append(jnp.dot(jnp.where(keep, attn, 0.0).astype(BF), v[hs, vs_], preferred_element_type=F32))
        o = jnp.concatenate(intra, axis=0)

        st = st_scr[hh]
        inter = []
        for c in range(n_chunks):
            sl = slice(c * C, (c + 1) * C)
            inter.append(lax.dot_general(qe[sl, ks_], st.astype(BF), NT, preferred_element_type=F32))
            upd = lax.dot_general(v[sl, vs_], kd[sl, ks_], TN, preferred_element_type=F32)
            st = st * decay[c][:, ks_] + upd
        st_scr[hh] = st
        o = o + jnp.concatenate(inter, axis=0)

        o = o * lax.rsqrt(jnp.mean(o * o, axis=-1, keepdims=True) + EPS) * ng
        r = r_all[:, vs_]
        o_ref[0, :, vs_] = (o * (r * jax.nn.sigmoid(r))).astype(BF)


def _gla(proj3, small3, w2p, b2, ng, blk_tokens=512, heads_per_step=4):
    B, S, _ = proj3.shape
    nblk = S // blk_tokens
    hb = heads_per_step
    kern = functools.partial(_gla_kernel, n_chunks=blk_tokens // GLA_CHUNK, n_heads=hb)
    wk, wv = hb * GLA_DK, hb * GLA_DV
    return pl.pallas_call(
        kern,
        grid=(B, GLA_HEADS // hb, nblk),
        in_specs=[
            pl.BlockSpec((1, blk_tokens, wk), lambda b, h, i: (b, i, OFF_GQ // wk + h)),
            pl.BlockSpec((1, blk_tokens, wk), lambda b, h, i: (b, i, OFF_GK // wk + h)),
            pl.BlockSpec((1, blk_tokens, wv), lambda b, h, i: (b, i, OFF_GV // wv + h)),
            pl.BlockSpec((1, blk_tokens, wv), lambda b, h, i: (b, i, OFF_GR // wv + h)),
            pl.BlockSpec((1, blk_tokens, LANES), lambda b, h, i: (b, i, 0)),
            pl.BlockSpec((LANES, wk), lambda b, h, i: (0, h)),
            pl.BlockSpec((1, wk), lambda b, h, i: (0, h)),
            pl.BlockSpec((1, GLA_DV), lambda b, h, i: (0, 0)),
        ],
        out_specs=pl.BlockSpec((1, blk_tokens, wv), lambda b, h, i: (b, i, h)),
        out_shape=jax.ShapeDtypeStruct((B, S, GLA_HEADS * GLA_DV), BF),
        scratch_shapes=[pltpu.VMEM((hb, GLA_DV, GLA_DK), F32)],
        compiler_params=_params(("parallel", "parallel", "arbitrary")),
    )(proj3, proj3, proj3, proj3, small3, w2p, b2, ng)


def _compress_kernel(xk_ref, xv_ref, pek_ref, pev_ref, w1k_ref, w1v_ref, w2k_ref, w2v_ref, kc_ref, vc_ref):
    n_rows = xk_ref.shape[1]
    half = CMP_STRIDE * LANES
    lane = lax.broadcasted_iota(jnp.int32, (n_rows, LANES), 1)
    row = lax.broadcasted_iota(jnp.int32, (n_rows, LANES), 0)
    end_c = CMP_STRIDE * row + (CMP_LEN - 1)
    c_k = _pos_columns(lane, end_c).astype(F32)
    c_v = jnp.where(lane == NSA_DH, 1.0, 0.0).astype(F32)

    def branch(x_ref, pe_ref, w1_ref, w2_ref, const, o_ref):
        x = x_ref[0].astype(F32)
        xa = (x + pe_ref[0:1, :]).astype(BF)
        xb = (x + pe_ref[1:2, :]).astype(BF)
        a = jnp.dot(xa, w1_ref[0:half, :], preferred_element_type=F32)
        b = jnp.dot(xb, w1_ref[half:2 * half, :], preferred_element_type=F32)
        pre = a + pltpu.roll(b, n_rows - 1, 0)
        hid = (pre * jax.nn.sigmoid(pre)).astype(BF)
        for g in range(NSA_GROUPS):
            o_ref[0, g] = (jnp.dot(hid, w2_ref[g], preferred_element_type=F32) + const).astype(BF)

    branch(xk_ref, pek_ref, w1k_ref, w2k_ref, c_k, kc_ref)
    branch(xv_ref, pev_ref, w1v_ref, w2v_ref, c_v, vc_ref)


def _nsa_compress(xk, xv, pek, pev, w1k, w1v, w2k, w2v):
    B, n_rows, width = xk.shape
    full = lambda shape: _resident(shape, lambda b: (0,) * len(shape))
    out = jax.ShapeDtypeStruct((B, NSA_GROUPS, n_rows, LANES), BF)
    return pl.pallas_call(
        _compress_kernel,
        grid=(B,),
        in_specs=[
            pl.BlockSpec((1, n_rows, width), lambda b: (b, 0, 0)),
            pl.BlockSpec((1, n_rows, width), lambda b: (b, 0, 0)),
            full(pek.shape), full(pev.shape), full(w1k.shape), full(w1v.shape),
            full(w2k.shape), full(w2v.shape),
        ],
        out_specs=[pl.BlockSpec((1, NSA_GROUPS, n_rows, LANES), lambda b: (b, 0, 0, 0))] * 2,
        out_shape=[out, out],
        compiler_params=_params(("parallel",)),
    )(xk, xv, pek, pev, w1k, w1v, w2k, w2v)


def _nsa_kernel(q_ref, ks_ref, vs_ref, kw_ref, vw_ref, kc_ref, vc_ref, gt_ref, sl_ref, ov_ref,
                o_ref, ksa, kwa, vst, vwt, vct, acc_scr, pw_scr, act_ref, *, seq):
    qi = pl.program_id(2)
    n_slc = seq // SLC_LEN
    R = NSA_REP
    KT = SLC_TILE
    WK = WINDOW + QB
    n_cmp = kc_ref.shape[2]

    def t_bf(x):
        return x.astype(F32).T[0:VROWS].astype(BF)

    @pl.when(qi == 0)
    def _():
        grp = pl.program_id(1)
        lane = lax.broadcasted_iota(jnp.int32, (seq, LANES), 1)
        pos = lax.broadcasted_iota(jnp.int32, (seq, LANES), 0)
        blk = pos >> 6
        off = pos & 63
        r_i = lax.broadcasted_iota(jnp.int32, (LANES, LANES), 0)
        c_i = lax.broadcasted_iota(jnp.int32, (LANES, LANES), 1)
        pick = jnp.where((c_i < NSA_DH) & (r_i == c_i + grp * NSA_DH), 1.0, 0.0).astype(BF)
        c_s = jnp.where(lane == LANES - 1, off,
                        jnp.where((lane >= NSA_DH) & (lane - (NSA_DH - 1) == blk), 1, 0))
        ksa[...] = (jnp.dot(ks_ref[0], pick, preferred_element_type=F32) + c_s.astype(F32)).astype(BF)
        c_w = _pos_columns(lane, pos)
        kwa[0:WINDOW, :] = jnp.zeros((WINDOW, LANES), BF)
        kwa[WINDOW:WINDOW + seq, :] = (jnp.dot(kw_ref[0], pick, preferred_element_type=F32)
                                       + c_w.astype(F32)).astype(BF)
        ones_rows = jnp.where(lax.broadcasted_iota(jnp.int32, (VROWS - NSA_DH, LANES), 0) == 0, 1.0, 0.0)
        n_pad = WINDOW // LANES
        for c in range(n_pad):
            vwt[c] = jnp.zeros((VROWS, LANES), BF)

        def v_tile(x):
            xt = x.astype(F32).T
            dims = jnp.where(grp == 0, xt[0:NSA_DH], xt[NSA_DH:2 * NSA_DH])
            return jnp.concatenate([dims, ones_rows], axis=0).astype(BF)

        def fill(c, carry):
            rows = pl.ds(pl.multiple_of(c * LANES, LANES), LANES)
            vst[c] = v_tile(vs_ref[0, rows, :])
            vwt[c + n_pad] = v_tile(vw_ref[0, rows, :])
            return carry

        lax.fori_loop(0, seq // LANES, fill, 0)
        for c in range(n_cmp // LANES):
            vct[:, c * LANES:(c + 1) * LANES] = t_bf(vc_ref[0, 0, c * LANES:(c + 1) * LANES, :])

    q0 = qi * QB
    rowi = lax.broadcasted_iota(jnp.int32, (NSA_DH, QB), 0)
    slopes = [jnp.concatenate([sl_ref[0, r:r + 1, :]] * (QB // LANES), axis=1) for r in range(R)]
    q_all = q_ref[0].astype(F32).T
    q_t = [q_all[r * NSA_DH:(r + 1) * NSA_DH] for r in range(R)]

    qw = jnp.concatenate(
        [jnp.concatenate([q_t[r] * LOG2E, _slope_rows(rowi, slopes[r] * LOG2E)], axis=0).astype(BF)
         for r in range(R)], axis=1)

    def tile4(x):
        return jnp.concatenate([x] * R, axis=1)

    ovt = ov_ref[...]

    def cmp_branch(rows):
        def fn():
            kw = kwa[pl.ds(pl.multiple_of(q0, QB), WK), :]
            s_w = jnp.dot(kw, qw, preferred_element_type=F32)
            c_w = lax.broadcasted_iota(jnp.int32, (WK, QB), 0)
            d_w = c_w - lax.broadcasted_iota(jnp.int32, (WK, QB), 1)
            mask_w = (d_w > 0) & (d_w <= WINDOW) & (c_w >= WINDOW - q0)
            s_w = s_w + tile4(jnp.where(mask_w, 0.0, NEG))
            m_w = jnp.max(s_w, axis=0, keepdims=True)
            pw_scr[...] = jnp.exp2(s_w - m_w).astype(BF)

            s_c = jnp.dot(kc_ref[0, 0, 0:rows, :], qw, preferred_element_type=F32)
            e_c = CMP_STRIDE * lax.broadcasted_iota(jnp.int32, (rows, QB), 0) + (CMP_LEN - 1)
            t_c = q0 + lax.broadcasted_iota(jnp.int32, (rows, QB), 1)
            s_c = s_c + tile4(jnp.where(e_c <= t_c, 0.0, NEG))
            m_c = jnp.maximum(jnp.max(s_c, axis=0, keepdims=True), 0.1 * NEG)
            p_c = jnp.exp2(s_c - m_c)
            l_c = jnp.sum(p_c, axis=0, keepdims=True)
            p_c = p_c * jnp.where(l_c > 0.0, 1.0 / l_c, 0.0)
            o_cmp = jnp.dot(vct[:, 0:rows], p_c.astype(BF), preferred_element_type=F32)
            psum = p_c[:, 0:QB]
            for r in range(1, R):
                psum = psum + p_c[:, r * QB:(r + 1) * QB]
            p_hi = psum.astype(BF)
            rem = psum - p_hi.astype(F32)
            p_mid = rem.astype(BF)
            p_lo = (rem - p_mid.astype(F32)).astype(BF)
            ov = ovt[:, 0:rows]
            return o_cmp, (jnp.dot(ov, p_hi, preferred_element_type=F32)
                           + jnp.dot(ov, p_mid, preferred_element_type=F32)
                           + jnp.dot(ov, p_lo, preferred_element_type=F32))
        return fn

    half = n_cmp // 2
    o_c, imp = lax.cond(q0 + QB <= CMP_STRIDE * half + CMP_LEN - 1, cmp_branch(half), cmp_branch(n_cmp))

    NR = ov_ref.shape[0]
    SUB = 8
    jblk = lax.broadcasted_iota(jnp.int32, (NR, QB), 0)
    t_q = q0 + lax.broadcasted_iota(jnp.int32, (NR, QB), 1)
    cur = t_q >> 6
    forced = (jblk == 0) | (jblk == cur) | (jblk == cur - 1)
    score = jnp.where(jblk > cur, NEG, jnp.where(forced, -NEG, imp))
    n_slab = -(-n_slc // SUB)
    isub = lax.broadcasted_iota(jnp.int32, (SUB, QB), 0)

    def rank_counts(ns):
        def fn():
            slabs = [score[a * SUB:(a + 1) * SUB, :] for a in range(ns)]
            cnts = [jnp.zeros((SUB, QB), F32) for _ in range(ns)]
            for jp in range(min(ns * SUB, n_slc)):
                rowv = jnp.broadcast_to(score[jp:jp + 1, :], (SUB, QB))
                for a in range(ns):
                    if a < jp // SUB:
                        beats = jnp.where(rowv > slabs[a], 1.0, 0.0)
                    elif a > jp // SUB:
                        beats = jnp.where(rowv >= slabs[a], 1.0, 0.0)
                    else:
                        beats = jnp.where(isub > jp % SUB, jnp.where(rowv >= slabs[a], 1.0, 0.0),
                                          jnp.where(rowv > slabs[a], 1.0, 0.0))
                    cnts[a] = cnts[a] + beats
            rest = NR - ns * SUB
            return jnp.concatenate(cnts + [jnp.full((rest, QB), float(NR), F32)] * (rest > 0), axis=0)
        return fn

    last_blk = (q0 + QB - 1) >> 6
    cnt = lax.switch(last_blk // SUB, [rank_counts(ns) for ns in range(1, n_slab + 1)])
    sel = (cnt < float(N_SEL)) & (jblk <= cur) & (jblk < n_slc)
    a_nat = jnp.where(jblk == 0, 1.0, jnp.where(sel, (SLC_LEN * jblk).astype(F32), NEG))
    a_nat = jnp.where(jblk < n_slc, a_nat, 0.0)
    a_t = pltpu.roll(a_nat, NR - 1, 0)

    any_q = jnp.max(jnp.where(sel, 1.0, 0.0), axis=1, keepdims=True)
    jcol = lax.broadcasted_iota(jnp.int32, (NR, 1), 0)
    bits = jnp.where(any_q > 0.0, lax.shift_left(jnp.int32(1), jcol & 31), 0)
    word0 = jnp.sum(jnp.where(jcol < 32, bits, 0))
    word1 = jnp.sum(jnp.where(jcol >= 32, bits, 0))

    qs = jnp.concatenate([jnp.concatenate([q_t[r], a_t * slopes[r]], axis=0).astype(BF) for r in range(R)],
                         axis=1)

    tiles_per = KT // LANES
    blocks_per = KT // SLC_LEN
    n_full = q0 // KT

    def scan(ti, n):
        word = jnp.where(ti < 32 // blocks_per, word0, word1)
        hit = (lax.shift_right_logical(word, (ti * blocks_per) & 31) & ((1 << blocks_per) - 1)) != 0
        act_ref[n] = ti
        return n + jnp.where(hit, 1, 0)

    n_act = lax.fori_loop(0, n_full, scan, 0)

    def scores(ti):
        k0 = pl.multiple_of(ti * KT, KT)
        return jnp.dot(ksa[pl.ds(k0, KT), :], qs, preferred_element_type=F32)

    def values_t(ti):
        return [vst[ti * tiles_per + c] for c in range(tiles_per)]

    def absorb(s, v_tiles, m_prev):
        m_new = jnp.maximum(m_prev, jnp.max(s, axis=0, keepdims=True))
        alpha = jnp.exp(m_prev - m_new)
        p = jnp.exp(s - m_new).astype(BF)
        acc_scr[...] = acc_scr[...] * alpha + jnp.dot(jnp.concatenate(v_tiles, axis=1), p,
                                                     preferred_element_type=F32)
        return m_new

    acc_scr[...] = jnp.zeros(acc_scr.shape, F32)

    def chains(tiles, diag_flags, m_run):
        s_all = []
        for t, is_diag in zip(tiles, diag_flags):
            s = scores(t)
            if is_diag:
                p_d = t * KT + lax.broadcasted_iota(jnp.int32, (KT, QB), 0)
                t_d = q0 + lax.broadcasted_iota(jnp.int32, (KT, QB), 1)
                s = s + tile4(jnp.where(p_d <= t_d, 0.0, NEG))
            s_all.append(s)
        for t, s in zip(tiles, s_all):
            m_run = absorb(s, values_t(t), m_run)
        return m_run

    n_trip = n_act // TRIP_TILES
    m_q = lax.fori_loop(0, n_trip, lambda j, m: chains([act_ref[TRIP_TILES * j + i] for i in range(TRIP_TILES)],
                                                       [False] * TRIP_TILES, m),
                        jnp.full((1, R * QB), NEG, F32))
    n_diag = QB // KT

    def finish(rest):
        def fn(m_run):
            tiles = [act_ref[TRIP_TILES * n_trip + i] for i in range(rest)] + [n_full + d for d in range(n_diag)]
            chains(tiles, [False] * rest + [True] * n_diag, m_run)
            acc_s = acc_scr[...]
            vw_t = jnp.concatenate([vwt[qi * (QB // LANES) + c] for c in range(WK // LANES)], axis=1)
            acc_w = jnp.dot(vw_t, pw_scr[...], preferred_element_type=F32)
            sg_t = jax.nn.sigmoid(gt_ref[0]).T
            outs = []
            for r in range(R):
                cs = slice(r * QB, (r + 1) * QB)
                a_s = acc_s[:, cs]
                a_w = acc_w[:, cs]
                g_c = sg_t[3 * r:3 * r + 1, :]
                g_s = sg_t[3 * r + 1:3 * r + 2, :] / a_s[NSA_DH:NSA_DH + 1, :]
                g_w = sg_t[3 * r + 2:3 * r + 3, :] / a_w[NSA_DH:NSA_DH + 1, :]
                outs.append((g_c * o_c[:, cs] + g_s * a_s + g_w * a_w)[0:NSA_DH])
            o_ref[0] = jnp.concatenate(outs, axis=0).T.astype(BF)
        return fn

    lax.switch(n_act - TRIP_TILES * n_trip, [finish(rest) for rest in range(TRIP_TILES)], m_q)


def _nsa_attend(proj3, small3, kc, vc, slope_tab, ovt):
    B, S, _ = proj3.shape
    G, R = NSA_GROUPS, NSA_REP
    n_cmp = kc.shape[2]
    kern = functools.partial(_nsa_kernel, seq=S)
    kv_spec = lambda off: pl.BlockSpec((1, S, LANES), lambda b, g, i: (b, 0, off // LANES))
    return pl.pallas_call(
        kern,
        grid=(B, G, S // QB),
        in_specs=[
            pl.BlockSpec((1, QB, R * NSA_DH), lambda b, g, i: (b, i, OFF_NQ // (R * NSA_DH) + g)),
            kv_spec(OFF_KS), kv_spec(OFF_VS), kv_spec(OFF_KW), kv_spec(OFF_VW),
            pl.BlockSpec((1, 1, n_cmp, LANES), lambda b, g, i: (b, g, 0, 0)),
            pl.BlockSpec((1, 1, n_cmp, LANES), lambda b, g, i: (b, g, 0, 0)),
            pl.BlockSpec((1, QB, LANES), lambda b, g, i: (b, i, 1 + g)),
            pl.BlockSpec((1, 8, LANES), lambda b, g, i: (g, 0, 0)),
            _resident(ovt.shape, lambda b, g, i: (0, 0)),
        ],
        out_specs=pl.BlockSpec((1, QB, R * NSA_DH), lambda b, g, i: (b, i, g)),
        out_shape=jax.ShapeDtypeStruct((B, S, G * R * NSA_DH), BF),
        scratch_shapes=[
            pltpu.VMEM((S, LANES), BF),
            pltpu.VMEM((S + WINDOW, LANES), BF),
            pltpu.VMEM((S // LANES, VROWS, LANES), BF),
            pltpu.VMEM(((S + WINDOW) // LANES, VROWS, LANES), BF),
            pltpu.VMEM((VROWS, n_cmp), BF),
            pltpu.VMEM((VROWS, R * QB), F32),
            pltpu.VMEM((WINDOW + QB, R * QB), BF),
            pltpu.SMEM((S // SLC_TILE + 1,), jnp.int32),
        ],
        compiler_params=_params(("parallel", "parallel", "arbitrary")),
    )(proj3, proj3, proj3, proj3, proj3, kc, vc, small3, slope_tab, ovt)


def _merge_kernel(og_ref, on_ref, mg_ref, mn_ref, x_ref, wg_ref, wn_ref, wo_ref, g_ref, o_ref):
    a = jnp.dot(og_ref[...], wg_ref[...], preferred_element_type=F32)
    b = jnp.dot(on_ref[...], wn_ref[...], preferred_element_type=F32)
    mixed = jax.nn.sigmoid(mg_ref[...].astype(F32)) * a + jax.nn.sigmoid(mn_ref[...].astype(F32)) * b
    y = jnp.dot(mixed.astype(BF), wo_ref[...], preferred_element_type=F32)
    y = y * lax.rsqrt(jnp.mean(y * y, axis=-1, keepdims=True) + EPS) * g_ref[...]
    o_ref[...] = x_ref[...] + y


def _merge_out(o_gla2, o_nsa2, proj2, x2, wg, wn, wo, g, tm=512):
    n_tok = x2.shape[0]
    D = D_MODEL
    return pl.pallas_call(
        _merge_kernel,
        grid=(n_tok // tm,),
        in_specs=[
            pl.BlockSpec((tm, D), lambda i: (i, 0)),
            pl.BlockSpec((tm, o_nsa2.shape[1]), lambda i: (i, 0)),
            pl.BlockSpec((tm, D), lambda i: (i, OFF_MG // D)),
            pl.BlockSpec((tm, D), lambda i: (i, OFF_MN // D)),
            pl.BlockSpec((tm, D), lambda i: (i, 0)),
            _resident(wg.shape, lambda i: (0, 0)),
            _resident(wn.shape, lambda i: (0, 0)),
            _resident(wo.shape, lambda i: (0, 0)),
            _resident((1, D), lambda i: (0, 0)),
        ],
        out_specs=pl.BlockSpec((tm, D), lambda i: (i, 0)),
        out_shape=jax.ShapeDtypeStruct((n_tok, D), F32),
        compiler_params=_params(("parallel",)),
    )(o_gla2, o_nsa2, proj2, proj2, x2, wg, wn, wo, g)


def _ffn_kernel(x_ref, gpre_ref, wg_ref, wu_ref, wd_ref, gpost_ref, o_ref, acc_scr, *, chunk):
    x = x_ref[...]
    h = (x * lax.rsqrt(jnp.mean(x * x, axis=-1, keepdims=True) + EPS) * gpre_ref[...]).astype(BF)
    d_ff = wg_ref.shape[1]
    for n, c0 in enumerate(range(0, d_ff, chunk)):
        c1 = min(c0 + chunk, d_ff)
        a = jnp.dot(h, wg_ref[:, c0:c1], preferred_element_type=F32)
        u = jnp.dot(h, wu_ref[:, c0:c1], preferred_element_type=F32)
        t = (a * jax.nn.sigmoid(a) * u).astype(BF)
        part = jnp.dot(t, wd_ref[c0:c1, :], preferred_element_type=F32)
        if n == 0:
            acc_scr[...] = part
        else:
            acc_scr[...] += part
    f = acc_scr[...]
    o_ref[...] = x + f * lax.rsqrt(jnp.mean(f * f, axis=-1, keepdims=True) + EPS) * gpost_ref[...]


def _ffn(x2, gpre, wg, wu, wd, gpost, tm=512, chunk=512):
    n_tok = x2.shape[0]
    D = D_MODEL
    kern = functools.partial(_ffn_kernel, chunk=chunk)
    return pl.pallas_call(
        kern,
        grid=(n_tok // tm,),
        in_specs=[
            pl.BlockSpec((tm, D), lambda i: (i, 0)),
            _resident((1, D), lambda i: (0, 0)),
            _resident(wg.shape, lambda i: (0, 0)),
            _resident(wu.shape, lambda i: (0, 0)),
            _resident(wd.shape, lambda i: (0, 0)),
            _resident((1, D), lambda i: (0, 0)),
        ],
        out_specs=pl.BlockSpec((tm, D), lambda i: (i, 0)),
        out_shape=jax.ShapeDtypeStruct((n_tok, D), F32),
        scratch_shapes=[pltpu.VMEM((tm, D), F32)],
        compiler_params=_params(("parallel",)),
    )(x2, gpre, wg, wu, wd, gpost)


def _prep_compress(pe, w1, w2):
    eye = jnp.eye(NSA_GROUPS, dtype=F32)
    w1r = w1.reshape(CMP_LEN, NSA_DH, NSA_DH)
    w1e = jnp.einsum('lde,gh->lgdhe', w1r, eye).reshape(CMP_LEN * NSA_GROUPS * NSA_DH, NSA_GROUPS * NSA_DH)
    pe_e = jnp.broadcast_to(pe[:, None, :], (CMP_LEN, NSA_GROUPS, NSA_DH)).reshape(2, CMP_STRIDE * LANES)
    pe_e = jnp.pad(pe_e, ((0, 6), (0, 0)))
    w2e = jnp.stack([
        jnp.pad(jnp.pad(w2, ((g * NSA_DH, (NSA_GROUPS - 1 - g) * NSA_DH), (0, 0))), ((0, 0), (0, LANES - NSA_DH)))
        for g in range(NSA_GROUPS)])
    return pe_e.astype(F32), w1e.astype(BF), w2e.astype(BF)


def _overlap_table(seq):
    n_cmp = (seq - CMP_LEN) // CMP_STRIDE + 1
    n_slc = seq // SLC_LEN
    sc = CMP_STRIDE * np.arange(n_cmp)
    ss = SLC_LEN * np.arange(n_slc)
    ov = np.clip(np.minimum(sc[:, None] + CMP_LEN, ss[None, :] + SLC_LEN)
                 - np.maximum(sc[:, None], ss[None, :]), 0, None).astype(np.float32) / CMP_LEN
    ovt = np.zeros((NSA_DH, n_cmp + 1), np.float32)
    ovt[:n_slc, :n_cmp] = ov.T
    return jnp.asarray(ovt, dtype=BF)


def kernel(x, norm_mix_pre, norm_mix_post, norm_ffn_pre, norm_ffn_post, w_in, gla_w_alpha2, gla_b_alpha, gla_norm_g, nsa_cmp_pe_k, nsa_cmp_w1_k, nsa_cmp_w2_k, nsa_cmp_pe_v, nsa_cmp_w1_v, nsa_cmp_w2_v, w_proj_gla, w_proj_nsa, w_out, w_ffn_gate, w_ffn_up, w_ffn_down):
    B, S, D = x.shape
    depth = w_in.shape[0]
    n_tok = B * S
    h_idx = jnp.arange(NSA_HEADS, dtype=F32)
    slopes = jnp.exp2(-8.0 * (h_idx + 1.0) / NSA_HEADS).reshape(NSA_GROUPS, NSA_REP, 1)
    slope_tab = jnp.broadcast_to(jnp.pad(slopes, ((0, 0), (0, 8 - NSA_REP), (0, 0))), (NSA_GROUPS, 8, LANES))
    ovt = _overlap_table(S)
    x2 = x.reshape(n_tok, D)
    for l in range(depth):
        w_raw = w_in[l].astype(BF)
        w_tail = jnp.pad(w_raw[:, TAIL_START:], ((0, 0), (0, TAIL_START + LANES - IN_WIDTH)))
        proj2, small2 = _in_proj(x2, norm_mix_pre[l][None, :], w_raw, w_tail)
        proj3 = proj2.reshape(B, S, N_MAIN)
        small3 = small2.reshape(B, S, N_SMALL)

        w2_hi = gla_w_alpha2[l].astype(BF)
        w2_lo = (gla_w_alpha2[l] - w2_hi.astype(F32)).astype(BF)
        w2p = jnp.pad(jnp.concatenate([w2_hi, w2_hi, w2_lo], axis=0), ((0, LANES - 3 * GLA_RANK), (0, 0)))
        o_gla = _gla(proj3, small3, w2p, gla_b_alpha[l][None, :], gla_norm_g[l][None, :])

        xk = proj3[:, :, OFF_KC:OFF_KC + LANES].reshape(B, S // CMP_STRIDE, CMP_STRIDE * LANES)
        xv = proj3[:, :, OFF_VC:OFF_VC + LANES].reshape(B, S // CMP_STRIDE, CMP_STRIDE * LANES)
        pek, w1k, w2k = _prep_compress(nsa_cmp_pe_k[l], nsa_cmp_w1_k[l], nsa_cmp_w2_k[l])
        pev, w1v, w2v = _prep_compress(nsa_cmp_pe_v[l], nsa_cmp_w1_v[l], nsa_cmp_w2_v[l])
        kc, vc = _nsa_compress(xk, xv, pek, pev, w1k, w1v, w2k, w2v)
        o_nsa = _nsa_attend(proj3, small3, kc, vc, slope_tab, ovt)

        x2 = _merge_out(o_gla.reshape(n_tok, -1), o_nsa.reshape(n_tok, -1), proj2, x2,
                        w_proj_gla[l].astype(BF), w_proj_nsa[l].astype(BF),
                        w_out[l].astype(BF), norm_mix_post[l][None, :])
        x2 = _ffn(x2, norm_ffn_pre[l][None, :], w_ffn_gate[l].astype(BF), w_ffn_up[l].astype(BF),
                  w_ffn_down[l].astype(BF), norm_ffn_post[l][None, :])
    return x2.reshape(B, S, D)
```

```python
import functools

import numpy as np
import jax
import jax.numpy as jnp
from jax import lax
from jax.experimental import pallas as pl
from jax.experimental.pallas import tpu as pltpu

D_MODEL = 1024
GLA_HEADS = 4
GLA_DK = 128
GLA_DV = 256
GLA_RANK = 16
GLA_TAU = 16.0
GLA_CHUNK = 64
NSA_HEADS = 8
NSA_GROUPS = 2
NSA_REP = 4
NSA_DH = 64
CMP_LEN = 32
CMP_STRIDE = 16
SLC_LEN = 64
N_SEL = 16
WINDOW = 512
QB = 256
SLC_TILE = 128
TRIP_TILES = 12
VROWS = 80
D_FF = 2816
EPS = 1e-6
NEG = -1e30

LANES = 128
VMEM_LIMIT = 56 * 1024 * 1024
BF = jnp.bfloat16
F32 = jnp.float32

OFF_GQ = 0
OFF_GK = 512
OFF_GV = 1024
OFF_GR = 2048
OFF_MG = 3072
OFF_MN = 4096
OFF_NQ = 5120
OFF_KC = 5632
OFF_VC = 5760
OFF_KS = 5888
OFF_VS = 6016
OFF_KW = 6144
OFF_VW = 6272
N_MAIN = 6400
N_SMALL = 384

NT = (((1,), (1,)), ((), ()))
TN = (((0,), (0,)), ((), ()))


LOG2E = 1.4426950408889634
POS_TERMS = 3


def _pos_columns(lane, pos):
    first = lane - NSA_DH
    return jnp.where((first >= 0) & (first < POS_TERMS), pos >> 6,
                     jnp.where((first >= POS_TERMS) & (first < 2 * POS_TERMS), pos & 63, 0))


def _slope_rows(rowi, coef):
    terms = []
    rem = coef
    for _ in range(POS_TERMS):
        t = rem.astype(BF).astype(F32)
        terms.append(t)
        rem = rem - t
    out = jnp.zeros(rowi.shape, F32)
    for i, t in enumerate(terms):
        out = jnp.where(rowi == i, t * float(SLC_LEN), jnp.where(rowi == POS_TERMS + i, t, out))
    return out


def _resident(shape, index_map):
    return pl.BlockSpec(shape, index_map, pipeline_mode=pl.Buffered(1))


def _params(sem):
    return pltpu.CompilerParams(dimension_semantics=sem, vmem_limit_bytes=VMEM_LIMIT)


SRC_GA = 3072
SRC_NQ = 3088
SRC_KV = 3600
SRC_GATE = 4368
SRC_MERGE = 4392
IN_WIDTH = 6440
TAIL_START = IN_WIDTH // LANES * LANES
ALIGNED = OFF_MG
SHIFTED_RUNS = ((SRC_MERGE, 2048, OFF_MG, 1.0), (SRC_NQ, 512, OFF_NQ, NSA_DH ** -0.5), (SRC_KV, 768, OFF_KC, 1.0))


def _in_proj_kernel(x_ref, g_ref, wr_ref, wt_ref, om_ref, os_ref, wm_scr, ws_scr):
    @pl.when(pl.program_id(0) == 0)
    def _():
        r_i = lax.broadcasted_iota(jnp.int32, (2 * LANES, LANES), 0)
        c_i = lax.broadcasted_iota(jnp.int32, (2 * LANES, LANES), 1)
        for src, width, dst, scale in SHIFTED_RUNS:
            shift = jnp.where(r_i == c_i + src % LANES, 1.0, 0.0).astype(BF)
            for j in range(width // LANES):
                b0 = (src // LANES + j) * LANES
                if b0 + 2 * LANES <= TAIL_START:
                    pair = wr_ref[:, b0:b0 + 2 * LANES]
                else:
                    pair = jnp.concatenate([wr_ref[:, b0:b0 + LANES], wt_ref[...]], axis=1)
                moved = jnp.dot(pair, shift, preferred_element_type=F32)
                d0 = dst - ALIGNED + j * LANES
                wm_scr[:, d0:d0 + LANES] = (moved * scale).astype(BF)
        r1 = lax.broadcasted_iota(jnp.int32, (LANES, LANES), 0)
        c1 = lax.broadcasted_iota(jnp.int32, (LANES, LANES), 1)
        rep3 = jnp.where((r1 < GLA_RANK) & (c1 < 3 * GLA_RANK) & ((c1 % GLA_RANK) == r1), 1.0, 0.0).astype(BF)
        ws_scr[:, 0:LANES] = jnp.dot(wr_ref[:, SRC_GA:SRC_GA + LANES], rep3,
                                     preferred_element_type=F32).astype(BF)
        g_blk = SRC_GATE // LANES * LANES
        per_g = NSA_REP * 3
        for g in range(NSA_GROUPS):
            pick = jnp.where((c1 < per_g) & (r1 == c1 + SRC_GATE - g_blk + per_g * g), 1.0, 0.0).astype(BF)
            ws_scr[:, (1 + g) * LANES:(2 + g) * LANES] = jnp.dot(
                wr_ref[:, g_blk:g_blk + LANES], pick, preferred_element_type=F32).astype(BF)

    x = x_ref[...]
    h = (x * lax.rsqrt(jnp.mean(x * x, axis=-1, keepdims=True) + EPS) * g_ref[...]).astype(BF)
    os_ref[...] = jnp.dot(h, ws_scr[...], preferred_element_type=F32)
    step = 512
    for c0 in range(0, ALIGNED, step):
        om_ref[:, c0:c0 + step] = jnp.dot(h, wr_ref[:, c0:c0 + step], preferred_element_type=F32).astype(BF)
    for c0 in range(0, N_MAIN - ALIGNED, step):
        c1_ = min(c0 + step, N_MAIN - ALIGNED)
        om_ref[:, ALIGNED + c0:ALIGNED + c1_] = jnp.dot(h, wm_scr[:, c0:c1_],
                                                        preferred_element_type=F32).astype(BF)


def _in_proj(x2, g, w_raw, w_tail, tm=512):
    n_tok = x2.shape[0]
    return pl.pallas_call(
        _in_proj_kernel,
        grid=(n_tok // tm,),
        in_specs=[
            pl.BlockSpec((tm, D_MODEL), lambda i: (i, 0)),
            _resident((1, D_MODEL), lambda i: (0, 0)),
            _resident((D_MODEL, IN_WIDTH), lambda i: (0, 0)),
            _resident((D_MODEL, LANES), lambda i: (0, 0)),
        ],
        out_specs=[
            pl.BlockSpec((tm, N_MAIN), lambda i: (i, 0)),
            pl.BlockSpec((tm, N_SMALL), lambda i: (i, 0)),
        ],
        out_shape=[
            jax.ShapeDtypeStruct((n_tok, N_MAIN), BF),
            jax.ShapeDtypeStruct((n_tok, N_SMALL), F32),
        ],
        scratch_shapes=[pltpu.VMEM((D_MODEL, N_MAIN - ALIGNED), BF), pltpu.VMEM((D_MODEL, N_SMALL), BF)],
        compiler_params=_params(("arbitrary",)),
    )(x2, g, w_raw, w_tail)


def _gla_kernel(q_ref, k_ref, v_ref, r_ref, a_ref, w2_ref, b2_ref, ng_ref, o_ref, st_scr, *, n_chunks, n_heads):
    blk = pl.program_id(2)

    @pl.when(blk == 0)
    def _():
        st_scr[...] = jnp.zeros_like(st_scr)

    C = GLA_CHUNK
    T = n_chunks * C
    W = n_heads * GLA_DK

    def split3(x):
        hi = x.astype(BF)
        rem = x - hi.astype(F32)
        mid = rem.astype(BF)
        return hi, mid, (rem - mid.astype(F32)).astype(BF)

    a = a_ref[0]
    a_hi = a.astype(BF)
    a_lo = (a - a_hi.astype(F32)).astype(BF)
    lane = lax.broadcasted_iota(jnp.int32, (T, LANES), 1)
    in_lo = (lane >= GLA_RANK) & (lane < 2 * GLA_RANK)
    z = jnp.dot(jnp.where(in_lo, a_lo, a_hi), w2_ref[...], preferred_element_type=F32) + b2_ref[...]
    log_a = (jnp.minimum(z, 0.0) - jnp.log(1.0 + jnp.exp(-jnp.abs(z)))) * (1.0 / GLA_TAU)

    x_wide = jnp.concatenate([log_a[c * C:(c + 1) * C] for c in range(n_chunks)], axis=1)
    x3 = jnp.concatenate(split3(x_wide), axis=0)
    r3 = lax.broadcasted_iota(jnp.int32, (C, 3 * C), 0)
    c3 = lax.broadcasted_iota(jnp.int32, (C, 3 * C), 1) & (C - 1)
    tri3 = jnp.where(c3 <= r3, 1.0, 0.0).astype(BF)
    b_wide = jnp.dot(tri3, x3, preferred_element_type=F32)
    bcum = jnp.concatenate([b_wide[:, c * W:(c + 1) * W] for c in range(n_chunks)], axis=0)
    last_rows = [b_wide[C - 1:C, c * W:(c + 1) * W] for c in range(n_chunks)]
    b_last = jnp.concatenate([jnp.broadcast_to(lr, (C, W)) for lr in last_rows], axis=0)
    decay = [jnp.exp(lr) for lr in last_rows]

    q = q_ref[0].astype(F32)
    k = k_ref[0].astype(F32)
    v = v_ref[0]
    qe = (q * ((GLA_DK ** -0.5) * jnp.exp(bcum))).astype(BF)
    ke = (k * jnp.exp(-bcum)).astype(BF)
    kd = (k * jnp.exp(b_last - bcum)).astype(BF)

    H = min(T, 4 * C)
    row = lax.broadcasted_iota(jnp.int32, (H, H), 0)
    col = lax.broadcasted_iota(jnp.int32, (H, H), 1)
    keep = (col <= row) & ((col >> 6) == (row >> 6))
    ng = ng_ref[...]
    r_all = r_ref[0].astype(F32)
    for hh in range(n_heads):
        ks_ = slice(hh * GLA_DK, (hh + 1) * GLA_DK)
        vs_ = slice(hh * GLA_DV, (hh + 1) * GLA_DV)
        intra = []
        for h0 in range(0, T, H):
            hs = slice(h0, h0 + H)
            attn = lax.dot_general(qe[hs, ks_], ke[hs, ks_], NT, preferred_element_type=F32)
            intra.append(jnp.dot(jnp.where(keep, attn, 0.0).astype(BF), v[hs, vs_], preferred_element_type=F32))
        o = jnp.concatenate(intra, axis=0)

        st = st_scr[hh]
        inter = []
        for c in range(n_chunks):
            sl = slice(c * C, (c + 1) * C)
            inter.append(lax.dot_general(qe[sl, ks_], st.astype(BF), NT, preferred_element_type=F32))
            upd = lax.dot_general(v[sl, vs_], kd[sl, ks_], TN, preferred_element_type=F32)
            st = st * decay[c][:, ks_] + upd
        st_scr[hh] = st
        o = o + jnp.concatenate(inter, axis=0)

        o = o * lax.rsqrt(jnp.mean(o * o, axis=-1, keepdims=True) + EPS) * ng
        r = r_all[:, vs_]
        o_ref[0, :, vs_] = (o * (r * jax.nn.sigmoid(r))).astype(BF)


def _gla(proj3, small3, w2p, b2, ng, blk_tokens=512, heads_per_step=4):
    B, S, _ = proj3.shape
    nblk = S // blk_tokens
    hb = heads_per_step
    kern = functools.partial(_gla_kernel, n_chunks=blk_tokens // GLA_CHUNK, n_heads=hb)
    wk, wv = hb * GLA_DK, hb * GLA_DV
    return pl.pallas_call(
        kern,
        grid=(B, GLA_HEADS // hb, nblk),
        in_specs=[
            pl.BlockSpec((1, blk_tokens, wk), lambda b, h, i: (b, i, OFF_GQ // wk + h)),
            pl.BlockSpec((1, blk_tokens, wk), lambda b, h, i: (b, i, OFF_GK // wk + h)),
            pl.BlockSpec((1, blk_tokens, wv), lambda b, h, i: (b, i, OFF_GV // wv + h)),
            pl.BlockSpec((1, blk_tokens, wv), lambda b, h, i: (b, i, OFF_GR // wv + h)),
            pl.BlockSpec((1, blk_tokens, LANES), lambda b, h, i: (b, i, 0)),
            pl.BlockSpec((LANES, wk), lambda b, h, i: (0, h)),
            pl.BlockSpec((1, wk), lambda b, h, i: (0, h)),
            pl.BlockSpec((1, GLA_DV), lambda b, h, i: (0, 0)),
        ],
        out_specs=pl.BlockSpec((1, blk_tokens, wv), lambda b, h, i: (b, i, h)),
        out_shape=jax.ShapeDtypeStruct((B, S, GLA_HEADS * GLA_DV), BF),
        scratch_shapes=[pltpu.VMEM((hb, GLA_DV, GLA_DK), F32)],
        compiler_params=_params(("parallel", "parallel", "arbitrary")),
    )(proj3, proj3, proj3, proj3, small3, w2p, b2, ng)


def _compress_kernel(xk_ref, xv_ref, pek_ref, pev_ref, w1k_ref, w1v_ref, w2k_ref, w2v_ref, kc_ref, vc_ref):
    n_rows = xk_ref.shape[1]
    half = CMP_STRIDE * LANES
    lane = lax.broadcasted_iota(jnp.int32, (n_rows, LANES), 1)
    row = lax.broadcasted_iota(jnp.int32, (n_rows, LANES), 0)
    end_c = CMP_STRIDE * row + (CMP_LEN - 1)
    c_k = _pos_columns(lane, end_c).astype(F32)
    c_v = jnp.where(lane == NSA_DH, 1.0, 0.0).astype(F32)

    def branch(x_ref, pe_ref, w1_ref, w2_ref, const, o_ref):
        x = x_ref[0].astype(F32)
        xa = (x + pe_ref[0:1, :]).astype(BF)
        xb = (x + pe_ref[1:2, :]).astype(BF)
        a = jnp.dot(xa, w1_ref[0:half, :], preferred_element_type=F32)
        b = jnp.dot(xb, w1_ref[half:2 * half, :], preferred_element_type=F32)
        pre = a + pltpu.roll(b, n_rows - 1, 0)
        hid = (pre * jax.nn.sigmoid(pre)).astype(BF)
        for g in range(NSA_GROUPS):
            o_ref[0, g] = (jnp.dot(hid, w2_ref[g], preferred_element_type=F32) + const).astype(BF)

    branch(xk_ref, pek_ref, w1k_ref, w2k_ref, c_k, kc_ref)
    branch(xv_ref, pev_ref, w1v_ref, w2v_ref, c_v, vc_ref)


def _nsa_compress(xk, xv, pek, pev, w1k, w1v, w2k, w2v):
    B, n_rows, width = xk.shape
    full = lambda shape: _resident(shape, lambda b: (0,) * len(shape))
    out = jax.ShapeDtypeStruct((B, NSA_GROUPS, n_rows, LANES), BF)
    return pl.pallas_call(
        _compress_kernel,
        grid=(B,),
        in_specs=[
            pl.BlockSpec((1, n_rows, width), lambda b: (b, 0, 0)),
            pl.BlockSpec((1, n_rows, width), lambda b: (b, 0, 0)),
            full(pek.shape), full(pev.shape), full(w1k.shape), full(w1v.shape),
            full(w2k.shape), full(w2v.shape),
        ],
        out_specs=[pl.BlockSpec((1, NSA_GROUPS, n_rows, LANES), lambda b: (b, 0, 0, 0))] * 2,
        out_shape=[out, out],
        compiler_params=_params(("parallel",)),
    )(xk, xv, pek, pev, w1k, w1v, w2k, w2v)


def _nsa_kernel(q_ref, ks_ref, vs_ref, kw_ref, vw_ref, kc_ref, vc_ref, gt_ref, sl_ref, ov_ref,
                o_ref, ksa, kwa, vst, vwt, vct, acc_scr, pw_scr, act_ref, *, seq):
    qi = pl.program_id(2)
    n_slc = seq // SLC_LEN
    R = NSA_REP
    KT = SLC_TILE
    WK = WINDOW + QB
    n_cmp = kc_ref.shape[2]

    def t_bf(x):
        return x.astype(F32).T[0:VROWS].astype(BF)

    @pl.when(qi == 0)
    def _():
        grp = pl.program_id(1)
        lane = lax.broadcasted_iota(jnp.int32, (seq, LANES), 1)
        pos = lax.broadcasted_iota(jnp.int32, (seq, LANES), 0)
        blk = pos >> 6
        off = pos & 63
        r_i = lax.broadcasted_iota(jnp.int32, (LANES, LANES), 0)
        c_i = lax.broadcasted_iota(jnp.int32, (LANES, LANES), 1)
        pick = jnp.where((c_i < NSA_DH) & (r_i == c_i + grp * NSA_DH), 1.0, 0.0).astype(BF)
        c_s = jnp.where(lane == LANES - 1, off,
                        jnp.where((lane >= NSA_DH) & (lane - (NSA_DH - 1) == blk), 1, 0))
        ksa[...] = (jnp.dot(ks_ref[0], pick, preferred_element_type=F32) + c_s.astype(F32)).astype(BF)
        c_w = _pos_columns(lane, pos)
        kwa[0:WINDOW, :] = jnp.zeros((WINDOW, LANES), BF)
        kwa[WINDOW:WINDOW + seq, :] = (jnp.dot(kw_ref[0], pick, preferred_element_type=F32)
                                       + c_w.astype(F32)).astype(BF)
        ones_rows = jnp.where(lax.broadcasted_iota(jnp.int32, (VROWS - NSA_DH, LANES), 0) == 0, 1.0, 0.0)
        n_pad = WINDOW // LANES
        for c in range(n_pad):
            vwt[c] = jnp.zeros((VROWS, LANES), BF)

        def v_tile(x):
            xt = x.astype(F32).T
            dims = jnp.where(grp == 0, xt[0:NSA_DH], xt[NSA_DH:2 * NSA_DH])
            return jnp.concatenate([dims, ones_rows], axis=0).astype(BF)

        def fill(c, carry):
            rows = pl.ds(pl.multiple_of(c * LANES, LANES), LANES)
            vst[c] = v_tile(vs_ref[0, rows, :])
            vwt[c + n_pad] = v_tile(vw_ref[0, rows, :])
            return carry

        lax.fori_loop(0, seq // LANES, fill, 0)
        for c in range(n_cmp // LANES):
            vct[:, c * LANES:(c + 1) * LANES] = t_bf(vc_ref[0, 0, c * LANES:(c + 1) * LANES, :])

    q0 = qi * QB
    rowi = lax.broadcasted_iota(jnp.int32, (NSA_DH, QB), 0)
    slopes = [jnp.concatenate([sl_ref[0, r:r + 1, :]] * (QB // LANES), axis=1) for r in range(R)]
    q_all = q_ref[0].astype(F32).T
    q_t = [q_all[r * NSA_DH:(r + 1) * NSA_DH] for r in range(R)]

    qw = jnp.concatenate(
        [jnp.concatenate([q_t[r] * LOG2E, _slope_rows(rowi, slopes[r] * LOG2E)], axis=0).astype(BF)
         for r in range(R)], axis=1)

    def tile4(x):
        return jnp.concatenate([x] * R, axis=1)

    ovt = ov_ref[...]

    def cmp_branch(rows):
        def fn():
            kw = kwa[pl.ds(pl.multiple_of(q0, QB), WK), :]
            s_w = jnp.dot(kw, qw, preferred_element_type=F32)
            c_w = lax.broadcasted_iota(jnp.int32, (WK, QB), 0)
            d_w = c_w - lax.broadcasted_iota(jnp.int32, (WK, QB), 1)
            mask_w = (d_w > 0) & (d_w <= WINDOW) & (c_w >= WINDOW - q0)
            s_w = s_w + tile4(jnp.where(mask_w, 0.0, NEG))
            m_w = jnp.max(s_w, axis=0, keepdims=True)
            pw_scr[...] = jnp.exp2(s_w - m_w).astype(BF)

            s_c = jnp.dot(kc_ref[0, 0, 0:rows, :], qw, preferred_element_type=F32)
            e_c = CMP_STRIDE * lax.broadcasted_iota(jnp.int32, (rows, QB), 0) + (CMP_LEN - 1)
            t_c = q0 + lax.broadcasted_iota(jnp.int32, (rows, QB), 1)
            s_c = s_c + tile4(jnp.where(e_c <= t_c, 0.0, NEG))
            m_c = jnp.maximum(jnp.max(s_c, axis=0, keepdims=True), 0.1 * NEG)
            p_c = jnp.exp2(s_c - m_c)
            l_c = jnp.sum(p_c, axis=0, keepdims=True)
            p_c = p_c * jnp.where(l_c > 0.0, 1.0 / l_c, 0.0)
            o_cmp = jnp.dot(vct[:, 0:rows], p_c.astype(BF), preferred_element_type=F32)
            psum = p_c[:, 0:QB]
            for r in range(1, R):
                psum = psum + p_c[:, r * QB:(r + 1) * QB]
            p_hi = psum.astype(BF)
            rem = psum - p_hi.astype(F32)
            p_mid = rem.astype(BF)
            p_lo = (rem - p_mid.astype(F32)).astype(BF)
            ov = ovt[:, 0:rows]
            return o_cmp, (jnp.dot(ov, p_hi, preferred_element_type=F32)
                           + jnp.dot(ov, p_mid, preferred_element_type=F32)
                           + jnp.dot(ov, p_lo, preferred_element_type=F32))
        return fn

    half = n_cmp // 2
    o_c, imp = lax.cond(q0 + QB <= CMP_STRIDE * half + CMP_LEN - 1, cmp_branch(half), cmp_branch(n_cmp))

    NR = ov_ref.shape[0]
    SUB = 8
    jblk = lax.broadcasted_iota(jnp.int32, (NR, QB), 0)
    t_q = q0 + lax.broadcasted_iota(jnp.int32, (NR, QB), 1)
    cur = t_q >> 6
    forced = (jblk == 0) | (jblk == cur) | (jblk == cur - 1)
    score = jnp.where(jblk > cur, NEG, jnp.where(forced, -NEG, imp))
    n_slab = -(-n_slc // SUB)
    isub = lax.broadcasted_iota(jnp.int32, (SUB, QB), 0)

    def rank_counts(ns):
        def fn():
            slabs = [score[a * SUB:(a + 1) * SUB, :] for a in range(ns)]
            cnts = [jnp.zeros((SUB, QB), F32) for _ in range(ns)]
            for jp in range(min(ns * SUB, n_slc)):
                rowv = jnp.broadcast_to(score[jp:jp + 1, :], (SUB, QB))
                for a in range(ns):
                    if a < jp // SUB:
                        beats = jnp.where(rowv > slabs[a], 1.0, 0.0)
                    elif a > jp // SUB:
                        beats = jnp.where(rowv >= slabs[a], 1.0, 0.0)
                    else:
                        beats = jnp.where(isub > jp % SUB, jnp.where(rowv >= slabs[a], 1.0, 0.0),
                                          jnp.where(rowv > slabs[a], 1.0, 0.0))
                    cnts[a] = cnts[a] + beats
            rest = NR - ns * SUB
            return jnp.concatenate(cnts + [jnp.full((rest, QB), float(NR), F32)] * (rest > 0), axis=0)
        return fn

    last_blk = (q0 + QB - 1) >> 6
    cnt = lax.switch(last_blk // SUB, [rank_counts(ns) for ns in range(1, n_slab + 1)])
    sel = (cnt < float(N_SEL)) & (jblk <= cur) & (jblk < n_slc)
    a_nat = jnp.where(jblk == 0, 1.0, jnp.where(sel, (SLC_LEN * jblk).astype(F32), NEG))
    a_nat = jnp.where(jblk < n_slc, a_nat, 0.0)
    a_t = pltpu.roll(a_nat, NR - 1, 0)

    any_q = jnp.max(jnp.where(sel, 1.0, 0.0), axis=1, keepdims=True)
    jcol = lax.broadcasted_iota(jnp.int32, (NR, 1), 0)
    bits = jnp.where(any_q > 0.0, lax.shift_left(jnp.int32(1), jcol & 31), 0)
    word0 = jnp.sum(jnp.where(jcol < 32, bits, 0))
    word1 = jnp.sum(jnp.where(jcol >= 32, bits, 0))

    qs = jnp.concatenate([jnp.concatenate([q_t[r], a_t * slopes[r]], axis=0).astype(BF) for r in range(R)],
                         axis=1)

    tiles_per = KT // LANES
    blocks_per = KT // SLC_LEN
    n_full = q0 // KT

    def scan(ti, n):
        word = jnp.where(ti < 32 // blocks_per, word0, word1)
        hit = (lax.shift_right_logical(word, (ti * blocks_per) & 31) & ((1 << blocks_per) - 1)) != 0
        act_ref[n] = ti
        return n + jnp.where(hit, 1, 0)

    n_act = lax.fori_loop(0, n_full, scan, 0)

    def scores(ti):
        k0 = pl.multiple_of(ti * KT, KT)
        return jnp.dot(ksa[pl.ds(k0, KT), :], qs, preferred_element_type=F32)

    def values_t(ti):
        return [vst[ti * tiles_per + c] for c in range(tiles_per)]

    def absorb(s, v_tiles, m_prev):
        m_new = jnp.maximum(m_prev, jnp.max(s, axis=0, keepdims=True))
        alpha = jnp.exp(m_prev - m_new)
        p = jnp.exp(s - m_new).astype(BF)
        acc_scr[...] = acc_scr[...] * alpha + jnp.dot(jnp.concatenate(v_tiles, axis=1), p,
                                                     preferred_element_type=F32)
        return m_new

    acc_scr[...] = jnp.zeros(acc_scr.shape, F32)

    def chains(tiles, diag_flags, m_run):
        s_all = []
        for t, is_diag in zip(tiles, diag_flags):
            s = scores(t)
            if is_diag:
                p_d = t * KT + lax.broadcasted_iota(jnp.int32, (KT, QB), 0)
                t_d = q0 + lax.broadcasted_iota(jnp.int32, (KT, QB), 1)
                s = s + tile4(jnp.where(p_d <= t_d, 0.0, NEG))
            s_all.append(s)
        for t, s in zip(tiles, s_all):
            m_run = absorb(s, values_t(t), m_run)
        return m_run

    n_trip = n_act // TRIP_TILES
    m_q = lax.fori_loop(0, n_trip, lambda j, m: chains([act_ref[TRIP_TILES * j + i] for i in range(TRIP_TILES)],
                                                       [False] * TRIP_TILES, m),
                        jnp.full((1, R * QB), NEG, F32))
    n_diag = QB // KT

    def finish(rest):
        def fn(m_run):
            tiles = [act_ref[TRIP_TILES * n_trip + i] for i in range(rest)] + [n_full + d for d in range(n_diag)]
            chains(tiles, [False] * rest + [True] * n_diag, m_run)
            acc_s = acc_scr[...]
            vw_t = jnp.concatenate([vwt[qi * (QB // LANES) + c] for c in range(WK // LANES)], axis=1)
            acc_w = jnp.dot(vw_t, pw_scr[...], preferred_element_type=F32)
            sg_t = jax.nn.sigmoid(gt_ref[0]).T
            outs = []
            for r in range(R):
                cs = slice(r * QB, (r + 1) * QB)
                a_s = acc_s[:, cs]
                a_w = acc_w[:, cs]
                g_c = sg_t[3 * r:3 * r + 1, :]
                g_s = sg_t[3 * r + 1:3 * r + 2, :] / a_s[NSA_DH:NSA_DH + 1, :]
                g_w = sg_t[3 * r + 2:3 * r + 3, :] / a_w[NSA_DH:NSA_DH + 1, :]
                outs.append((g_c * o_c[:, cs] + g_s * a_s + g_w * a_w)[0:NSA_DH])
            o_ref[0] = jnp.concatenate(outs, axis=0).T.astype(BF)
        return fn

    lax.switch(n_act - TRIP_TILES * n_trip, [finish(rest) for rest in range(TRIP_TILES)], m_q)


def _nsa_attend(proj3, small3, kc, vc, slope_tab, ovt):
    B, S, _ = proj3.shape
    G, R = NSA_GROUPS, NSA_REP
    n_cmp = kc.shape[2]
    kern = functools.partial(_nsa_kernel, seq=S)
    kv_spec = lambda off: pl.BlockSpec((1, S, LANES), lambda b, g, i: (b, 0, off // LANES))
    return pl.pallas_call(
        kern,
        grid=(B, G, S // QB),
        in_specs=[
            pl.BlockSpec((1, QB, R * NSA_DH), lambda b, g, i: (b, i, OFF_NQ // (R * NSA_DH) + g)),
            kv_spec(OFF_KS), kv_spec(OFF_VS), kv_spec(OFF_KW), kv_spec(OFF_VW),
            pl.BlockSpec((1, 1, n_cmp, LANES), lambda b, g, i: (b, g, 0, 0)),
            pl.BlockSpec((1, 1, n_cmp, LANES), lambda b, g, i: (b, g, 0, 0)),
            pl.BlockSpec((1, QB, LANES), lambda b, g, i: (b, i, 1 + g)),
            pl.BlockSpec((1, 8, LANES), lambda b, g, i: (g, 0, 0)),
            _resident(ovt.shape, lambda b, g, i: (0, 0)),
        ],
        out_specs=pl.BlockSpec((1, QB, R * NSA_DH), lambda b, g, i: (b, i, g)),
        out_shape=jax.ShapeDtypeStruct((B, S, G * R * NSA_DH), BF),
        scratch_shapes=[
            pltpu.VMEM((S, LANES), BF),
            pltpu.VMEM((S + WINDOW, LANES), BF),
            pltpu.VMEM((S // LANES, VROWS, LANES), BF),
            pltpu.VMEM(((S + WINDOW) // LANES, VROWS, LANES), BF),
            pltpu.VMEM((VROWS, n_cmp), BF),
            pltpu.VMEM((VROWS, R * QB), F32),
            pltpu.VMEM((WINDOW + QB, R * QB), BF),
            pltpu.SMEM((S // SLC_TILE + 1,), jnp.int32),
        ],
        compiler_params=_params(("parallel", "parallel", "arbitrary")),
    )(proj3, proj3, proj3, proj3, proj3, kc, vc, small3, slope_tab, ovt)


def _merge_kernel(og_ref, on_ref, mg_ref, mn_ref, x_ref, wg_ref, wn_ref, wo_ref, g_ref, o_ref):
    a = jnp.dot(og_ref[...], wg_ref[...], preferred_element_type=F32)
    b = jnp.dot(on_ref[...], wn_ref[...], preferred_element_type=F32)
    mixed = jax.nn.sigmoid(mg_ref[...].astype(F32)) * a + jax.nn.sigmoid(mn_ref[...].astype(F32)) * b
    y = jnp.dot(mixed.astype(BF), wo_ref[...], preferred_element_type=F32)
    y = y * lax.rsqrt(jnp.mean(y * y, axis=-1, keepdims=True) + EPS) * g_ref[...]
    o_ref[...] = x_ref[...] + y


def _merge_out(o_gla2, o_nsa2, proj2, x2, wg, wn, wo, g, tm=512):
    n_tok = x2.shape[0]
    D = D_MODEL
    return pl.pallas_call(
        _merge_kernel,
        grid=(n_tok // tm,),
        in_specs=[
            pl.BlockSpec((tm, D), lambda i: (i, 0)),
            pl.BlockSpec((tm, o_nsa2.shape[1]), lambda i: (i, 0)),
            pl.BlockSpec((tm, D), lambda i: (i, OFF_MG // D)),
            pl.BlockSpec((tm, D), lambda i: (i, OFF_MN // D)),
            pl.BlockSpec((tm, D), lambda i: (i, 0)),
            _resident(wg.shape, lambda i: (0, 0)),
            _resident(wn.shape, lambda i: (0, 0)),
            _resident(wo.shape, lambda i: (0, 0)),
            _resident((1, D), lambda i: (0, 0)),
        ],
        out_specs=pl.BlockSpec((tm, D), lambda i: (i, 0)),
        out_shape=jax.ShapeDtypeStruct((n_tok, D), F32),
        compiler_params=_params(("parallel",)),
    )(o_gla2, o_nsa2, proj2, proj2, x2, wg, wn, wo, g)


def _ffn_kernel(x_ref, gpre_ref, wg_ref, wu_ref, wd_ref, gpost_ref, o_ref, acc_scr, *, chunk):
    x = x_ref[...]
    h = (x * lax.rsqrt(jnp.mean(x * x, axis=-1, keepdims=True) + EPS) * gpre_ref[...]).astype(BF)
    d_ff = wg_ref.shape[1]
    for n, c0 in enumerate(range(0, d_ff, chunk)):
        c1 = min(c0 + chunk, d_ff)
        a = jnp.dot(h, wg_ref[:, c0:c1], preferred_element_type=F32)
        u = jnp.dot(h, wu_ref[:, c0:c1], preferred_element_type=F32)
        t = (a * jax.nn.sigmoid(a) * u).astype(BF)
        part = jnp.dot(t, wd_ref[c0:c1, :], preferred_element_type=F32)
        if n == 0:
            acc_scr[...] = part
        else:
            acc_scr[...] += part
    f = acc_scr[...]
    o_ref[...] = x + f * lax.rsqrt(jnp.mean(f * f, axis=-1, keepdims=True) + EPS) * gpost_ref[...]


def _ffn(x2, gpre, wg, wu, wd, gpost, tm=512, chunk=512):
    n_tok = x2.shape[0]
    D = D_MODEL
    kern = functools.partial(_ffn_kernel, chunk=chunk)
    return pl.pallas_call(
        kern,
        grid=(n_tok // tm,),
        in_specs=[
            pl.BlockSpec((tm, D), lambda i: (i, 0)),
            _resident((1, D), lambda i: (0, 0)),
            _resident(wg.shape, lambda i: (0, 0)),
            _resident(wu.shape, lambda i: (0, 0)),
            _resident(wd.shape, lambda i: (0, 0)),
            _resident((1, D), lambda i: (0, 0)),
        ],
        out_specs=pl.BlockSpec((tm, D), lambda i: (i, 0)),
        out_shape=jax.ShapeDtypeStruct((n_tok, D), F32),
        scratch_shapes=[pltpu.VMEM((tm, D), F32)],
        compiler_params=_params(("parallel",)),
    )(x2, gpre, wg, wu, wd, gpost)


def _prep_compress(pe, w1, w2):
    eye = jnp.eye(NSA_GROUPS, dtype=F32)
    w1r = w1.reshape(CMP_LEN, NSA_DH, NSA_DH)
    w1e = jnp.einsum('lde,gh->lgdhe', w1r, eye).reshape(CMP_LEN * NSA_GROUPS * NSA_DH, NSA_GROUPS * NSA_DH)
    pe_e = jnp.broadcast_to(pe[:, None, :], (CMP_LEN, NSA_GROUPS, NSA_DH)).reshape(2, CMP_STRIDE * LANES)
    pe_e = jnp.pad(pe_e, ((0, 6), (0, 0)))
    w2e = jnp.stack([
        jnp.pad(jnp.pad(w2, ((g * NSA_DH, (NSA_GROUPS - 1 - g) * NSA_DH), (0, 0))), ((0, 0), (0, LANES - NSA_DH)))
        for g in range(NSA_GROUPS)])
    return pe_e.astype(F32), w1e.astype(BF), w2e.astype(BF)


def _overlap_table(seq):
    n_cmp = (seq - CMP_LEN) // CMP_STRIDE + 1
    n_slc = seq // SLC_LEN
    sc = CMP_STRIDE * np.arange(n_cmp)
    ss = SLC_LEN * np.arange(n_slc)
    ov = np.clip(np.minimum(sc[:, None] + CMP_LEN, ss[None, :] + SLC_LEN)
                 - np.maximum(sc[:, None], ss[None, :]), 0, None).astype(np.float32) / CMP_LEN
    ovt = np.zeros((NSA_DH, n_cmp + 1), np.float32)
    ovt[:n_slc, :n_cmp] = ov.T
    return jnp.asarray(ovt, dtype=BF)


def kernel(x, norm_mix_pre, norm_mix_post, norm_ffn_pre, norm_ffn_post, w_in, gla_w_alpha2, gla_b_alpha, gla_norm_g, nsa_cmp_pe_k, nsa_cmp_w1_k, nsa_cmp_w2_k, nsa_cmp_pe_v, nsa_cmp_w1_v, nsa_cmp_w2_v, w_proj_gla, w_proj_nsa, w_out, w_ffn_gate, w_ffn_up, w_ffn_down):
    B, S, D = x.shape
    depth = w_in.shape[0]
    n_tok = B * S
    h_idx = jnp.arange(NSA_HEADS, dtype=F32)
    slopes = jnp.exp2(-8.0 * (h_idx + 1.0) / NSA_HEADS).reshape(NSA_GROUPS, NSA_REP, 1)
    slope_tab = jnp.broadcast_to(jnp.pad(slopes, ((0, 0), (0, 8 - NSA_REP), (0, 0))), (NSA_GROUPS, 8, LANES))
    ovt = _overlap_table(S)
    x2 = x.reshape(n_tok, D)
    for l in range(depth):
        w_raw = w_in[l].astype(BF)
        w_tail = jnp.pad(w_raw[:, TAIL_START:], ((0, 0), (0, TAIL_START + LANES - IN_WIDTH)))
        proj2, small2 = _in_proj(x2, norm_mix_pre[l][None, :], w_raw, w_tail)
        proj3 = proj2.reshape(B, S, N_MAIN)
        small3 = small2.reshape(B, S, N_SMALL)

        w2_hi = gla_w_alpha2[l].astype(BF)
        w2_lo = (gla_w_alpha2[l] - w2_hi.astype(F32)).astype(BF)
        w2p = jnp.pad(jnp.concatenate([w2_hi, w2_hi, w2_lo], axis=0), ((0, LANES - 3 * GLA_RANK), (0, 0)))
        o_gla = _gla(proj3, small3, w2p, gla_b_alpha[l][None, :], gla_norm_g[l][None, :])

        xk = proj3[:, :, OFF_KC:OFF_KC + LANES].reshape(B, S // CMP_STRIDE, CMP_STRIDE * LANES)
        xv = proj3[:, :, OFF_VC:OFF_VC + LANES].reshape(B, S // CMP_STRIDE, CMP_STRIDE * LANES)
        pek, w1k, w2k = _prep_compress(nsa_cmp_pe_k[l], nsa_cmp_w1_k[l], nsa_cmp_w2_k[l])
        pev, w1v, w2v = _prep_compress(nsa_cmp_pe_v[l], nsa_cmp_w1_v[l], nsa_cmp_w2_v[l])
        kc, vc = _nsa_compress(xk, xv, pek, pev, w1k, w1v, w2k, w2v)
        o_nsa = _nsa_attend(proj3, small3, kc, vc, slope_tab, ovt)

        x2 = _merge_out(o_gla.reshape(n_tok, -1), o_nsa.reshape(n_tok, -1), proj2, x2,
                        w_proj_gla[l].astype(BF), w_proj_nsa[l].astype(BF),
                        w_out[l].astype(BF), norm_mix_post[l][None, :])
        x2 = _ffn(x2, norm_ffn_pre[l][None, :], w_ffn_gate[l].astype(BF), w_ffn_up[l].astype(BF),
                  w_ffn_down[l].astype(BF), norm_ffn_post[l][None, :])
    return x2.reshape(B, S, D)
```

```python
import functools

import numpy as np
import jax
import jax.numpy as jnp
from jax import lax
from jax.experimental import pallas as pl
from jax.experimental.pallas import tpu as pltpu

D_MODEL = 1024
GLA_HEADS = 4
GLA_DK = 128
GLA_DV = 256
GLA_RANK = 16
GLA_TAU = 16.0
GLA_CHUNK = 64
NSA_HEADS = 8
NSA_GROUPS = 2
NSA_REP = 4
NSA_DH = 64
CMP_LEN = 32
CMP_STRIDE = 16
SLC_LEN = 64
N_SEL = 16
WINDOW = 512
QB = 256
SLC_TILE = 128
TRIP_TILES = 6
VROWS = 80
D_FF = 2816
EPS = 1e-6
NEG = -1e30

LANES = 128
VMEM_LIMIT = 56 * 1024 * 1024
BF = jnp.bfloat16
F32 = jnp.float32

OFF_GQ = 0
OFF_GK = 512
OFF_GV = 1024
OFF_GR = 2048
OFF_MG = 3072
OFF_MN = 4096
OFF_NQ = 5120
OFF_KC = 5632
OFF_VC = 5760
OFF_KS = 5888
OFF_VS = 6016
OFF_KW = 6144
OFF_VW = 6272
N_MAIN = 6400
N_SMALL = 128
GATE_LANE0 = 3 * GLA_RANK

NT = (((1,), (1,)), ((), ()))
TN = (((0,), (0,)), ((), ()))


LOG2E = 1.4426950408889634
POS_TERMS = 3


def _pos_columns(lane, pos):
    first = lane - NSA_DH
    return jnp.where((first >= 0) & (first < POS_TERMS), pos >> 6,
                     jnp.where((first >= POS_TERMS) & (first < 2 * POS_TERMS), pos & 63, 0))


def _slope_rows(rowi, coef):
    terms = []
    rem = coef
    for _ in range(POS_TERMS):
        t = rem.astype(BF).astype(F32)
        terms.append(t)
        rem = rem - t
    out = jnp.zeros(rowi.shape, F32)
    for i, t in enumerate(terms):
        out = jnp.where(rowi == i, t * float(SLC_LEN), jnp.where(rowi == POS_TERMS + i, t, out))
    return out


def _resident(shape, index_map):
    return pl.BlockSpec(shape, index_map, pipeline_mode=pl.Buffered(1))


def _params(sem):
    return pltpu.CompilerParams(dimension_semantics=sem, vmem_limit_bytes=VMEM_LIMIT)


SRC_GA = 3072
SRC_NQ = 3088
SRC_KV = 3600
SRC_GATE = 4368
SRC_MERGE = 4392
IN_WIDTH = 6440
TAIL_START = IN_WIDTH // LANES * LANES
ALIGNED = OFF_MG
SHIFTED_RUNS = ((SRC_MERGE, 2048, OFF_MG, 1.0), (SRC_NQ, 512, OFF_NQ, NSA_DH ** -0.5), (SRC_KV, 768, OFF_KC, 1.0))


def _in_proj_kernel(x_ref, g_ref, wr_ref, wt_ref, om_ref, os_ref, wm_scr, ws_scr):
    @pl.when(pl.program_id(0) == 0)
    def _():
        r_i = lax.broadcasted_iota(jnp.int32, (2 * LANES, LANES), 0)
        c_i = lax.broadcasted_iota(jnp.int32, (2 * LANES, LANES), 1)
        for src, width, dst, scale in SHIFTED_RUNS:
            shift = jnp.where(r_i == c_i + src % LANES, 1.0, 0.0).astype(BF)
            for j in range(width // LANES):
                b0 = (src // LANES + j) * LANES
                if b0 + 2 * LANES <= TAIL_START:
                    pair = wr_ref[:, b0:b0 + 2 * LANES]
                else:
                    pair = jnp.concatenate([wr_ref[:, b0:b0 + LANES], wt_ref[...]], axis=1)
                moved = jnp.dot(pair, shift, preferred_element_type=F32)
                d0 = dst - ALIGNED + j * LANES
                wm_scr[:, d0:d0 + LANES] = (moved * scale).astype(BF)
        r1 = lax.broadcasted_iota(jnp.int32, (LANES, LANES), 0)
        c1 = lax.broadcasted_iota(jnp.int32, (LANES, LANES), 1)
        rep3 = jnp.where((r1 < GLA_RANK) & (c1 < GATE_LANE0) & ((c1 % GLA_RANK) == r1), 1.0, 0.0).astype(BF)
        g_blk = SRC_GATE // LANES * LANES
        n_gate = NSA_HEADS * 3
        pick = jnp.where((c1 >= GATE_LANE0) & (c1 < GATE_LANE0 + n_gate)
                         & (r1 == c1 - GATE_LANE0 + SRC_GATE - g_blk), 1.0, 0.0).astype(BF)
        ws_scr[...] = (jnp.dot(wr_ref[:, SRC_GA:SRC_GA + LANES], rep3, preferred_element_type=F32)
                       + jnp.dot(wr_ref[:, g_blk:g_blk + LANES], pick, preferred_element_type=F32)).astype(BF)

    x = x_ref[...]
    h = (x * lax.rsqrt(jnp.mean(x * x, axis=-1, keepdims=True) + EPS) * g_ref[...]).astype(BF)
    os_ref[...] = jnp.dot(h, ws_scr[...], preferred_element_type=F32)
    step = 512
    for c0 in range(0, ALIGNED, step):
        om_ref[:, c0:c0 + step] = jnp.dot(h, wr_ref[:, c0:c0 + step], preferred_element_type=F32).astype(BF)
    for c0 in range(0, N_MAIN - ALIGNED, step):
        c1_ = min(c0 + step, N_MAIN - ALIGNED)
        om_ref[:, ALIGNED + c0:ALIGNED + c1_] = jnp.dot(h, wm_scr[:, c0:c1_],
                                                        preferred_element_type=F32).astype(BF)


def _in_proj(x2, g, w_raw, w_tail, tm=512):
    n_tok = x2.shape[0]
    return pl.pallas_call(
        _in_proj_kernel,
        grid=(n_tok // tm,),
        in_specs=[
            pl.BlockSpec((tm, D_MODEL), lambda i: (i, 0)),
            _resident((1, D_MODEL), lambda i: (0, 0)),
            _resident((D_MODEL, IN_WIDTH), lambda i: (0, 0)),
            _resident((D_MODEL, LANES), lambda i: (0, 0)),
        ],
        out_specs=[
            pl.BlockSpec((tm, N_MAIN), lambda i: (i, 0)),
            pl.BlockSpec((tm, N_SMALL), lambda i: (i, 0)),
        ],
        out_shape=[
            jax.ShapeDtypeStruct((n_tok, N_MAIN), BF),
            jax.ShapeDtypeStruct((n_tok, N_SMALL), F32),
        ],
        scratch_shapes=[pltpu.VMEM((D_MODEL, N_MAIN - ALIGNED), BF), pltpu.VMEM((D_MODEL, N_SMALL), BF)],
        compiler_params=_params(("arbitrary",)),
    )(x2, g, w_raw, w_tail)


def _gla_kernel(q_ref, k_ref, v_ref, r_ref, a_ref, w2_ref, b2_ref, ng_ref, o_ref, st_scr, *, n_chunks, n_heads):
    blk = pl.program_id(2)

    @pl.when(blk == 0)
    def _():
        st_scr[...] = jnp.zeros_like(st_scr)

    C = GLA_CHUNK
    T = n_chunks * C
    W = n_heads * GLA_DK

    def split3(x):
        hi = x.astype(BF)
        rem = x - hi.astype(F32)
        mid = rem.astype(BF)
        return hi, mid, (rem - mid.astype(F32)).astype(BF)

    a = a_ref[0]
    a_hi = a.astype(BF)
    a_lo = (a - a_hi.astype(F32)).astype(BF)
    lane = lax.broadcasted_iota(jnp.int32, (T, LANES), 1)
    in_lo = (lane >= GLA_RANK) & (lane < 2 * GLA_RANK)
    z = jnp.dot(jnp.where(in_lo, a_lo, a_hi), w2_ref[...], preferred_element_type=F32) + b2_ref[...]
    log_a = (jnp.minimum(z, 0.0) - jnp.log(1.0 + jnp.exp(-jnp.abs(z)))) * (1.0 / GLA_TAU)

    x_wide = jnp.concatenate([log_a[c * C:(c + 1) * C] for c in range(n_chunks)], axis=1)
    x3 = jnp.concatenate(split3(x_wide), axis=0)
    r3 = lax.broadcasted_iota(jnp.int32, (C, 3 * C), 0)
    c3 = lax.broadcasted_iota(jnp.int32, (C, 3 * C), 1) & (C - 1)
    tri3 = jnp.where(c3 <= r3, 1.0, 0.0).astype(BF)
    b_wide = jnp.dot(tri3, x3, preferred_element_type=F32)
    bcum = jnp.concatenate([b_wide[:, c * W:(c + 1) * W] for c in range(n_chunks)], axis=0)
    last_rows = [b_wide[C - 1:C, c * W:(c + 1) * W] for c in range(n_chunks)]
    b_last = jnp.concatenate([jnp.broadcast_to(lr, (C, W)) for lr in last_rows], axis=0)
    decay = [jnp.exp(lr) for lr in last_rows]

    q = q_ref[0].astype(F32)
    k = k_ref[0].astype(F32)
    v = v_ref[0]
    qe = (q * ((GLA_DK ** -0.5) * jnp.exp(bcum))).astype(BF)
    ke = (k * jnp.exp(-bcum)).astype(BF)
    kd = (k * jnp.exp(b_last - bcum)).astype(BF)

    H = min(T, 4 * C)
    row = lax.broadcasted_iota(jnp.int32, (H, H), 0)
    col = lax.broadcasted_iota(jnp.int32, (H, H), 1)
    keep = (col <= row) & ((col >> 6) == (row >> 6))
    ng = ng_ref[...]
    r_all = r_ref[0].astype(F32)
    for hh in range(n_heads):
        ks_ = slice(hh * GLA_DK, (hh + 1) * GLA_DK)
        vs_ = slice(hh * GLA_DV, (hh + 1) * GLA_DV)
        intra = []
        for h0 in range(0, T, H):
            hs = slice(h0, h0 + H)
            attn = lax.dot_general(qe[hs, ks_], ke[hs, ks_], NT, preferred_element_type=F32)
            intra.append(jnp.dot(jnp.where(keep, attn, 0.0).astype(BF), v[hs, vs_], preferred_element_type=F32))
        o = jnp.concatenate(intra, axis=0)

        st = st_scr[hh]
        inter = []
        for c in range(n_chunks):
            sl = slice(c * C, (c + 1) * C)
            inter.append(lax.dot_general(qe[sl, ks_], st.astype(BF), NT, preferred_element_type=F32))
            upd = lax.dot_general(v[sl, vs_], kd[sl, ks_], TN, preferred_element_type=F32)
            st = st * decay[c][:, ks_] + upd
        st_scr[hh] = st
        o = o + jnp.concatenate(inter, axis=0)

        o = o * lax.rsqrt(jnp.mean(o * o, axis=-1, keepdims=True) + EPS) * ng
        r = r_all[:, vs_]
        o_ref[0, :, vs_] = (o * (r * jax.nn.sigmoid(r))).astype(BF)


def _gla(proj3, small3, w2p, b2, ng, blk_tokens=512, heads_per_step=4):
    B, S, _ = proj3.shape
    nblk = S // blk_tokens
    hb = heads_per_step
    kern = functools.partial(_gla_kernel, n_chunks=blk_tokens // GLA_CHUNK, n_heads=hb)
    wk, wv = hb * GLA_DK, hb * GLA_DV
    return pl.pallas_call(
        kern,
        grid=(B, GLA_HEADS // hb, nblk),
        in_specs=[
            pl.BlockSpec((1, blk_tokens, wk), lambda b, h, i: (b, i, OFF_GQ // wk + h)),
            pl.BlockSpec((1, blk_tokens, wk), lambda b, h, i: (b, i, OFF_GK // wk + h)),
            pl.BlockSpec((1, blk_tokens, wv), lambda b, h, i: (b, i, OFF_GV // wv + h)),
            pl.BlockSpec((1, blk_tokens, wv), lambda b, h, i: (b, i, OFF_GR // wv + h)),
            pl.BlockSpec((1, blk_tokens, LANES), lambda b, h, i: (b, i, 0)),
            pl.BlockSpec((LANES, wk), lambda b, h, i: (0, h)),
            pl.BlockSpec((1, wk), lambda b, h, i: (0, h)),
            pl.BlockSpec((1, GLA_DV), lambda b, h, i: (0, 0)),
        ],
        out_specs=pl.BlockSpec((1, blk_tokens, wv), lambda b, h, i: (b, i, h)),
        out_shape=jax.ShapeDtypeStruct((B, S, GLA_HEADS * GLA_DV), BF),
        scratch_shapes=[pltpu.VMEM((hb, GLA_DV, GLA_DK), F32)],
        compiler_params=_params(("parallel", "parallel", "arbitrary")),
    )(proj3, proj3, proj3, proj3, small3, w2p, b2, ng)


def _compress_kernel(xk_ref, xv_ref, pek_ref, pev_ref, w1k_ref, w1v_ref, w2k_ref, w2v_ref, kc_ref, vc_ref):
    n_rows = xk_ref.shape[1]
    half = CMP_STRIDE * LANES
    lane = lax.broadcasted_iota(jnp.int32, (n_rows, LANES), 1)
    row = lax.broadcasted_iota(jnp.int32, (n_rows, LANES), 0)
    end_c = CMP_STRIDE * row + (CMP_LEN - 1)
    c_k = _pos_columns(lane, end_c).astype(F32)
    c_v = jnp.where(lane == NSA_DH, 1.0, 0.0).astype(F32)

    def branch(x_ref, pe_ref, w1_ref, w2_ref, const, o_ref):
        x = x_ref[0].astype(F32)
        xa = (x + pe_ref[0:1, :]).astype(BF)
        xb = (x + pe_ref[1:2, :]).astype(BF)
        a = jnp.dot(xa, w1_ref[0:half, :], preferred_element_type=F32)
        b = jnp.dot(xb, w1_ref[half:2 * half, :], preferred_element_type=F32)
        pre = a + pltpu.roll(b, n_rows - 1, 0)
        hid = (pre * jax.nn.sigmoid(pre)).astype(BF)
        for g in range(NSA_GROUPS):
            o_ref[0, g] = (jnp.dot(hid, w2_ref[g], preferred_element_type=F32) + const).astype(BF)

    branch(xk_ref, pek_ref, w1k_ref, w2k_ref, c_k, kc_ref)
    branch(xv_ref, pev_ref, w1v_ref, w2v_ref, c_v, vc_ref)


def _nsa_compress(xk, xv, pek, pev, w1k, w1v, w2k, w2v):
    B, n_rows, width = xk.shape
    full = lambda shape: _resident(shape, lambda b: (0,) * len(shape))
    out = jax.ShapeDtypeStruct((B, NSA_GROUPS, n_rows, LANES), BF)
    return pl.pallas_call(
        _compress_kernel,
        grid=(B,),
        in_specs=[
            pl.BlockSpec((1, n_rows, width), lambda b: (b, 0, 0)),
            pl.BlockSpec((1, n_rows, width), lambda b: (b, 0, 0)),
            full(pek.shape), full(pev.shape), full(w1k.shape), full(w1v.shape),
            full(w2k.shape), full(w2v.shape),
        ],
        out_specs=[pl.BlockSpec((1, NSA_GROUPS, n_rows, LANES), lambda b: (b, 0, 0, 0))] * 2,
        out_shape=[out, out],
        compiler_params=_params(("parallel",)),
    )(xk, xv, pek, pev, w1k, w1v, w2k, w2v)


def _nsa_kernel(q_ref, ks_ref, vs_ref, kw_ref, vw_ref, kc_ref, vc_ref, gt_ref, sl_ref, ov_ref,
                o_ref, ksa, kwa, vst, vwt, vct, acc_scr, pw_scr, act_ref, *, seq):
    qi = pl.program_id(2)
    grp = pl.program_id(1)
    n_slc = seq // SLC_LEN
    R = NSA_REP
    KT = SLC_TILE
    WK = WINDOW + QB
    n_cmp = kc_ref.shape[2]

    def t_bf(x):
        return x.astype(F32).T[0:VROWS].astype(BF)

    @pl.when(qi == 0)
    def _():
        lane = lax.broadcasted_iota(jnp.int32, (seq, LANES), 1)
        pos = lax.broadcasted_iota(jnp.int32, (seq, LANES), 0)
        blk = pos >> 6
        off = pos & 63
        r_i = lax.broadcasted_iota(jnp.int32, (LANES, LANES), 0)
        c_i = lax.broadcasted_iota(jnp.int32, (LANES, LANES), 1)
        pick = jnp.where((c_i < NSA_DH) & (r_i == c_i + grp * NSA_DH), 1.0, 0.0).astype(BF)
        c_s = jnp.where(lane == LANES - 1, off,
                        jnp.where((lane >= NSA_DH) & (lane - (NSA_DH - 1) == blk), 1, 0))
        ksa[...] = (jnp.dot(ks_ref[0], pick, preferred_element_type=F32) + c_s.astype(F32)).astype(BF)
        c_w = _pos_columns(lane, pos)
        kwa[0:WINDOW, :] = jnp.zeros((WINDOW, LANES), BF)
        kwa[WINDOW:WINDOW + seq, :] = (jnp.dot(kw_ref[0], pick, preferred_element_type=F32)
                                       + c_w.astype(F32)).astype(BF)
        ones_rows = jnp.where(lax.broadcasted_iota(jnp.int32, (VROWS - NSA_DH, LANES), 0) == 0, 1.0, 0.0)
        n_pad = WINDOW // LANES
        for c in range(n_pad):
            vwt[c] = jnp.zeros((VROWS, LANES), BF)

        def v_tile(x):
            xt = x.astype(F32).T
            dims = jnp.where(grp == 0, xt[0:NSA_DH], xt[NSA_DH:2 * NSA_DH])
            return jnp.concatenate([dims, ones_rows], axis=0).astype(BF)

        def fill(c, carry):
            rows = pl.ds(pl.multiple_of(c * LANES, LANES), LANES)
            vst[c] = v_tile(vs_ref[0, rows, :])
            vwt[c + n_pad] = v_tile(vw_ref[0, rows, :])
            return carry

        lax.fori_loop(0, seq // LANES, fill, 0)
        for c in range(n_cmp // LANES):
            vct[:, c * LANES:(c + 1) * LANES] = t_bf(vc_ref[0, 0, c * LANES:(c + 1) * LANES, :])

    q0 = qi * QB
    rowi = lax.broadcasted_iota(jnp.int32, (NSA_DH, QB), 0)
    slopes = [jnp.concatenate([sl_ref[0, r:r + 1, :]] * (QB // LANES), axis=1) for r in range(R)]
    q_all = q_ref[0].astype(F32).T
    q_t = [q_all[r * NSA_DH:(r + 1) * NSA_DH] for r in range(R)]

    qw = jnp.concatenate(
        [jnp.concatenate([q_t[r] * LOG2E, _slope_rows(rowi, slopes[r] * LOG2E)], axis=0).astype(BF)
         for r in range(R)], axis=1)

    def tile4(x):
        return jnp.concatenate([x] * R, axis=1)

    ovt = ov_ref[...]

    def cmp_branch(rows):
        def fn():
            kw = kwa[pl.ds(pl.multiple_of(q0, QB), WK), :]
            s_w = jnp.dot(kw, qw, preferred_element_type=F32)
            c_w = lax.broadcasted_iota(jnp.int32, (WK, QB), 0)
            d_w = c_w - lax.broadcasted_iota(jnp.int32, (WK, QB), 1)
            mask_w = (d_w > 0) & (d_w <= WINDOW) & (c_w >= WINDOW - q0)
            s_w = s_w + tile4(jnp.where(mask_w, 0.0, NEG))
            m_w = jnp.max(s_w, axis=0, keepdims=True)
            pw_scr[...] = jnp.exp2(s_w - m_w).astype(BF)

            s_c = jnp.dot(kc_ref[0, 0, 0:rows, :], qw, preferred_element_type=F32)
            e_c = CMP_STRIDE * lax.broadcasted_iota(jnp.int32, (rows, QB), 0) + (CMP_LEN - 1)
            t_c = q0 + lax.broadcasted_iota(jnp.int32, (rows, QB), 1)
            s_c = s_c + tile4(jnp.where(e_c <= t_c, 0.0, NEG))
            m_c = jnp.maximum(jnp.max(s_c, axis=0, keepdims=True), 0.1 * NEG)
            p_c = jnp.exp2(s_c - m_c)
            l_c = jnp.sum(p_c, axis=0, keepdims=True)
            p_c = p_c * jnp.where(l_c > 0.0, 1.0 / l_c, 0.0)
            o_cmp = jnp.dot(vct[:, 0:rows], p_c.astype(BF), preferred_element_type=F32)
            psum = p_c[:, 0:QB]
            for r in range(1, R):
                psum = psum + p_c[:, r * QB:(r + 1) * QB]
            p_hi = psum.astype(BF)
            rem = psum - p_hi.astype(F32)
            p_mid = rem.astype(BF)
            p_lo = (rem - p_mid.astype(F32)).astype(BF)
            ov = ovt[:, 0:rows]
            return o_cmp, (jnp.dot(ov, p_hi, preferred_element_type=F32)
                           + jnp.dot(ov, p_mid, preferred_element_type=F32)
                           + jnp.dot(ov, p_lo, preferred_element_type=F32))
        return fn

    half = n_cmp // 2
    o_c, imp = lax.cond(q0 + QB <= CMP_STRIDE * half + CMP_LEN - 1, cmp_branch(half), cmp_branch(n_cmp))

    NR = ov_ref.shape[0]
    SUB = 8
    jblk = lax.broadcasted_iota(jnp.int32, (NR, QB), 0)
    t_q = q0 + lax.broadcasted_iota(jnp.int32, (NR, QB), 1)
    cur = t_q >> 6
    forced = (jblk == 0) | (jblk == cur) | (jblk == cur - 1)
    score = jnp.where(jblk > cur, NEG, jnp.where(forced, -NEG, imp))
    n_slab = -(-n_slc // SUB)
    isub = lax.broadcasted_iota(jnp.int32, (SUB, QB), 0)

    def rank_counts(ns):
        def fn():
            slabs = [score[a * SUB:(a + 1) * SUB, :] for a in range(ns)]
            cnts = [jnp.zeros((SUB, QB), F32) for _ in range(ns)]
            for jp in range(min(ns * SUB, n_slc)):
                rowv = jnp.broadcast_to(score[jp:jp + 1, :], (SUB, QB))
                for a in range(ns):
                    if a < jp // SUB:
                        beats = jnp.where(rowv > slabs[a], 1.0, 0.0)
                    elif a > jp // SUB:
                        beats = jnp.where(rowv >= slabs[a], 1.0, 0.0)
                    else:
                        beats = jnp.where(isub > jp % SUB, jnp.where(rowv >= slabs[a], 1.0, 0.0),
                                          jnp.where(rowv > slabs[a], 1.0, 0.0))
                    cnts[a] = cnts[a] + beats
            rest = NR - ns * SUB
            return jnp.concatenate(cnts + [jnp.full((rest, QB), float(NR), F32)] * (rest > 0), axis=0)
        return fn

    last_blk = (q0 + QB - 1) >> 6
    cnt = lax.switch(last_blk // SUB, [rank_counts(ns) for ns in range(1, n_slab + 1)])
    sel = (cnt < float(N_SEL)) & (jblk <= cur) & (jblk < n_slc)
    a_nat = jnp.where(jblk == 0, 1.0, jnp.where(sel, (SLC_LEN * jblk).astype(F32), NEG))
    a_nat = jnp.where(jblk < n_slc, a_nat, 0.0)
    a_t = pltpu.roll(a_nat, NR - 1, 0)

    any_q = jnp.max(jnp.where(sel, 1.0, 0.0), axis=1, keepdims=True)
    jcol = lax.broadcasted_iota(jnp.int32, (NR, 1), 0)
    bits = jnp.where(any_q > 0.0, lax.shift_left(jnp.int32(1), jcol & 31), 0)
    word0 = jnp.sum(jnp.where(jcol < 32, bits, 0))
    word1 = jnp.sum(jnp.where(jcol >= 32, bits, 0))

    qs = jnp.concatenate([jnp.concatenate([q_t[r], a_t * slopes[r]], axis=0).astype(BF) for r in range(R)],
                         axis=1)

    tiles_per = KT // LANES
    blocks_per = KT // SLC_LEN
    n_full = q0 // KT

    def scan(ti, n):
        word = jnp.where(ti < 32 // blocks_per, word0, word1)
        hit = (lax.shift_right_logical(word, (ti * blocks_per) & 31) & ((1 << blocks_per) - 1)) != 0
        act_ref[n] = ti
        return n + jnp.where(hit, 1, 0)

    n_act = lax.fori_loop(0, n_full, scan, 0)

    def scores(ti):
        k0 = pl.multiple_of(ti * KT, KT)
        return jnp.dot(ksa[pl.ds(k0, KT), :], qs, preferred_element_type=F32)

    def values_t(ti):
        return [vst[ti * tiles_per + c] for c in range(tiles_per)]

    def absorb(s, v_tiles, m_prev):
        m_new = jnp.maximum(m_prev, jnp.max(s, axis=0, keepdims=True))
        alpha = jnp.exp(m_prev - m_new)
        p = jnp.exp(s - m_new).astype(BF)
        acc_scr[...] = acc_scr[...] * alpha + jnp.dot(jnp.concatenate(v_tiles, axis=1), p,
                                                     preferred_element_type=F32)
        return m_new

    acc_scr[...] = jnp.zeros(acc_scr.shape, F32)

    def chains(tiles, diag_flags, m_run):
        s_all = []
        for t, is_diag in zip(tiles, diag_flags):
            s = scores(t)
            if is_diag:
                p_d = t * KT + lax.broadcasted_iota(jnp.int32, (KT, QB), 0)
                t_d = q0 + lax.broadcasted_iota(jnp.int32, (KT, QB), 1)
                s = s + tile4(jnp.where(p_d <= t_d, 0.0, NEG))
            s_all.append(s)
        for t, s in zip(tiles, s_all):
            m_run = absorb(s, values_t(t), m_run)
        return m_run

    n_trip = n_act // TRIP_TILES
    m_q = lax.fori_loop(0, n_trip, lambda j, m: chains([act_ref[TRIP_TILES * j + i] for i in range(TRIP_TILES)],
                                                       [False] * TRIP_TILES, m),
                        jnp.full((1, R * QB), NEG, F32))
    n_diag = QB // KT

    def finish(rest):
        def fn(m_run):
            tiles = [act_ref[TRIP_TILES * n_trip + i] for i in range(rest)] + [n_full + d for d in range(n_diag)]
            chains(tiles, [False] * rest + [True] * n_diag, m_run)
            acc_s = acc_scr[...]
            vw_t = jnp.concatenate([vwt[qi * (QB // LANES) + c] for c in range(WK // LANES)], axis=1)
            acc_w = jnp.dot(vw_t, pw_scr[...], preferred_element_type=F32)
            sg_all = jax.nn.sigmoid(gt_ref[0]).T
            per_g = R * 3
            sg_t = jnp.where(grp == 0, sg_all[GATE_LANE0:GATE_LANE0 + per_g],
                             sg_all[GATE_LANE0 + per_g:GATE_LANE0 + 2 * per_g])
            outs = []
            for r in range(R):
                cs = slice(r * QB, (r + 1) * QB)
                a_s = acc_s[:, cs]
                a_w = acc_w[:, cs]
                g_c = sg_t[3 * r:3 * r + 1, :]
                g_s = sg_t[3 * r + 1:3 * r + 2, :] / a_s[NSA_DH:NSA_DH + 1, :]
                g_w = sg_t[3 * r + 2:3 * r + 3, :] / a_w[NSA_DH:NSA_DH + 1, :]
                outs.append((g_c * o_c[:, cs] + g_s * a_s + g_w * a_w)[0:NSA_DH])
            o_ref[0] = jnp.concatenate(outs, axis=0).T.astype(BF)
        return fn

    lax.switch(n_act - TRIP_TILES * n_trip, [finish(rest) for rest in range(TRIP_TILES)], m_q)


def _nsa_attend(proj3, small3, kc, vc, slope_tab, ovt):
    B, S, _ = proj3.shape
    G, R = NSA_GROUPS, NSA_REP
    n_cmp = kc.shape[2]
    kern = functools.partial(_nsa_kernel, seq=S)
    kv_spec = lambda off: pl.BlockSpec((1, S, LANES), lambda b, g, i: (b, 0, off // LANES))
    return pl.pallas_call(
        kern,
        grid=(B, G, S // QB),
        in_specs=[
            pl.BlockSpec((1, QB, R * NSA_DH), lambda b, g, i: (b, i, OFF_NQ // (R * NSA_DH) + g)),
            kv_spec(OFF_KS), kv_spec(OFF_VS), kv_spec(OFF_KW), kv_spec(OFF_VW),
            pl.BlockSpec((1, 1, n_cmp, LANES), lambda b, g, i: (b, g, 0, 0)),
            pl.BlockSpec((1, 1, n_cmp, LANES), lambda b, g, i: (b, g, 0, 0)),
            pl.BlockSpec((1, QB, LANES), lambda b, g, i: (b, i, 0)),
            pl.BlockSpec((1, 8, LANES), lambda b, g, i: (g, 0, 0)),
            _resident(ovt.shape, lambda b, g, i: (0, 0)),
        ],
        out_specs=pl.BlockSpec((1, QB, R * NSA_DH), lambda b, g, i: (b, i, g)),
        out_shape=jax.ShapeDtypeStruct((B, S, G * R * NSA_DH), BF),
        scratch_shapes=[
            pltpu.VMEM((S, LANES), BF),
            pltpu.VMEM((S + WINDOW, LANES), BF),
            pltpu.VMEM((S // LANES, VROWS, LANES), BF),
            pltpu.VMEM(((S + WINDOW) // LANES, VROWS, LANES), BF),
            pltpu.VMEM((VROWS, n_cmp), BF),
            pltpu.VMEM((VROWS, R * QB), F32),
            pltpu.VMEM((WINDOW + QB, R * QB), BF),
            pltpu.SMEM((S // SLC_TILE + 1,), jnp.int32),
        ],
        compiler_params=_params(("parallel", "parallel", "arbitrary")),
    )(proj3, proj3, proj3, proj3, proj3, kc, vc, small3, slope_tab, ovt)


def _merge_kernel(og_ref, on_ref, mg_ref, mn_ref, x_ref, wg_ref, wn_ref, wo_ref, g_ref, o_ref):
    a = jnp.dot(og_ref[...], wg_ref[...], preferred_element_type=F32)
    b = jnp.dot(on_ref[...], wn_ref[...], preferred_element_type=F32)
    mixed = jax.nn.sigmoid(mg_ref[...].astype(F32)) * a + jax.nn.sigmoid(mn_ref[...].astype(F32)) * b
    y = jnp.dot(mixed.astype(BF), wo_ref[...], preferred_element_type=F32)
    y = y * lax.rsqrt(jnp.mean(y * y, axis=-1, keepdims=True) + EPS) * g_ref[...]
    o_ref[...] = x_ref[...] + y


def _merge_out(o_gla2, o_nsa2, proj2, x2, wg, wn, wo, g, tm=512):
    n_tok = x2.shape[0]
    D = D_MODEL
    return pl.pallas_call(
        _merge_kernel,
        grid=(n_tok // tm,),
        in_specs=[
            pl.BlockSpec((tm, D), lambda i: (i, 0)),
            pl.BlockSpec((tm, o_nsa2.shape[1]), lambda i: (i, 0)),
            pl.BlockSpec((tm, D), lambda i: (i, OFF_MG // D)),
            pl.BlockSpec((tm, D), lambda i: (i, OFF_MN // D)),
            pl.BlockSpec((tm, D), lambda i: (i, 0)),
            _resident(wg.shape, lambda i: (0, 0)),
            _resident(wn.shape, lambda i: (0, 0)),
            _resident(wo.shape, lambda i: (0, 0)),
            _resident((1, D), lambda i: (0, 0)),
        ],
        out_specs=pl.BlockSpec((tm, D), lambda i: (i, 0)),
        out_shape=jax.ShapeDtypeStruct((n_tok, D), F32),
        compiler_params=_params(("parallel",)),
    )(o_gla2, o_nsa2, proj2, proj2, x2, wg, wn, wo, g)


def _ffn_kernel(x_ref, gpre_ref, wg_ref, wu_ref, wd_ref, gpost_ref, o_ref, acc_scr, *, chunk):
    x = x_ref[...]
    h = (x * lax.rsqrt(jnp.mean(x * x, axis=-1, keepdims=True) + EPS) * gpre_ref[...]).astype(BF)
    d_ff = wg_ref.shape[1]
    for n, c0 in enumerate(range(0, d_ff, chunk)):
        c1 = min(c0 + chunk, d_ff)
        a = jnp.dot(h, wg_ref[:, c0:c1], preferred_element_type=F32)
        u = jnp.dot(h, wu_ref[:, c0:c1], preferred_element_type=F32)
        t = (a * jax.nn.sigmoid(a) * u).astype(BF)
        part = jnp.dot(t, wd_ref[c0:c1, :], preferred_element_type=F32)
        if n == 0:
            acc_scr[...] = part
        else:
            acc_scr[...] += part
    f = acc_scr[...]
    o_ref[...] = x + f * lax.rsqrt(jnp.mean(f * f, axis=-1, keepdims=True) + EPS) * gpost_ref[...]


def _ffn(x2, gpre, wg, wu, wd, gpost, tm=512, chunk=512):
    n_tok = x2.shape[0]
    D = D_MODEL
    kern = functools.partial(_ffn_kernel, chunk=chunk)
    return pl.pallas_call(
        kern,
        grid=(n_tok // tm,),
        in_specs=[
            pl.BlockSpec((tm, D), lambda i: (i, 0)),
            _resident((1, D), lambda i: (0, 0)),
            _resident(wg.shape, lambda i: (0, 0)),
            _resident(wu.shape, lambda i: (0, 0)),
            _resident(wd.shape, lambda i: (0, 0)),
            _resident((1, D), lambda i: (0, 0)),
        ],
        out_specs=pl.BlockSpec((tm, D), lambda i: (i, 0)),
        out_shape=jax.ShapeDtypeStruct((n_tok, D), F32),
        scratch_shapes=[pltpu.VMEM((tm, D), F32)],
        compiler_params=_params(("parallel",)),
    )(x2, gpre, wg, wu, wd, gpost)


def _prep_compress(pe, w1, w2):
    eye = jnp.eye(NSA_GROUPS, dtype=F32)
    w1r = w1.reshape(CMP_LEN, NSA_DH, NSA_DH)
    w1e = jnp.einsum('lde,gh->lgdhe', w1r, eye).reshape(CMP_LEN * NSA_GROUPS * NSA_DH, NSA_GROUPS * NSA_DH)
    pe_e = jnp.broadcast_to(pe[:, None, :], (CMP_LEN, NSA_GROUPS, NSA_DH)).reshape(2, CMP_STRIDE * LANES)
    pe_e = jnp.pad(pe_e, ((0, 6), (0, 0)))
    w2e = jnp.stack([
        jnp.pad(jnp.pad(w2, ((g * NSA_DH, (NSA_GROUPS - 1 - g) * NSA_DH), (0, 0))), ((0, 0), (0, LANES - NSA_DH)))
        for g in range(NSA_GROUPS)])
    return pe_e.astype(F32), w1e.astype(BF), w2e.astype(BF)


def _overlap_table(seq):
    n_cmp = (seq - CMP_LEN) // CMP_STRIDE + 1
    n_slc = seq // SLC_LEN
    sc = CMP_STRIDE * np.arange(n_cmp)
    ss = SLC_LEN * np.arange(n_slc)
    ov = np.clip(np.minimum(sc[:, None] + CMP_LEN, ss[None, :] + SLC_LEN)
                 - np.maximum(sc[:, None], ss[None, :]), 0, None).astype(np.float32) / CMP_LEN
    ovt = np.zeros((NSA_DH, n_cmp + 1), np.float32)
    ovt[:n_slc, :n_cmp] = ov.T
    return jnp.asarray(ovt, dtype=BF)


def kernel(x, norm_mix_pre, norm_mix_post, norm_ffn_pre, norm_ffn_post, w_in, gla_w_alpha2, gla_b_alpha, gla_norm_g, nsa_cmp_pe_k, nsa_cmp_w1_k, nsa_cmp_w2_k, nsa_cmp_pe_v, nsa_cmp_w1_v, nsa_cmp_w2_v, w_proj_gla, w_proj_nsa, w_out, w_ffn_gate, w_ffn_up, w_ffn_down):
    B, S, D = x.shape
    depth = w_in.shape[0]
    n_tok = B * S
    h_idx = jnp.arange(NSA_HEADS, dtype=F32)
    slopes = jnp.exp2(-8.0 * (h_idx + 1.0) / NSA_HEADS).reshape(NSA_GROUPS, NSA_REP, 1)
    slope_tab = jnp.broadcast_to(jnp.pad(slopes, ((0, 0), (0, 8 - NSA_REP), (0, 0))), (NSA_GROUPS, 8, LANES))
    ovt = _overlap_table(S)
    x2 = x.reshape(n_tok, D)
    for l in range(depth):
        w_raw = w_in[l].astype(BF)
        w_tail = jnp.pad(w_raw[:, TAIL_START:], ((0, 0), (0, TAIL_START + LANES - IN_WIDTH)))
        proj2, small2 = _in_proj(x2, norm_mix_pre[l][None, :], w_raw, w_tail)
        proj3 = proj2.reshape(B, S, N_MAIN)
        small3 = small2.reshape(B, S, N_SMALL)

        w2_hi = gla_w_alpha2[l].astype(BF)
        w2_lo = (gla_w_alpha2[l] - w2_hi.astype(F32)).astype(BF)
        w2p = jnp.pad(jnp.concatenate([w2_hi, w2_hi, w2_lo], axis=0), ((0, LANES - 3 * GLA_RANK), (0, 0)))
        o_gla = _gla(proj3, small3, w2p, gla_b_alpha[l][None, :], gla_norm_g[l][None, :])

        xk = proj3[:, :, OFF_KC:OFF_KC + LANES].reshape(B, S // CMP_STRIDE, CMP_STRIDE * LANES)
        xv = proj3[:, :, OFF_VC:OFF_VC + LANES].reshape(B, S // CMP_STRIDE, CMP_STRIDE * LANES)
        pek, w1k, w2k = _prep_compress(nsa_cmp_pe_k[l], nsa_cmp_w1_k[l], nsa_cmp_w2_k[l])
        pev, w1v, w2v = _prep_compress(nsa_cmp_pe_v[l], nsa_cmp_w1_v[l], nsa_cmp_w2_v[l])
        kc, vc = _nsa_compress(xk, xv, pek, pev, w1k, w1v, w2k, w2v)
        o_nsa = _nsa_attend(proj3, small3, kc, vc, slope_tab, ovt)

        x2 = _merge_out(o_gla.reshape(n_tok, -1), o_nsa.reshape(n_tok, -1), proj2, x2,
                        w_proj_gla[l].astype(BF), w_proj_nsa[l].astype(BF),
                        w_out[l].astype(BF), norm_mix_post[l][None, :])
        x2 = _ffn(x2, norm_ffn_pre[l][None, :], w_ffn_gate[l].astype(BF), w_ffn_up[l].astype(BF),
                  w_ffn_down[l].astype(BF), norm_ffn_post[l][None, :])
    return x2.reshape(B, S, D)
```

```python
import functools

import numpy as np
import jax
import jax.numpy as jnp
from jax import lax
from jax.experimental import pallas as pl
from jax.experimental.pallas import tpu as pltpu

D_MODEL = 1024
GLA_HEADS = 4
GLA_DK = 128
GLA_DV = 256
GLA_RANK = 16
GLA_TAU = 16.0
GLA_CHUNK = 64
NSA_HEADS = 8
NSA_GROUPS = 2
NSA_REP = 4
NSA_DH = 64
CMP_LEN = 32
CMP_STRIDE = 16
SLC_LEN = 64
N_SEL = 16
WINDOW = 512
QB = 256
SLC_TILE = 128
TRIP_TILES = 6
VROWS = 80
D_FF = 2816
EPS = 1e-6
NEG = -1e30

LANES = 128
VMEM_LIMIT = 56 * 1024 * 1024
BF = jnp.bfloat16
F32 = jnp.float32

OFF_GQ = 0
OFF_GK = 512
OFF_GV = 1024
OFF_GR = 2048
OFF_MG = 3072
OFF_MN = 4096
OFF_NQ = 5120
OFF_KC = 5632
OFF_VC = 5760
OFF_KS = 5888
OFF_VS = 6016
OFF_KW = 6144
OFF_VW = 6272
N_MAIN = 6400
N_SMALL = 128
GATE_LANE0 = 3 * GLA_RANK

NT = (((1,), (1,)), ((), ()))
TN = (((0,), (0,)), ((), ()))


LOG2E = 1.4426950408889634
POS_TERMS = 3


def _pos_columns(lane, pos):
    first = lane - NSA_DH
    return jnp.where((first >= 0) & (first < POS_TERMS), pos >> 6,
                     jnp.where((first >= POS_TERMS) & (first < 2 * POS_TERMS), pos & 63, 0))


def _slope_rows(rowi, coef):
    terms = []
    rem = coef
    for _ in range(POS_TERMS):
        t = rem.astype(BF).astype(F32)
        terms.append(t)
        rem = rem - t
    out = jnp.zeros(rowi.shape, F32)
    for i, t in enumerate(terms):
        out = jnp.where(rowi == i, t * float(SLC_LEN), jnp.where(rowi == POS_TERMS + i, t, out))
    return out


def _resident(shape, index_map):
    return pl.BlockSpec(shape, index_map, pipeline_mode=pl.Buffered(1))


def _params(sem):
    return pltpu.CompilerParams(dimension_semantics=sem, vmem_limit_bytes=VMEM_LIMIT)


SRC_GA = 3072
SRC_NQ = 3088
SRC_KV = 3600
SRC_GATE = 4368
SRC_MERGE = 4392
IN_WIDTH = 6440
TAIL_START = IN_WIDTH // LANES * LANES
ALIGNED = OFF_MG
SHIFTED_RUNS = ((SRC_MERGE, 2048, OFF_MG, 1.0), (SRC_NQ, 512, OFF_NQ, NSA_DH ** -0.5), (SRC_KV, 768, OFF_KC, 1.0))


def _in_proj_kernel(x_ref, g_ref, wr_ref, wt_ref, om_ref, os_ref, wm_scr, ws_scr):
    @pl.when(pl.program_id(0) == 0)
    def _():
        r_i = lax.broadcasted_iota(jnp.int32, (2 * LANES, LANES), 0)
        c_i = lax.broadcasted_iota(jnp.int32, (2 * LANES, LANES), 1)
        for src, width, dst, scale in SHIFTED_RUNS:
            shift = jnp.where(r_i == c_i + src % LANES, 1.0, 0.0).astype(BF)
            for j in range(width // LANES):
                b0 = (src // LANES + j) * LANES
                if b0 + 2 * LANES <= TAIL_START:
                    pair = wr_ref[:, b0:b0 + 2 * LANES]
                else:
                    pair = jnp.concatenate([wr_ref[:, b0:b0 + LANES], wt_ref[...]], axis=1)
                moved = jnp.dot(pair, shift, preferred_element_type=F32)
                d0 = dst - ALIGNED + j * LANES
                wm_scr[:, d0:d0 + LANES] = (moved * scale).astype(BF)
        r1 = lax.broadcasted_iota(jnp.int32, (LANES, LANES), 0)
        c1 = lax.broadcasted_iota(jnp.int32, (LANES, LANES), 1)
        rep3 = jnp.where((r1 < GLA_RANK) & (c1 < GATE_LANE0) & ((c1 % GLA_RANK) == r1), 1.0, 0.0).astype(BF)
        g_blk = SRC_GATE // LANES * LANES
        n_gate = NSA_HEADS * 3
        pick = jnp.where((c1 >= GATE_LANE0) & (c1 < GATE_LANE0 + n_gate)
                         & (r1 == c1 - GATE_LANE0 + SRC_GATE - g_blk), 1.0, 0.0).astype(BF)
        ws_scr[...] = (jnp.dot(wr_ref[:, SRC_GA:SRC_GA + LANES], rep3, preferred_element_type=F32)
                       + jnp.dot(wr_ref[:, g_blk:g_blk + LANES], pick, preferred_element_type=F32)).astype(BF)

    x = x_ref[...]
    h = (x * lax.rsqrt(jnp.mean(x * x, axis=-1, keepdims=True) + EPS) * g_ref[...]).astype(BF)
    os_ref[...] = jnp.dot(h, ws_scr[...], preferred_element_type=F32)
    step = 512
    for c0 in range(0, ALIGNED, step):
        om_ref[:, c0:c0 + step] = jnp.dot(h, wr_ref[:, c0:c0 + step], preferred_element_type=F32).astype(BF)
    for c0 in range(0, N_MAIN - ALIGNED, step):
        c1_ = min(c0 + step, N_MAIN - ALIGNED)
        om_ref[:, ALIGNED + c0:ALIGNED + c1_] = jnp.dot(h, wm_scr[:, c0:c1_],
                                                        preferred_element_type=F32).astype(BF)


def _in_proj(x2, g, w_raw, w_tail, tm=512):
    n_tok = x2.shape[0]
    return pl.pallas_call(
        _in_proj_kernel,
        grid=(n_tok // tm,),
        in_specs=[
            pl.BlockSpec((tm, D_MODEL), lambda i: (i, 0)),
            _resident((1, D_MODEL), lambda i: (0, 0)),
            _resident((D_MODEL, IN_WIDTH), lambda i: (0, 0)),
            _resident((D_MODEL, LANES), lambda i: (0, 0)),
        ],
        out_specs=[
            pl.BlockSpec((tm, N_MAIN), lambda i: (i, 0)),
            pl.BlockSpec((tm, N_SMALL), lambda i: (i, 0)),
        ],
        out_shape=[
            jax.ShapeDtypeStruct((n_tok, N_MAIN), BF),
            jax.ShapeDtypeStruct((n_tok, N_SMALL), F32),
        ],
        scratch_shapes=[pltpu.VMEM((D_MODEL, N_MAIN - ALIGNED), BF), pltpu.VMEM((D_MODEL, N_SMALL), BF)],
        compiler_params=_params(("arbitrary",)),
    )(x2, g, w_raw, w_tail)


def _gla_kernel(q_ref, k_ref, v_ref, r_ref, a_ref, w2_ref, b2_ref, ng_ref, o_ref, st_scr, *, n_chunks, n_heads):
    blk = pl.program_id(2)

    @pl.when(blk == 0)
    def _():
        st_scr[...] = jnp.zeros_like(st_scr)

    C = GLA_CHUNK
    T = n_chunks * C
    W = n_heads * GLA_DK

    def split3(x):
        hi = x.astype(BF)
        rem = x - hi.astype(F32)
        mid = rem.astype(BF)
        return hi, mid, (rem - mid.astype(F32)).astype(BF)

    a = a_ref[0]
    a_hi = a.astype(BF)
    a_lo = (a - a_hi.astype(F32)).astype(BF)
    lane = lax.broadcasted_iota(jnp.int32, (T, LANES), 1)
    in_lo = (lane >= GLA_RANK) & (lane < 2 * GLA_RANK)
    z = jnp.dot(jnp.where(in_lo, a_lo, a_hi), w2_ref[...], preferred_element_type=F32) + b2_ref[...]
    log_a = (jnp.minimum(z, 0.0) - jnp.log(1.0 + jnp.exp(-jnp.abs(z)))) * (1.0 / GLA_TAU)

    x_wide = jnp.concatenate([log_a[c * C:(c + 1) * C] for c in range(n_chunks)], axis=1)
    x3 = jnp.concatenate(split3(x_wide), axis=0)
    r3 = lax.broadcasted_iota(jnp.int32, (C, 3 * C), 0)
    c3 = lax.broadcasted_iota(jnp.int32, (C, 3 * C), 1) & (C - 1)
    tri3 = jnp.where(c3 <= r3, 1.0, 0.0).astype(BF)
    b_wide = jnp.dot(tri3, x3, preferred_element_type=F32)
    bcum = jnp.concatenate([b_wide[:, c * W:(c + 1) * W] for c in range(n_chunks)], axis=0)
    last_rows = [b_wide[C - 1:C, c * W:(c + 1) * W] for c in range(n_chunks)]
    b_last = jnp.concatenate([jnp.broadcast_to(lr, (C, W)) for lr in last_rows], axis=0)
    decay = [jnp.exp(lr) for lr in last_rows]

    q = q_ref[0].astype(F32)
    k = k_ref[0].astype(F32)
    v = v_ref[0]
    qe = (q * ((GLA_DK ** -0.5) * jnp.exp(bcum))).astype(BF)
    ke = (k * jnp.exp(-bcum)).astype(BF)
    kd = (k * jnp.exp(b_last - bcum)).astype(BF)

    H = min(T, 4 * C)
    row = lax.broadcasted_iota(jnp.int32, (H, H), 0)
    col = lax.broadcasted_iota(jnp.int32, (H, H), 1)
    keep = (col <= row) & ((col >> 6) == (row >> 6))
    ng = ng_ref[...]
    r_all = r_ref[0].astype(F32)
    for hh in range(n_heads):
        ks_ = slice(hh * GLA_DK, (hh + 1) * GLA_DK)
        vs_ = slice(hh * GLA_DV, (hh + 1) * GLA_DV)
        intra = []
        for h0 in range(0, T, H):
            hs = slice(h0, h0 + H)
            attn = lax.dot_general(qe[hs, ks_], ke[hs, ks_], NT, preferred_element_type=F32)
            intra.append(jnp.dot(jnp.where(keep, attn, 0.0).astype(BF), v[hs, vs_], preferred_element_type=F32))
        o = jnp.concatenate(intra, axis=0)

        st = st_scr[hh]
        inter = []
        for c in range(n_chunks):
            sl = slice(c * C, (c + 1) * C)
            inter.append(lax.dot_general(qe[sl, ks_], st.astype(BF), NT, preferred_element_type=F32))
            upd = lax.dot_general(v[sl, vs_], kd[sl, ks_], TN, preferred_element_type=F32)
            st = st * decay[c][:, ks_] + upd
        st_scr[hh] = st
        o = o + jnp.concatenate(inter, axis=0)

        o = o * lax.rsqrt(jnp.mean(o * o, axis=-1, keepdims=True) + EPS) * ng
        r = r_all[:, vs_]
        o_ref[0, :, vs_] = (o * (r * jax.nn.sigmoid(r))).astype(BF)


def _gla(proj3, small3, w2p, b2, ng, blk_tokens=512, heads_per_step=4):
    B, S, _ = proj3.shape
    nblk = S // blk_tokens
    hb = heads_per_step
    kern = functools.partial(_gla_kernel, n_chunks=blk_tokens // GLA_CHUNK, n_heads=hb)
    wk, wv = hb * GLA_DK, hb * GLA_DV
    return pl.pallas_call(
        kern,
        grid=(B, GLA_HEADS // hb, nblk),
        in_specs=[
            pl.BlockSpec((1, blk_tokens, wk), lambda b, h, i: (b, i, OFF_GQ // wk + h)),
            pl.BlockSpec((1, blk_tokens, wk), lambda b, h, i: (b, i, OFF_GK // wk + h)),
            pl.BlockSpec((1, blk_tokens, wv), lambda b, h, i: (b, i, OFF_GV // wv + h)),
            pl.BlockSpec((1, blk_tokens, wv), lambda b, h, i: (b, i, OFF_GR // wv + h)),
            pl.BlockSpec((1, blk_tokens, LANES), lambda b, h, i: (b, i, 0)),
            pl.BlockSpec((LANES, wk), lambda b, h, i: (0, h)),
            pl.BlockSpec((1, wk), lambda b, h, i: (0, h)),
            pl.BlockSpec((1, GLA_DV), lambda b, h, i: (0, 0)),
        ],
        out_specs=pl.BlockSpec((1, blk_tokens, wv), lambda b, h, i: (b, i, h)),
        out_shape=jax.ShapeDtypeStruct((B, S, GLA_HEADS * GLA_DV), BF),
        scratch_shapes=[pltpu.VMEM((hb, GLA_DV, GLA_DK), F32)],
        compiler_params=_params(("parallel", "parallel", "arbitrary")),
    )(proj3, proj3, proj3, proj3, small3, w2p, b2, ng)


def _compress_kernel(xk_ref, xv_ref, pek_ref, pev_ref, w1k_ref, w1v_ref, w2k_ref, w2v_ref, kc_ref, vc_ref):
    n_rows = xk_ref.shape[1]
    half = CMP_STRIDE * LANES
    lane = lax.broadcasted_iota(jnp.int32, (n_rows, LANES), 1)
    row = lax.broadcasted_iota(jnp.int32, (n_rows, LANES), 0)
    end_c = CMP_STRIDE * row + (CMP_LEN - 1)
    c_k = _pos_columns(lane, end_c).astype(F32)
    c_v = jnp.where(lane == NSA_DH, 1.0, 0.0).astype(F32)

    def branch(x_ref, pe_ref, w1_ref, w2_ref, const, o_ref):
        x = x_ref[0].astype(F32)
        xa = (x + pe_ref[0:1, :]).astype(BF)
        xb = (x + pe_ref[1:2, :]).astype(BF)
        a = jnp.dot(xa, w1_ref[0:half, :], preferred_element_type=F32)
        b = jnp.dot(xb, w1_ref[half:2 * half, :], preferred_element_type=F32)
        pre = a + pltpu.roll(b, n_rows - 1, 0)
        hid = (pre * jax.nn.sigmoid(pre)).astype(BF)
        for g in range(NSA_GROUPS):
            o_ref[0, g] = (jnp.dot(hid, w2_ref[g], preferred_element_type=F32) + const).astype(BF)

    branch(xk_ref, pek_ref, w1k_ref, w2k_ref, c_k, kc_ref)
    branch(xv_ref, pev_ref, w1v_ref, w2v_ref, c_v, vc_ref)


def _nsa_compress(xk, xv, pek, pev, w1k, w1v, w2k, w2v):
    B, n_rows, width = xk.shape
    full = lambda shape: _resident(shape, lambda b: (0,) * len(shape))
    out = jax.ShapeDtypeStruct((B, NSA_GROUPS, n_rows, LANES), BF)
    return pl.pallas_call(
        _compress_kernel,
        grid=(B,),
        in_specs=[
            pl.BlockSpec((1, n_rows, width), lambda b: (b, 0, 0)),
            pl.BlockSpec((1, n_rows, width), lambda b: (b, 0, 0)),
            full(pek.shape), full(pev.shape), full(w1k.shape), full(w1v.shape),
            full(w2k.shape), full(w2v.shape),
        ],
        out_specs=[pl.BlockSpec((1, NSA_GROUPS, n_rows, LANES), lambda b: (b, 0, 0, 0))] * 2,
        out_shape=[out, out],
        compiler_params=_params(("parallel",)),
    )(xk, xv, pek, pev, w1k, w1v, w2k, w2v)


def _nsa_kernel(q_ref, ks_ref, vs_ref, kw_ref, vw_ref, kc_ref, vc_ref, gt_ref, sl_ref, ov_ref,
                o_ref, ksa, kwa, vst, vwt, vct, acc_scr, pw_scr, act_ref, *, seq):
    qi = pl.program_id(2)
    grp = pl.program_id(1)
    n_slc = seq // SLC_LEN
    R = NSA_REP
    KT = SLC_TILE
    WK = WINDOW + QB
    n_cmp = kc_ref.shape[2]

    def t_bf(x):
        return x.astype(F32).T[0:VROWS].astype(BF)

    @pl.when(qi == 0)
    def _():
        lane = lax.broadcasted_iota(jnp.int32, (seq, LANES), 1)
        pos = lax.broadcasted_iota(jnp.int32, (seq, LANES), 0)
        blk = pos >> 6
        off = pos & 63
        r_i = lax.broadcasted_iota(jnp.int32, (LANES, LANES), 0)
        c_i = lax.broadcasted_iota(jnp.int32, (LANES, LANES), 1)
        pick = jnp.where((c_i < NSA_DH) & (r_i == c_i + grp * NSA_DH), 1.0, 0.0).astype(BF)
        c_s = jnp.where(lane == LANES - 1, off,
                        jnp.where((lane >= NSA_DH) & (lane - (NSA_DH - 1) == blk), 1, 0))
        ksa[...] = (jnp.dot(ks_ref[0], pick, preferred_element_type=F32) + c_s.astype(F32)).astype(BF)
        c_w = _pos_columns(lane, pos)
        kwa[0:WINDOW, :] = jnp.zeros((WINDOW, LANES), BF)
        kwa[WINDOW:WINDOW + seq, :] = (jnp.dot(kw_ref[0], pick, preferred_element_type=F32)
                                       + c_w.astype(F32)).astype(BF)
        ones_rows = jnp.where(lax.broadcasted_iota(jnp.int32, (VROWS - NSA_DH, LANES), 0) == 0, 1.0, 0.0)
        n_pad = WINDOW // LANES
        for c in range(n_pad):
            vwt[c] = jnp.zeros((VROWS, LANES), BF)

        def v_tile(x):
            xt = x.astype(F32).T
            dims = jnp.where(grp == 0, xt[0:NSA_DH], xt[NSA_DH:2 * NSA_DH])
            return jnp.concatenate([dims, ones_rows], axis=0).astype(BF)

        def fill(c, carry):
            rows = pl.ds(pl.multiple_of(c * LANES, LANES), LANES)
            vst[c] = v_tile(vs_ref[0, rows, :])
            vwt[c + n_pad] = v_tile(vw_ref[0, rows, :])
            return carry

        lax.fori_loop(0, seq // LANES, fill, 0)
        for c in range(n_cmp // LANES):
            vct[:, c * LANES:(c + 1) * LANES] = t_bf(vc_ref[0, 0, c * LANES:(c + 1) * LANES, :])

    q0 = qi * QB
    rowi = lax.broadcasted_iota(jnp.int32, (NSA_DH, QB), 0)
    slopes = [jnp.concatenate([sl_ref[0, r:r + 1, :]] * (QB // LANES), axis=1) for r in range(R)]
    q_all = q_ref[0].astype(F32).T
    q_t = [q_all[r * NSA_DH:(r + 1) * NSA_DH] for r in range(R)]

    qw = jnp.concatenate(
        [jnp.concatenate([q_t[r] * LOG2E, _slope_rows(rowi, slopes[r] * LOG2E)], axis=0).astype(BF)
         for r in range(R)], axis=1)

    def tile4(x):
        return jnp.concatenate([x] * R, axis=1)

    ovt = ov_ref[...]

    def cmp_branch(rows):
        def fn():
            kw = kwa[pl.ds(pl.multiple_of(q0, QB), WK), :]
            s_w = jnp.dot(kw, qw, preferred_element_type=F32)
            c_w = lax.broadcasted_iota(jnp.int32, (WK, QB), 0)
            d_w = c_w - lax.broadcasted_iota(jnp.int32, (WK, QB), 1)
            mask_w = (d_w > 0) & (d_w <= WINDOW) & (c_w >= WINDOW - q0)
            s_w = s_w + tile4(jnp.where(mask_w, 0.0, NEG))
            m_w = jnp.max(s_w, axis=0, keepdims=True)
            pw_scr[...] = jnp.exp2(s_w - m_w).astype(BF)

            s_c = jnp.dot(kc_ref[0, 0, 0:rows, :], qw, preferred_element_type=F32)
            e_c = CMP_STRIDE * lax.broadcasted_iota(jnp.int32, (rows, QB), 0) + (CMP_LEN - 1)
            t_c = q0 + lax.broadcasted_iota(jnp.int32, (rows, QB), 1)
            s_c = s_c + tile4(jnp.where(e_c <= t_c, 0.0, NEG))
            m_c = jnp.maximum(jnp.max(s_c, axis=0, keepdims=True), 0.1 * NEG)
            p_c = jnp.exp2(s_c - m_c)
            l_c = jnp.sum(p_c, axis=0, keepdims=True)
            p_c = p_c * jnp.where(l_c > 0.0, 1.0 / l_c, 0.0)
            o_cmp = jnp.dot(vct[:, 0:rows], p_c.astype(BF), preferred_element_type=F32)
            psum = p_c[:, 0:QB]
            for r in range(1, R):
                psum = psum + p_c[:, r * QB:(r + 1) * QB]
            p_hi = psum.astype(BF)
            rem = psum - p_hi.astype(F32)
            p_mid = rem.astype(BF)
            p_lo = (rem - p_mid.astype(F32)).astype(BF)
            ov = ovt[:, 0:rows]
            return o_cmp, (jnp.dot(ov, p_hi, preferred_element_type=F32)
                           + jnp.dot(ov, p_mid, preferred_element_type=F32)
                           + jnp.dot(ov, p_lo, preferred_element_type=F32))
        return fn

    half = n_cmp // 2
    o_c, imp = lax.cond(q0 + QB <= CMP_STRIDE * half + CMP_LEN - 1, cmp_branch(half), cmp_branch(n_cmp))

    NR = ov_ref.shape[0]
    SUB = 8
    jblk = lax.broadcasted_iota(jnp.int32, (NR, QB), 0)
    t_q = q0 + lax.broadcasted_iota(jnp.int32, (NR, QB), 1)
    cur = t_q >> 6
    forced = (jblk == 0) | (jblk == cur) | (jblk == cur - 1)
    score = jnp.where(jblk > cur, NEG, jnp.where(forced, -NEG, imp))
    n_slab = -(-n_slc // SUB)
    isub = lax.broadcasted_iota(jnp.int32, (SUB, QB), 0)

    def rank_counts(ns):
        def fn():
            slabs = [score[a * SUB:(a + 1) * SUB, :] for a in range(ns)]
            cnts = [jnp.zeros((SUB, QB), F32) for _ in range(ns)]
            for jp in range(min(ns * SUB, n_slc)):
                rowv = jnp.broadcast_to(score[jp:jp + 1, :], (SUB, QB))
                for a in range(ns):
                    if a < jp // SUB:
                        beats = jnp.where(rowv > slabs[a], 1.0, 0.0)
                    elif a > jp // SUB:
                        beats = jnp.where(rowv >= slabs[a], 1.0, 0.0)
                    else:
                        beats = jnp.where(isub > jp % SUB, jnp.where(rowv >= slabs[a], 1.0, 0.0),
                                          jnp.where(rowv > slabs[a], 1.0, 0.0))
                    cnts[a] = cnts[a] + beats
            rest = NR - ns * SUB
            return jnp.concatenate(cnts + [jnp.full((rest, QB), float(NR), F32)] * (rest > 0), axis=0)
        return fn

    last_blk = (q0 + QB - 1) >> 6
    cnt = lax.switch(last_blk // SUB, [rank_counts(ns) for ns in range(1, n_slab + 1)])
    sel = (cnt < float(N_SEL)) & (jblk <= cur) & (jblk < n_slc)
    a_nat = jnp.where(jblk == 0, 1.0, jnp.where(sel, (SLC_LEN * jblk).astype(F32), NEG))
    a_nat = jnp.where(jblk < n_slc, a_nat, 0.0)
    a_t = pltpu.roll(a_nat, NR - 1, 0)

    any_q = jnp.max(jnp.where(sel, 1.0, 0.0), axis=1, keepdims=True)
    jcol = lax.broadcasted_iota(jnp.int32, (NR, 1), 0)
    bits = jnp.where(any_q > 0.0, lax.shift_left(jnp.int32(1), jcol & 31), 0)
    word0 = jnp.sum(jnp.where(jcol < 32, bits, 0))
    word1 = jnp.sum(jnp.where(jcol >= 32, bits, 0))

    qs = jnp.concatenate([jnp.concatenate([q_t[r], a_t * slopes[r]], axis=0).astype(BF) for r in range(R)],
                         axis=1)

    tiles_per = KT // LANES
    blocks_per = KT // SLC_LEN
    n_full = q0 // KT

    def scan(ti, n):
        word = jnp.where(ti < 32 // blocks_per, word0, word1)
        hit = (lax.shift_right_logical(word, (ti * blocks_per) & 31) & ((1 << blocks_per) - 1)) != 0
        act_ref[n] = ti
        return n + jnp.where(hit, 1, 0)

    n_act = lax.fori_loop(0, n_full, scan, 0)

    def scores(ti):
        k0 = pl.multiple_of(ti * KT, KT)
        return jnp.dot(ksa[pl.ds(k0, KT), :], qs, preferred_element_type=F32)

    def values_t(ti):
        return [vst[ti * tiles_per + c] for c in range(tiles_per)]

    def absorb(s, v_tiles, m_prev):
        m_new = jnp.maximum(m_prev, jnp.max(s, axis=0, keepdims=True))
        alpha = jnp.exp(m_prev - m_new)
        p = jnp.exp(s - m_new).astype(BF)
        acc_scr[...] = acc_scr[...] * alpha + jnp.dot(jnp.concatenate(v_tiles, axis=1), p,
                                                     preferred_element_type=F32)
        return m_new

    acc_scr[...] = jnp.zeros(acc_scr.shape, F32)

    def chains(tiles, diag_flags, m_run):
        s_all = []
        for t, is_diag in zip(tiles, diag_flags):
            s = scores(t)
            if is_diag:
                p_d = t * KT + lax.broadcasted_iota(jnp.int32, (KT, QB), 0)
                t_d = q0 + lax.broadcasted_iota(jnp.int32, (KT, QB), 1)
                s = s + tile4(jnp.where(p_d <= t_d, 0.0, NEG))
            s_all.append(s)
        for t, s in zip(tiles, s_all):
            m_run = absorb(s, values_t(t), m_run)
        return m_run

    n_trip = n_act // TRIP_TILES
    m_q = lax.fori_loop(0, n_trip, lambda j, m: chains([act_ref[TRIP_TILES * j + i] for i in range(TRIP_TILES)],
                                                       [False] * TRIP_TILES, m),
                        jnp.full((1, R * QB), NEG, F32))
    n_diag = QB // KT

    def finish(rest):
        def fn(m_run):
            tiles = [act_ref[TRIP_TILES * n_trip + i] for i in range(rest)] + [n_full + d for d in range(n_diag)]
            chains(tiles, [False] * rest + [True] * n_diag, m_run)
            acc_s = acc_scr[...]
            vw_t = jnp.concatenate([vwt[qi * (QB // LANES) + c] for c in range(WK // LANES)], axis=1)
            acc_w = jnp.dot(vw_t, pw_scr[...], preferred_element_type=F32)
            sg_all = jax.nn.sigmoid(gt_ref[0]).T
            per_g = R * 3
            sg_t = jnp.where(grp == 0, sg_all[GATE_LANE0:GATE_LANE0 + per_g],
                             sg_all[GATE_LANE0 + per_g:GATE_LANE0 + 2 * per_g])
            outs = []
            for r in range(R):
                cs = slice(r * QB, (r + 1) * QB)
                a_s = acc_s[:, cs]
                a_w = acc_w[:, cs]
                g_c = sg_t[3 * r:3 * r + 1, :]
                g_s = sg_t[3 * r + 1:3 * r + 2, :] / a_s[NSA_DH:NSA_DH + 1, :]
                g_w = sg_t[3 * r + 2:3 * r + 3, :] / a_w[NSA_DH:NSA_DH + 1, :]
                outs.append((g_c * o_c[:, cs] + g_s * a_s + g_w * a_w)[0:NSA_DH])
            o_ref[0] = jnp.concatenate(outs, axis=0).T.astype(BF)
        return fn

    lax.switch(n_act - TRIP_TILES * n_trip, [finish(rest) for rest in range(TRIP_TILES)], m_q)


def _nsa_attend(proj3, small3, kc, vc, slope_tab, ovt):
    B, S, _ = proj3.shape
    G, R = NSA_GROUPS, NSA_REP
    n_cmp = kc.shape[2]
    kern = functools.partial(_nsa_kernel, seq=S)
    kv_spec = lambda off: pl.BlockSpec((1, S, LANES), lambda b, g, i: (b, 0, off // LANES))
    return pl.pallas_call(
        kern,
        grid=(B, G, S // QB),
        in_specs=[
            pl.BlockSpec((1, QB, R * NSA_DH), lambda b, g, i: (b, i, OFF_NQ // (R * NSA_DH) + g)),
            kv_spec(OFF_KS), kv_spec(OFF_VS), kv_spec(OFF_KW), kv_spec(OFF_VW),
            pl.BlockSpec((1, 1, n_cmp, LANES), lambda b, g, i: (b, g, 0, 0)),
            pl.BlockSpec((1, 1, n_cmp, LANES), lambda b, g, i: (b, g, 0, 0)),
            pl.BlockSpec((1, QB, LANES), lambda b, g, i: (b, i, 0)),
            pl.BlockSpec((1, 8, LANES), lambda b, g, i: (g, 0, 0)),
            _resident(ovt.shape, lambda b, g, i: (0, 0)),
        ],
        out_specs=pl.BlockSpec((1, QB, R * NSA_DH), lambda b, g, i: (b, i, g)),
        out_shape=jax.ShapeDtypeStruct((B, S, G * R * NSA_DH), BF),
        scratch_shapes=[
            pltpu.VMEM((S, LANES), BF),
            pltpu.VMEM((S + WINDOW, LANES), BF),
            pltpu.VMEM((S // LANES, VROWS, LANES), BF),
            pltpu.VMEM(((S + WINDOW) // LANES, VROWS, LANES), BF),
            pltpu.VMEM((VROWS, n_cmp), BF),
            pltpu.VMEM((VROWS, R * QB), F32),
            pltpu.VMEM((WINDOW + QB, R * QB), BF),
            pltpu.SMEM((S // SLC_TILE + 1,), jnp.int32),
        ],
        compiler_params=_params(("parallel", "parallel", "arbitrary")),
    )(proj3, proj3, proj3, proj3, proj3, kc, vc, small3, slope_tab, ovt)


def _merge_kernel(og_ref, on_ref, mg_ref, mn_ref, x_ref, wg_ref, wn_ref, wo_ref, g_ref, o_ref):
    a = jnp.dot(og_ref[...], wg_ref[...], preferred_element_type=F32)
    b = jnp.dot(on_ref[...], wn_ref[...], preferred_element_type=F32)
    mixed = jax.nn.sigmoid(mg_ref[...].astype(F32)) * a + jax.nn.sigmoid(mn_ref[...].astype(F32)) * b
    y = jnp.dot(mixed.astype(BF), wo_ref[...], preferred_element_type=F32)
    y = y * lax.rsqrt(jnp.mean(y * y, axis=-1, keepdims=True) + EPS) * g_ref[...]
    o_ref[...] = x_ref[...] + y


def _merge_out(o_gla2, o_nsa2, proj2, x2, wg, wn, wo, g, tm=512):
    n_tok = x2.shape[0]
    D = D_MODEL
    return pl.pallas_call(
        _merge_kernel,
        grid=(n_tok // tm,),
        in_specs=[
            pl.BlockSpec((tm, D), lambda i: (i, 0)),
            pl.BlockSpec((tm, o_nsa2.shape[1]), lambda i: (i, 0)),
            pl.BlockSpec((tm, D), lambda i: (i, OFF_MG // D)),
            pl.BlockSpec((tm, D), lambda i: (i, OFF_MN // D)),
            pl.BlockSpec((tm, D), lambda i: (i, 0)),
            _resident(wg.shape, lambda i: (0, 0)),
            _resident(wn.shape, lambda i: (0, 0)),
            _resident(wo.shape, lambda i: (0, 0)),
            _resident((1, D), lambda i: (0, 0)),
        ],
        out_specs=pl.BlockSpec((tm, D), lambda i: (i, 0)),
        out_shape=jax.ShapeDtypeStruct((n_tok, D), F32),
        compiler_params=_params(("parallel",)),
    )(o_gla2, o_nsa2, proj2, proj2, x2, wg, wn, wo, g)


def _ffn_kernel(x_ref, gpre_ref, wg_ref, wu_ref, wd_ref, gpost_ref, o_ref, wg_s, wu_s, wd_s, acc_scr,
                *, chunk, n_prep):
    i = pl.program_id(0)

    @pl.when(i < n_prep)
    def _():
        rk = wg_ref.shape[0]
        r0 = pl.multiple_of(i * rk, rk)
        wg_s[pl.ds(r0, rk), :] = wg_ref[...].astype(BF)
        wu_s[pl.ds(r0, rk), :] = wu_ref[...].astype(BF)
        rd = wd_ref.shape[0]
        d0 = pl.multiple_of(i * rd, rd)
        wd_s[pl.ds(d0, rd), :] = wd_ref[...].astype(BF)

    @pl.when(i >= n_prep)
    def _():
        x = x_ref[...]
        h = (x * lax.rsqrt(jnp.mean(x * x, axis=-1, keepdims=True) + EPS) * gpre_ref[...]).astype(BF)
        d_ff = wg_s.shape[1]
        for n, c0 in enumerate(range(0, d_ff, chunk)):
            c1 = min(c0 + chunk, d_ff)
            a = jnp.dot(h, wg_s[:, c0:c1], preferred_element_type=F32)
            u = jnp.dot(h, wu_s[:, c0:c1], preferred_element_type=F32)
            t = (a * jax.nn.sigmoid(a) * u).astype(BF)
            part = jnp.dot(t, wd_s[c0:c1, :], preferred_element_type=F32)
            if n == 0:
                acc_scr[...] = part
            else:
                acc_scr[...] += part
        f = acc_scr[...]
        o_ref[...] = x + f * lax.rsqrt(jnp.mean(f * f, axis=-1, keepdims=True) + EPS) * gpost_ref[...]


def _ffn(x2, gpre, wg, wu, wd, gpost, tm=512, chunk=512, n_prep=8):
    n_tok = x2.shape[0]
    D = D_MODEL
    d_ff = wg.shape[1]
    kern = functools.partial(_ffn_kernel, chunk=chunk, n_prep=n_prep)
    tile = lambda i: (jnp.maximum(i - n_prep, 0), 0)
    piece = lambda i: (jnp.minimum(i, n_prep - 1), 0)
    return pl.pallas_call(
        kern,
        grid=(n_prep + n_tok // tm,),
        in_specs=[
            pl.BlockSpec((tm, D), tile),
            _resident((1, D), lambda i: (0, 0)),
            pl.BlockSpec((D // n_prep, d_ff), piece),
            pl.BlockSpec((D // n_prep, d_ff), piece),
            pl.BlockSpec((d_ff // n_prep, D), piece),
            _resident((1, D), lambda i: (0, 0)),
        ],
        out_specs=pl.BlockSpec((tm, D), tile),
        out_shape=jax.ShapeDtypeStruct((n_tok, D), F32),
        scratch_shapes=[pltpu.VMEM((D, d_ff), BF), pltpu.VMEM((D, d_ff), BF), pltpu.VMEM((d_ff, D), BF),
                        pltpu.VMEM((tm, D), F32)],
        compiler_params=_params(("arbitrary",)),
    )(x2, gpre, wg, wu, wd, gpost)


def _prep_compress(pe, w1, w2):
    eye = jnp.eye(NSA_GROUPS, dtype=F32)
    w1r = w1.reshape(CMP_LEN, NSA_DH, NSA_DH)
    w1e = jnp.einsum('lde,gh->lgdhe', w1r, eye).reshape(CMP_LEN * NSA_GROUPS * NSA_DH, NSA_GROUPS * NSA_DH)
    pe_e = jnp.broadcast_to(pe[:, None, :], (CMP_LEN, NSA_GROUPS, NSA_DH)).reshape(2, CMP_STRIDE * LANES)
    pe_e = jnp.pad(pe_e, ((0, 6), (0, 0)))
    w2e = jnp.stack([
        jnp.pad(jnp.pad(w2, ((g * NSA_DH, (NSA_GROUPS - 1 - g) * NSA_DH), (0, 0))), ((0, 0), (0, LANES - NSA_DH)))
        for g in range(NSA_GROUPS)])
    return pe_e.astype(F32), w1e.astype(BF), w2e.astype(BF)


def _overlap_table(seq):
    n_cmp = (seq - CMP_LEN) // CMP_STRIDE + 1
    n_slc = seq // SLC_LEN
    sc = CMP_STRIDE * np.arange(n_cmp)
    ss = SLC_LEN * np.arange(n_slc)
    ov = np.clip(np.minimum(sc[:, None] + CMP_LEN, ss[None, :] + SLC_LEN)
                 - np.maximum(sc[:, None], ss[None, :]), 0, None).astype(np.float32) / CMP_LEN
    ovt = np.zeros((NSA_DH, n_cmp + 1), np.float32)
    ovt[:n_slc, :n_cmp] = ov.T
    return jnp.asarray(ovt, dtype=BF)


def kernel(x, norm_mix_pre, norm_mix_post, norm_ffn_pre, norm_ffn_post, w_in, gla_w_alpha2, gla_b_alpha, gla_norm_g, nsa_cmp_pe_k, nsa_cmp_w1_k, nsa_cmp_w2_k, nsa_cmp_pe_v, nsa_cmp_w1_v, nsa_cmp_w2_v, w_proj_gla, w_proj_nsa, w_out, w_ffn_gate, w_ffn_up, w_ffn_down):
    B, S, D = x.shape
    depth = w_in.shape[0]
    n_tok = B * S
    h_idx = jnp.arange(NSA_HEADS, dtype=F32)
    slopes = jnp.exp2(-8.0 * (h_idx + 1.0) / NSA_HEADS).reshape(NSA_GROUPS, NSA_REP, 1)
    slope_tab = jnp.broadcast_to(jnp.pad(slopes, ((0, 0), (0, 8 - NSA_REP), (0, 0))), (NSA_GROUPS, 8, LANES))
    ovt = _overlap_table(S)
    x2 = x.reshape(n_tok, D)
    for l in range(depth):
        w_raw = w_in[l].astype(BF)
        w_tail = jnp.pad(w_raw[:, TAIL_START:], ((0, 0), (0, TAIL_START + LANES - IN_WIDTH)))
        proj2, small2 = _in_proj(x2, norm_mix_pre[l][None, :], w_raw, w_tail)
        proj3 = proj2.reshape(B, S, N_MAIN)
        small3 = small2.reshape(B, S, N_SMALL)

        w2_hi = gla_w_alpha2[l].astype(BF)
        w2_lo = (gla_w_alpha2[l] - w2_hi.astype(F32)).astype(BF)
        w2p = jnp.pad(jnp.concatenate([w2_hi, w2_hi, w2_lo], axis=0), ((0, LANES - 3 * GLA_RANK), (0, 0)))
        o_gla = _gla(proj3, small3, w2p, gla_b_alpha[l][None, :], gla_norm_g[l][None, :])

        xk = proj3[:, :, OFF_KC:OFF_KC + LANES].reshape(B, S // CMP_STRIDE, CMP_STRIDE * LANES)
        xv = proj3[:, :, OFF_VC:OFF_VC + LANES].reshape(B, S // CMP_STRIDE, CMP_STRIDE * LANES)
        pek, w1k, w2k = _prep_compress(nsa_cmp_pe_k[l], nsa_cmp_w1_k[l], nsa_cmp_w2_k[l])
        pev, w1v, w2v = _prep_compress(nsa_cmp_pe_v[l], nsa_cmp_w1_v[l], nsa_cmp_w2_v[l])
        kc, vc = _nsa_compress(xk, xv, pek, pev, w1k, w1v, w2k, w2v)
        o_nsa = _nsa_attend(proj3, small3, kc, vc, slope_tab, ovt)

        x2 = _merge_out(o_gla.reshape(n_tok, -1), o_nsa.reshape(n_tok, -1), proj2, x2,
                        w_proj_gla[l].astype(BF), w_proj_nsa[l].astype(BF),
                        w_out[l].astype(BF), norm_mix_post[l][None, :])
        x2 = _ffn(x2, norm_ffn_pre[l][None, :], w_ffn_gate[l], w_ffn_up[l], w_ffn_down[l],
                  norm_ffn_post[l][None, :])
    return x2.reshape(B, S, D)
```

```python
import functools

import numpy as np
import jax
import jax.numpy as jnp
from jax import lax
from jax.experimental import pallas as pl
from jax.experimental.pallas import tpu as pltpu

D_MODEL = 1024
GLA_HEADS = 4
GLA_DK = 128
GLA_DV = 256
GLA_RANK = 16
GLA_TAU = 16.0
GLA_CHUNK = 64
NSA_HEADS = 8
NSA_GROUPS = 2
NSA_REP = 4
NSA_DH = 64
CMP_LEN = 32
CMP_STRIDE = 16
SLC_LEN = 64
N_SEL = 16
WINDOW = 512
QB = 256
SLC_TILE = 128
TRIP_TILES = 6
VROWS = 80
D_FF = 2816
EPS = 1e-6
NEG = -1e30

LANES = 128
VMEM_LIMIT = 56 * 1024 * 1024
BF = jnp.bfloat16
F32 = jnp.float32

OFF_GQ = 0
OFF_GK = 512
OFF_GV = 1024
OFF_GR = 2048
OFF_MG = 3072
OFF_MN = 4096
OFF_NQ = 5120
OFF_KC = 5632
OFF_VC = 5760
OFF_KS = 5888
OFF_VS = 6016
OFF_KW = 6144
OFF_VW = 6272
N_MAIN = 6400
N_SMALL = 128
GATE_LANE0 = 3 * GLA_RANK

NT = (((1,), (1,)), ((), ()))
TN = (((0,), (0,)), ((), ()))


LOG2E = 1.4426950408889634
POS_TERMS = 3


def _pos_columns(lane, pos):
    first = lane - NSA_DH
    return jnp.where((first >= 0) & (first < POS_TERMS), pos >> 6,
                     jnp.where((first >= POS_TERMS) & (first < 2 * POS_TERMS), pos & 63, 0))


def _slope_rows(rowi, coef):
    terms = []
    rem = coef
    for _ in range(POS_TERMS):
        t = rem.astype(BF).astype(F32)
        terms.append(t)
        rem = rem - t
    out = jnp.zeros(rowi.shape, F32)
    for i, t in enumerate(terms):
        out = jnp.where(rowi == i, t * float(SLC_LEN), jnp.where(rowi == POS_TERMS + i, t, out))
    return out


def _resident(shape, index_map):
    return pl.BlockSpec(shape, index_map, pipeline_mode=pl.Buffered(1))


def _params(sem):
    return pltpu.CompilerParams(dimension_semantics=sem, vmem_limit_bytes=VMEM_LIMIT)


SRC_GA = 3072
SRC_NQ = 3088
SRC_KV = 3600
SRC_GATE = 4368
SRC_MERGE = 4392
IN_WIDTH = 6440
TAIL_START = IN_WIDTH // LANES * LANES
ALIGNED = OFF_MG
SHIFTED_RUNS = ((SRC_MERGE, 2048, OFF_MG, 1.0), (SRC_NQ, 512, OFF_NQ, NSA_DH ** -0.5), (SRC_KV, 768, OFF_KC, 1.0))


def _in_proj_kernel(x_ref, g_ref, wr_ref, wt_ref, om_ref, os_ref, xk_ref, xv_ref, wm_scr, ws_scr, perm_scr):
    @pl.when(pl.program_id(0) == 0)
    def _():
        r_i = lax.broadcasted_iota(jnp.int32, (2 * LANES, LANES), 0)
        c_i = lax.broadcasted_iota(jnp.int32, (2 * LANES, LANES), 1)
        for src, width, dst, scale in SHIFTED_RUNS:
            shift = jnp.where(r_i == c_i + src % LANES, 1.0, 0.0).astype(BF)
            for j in range(width // LANES):
                b0 = (src // LANES + j) * LANES
                if b0 + 2 * LANES <= TAIL_START:
                    pair = wr_ref[:, b0:b0 + 2 * LANES]
                else:
                    pair = jnp.concatenate([wr_ref[:, b0:b0 + LANES], wt_ref[...]], axis=1)
                moved = jnp.dot(pair, shift, preferred_element_type=F32)
                d0 = dst - ALIGNED + j * LANES
                wm_scr[:, d0:d0 + LANES] = (moved * scale).astype(BF)
        r1 = lax.broadcasted_iota(jnp.int32, (LANES, LANES), 0)
        c1 = lax.broadcasted_iota(jnp.int32, (LANES, LANES), 1)
        rep3 = jnp.where((r1 < GLA_RANK) & (c1 < GATE_LANE0) & ((c1 % GLA_RANK) == r1), 1.0, 0.0).astype(BF)
        g_blk = SRC_GATE // LANES * LANES
        n_gate = NSA_HEADS * 3
        pick = jnp.where((c1 >= GATE_LANE0) & (c1 < GATE_LANE0 + n_gate)
                         & (r1 == c1 - GATE_LANE0 + SRC_GATE - g_blk), 1.0, 0.0).astype(BF)
        ws_scr[...] = (jnp.dot(wr_ref[:, SRC_GA:SRC_GA + LANES], rep3, preferred_element_type=F32)
                       + jnp.dot(wr_ref[:, g_blk:g_blk + LANES], pick, preferred_element_type=F32)).astype(BF)
        tm_ = perm_scr.shape[0]
        p_r = lax.broadcasted_iota(jnp.int32, (tm_, tm_), 0)
        p_c = lax.broadcasted_iota(jnp.int32, (tm_, tm_), 1)
        n_blk = tm_ // CMP_STRIDE
        perm_scr[...] = jnp.where(p_c == CMP_STRIDE * (p_r % n_blk) + p_r // n_blk, 1.0, 0.0).astype(BF)

    x = x_ref[...]
    h = (x * lax.rsqrt(jnp.mean(x * x, axis=-1, keepdims=True) + EPS) * g_ref[...]).astype(BF)
    os_ref[...] = jnp.dot(h, ws_scr[...], preferred_element_type=F32)
    step = 512
    for c0 in range(0, ALIGNED, step):
        om_ref[:, c0:c0 + step] = jnp.dot(h, wr_ref[:, c0:c0 + step], preferred_element_type=F32).astype(BF)
    for c0 in range(0, N_MAIN - ALIGNED, step):
        c1_ = min(c0 + step, N_MAIN - ALIGNED)
        om_ref[:, ALIGNED + c0:ALIGNED + c1_] = jnp.dot(h, wm_scr[:, c0:c1_],
                                                        preferred_element_type=F32).astype(BF)
    kv = om_ref[:, OFF_KC:OFF_KC + 2 * LANES]
    moved = jnp.dot(perm_scr[...], kv, preferred_element_type=F32).astype(BF)
    n_blk = x_ref.shape[0] // CMP_STRIDE
    for l_ in range(CMP_STRIDE):
        xk_ref[:, l_ * LANES:(l_ + 1) * LANES] = moved[l_ * n_blk:(l_ + 1) * n_blk, 0:LANES]
        xv_ref[:, l_ * LANES:(l_ + 1) * LANES] = moved[l_ * n_blk:(l_ + 1) * n_blk, LANES:2 * LANES]


def _in_proj(x2, g, w_raw, w_tail, tm=512):
    n_tok = x2.shape[0]
    return pl.pallas_call(
        _in_proj_kernel,
        grid=(n_tok // tm,),
        in_specs=[
            pl.BlockSpec((tm, D_MODEL), lambda i: (i, 0)),
            _resident((1, D_MODEL), lambda i: (0, 0)),
            _resident((D_MODEL, IN_WIDTH), lambda i: (0, 0)),
            _resident((D_MODEL, LANES), lambda i: (0, 0)),
        ],
        out_specs=[
            pl.BlockSpec((tm, N_MAIN), lambda i: (i, 0)),
            pl.BlockSpec((tm, N_SMALL), lambda i: (i, 0)),
            pl.BlockSpec((tm // CMP_STRIDE, CMP_STRIDE * LANES), lambda i: (i, 0)),
            pl.BlockSpec((tm // CMP_STRIDE, CMP_STRIDE * LANES), lambda i: (i, 0)),
        ],
        out_shape=[
            jax.ShapeDtypeStruct((n_tok, N_MAIN), BF),
            jax.ShapeDtypeStruct((n_tok, N_SMALL), F32),
            jax.ShapeDtypeStruct((n_tok // CMP_STRIDE, CMP_STRIDE * LANES), BF),
            jax.ShapeDtypeStruct((n_tok // CMP_STRIDE, CMP_STRIDE * LANES), BF),
        ],
        scratch_shapes=[pltpu.VMEM((D_MODEL, N_MAIN - ALIGNED), BF), pltpu.VMEM((D_MODEL, N_SMALL), BF),
                        pltpu.VMEM((tm, tm), BF)],
        compiler_params=_params(("arbitrary",)),
    )(x2, g, w_raw, w_tail)


def _gla_kernel(q_ref, k_ref, v_ref, r_ref, a_ref, w2_ref, b2_ref, ng_ref, o_ref, st_scr, *, n_chunks, n_heads):
    blk = pl.program_id(2)

    @pl.when(blk == 0)
    def _():
        st_scr[...] = jnp.zeros_like(st_scr)

    C = GLA_CHUNK
    T = n_chunks * C
    W = n_heads * GLA_DK

    def split3(x):
        hi = x.astype(BF)
        rem = x - hi.astype(F32)
        mid = rem.astype(BF)
        return hi, mid, (rem - mid.astype(F32)).astype(BF)

    a = a_ref[0]
    a_hi = a.astype(BF)
    a_lo = (a - a_hi.astype(F32)).astype(BF)
    lane = lax.broadcasted_iota(jnp.int32, (T, LANES), 1)
    in_lo = (lane >= GLA_RANK) & (lane < 2 * GLA_RANK)
    z = jnp.dot(jnp.where(in_lo, a_lo, a_hi), w2_ref[...], preferred_element_type=F32) + b2_ref[...]
    log_a = (jnp.minimum(z, 0.0) - jnp.log(1.0 + jnp.exp(-jnp.abs(z)))) * (1.0 / GLA_TAU)

    x_wide = jnp.concatenate([log_a[c * C:(c + 1) * C] for c in range(n_chunks)], axis=1)
    x3 = jnp.concatenate(split3(x_wide), axis=0)
    r3 = lax.broadcasted_iota(jnp.int32, (C, 3 * C), 0)
    c3 = lax.broadcasted_iota(jnp.int32, (C, 3 * C), 1) & (C - 1)
    tri3 = jnp.where(c3 <= r3, 1.0, 0.0).astype(BF)
    b_wide = jnp.dot(tri3, x3, preferred_element_type=F32)
    bcum = jnp.concatenate([b_wide[:, c * W:(c + 1) * W] for c in range(n_chunks)], axis=0)
    last_rows = [b_wide[C - 1:C, c * W:(c + 1) * W] for c in range(n_chunks)]
    b_last = jnp.concatenate([jnp.broadcast_to(lr, (C, W)) for lr in last_rows], axis=0)
    decay = [jnp.exp(lr) for lr in last_rows]

    q = q_ref[0].astype(F32)
    k = k_ref[0].astype(F32)
    v = v_ref[0]
    qe = (q * ((GLA_DK ** -0.5) * jnp.exp(bcum))).astype(BF)
    ke = (k * jnp.exp(-bcum)).astype(BF)
    kd = (k * jnp.exp(b_last - bcum)).astype(BF)

    H = min(T, 4 * C)
    row = lax.broadcasted_iota(jnp.int32, (H, H), 0)
    col = lax.broadcasted_iota(jnp.int32, (H, H), 1)
    keep = (col <= row) & ((col >> 6) == (row >> 6))
    ng = ng_ref[...]
    r_all = r_ref[0].astype(F32)
    for hh in range(n_heads):
        ks_ = slice(hh * GLA_DK, (hh + 1) * GLA_DK)
        vs_ = slice(hh * GLA_DV, (hh + 1) * GLA_DV)
        intra = []
        for h0 in range(0, T, H):
            hs = slice(h0, h0 + H)
            attn = lax.dot_general(qe[hs, ks_], ke[hs, ks_], NT, preferred_element_type=F32)
            intra.append(jnp.dot(jnp.where(keep, attn, 0.0).astype(BF), v[hs, vs_], preferred_element_type=F32))
        o = jnp.concatenate(intra, axis=0)

        st = st_scr[hh]
        inter = []
        for c in range(n_chunks):
            sl = slice(c * C, (c + 1) * C)
            inter.append(lax.dot_general(qe[sl, ks_], st.astype(BF), NT, preferred_element_type=F32))
            upd = lax.dot_general(v[sl, vs_], kd[sl, ks_], TN, preferred_element_type=F32)
            st = st * decay[c][:, ks_] + upd
        st_scr[hh] = st
        o = o + jnp.concatenate(inter, axis=0)

        o = o * lax.rsqrt(jnp.mean(o * o, axis=-1, keepdims=True) + EPS) * ng
        r = r_all[:, vs_]
        o_ref[0, :, vs_] = (o * (r * jax.nn.sigmoid(r))).astype(BF)


def _gla(proj3, small3, w2p, b2, ng, blk_tokens=512, heads_per_step=4):
    B, S, _ = proj3.shape
    nblk = S // blk_tokens
    hb = heads_per_step
    kern = functools.partial(_gla_kernel, n_chunks=blk_tokens // GLA_CHUNK, n_heads=hb)
    wk, wv = hb * GLA_DK, hb * GLA_DV
    return pl.pallas_call(
        kern,
        grid=(B, GLA_HEADS // hb, nblk),
        in_specs=[
            pl.BlockSpec((1, blk_tokens, wk), lambda b, h, i: (b, i, OFF_GQ // wk + h)),
            pl.BlockSpec((1, blk_tokens, wk), lambda b, h, i: (b, i, OFF_GK // wk + h)),
            pl.BlockSpec((1, blk_tokens, wv), lambda b, h, i: (b, i, OFF_GV // wv + h)),
            pl.BlockSpec((1, blk_tokens, wv), lambda b, h, i: (b, i, OFF_GR // wv + h)),
            pl.BlockSpec((1, blk_tokens, LANES), lambda b, h, i: (b, i, 0)),
            pl.BlockSpec((LANES, wk), lambda b, h, i: (0, h)),
            pl.BlockSpec((1, wk), lambda b, h, i: (0, h)),
            pl.BlockSpec((1, GLA_DV), lambda b, h, i: (0, 0)),
        ],
        out_specs=pl.BlockSpec((1, blk_tokens, wv), lambda b, h, i: (b, i, h)),
        out_shape=jax.ShapeDtypeStruct((B, S, GLA_HEADS * GLA_DV), BF),
        scratch_shapes=[pltpu.VMEM((hb, GLA_DV, GLA_DK), F32)],
        compiler_params=_params(("parallel", "parallel", "arbitrary")),
    )(proj3, proj3, proj3, proj3, small3, w2p, b2, ng)


def _compress_kernel(xk_ref, xv_ref, pek_ref, pev_ref, w1k_ref, w1v_ref, w2k_ref, w2v_ref, kc_ref, vc_ref):
    n_rows = xk_ref.shape[1]
    half = CMP_STRIDE * LANES
    lane = lax.broadcasted_iota(jnp.int32, (n_rows, LANES), 1)
    row = lax.broadcasted_iota(jnp.int32, (n_rows, LANES), 0)
    end_c = CMP_STRIDE * row + (CMP_LEN - 1)
    c_k = _pos_columns(lane, end_c).astype(F32)
    c_v = jnp.where(lane == NSA_DH, 1.0, 0.0).astype(F32)

    def branch(x_ref, pe_ref, w1_ref, w2_ref, const, o_ref):
        x = x_ref[0].astype(F32)
        xa = (x + pe_ref[0:1, :]).astype(BF)
        xb = (x + pe_ref[1:2, :]).astype(BF)
        a = jnp.dot(xa, w1_ref[0:half, :], preferred_element_type=F32)
        b = jnp.dot(xb, w1_ref[half:2 * half, :], preferred_element_type=F32)
        pre = a + pltpu.roll(b, n_rows - 1, 0)
        hid = (pre * jax.nn.sigmoid(pre)).astype(BF)
        for g in range(NSA_GROUPS):
            o_ref[0, g] = (jnp.dot(hid, w2_ref[g], preferred_element_type=F32) + const).astype(BF)

    branch(xk_ref, pek_ref, w1k_ref, w2k_ref, c_k, kc_ref)
    branch(xv_ref, pev_ref, w1v_ref, w2v_ref, c_v, vc_ref)


def _nsa_compress(xk, xv, pek, pev, w1k, w1v, w2k, w2v):
    B, n_rows, width = xk.shape
    full = lambda shape: _resident(shape, lambda b: (0,) * len(shape))
    out = jax.ShapeDtypeStruct((B, NSA_GROUPS, n_rows, LANES), BF)
    return pl.pallas_call(
        _compress_kernel,
        grid=(B,),
        in_specs=[
            pl.BlockSpec((1, n_rows, width), lambda b: (b, 0, 0)),
            pl.BlockSpec((1, n_rows, width), lambda b: (b, 0, 0)),
            full(pek.shape), full(pev.shape), full(w1k.shape), full(w1v.shape),
            full(w2k.shape), full(w2v.shape),
        ],
        out_specs=[pl.BlockSpec((1, NSA_GROUPS, n_rows, LANES), lambda b: (b, 0, 0, 0))] * 2,
        out_shape=[out, out],
        compiler_params=_params(("parallel",)),
    )(xk, xv, pek, pev, w1k, w1v, w2k, w2v)


def _nsa_kernel(q_ref, ks_ref, vs_ref, kw_ref, vw_ref, kc_ref, vc_ref, gt_ref, sl_ref, ov_ref,
                o_ref, ksa, kwa, vst, vwt, vct, acc_scr, pw_scr, act_ref, *, seq):
    qi = pl.program_id(2)
    grp = pl.program_id(1)
    n_slc = seq // SLC_LEN
    R = NSA_REP
    KT = SLC_TILE
    WK = WINDOW + QB
    n_cmp = kc_ref.shape[2]

    def t_bf(x):
        return x.astype(F32).T[0:VROWS].astype(BF)

    @pl.when(qi == 0)
    def _():
        lane = lax.broadcasted_iota(jnp.int32, (seq, LANES), 1)
        pos = lax.broadcasted_iota(jnp.int32, (seq, LANES), 0)
        blk = pos >> 6
        off = pos & 63
        r_i = lax.broadcasted_iota(jnp.int32, (LANES, LANES), 0)
        c_i = lax.broadcasted_iota(jnp.int32, (LANES, LANES), 1)
        pick = jnp.where((c_i < NSA_DH) & (r_i == c_i + grp * NSA_DH), 1.0, 0.0).astype(BF)
        c_s = jnp.where(lane == LANES - 1, off,
                        jnp.where((lane >= NSA_DH) & (lane - (NSA_DH - 1) == blk), 1, 0))
        ksa[...] = (jnp.dot(ks_ref[0], pick, preferred_element_type=F32) + c_s.astype(F32)).astype(BF)
        c_w = _pos_columns(lane, pos)
        kwa[0:WINDOW, :] = jnp.zeros((WINDOW, LANES), BF)
        kwa[WINDOW:WINDOW + seq, :] = (jnp.dot(kw_ref[0], pick, preferred_element_type=F32)
                                       + c_w.astype(F32)).astype(BF)
        ones_rows = jnp.where(lax.broadcasted_iota(jnp.int32, (VROWS - NSA_DH, LANES), 0) == 0, 1.0, 0.0)
        n_pad = WINDOW // LANES
        for c in range(n_pad):
            vwt[c] = jnp.zeros((VROWS, LANES), BF)

        def v_tile(x):
            xt = x.astype(F32).T
            dims = jnp.where(grp == 0, xt[0:NSA_DH], xt[NSA_DH:2 * NSA_DH])
            return jnp.concatenate([dims, ones_rows], axis=0).astype(BF)

        def fill(c, carry):
            rows = pl.ds(pl.multiple_of(c * LANES, LANES), LANES)
            vst[c] = v_tile(vs_ref[0, rows, :])
            vwt[c + n_pad] = v_tile(vw_ref[0, rows, :])
            return carry

        lax.fori_loop(0, seq // LANES, fill, 0)
        for c in range(n_cmp // LANES):
            vct[:, c * LANES:(c + 1) * LANES] = t_bf(vc_ref[0, 0, c * LANES:(c + 1) * LANES, :])

    q0 = qi * QB
    rowi = lax.broadcasted_iota(jnp.int32, (NSA_DH, QB), 0)
    slopes = [jnp.concatenate([sl_ref[0, r:r + 1, :]] * (QB // LANES), axis=1) for r in range(R)]
    q_all = q_ref[0].astype(F32).T
    q_t = [q_all[r * NSA_DH:(r + 1) * NSA_DH] for r in range(R)]

    qw = jnp.concatenate(
        [jnp.concatenate([q_t[r] * LOG2E, _slope_rows(rowi, slopes[r] * LOG2E)], axis=0).astype(BF)
         for r in range(R)], axis=1)

    def tile4(x):
        return jnp.concatenate([x] * R, axis=1)

    ovt = ov_ref[...]

    def cmp_branch(rows):
        def fn():
            kw = kwa[pl.ds(pl.multiple_of(q0, QB), WK), :]
            s_w = jnp.dot(kw, qw, preferred_element_type=F32)
            c_w = lax.broadcasted_iota(jnp.int32, (WK, QB), 0)
            d_w = c_w - lax.broadcasted_iota(jnp.int32, (WK, QB), 1)
            mask_w = (d_w > 0) & (d_w <= WINDOW) & (c_w >= WINDOW - q0)
            s_w = s_w + tile4(jnp.where(mask_w, 0.0, NEG))
            m_w = jnp.max(s_w, axis=0, keepdims=True)
            pw_scr[...] = jnp.exp2(s_w - m_w).astype(BF)

            s_c = jnp.dot(kc_ref[0, 0, 0:rows, :], qw, preferred_element_type=F32)
            e_c = CMP_STRIDE * lax.broadcasted_iota(jnp.int32, (rows, QB), 0) + (CMP_LEN - 1)
            t_c = q0 + lax.broadcasted_iota(jnp.int32, (rows, QB), 1)
            s_c = s_c + tile4(jnp.where(e_c <= t_c, 0.0, NEG))
            m_c = jnp.maximum(jnp.max(s_c, axis=0, keepdims=True), 0.1 * NEG)
            p_c = jnp.exp2(s_c - m_c)
            l_c = jnp.sum(p_c, axis=0, keepdims=True)
            p_c = p_c * jnp.where(l_c > 0.0, 1.0 / l_c, 0.0)
            o_cmp = jnp.dot(vct[:, 0:rows], p_c.astype(BF), preferred_element_type=F32)
            psum = p_c[:, 0:QB]
            for r in range(1, R):
                psum = psum + p_c[:, r * QB:(r + 1) * QB]
            p_hi = psum.astype(BF)
            rem = psum - p_hi.astype(F32)
            p_mid = rem.astype(BF)
            p_lo = (rem - p_mid.astype(F32)).astype(BF)
            ov = ovt[:, 0:rows]
            return o_cmp, (jnp.dot(ov, p_hi, preferred_element_type=F32)
                           + jnp.dot(ov, p_mid, preferred_element_type=F32)
                           + jnp.dot(ov, p_lo, preferred_element_type=F32))
        return fn

    half = n_cmp // 2
    o_c, imp = lax.cond(q0 + QB <= CMP_STRIDE * half + CMP_LEN - 1, cmp_branch(half), cmp_branch(n_cmp))

    NR = ov_ref.shape[0]
    SUB = 8
    jblk = lax.broadcasted_iota(jnp.int32, (NR, QB), 0)
    t_q = q0 + lax.broadcasted_iota(jnp.int32, (NR, QB), 1)
    cur = t_q >> 6
    forced = (jblk == 0) | (jblk == cur) | (jblk == cur - 1)
    score = jnp.where(jblk > cur, NEG, jnp.where(forced, -NEG, imp))
    n_slab = -(-n_slc // SUB)
    isub = lax.broadcasted_iota(jnp.int32, (SUB, QB), 0)

    def rank_counts(ns):
        def fn():
            slabs = [score[a * SUB:(a + 1) * SUB, :] for a in range(ns)]
            cnts = [jnp.zeros((SUB, QB), F32) for _ in range(ns)]
            for jp in range(min(ns * SUB, n_slc)):
                rowv = jnp.broadcast_to(score[jp:jp + 1, :], (SUB, QB))
                for a in range(ns):
                    if a < jp // SUB:
                        beats = jnp.where(rowv > slabs[a], 1.0, 0.0)
                    elif a > jp // SUB:
                        beats = jnp.where(rowv >= slabs[a], 1.0, 0.0)
                    else:
                        beats = jnp.where(isub > jp % SUB, jnp.where(rowv >= slabs[a], 1.0, 0.0),
                                          jnp.where(rowv > slabs[a], 1.0, 0.0))
                    cnts[a] = cnts[a] + beats
            rest = NR - ns * SUB
            return jnp.concatenate(cnts + [jnp.full((rest, QB), float(NR), F32)] * (rest > 0), axis=0)
        return fn

    last_blk = (q0 + QB - 1) >> 6
    cnt = lax.switch(last_blk // SUB, [rank_counts(ns) for ns in range(1, n_slab + 1)])
    sel = (cnt < float(N_SEL)) & (jblk <= cur) & (jblk < n_slc)
    a_nat = jnp.where(jblk == 0, 1.0, jnp.where(sel, (SLC_LEN * jblk).astype(F32), NEG))
    a_nat = jnp.where(jblk < n_slc, a_nat, 0.0)
    a_t = pltpu.roll(a_nat, NR - 1, 0)

    any_q = jnp.max(jnp.where(sel, 1.0, 0.0), axis=1, keepdims=True)
    jcol = lax.broadcasted_iota(jnp.int32, (NR, 1), 0)
    bits = jnp.where(any_q > 0.0, lax.shift_left(jnp.int32(1), jcol & 31), 0)
    word0 = jnp.sum(jnp.where(jcol < 32, bits, 0))
    word1 = jnp.sum(jnp.where(jcol >= 32, bits, 0))

    qs = jnp.concatenate([jnp.concatenate([q_t[r], a_t * slopes[r]], axis=0).astype(BF) for r in range(R)],
                         axis=1)

    tiles_per = KT // LANES
    blocks_per = KT // SLC_LEN
    n_full = q0 // KT

    def scan(ti, n):
        word = jnp.where(ti < 32 // blocks_per, word0, word1)
        hit = (lax.shift_right_logical(word, (ti * blocks_per) & 31) & ((1 << blocks_per) - 1)) != 0
        act_ref[n] = ti
        return n + jnp.where(hit, 1, 0)

    n_act = lax.fori_loop(0, n_full, scan, 0)

    def scores(ti):
        k0 = pl.multiple_of(ti * KT, KT)
        return jnp.dot(ksa[pl.ds(k0, KT), :], qs, preferred_element_type=F32)

    def values_t(ti):
        return [vst[ti * tiles_per + c] for c in range(tiles_per)]

    def absorb(s, v_tiles, m_prev):
        m_new = jnp.maximum(m_prev, jnp.max(s, axis=0, keepdims=True))
        alpha = jnp.exp(m_prev - m_new)
        p = jnp.exp(s - m_new).astype(BF)
        acc_scr[...] = acc_scr[...] * alpha + jnp.dot(jnp.concatenate(v_tiles, axis=1), p,
                                                     preferred_element_type=F32)
        return m_new

    acc_scr[...] = jnp.zeros(acc_scr.shape, F32)

    def chains(tiles, diag_flags, m_run):
        s_all = []
        for t, is_diag in zip(tiles, diag_flags):
            s = scores(t)
            if is_diag:
                p_d = t * KT + lax.broadcasted_iota(jnp.int32, (KT, QB), 0)
                t_d = q0 + lax.broadcasted_iota(jnp.int32, (KT, QB), 1)
                s = s + tile4(jnp.where(p_d <= t_d, 0.0, NEG))
            s_all.append(s)
        for t, s in zip(tiles, s_all):
            m_run = absorb(s, values_t(t), m_run)
        return m_run

    n_trip = n_act // TRIP_TILES
    m_q = lax.fori_loop(0, n_trip, lambda j, m: chains([act_ref[TRIP_TILES * j + i] for i in range(TRIP_TILES)],
                                                       [False] * TRIP_TILES, m),
                        jnp.full((1, R * QB), NEG, F32))
    n_diag = QB // KT

    def finish(rest):
        def fn(m_run):
            tiles = [act_ref[TRIP_TILES * n_trip + i] for i in range(rest)] + [n_full + d for d in range(n_diag)]
            chains(tiles, [False] * rest + [True] * n_diag, m_run)
            acc_s = acc_scr[...]
            vw_t = jnp.concatenate([vwt[qi * (QB // LANES) + c] for c in range(WK // LANES)], axis=1)
            acc_w = jnp.dot(vw_t, pw_scr[...], preferred_element_type=F32)
            sg_all = jax.nn.sigmoid(gt_ref[0]).T
            per_g = R * 3
            sg_t = jnp.where(grp == 0, sg_all[GATE_LANE0:GATE_LANE0 + per_g],
                             sg_all[GATE_LANE0 + per_g:GATE_LANE0 + 2 * per_g])
            outs = []
            for r in range(R):
                cs = slice(r * QB, (r + 1) * QB)
                a_s = acc_s[:, cs]
                a_w = acc_w[:, cs]
                g_c = sg_t[3 * r:3 * r + 1, :]
                g_s = sg_t[3 * r + 1:3 * r + 2, :] / a_s[NSA_DH:NSA_DH + 1, :]
                g_w = sg_t[3 * r + 2:3 * r + 3, :] / a_w[NSA_DH:NSA_DH + 1, :]
                outs.append((g_c * o_c[:, cs] + g_s * a_s + g_w * a_w)[0:NSA_DH])
            o_ref[0] = jnp.concatenate(outs, axis=0).T.astype(BF)
        return fn

    lax.switch(n_act - TRIP_TILES * n_trip, [finish(rest) for rest in range(TRIP_TILES)], m_q)


def _nsa_attend(proj3, small3, kc, vc, slope_tab, ovt):
    B, S, _ = proj3.shape
    G, R = NSA_GROUPS, NSA_REP
    n_cmp = kc.shape[2]
    kern = functools.partial(_nsa_kernel, seq=S)
    kv_spec = lambda off: pl.BlockSpec((1, S, LANES), lambda b, g, i: (b, 0, off // LANES))
    return pl.pallas_call(
        kern,
        grid=(B, G, S // QB),
        in_specs=[
            pl.BlockSpec((1, QB, R * NSA_DH), lambda b, g, i: (b, i, OFF_NQ // (R * NSA_DH) + g)),
            kv_spec(OFF_KS), kv_spec(OFF_VS), kv_spec(OFF_KW), kv_spec(OFF_VW),
            pl.BlockSpec((1, 1, n_cmp, LANES), lambda b, g, i: (b, g, 0, 0)),
            pl.BlockSpec((1, 1, n_cmp, LANES), lambda b, g, i: (b, g, 0, 0)),
            pl.BlockSpec((1, QB, LANES), lambda b, g, i: (b, i, 0)),
            pl.BlockSpec((1, 8, LANES), lambda b, g, i: (g, 0, 0)),
            _resident(ovt.shape, lambda b, g, i: (0, 0)),
        ],
        out_specs=pl.BlockSpec((1, QB, R * NSA_DH), lambda b, g, i: (b, i, g)),
        out_shape=jax.ShapeDtypeStruct((B, S, G * R * NSA_DH), BF),
        scratch_shapes=[
            pltpu.VMEM((S, LANES), BF),
            pltpu.VMEM((S + WINDOW, LANES), BF),
            pltpu.VMEM((S // LANES, VROWS, LANES), BF),
            pltpu.VMEM(((S + WINDOW) // LANES, VROWS, LANES), BF),
            pltpu.VMEM((VROWS, n_cmp), BF),
            pltpu.VMEM((VROWS, R * QB), F32),
            pltpu.VMEM((WINDOW + QB, R * QB), BF),
            pltpu.SMEM((S // SLC_TILE + 1,), jnp.int32),
        ],
        compiler_params=_params(("parallel", "parallel", "arbitrary")),
    )(proj3, proj3, proj3, proj3, proj3, kc, vc, small3, slope_tab, ovt)


def _merge_kernel(og_ref, on_ref, mg_ref, mn_ref, x_ref, wg_ref, wn_ref, wo_ref, g_ref, o_ref):
    a = jnp.dot(og_ref[...], wg_ref[...], preferred_element_type=F32)
    b = jnp.dot(on_ref[...], wn_ref[...], preferred_element_type=F32)
    mixed = jax.nn.sigmoid(mg_ref[...].astype(F32)) * a + jax.nn.sigmoid(mn_ref[...].astype(F32)) * b
    y = jnp.dot(mixed.astype(BF), wo_ref[...], preferred_element_type=F32)
    y = y * lax.rsqrt(jnp.mean(y * y, axis=-1, keepdims=True) + EPS) * g_ref[...]
    o_ref[...] = x_ref[...] + y


def _merge_out(o_gla2, o_nsa2, proj2, x2, wg, wn, wo, g, tm=512):
    n_tok = x2.shape[0]
    D = D_MODEL
    return pl.pallas_call(
        _merge_kernel,
        grid=(n_tok // tm,),
        in_specs=[
            pl.BlockSpec((tm, D), lambda i: (i, 0)),
            pl.BlockSpec((tm, o_nsa2.shape[1]), lambda i: (i, 0)),
            pl.BlockSpec((tm, D), lambda i: (i, OFF_MG // D)),
            pl.BlockSpec((tm, D), lambda i: (i, OFF_MN // D)),
            pl.BlockSpec((tm, D), lambda i: (i, 0)),
            _resident(wg.shape, lambda i: (0, 0)),
            _resident(wn.shape, lambda i: (0, 0)),
            _resident(wo.shape, lambda i: (0, 0)),
            _resident((1, D), lambda i: (0, 0)),
        ],
        out_specs=pl.BlockSpec((tm, D), lambda i: (i, 0)),
        out_shape=jax.ShapeDtypeStruct((n_tok, D), F32),
        compiler_params=_params(("parallel",)),
    )(o_gla2, o_nsa2, proj2, proj2, x2, wg, wn, wo, g)


def _ffn_kernel(x_ref, gpre_ref, wg_ref, wu_ref, wd_ref, gpost_ref, o_ref, acc_scr, *, chunk):
    x = x_ref[...]
    h = (x * lax.rsqrt(jnp.mean(x * x, axis=-1, keepdims=True) + EPS) * gpre_ref[...]).astype(BF)
    d_ff = wg_ref.shape[1]
    for n, c0 in enumerate(range(0, d_ff, chunk)):
        c1 = min(c0 + chunk, d_ff)
        a = jnp.dot(h, wg_ref[:, c0:c1], preferred_element_type=F32)
        u = jnp.dot(h, wu_ref[:, c0:c1], preferred_element_type=F32)
        t = (a * jax.nn.sigmoid(a) * u).astype(BF)
        part = jnp.dot(t, wd_ref[c0:c1, :], preferred_element_type=F32)
        if n == 0:
            acc_scr[...] = part
        else:
            acc_scr[...] += part
    f = acc_scr[...]
    o_ref[...] = x + f * lax.rsqrt(jnp.mean(f * f, axis=-1, keepdims=True) + EPS) * gpost_ref[...]


def _ffn(x2, gpre, wg, wu, wd, gpost, tm=512, chunk=512):
    n_tok = x2.shape[0]
    D = D_MODEL
    kern = functools.partial(_ffn_kernel, chunk=chunk)
    return pl.pallas_call(
        kern,
        grid=(n_tok // tm,),
        in_specs=[
            pl.BlockSpec((tm, D), lambda i: (i, 0)),
            _resident((1, D), lambda i: (0, 0)),
            _resident(wg.shape, lambda i: (0, 0)),
            _resident(wu.shape, lambda i: (0, 0)),
            _resident(wd.shape, lambda i: (0, 0)),
            _resident((1, D), lambda i: (0, 0)),
        ],
        out_specs=pl.BlockSpec((tm, D), lambda i: (i, 0)),
        out_shape=jax.ShapeDtypeStruct((n_tok, D), F32),
        scratch_shapes=[pltpu.VMEM((tm, D), F32)],
        compiler_params=_params(("parallel",)),
    )(x2, gpre, wg, wu, wd, gpost)


def _prep_compress(pe, w1, w2):
    eye = jnp.eye(NSA_GROUPS, dtype=F32)
    w1r = w1.reshape(CMP_LEN, NSA_DH, NSA_DH)
    w1e = jnp.einsum('lde,gh->lgdhe', w1r, eye).reshape(CMP_LEN * NSA_GROUPS * NSA_DH, NSA_GROUPS * NSA_DH)
    pe_e = jnp.broadcast_to(pe[:, None, :], (CMP_LEN, NSA_GROUPS, NSA_DH)).reshape(2, CMP_STRIDE * LANES)
    pe_e = jnp.pad(pe_e, ((0, 6), (0, 0)))
    w2e = jnp.stack([
        jnp.pad(jnp.pad(w2, ((g * NSA_DH, (NSA_GROUPS - 1 - g) * NSA_DH), (0, 0))), ((0, 0), (0, LANES - NSA_DH)))
        for g in range(NSA_GROUPS)])
    return pe_e.astype(F32), w1e.astype(BF), w2e.astype(BF)


def _overlap_table(seq):
    n_cmp = (seq - CMP_LEN) // CMP_STRIDE + 1
    n_slc = seq // SLC_LEN
    sc = CMP_STRIDE * np.arange(n_cmp)
    ss = SLC_LEN * np.arange(n_slc)
    ov = np.clip(np.minimum(sc[:, None] + CMP_LEN, ss[None, :] + SLC_LEN)
                 - np.maximum(sc[:, None], ss[None, :]), 0, None).astype(np.float32) / CMP_LEN
    ovt = np.zeros((NSA_DH, n_cmp + 1), np.float32)
    ovt[:n_slc, :n_cmp] = ov.T
    return jnp.asarray(ovt, dtype=BF)


def kernel(x, norm_mix_pre, norm_mix_post, norm_ffn_pre, norm_ffn_post, w_in, gla_w_alpha2, gla_b_alpha, gla_norm_g, nsa_cmp_pe_k, nsa_cmp_w1_k, nsa_cmp_w2_k, nsa_cmp_pe_v, nsa_cmp_w1_v, nsa_cmp_w2_v, w_proj_gla, w_proj_nsa, w_out, w_ffn_gate, w_ffn_up, w_ffn_down):
    B, S, D = x.shape
    depth = w_in.shape[0]
    n_tok = B * S
    h_idx = jnp.arange(NSA_HEADS, dtype=F32)
    slopes = jnp.exp2(-8.0 * (h_idx + 1.0) / NSA_HEADS).reshape(NSA_GROUPS, NSA_REP, 1)
    slope_tab = jnp.broadcast_to(jnp.pad(slopes, ((0, 0), (0, 8 - NSA_REP), (0, 0))), (NSA_GROUPS, 8, LANES))
    ovt = _overlap_table(S)
    x2 = x.reshape(n_tok, D)
    for l in range(depth):
        w_raw = w_in[l].astype(BF)
        w_tail = jnp.pad(w_raw[:, TAIL_START:], ((0, 0), (0, TAIL_START + LANES - IN_WIDTH)))
        proj2, small2, xk2, xv2 = _in_proj(x2, norm_mix_pre[l][None, :], w_raw, w_tail)
        proj3 = proj2.reshape(B, S, N_MAIN)
        small3 = small2.reshape(B, S, N_SMALL)

        w2_hi = gla_w_alpha2[l].astype(BF)
        w2_lo = (gla_w_alpha2[l] - w2_hi.astype(F32)).astype(BF)
        w2p = jnp.pad(jnp.concatenate([w2_hi, w2_hi, w2_lo], axis=0), ((0, LANES - 3 * GLA_RANK), (0, 0)))
        o_gla = _gla(proj3, small3, w2p, gla_b_alpha[l][None, :], gla_norm_g[l][None, :])

        xk = xk2.reshape(B, S // CMP_STRIDE, CMP_STRIDE * LANES)
        xv = xv2.reshape(B, S // CMP_STRIDE, CMP_STRIDE * LANES)
        pek, w1k, w2k = _prep_compress(nsa_cmp_pe_k[l], nsa_cmp_w1_k[l], nsa_cmp_w2_k[l])
        pev, w1v, w2v = _prep_compress(nsa_cmp_pe_v[l], nsa_cmp_w1_v[l], nsa_cmp_w2_v[l])
        kc, vc = _nsa_compress(xk, xv, pek, pev, w1k, w1v, w2k, w2v)
        o_nsa = _nsa_attend(proj3, small3, kc, vc, slope_tab, ovt)

        x2 = _merge_out(o_gla.reshape(n_tok, -1), o_nsa.reshape(n_tok, -1), proj2, x2,
                        w_proj_gla[l].astype(BF), w_proj_nsa[l].astype(BF),
                        w_out[l].astype(BF), norm_mix_post[l][None, :])
        x2 = _ffn(x2, norm_ffn_pre[l][None, :], w_ffn_gate[l].astype(BF), w_ffn_up[l].astype(BF),
                  w_ffn_down[l].astype(BF), norm_ffn_post[l][None, :])
    return x2.reshape(B, S, D)
```

```python
import functools

import numpy as np
import jax
import jax.numpy as jnp
from jax import lax
from jax.experimental import pallas as pl
from jax.experimental.pallas import tpu as pltpu

D_MODEL = 1024
GLA_HEADS = 4
GLA_DK = 128
GLA_DV = 256
GLA_RANK = 16
GLA_TAU = 16.0
GLA_CHUNK = 64
NSA_HEADS = 8
NSA_GROUPS = 2
NSA_REP = 4
NSA_DH = 64
CMP_LEN = 32
CMP_STRIDE = 16
SLC_LEN = 64
N_SEL = 16
WINDOW = 512
QB = 256
SLC_TILE = 128
TRIP_TILES = 6
VROWS = 80
D_FF = 2816
EPS = 1e-6
NEG = -1e30

LANES = 128
VMEM_LIMIT = 56 * 1024 * 1024
BF = jnp.bfloat16
F32 = jnp.float32

OFF_GQ = 0
OFF_GK = 512
OFF_GV = 1024
OFF_GR = 2048
OFF_MG = 3072
OFF_MN = 4096
OFF_NQ = 5120
OFF_KC = 5632
OFF_VC = 5760
OFF_KS = 5888
OFF_VS = 6016
OFF_KW = 6144
OFF_VW = 6272
N_MAIN = 6400
N_SMALL = 128
GATE_LANE0 = 3 * GLA_RANK

NT = (((1,), (1,)), ((), ()))
TN = (((0,), (0,)), ((), ()))


LOG2E = 1.4426950408889634
POS_TERMS = 3
PAD_FLAG = 2 * POS_TERMS


def _pos_columns(lane, pos):
    first = lane - NSA_DH
    return jnp.where((first >= 0) & (first < POS_TERMS), pos >> 6,
                     jnp.where((first >= POS_TERMS) & (first < 2 * POS_TERMS), pos & 63, 0))


def _slope_rows(rowi, coef):
    terms = []
    rem = coef
    for _ in range(POS_TERMS):
        t = rem.astype(BF).astype(F32)
        terms.append(t)
        rem = rem - t
    out = jnp.where(rowi == PAD_FLAG, NEG, 0.0)
    for i, t in enumerate(terms):
        out = jnp.where(rowi == i, t * float(SLC_LEN), jnp.where(rowi == POS_TERMS + i, t, out))
    return out


def _resident(shape, index_map):
    return pl.BlockSpec(shape, index_map, pipeline_mode=pl.Buffered(1))


def _params(sem):
    return pltpu.CompilerParams(dimension_semantics=sem, vmem_limit_bytes=VMEM_LIMIT)


SRC_GA = 3072
SRC_NQ = 3088
SRC_KV = 3600
SRC_GATE = 4368
SRC_MERGE = 4392
IN_WIDTH = 6440
TAIL_START = IN_WIDTH // LANES * LANES
ALIGNED = OFF_MG
SHIFTED_RUNS = ((SRC_MERGE, 2048, OFF_MG, 1.0), (SRC_NQ, 512, OFF_NQ, NSA_DH ** -0.5), (SRC_KV, 768, OFF_KC, 1.0))


def _in_proj_kernel(x_ref, g_ref, wr_ref, wt_ref, om_ref, os_ref, xk_ref, xv_ref, wm_scr, ws_scr, perm_scr):
    @pl.when(pl.program_id(0) == 0)
    def _():
        r_i = lax.broadcasted_iota(jnp.int32, (2 * LANES, LANES), 0)
        c_i = lax.broadcasted_iota(jnp.int32, (2 * LANES, LANES), 1)
        for src, width, dst, scale in SHIFTED_RUNS:
            shift = jnp.where(r_i == c_i + src % LANES, 1.0, 0.0).astype(BF)
            for j in range(width // LANES):
                b0 = (src // LANES + j) * LANES
                if b0 + 2 * LANES <= TAIL_START:
                    pair = wr_ref[:, b0:b0 + 2 * LANES]
                else:
                    pair = jnp.concatenate([wr_ref[:, b0:b0 + LANES], wt_ref[...]], axis=1)
                moved = jnp.dot(pair, shift, preferred_element_type=F32)
                d0 = dst - ALIGNED + j * LANES
                wm_scr[:, d0:d0 + LANES] = (moved * scale).astype(BF)
        r1 = lax.broadcasted_iota(jnp.int32, (LANES, LANES), 0)
        c1 = lax.broadcasted_iota(jnp.int32, (LANES, LANES), 1)
        rep3 = jnp.where((r1 < GLA_RANK) & (c1 < GATE_LANE0) & ((c1 % GLA_RANK) == r1), 1.0, 0.0).astype(BF)
        g_blk = SRC_GATE // LANES * LANES
        n_gate = NSA_HEADS * 3
        pick = jnp.where((c1 >= GATE_LANE0) & (c1 < GATE_LANE0 + n_gate)
                         & (r1 == c1 - GATE_LANE0 + SRC_GATE - g_blk), 1.0, 0.0).astype(BF)
        ws_scr[...] = (jnp.dot(wr_ref[:, SRC_GA:SRC_GA + LANES], rep3, preferred_element_type=F32)
                       + jnp.dot(wr_ref[:, g_blk:g_blk + LANES], pick, preferred_element_type=F32)).astype(BF)
        tm_ = perm_scr.shape[0]
        p_r = lax.broadcasted_iota(jnp.int32, (tm_, tm_), 0)
        p_c = lax.broadcasted_iota(jnp.int32, (tm_, tm_), 1)
        n_blk = tm_ // CMP_STRIDE
        perm_scr[...] = jnp.where(p_c == CMP_STRIDE * (p_r % n_blk) + p_r // n_blk, 1.0, 0.0).astype(BF)

    x = x_ref[...]
    h = (x * lax.rsqrt(jnp.mean(x * x, axis=-1, keepdims=True) + EPS) * g_ref[...]).astype(BF)
    os_ref[...] = jnp.dot(h, ws_scr[...], preferred_element_type=F32)
    step = 512
    for c0 in range(0, ALIGNED, step):
        om_ref[:, c0:c0 + step] = jnp.dot(h, wr_ref[:, c0:c0 + step], preferred_element_type=F32).astype(BF)
    for c0 in range(0, N_MAIN - ALIGNED, step):
        c1_ = min(c0 + step, N_MAIN - ALIGNED)
        om_ref[:, ALIGNED + c0:ALIGNED + c1_] = jnp.dot(h, wm_scr[:, c0:c1_],
                                                        preferred_element_type=F32).astype(BF)
    kv = om_ref[:, OFF_KC:OFF_KC + 2 * LANES]
    moved = jnp.dot(perm_scr[...], kv, preferred_element_type=F32).astype(BF)
    n_blk = x_ref.shape[0] // CMP_STRIDE
    for l_ in range(CMP_STRIDE):
        xk_ref[:, l_ * LANES:(l_ + 1) * LANES] = moved[l_ * n_blk:(l_ + 1) * n_blk, 0:LANES]
        xv_ref[:, l_ * LANES:(l_ + 1) * LANES] = moved[l_ * n_blk:(l_ + 1) * n_blk, LANES:2 * LANES]


def _in_proj(x2, g, w_raw, w_tail, tm=512):
    n_tok = x2.shape[0]
    return pl.pallas_call(
        _in_proj_kernel,
        grid=(n_tok // tm,),
        in_specs=[
            pl.BlockSpec((tm, D_MODEL), lambda i: (i, 0)),
            _resident((1, D_MODEL), lambda i: (0, 0)),
            _resident((D_MODEL, IN_WIDTH), lambda i: (0, 0)),
            _resident((D_MODEL, LANES), lambda i: (0, 0)),
        ],
        out_specs=[
            pl.BlockSpec((tm, N_MAIN), lambda i: (i, 0)),
            pl.BlockSpec((tm, N_SMALL), lambda i: (i, 0)),
            pl.BlockSpec((tm // CMP_STRIDE, CMP_STRIDE * LANES), lambda i: (i, 0)),
            pl.BlockSpec((tm // CMP_STRIDE, CMP_STRIDE * LANES), lambda i: (i, 0)),
        ],
        out_shape=[
            jax.ShapeDtypeStruct((n_tok, N_MAIN), BF),
            jax.ShapeDtypeStruct((n_tok, N_SMALL), F32),
            jax.ShapeDtypeStruct((n_tok // CMP_STRIDE, CMP_STRIDE * LANES), BF),
            jax.ShapeDtypeStruct((n_tok // CMP_STRIDE, CMP_STRIDE * LANES), BF),
        ],
        scratch_shapes=[pltpu.VMEM((D_MODEL, N_MAIN - ALIGNED), BF), pltpu.VMEM((D_MODEL, N_SMALL), BF),
                        pltpu.VMEM((tm, tm), BF)],
        compiler_params=_params(("arbitrary",)),
    )(x2, g, w_raw, w_tail)


def _gla_kernel(q_ref, k_ref, v_ref, r_ref, a_ref, w2_ref, b2_ref, ng_ref, o_ref, st_scr, *, n_chunks, n_heads):
    blk = pl.program_id(2)

    @pl.when(blk == 0)
    def _():
        st_scr[...] = jnp.zeros_like(st_scr)

    C = GLA_CHUNK
    T = n_chunks * C
    W = n_heads * GLA_DK

    def split3(x):
        hi = x.astype(BF)
        rem = x - hi.astype(F32)
        mid = rem.astype(BF)
        return hi, mid, (rem - mid.astype(F32)).astype(BF)

    a = a_ref[0]
    a_hi = a.astype(BF)
    a_lo = (a - a_hi.astype(F32)).astype(BF)
    lane = lax.broadcasted_iota(jnp.int32, (T, LANES), 1)
    in_lo = (lane >= GLA_RANK) & (lane < 2 * GLA_RANK)
    z = jnp.dot(jnp.where(in_lo, a_lo, a_hi), w2_ref[...], preferred_element_type=F32) + b2_ref[...]
    log_a = (jnp.minimum(z, 0.0) - jnp.log(1.0 + jnp.exp(-jnp.abs(z)))) * (1.0 / GLA_TAU)

    x_wide = jnp.concatenate([log_a[c * C:(c + 1) * C] for c in range(n_chunks)], axis=1)
    x3 = jnp.concatenate(split3(x_wide), axis=0)
    r3 = lax.broadcasted_iota(jnp.int32, (C, 3 * C), 0)
    c3 = lax.broadcasted_iota(jnp.int32, (C, 3 * C), 1) & (C - 1)
    tri3 = jnp.where(c3 <= r3, 1.0, 0.0).astype(BF)
    b_wide = jnp.dot(tri3, x3, preferred_element_type=F32)
    bcum = jnp.concatenate([b_wide[:, c * W:(c + 1) * W] for c in range(n_chunks)], axis=0)
    last_rows = [b_wide[C - 1:C, c * W:(c + 1) * W] for c in range(n_chunks)]
    b_last = jnp.concatenate([jnp.broadcast_to(lr, (C, W)) for lr in last_rows], axis=0)
    decay = [jnp.exp(lr) for lr in last_rows]

    q = q_ref[0].astype(F32)
    k = k_ref[0].astype(F32)
    v = v_ref[0]
    qe = (q * ((GLA_DK ** -0.5) * jnp.exp(bcum))).astype(BF)
    ke = (k * jnp.exp(-bcum)).astype(BF)
    kd = (k * jnp.exp(b_last - bcum)).astype(BF)

    H = min(T, 4 * C)
    row = lax.broadcasted_iota(jnp.int32, (H, H), 0)
    col = lax.broadcasted_iota(jnp.int32, (H, H), 1)
    keep = (col <= row) & ((col >> 6) == (row >> 6))
    ng = ng_ref[...]
    r_all = r_ref[0].astype(F32)
    for hh in range(n_heads):
        ks_ = slice(hh * GLA_DK, (hh + 1) * GLA_DK)
        vs_ = slice(hh * GLA_DV, (hh + 1) * GLA_DV)
        intra = []
        for h0 in range(0, T, H):
            hs = slice(h0, h0 + H)
            attn = lax.dot_general(qe[hs, ks_], ke[hs, ks_], NT, preferred_element_type=F32)
            intra.append(jnp.dot(jnp.where(keep, attn, 0.0).astype(BF), v[hs, vs_], preferred_element_type=F32))
        o = jnp.concatenate(intra, axis=0)

        st = st_scr[hh]
        inter = []
        for c in range(n_chunks):
            sl = slice(c * C, (c + 1) * C)
            inter.append(lax.dot_general(qe[sl, ks_], st.astype(BF), NT, preferred_element_type=F32))
            upd = lax.dot_general(v[sl, vs_], kd[sl, ks_], TN, preferred_element_type=F32)
            st = st * decay[c][:, ks_] + upd
        st_scr[hh] = st
        o = o + jnp.concatenate(inter, axis=0)

        o = o * lax.rsqrt(jnp.mean(o * o, axis=-1, keepdims=True) + EPS) * ng
        r = r_all[:, vs_]
        o_ref[0, :, vs_] = (o * (r * jax.nn.sigmoid(r))).astype(BF)


def _gla(proj3, small3, w2p, b2, ng, blk_tokens=512, heads_per_step=4):
    B, S, _ = proj3.shape
    nblk = S // blk_tokens
    hb = heads_per_step
    kern = functools.partial(_gla_kernel, n_chunks=blk_tokens // GLA_CHUNK, n_heads=hb)
    wk, wv = hb * GLA_DK, hb * GLA_DV
    return pl.pallas_call(
        kern,
        grid=(B, GLA_HEADS // hb, nblk),
        in_specs=[
            pl.BlockSpec((1, blk_tokens, wk), lambda b, h, i: (b, i, OFF_GQ // wk + h)),
            pl.BlockSpec((1, blk_tokens, wk), lambda b, h, i: (b, i, OFF_GK // wk + h)),
            pl.BlockSpec((1, blk_tokens, wv), lambda b, h, i: (b, i, OFF_GV // wv + h)),
            pl.BlockSpec((1, blk_tokens, wv), lambda b, h, i: (b, i, OFF_GR // wv + h)),
            pl.BlockSpec((1, blk_tokens, LANES), lambda b, h, i: (b, i, 0)),
            pl.BlockSpec((LANES, wk), lambda b, h, i: (0, h)),
            pl.BlockSpec((1, wk), lambda b, h, i: (0, h)),
            pl.BlockSpec((1, GLA_DV), lambda b, h, i: (0, 0)),
        ],
        out_specs=pl.BlockSpec((1, blk_tokens, wv), lambda b, h, i: (b, i, h)),
        out_shape=jax.ShapeDtypeStruct((B, S, GLA_HEADS * GLA_DV), BF),
        scratch_shapes=[pltpu.VMEM((hb, GLA_DV, GLA_DK), F32)],
        compiler_params=_params(("parallel", "parallel", "arbitrary")),
    )(proj3, proj3, proj3, proj3, small3, w2p, b2, ng)


def _compress_kernel(xk_ref, xv_ref, pek_ref, pev_ref, w1k_ref, w1v_ref, w2k_ref, w2v_ref, kc_ref, vc_ref):
    n_rows = xk_ref.shape[1]
    half = CMP_STRIDE * LANES
    lane = lax.broadcasted_iota(jnp.int32, (n_rows, LANES), 1)
    row = lax.broadcasted_iota(jnp.int32, (n_rows, LANES), 0)
    end_c = CMP_STRIDE * row + (CMP_LEN - 1)
    c_k = _pos_columns(lane, end_c).astype(F32)
    c_v = jnp.where(lane == NSA_DH, 1.0, 0.0).astype(F32)

    def branch(x_ref, pe_ref, w1_ref, w2_ref, const, o_ref):
        x = x_ref[0].astype(F32)
        xa = (x + pe_ref[0:1, :]).astype(BF)
        xb = (x + pe_ref[1:2, :]).astype(BF)
        a = jnp.dot(xa, w1_ref[0:half, :], preferred_element_type=F32)
        b = jnp.dot(xb, w1_ref[half:2 * half, :], preferred_element_type=F32)
        pre = a + pltpu.roll(b, n_rows - 1, 0)
        hid = (pre * jax.nn.sigmoid(pre)).astype(BF)
        for g in range(NSA_GROUPS):
            o_ref[0, g] = (jnp.dot(hid, w2_ref[g], preferred_element_type=F32) + const).astype(BF)

    branch(xk_ref, pek_ref, w1k_ref, w2k_ref, c_k, kc_ref)
    branch(xv_ref, pev_ref, w1v_ref, w2v_ref, c_v, vc_ref)


def _nsa_compress(xk, xv, pek, pev, w1k, w1v, w2k, w2v):
    B, n_rows, width = xk.shape
    full = lambda shape: _resident(shape, lambda b: (0,) * len(shape))
    out = jax.ShapeDtypeStruct((B, NSA_GROUPS, n_rows, LANES), BF)
    return pl.pallas_call(
        _compress_kernel,
        grid=(B,),
        in_specs=[
            pl.BlockSpec((1, n_rows, width), lambda b: (b, 0, 0)),
            pl.BlockSpec((1, n_rows, width), lambda b: (b, 0, 0)),
            full(pek.shape), full(pev.shape), full(w1k.shape), full(w1v.shape),
            full(w2k.shape), full(w2v.shape),
        ],
        out_specs=[pl.BlockSpec((1, NSA_GROUPS, n_rows, LANES), lambda b: (b, 0, 0, 0))] * 2,
        out_shape=[out, out],
        compiler_params=_params(("parallel",)),
    )(xk, xv, pek, pev, w1k, w1v, w2k, w2v)


def _nsa_kernel(q_ref, ks_ref, vs_ref, kw_ref, vw_ref, kc_ref, vc_ref, gt_ref, sl_ref, ov_ref,
                o_ref, ksa, kwa, vst, vwt, vct, acc_scr, pw_scr, wb_scr, act_ref, *, seq):
    qi = pl.program_id(2)
    grp = pl.program_id(1)
    n_slc = seq // SLC_LEN
    R = NSA_REP
    KT = SLC_TILE
    WK = WINDOW + QB
    n_cmp = kc_ref.shape[2]

    def t_bf(x):
        return x.astype(F32).T[0:VROWS].astype(BF)

    @pl.when(qi == 0)
    def _():
        lane = lax.broadcasted_iota(jnp.int32, (seq, LANES), 1)
        pos = lax.broadcasted_iota(jnp.int32, (seq, LANES), 0)
        blk = pos >> 6
        off = pos & 63
        r_i = lax.broadcasted_iota(jnp.int32, (LANES, LANES), 0)
        c_i = lax.broadcasted_iota(jnp.int32, (LANES, LANES), 1)
        pick = jnp.where((c_i < NSA_DH) & (r_i == c_i + grp * NSA_DH), 1.0, 0.0).astype(BF)
        c_s = jnp.where(lane == LANES - 1, off,
                        jnp.where((lane >= NSA_DH) & (lane - (NSA_DH - 1) == blk), 1, 0))
        ksa[...] = (jnp.dot(ks_ref[0], pick, preferred_element_type=F32) + c_s.astype(F32)).astype(BF)
        c_w = _pos_columns(lane, pos)
        lane_p = lax.broadcasted_iota(jnp.int32, (WINDOW, LANES), 1)
        kwa[0:WINDOW, :] = jnp.where(lane_p == NSA_DH + PAD_FLAG, 1.0, 0.0).astype(BF)
        d_w = (lax.broadcasted_iota(jnp.int32, (WK, QB), 0) - lax.broadcasted_iota(jnp.int32, (WK, QB), 1))
        wb_scr[...] = jnp.where((d_w > 0) & (d_w <= WINDOW), 0.0, NEG)
        kwa[WINDOW:WINDOW + seq, :] = (jnp.dot(kw_ref[0], pick, preferred_element_type=F32)
                                       + c_w.astype(F32)).astype(BF)
        ones_rows = jnp.where(lax.broadcasted_iota(jnp.int32, (VROWS - NSA_DH, LANES), 0) == 0, 1.0, 0.0)
        n_pad = WINDOW // LANES
        for c in range(n_pad):
            vwt[c] = jnp.zeros((VROWS, LANES), BF)

        def v_tile(x):
            xt = x.astype(F32).T
            dims = jnp.where(grp == 0, xt[0:NSA_DH], xt[NSA_DH:2 * NSA_DH])
            return jnp.concatenate([dims, ones_rows], axis=0).astype(BF)

        def fill(c, carry):
            rows = pl.ds(pl.multiple_of(c * LANES, LANES), LANES)
            vst[c] = v_tile(vs_ref[0, rows, :])
            vwt[c + n_pad] = v_tile(vw_ref[0, rows, :])
            return carry

        lax.fori_loop(0, seq // LANES, fill, 0)
        for c in range(n_cmp // LANES):
            vct[:, c * LANES:(c + 1) * LANES] = t_bf(vc_ref[0, 0, c * LANES:(c + 1) * LANES, :])

    q0 = qi * QB
    rowi = lax.broadcasted_iota(jnp.int32, (NSA_DH, QB), 0)
    slopes = [jnp.concatenate([sl_ref[0, r:r + 1, :]] * (QB // LANES), axis=1) for r in range(R)]
    q_all = q_ref[0].astype(F32).T
    q_t = [q_all[r * NSA_DH:(r + 1) * NSA_DH] for r in range(R)]

    qw = jnp.concatenate(
        [jnp.concatenate([q_t[r] * LOG2E, _slope_rows(rowi, slopes[r] * LOG2E)], axis=0).astype(BF)
         for r in range(R)], axis=1)

    def tile4(x):
        return jnp.concatenate([x] * R, axis=1)

    ovt = ov_ref[...]

    def cmp_branch(rows):
        def fn():
            kw = kwa[pl.ds(pl.multiple_of(q0, QB), WK), :]
            s_w = jnp.dot(kw, qw, preferred_element_type=F32)
            s_w = s_w + tile4(wb_scr[...])
            m_w = jnp.max(s_w, axis=0, keepdims=True)
            pw_scr[...] = jnp.exp2(s_w - m_w).astype(BF)

            s_c = jnp.dot(kc_ref[0, 0, 0:rows, :], qw, preferred_element_type=F32)
            e_c = CMP_STRIDE * lax.broadcasted_iota(jnp.int32, (rows, QB), 0) + (CMP_LEN - 1)
            t_c = q0 + lax.broadcasted_iota(jnp.int32, (rows, QB), 1)
            s_c = s_c + tile4(jnp.where(e_c <= t_c, 0.0, NEG))
            m_c = jnp.maximum(jnp.max(s_c, axis=0, keepdims=True), 0.1 * NEG)
            p_c = jnp.exp2(s_c - m_c)
            l_c = jnp.sum(p_c, axis=0, keepdims=True)
            p_c = p_c * jnp.where(l_c > 0.0, 1.0 / l_c, 0.0)
            o_cmp = jnp.dot(vct[:, 0:rows], p_c.astype(BF), preferred_element_type=F32)
            psum = p_c[:, 0:QB]
            for r in range(1, R):
                psum = psum + p_c[:, r * QB:(r + 1) * QB]
            p_hi = psum.astype(BF)
            rem = psum - p_hi.astype(F32)
            p_mid = rem.astype(BF)
            p_lo = (rem - p_mid.astype(F32)).astype(BF)
            ov = ovt[:, 0:rows]
            return o_cmp, (jnp.dot(ov, p_hi, preferred_element_type=F32)
                           + jnp.dot(ov, p_mid, preferred_element_type=F32)
                           + jnp.dot(ov, p_lo, preferred_element_type=F32))
        return fn

    half = n_cmp // 2
    o_c, imp = lax.cond(q0 + QB <= CMP_STRIDE * half + CMP_LEN - 1, cmp_branch(half), cmp_branch(n_cmp))

    NR = ov_ref.shape[0]
    SUB = 8
    jblk = lax.broadcasted_iota(jnp.int32, (NR, QB), 0)
    t_q = q0 + lax.broadcasted_iota(jnp.int32, (NR, QB), 1)
    cur = t_q >> 6
    forced = (jblk == 0) | (jblk == cur) | (jblk == cur - 1)
    score = jnp.where(jblk > cur, NEG, jnp.where(forced, -NEG, imp))
    n_slab = -(-n_slc // SUB)
    isub = lax.broadcasted_iota(jnp.int32, (SUB, QB), 0)

    def rank_counts(ns):
        def fn():
            slabs = [score[a * SUB:(a + 1) * SUB, :] for a in range(ns)]
            cnts = [jnp.zeros((SUB, QB), F32) for _ in range(ns)]
            for jp in range(min(ns * SUB, n_slc)):
                rowv = jnp.broadcast_to(score[jp:jp + 1, :], (SUB, QB))
                for a in range(ns):
                    if a < jp // SUB:
                        beats = jnp.where(rowv > slabs[a], 1.0, 0.0)
                    elif a > jp // SUB:
                        beats = jnp.where(rowv >= slabs[a], 1.0, 0.0)
                    else:
                        beats = jnp.where(isub > jp % SUB, jnp.where(rowv >= slabs[a], 1.0, 0.0),
                                          jnp.where(rowv > slabs[a], 1.0, 0.0))
                    cnts[a] = cnts[a] + beats
            rest = NR - ns * SUB
            return jnp.concatenate(cnts + [jnp.full((rest, QB), float(NR), F32)] * (rest > 0), axis=0)
        return fn

    last_blk = (q0 + QB - 1) >> 6
    cnt = lax.switch(last_blk // SUB, [rank_counts(ns) for ns in range(1, n_slab + 1)])
    sel = (cnt < float(N_SEL)) & (jblk <= cur) & (jblk < n_slc)
    a_nat = jnp.where(jblk == 0, 1.0, jnp.where(sel, (SLC_LEN * jblk).astype(F32), NEG))
    a_nat = jnp.where(jblk < n_slc, a_nat, 0.0)
    a_t = pltpu.roll(a_nat, NR - 1, 0)

    any_q = jnp.max(jnp.where(sel, 1.0, 0.0), axis=1, keepdims=True)
    jcol = lax.broadcasted_iota(jnp.int32, (NR, 1), 0)
    bits = jnp.where(any_q > 0.0, lax.shift_left(jnp.int32(1), jcol & 31), 0)
    word0 = jnp.sum(jnp.where(jcol < 32, bits, 0))
    word1 = jnp.sum(jnp.where(jcol >= 32, bits, 0))

    qs = jnp.concatenate([jnp.concatenate([q_t[r], a_t * slopes[r]], axis=0).astype(BF) for r in range(R)],
                         axis=1)

    tiles_per = KT // LANES
    blocks_per = KT // SLC_LEN
    n_full = q0 // KT

    def scan(ti, n):
        word = jnp.where(ti < 32 // blocks_per, word0, word1)
        hit = (lax.shift_right_logical(word, (ti * blocks_per) & 31) & ((1 << blocks_per) - 1)) != 0
        act_ref[n] = ti
        return n + jnp.where(hit, 1, 0)

    n_act = lax.fori_loop(0, n_full, scan, 0)

    def scores(ti):
        k0 = pl.multiple_of(ti * KT, KT)
        return jnp.dot(ksa[pl.ds(k0, KT), :], qs, preferred_element_type=F32)

    def values_t(ti):
        return [vst[ti * tiles_per + c] for c in range(tiles_per)]

    def absorb(s, v_tiles, m_prev):
        m_new = jnp.maximum(m_prev, jnp.max(s, axis=0, keepdims=True))
        alpha = jnp.exp(m_prev - m_new)
        p = jnp.exp(s - m_new).astype(BF)
        acc_scr[...] = acc_scr[...] * alpha + jnp.dot(jnp.concatenate(v_tiles, axis=1), p,
                                                     preferred_element_type=F32)
        return m_new

    acc_scr[...] = jnp.zeros(acc_scr.shape, F32)

    def chains(tiles, diag_flags, m_run):
        s_all = []
        for t, is_diag in zip(tiles, diag_flags):
            s = scores(t)
            if is_diag:
                p_d = t * KT + lax.broadcasted_iota(jnp.int32, (KT, QB), 0)
                t_d = q0 + lax.broadcasted_iota(jnp.int32, (KT, QB), 1)
                s = s + tile4(jnp.where(p_d <= t_d, 0.0, NEG))
            s_all.append(s)
        for t, s in zip(tiles, s_all):
            m_run = absorb(s, values_t(t), m_run)
        return m_run

    n_trip = n_act // TRIP_TILES
    m_q = lax.fori_loop(0, n_trip, lambda j, m: chains([act_ref[TRIP_TILES * j + i] for i in range(TRIP_TILES)],
                                                       [False] * TRIP_TILES, m),
                        jnp.full((1, R * QB), NEG, F32))
    n_diag = QB // KT

    def finish(rest):
        def fn(m_run):
            tiles = [act_ref[TRIP_TILES * n_trip + i] for i in range(rest)] + [n_full + d for d in range(n_diag)]
            chains(tiles, [False] * rest + [True] * n_diag, m_run)
            acc_s = acc_scr[...]
            vw_t = jnp.concatenate([vwt[qi * (QB // LANES) + c] for c in range(WK // LANES)], axis=1)
            acc_w = jnp.dot(vw_t, pw_scr[...], preferred_element_type=F32)
            sg_all = jax.nn.sigmoid(gt_ref[0]).T
            per_g = R * 3
            sg_t = jnp.where(grp == 0, sg_all[GATE_LANE0:GATE_LANE0 + per_g],
                             sg_all[GATE_LANE0 + per_g:GATE_LANE0 + 2 * per_g])
            outs = []
            for r in range(R):
                cs = slice(r * QB, (r + 1) * QB)
                a_s = acc_s[:, cs]
                a_w = acc_w[:, cs]
                g_c = sg_t[3 * r:3 * r + 1, :]
                g_s = sg_t[3 * r + 1:3 * r + 2, :] / a_s[NSA_DH:NSA_DH + 1, :]
                g_w = sg_t[3 * r + 2:3 * r + 3, :] / a_w[NSA_DH:NSA_DH + 1, :]
                outs.append((g_c * o_c[:, cs] + g_s * a_s + g_w * a_w)[0:NSA_DH])
            o_ref[0] = jnp.concatenate(outs, axis=0).T.astype(BF)
        return fn

    lax.switch(n_act - TRIP_TILES * n_trip, [finish(rest) for rest in range(TRIP_TILES)], m_q)


def _nsa_attend(proj3, small3, kc, vc, slope_tab, ovt):
    B, S, _ = proj3.shape
    G, R = NSA_GROUPS, NSA_REP
    n_cmp = kc.shape[2]
    kern = functools.partial(_nsa_kernel, seq=S)
    kv_spec = lambda off: pl.BlockSpec((1, S, LANES), lambda b, g, i: (b, 0, off // LANES))
    return pl.pallas_call(
        kern,
        grid=(B, G, S // QB),
        in_specs=[
            pl.BlockSpec((1, QB, R * NSA_DH), lambda b, g, i: (b, i, OFF_NQ // (R * NSA_DH) + g)),
            kv_spec(OFF_KS), kv_spec(OFF_VS), kv_spec(OFF_KW), kv_spec(OFF_VW),
            pl.BlockSpec((1, 1, n_cmp, LANES), lambda b, g, i: (b, g, 0, 0)),
            pl.BlockSpec((1, 1, n_cmp, LANES), lambda b, g, i: (b, g, 0, 0)),
            pl.BlockSpec((1, QB, LANES), lambda b, g, i: (b, i, 0)),
            pl.BlockSpec((1, 8, LANES), lambda b, g, i: (g, 0, 0)),
            _resident(ovt.shape, lambda b, g, i: (0, 0)),
        ],
        out_specs=pl.BlockSpec((1, QB, R * NSA_DH), lambda b, g, i: (b, i, g)),
        out_shape=jax.ShapeDtypeStruct((B, S, G * R * NSA_DH), BF),
        scratch_shapes=[
            pltpu.VMEM((S, LANES), BF),
            pltpu.VMEM((S + WINDOW, LANES), BF),
            pltpu.VMEM((S // LANES, VROWS, LANES), BF),
            pltpu.VMEM(((S + WINDOW) // LANES, VROWS, LANES), BF),
            pltpu.VMEM((VROWS, n_cmp), BF),
            pltpu.VMEM((VROWS, R * QB), F32),
            pltpu.VMEM((WINDOW + QB, R * QB), BF),
            pltpu.VMEM((WINDOW + QB, QB), F32),
            pltpu.SMEM((S // SLC_TILE + 1,), jnp.int32),
        ],
        compiler_params=_params(("parallel", "parallel", "arbitrary")),
    )(proj3, proj3, proj3, proj3, proj3, kc, vc, small3, slope_tab, ovt)


def _merge_kernel(og_ref, on_ref, mg_ref, mn_ref, x_ref, wg_ref, wn_ref, wo_ref, g_ref, o_ref):
    a = jnp.dot(og_ref[...], wg_ref[...], preferred_element_type=F32)
    b = jnp.dot(on_ref[...], wn_ref[...], preferred_element_type=F32)
    mixed = jax.nn.sigmoid(mg_ref[...].astype(F32)) * a + jax.nn.sigmoid(mn_ref[...].astype(F32)) * b
    y = jnp.dot(mixed.astype(BF), wo_ref[...], preferred_element_type=F32)
    y = y * lax.rsqrt(jnp.mean(y * y, axis=-1, keepdims=True) + EPS) * g_ref[...]
    o_ref[...] = x_ref[...] + y


def _merge_out(o_gla2, o_nsa2, proj2, x2, wg, wn, wo, g, tm=512):
    n_tok = x2.shape[0]
    D = D_MODEL
    return pl.pallas_call(
        _merge_kernel,
        grid=(n_tok // tm,),
        in_specs=[
            pl.BlockSpec((tm, D), lambda i: (i, 0)),
            pl.BlockSpec((tm, o_nsa2.shape[1]), lambda i: (i, 0)),
            pl.BlockSpec((tm, D), lambda i: (i, OFF_MG // D)),
            pl.BlockSpec((tm, D), lambda i: (i, OFF_MN // D)),
            pl.BlockSpec((tm, D), lambda i: (i, 0)),
            _resident(wg.shape, lambda i: (0, 0)),
            _resident(wn.shape, lambda i: (0, 0)),
            _resident(wo.shape, lambda i: (0, 0)),
            _resident((1, D), lambda i: (0, 0)),
        ],
        out_specs=pl.BlockSpec((tm, D), lambda i: (i, 0)),
        out_shape=jax.ShapeDtypeStruct((n_tok, D), F32),
        compiler_params=_params(("parallel",)),
    )(o_gla2, o_nsa2, proj2, proj2, x2, wg, wn, wo, g)


def _ffn_kernel(x_ref, gpre_ref, wg_ref, wu_ref, wd_ref, gpost_ref, o_ref, acc_scr, *, chunk):
    x = x_ref[...]
    h = (x * lax.rsqrt(jnp.mean(x * x, axis=-1, keepdims=True) + EPS) * gpre_ref[...]).astype(BF)
    d_ff = wg_ref.shape[1]
    for n, c0 in enumerate(range(0, d_ff, chunk)):
        c1 = min(c0 + chunk, d_ff)
        a = jnp.dot(h, wg_ref[:, c0:c1], preferred_element_type=F32)
        u = jnp.dot(h, wu_ref[:, c0:c1], preferred_element_type=F32)
        t = (a * jax.nn.sigmoid(a) * u).astype(BF)
        part = jnp.dot(t, wd_ref[c0:c1, :], preferred_element_type=F32)
        if n == 0:
            acc_scr[...] = part
        else:
            acc_scr[...] += part
    f = acc_scr[...]
    o_ref[...] = x + f * lax.rsqrt(jnp.mean(f * f, axis=-1, keepdims=True) + EPS) * gpost_ref[...]


def _ffn(x2, gpre, wg, wu, wd, gpost, tm=512, chunk=512):
    n_tok = x2.shape[0]
    D = D_MODEL
    kern = functools.partial(_ffn_kernel, chunk=chunk)
    return pl.pallas_call(
        kern,
        grid=(n_tok // tm,),
        in_specs=[
            pl.BlockSpec((tm, D), lambda i: (i, 0)),
            _resident((1, D), lambda i: (0, 0)),
            _resident(wg.shape, lambda i: (0, 0)),
            _resident(wu.shape, lambda i: (0, 0)),
            _resident(wd.shape, lambda i: (0, 0)),
            _resident((1, D), lambda i: (0, 0)),
        ],
        out_specs=pl.BlockSpec((tm, D), lambda i: (i, 0)),
        out_shape=jax.ShapeDtypeStruct((n_tok, D), F32),
        scratch_shapes=[pltpu.VMEM((tm, D), F32)],
        compiler_params=_params(("parallel",)),
    )(x2, gpre, wg, wu, wd, gpost)


def _prep_compress(pe, w1, w2):
    w1r = w1.astype(BF).reshape(CMP_LEN, NSA_DH, NSA_DH)
    w1e = jnp.stack([jnp.pad(w1r, ((0, 0), (0, 0), (g * NSA_DH, (NSA_GROUPS - 1 - g) * NSA_DH)))
                     for g in range(NSA_GROUPS)], axis=1).reshape(CMP_LEN * NSA_GROUPS * NSA_DH, NSA_GROUPS * NSA_DH)
    pe_e = jnp.broadcast_to(pe[:, None, :], (CMP_LEN, NSA_GROUPS, NSA_DH)).reshape(2, CMP_STRIDE * LANES)
    pe_e = jnp.pad(pe_e, ((0, 6), (0, 0)))
    w2e = jnp.stack([
        jnp.pad(jnp.pad(w2, ((g * NSA_DH, (NSA_GROUPS - 1 - g) * NSA_DH), (0, 0))), ((0, 0), (0, LANES - NSA_DH)))
        for g in range(NSA_GROUPS)])
    return pe_e.astype(F32), w1e.astype(BF), w2e.astype(BF)


def _overlap_table(seq):
    n_cmp = (seq - CMP_LEN) // CMP_STRIDE + 1
    n_slc = seq // SLC_LEN
    sc = CMP_STRIDE * np.arange(n_cmp)
    ss = SLC_LEN * np.arange(n_slc)
    ov = np.clip(np.minimum(sc[:, None] + CMP_LEN, ss[None, :] + SLC_LEN)
                 - np.maximum(sc[:, None], ss[None, :]), 0, None).astype(np.float32) / CMP_LEN
    ovt = np.zeros((NSA_DH, n_cmp + 1), np.float32)
    ovt[:n_slc, :n_cmp] = ov.T
    return jnp.asarray(ovt, dtype=BF)


def kernel(x, norm_mix_pre, norm_mix_post, norm_ffn_pre, norm_ffn_post, w_in, gla_w_alpha2, gla_b_alpha, gla_norm_g, nsa_cmp_pe_k, nsa_cmp_w1_k, nsa_cmp_w2_k, nsa_cmp_pe_v, nsa_cmp_w1_v, nsa_cmp_w2_v, w_proj_gla, w_proj_nsa, w_out, w_ffn_gate, w_ffn_up, w_ffn_down):
    B, S, D = x.shape
    depth = w_in.shape[0]
    n_tok = B * S
    h_idx = jnp.arange(NSA_HEADS, dtype=F32)
    slopes = jnp.exp2(-8.0 * (h_idx + 1.0) / NSA_HEADS).reshape(NSA_GROUPS, NSA_REP, 1)
    slope_tab = jnp.broadcast_to(jnp.pad(slopes, ((0, 0), (0, 8 - NSA_REP), (0, 0))), (NSA_GROUPS, 8, LANES))
    ovt = _overlap_table(S)
    x2 = x.reshape(n_tok, D)
    for l in range(depth):
        w_raw = w_in[l].astype(BF)
        w_tail = jnp.pad(w_raw[:, TAIL_START:], ((0, 0), (0, TAIL_START + LANES - IN_WIDTH)))
        proj2, small2, xk2, xv2 = _in_proj(x2, norm_mix_pre[l][None, :], w_raw, w_tail)
        proj3 = proj2.reshape(B, S, N_MAIN)
        small3 = small2.reshape(B, S, N_SMALL)

        w2_hi = gla_w_alpha2[l].astype(BF)
        w2_lo = (gla_w_alpha2[l] - w2_hi.astype(F32)).astype(BF)
        w2p = jnp.pad(jnp.concatenate([w2_hi, w2_hi, w2_lo], axis=0), ((0, LANES - 3 * GLA_RANK), (0, 0)))
        o_gla = _gla(proj3, small3, w2p, gla_b_alpha[l][None, :], gla_norm_g[l][None, :])

        xk = xk2.reshape(B, S // CMP_STRIDE, CMP_STRIDE * LANES)
        xv = xv2.reshape(B, S // CMP_STRIDE, CMP_STRIDE * LANES)
        pek, w1k, w2k = _prep_compress(nsa_cmp_pe_k[l], nsa_cmp_w1_k[l], nsa_cmp_w2_k[l])
        pev, w1v, w2v = _prep_compress(nsa_cmp_pe_v[l], nsa_cmp_w1_v[l], nsa_cmp_w2_v[l])
        kc, vc = _nsa_compress(xk, xv, pek, pev, w1k, w1v, w2k, w2v)
        o_nsa = _nsa_attend(proj3, small3, kc, vc, slope_tab, ovt)

        x2 = _merge_out(o_gla.reshape(n_tok, -1), o_nsa.reshape(n_tok, -1), proj2, x2,
                        w_proj_gla[l].astype(BF), w_proj_nsa[l].astype(BF),
                        w_out[l].astype(BF), norm_mix_post[l][None, :])
        x2 = _ffn(x2, norm_ffn_pre[l][None, :], w_ffn_gate[l].astype(BF), w_ffn_up[l].astype(BF),
                  w_ffn_down[l].astype(BF), norm_ffn_post[l][None, :])
    return x2.reshape(B, S, D)
```

```python
import functools

import numpy as np
import jax
import jax.numpy as jnp
from jax import lax
from jax.experimental import pallas as pl
from jax.experimental.pallas import tpu as pltpu

D_MODEL = 1024
GLA_HEADS = 4
GLA_DK = 128
GLA_DV = 256
GLA_RANK = 16
GLA_TAU = 16.0
GLA_CHUNK = 64
NSA_HEADS = 8
NSA_GROUPS = 2
NSA_REP = 4
NSA_DH = 64
CMP_LEN = 32
CMP_STRIDE = 16
SLC_LEN = 64
N_SEL = 16
WINDOW = 512
QB = 256
SLC_TILE = 128
TRIP_TILES = 6
VROWS = 80
D_FF = 2816
EPS = 1e-6
NEG = -1e30

LANES = 128
VMEM_LIMIT = 56 * 1024 * 1024
BF = jnp.bfloat16
F32 = jnp.float32

OFF_GQ = 0
OFF_GK = 512
OFF_GV = 1024
OFF_GR = 2048
OFF_MG = 3072
OFF_MN = 4096
OFF_NQ = 5120
OFF_KC = 5632
OFF_VC = 5760
OFF_KS = 5888
OFF_VS = 6016
OFF_KW = 6144
OFF_VW = 6272
N_MAIN = 6400
N_SMALL = 128
GATE_LANE0 = 3 * GLA_RANK

NT = (((1,), (1,)), ((), ()))
TN = (((0,), (0,)), ((), ()))


LOG2E = 1.4426950408889634
POS_TERMS = 3


def _pos_columns(lane, pos):
    first = lane - NSA_DH
    return jnp.where((first >= 0) & (first < POS_TERMS), pos >> 6,
                     jnp.where((first >= POS_TERMS) & (first < 2 * POS_TERMS), pos & 63, 0))


def _slope_rows(rowi, coef):
    terms = []
    rem = coef
    for _ in range(POS_TERMS):
        t = rem.astype(BF).astype(F32)
        terms.append(t)
        rem = rem - t
    out = jnp.zeros(rowi.shape, F32)
    for i, t in enumerate(terms):
        out = jnp.where(rowi == i, t * float(SLC_LEN), jnp.where(rowi == POS_TERMS + i, t, out))
    return out


def _resident(shape, index_map):
    return pl.BlockSpec(shape, index_map, pipeline_mode=pl.Buffered(1))


def _params(sem):
    return pltpu.CompilerParams(dimension_semantics=sem, vmem_limit_bytes=VMEM_LIMIT)


SRC_GA = 3072
SRC_NQ = 3088
SRC_KV = 3600
SRC_GATE = 4368
SRC_MERGE = 4392
IN_WIDTH = 6440
TAIL_START = IN_WIDTH // LANES * LANES
ALIGNED = OFF_MG
SHIFTED_RUNS = ((SRC_MERGE, 2048, OFF_MG, 1.0), (SRC_NQ, 512, OFF_NQ, NSA_DH ** -0.5), (SRC_KV, 768, OFF_KC, 1.0))


def _in_proj_kernel(x_ref, g_ref, wr_ref, wt_ref, om_ref, os_ref, xk_ref, xv_ref, wm_scr, ws_scr, perm_scr):
    @pl.when(pl.program_id(0) == 0)
    def _():
        r_i = lax.broadcasted_iota(jnp.int32, (2 * LANES, LANES), 0)
        c_i = lax.broadcasted_iota(jnp.int32, (2 * LANES, LANES), 1)
        for src, width, dst, scale in SHIFTED_RUNS:
            shift = jnp.where(r_i == c_i + src % LANES, 1.0, 0.0).astype(BF)
            for j in range(width // LANES):
                b0 = (src // LANES + j) * LANES
                if b0 + 2 * LANES <= TAIL_START:
                    pair = wr_ref[:, b0:b0 + 2 * LANES]
                else:
                    pair = jnp.concatenate([wr_ref[:, b0:b0 + LANES], wt_ref[...]], axis=1)
                moved = jnp.dot(pair, shift, preferred_element_type=F32)
                d0 = dst - ALIGNED + j * LANES
                wm_scr[:, d0:d0 + LANES] = (moved * scale).astype(BF)
        r1 = lax.broadcasted_iota(jnp.int32, (LANES, LANES), 0)
        c1 = lax.broadcasted_iota(jnp.int32, (LANES, LANES), 1)
        rep3 = jnp.where((r1 < GLA_RANK) & (c1 < GATE_LANE0) & ((c1 % GLA_RANK) == r1), 1.0, 0.0).astype(BF)
        g_blk = SRC_GATE // LANES * LANES
        n_gate = NSA_HEADS * 3
        pick = jnp.where((c1 >= GATE_LANE0) & (c1 < GATE_LANE0 + n_gate)
                         & (r1 == c1 - GATE_LANE0 + SRC_GATE - g_blk), 1.0, 0.0).astype(BF)
        ws_scr[...] = (jnp.dot(wr_ref[:, SRC_GA:SRC_GA + LANES], rep3, preferred_element_type=F32)
                       + jnp.dot(wr_ref[:, g_blk:g_blk + LANES], pick, preferred_element_type=F32)).astype(BF)
        tm_ = perm_scr.shape[0]
        p_r = lax.broadcasted_iota(jnp.int32, (tm_, tm_), 0)
        p_c = lax.broadcasted_iota(jnp.int32, (tm_, tm_), 1)
        n_blk = tm_ // CMP_STRIDE
        perm_scr[...] = jnp.where(p_c == CMP_STRIDE * (p_r % n_blk) + p_r // n_blk, 1.0, 0.0).astype(BF)

    x = x_ref[...]
    h = (x * lax.rsqrt(jnp.mean(x * x, axis=-1, keepdims=True) + EPS) * g_ref[...]).astype(BF)
    os_ref[...] = jnp.dot(h, ws_scr[...], preferred_element_type=F32)
    step = 512
    for c0 in range(0, ALIGNED, step):
        om_ref[:, c0:c0 + step] = jnp.dot(h, wr_ref[:, c0:c0 + step], preferred_element_type=F32).astype(BF)
    for c0 in range(0, N_MAIN - ALIGNED, step):
        c1_ = min(c0 + step, N_MAIN - ALIGNED)
        om_ref[:, ALIGNED + c0:ALIGNED + c1_] = jnp.dot(h, wm_scr[:, c0:c1_],
                                                        preferred_element_type=F32).astype(BF)
    kv = om_ref[:, OFF_KC:OFF_KC + 2 * LANES]
    moved = jnp.dot(perm_scr[...], kv, preferred_element_type=F32).astype(BF)
    n_blk = x_ref.shape[0] // CMP_STRIDE
    for l_ in range(CMP_STRIDE):
        xk_ref[:, l_ * LANES:(l_ + 1) * LANES] = moved[l_ * n_blk:(l_ + 1) * n_blk, 0:LANES]
        xv_ref[:, l_ * LANES:(l_ + 1) * LANES] = moved[l_ * n_blk:(l_ + 1) * n_blk, LANES:2 * LANES]


def _in_proj(x2, g, w_raw, w_tail, tm=512):
    n_tok = x2.shape[0]
    return pl.pallas_call(
        _in_proj_kernel,
        grid=(n_tok // tm,),
        in_specs=[
            pl.BlockSpec((tm, D_MODEL), lambda i: (i, 0)),
            _resident((1, D_MODEL), lambda i: (0, 0)),
            _resident((D_MODEL, IN_WIDTH), lambda i: (0, 0)),
            _resident((D_MODEL, LANES), lambda i: (0, 0)),
        ],
        out_specs=[
            pl.BlockSpec((tm, N_MAIN), lambda i: (i, 0)),
            pl.BlockSpec((tm, N_SMALL), lambda i: (i, 0)),
            pl.BlockSpec((tm // CMP_STRIDE, CMP_STRIDE * LANES), lambda i: (i, 0)),
            pl.BlockSpec((tm // CMP_STRIDE, CMP_STRIDE * LANES), lambda i: (i, 0)),
        ],
        out_shape=[
            jax.ShapeDtypeStruct((n_tok, N_MAIN), BF),
            jax.ShapeDtypeStruct((n_tok, N_SMALL), F32),
            jax.ShapeDtypeStruct((n_tok // CMP_STRIDE, CMP_STRIDE * LANES), BF),
            jax.ShapeDtypeStruct((n_tok // CMP_STRIDE, CMP_STRIDE * LANES), BF),
        ],
        scratch_shapes=[pltpu.VMEM((D_MODEL, N_MAIN - ALIGNED), BF), pltpu.VMEM((D_MODEL, N_SMALL), BF),
                        pltpu.VMEM((tm, tm), BF)],
        compiler_params=_params(("arbitrary",)),
    )(x2, g, w_raw, w_tail)


def _gla_kernel(q_ref, k_ref, v_ref, r_ref, a_ref, w2_ref, b2_ref, ng_ref, o_ref, st_scr, *, n_chunks, n_heads):
    blk = pl.program_id(2)

    @pl.when(blk == 0)
    def _():
        st_scr[...] = jnp.zeros_like(st_scr)

    C = GLA_CHUNK
    T = n_chunks * C
    W = n_heads * GLA_DK

    def split3(x):
        hi = x.astype(BF)
        rem = x - hi.astype(F32)
        mid = rem.astype(BF)
        return hi, mid, (rem - mid.astype(F32)).astype(BF)

    a = a_ref[0]
    a_hi = a.astype(BF)
    a_lo = (a - a_hi.astype(F32)).astype(BF)
    lane = lax.broadcasted_iota(jnp.int32, (T, LANES), 1)
    in_lo = (lane >= GLA_RANK) & (lane < 2 * GLA_RANK)
    z = jnp.dot(jnp.where(in_lo, a_lo, a_hi), w2_ref[...], preferred_element_type=F32) + b2_ref[...]
    log_a = (jnp.minimum(z, 0.0) - jnp.log(1.0 + jnp.exp(-jnp.abs(z)))) * (1.0 / GLA_TAU)

    x_wide = jnp.concatenate([log_a[c * C:(c + 1) * C] for c in range(n_chunks)], axis=1)
    x3 = jnp.concatenate(split3(x_wide), axis=0)
    r3 = lax.broadcasted_iota(jnp.int32, (C, 3 * C), 0)
    c3 = lax.broadcasted_iota(jnp.int32, (C, 3 * C), 1) & (C - 1)
    tri3 = jnp.where(c3 <= r3, 1.0, 0.0).astype(BF)
    b_wide = jnp.dot(tri3, x3, preferred_element_type=F32)
    bcum = jnp.concatenate([b_wide[:, c * W:(c + 1) * W] for c in range(n_chunks)], axis=0)
    last_rows = [b_wide[C - 1:C, c * W:(c + 1) * W] for c in range(n_chunks)]
    b_last = jnp.concatenate([jnp.broadcast_to(lr, (C, W)) for lr in last_rows], axis=0)
    decay = [jnp.exp(lr) for lr in last_rows]

    q = q_ref[0].astype(F32)
    k = k_ref[0].astype(F32)
    v = v_ref[0]
    qe = (q * ((GLA_DK ** -0.5) * jnp.exp(bcum))).astype(BF)
    ke = (k * jnp.exp(-bcum)).astype(BF)
    kd = (k * jnp.exp(b_last - bcum)).astype(BF)

    H = min(T, 4 * C)
    row = lax.broadcasted_iota(jnp.int32, (H, H), 0)
    col = lax.broadcasted_iota(jnp.int32, (H, H), 1)
    keep = (col <= row) & ((col >> 6) == (row >> 6))
    ng = ng_ref[...]
    r_all = r_ref[0].astype(F32)
    for hh in range(n_heads):
        ks_ = slice(hh * GLA_DK, (hh + 1) * GLA_DK)
        vs_ = slice(hh * GLA_DV, (hh + 1) * GLA_DV)
        intra = []
        for h0 in range(0, T, H):
            hs = slice(h0, h0 + H)
            attn = lax.dot_general(qe[hs, ks_], ke[hs, ks_], NT, preferred_element_type=F32)
            intra.append(jnp.dot(jnp.where(keep, attn, 0.0).astype(BF), v[hs, vs_], preferred_element_type=F32))
        o = jnp.concatenate(intra, axis=0)

        st = st_scr[hh]
        inter = []
        for c in range(n_chunks):
            sl = slice(c * C, (c + 1) * C)
            inter.append(lax.dot_general(qe[sl, ks_], st.astype(BF), NT, preferred_element_type=F32))
            upd = lax.dot_general(v[sl, vs_], kd[sl, ks_], TN, preferred_element_type=F32)
            st = st * decay[c][:, ks_] + upd
        st_scr[hh] = st
        o = o + jnp.concatenate(inter, axis=0)

        o = o * lax.rsqrt(jnp.mean(o * o, axis=-1, keepdims=True) + EPS) * ng
        r = r_all[:, vs_]
        o_ref[0, :, vs_] = (o * (r * jax.nn.sigmoid(r))).astype(BF)


def _gla(proj3, small3, w2p, b2, ng, blk_tokens=1024, heads_per_step=4):
    B, S, _ = proj3.shape
    nblk = S // blk_tokens
    hb = heads_per_step
    kern = functools.partial(_gla_kernel, n_chunks=blk_tokens // GLA_CHUNK, n_heads=hb)
    wk, wv = hb * GLA_DK, hb * GLA_DV
    return pl.pallas_call(
        kern,
        grid=(B, GLA_HEADS // hb, nblk),
        in_specs=[
            pl.BlockSpec((1, blk_tokens, wk), lambda b, h, i: (b, i, OFF_GQ // wk + h)),
            pl.BlockSpec((1, blk_tokens, wk), lambda b, h, i: (b, i, OFF_GK // wk + h)),
            pl.BlockSpec((1, blk_tokens, wv), lambda b, h, i: (b, i, OFF_GV // wv + h)),
            pl.BlockSpec((1, blk_tokens, wv), lambda b, h, i: (b, i, OFF_GR // wv + h)),
            pl.BlockSpec((1, blk_tokens, LANES), lambda b, h, i: (b, i, 0)),
            pl.BlockSpec((LANES, wk), lambda b, h, i: (0, h)),
            pl.BlockSpec((1, wk), lambda b, h, i: (0, h)),
            pl.BlockSpec((1, GLA_DV), lambda b, h, i: (0, 0)),
        ],
        out_specs=pl.BlockSpec((1, blk_tokens, wv), lambda b, h, i: (b, i, h)),
        out_shape=jax.ShapeDtypeStruct((B, S, GLA_HEADS * GLA_DV), BF),
        scratch_shapes=[pltpu.VMEM((hb, GLA_DV, GLA_DK), F32)],
        compiler_params=_params(("parallel", "parallel", "arbitrary")),
    )(proj3, proj3, proj3, proj3, small3, w2p, b2, ng)


def _compress_kernel(xk_ref, xv_ref, pek_ref, pev_ref, w1k_ref, w1v_ref, w2k_ref, w2v_ref, kc_ref, vc_ref):
    n_rows = xk_ref.shape[1]
    half = CMP_STRIDE * LANES
    lane = lax.broadcasted_iota(jnp.int32, (n_rows, LANES), 1)
    row = lax.broadcasted_iota(jnp.int32, (n_rows, LANES), 0)
    end_c = CMP_STRIDE * row + (CMP_LEN - 1)
    c_k = _pos_columns(lane, end_c).astype(F32)
    c_v = jnp.where(lane == NSA_DH, 1.0, 0.0).astype(F32)

    def branch(x_ref, pe_ref, w1_ref, w2_ref, const, o_ref):
        x = x_ref[0].astype(F32)
        xa = (x + pe_ref[0:1, :]).astype(BF)
        xb = (x + pe_ref[1:2, :]).astype(BF)
        a = jnp.dot(xa, w1_ref[0:half, :], preferred_element_type=F32)
        b = jnp.dot(xb, w1_ref[half:2 * half, :], preferred_element_type=F32)
        pre = a + pltpu.roll(b, n_rows - 1, 0)
        hid = (pre * jax.nn.sigmoid(pre)).astype(BF)
        for g in range(NSA_GROUPS):
            o_ref[0, g] = (jnp.dot(hid, w2_ref[g], preferred_element_type=F32) + const).astype(BF)

    branch(xk_ref, pek_ref, w1k_ref, w2k_ref, c_k, kc_ref)
    branch(xv_ref, pev_ref, w1v_ref, w2v_ref, c_v, vc_ref)


def _nsa_compress(xk, xv, pek, pev, w1k, w1v, w2k, w2v):
    B, n_rows, width = xk.shape
    full = lambda shape: _resident(shape, lambda b: (0,) * len(shape))
    out = jax.ShapeDtypeStruct((B, NSA_GROUPS, n_rows, LANES), BF)
    return pl.pallas_call(
        _compress_kernel,
        grid=(B,),
        in_specs=[
            pl.BlockSpec((1, n_rows, width), lambda b: (b, 0, 0)),
            pl.BlockSpec((1, n_rows, width), lambda b: (b, 0, 0)),
            full(pek.shape), full(pev.shape), full(w1k.shape), full(w1v.shape),
            full(w2k.shape), full(w2v.shape),
        ],
        out_specs=[pl.BlockSpec((1, NSA_GROUPS, n_rows, LANES), lambda b: (b, 0, 0, 0))] * 2,
        out_shape=[out, out],
        compiler_params=_params(("parallel",)),
    )(xk, xv, pek, pev, w1k, w1v, w2k, w2v)


def _nsa_kernel(q_ref, ks_ref, vs_ref, kw_ref, vw_ref, kc_ref, vc_ref, gt_ref, sl_ref, ov_ref,
                o_ref, ksa, kwa, vst, vwt, vct, acc_scr, pw_scr, act_ref, *, seq):
    qi = pl.program_id(2)
    grp = pl.program_id(1)
    n_slc = seq // SLC_LEN
    R = NSA_REP
    KT = SLC_TILE
    WK = WINDOW + QB
    n_cmp = kc_ref.shape[2]

    def t_bf(x):
        return x.astype(F32).T[0:VROWS].astype(BF)

    @pl.when(qi == 0)
    def _():
        lane = lax.broadcasted_iota(jnp.int32, (seq, LANES), 1)
        pos = lax.broadcasted_iota(jnp.int32, (seq, LANES), 0)
        blk = pos >> 6
        off = pos & 63
        r_i = lax.broadcasted_iota(jnp.int32, (LANES, LANES), 0)
        c_i = lax.broadcasted_iota(jnp.int32, (LANES, LANES), 1)
        pick = jnp.where((c_i < NSA_DH) & (r_i == c_i + grp * NSA_DH), 1.0, 0.0).astype(BF)
        c_s = jnp.where(lane == LANES - 1, off,
                        jnp.where((lane >= NSA_DH) & (lane - (NSA_DH - 1) == blk), 1, 0))
        ksa[...] = (jnp.dot(ks_ref[0], pick, preferred_element_type=F32) + c_s.astype(F32)).astype(BF)
        c_w = _pos_columns(lane, pos)
        kwa[0:WINDOW, :] = jnp.zeros((WINDOW, LANES), BF)
        kwa[WINDOW:WINDOW + seq, :] = (jnp.dot(kw_ref[0], pick, preferred_element_type=F32)
                                       + c_w.astype(F32)).astype(BF)
        ones_rows = jnp.where(lax.broadcasted_iota(jnp.int32, (VROWS - NSA_DH, LANES), 0) == 0, 1.0, 0.0)
        n_pad = WINDOW // LANES
        for c in range(n_pad):
            vwt[c] = jnp.zeros((VROWS, LANES), BF)

        def v_tile(x):
            xt = x.astype(F32).T
            dims = jnp.where(grp == 0, xt[0:NSA_DH], xt[NSA_DH:2 * NSA_DH])
            return jnp.concatenate([dims, ones_rows], axis=0).astype(BF)

        def fill(c, carry):
            rows = pl.ds(pl.multiple_of(c * LANES, LANES), LANES)
            vst[c] = v_tile(vs_ref[0, rows, :])
            vwt[c + n_pad] = v_tile(vw_ref[0, rows, :])
            return carry

        lax.fori_loop(0, seq // LANES, fill, 0)
        for c in range(n_cmp // LANES):
            vct[:, c * LANES:(c + 1) * LANES] = t_bf(vc_ref[0, 0, c * LANES:(c + 1) * LANES, :])

    q0 = qi * QB
    rowi = lax.broadcasted_iota(jnp.int32, (NSA_DH, QB), 0)
    slopes = [jnp.concatenate([sl_ref[0, r:r + 1, :]] * (QB // LANES), axis=1) for r in range(R)]
    q_all = q_ref[0].astype(F32).T
    q_t = [q_all[r * NSA_DH:(r + 1) * NSA_DH] for r in range(R)]

    qw = jnp.concatenate(
        [jnp.concatenate([q_t[r] * LOG2E, _slope_rows(rowi, slopes[r] * LOG2E)], axis=0).astype(BF)
         for r in range(R)], axis=1)

    def tile4(x):
        return jnp.concatenate([x] * R, axis=1)

    ovt = ov_ref[...]

    def cmp_branch(rows):
        def fn():
            kw = kwa[pl.ds(pl.multiple_of(q0, QB), WK), :]
            s_w = jnp.dot(kw, qw, preferred_element_type=F32)
            c_w = lax.broadcasted_iota(jnp.int32, (WK, QB), 0)
            d_w = c_w - lax.broadcasted_iota(jnp.int32, (WK, QB), 1)
            mask_w = (d_w > 0) & (d_w <= WINDOW) & (c_w >= WINDOW - q0)
            s_w = s_w + tile4(jnp.where(mask_w, 0.0, NEG))
            m_w = jnp.max(s_w, axis=0, keepdims=True)
            pw_scr[...] = jnp.exp2(s_w - m_w).astype(BF)

            s_c = jnp.dot(kc_ref[0, 0, 0:rows, :], qw, preferred_element_type=F32)
            e_c = CMP_STRIDE * lax.broadcasted_iota(jnp.int32, (rows, QB), 0) + (CMP_LEN - 1)
            t_c = q0 + lax.broadcasted_iota(jnp.int32, (rows, QB), 1)
            s_c = s_c + tile4(jnp.where(e_c <= t_c, 0.0, NEG))
            m_c = jnp.maximum(jnp.max(s_c, axis=0, keepdims=True), 0.1 * NEG)
            p_c = jnp.exp2(s_c - m_c)
            l_c = jnp.sum(p_c, axis=0, keepdims=True)
            p_c = p_c * jnp.where(l_c > 0.0, 1.0 / l_c, 0.0)
            o_cmp = jnp.dot(vct[:, 0:rows], p_c.astype(BF), preferred_element_type=F32)
            psum = p_c[:, 0:QB]
            for r in range(1, R):
                psum = psum + p_c[:, r * QB:(r + 1) * QB]
            p_hi = psum.astype(BF)
            rem = psum - p_hi.astype(F32)
            p_mid = rem.astype(BF)
            p_lo = (rem - p_mid.astype(F32)).astype(BF)
            ov = ovt[:, 0:rows]
            return o_cmp, (jnp.dot(ov, p_hi, preferred_element_type=F32)
                           + jnp.dot(ov, p_mid, preferred_element_type=F32)
                           + jnp.dot(ov, p_lo, preferred_element_type=F32))
        return fn

    half = n_cmp // 2
    o_c, imp = lax.cond(q0 + QB <= CMP_STRIDE * half + CMP_LEN - 1, cmp_branch(half), cmp_branch(n_cmp))

    NR = ov_ref.shape[0]
    SUB = 8
    jblk = lax.broadcasted_iota(jnp.int32, (NR, QB), 0)
    t_q = q0 + lax.broadcasted_iota(jnp.int32, (NR, QB), 1)
    cur = t_q >> 6
    forced = (jblk == 0) | (jblk == cur) | (jblk == cur - 1)
    score = jnp.where(jblk > cur, NEG, jnp.where(forced, -NEG, imp))
    n_slab = -(-n_slc // SUB)
    isub = lax.broadcasted_iota(jnp.int32, (SUB, QB), 0)

    def rank_counts(ns):
        def fn():
            slabs = [score[a * SUB:(a + 1) * SUB, :] for a in range(ns)]
            cnts = [jnp.zeros((SUB, QB), F32) for _ in range(ns)]
            for jp in range(min(ns * SUB, n_slc)):
                rowv = jnp.broadcast_to(score[jp:jp + 1, :], (SUB, QB))
                for a in range(ns):
                    if a < jp // SUB:
                        beats = jnp.where(rowv > slabs[a], 1.0, 0.0)
                    elif a > jp // SUB:
                        beats = jnp.where(rowv >= slabs[a], 1.0, 0.0)
                    else:
                        beats = jnp.where(isub > jp % SUB, jnp.where(rowv >= slabs[a], 1.0, 0.0),
                                          jnp.where(rowv > slabs[a], 1.0, 0.0))
                    cnts[a] = cnts[a] + beats
            rest = NR - ns * SUB
            return jnp.concatenate(cnts + [jnp.full((rest, QB), float(NR), F32)] * (rest > 0), axis=0)
        return fn

    last_blk = (q0 + QB - 1) >> 6
    cnt = lax.switch(last_blk // SUB, [rank_counts(ns) for ns in range(1, n_slab + 1)])
    sel = (cnt < float(N_SEL)) & (jblk <= cur) & (jblk < n_slc)
    a_nat = jnp.where(jblk == 0, 1.0, jnp.where(sel, (SLC_LEN * jblk).astype(F32), NEG))
    a_nat = jnp.where(jblk < n_slc, a_nat, 0.0)
    a_t = pltpu.roll(a_nat, NR - 1, 0)

    any_q = jnp.max(jnp.where(sel, 1.0, 0.0), axis=1, keepdims=True)
    jcol = lax.broadcasted_iota(jnp.int32, (NR, 1), 0)
    bits = jnp.where(any_q > 0.0, lax.shift_left(jnp.int32(1), jcol & 31), 0)
    word0 = jnp.sum(jnp.where(jcol < 32, bits, 0))
    word1 = jnp.sum(jnp.where(jcol >= 32, bits, 0))

    qs = jnp.concatenate([jnp.concatenate([q_t[r], a_t * slopes[r]], axis=0).astype(BF) for r in range(R)],
                         axis=1)

    tiles_per = KT // LANES
    blocks_per = KT // SLC_LEN
    n_full = q0 // KT

    def scan(ti, n):
        word = jnp.where(ti < 32 // blocks_per, word0, word1)
        hit = (lax.shift_right_logical(word, (ti * blocks_per) & 31) & ((1 << blocks_per) - 1)) != 0
        act_ref[n] = ti
        return n + jnp.where(hit, 1, 0)

    n_act = lax.fori_loop(0, n_full, scan, 0)

    def scores(ti):
        k0 = pl.multiple_of(ti * KT, KT)
        return jnp.dot(ksa[pl.ds(k0, KT), :], qs, preferred_element_type=F32)

    def values_t(ti):
        return [vst[ti * tiles_per + c] for c in range(tiles_per)]

    def absorb(s, v_tiles, m_prev):
        m_new = jnp.maximum(m_prev, jnp.max(s, axis=0, keepdims=True))
        alpha = jnp.exp(m_prev - m_new)
        p = jnp.exp(s - m_new).astype(BF)
        acc_scr[...] = acc_scr[...] * alpha + jnp.dot(jnp.concatenate(v_tiles, axis=1), p,
                                                     preferred_element_type=F32)
        return m_new

    acc_scr[...] = jnp.zeros(acc_scr.shape, F32)

    def chains(tiles, diag_flags, m_run):
        s_all = []
        for t, is_diag in zip(tiles, diag_flags):
            s = scores(t)
            if is_diag:
                p_d = t * KT + lax.broadcasted_iota(jnp.int32, (KT, QB), 0)
                t_d = q0 + lax.broadcasted_iota(jnp.int32, (KT, QB), 1)
                s = s + tile4(jnp.where(p_d <= t_d, 0.0, NEG))
            s_all.append(s)
        for t, s in zip(tiles, s_all):
            m_run = absorb(s, values_t(t), m_run)
        return m_run

    n_trip = n_act // TRIP_TILES
    m_q = lax.fori_loop(0, n_trip, lambda j, m: chains([act_ref[TRIP_TILES * j + i] for i in range(TRIP_TILES)],
                                                       [False] * TRIP_TILES, m),
                        jnp.full((1, R * QB), NEG, F32))
    n_diag = QB // KT

    def finish(rest):
        def fn(m_run):
            tiles = [act_ref[TRIP_TILES * n_trip + i] for i in range(rest)] + [n_full + d for d in range(n_diag)]
            chains(tiles, [False] * rest + [True] * n_diag, m_run)
            acc_s = acc_scr[...]
            vw_t = jnp.concatenate([vwt[qi * (QB // LANES) + c] for c in range(WK // LANES)], axis=1)
            acc_w = jnp.dot(vw_t, pw_scr[...], preferred_element_type=F32)
            sg_all = jax.nn.sigmoid(gt_ref[0]).T
            per_g = R * 3
            sg_t = jnp.where(grp == 0, sg_all[GATE_LANE0:GATE_LANE0 + per_g],
                             sg_all[GATE_LANE0 + per_g:GATE_LANE0 + 2 * per_g])
            outs = []
            for r in range(R):
                cs = slice(r * QB, (r + 1) * QB)
                a_s = acc_s[:, cs]
                a_w = acc_w[:, cs]
                g_c = sg_t[3 * r:3 * r + 1, :]
                g_s = sg_t[3 * r + 1:3 * r + 2, :] / a_s[NSA_DH:NSA_DH + 1, :]
                g_w = sg_t[3 * r + 2:3 * r + 3, :] / a_w[NSA_DH:NSA_DH + 1, :]
                outs.append((g_c * o_c[:, cs] + g_s * a_s + g_w * a_w)[0:NSA_DH])
            o_ref[0] = jnp.concatenate(outs, axis=0).T.astype(BF)
        return fn

    lax.switch(n_act - TRIP_TILES * n_trip, [finish(rest) for rest in range(TRIP_TILES)], m_q)


def _nsa_attend(proj3, small3, kc, vc, slope_tab, ovt):
    B, S, _ = proj3.shape
    G, R = NSA_GROUPS, NSA_REP
    n_cmp = kc.shape[2]
    kern = functools.partial(_nsa_kernel, seq=S)
    kv_spec = lambda off: pl.BlockSpec((1, S, LANES), lambda b, g, i: (b, 0, off // LANES))
    return pl.pallas_call(
        kern,
        grid=(B, G, S // QB),
        in_specs=[
            pl.BlockSpec((1, QB, R * NSA_DH), lambda b, g, i: (b, i, OFF_NQ // (R * NSA_DH) + g)),
            kv_spec(OFF_KS), kv_spec(OFF_VS), kv_spec(OFF_KW), kv_spec(OFF_VW),
            pl.BlockSpec((1, 1, n_cmp, LANES), lambda b, g, i: (b, g, 0, 0)),
            pl.BlockSpec((1, 1, n_cmp, LANES), lambda b, g, i: (b, g, 0, 0)),
            pl.BlockSpec((1, QB, LANES), lambda b, g, i: (b, i, 0)),
            pl.BlockSpec((1, 8, LANES), lambda b, g, i: (g, 0, 0)),
            _resident(ovt.shape, lambda b, g, i: (0, 0)),
        ],
        out_specs=pl.BlockSpec((1, QB, R * NSA_DH), lambda b, g, i: (b, i, g)),
        out_shape=jax.ShapeDtypeStruct((B, S, G * R * NSA_DH), BF),
        scratch_shapes=[
            pltpu.VMEM((S, LANES), BF),
            pltpu.VMEM((S + WINDOW, LANES), BF),
            pltpu.VMEM((S // LANES, VROWS, LANES), BF),
            pltpu.VMEM(((S + WINDOW) // LANES, VROWS, LANES), BF),
            pltpu.VMEM((VROWS, n_cmp), BF),
            pltpu.VMEM((VROWS, R * QB), F32),
            pltpu.VMEM((WINDOW + QB, R * QB), BF),
            pltpu.SMEM((S // SLC_TILE + 1,), jnp.int32),
        ],
        compiler_params=_params(("parallel", "parallel", "arbitrary")),
    )(proj3, proj3, proj3, proj3, proj3, kc, vc, small3, slope_tab, ovt)


def _merge_kernel(og_ref, on_ref, mg_ref, mn_ref, x_ref, wg_ref, wn_ref, wo_ref, g_ref, o_ref):
    a = jnp.dot(og_ref[...], wg_ref[...], preferred_element_type=F32)
    b = jnp.dot(on_ref[...], wn_ref[...], preferred_element_type=F32)
    mixed = jax.nn.sigmoid(mg_ref[...].astype(F32)) * a + jax.nn.sigmoid(mn_ref[...].astype(F32)) * b
    y = jnp.dot(mixed.astype(BF), wo_ref[...], preferred_element_type=F32)
    y = y * lax.rsqrt(jnp.mean(y * y, axis=-1, keepdims=True) + EPS) * g_ref[...]
    o_ref[...] = x_ref[...] + y


def _merge_out(o_gla2, o_nsa2, proj2, x2, wg, wn, wo, g, tm=512):
    n_tok = x2.shape[0]
    D = D_MODEL
    return pl.pallas_call(
        _merge_kernel,
        grid=(n_tok // tm,),
        in_specs=[
            pl.BlockSpec((tm, D), lambda i: (i, 0)),
            pl.BlockSpec((tm, o_nsa2.shape[1]), lambda i: (i, 0)),
            pl.BlockSpec((tm, D), lambda i: (i, OFF_MG // D)),
            pl.BlockSpec((tm, D), lambda i: (i, OFF_MN // D)),
            pl.BlockSpec((tm, D), lambda i: (i, 0)),
            _resident(wg.shape, lambda i: (0, 0)),
            _resident(wn.shape, lambda i: (0, 0)),
            _resident(wo.shape, lambda i: (0, 0)),
            _resident((1, D), lambda i: (0, 0)),
        ],
        out_specs=pl.BlockSpec((tm, D), lambda i: (i, 0)),
        out_shape=jax.ShapeDtypeStruct((n_tok, D), F32),
        compiler_params=_params(("parallel",)),
    )(o_gla2, o_nsa2, proj2, proj2, x2, wg, wn, wo, g)


def _ffn_kernel(x_ref, gpre_ref, wg_ref, wu_ref, wd_ref, gpost_ref, o_ref, acc_scr, *, chunk):
    x = x_ref[...]
    h = (x * lax.rsqrt(jnp.mean(x * x, axis=-1, keepdims=True) + EPS) * gpre_ref[...]).astype(BF)
    d_ff = wg_ref.shape[1]
    for n, c0 in enumerate(range(0, d_ff, chunk)):
        c1 = min(c0 + chunk, d_ff)
        a = jnp.dot(h, wg_ref[:, c0:c1], preferred_element_type=F32)
        u = jnp.dot(h, wu_ref[:, c0:c1], preferred_element_type=F32)
        t = (a * jax.nn.sigmoid(a) * u).astype(BF)
        part = jnp.dot(t, wd_ref[c0:c1, :], preferred_element_type=F32)
        if n == 0:
            acc_scr[...] = part
        else:
            acc_scr[...] += part
    f = acc_scr[...]
    o_ref[...] = x + f * lax.rsqrt(jnp.mean(f * f, axis=-1, keepdims=True) + EPS) * gpost_ref[...]


def _ffn(x2, gpre, wg, wu, wd, gpost, tm=512, chunk=512):
    n_tok = x2.shape[0]
    D = D_MODEL
    kern = functools.partial(_ffn_kernel, chunk=chunk)
    return pl.pallas_call(
        kern,
        grid=(n_tok // tm,),
        in_specs=[
            pl.BlockSpec((tm, D), lambda i: (i, 0)),
            _resident((1, D), lambda i: (0, 0)),
            _resident(wg.shape, lambda i: (0, 0)),
            _resident(wu.shape, lambda i: (0, 0)),
            _resident(wd.shape, lambda i: (0, 0)),
            _resident((1, D), lambda i: (0, 0)),
        ],
        out_specs=pl.BlockSpec((tm, D), lambda i: (i, 0)),
        out_shape=jax.ShapeDtypeStruct((n_tok, D), F32),
        scratch_shapes=[pltpu.VMEM((tm, D), F32)],
        compiler_params=_params(("parallel",)),
    )(x2, gpre, wg, wu, wd, gpost)


def _prep_compress(pe, w1, w2):
    w1r = w1.astype(BF).reshape(CMP_LEN, NSA_DH, NSA_DH)
    w1e = jnp.stack([jnp.pad(w1r, ((0, 0), (0, 0), (g * NSA_DH, (NSA_GROUPS - 1 - g) * NSA_DH)))
                     for g in range(NSA_GROUPS)], axis=1).reshape(CMP_LEN * NSA_GROUPS * NSA_DH, NSA_GROUPS * NSA_DH)
    pe_e = jnp.broadcast_to(pe[:, None, :], (CMP_LEN, NSA_GROUPS, NSA_DH)).reshape(2, CMP_STRIDE * LANES)
    pe_e = jnp.pad(pe_e, ((0, 6), (0, 0)))
    w2e = jnp.stack([
        jnp.pad(jnp.pad(w2, ((g * NSA_DH, (NSA_GROUPS - 1 - g) * NSA_DH), (0, 0))), ((0, 0), (0, LANES - NSA_DH)))
        for g in range(NSA_GROUPS)])
    return pe_e.astype(F32), w1e.astype(BF), w2e.astype(BF)


def _overlap_table(seq):
    n_cmp = (seq - CMP_LEN) // CMP_STRIDE + 1
    n_slc = seq // SLC_LEN
    sc = CMP_STRIDE * np.arange(n_cmp)
    ss = SLC_LEN * np.arange(n_slc)
    ov = np.clip(np.minimum(sc[:, None] + CMP_LEN, ss[None, :] + SLC_LEN)
                 - np.maximum(sc[:, None], ss[None, :]), 0, None).astype(np.float32) / CMP_LEN
    ovt = np.zeros((NSA_DH, n_cmp + 1), np.float32)
    ovt[:n_slc, :n_cmp] = ov.T
    return jnp.asarray(ovt, dtype=BF)


def kernel(x, norm_mix_pre, norm_mix_post, norm_ffn_pre, norm_ffn_post, w_in, gla_w_alpha2, gla_b_alpha, gla_norm_g, nsa_cmp_pe_k, nsa_cmp_w1_k, nsa_cmp_w2_k, nsa_cmp_pe_v, nsa_cmp_w1_v, nsa_cmp_w2_v, w_proj_gla, w_proj_nsa, w_out, w_ffn_gate, w_ffn_up, w_ffn_down):
    B, S, D = x.shape
    depth = w_in.shape[0]
    n_tok = B * S
    h_idx = jnp.arange(NSA_HEADS, dtype=F32)
    slopes = jnp.exp2(-8.0 * (h_idx + 1.0) / NSA_HEADS).reshape(NSA_GROUPS, NSA_REP, 1)
    slope_tab = jnp.broadcast_to(jnp.pad(slopes, ((0, 0), (0, 8 - NSA_REP), (0, 0))), (NSA_GROUPS, 8, LANES))
    ovt = _overlap_table(S)
    x2 = x.reshape(n_tok, D)
    for l in range(depth):
        w_raw = w_in[l].astype(BF)
        w_tail = jnp.pad(w_raw[:, TAIL_START:], ((0, 0), (0, TAIL_START + LANES - IN_WIDTH)))
        proj2, small2, xk2, xv2 = _in_proj(x2, norm_mix_pre[l][None, :], w_raw, w_tail)
        proj3 = proj2.reshape(B, S, N_MAIN)
        small3 = small2.reshape(B, S, N_SMALL)

        w2_hi = gla_w_alpha2[l].astype(BF)
        w2_lo = (gla_w_alpha2[l] - w2_hi.astype(F32)).astype(BF)
        w2p = jnp.pad(jnp.concatenate([w2_hi, w2_hi, w2_lo], axis=0), ((0, LANES - 3 * GLA_RANK), (0, 0)))
        o_gla = _gla(proj3, small3, w2p, gla_b_alpha[l][None, :], gla_norm_g[l][None, :])

        xk = xk2.reshape(B, S // CMP_STRIDE, CMP_STRIDE * LANES)
        xv = xv2.reshape(B, S // CMP_STRIDE, CMP_STRIDE * LANES)
        pek, w1k, w2k = _prep_compress(nsa_cmp_pe_k[l], nsa_cmp_w1_k[l], nsa_cmp_w2_k[l])
        pev, w1v, w2v = _prep_compress(nsa_cmp_pe_v[l], nsa_cmp_w1_v[l], nsa_cmp_w2_v[l])
        kc, vc = _nsa_compress(xk, xv, pek, pev, w1k, w1v, w2k, w2v)
        o_nsa = _nsa_attend(proj3, small3, kc, vc, slope_tab, ovt)

        x2 = _merge_out(o_gla.reshape(n_tok, -1), o_nsa.reshape(n_tok, -1), proj2, x2,
                        w_proj_gla[l].astype(BF), w_proj_nsa[l].astype(BF),
                        w_out[l].astype(BF), norm_mix_post[l][None, :])
        x2 = _ffn(x2, norm_ffn_pre[l][None, :], w_ffn_gate[l].astype(BF), w_ffn_up[l].astype(BF),
                  w_ffn_down[l].astype(BF), norm_ffn_post[l][None, :])
    return x2.reshape(B, S, D)
```
